```python
import jax, jax.numpy as jnp
from jax import lax
import numpy as np

D_MODEL = 1024
BATCH = 4
SEQ = 4096
DEPTH = 1

MEM_LEN = 256
POOL_WINDOWS = (2, 4, 8, 16)
POOL_GROUPS = 4
POOL_GROUP_DIM = D_MODEL // 8
POOL_WIDTH = POOL_GROUPS * POOL_GROUP_DIM
RET_HEADS = 4
RET_QK_DIM = D_MODEL // 8
RET_V_DIM = D_MODEL // 4
RET_QK_WIDTH = RET_HEADS * RET_QK_DIM
RET_V_WIDTH = RET_HEADS * RET_V_DIM
RET_CHUNK = 128
ROPE_BASE = 10000.0
XA_HEADS = 4
XA_HEAD_DIM = D_MODEL // 8
XA_WIDTH = XA_HEADS * XA_HEAD_DIM
N_BRANCHES = 3
IN_WIDTHS = (POOL_WIDTH, RET_QK_WIDTH, RET_QK_WIDTH, RET_V_WIDTH, RET_V_WIDTH, XA_WIDTH, N_BRANCHES * D_MODEL)
N_GROUPS = 4
EXPERTS_PER_GROUP = 8
TOP_K = 2
D_EXPERT = D_MODEL // 4
LN_EPS = 1e-5
ALPHA = (2.0 * DEPTH) ** 0.25
BETA = (8.0 * DEPTH) ** -0.25

kernel_name = 'hybrid_pool_retention_memory_hmoe'


def layer_norm(x, w, b):
    xf = x.astype(jnp.float32)
    mu = jnp.mean(xf, axis=-1, keepdims=True)
    xc = xf - mu
    var = jnp.mean(xc * xc, axis=-1, keepdims=True)
    return (xc * lax.rsqrt(var + LN_EPS) * w + b).astype(x.dtype)


def head_group_norm(y, w):
    mu = jnp.mean(y, axis=-1, keepdims=True)
    yc = y - mu
    var = jnp.mean(yc * yc, axis=-1, keepdims=True)
    return yc * lax.rsqrt(var + LN_EPS) * w


def multiscale_pool(u, w_grp, scale):
    B, S, _ = u.shape
    uf = u.astype(jnp.float32)
    csum = jnp.pad(jnp.cumsum(uf, axis=1), ((0, 0), (1, 0), (0, 0)))
    hi = jnp.arange(1, S + 1)
    outs = []
    for g, w in enumerate(POOL_WINDOWS):
        sl = slice(g * POOL_GROUP_DIM, (g + 1) * POOL_GROUP_DIM)
        lo = jnp.maximum(hi - w, 0)
        cnt = (hi - lo).astype(jnp.float32)[None, :, None]
        mean = (csum[:, 1:, sl] - csum[:, lo, sl]) / cnt
        outs.append(mean - uf[:, :, sl])
    pooled = jnp.stack(outs, axis=2)
    mixed = jnp.einsum('bsgc,gcd->bsgd', pooled, w_grp.astype(jnp.float32))
    return (mixed.reshape(B, S, POOL_WIDTH) * scale).astype(u.dtype)


def rotary(x, positions):
    d = x.shape[-1]
    inv_freq = ROPE_BASE ** (-jnp.arange(d // 2, dtype=jnp.float32) / (d // 2))
    ang = positions.astype(jnp.float32)[:, :, None, None] * inv_freq
    cos, sin = jnp.cos(ang), jnp.sin(ang)
    xf = x.astype(jnp.float32)
    x1, x2 = xf[..., : d // 2], xf[..., d // 2:]
    return jnp.concatenate([x1 * cos - x2 * sin, x2 * cos + x1 * sin], axis=-1)


def retention(q, k, v):
    B, S, H, dk = q.shape
    dv = v.shape[-1]
    C = RET_CHUNK
    N = S // C
    log_gamma = jnp.log1p(-jnp.exp2(-5.0 - jnp.arange(H, dtype=jnp.float32)))
    qc = q.astype(jnp.float32).reshape(B, N, C, H, dk)
    kc = (k.astype(jnp.float32) * dk ** -0.5).reshape(B, N, C, H, dk)
    vc = v.astype(jnp.float32).reshape(B, N, C, H, dv)
    pos = jnp.arange(C, dtype=jnp.float32)
    diff = pos[:, None] - pos[None, :]
    intra_decay = jnp.where(diff >= 0, jnp.exp(log_gamma[:, None, None] * jnp.maximum(diff, 0.0)), 0.0)
    scores = jnp.einsum('bnchd,bnshd->bnhcs', qc, kc) * intra_decay
    intra = jnp.einsum('bnhcs,bnshe->bnche', scores, vc)
    k_decay = jnp.exp(log_gamma[:, None] * (C - 1.0 - pos)[None, :])
    kv = jnp.einsum('bnshd,hs,bnshe->nbhde', kc, k_decay, vc)
    chunk_decay = jnp.exp(log_gamma * C)[None, :, None, None]

    def step(state, kv_n):
        return chunk_decay * state + kv_n, state

    _, prev = lax.scan(step, jnp.zeros((B, H, dk, dv), jnp.float32), kv)
    q_decay = jnp.exp(log_gamma[:, None] * (pos + 1.0)[None, :])
    cross = jnp.einsum('bnchd,hc,nbhde->bnche', qc, q_decay, prev)
    return (intra + cross).reshape(B, S, H, dv)


def memory_cross_attention(q, mem, w_mem_kv):
    B, S, _ = q.shape
    kv = mem @ w_mem_kv
    k, v = jnp.split(kv, 2, axis=-1)
    qh = q.reshape(B, S, XA_HEADS, XA_HEAD_DIM)
    kh = k.reshape(B, -1, XA_HEADS, XA_HEAD_DIM)
    vh = v.reshape(B, -1, XA_HEADS, XA_HEAD_DIM)
    s = jnp.einsum('bshd,bmhd->bhsm', qh, kh).astype(jnp.float32) * XA_HEAD_DIM ** -0.5
    p = jax.nn.softmax(s, axis=-1).astype(vh.dtype)
    o = jnp.einsum('bhsm,bmhd->bshd', p, vh)
    return o.reshape(B, S, XA_WIDTH)


def hierarchical_moe(h, w_grp_router, b_grp_router, w_exp_router, b_exp_router, w_gate, w_up, w_down):
    B, S, D = h.shape
    t = h.reshape(B * S, D)
    grp_prob = jax.nn.softmax((t @ w_grp_router).astype(jnp.float32) + b_grp_router, axis=-1)
    p_grp, grp_sel = lax.top_k(grp_prob, 1)
    grp_onehot = jax.nn.one_hot(grp_sel[:, 0], N_GROUPS, dtype=jnp.float32)
    exp_logits = ((t @ w_exp_router).astype(jnp.float32) + b_exp_router).reshape(-1, N_GROUPS, EXPERTS_PER_GROUP)
    exp_logits = jnp.einsum('tge,tg->te', exp_logits, grp_onehot)
    top_v, top_i = lax.top_k(exp_logits, TOP_K)
    top_w = jax.nn.softmax(top_v, axis=-1) * p_grp
    w_in_grp = jnp.einsum('tk,tke->te', top_w, jax.nn.one_hot(top_i, EXPERTS_PER_GROUP, dtype=jnp.float32))
    gate = (grp_onehot[:, :, None] * w_in_grp[:, None, :]).astype(h.dtype)
    out = jnp.zeros_like(t)
    for g in range(N_GROUPS):
        a = jnp.einsum('td,edf->tef', t, w_gate[g])
        b = jnp.einsum('td,edf->tef', t, w_up[g])
        act = jax.nn.silu(a) * b * gate[:, g, :, None]
        out = out + jnp.einsum('tef,efd->td', act, w_down[g])
    return out.reshape(B, S, D)


def setup_inputs(seed: int = 0) -> dict:
    key = jax.random.key(seed)
    ks = jax.random.split(key, 24)
    f32 = jnp.float32

    def nrm(k, shape, fan_in, gain=1.0):
        return jax.random.normal(k, shape, f32) * (gain * fan_in ** -0.5)

    x = jax.random.normal(ks[0], (BATCH, SEQ, D_MODEL), f32)
    mem = jax.random.normal(ks[1], (BATCH, MEM_LEN, D_MODEL), f32)
    positions = jnp.arange(SEQ, dtype=jnp.int32)[None, :] + jax.random.randint(ks[2], (BATCH, 1), 0, 1024, dtype=jnp.int32)
    in_keys = jax.random.split(ks[3], len(IN_WIDTHS))
    in_gains = (1.0, 1.0, 1.0, BETA, 1.0, 1.0, 1.0)
    w_in = jnp.concatenate([nrm(k, (DEPTH, D_MODEL, w), D_MODEL, g) for k, w, g in zip(in_keys, IN_WIDTHS, in_gains)], axis=-1)
    w_pool_grp = nrm(ks[4], (DEPTH, POOL_GROUPS, POOL_GROUP_DIM, POOL_GROUP_DIM), POOL_GROUP_DIM)
    pool_scale = 0.5 + 0.05 * jax.random.normal(ks[5], (DEPTH, POOL_WIDTH), f32)
    ret_gn_w = 1.0 + 0.02 * jax.random.normal(ks[6], (DEPTH, RET_HEADS, RET_V_DIM), f32)
    w_mem_kv = jnp.concatenate([nrm(ks[7], (DEPTH, D_MODEL, XA_WIDTH), D_MODEL),
                                nrm(ks[8], (DEPTH, D_MODEL, XA_WIDTH), D_MODEL, BETA)], axis=-1)
    w_br_pool = nrm(ks[9], (DEPTH, POOL_WIDTH, D_MODEL), POOL_WIDTH)
    w_br_ret = nrm(ks[10], (DEPTH, RET_V_WIDTH, D_MODEL), RET_V_WIDTH)
    w_br_xa = nrm(ks[11], (DEPTH, XA_WIDTH, D_MODEL), XA_WIDTH)
    w_out = nrm(ks[12], (DEPTH, D_MODEL, D_MODEL), D_MODEL, BETA)
    ln1_w = 1.0 + 0.02 * jax.random.normal(ks[13], (DEPTH, D_MODEL), f32)
    ln1_b = 0.02 * jax.random.normal(ks[14], (DEPTH, D_MODEL), f32)
    w_grp_router = nrm(ks[15], (DEPTH, D_MODEL, N_GROUPS), D_MODEL)
    b_grp_router = 0.01 * jax.random.normal(ks[16], (DEPTH, N_GROUPS), f32)
    w_exp_router = nrm(ks[17], (DEPTH, D_MODEL, N_GROUPS * EXPERTS_PER_GROUP), D_MODEL)
    b_exp_router = 0.01 * jax.random.normal(ks[18], (DEPTH, N_GROUPS * EXPERTS_PER_GROUP), f32)
    w_exp_gate = nrm(ks[19], (DEPTH, N_GROUPS, EXPERTS_PER_GROUP, D_MODEL, D_EXPERT), D_MODEL)
    w_exp_up = nrm(ks[20], (DEPTH, N_GROUPS, EXPERTS_PER_GROUP, D_MODEL, D_EXPERT), D_MODEL)
    w_exp_down = nrm(ks[21], (DEPTH, N_GROUPS, EXPERTS_PER_GROUP, D_EXPERT, D_MODEL), D_EXPERT, BETA)
    ln2_w = 1.0 + 0.02 * jax.random.normal(ks[22], (DEPTH, D_MODEL), f32)
    ln2_b = 0.02 * jax.random.normal(ks[23], (DEPTH, D_MODEL), f32)
    return {'x': x, 'mem': mem, 'positions': positions, 'w_in': w_in, 'w_pool_grp': w_pool_grp,
            'pool_scale': pool_scale, 'ret_gn_w': ret_gn_w, 'w_mem_kv': w_mem_kv, 'w_br_pool': w_br_pool,
            'w_br_ret': w_br_ret, 'w_br_xa': w_br_xa, 'w_out': w_out, 'ln1_w': ln1_w, 'ln1_b': ln1_b,
            'w_grp_router': w_grp_router, 'b_grp_router': b_grp_router, 'w_exp_router': w_exp_router,
            'b_exp_router': b_exp_router, 'w_exp_gate': w_exp_gate, 'w_exp_up': w_exp_up,
            'w_exp_down': w_exp_down, 'ln2_w': ln2_w, 'ln2_b': ln2_b}


def reference(x, mem, positions, w_in, w_pool_grp, pool_scale, ret_gn_w, w_mem_kv, w_br_pool, w_br_ret,
              w_br_xa, w_out, ln1_w, ln1_b, w_grp_router, b_grp_router, w_exp_router, b_exp_router,
              w_exp_gate, w_exp_up, w_exp_down, ln2_w, ln2_b):
    B, S, _ = x.shape
    split_points = [int(p) for p in np.cumsum(IN_WIDTHS)[:-1]]
    for l in range(DEPTH):
        proj = x @ w_in[l]
        u_pool, r_q, r_k, r_v, r_g, xa_q, gate_logits = jnp.split(proj, split_points, axis=-1)
        y_pool = multiscale_pool(u_pool, w_pool_grp[l], pool_scale[l])
        q = rotary(r_q.reshape(B, S, RET_HEADS, RET_QK_DIM), positions)
        k = rotary(r_k.reshape(B, S, RET_HEADS, RET_QK_DIM), positions)
        y_ret = retention(q, k, r_v.reshape(B, S, RET_HEADS, RET_V_DIM))
        y_ret = head_group_norm(y_ret, ret_gn_w[l]).reshape(B, S, RET_V_WIDTH)
        y_ret = (jax.nn.silu(r_g.astype(jnp.float32)) * y_ret).astype(x.dtype)
        y_xa = memory_cross_attention(xa_q, mem, w_mem_kv[l])
        g = jax.nn.sigmoid(gate_logits.astype(jnp.float32)).astype(x.dtype).reshape(B, S, N_BRANCHES, D_MODEL)
        merged = (g[:, :, 0] * (y_pool @ w_br_pool[l])
                  + g[:, :, 1] * (y_ret @ w_br_ret[l])
                  + g[:, :, 2] * (y_xa @ w_br_xa[l]))
        x = layer_norm(ALPHA * x + merged @ w_out[l], ln1_w[l], ln1_b[l])
        moe = hierarchical_moe(x, w_grp_router[l], b_grp_router[l], w_exp_router[l], b_exp_router[l],
                               w_exp_gate[l], w_exp_up[l], w_exp_down[l])
        x = layer_norm(ALPHA * x + moe, ln2_w[l], ln2_b[l])
    return x
```

```python
import functools

import numpy as np
import jax
import jax.numpy as jnp
from jax import lax
from jax.experimental import pallas as pl
from jax.experimental.pallas import tpu as pltpu

F32 = jnp.float32
BF16 = jnp.bfloat16

D_MODEL = 1024
POOL_WINDOWS = (2, 4, 8, 16)
POOL_GROUP_DIM = 128
POOL_WIDTH = 512
POOL_HALO = 16
RET_HEADS = 4
RET_QK_DIM = 128
RET_V_DIM = 256
RET_CHUNK = 128
ROPE_BASE = 10000.0
XA_HEADS = 4
XA_HEAD_DIM = 128
XA_WIDTH = 512
N_GROUPS = 4
EXPERTS_PER_GROUP = 8
N_EXPERTS = N_GROUPS * EXPERTS_PER_GROUP
D_EXPERT = 256
LN_EPS = 1e-5
DEPTH = 1
ALPHA = (2.0 * DEPTH) ** 0.25
NEG_BIG = -1e30

COL_GATES, COL_V, COL_G, COL_POOL, COL_Q, COL_K, COL_XAQ = 0, 3072, 4096, 5120, 5632, 6144, 6656
IN_TOTAL = 7168

VMEM_LIMIT = 56 * 1024 * 1024


def _dot(a, b):
    return jnp.dot(a, b, preferred_element_type=F32)


def _dot_nt(a, b, precision=None):
    return lax.dot_general(a, b, (((1,), (1,)), ((), ())), preferred_element_type=F32, precision=precision)


def _sigmoid(z):
    return 1.0 / (1.0 + jnp.exp(-z))


def _layer_norm(h, w, b):
    mu = jnp.mean(h, axis=-1, keepdims=True)
    hc = h - mu
    var = jnp.mean(hc * hc, axis=-1, keepdims=True)
    return hc * lax.rsqrt(var + LN_EPS) * w + b


def _rope_kernel(pos_ref, freq_ref, sign_ref, cos_ref, sin_ref):
    ang = pos_ref[...].astype(F32) * freq_ref[...]
    cos_ref[...] = jnp.cos(ang)
    sin_ref[...] = jnp.sin(ang) * sign_ref[...]


def _rope_table(pos_col, tile=2048):
    T = pos_col.shape[0]
    half = RET_QK_DIM // 2
    inv_freq = (ROPE_BASE ** (-np.arange(half, dtype=np.float64) / half)).astype(np.float32)
    freq = jnp.asarray(np.concatenate([inv_freq, inv_freq])[None, :])
    sign = jnp.asarray(np.concatenate([-np.ones(half, np.float32), np.ones(half, np.float32)])[None, :])
    row = pl.BlockSpec((1, RET_QK_DIM), lambda i: (0, 0))
    out = pl.BlockSpec((tile, RET_QK_DIM), lambda i: (i, 0))
    return pl.pallas_call(
        _rope_kernel,
        grid=(T // tile,),
        in_specs=[pl.BlockSpec((tile, 1), lambda i: (i, 0)), row, row],
        out_specs=[out, out],
        out_shape=[jax.ShapeDtypeStruct((T, RET_QK_DIM), F32)] * 2,
        name="rope_table",
    )(pos_col, freq, sign)


def _inproj_kernel(x_ref, w_ref, o_ref, xb_ref):
    @pl.when(pl.program_id(1) == 0)
    def _():
        xb_ref[...] = x_ref[...].astype(BF16)

    o_ref[...] = _dot(xb_ref[...], w_ref[...]).astype(BF16)


def _in_proj(xf, w_in_b, bm=1024, bn=1024):
    T, D = xf.shape
    N = w_in_b.shape[1]
    return pl.pallas_call(
        _inproj_kernel,
        grid=(T // bm, N // bn),
        in_specs=[pl.BlockSpec((bm, D), lambda i, j: (i, 0)),
                  pl.BlockSpec((D, bn), lambda i, j: (0, j))],
        out_specs=pl.BlockSpec((bm, bn), lambda i, j: (i, j)),
        out_shape=jax.ShapeDtypeStruct((T, N), BF16),
        scratch_shapes=[pltpu.VMEM((bm, D), BF16)],
        compiler_params=pltpu.CompilerParams(dimension_semantics=("arbitrary", "arbitrary"),
                                             vmem_limit_bytes=VMEM_LIMIT),
        name="in_proj",
    )(xf, w_in_b)


POOL_SUB = 256


def _pool_bands():
    r = np.arange(POOL_SUB)[:, None]
    c = np.arange(POOL_SUB)[None, :]
    ch = np.arange(POOL_HALO)[None, :] - POOL_HALO
    main = np.stack([((r - c >= 0) & (r - c < w)) for w in POOL_WINDOWS]).astype(np.float32)
    halo = np.stack([((r - ch >= 0) & (r - ch < w)) for w in POOL_WINDOWS]).astype(np.float32)
    return jnp.asarray(main, BF16), jnp.asarray(halo, BF16)


def _pool_kernel(u_ref, halo_ref, bmain_ref, bhalo_ref, wg_ref, scale_ref, o_ref, *, tile, seq):
    s0 = lax.rem(pl.program_id(0) * tile, seq)
    first = s0 == 0
    for sb in range(tile // POOL_SUB):
        r0 = sb * POOL_SUB
        u = u_ref[r0:r0 + POOL_SUB, :]
        if sb == 0:
            prev = halo_ref[...]
            prev = jnp.where(first, jnp.zeros_like(prev), prev)
        else:
            prev = u_ref[r0 - POOL_HALO:r0, :]
        pos = s0 + r0 + lax.broadcasted_iota(jnp.int32, (POOL_SUB, POOL_GROUP_DIM), 0)
        for g, w in enumerate(POOL_WINDOWS):
            cols = slice(g * POOL_GROUP_DIM, (g + 1) * POOL_GROUP_DIM)
            ug = u[:, cols]
            wsum = _dot(bmain_ref[g], ug) + _dot(bhalo_ref[g], prev[:, cols])
            cnt = jnp.minimum(pos + 1, w).astype(F32)
            pooled = wsum / cnt - ug.astype(F32)
            mixed = _dot(pooled.astype(BF16), wg_ref[g]) * scale_ref[:, cols]
            o_ref[r0:r0 + POOL_SUB, cols] = mixed.astype(BF16)


def _pool(proj, w_grp_b, scale, seq, tile=512):
    T = proj.shape[0]
    bmain, bhalo = _pool_bands()
    cb = COL_POOL // POOL_WIDTH
    hb = tile // POOL_HALO
    full = lambda a: pl.BlockSpec(a.shape, lambda i: (0,) * a.ndim)
    return pl.pallas_call(
        functools.partial(_pool_kernel, tile=tile, seq=seq),
        grid=(T // tile,),
        in_specs=[pl.BlockSpec((tile, POOL_WIDTH), lambda i: (i, cb)),
                  pl.BlockSpec((POOL_HALO, POOL_WIDTH), lambda i: (jnp.maximum(i * hb - 1, 0), cb)),
                  full(bmain), full(bhalo), full(w_grp_b), full(scale)],
        out_specs=pl.BlockSpec((tile, POOL_WIDTH), lambda i: (i, 0)),
        out_shape=jax.ShapeDtypeStruct((T, POOL_WIDTH), BF16),
        name="pool",
    )(proj, proj, bmain, bhalo, w_grp_b, scale)


def _ret_consts():
    h = np.arange(RET_HEADS, dtype=np.float64)
    log_gamma = np.log1p(-np.exp2(-5.0 - h))
    pos = np.arange(RET_CHUNK, dtype=np.float64)
    diff = pos[:, None] - pos[None, :]
    dmask = np.where(diff >= 0, np.exp(log_gamma[:, None, None] * np.maximum(diff, 0.0)), 0.0)
    qdec = np.exp(log_gamma[:, None] * (pos + 1.0)[None, :])
    kdec = np.exp(log_gamma[:, None] * (RET_CHUNK - 1.0 - pos)[None, :])
    cdec = np.exp(log_gamma * RET_CHUNK)
    lanes = lambda a: np.broadcast_to(a[:, :, None], (RET_HEADS, RET_CHUNK, RET_QK_DIM))
    return (jnp.asarray(dmask, F32), jnp.asarray(lanes(qdec), F32), jnp.asarray(lanes(kdec), F32),
            tuple(float(v) for v in cdec))


def _ret_kernel(q_ref, k_ref, v_ref, g_ref, cos_ref, sin_ref, dmask_ref, qdec_ref, kdec_ref, gnw_ref,
                o_ref, state_ref, *, rb, cdec):
    @pl.when(pl.program_id(1) == 0)
    def _():
        state_ref[...] = jnp.zeros_like(state_ref)

    kscale = RET_QK_DIM ** -0.5
    half = RET_QK_DIM // 2
    for c in range(rb // RET_CHUNK):
        rows = slice(c * RET_CHUNK, (c + 1) * RET_CHUNK)
        cos = cos_ref[rows, :]
        sin = sin_ref[rows, :]
        for h in range(RET_HEADS):
            qk_cols = slice(h * RET_QK_DIM, (h + 1) * RET_QK_DIM)
            v_cols = slice(h * RET_V_DIM, (h + 1) * RET_V_DIM)
            q = q_ref[rows, qk_cols].astype(F32)
            k = k_ref[rows, qk_cols].astype(F32)
            qr = q * cos + pltpu.roll(q, half, 1) * sin
            kr = (k * cos + pltpu.roll(k, half, 1) * sin) * kscale
            v = v_ref[rows, v_cols]
            scores = _dot_nt(qr.astype(BF16), kr.astype(BF16)) * dmask_ref[h]
            intra = _dot(scores.astype(BF16), v)
            st = state_ref[h]
            cross = _dot((qr * qdec_ref[h]).astype(BF16), st.astype(BF16))
            kd_t = jnp.transpose(kr * kdec_ref[h]).astype(BF16)
            state_ref[h] = cdec[h] * st + _dot(kd_t, v)
            y = intra + cross
            mu = jnp.mean(y, axis=-1, keepdims=True)
            yc = y - mu
            var = jnp.mean(yc * yc, axis=-1, keepdims=True)
            yn = yc * lax.rsqrt(var + LN_EPS) * gnw_ref[:, v_cols]
            gt = g_ref[rows, v_cols].astype(F32)
            o_ref[rows, v_cols] = (gt * _sigmoid(gt) * yn).astype(BF16)


def _retention(proj, cos, sin, gnw, batch, seq, rb=512):
    T = proj.shape[0]
    dmask, qdec, kdec, cdec = _ret_consts()
    nj = seq // rb
    row = lambda b, j: b * nj + j
    full = lambda a: pl.BlockSpec(a.shape, lambda b, j: (0,) * a.ndim)
    vw = RET_HEADS * RET_V_DIM
    qw = RET_HEADS * RET_QK_DIM
    return pl.pallas_call(
        functools.partial(_ret_kernel, rb=rb, cdec=cdec),
        grid=(batch, nj),
        in_specs=[pl.BlockSpec((rb, qw), lambda b, j: (row(b, j), COL_Q // qw)),
                  pl.BlockSpec((rb, qw), lambda b, j: (row(b, j), COL_K // qw)),
                  pl.BlockSpec((rb, vw), lambda b, j: (row(b, j), COL_V // vw)),
                  pl.BlockSpec((rb, vw), lambda b, j: (row(b, j), COL_G // vw)),
                  pl.BlockSpec((rb, RET_QK_DIM), lambda b, j: (row(b, j), 0)),
                  pl.BlockSpec((rb, RET_QK_DIM), lambda b, j: (row(b, j), 0)),
                  full(dmask), full(qdec), full(kdec), full(gnw)],
        out_specs=pl.BlockSpec((rb, vw), lambda b, j: (row(b, j), 0)),
        out_shape=jax.ShapeDtypeStruct((T, vw), BF16),
        scratch_shapes=[pltpu.VMEM((RET_HEADS, RET_QK_DIM, RET_V_DIM), F32)],
        compiler_params=pltpu.CompilerParams(dimension_semantics=("arbitrary", "arbitrary"),
                                             vmem_limit_bytes=VMEM_LIMIT),
        name="retention",
    )(proj, proj, proj, proj, cos, sin, dmask, qdec, kdec, gnw)


def _memkv_kernel(m_ref, w_ref, o_ref):
    o_ref[...] = _dot(m_ref[...].astype(BF16), w_ref[...]).astype(BF16)


def _mem_kv(memf, w_b):
    M, D = memf.shape
    N = w_b.shape[1]
    return pl.pallas_call(
        _memkv_kernel,
        grid=(1,),
        in_specs=[pl.BlockSpec((M, D), lambda i: (0, 0)), pl.BlockSpec((D, N), lambda i: (0, 0))],
        out_specs=pl.BlockSpec((M, N), lambda i: (0, 0)),
        out_shape=jax.ShapeDtypeStruct((M, N), BF16),
        name="mem_kv",
    )(memf, w_b)


def _xa_kernel(q_ref, k_ref, v_ref, o_ref):
    scale = XA_HEAD_DIM ** -0.5
    for h in range(XA_HEADS):
        cols = slice(h * XA_HEAD_DIM, (h + 1) * XA_HEAD_DIM)
        s = _dot_nt(q_ref[:, cols], k_ref[:, cols]) * scale
        m = jnp.max(s, axis=-1, keepdims=True)
        p = jnp.exp(s - m)
        l = jnp.sum(p, axis=-1, keepdims=True)
        o = _dot(p.astype(BF16), v_ref[:, cols]) / l
        o_ref[:, cols] = o.astype(BF16)


def _cross_attention(proj, kv, batch, seq, mem_len, tile=512):
    T = proj.shape[0]
    nj = seq // tile
    return pl.pallas_call(
        _xa_kernel,
        grid=(batch, nj),
        in_specs=[pl.BlockSpec((tile, XA_WIDTH), lambda b, j: (b * nj + j, COL_XAQ // XA_WIDTH)),
                  pl.BlockSpec((mem_len, XA_WIDTH), lambda b, j: (b, 0)),
                  pl.BlockSpec((mem_len, XA_WIDTH), lambda b, j: (b, 1))],
        out_specs=pl.BlockSpec((tile, XA_WIDTH), lambda b, j: (b * nj + j, 0)),
        out_shape=jax.ShapeDtypeStruct((T, XA_WIDTH), BF16),
        name="cross_attention",
    )(proj, kv, kv)


ROUTER_ROWS = 128
EXP_ROW0 = 8


def _route(logits_t):
    gl = logits_t[0:8]
    gmax = jnp.max(gl, axis=0, keepdims=True)
    p_grp = 1.0 / jnp.sum(jnp.exp(gl - gmax), axis=0, keepdims=True)
    idx8 = lax.broadcasted_iota(jnp.int32, gl.shape, 0)
    gsel = jnp.min(jnp.where(gl == gmax, idx8, 8), axis=0, keepdims=True)
    cl = jnp.zeros_like(gl)
    for g in range(N_GROUPS):
        r0 = EXP_ROW0 + g * EXPERTS_PER_GROUP
        cl = cl + jnp.where(gsel == g, logits_t[r0:r0 + EXPERTS_PER_GROUP], 0.0)
    v1 = jnp.max(cl, axis=0, keepdims=True)
    i1 = jnp.min(jnp.where(cl == v1, idx8, 8), axis=0, keepdims=True)
    cl2 = jnp.where(idx8 == i1, -jnp.inf, cl)
    v2 = jnp.max(cl2, axis=0, keepdims=True)
    i2 = jnp.min(jnp.where(cl2 == v2, idx8, 8), axis=0, keepdims=True)
    e21 = jnp.exp(v2 - v1)
    w1 = p_grp / (1.0 + e21)
    w2 = p_grp * e21 / (1.0 + e21)
    wrow = jnp.where(idx8 == i1, w1, 0.0) + jnp.where(idx8 == i2, w2, 0.0)
    return jnp.concatenate([jnp.where(gsel == g, wrow, 0.0) for g in range(N_GROUPS)], axis=0)


def _merge_kernel(yp_ref, yr_ref, ya_ref, gl_ref, x_ref, wp_ref, wr_ref, wa_ref, wo_ref, lnw_ref, lnb_ref,
                  rw_ref, rb_ref, x1_ref, x1b_ref, gate_ref):
    merged = _sigmoid(gl_ref[:, 0:D_MODEL].astype(F32)) * _dot(yp_ref[...], wp_ref[...])
    merged = merged + _sigmoid(gl_ref[:, D_MODEL:2 * D_MODEL].astype(F32)) * _dot(yr_ref[...], wr_ref[...])
    merged = merged + _sigmoid(gl_ref[:, 2 * D_MODEL:3 * D_MODEL].astype(F32)) * _dot(ya_ref[...], wa_ref[...])
    h = ALPHA * x_ref[...] + _dot(merged.astype(BF16), wo_ref[...])
    x1 = _layer_norm(h, lnw_ref[...], lnb_ref[...])
    x1_ref[...] = x1
    x1b_ref[...] = x1.astype(BF16)
    logits_t = _dot_nt(rw_ref[...], x1, precision=lax.Precision.HIGHEST) + rb_ref[...]
    gate_t = _route(logits_t)
    tile = gate_t.shape[1]
    gate_t = jnp.concatenate([gate_t, jnp.zeros((ROUTER_ROWS - N_EXPERTS, tile), F32)], axis=0)
    gate_ref[...] = jnp.transpose(gate_t)


def _merge(y_pool, y_ret, y_xa, proj, xf, wp, wr, wa, wo, lnw, lnb, rw, rb, tile=512):
    T = xf.shape[0]
    full = lambda a: pl.BlockSpec(a.shape, lambda i: (0,) * a.ndim)
    rowblk = lambda w: pl.BlockSpec((tile, w), lambda i: (i, 0))
    gw = 3 * D_MODEL
    return pl.pallas_call(
        _merge_kernel,
        grid=(T // tile,),
        in_specs=[rowblk(POOL_WIDTH), rowblk(D_MODEL), rowblk(XA_WIDTH),
                  pl.BlockSpec((tile, gw), lambda i: (i, COL_GATES // gw)),
                  rowblk(D_MODEL), full(wp), full(wr), full(wa), full(wo), full(lnw), full(lnb),
                  full(rw), full(rb)],
        out_specs=[rowblk(D_MODEL), rowblk(D_MODEL), rowblk(ROUTER_ROWS)],
        out_shape=[jax.ShapeDtypeStruct((T, D_MODEL), F32), jax.ShapeDtypeStruct((T, D_MODEL), BF16),
                   jax.ShapeDtypeStruct((T, ROUTER_ROWS), F32)],
        compiler_params=pltpu.CompilerParams(dimension_semantics=("arbitrary",), vmem_limit_bytes=VMEM_LIMIT),
        name="merge_ln1_router",
    )(y_pool, y_ret, y_xa, proj, xf, wp, wr, wa, wo, lnw, lnb, rw, rb)


def _moe_kernel(xb_ref, x1_ref, gate_ref, wg_ref, wu_ref, wd_ref, lnw_ref, lnb_ref, o_ref, acc_ref):
    e = pl.program_id(1)

    @pl.when(e == 0)
    def _():
        acc_ref[...] = jnp.zeros_like(acc_ref)

    xb = xb_ref[...]
    a = _dot(xb, wg_ref[0])
    b = _dot(xb, wu_ref[0])
    gate = gate_ref[...]
    lane = lax.broadcasted_iota(jnp.int32, gate.shape, 1)
    gcol = jnp.sum(jnp.where(lane == e, gate, 0.0), axis=-1, keepdims=True)
    act = a * _sigmoid(a) * b * gcol
    acc_ref[...] += _dot(act.astype(BF16), wd_ref[0])

    @pl.when(e == pl.num_programs(1) - 1)
    def _():
        h = ALPHA * x1_ref[...] + acc_ref[...]
        o_ref[...] = _layer_norm(h, lnw_ref[...], lnb_ref[...])


def _moe_dense(x1b, x1, gate, wg, wu, wd, lnw, lnb, tm=1024):
    T = x1.shape[0]
    full = lambda a: pl.BlockSpec(a.shape, lambda i, e: (0,) * a.ndim)
    return pl.pallas_call(
        _moe_kernel,
        grid=(T // tm, N_EXPERTS),
        in_specs=[pl.BlockSpec((tm, D_MODEL), lambda i, e: (i, 0)),
                  pl.BlockSpec((tm, D_MODEL), lambda i, e: (i, 0)),
                  pl.BlockSpec((tm, ROUTER_ROWS), lambda i, e: (i, 0)),
                  pl.BlockSpec((1, D_MODEL, D_EXPERT), lambda i, e: (e, 0, 0)),
                  pl.BlockSpec((1, D_MODEL, D_EXPERT), lambda i, e: (e, 0, 0)),
                  pl.BlockSpec((1, D_EXPERT, D_MODEL), lambda i, e: (e, 0, 0)),
                  full(lnw), full(lnb)],
        out_specs=pl.BlockSpec((tm, D_MODEL), lambda i, e: (i, 0)),
        out_shape=jax.ShapeDtypeStruct((T, D_MODEL), F32),
        scratch_shapes=[pltpu.VMEM((tm, D_MODEL), F32)],
        compiler_params=pltpu.CompilerParams(dimension_semantics=("arbitrary", "arbitrary"),
                                             vmem_limit_bytes=VMEM_LIMIT),
        name="moe_ln2",
    )(x1b, x1, gate, wg, wu, wd, lnw, lnb)


def _reorder_in_cols(w):
    pool, q, k, v, g, xaq, gates = jnp.split(w, [512, 1024, 1536, 2560, 3584, 4096], axis=-1)
    return jnp.concatenate([gates, v, g, pool, q, k, xaq], axis=-1)


def _router_params(w_grp, b_grp, w_exp, b_exp):
    rw = jnp.zeros((ROUTER_ROWS, D_MODEL), F32)
    rw = rw.at[0:N_GROUPS].set(w_grp.T).at[EXP_ROW0:EXP_ROW0 + N_EXPERTS].set(w_exp.T)
    rb = jnp.zeros((ROUTER_ROWS,), F32).at[N_GROUPS:8].set(NEG_BIG)
    rb = rb.at[0:N_GROUPS].set(b_grp).at[EXP_ROW0:EXP_ROW0 + N_EXPERTS].set(b_exp)
    return rw, rb[:, None]


def kernel(x, mem, positions, w_in, w_pool_grp, pool_scale, ret_gn_w, w_mem_kv, w_br_pool, w_br_ret, w_br_xa,
           w_out, ln1_w, ln1_b, w_grp_router, b_grp_router, w_exp_router, b_exp_router, w_exp_gate, w_exp_up,
           w_exp_down, ln2_w, ln2_b):
    B, S, D = x.shape
    assert D == D_MODEL and w_in.shape[0] == DEPTH and S % 512 == 0
    T = B * S
    M = mem.shape[1]
    l = 0
    xf = x.reshape(T, D)

    cos, sin = _rope_table(positions.reshape(T, 1))
    proj = _in_proj(xf, _reorder_in_cols(w_in[l]).astype(BF16))
    y_pool = _pool(proj, w_pool_grp[l].astype(BF16), pool_scale[l][None, :], S)
    y_ret = _retention(proj, cos, sin, ret_gn_w[l].reshape(1, -1), B, S)
    kv = _mem_kv(mem.reshape(B * M, D), w_mem_kv[l].astype(BF16))
    y_xa = _cross_attention(proj, kv, B, S, M)
    rw, rb = _router_params(w_grp_router[l], b_grp_router[l], w_exp_router[l], b_exp_router[l])
    x1, x1b, gate = _merge(y_pool, y_ret, y_xa, proj, xf, w_br_pool[l].astype(BF16), w_br_ret[l].astype(BF16),
                           w_br_xa[l].astype(BF16), w_out[l].astype(BF16), ln1_w[l][None, :], ln1_b[l][None, :],
                           rw, rb)
    out = _moe_dense(x1b, x1, gate,
                     w_exp_gate[l].reshape(N_EXPERTS, D_MODEL, D_EXPERT).astype(BF16),
                     w_exp_up[l].reshape(N_EXPERTS, D_MODEL, D_EXPERT).astype(BF16),
                     w_exp_down[l].reshape(N_EXPERTS, D_EXPERT, D_MODEL).astype(BF16),
                     ln2_w[l][None, :], ln2_b[l][None, :])
    return out.reshape(B, S, D)
```

```python
import functools

import numpy as np
import jax
import jax.numpy as jnp
from jax import lax
from jax.experimental import pallas as pl
from jax.experimental.pallas import tpu as pltpu
from jax.experimental.pallas import tpu_sc as plsc

F32 = jnp.float32
BF16 = jnp.bfloat16
I32 = jnp.int32
U32 = jnp.uint32

D_MODEL = 1024
POOL_WINDOWS = (2, 4, 8, 16)
POOL_GROUP_DIM = 128
POOL_WIDTH = 512
POOL_HALO = 16
RET_HEADS = 4
RET_QK_DIM = 128
RET_V_DIM = 256
RET_CHUNK = 128
ROPE_BASE = 10000.0
XA_HEADS = 4
XA_HEAD_DIM = 128
XA_WIDTH = 512
N_GROUPS = 4
EXPERTS_PER_GROUP = 8
N_EXPERTS = N_GROUPS * EXPERTS_PER_GROUP
D_EXPERT = 256
LN_EPS = 1e-5
DEPTH = 1
ALPHA = (2.0 * DEPTH) ** 0.25
NEG_BIG = -1e30

COL_GATES, COL_V, COL_G, COL_POOL, COL_Q, COL_K, COL_XAQ = 0, 3072, 4096, 5120, 5632, 6144, 6656
IN_TOTAL = 7168

VMEM_LIMIT = 56 * 1024 * 1024

TOP_K = 2
PACK_W = D_MODEL // 2
MOE_TM = 256
SC_CHUNK = 128
RANK_CHUNK = 512


def _dot(a, b):
    return jnp.dot(a, b, preferred_element_type=F32)


def _dot_nt(a, b, precision=None):
    return lax.dot_general(a, b, (((1,), (1,)), ((), ())), preferred_element_type=F32, precision=precision)


def _sigmoid(z):
    return 1.0 / (1.0 + jnp.exp(-z))


def _layer_norm(h, w, b):
    mu = jnp.mean(h, axis=-1, keepdims=True)
    hc = h - mu
    var = jnp.mean(hc * hc, axis=-1, keepdims=True)
    return hc * lax.rsqrt(var + LN_EPS) * w + b


def _rope_kernel(pos_ref, freq_ref, sign_ref, cos_ref, sin_ref):
    ang = pos_ref[...].astype(F32) * freq_ref[...]
    cos_ref[...] = jnp.cos(ang)
    sin_ref[...] = jnp.sin(ang) * sign_ref[...]


def _rope_table(pos_col, tile=2048):
    T = pos_col.shape[0]
    half = RET_QK_DIM // 2
    inv_freq = (ROPE_BASE ** (-np.arange(half, dtype=np.float64) / half)).astype(np.float32)
    freq = jnp.asarray(np.concatenate([inv_freq, inv_freq])[None, :])
    sign = jnp.asarray(np.concatenate([-np.ones(half, np.float32), np.ones(half, np.float32)])[None, :])
    row = pl.BlockSpec((1, RET_QK_DIM), lambda i: (0, 0))
    out = pl.BlockSpec((tile, RET_QK_DIM), lambda i: (i, 0))
    return pl.pallas_call(
        _rope_kernel,
        grid=(T // tile,),
        in_specs=[pl.BlockSpec((tile, 1), lambda i: (i, 0)), row, row],
        out_specs=[out, out],
        out_shape=[jax.ShapeDtypeStruct((T, RET_QK_DIM), F32)] * 2,
        name="rope_table",
    )(pos_col, freq, sign)


def _inproj_kernel(x_ref, w_ref, o_ref, xb_ref):
    @pl.when(pl.program_id(1) == 0)
    def _():
        xb_ref[...] = x_ref[...].astype(BF16)

    o_ref[...] = _dot(xb_ref[...], w_ref[...]).astype(BF16)


def _in_proj(xf, w_in_b, bm=1024, bn=1024):
    T, D = xf.shape
    N = w_in_b.shape[1]
    return pl.pallas_call(
        _inproj_kernel,
        grid=(T // bm, N // bn),
        in_specs=[pl.BlockSpec((bm, D), lambda i, j: (i, 0)),
                  pl.BlockSpec((D, bn), lambda i, j: (0, j))],
        out_specs=pl.BlockSpec((bm, bn), lambda i, j: (i, j)),
        out_shape=jax.ShapeDtypeStruct((T, N), BF16),
        scratch_shapes=[pltpu.VMEM((bm, D), BF16)],
        compiler_params=pltpu.CompilerParams(dimension_semantics=("arbitrary", "arbitrary"),
                                             vmem_limit_bytes=VMEM_LIMIT),
        name="in_proj",
    )(xf, w_in_b)


POOL_SUB = 256


def _pool_bands():
    r = np.arange(POOL_SUB)[:, None]
    c = np.arange(POOL_SUB)[None, :]
    ch = np.arange(POOL_HALO)[None, :] - POOL_HALO
    main = np.stack([((r - c >= 0) & (r - c < w)) for w in POOL_WINDOWS]).astype(np.float32)
    halo = np.stack([((r - ch >= 0) & (r - ch < w)) for w in POOL_WINDOWS]).astype(np.float32)
    return jnp.asarray(main, BF16), jnp.asarray(halo, BF16)


def _pool_kernel(u_ref, halo_ref, bmain_ref, bhalo_ref, wg_ref, scale_ref, o_ref, *, tile, seq):
    s0 = lax.rem(pl.program_id(0) * tile, seq)
    first = s0 == 0
    for sb in range(tile // POOL_SUB):
        r0 = sb * POOL_SUB
        u = u_ref[r0:r0 + POOL_SUB, :]
        if sb == 0:
            prev = halo_ref[...]
            prev = jnp.where(first, jnp.zeros_like(prev), prev)
        else:
            prev = u_ref[r0 - POOL_HALO:r0, :]
        pos = s0 + r0 + lax.broadcasted_iota(jnp.int32, (POOL_SUB, POOL_GROUP_DIM), 0)
        for g, w in enumerate(POOL_WINDOWS):
            cols = slice(g * POOL_GROUP_DIM, (g + 1) * POOL_GROUP_DIM)
            ug = u[:, cols]
            wsum = _dot(bmain_ref[g], ug) + _dot(bhalo_ref[g], prev[:, cols])
            cnt = jnp.minimum(pos + 1, w).astype(F32)
            pooled = wsum / cnt - ug.astype(F32)
            mixed = _dot(pooled.astype(BF16), wg_ref[g]) * scale_ref[:, cols]
            o_ref[r0:r0 + POOL_SUB, cols] = mixed.astype(BF16)


def _pool(proj, w_grp_b, scale, seq, tile=512):
    T = proj.shape[0]
    bmain, bhalo = _pool_bands()
    cb = COL_POOL // POOL_WIDTH
    hb = tile // POOL_HALO
    full = lambda a: pl.BlockSpec(a.shape, lambda i: (0,) * a.ndim)
    return pl.pallas_call(
        functools.partial(_pool_kernel, tile=tile, seq=seq),
        grid=(T // tile,),
        in_specs=[pl.BlockSpec((tile, POOL_WIDTH), lambda i: (i, cb)),
                  pl.BlockSpec((POOL_HALO, POOL_WIDTH), lambda i: (jnp.maximum(i * hb - 1, 0), cb)),
                  full(bmain), full(bhalo), full(w_grp_b), full(scale)],
        out_specs=pl.BlockSpec((tile, POOL_WIDTH), lambda i: (i, 0)),
        out_shape=jax.ShapeDtypeStruct((T, POOL_WIDTH), BF16),
        name="pool",
    )(proj, proj, bmain, bhalo, w_grp_b, scale)


def _ret_consts():
    h = np.arange(RET_HEADS, dtype=np.float64)
    log_gamma = np.log1p(-np.exp2(-5.0 - h))
    pos = np.arange(RET_CHUNK, dtype=np.float64)
    diff = pos[:, None] - pos[None, :]
    dmask = np.where(diff >= 0, np.exp(log_gamma[:, None, None] * np.maximum(diff, 0.0)), 0.0)
    qdec = np.exp(log_gamma[:, None] * (pos + 1.0)[None, :])
    kdec = np.exp(log_gamma[:, None] * (RET_CHUNK - 1.0 - pos)[None, :])
    cdec = np.exp(log_gamma * RET_CHUNK)
    lanes = lambda a: np.broadcast_to(a[:, :, None], (RET_HEADS, RET_CHUNK, RET_QK_DIM))
    return (jnp.asarray(dmask, F32), jnp.asarray(lanes(qdec), F32), jnp.asarray(lanes(kdec), F32),
            tuple(float(v) for v in cdec))


def _ret_kernel(q_ref, k_ref, v_ref, g_ref, cos_ref, sin_ref, dmask_ref, qdec_ref, kdec_ref, gnw_ref,
                o_ref, state_ref, *, rb, cdec):
    @pl.when(pl.program_id(1) == 0)
    def _():
        state_ref[...] = jnp.zeros_like(state_ref)

    kscale = RET_QK_DIM ** -0.5
    half = RET_QK_DIM // 2
    for c in range(rb // RET_CHUNK):
        rows = slice(c * RET_CHUNK, (c + 1) * RET_CHUNK)
        cos = cos_ref[rows, :]
        sin = sin_ref[rows, :]
        for h in range(RET_HEADS):
            qk_cols = slice(h * RET_QK_DIM, (h + 1) * RET_QK_DIM)
            v_cols = slice(h * RET_V_DIM, (h + 1) * RET_V_DIM)
            q = q_ref[rows, qk_cols].astype(F32)
            k = k_ref[rows, qk_cols].astype(F32)
            qr = q * cos + pltpu.roll(q, half, 1) * sin
            kr = (k * cos + pltpu.roll(k, half, 1) * sin) * kscale
            v = v_ref[rows, v_cols]
            scores = _dot_nt(qr.astype(BF16), kr.astype(BF16)) * dmask_ref[h]
            intra = _dot(scores.astype(BF16), v)
            st = state_ref[h]
            cross = _dot((qr * qdec_ref[h]).astype(BF16), st.astype(BF16))
            kd_t = jnp.transpose(kr * kdec_ref[h]).astype(BF16)
            state_ref[h] = cdec[h] * st + _dot(kd_t, v)
            y = intra + cross
            mu = jnp.mean(y, axis=-1, keepdims=True)
            yc = y - mu
            var = jnp.mean(yc * yc, axis=-1, keepdims=True)
            yn = yc * lax.rsqrt(var + LN_EPS) * gnw_ref[:, v_cols]
            gt = g_ref[rows, v_cols].astype(F32)
            o_ref[rows, v_cols] = (gt * _sigmoid(gt) * yn).astype(BF16)


def _retention(proj, cos, sin, gnw, batch, seq, rb=512):
    T = proj.shape[0]
    dmask, qdec, kdec, cdec = _ret_consts()
    nj = seq // rb
    row = lambda b, j: b * nj + j
    full = lambda a: pl.BlockSpec(a.shape, lambda b, j: (0,) * a.ndim)
    vw = RET_HEADS * RET_V_DIM
    qw = RET_HEADS * RET_QK_DIM
    return pl.pallas_call(
        functools.partial(_ret_kernel, rb=rb, cdec=cdec),
        grid=(batch, nj),
        in_specs=[pl.BlockSpec((rb, qw), lambda b, j: (row(b, j), COL_Q // qw)),
                  pl.BlockSpec((rb, qw), lambda b, j: (row(b, j), COL_K // qw)),
                  pl.BlockSpec((rb, vw), lambda b, j: (row(b, j), COL_V // vw)),
                  pl.BlockSpec((rb, vw), lambda b, j: (row(b, j), COL_G // vw)),
                  pl.BlockSpec((rb, RET_QK_DIM), lambda b, j: (row(b, j), 0)),
                  pl.BlockSpec((rb, RET_QK_DIM), lambda b, j: (row(b, j), 0)),
                  full(dmask), full(qdec), full(kdec), full(gnw)],
        out_specs=pl.BlockSpec((rb, vw), lambda b, j: (row(b, j), 0)),
        out_shape=jax.ShapeDtypeStruct((T, vw), BF16),
        scratch_shapes=[pltpu.VMEM((RET_HEADS, RET_QK_DIM, RET_V_DIM), F32)],
        compiler_params=pltpu.CompilerParams(dimension_semantics=("arbitrary", "arbitrary"),
                                             vmem_limit_bytes=VMEM_LIMIT),
        name="retention",
    )(proj, proj, proj, proj, cos, sin, dmask, qdec, kdec, gnw)


def _memkv_kernel(m_ref, w_ref, o_ref):
    o_ref[...] = _dot(m_ref[...].astype(BF16), w_ref[...]).astype(BF16)


def _mem_kv(memf, w_b):
    M, D = memf.shape
    N = w_b.shape[1]
    return pl.pallas_call(
        _memkv_kernel,
        grid=(1,),
        in_specs=[pl.BlockSpec((M, D), lambda i: (0, 0)), pl.BlockSpec((D, N), lambda i: (0, 0))],
        out_specs=pl.BlockSpec((M, N), lambda i: (0, 0)),
        out_shape=jax.ShapeDtypeStruct((M, N), BF16),
        name="mem_kv",
    )(memf, w_b)


def _xa_kernel(q_ref, k_ref, v_ref, o_ref):
    scale = XA_HEAD_DIM ** -0.5
    for h in range(XA_HEADS):
        cols = slice(h * XA_HEAD_DIM, (h + 1) * XA_HEAD_DIM)
        s = _dot_nt(q_ref[:, cols], k_ref[:, cols]) * scale
        m = jnp.max(s, axis=-1, keepdims=True)
        p = jnp.exp(s - m)
        l = jnp.sum(p, axis=-1, keepdims=True)
        o = _dot(p.astype(BF16), v_ref[:, cols]) / l
        o_ref[:, cols] = o.astype(BF16)


def _cross_attention(proj, kv, batch, seq, mem_len, tile=512):
    T = proj.shape[0]
    nj = seq // tile
    return pl.pallas_call(
        _xa_kernel,
        grid=(batch, nj),
        in_specs=[pl.BlockSpec((tile, XA_WIDTH), lambda b, j: (b * nj + j, COL_XAQ // XA_WIDTH)),
                  pl.BlockSpec((mem_len, XA_WIDTH), lambda b, j: (b, 0)),
                  pl.BlockSpec((mem_len, XA_WIDTH), lambda b, j: (b, 1))],
        out_specs=pl.BlockSpec((tile, XA_WIDTH), lambda b, j: (b * nj + j, 0)),
        out_shape=jax.ShapeDtypeStruct((T, XA_WIDTH), BF16),
        name="cross_attention",
    )(proj, kv, kv)


ROUTER_ROWS = 128
EXP_ROW0 = 8


def _route(logits_t):
    gl = logits_t[0:8]
    gmax = jnp.max(gl, axis=0, keepdims=True)
    p_grp = 1.0 / jnp.sum(jnp.exp(gl - gmax), axis=0, keepdims=True)
    idx8 = lax.broadcasted_iota(jnp.int32, gl.shape, 0)
    gsel = jnp.min(jnp.where(gl == gmax, idx8, 8), axis=0, keepdims=True)
    cl = jnp.zeros_like(gl)
    for g in range(N_GROUPS):
        r0 = EXP_ROW0 + g * EXPERTS_PER_GROUP
        cl = cl + jnp.where(gsel == g, logits_t[r0:r0 + EXPERTS_PER_GROUP], 0.0)
    v1 = jnp.max(cl, axis=0, keepdims=True)
    i1 = jnp.min(jnp.where(cl == v1, idx8, 8), axis=0, keepdims=True)
    cl2 = jnp.where(idx8 == i1, -jnp.inf, cl)
    v2 = jnp.max(cl2, axis=0, keepdims=True)
    i2 = jnp.min(jnp.where(cl2 == v2, idx8, 8), axis=0, keepdims=True)
    e21 = jnp.exp(v2 - v1)
    w1 = p_grp / (1.0 + e21)
    w2 = p_grp * e21 / (1.0 + e21)
    return gsel * EXPERTS_PER_GROUP + i1, gsel * EXPERTS_PER_GROUP + i2, w1, w2


def _pack_halves(v):
    half = v.shape[1] // 2
    lo = lax.bitcast_convert_type(v[:, :half].astype(BF16).astype(F32), U32)
    hi = lax.bitcast_convert_type(v[:, half:].astype(BF16).astype(F32), U32)
    return lax.bitcast_convert_type(lax.shift_right_logical(lo, U32(16)) | hi, I32)


def _unpack_halves(w):
    u = lax.bitcast_convert_type(w, U32)
    lo = lax.bitcast_convert_type(lax.shift_left(u, U32(16)), F32)
    hi = lax.bitcast_convert_type(u & U32(0xFFFF0000), F32)
    return lo, hi


def _merge_kernel(yp_ref, yr_ref, ya_ref, gl_ref, x_ref, wp_ref, wr_ref, wa_ref, wo_ref, lnw_ref, lnb_ref,
                  rw_ref, rb_ref, x1_ref, xp_ref, eid_ref, wtok_ref):
    merged = _sigmoid(gl_ref[:, 0:D_MODEL].astype(F32)) * _dot(yp_ref[...], wp_ref[...])
    merged = merged + _sigmoid(gl_ref[:, D_MODEL:2 * D_MODEL].astype(F32)) * _dot(yr_ref[...], wr_ref[...])
    merged = merged + _sigmoid(gl_ref[:, 2 * D_MODEL:3 * D_MODEL].astype(F32)) * _dot(ya_ref[...], wa_ref[...])
    h = ALPHA * x_ref[...] + _dot(merged.astype(BF16), wo_ref[...])
    x1 = _layer_norm(h, lnw_ref[...], lnb_ref[...])
    x1_ref[...] = x1
    xp_ref[...] = _pack_halves(x1)
    logits_t = _dot_nt(rw_ref[...], x1, precision=lax.Precision.HIGHEST) + rb_ref[...]
    e0, e1, w0, w1 = _route(logits_t)
    tile = e0.shape[1]
    eid_ref[...] = jnp.concatenate([e0, e1, jnp.zeros((8 - TOP_K, tile), I32)], axis=0)
    w_t = jnp.concatenate([w0, w1, jnp.zeros((ROUTER_ROWS - TOP_K, tile), F32)], axis=0)
    wtok_ref[...] = jnp.transpose(w_t)


def _merge(y_pool, y_ret, y_xa, proj, xf, wp, wr, wa, wo, lnw, lnb, rw, rb, tile=512):
    T = xf.shape[0]
    full = lambda a: pl.BlockSpec(a.shape, lambda i: (0,) * a.ndim)
    rowblk = lambda w: pl.BlockSpec((tile, w), lambda i: (i, 0))
    gw = 3 * D_MODEL
    return pl.pallas_call(
        _merge_kernel,
        grid=(T // tile,),
        in_specs=[rowblk(POOL_WIDTH), rowblk(D_MODEL), rowblk(XA_WIDTH),
                  pl.BlockSpec((tile, gw), lambda i: (i, COL_GATES // gw)),
                  rowblk(D_MODEL), full(wp), full(wr), full(wa), full(wo), full(lnw), full(lnb),
                  full(rw), full(rb)],
        out_specs=[rowblk(D_MODEL), rowblk(PACK_W), pl.BlockSpec((8, tile), lambda i: (0, i)),
                   rowblk(ROUTER_ROWS)],
        out_shape=[jax.ShapeDtypeStruct((T, D_MODEL), F32), jax.ShapeDtypeStruct((T, PACK_W), I32),
                   jax.ShapeDtypeStruct((8, T), I32), jax.ShapeDtypeStruct((T, ROUTER_ROWS), F32)],
        compiler_params=pltpu.CompilerParams(dimension_semantics=("arbitrary",), vmem_limit_bytes=VMEM_LIMIT),
        name="merge_ln1_router",
    )(y_pool, y_ret, y_xa, proj, xf, wp, wr, wa, wo, lnw, lnb, rw, rb)


META_LANES = 256


def _positions_kernel(eid_ref, tri_ref, low_ref, pos_ref, meta_ref, *, n_tok):
    n_chunks = n_tok // RANK_CHUNK
    erow = lax.broadcasted_iota(I32, (N_EXPERTS, RANK_CHUNK), 0)

    def onehot(c):
        sl = slice(c * RANK_CHUNK, (c + 1) * RANK_CHUNK)
        m0 = eid_ref[0:1, sl] == erow
        m1 = eid_ref[1:2, sl] == erow
        return m0, m1, jnp.where(m0, 1.0, 0.0) + jnp.where(m1, 1.0, 0.0)

    counts = jnp.zeros((N_EXPERTS, 1), F32)
    for c in range(n_chunks):
        counts = counts + jnp.sum(onehot(c)[2], axis=1, keepdims=True)
    ptiles = jnp.floor((counts + (MOE_TM - 1)) * (1.0 / MOE_TM))
    ptiles_b = jnp.broadcast_to(ptiles, (N_EXPERTS, 128)).astype(BF16)
    start = _dot(low_ref[...], ptiles_b)[:, 0:1] * MOE_TM
    end = start + ptiles * MOE_TM

    pos_ref[...] = jnp.zeros_like(pos_ref)
    carry = start - 1.0
    for c in range(n_chunks):
        sl = slice(c * RANK_CHUNK, (c + 1) * RANK_CHUNK)
        m0, m1, oh = onehot(c)
        rank = _dot(oh.astype(BF16), tri_ref[...]) + carry
        pos_ref[0:1, sl] = jnp.sum(jnp.where(m0, rank, 0.0), axis=0, keepdims=True).astype(I32)
        pos_ref[1:2, sl] = jnp.sum(jnp.where(m1, rank, 0.0), axis=0, keepdims=True).astype(I32)
        carry = carry + jnp.sum(oh, axis=1, keepdims=True)

    tile_start = lax.broadcasted_iota(I32, (N_EXPERTS, META_LANES), 1).astype(F32) * MOE_TM
    tile_e = jnp.sum(jnp.where(end <= tile_start, 1.0, 0.0), axis=0, keepdims=True)
    tile_e = jnp.minimum(tile_e, N_EXPERTS - 1.0)
    n_used = jnp.broadcast_to(jnp.sum(ptiles, axis=0, keepdims=True), (1, META_LANES))
    meta_ref[...] = jnp.concatenate([tile_e, n_used, jnp.zeros((6, META_LANES), F32)], axis=0).astype(I32)


def _positions(eid):
    T = eid.shape[1]
    r = np.arange(RANK_CHUNK)
    tri = jnp.asarray(r[:, None] <= r[None, :], BF16)
    e = np.arange(N_EXPERTS)
    low = jnp.asarray(e[None, :] < e[:, None], BF16)
    full = lambda a: pl.BlockSpec(a.shape, lambda i: (0,) * a.ndim)
    return pl.pallas_call(
        functools.partial(_positions_kernel, n_tok=T),
        grid=(1,),
        in_specs=[full(eid), full(tri), full(low)],
        out_specs=[pl.BlockSpec((8, T), lambda i: (0, 0)), pl.BlockSpec((8, META_LANES), lambda i: (0, 0))],
        out_shape=[jax.ShapeDtypeStruct((8, T), I32), jax.ShapeDtypeStruct((8, META_LANES), I32)],
        name="route_positions",
    )(eid, tri, low)


def _sc_workers():
    info = plsc.get_sparse_core_info()
    return info.num_cores, info.num_cores * info.num_subcores


def _sc_scatter_rows(xp, pos2d, n_out):
    T, W = xp.shape
    n_cores, n_workers = _sc_workers()
    cpw = T // SC_CHUNK // n_workers
    mesh = plsc.VectorSubcoreMesh(core_axis_name="c", subcore_axis_name="s")

    @functools.partial(
        pl.kernel, mesh=mesh, out_type=jax.ShapeDtypeStruct((n_out, W), I32),
        scratch_types=[pltpu.VMEM((TOP_K * cpw, SC_CHUNK), I32), pltpu.VMEM((SC_CHUNK, W), I32)],
        name="sc_scatter_rows")
    def k(x_hbm, pos_hbm, out_hbm, idx_v, rows_v):
        wid = lax.axis_index("s") * n_cores + lax.axis_index("c")
        for s in range(TOP_K):
            pltpu.sync_copy(pos_hbm.at[pl.ds(s * (T // SC_CHUNK) + wid * cpw, cpw)],
                            idx_v.at[pl.ds(s * cpw, cpw)])
        for j in range(cpw):
            pltpu.sync_copy(x_hbm.at[pl.ds((wid * cpw + j) * SC_CHUNK, SC_CHUNK)], rows_v)
            for s in range(TOP_K):
                pltpu.sync_copy(rows_v, out_hbm.at[idx_v.at[s * cpw + j]])

    return k(xp, pos2d)


def _sc_gather_rows(y, idx2d):
    W = y.shape[1]
    n = idx2d.shape[0] * SC_CHUNK
    n_cores, n_workers = _sc_workers()
    cpw = n // SC_CHUNK // n_workers
    mesh = plsc.VectorSubcoreMesh(core_axis_name="c", subcore_axis_name="s")

    @functools.partial(
        pl.kernel, mesh=mesh, out_type=jax.ShapeDtypeStruct((n, W), I32),
        scratch_types=[pltpu.VMEM((cpw, SC_CHUNK), I32), pltpu.VMEM((SC_CHUNK, W), I32)],
        name="sc_gather_rows")
    def k(y_hbm, idx_hbm, out_hbm, idx_v, rows_v):
        wid = lax.axis_index("s") * n_cores + lax.axis_index("c")
        pltpu.sync_copy(idx_hbm.at[pl.ds(wid * cpw, cpw)], idx_v)
        for j in range(cpw):
            pltpu.sync_copy(y_hbm.at[idx_v.at[j]], rows_v)
            pltpu.sync_copy(rows_v, out_hbm.at[pl.ds((wid * cpw + j) * SC_CHUNK, SC_CHUNK)])

    return k(y, idx2d)


def _routed_kernel(te_ref, nu_ref, xs_ref, wg_ref, wu_ref, wd_ref, ys_ref):
    used = pl.program_id(0) < nu_ref[0]

    @pl.when(used)
    def _():
        lo, hi = _unpack_halves(xs_ref[...])
        lo = lo.astype(BF16)
        hi = hi.astype(BF16)
        a = _dot(lo, wg_ref[0, :PACK_W, :]) + _dot(hi, wg_ref[0, PACK_W:, :])
        b = _dot(lo, wu_ref[0, :PACK_W, :]) + _dot(hi, wu_ref[0, PACK_W:, :])
        act = (a * _sigmoid(a) * b).astype(BF16)
        ys_ref[...] = _pack_halves(_dot(act, wd_ref[0]))

    @pl.when(jnp.logical_not(used))
    def _():
        ys_ref[...] = jnp.zeros_like(ys_ref)


def _routed_mlp(tile_e, n_used, xs, wg, wu, wd):
    R = xs.shape[0]
    rows = pl.BlockSpec((MOE_TM, PACK_W), lambda i, te, nu: (i, 0))
    return pl.pallas_call(
        _routed_kernel,
        grid_spec=pltpu.PrefetchScalarGridSpec(
            num_scalar_prefetch=2,
            grid=(R // MOE_TM,),
            in_specs=[rows,
                      pl.BlockSpec((1, D_MODEL, D_EXPERT), lambda i, te, nu: (te[i], 0, 0)),
                      pl.BlockSpec((1, D_MODEL, D_EXPERT), lambda i, te, nu: (te[i], 0, 0)),
                      pl.BlockSpec((1, D_EXPERT, D_MODEL), lambda i, te, nu: (te[i], 0, 0))],
            out_specs=rows),
        out_shape=jax.ShapeDtypeStruct((R, PACK_W), I32),
        compiler_params=pltpu.CompilerParams(dimension_semantics=("arbitrary",)),
        name="routed_mlp",
    )(tile_e, n_used, xs, wg, wu, wd)


def _combine_kernel(x1_ref, y0_ref, y1_ref, wtok_ref, lnw_ref, lnb_ref, o_ref):
    w0 = wtok_ref[:, 0:1]
    w1 = wtok_ref[:, 1:2]
    y0lo, y0hi = _unpack_halves(y0_ref[...])
    y1lo, y1hi = _unpack_halves(y1_ref[...])
    moe = jnp.concatenate([w0 * y0lo + w1 * y1lo, w0 * y0hi + w1 * y1hi], axis=1)
    o_ref[...] = _layer_norm(ALPHA * x1_ref[...] + moe, lnw_ref[...], lnb_ref[...])


def _combine_ln2(x1, yg, wtok, lnw, lnb, tile=512):
    T = x1.shape[0]
    nt = T // tile
    full = lambda a: pl.BlockSpec(a.shape, lambda i: (0,) * a.ndim)
    return pl.pallas_call(
        _combine_kernel,
        grid=(nt,),
        in_specs=[pl.BlockSpec((tile, D_MODEL), lambda i: (i, 0)),
                  pl.BlockSpec((tile, PACK_W), lambda i: (i, 0)),
                  pl.BlockSpec((tile, PACK_W), lambda i: (i + nt, 0)),
                  pl.BlockSpec((tile, ROUTER_ROWS), lambda i: (i, 0)),
                  full(lnw), full(lnb)],
        out_specs=pl.BlockSpec((tile, D_MODEL), lambda i: (i, 0)),
        out_shape=jax.ShapeDtypeStruct((T, D_MODEL), F32),
        name="combine_ln2",
    )(x1, yg, yg, wtok, lnw, lnb)


def _reorder_in_cols(w):
    pool, q, k, v, g, xaq, gates = jnp.split(w, [512, 1024, 1536, 2560, 3584, 4096], axis=-1)
    return jnp.concatenate([gates, v, g, pool, q, k, xaq], axis=-1)


def _router_params(w_grp, b_grp, w_exp, b_exp):
    rw = jnp.zeros((ROUTER_ROWS, D_MODEL), F32)
    rw = rw.at[0:N_GROUPS].set(w_grp.T).at[EXP_ROW0:EXP_ROW0 + N_EXPERTS].set(w_exp.T)
    rb = jnp.zeros((ROUTER_ROWS,), F32).at[N_GROUPS:8].set(NEG_BIG)
    rb = rb.at[0:N_GROUPS].set(b_grp).at[EXP_ROW0:EXP_ROW0 + N_EXPERTS].set(b_exp)
    return rw, rb[:, None]


def kernel(x, mem, positions, w_in, w_pool_grp, pool_scale, ret_gn_w, w_mem_kv, w_br_pool, w_br_ret, w_br_xa,
           w_out, ln1_w, ln1_b, w_grp_router, b_grp_router, w_exp_router, b_exp_router, w_exp_gate, w_exp_up,
           w_exp_down, ln2_w, ln2_b):
    B, S, D = x.shape
    assert D == D_MODEL and w_in.shape[0] == DEPTH and S % 512 == 0
    T = B * S
    M = mem.shape[1]
    l = 0
    xf = x.reshape(T, D)

    cos, sin = _rope_table(positions.reshape(T, 1))
    proj = _in_proj(xf, _reorder_in_cols(w_in[l]).astype(BF16))
    y_pool = _pool(proj, w_pool_grp[l].astype(BF16), pool_scale[l][None, :], S)
    y_ret = _retention(proj, cos, sin, ret_gn_w[l].reshape(1, -1), B, S)
    kv = _mem_kv(mem.reshape(B * M, D), w_mem_kv[l].astype(BF16))
    y_xa = _cross_attention(proj, kv, B, S, M)
    rw, rb = _router_params(w_grp_router[l], b_grp_router[l], w_exp_router[l], b_exp_router[l])
    x1, xp, eid, wtok = _merge(y_pool, y_ret, y_xa, proj, xf, w_br_pool[l].astype(BF16),
                               w_br_ret[l].astype(BF16), w_br_xa[l].astype(BF16), w_out[l].astype(BF16),
                               ln1_w[l][None, :], ln1_b[l][None, :], rw, rb)

    pos, meta = _positions(eid)
    pos2d = pos[0:TOP_K].reshape(TOP_K * T // SC_CHUNK, SC_CHUNK)
    n_tiles = -(-((TOP_K * T + N_EXPERTS * (MOE_TM - 1)) // MOE_TM) // 8) * 8
    assert n_tiles <= META_LANES
    xs = _sc_scatter_rows(xp, pos2d, n_tiles * MOE_TM)
    ys = _routed_mlp(meta[0, :n_tiles], meta[1, :1], xs,
                     w_exp_gate[l].reshape(N_EXPERTS, D_MODEL, D_EXPERT).astype(BF16),
                     w_exp_up[l].reshape(N_EXPERTS, D_MODEL, D_EXPERT).astype(BF16),
                     w_exp_down[l].reshape(N_EXPERTS, D_EXPERT, D_MODEL).astype(BF16))
    yg = _sc_gather_rows(ys, pos2d)
    out = _combine_ln2(x1, yg, wtok, ln2_w[l][None, :], ln2_b[l][None, :])
    return out.reshape(B, S, D)
```

```python
import functools

import numpy as np
import jax
import jax.numpy as jnp
from jax import lax
from jax.experimental import pallas as pl
from jax.experimental.pallas import tpu as pltpu
from jax.experimental.pallas import tpu_sc as plsc

F32 = jnp.float32
BF16 = jnp.bfloat16
I32 = jnp.int32
U32 = jnp.uint32

D_MODEL = 1024
POOL_WINDOWS = (2, 4, 8, 16)
POOL_GROUP_DIM = 128
POOL_WIDTH = 512
POOL_HALO = 16
RET_HEADS = 4
RET_QK_DIM = 128
RET_V_DIM = 256
RET_CHUNK = 128
ROPE_BASE = 10000.0
XA_HEADS = 4
XA_HEAD_DIM = 128
XA_WIDTH = 512
N_GROUPS = 4
EXPERTS_PER_GROUP = 8
N_EXPERTS = N_GROUPS * EXPERTS_PER_GROUP
D_EXPERT = 256
LN_EPS = 1e-5
DEPTH = 1
ALPHA = (2.0 * DEPTH) ** 0.25
NEG_BIG = -1e30

COL_GATES, COL_V, COL_G, COL_POOL, COL_Q, COL_K, COL_XAQ = 0, 3072, 4096, 5120, 5632, 6144, 6656
IN_TOTAL = 7168

VMEM_LIMIT = 56 * 1024 * 1024

TOP_K = 2
PACK_W = D_MODEL // 2
MOE_TM = 512
SC_CHUNK = 128
RANK_CHUNK = 512


def _dot(a, b):
    return jnp.dot(a, b, preferred_element_type=F32)


def _dot_nt(a, b, precision=None):
    return lax.dot_general(a, b, (((1,), (1,)), ((), ())), preferred_element_type=F32, precision=precision)


def _sigmoid(z):
    return 1.0 / (1.0 + jnp.exp(-z))


def _layer_norm(h, w, b):
    mu = jnp.mean(h, axis=-1, keepdims=True)
    hc = h - mu
    var = jnp.mean(hc * hc, axis=-1, keepdims=True)
    return hc * lax.rsqrt(var + LN_EPS) * w + b


def _rope_kernel(pos_ref, freq_ref, sign_ref, cos_ref, sin_ref):
    ang = pos_ref[...].astype(F32) * freq_ref[...]
    cos_ref[...] = jnp.cos(ang)
    sin_ref[...] = jnp.sin(ang) * sign_ref[...]


def _rope_table(pos_col, tile=2048):
    T = pos_col.shape[0]
    half = RET_QK_DIM // 2
    inv_freq = (ROPE_BASE ** (-np.arange(half, dtype=np.float64) / half)).astype(np.float32)
    freq = jnp.asarray(np.concatenate([inv_freq, inv_freq])[None, :])
    sign = jnp.asarray(np.concatenate([-np.ones(half, np.float32), np.ones(half, np.float32)])[None, :])
    row = pl.BlockSpec((1, RET_QK_DIM), lambda i: (0, 0))
    out = pl.BlockSpec((tile, RET_QK_DIM), lambda i: (i, 0))
    return pl.pallas_call(
        _rope_kernel,
        grid=(T // tile,),
        in_specs=[pl.BlockSpec((tile, 1), lambda i: (i, 0)), row, row],
        out_specs=[out, out],
        out_shape=[jax.ShapeDtypeStruct((T, RET_QK_DIM), F32)] * 2,
        name="rope_table",
    )(pos_col, freq, sign)


def _inproj_kernel(x_ref, w_ref, o_ref, xb_ref):
    @pl.when(pl.program_id(1) == 0)
    def _():
        xb_ref[...] = x_ref[...].astype(BF16)

    o_ref[...] = _dot(xb_ref[...], w_ref[...]).astype(BF16)


def _in_proj(xf, w_in_b, bm=2048, bn=1024):
    T, D = xf.shape
    N = w_in_b.shape[1]
    return pl.pallas_call(
        _inproj_kernel,
        grid=(T // bm, N // bn),
        in_specs=[pl.BlockSpec((bm, D), lambda i, j: (i, 0)),
                  pl.BlockSpec((D, bn), lambda i, j: (0, j))],
        out_specs=pl.BlockSpec((bm, bn), lambda i, j: (i, j)),
        out_shape=jax.ShapeDtypeStruct((T, N), BF16),
        scratch_shapes=[pltpu.VMEM((bm, D), BF16)],
        compiler_params=pltpu.CompilerParams(dimension_semantics=("arbitrary", "arbitrary"),
                                             vmem_limit_bytes=VMEM_LIMIT),
        name="in_proj",
    )(xf, w_in_b)


POOL_SUB = 256


def _pool_bands():
    r = np.arange(POOL_SUB)[:, None]
    c = np.arange(POOL_SUB)[None, :]
    ch = np.arange(POOL_HALO)[None, :] - POOL_HALO
    main = np.stack([((r - c >= 0) & (r - c < w)) for w in POOL_WINDOWS]).astype(np.float32)
    halo = np.stack([((r - ch >= 0) & (r - ch < w)) for w in POOL_WINDOWS]).astype(np.float32)
    return jnp.asarray(main, BF16), jnp.asarray(halo, BF16)


def _pool_kernel(u_ref, halo_ref, bmain_ref, bhalo_ref, wg_ref, scale_ref, o_ref, *, tile, seq):
    s0 = lax.rem(pl.program_id(0) * tile, seq)
    first = s0 == 0
    for sb in range(tile // POOL_SUB):
        r0 = sb * POOL_SUB
        u = u_ref[r0:r0 + POOL_SUB, :]
        if sb == 0:
            prev = halo_ref[...]
            prev = jnp.where(first, jnp.zeros_like(prev), prev)
        else:
            prev = u_ref[r0 - POOL_HALO:r0, :]
        pos = s0 + r0 + lax.broadcasted_iota(jnp.int32, (POOL_SUB, POOL_GROUP_DIM), 0)
        for g, w in enumerate(POOL_WINDOWS):
            cols = slice(g * POOL_GROUP_DIM, (g + 1) * POOL_GROUP_DIM)
            ug = u[:, cols]
            wsum = _dot(bmain_ref[g], ug) + _dot(bhalo_ref[g], prev[:, cols])
            cnt = jnp.minimum(pos + 1, w).astype(F32)
            pooled = wsum / cnt - ug.astype(F32)
            mixed = _dot(pooled.astype(BF16), wg_ref[g]) * scale_ref[:, cols]
            o_ref[r0:r0 + POOL_SUB, cols] = mixed.astype(BF16)


def _pool(proj, w_grp_b, scale, seq, tile=512):
    T = proj.shape[0]
    bmain, bhalo = _pool_bands()
    cb = COL_POOL // POOL_WIDTH
    hb = tile // POOL_HALO
    full = lambda a: pl.BlockSpec(a.shape, lambda i: (0,) * a.ndim)
    return pl.pallas_call(
        functools.partial(_pool_kernel, tile=tile, seq=seq),
        grid=(T // tile,),
        in_specs=[pl.BlockSpec((tile, POOL_WIDTH), lambda i: (i, cb)),
                  pl.BlockSpec((POOL_HALO, POOL_WIDTH), lambda i: (jnp.maximum(i * hb - 1, 0), cb)),
                  full(bmain), full(bhalo), full(w_grp_b), full(scale)],
        out_specs=pl.BlockSpec((tile, POOL_WIDTH), lambda i: (i, 0)),
        out_shape=jax.ShapeDtypeStruct((T, POOL_WIDTH), BF16),
        name="pool",
    )(proj, proj, bmain, bhalo, w_grp_b, scale)


def _ret_consts():
    h = np.arange(RET_HEADS, dtype=np.float64)
    log_gamma = np.log1p(-np.exp2(-5.0 - h))
    pos = np.arange(RET_CHUNK, dtype=np.float64)
    diff = pos[:, None] - pos[None, :]
    dmask = np.where(diff >= 0, np.exp(log_gamma[:, None, None] * np.maximum(diff, 0.0)), 0.0)
    qdec = np.exp(log_gamma[:, None] * (pos + 1.0)[None, :])
    kdec = np.exp(log_gamma[:, None] * (RET_CHUNK - 1.0 - pos)[None, :])
    cdec = np.exp(log_gamma * RET_CHUNK)
    lanes = lambda a: np.broadcast_to(a[:, :, None], (RET_HEADS, RET_CHUNK, RET_QK_DIM))
    return (jnp.asarray(dmask, F32), jnp.asarray(lanes(qdec), F32), jnp.asarray(lanes(kdec), F32),
            tuple(float(v) for v in cdec))


def _ret_kernel(q_ref, k_ref, v_ref, g_ref, cos_ref, sin_ref, dmask_ref, qdec_ref, kdec_ref, gnw_ref,
                o_ref, state_ref, *, rb, cdec):
    @pl.when(pl.program_id(1) == 0)
    def _():
        state_ref[...] = jnp.zeros_like(state_ref)

    kscale = RET_QK_DIM ** -0.5
    half = RET_QK_DIM // 2
    for c in range(rb // RET_CHUNK):
        rows = slice(c * RET_CHUNK, (c + 1) * RET_CHUNK)
        cos = cos_ref[rows, :]
        sin = sin_ref[rows, :]
        for h in range(RET_HEADS):
            qk_cols = slice(h * RET_QK_DIM, (h + 1) * RET_QK_DIM)
            v_cols = slice(h * RET_V_DIM, (h + 1) * RET_V_DIM)
            q = q_ref[rows, qk_cols].astype(F32)
            k = k_ref[rows, qk_cols].astype(F32)
            qr = q * cos + pltpu.roll(q, half, 1) * sin
            kr = (k * cos + pltpu.roll(k, half, 1) * sin) * kscale
            v = v_ref[rows, v_cols]
            scores = _dot_nt(qr.astype(BF16), kr.astype(BF16)) * dmask_ref[h]
            intra = _dot(scores.astype(BF16), v)
            st = state_ref[h]
            cross = _dot((qr * qdec_ref[h]).astype(BF16), st.astype(BF16))
            kd_t = jnp.transpose(kr * kdec_ref[h]).astype(BF16)
            state_ref[h] = cdec[h] * st + _dot(kd_t, v)
            y = intra + cross
            mu = jnp.mean(y, axis=-1, keepdims=True)
            yc = y - mu
            var = jnp.mean(yc * yc, axis=-1, keepdims=True)
            yn = yc * lax.rsqrt(var + LN_EPS) * gnw_ref[:, v_cols]
            gt = g_ref[rows, v_cols].astype(F32)
            o_ref[rows, v_cols] = (gt * _sigmoid(gt) * yn).astype(BF16)


def _retention(proj, cos, sin, gnw, batch, seq, rb=512):
    T = proj.shape[0]
    dmask, qdec, kdec, cdec = _ret_consts()
    nj = seq // rb
    row = lambda b, j: b * nj + j
    full = lambda a: pl.BlockSpec(a.shape, lambda b, j: (0,) * a.ndim)
    vw = RET_HEADS * RET_V_DIM
    qw = RET_HEADS * RET_QK_DIM
    return pl.pallas_call(
        functools.partial(_ret_kernel, rb=rb, cdec=cdec),
        grid=(batch, nj),
        in_specs=[pl.BlockSpec((rb, qw), lambda b, j: (row(b, j), COL_Q // qw)),
                  pl.BlockSpec((rb, qw), lambda b, j: (row(b, j), COL_K // qw)),
                  pl.BlockSpec((rb, vw), lambda b, j: (row(b, j), COL_V // vw)),
                  pl.BlockSpec((rb, vw), lambda b, j: (row(b, j), COL_G // vw)),
                  pl.BlockSpec((rb, RET_QK_DIM), lambda b, j: (row(b, j), 0)),
                  pl.BlockSpec((rb, RET_QK_DIM), lambda b, j: (row(b, j), 0)),
                  full(dmask), full(qdec), full(kdec), full(gnw)],
        out_specs=pl.BlockSpec((rb, vw), lambda b, j: (row(b, j), 0)),
        out_shape=jax.ShapeDtypeStruct((T, vw), BF16),
        scratch_shapes=[pltpu.VMEM((RET_HEADS, RET_QK_DIM, RET_V_DIM), F32)],
        compiler_params=pltpu.CompilerParams(dimension_semantics=("arbitrary", "arbitrary"),
                                             vmem_limit_bytes=VMEM_LIMIT),
        name="retention",
    )(proj, proj, proj, proj, cos, sin, dmask, qdec, kdec, gnw)


def _memkv_kernel(m_ref, w_ref, o_ref):
    o_ref[...] = _dot(m_ref[...].astype(BF16), w_ref[...]).astype(BF16)


def _mem_kv(memf, w_b):
    M, D = memf.shape
    N = w_b.shape[1]
    return pl.pallas_call(
        _memkv_kernel,
        grid=(1,),
        in_specs=[pl.BlockSpec((M, D), lambda i: (0, 0)), pl.BlockSpec((D, N), lambda i: (0, 0))],
        out_specs=pl.BlockSpec((M, N), lambda i: (0, 0)),
        out_shape=jax.ShapeDtypeStruct((M, N), BF16),
        name="mem_kv",
    )(memf, w_b)


def _xa_kernel(q_ref, k_ref, v_ref, o_ref):
    scale = XA_HEAD_DIM ** -0.5
    for h in range(XA_HEADS):
        cols = slice(h * XA_HEAD_DIM, (h + 1) * XA_HEAD_DIM)
        s = _dot_nt(q_ref[:, cols], k_ref[:, cols]) * scale
        m = jnp.max(s, axis=-1, keepdims=True)
        p = jnp.exp(s - m)
        l = jnp.sum(p, axis=-1, keepdims=True)
        o = _dot(p.astype(BF16), v_ref[:, cols]) / l
        o_ref[:, cols] = o.astype(BF16)


def _cross_attention(proj, kv, batch, seq, mem_len, tile=512):
    T = proj.shape[0]
    nj = seq // tile
    return pl.pallas_call(
        _xa_kernel,
        grid=(batch, nj),
        in_specs=[pl.BlockSpec((tile, XA_WIDTH), lambda b, j: (b * nj + j, COL_XAQ // XA_WIDTH)),
                  pl.BlockSpec((mem_len, XA_WIDTH), lambda b, j: (b, 0)),
                  pl.BlockSpec((mem_len, XA_WIDTH), lambda b, j: (b, 1))],
        out_specs=pl.BlockSpec((tile, XA_WIDTH), lambda b, j: (b * nj + j, 0)),
        out_shape=jax.ShapeDtypeStruct((T, XA_WIDTH), BF16),
        name="cross_attention",
    )(proj, kv, kv)


ROUTER_ROWS = 128
EXP_ROW0 = 8


def _route(logits_t):
    gl = logits_t[0:8]
    gmax = jnp.max(gl, axis=0, keepdims=True)
    p_grp = 1.0 / jnp.sum(jnp.exp(gl - gmax), axis=0, keepdims=True)
    idx8 = lax.broadcasted_iota(jnp.int32, gl.shape, 0)
    gsel = jnp.min(jnp.where(gl == gmax, idx8, 8), axis=0, keepdims=True)
    cl = jnp.zeros_like(gl)
    for g in range(N_GROUPS):
        r0 = EXP_ROW0 + g * EXPERTS_PER_GROUP
        cl = cl + jnp.where(gsel == g, logits_t[r0:r0 + EXPERTS_PER_GROUP], 0.0)
    v1 = jnp.max(cl, axis=0, keepdims=True)
    i1 = jnp.min(jnp.where(cl == v1, idx8, 8), axis=0, keepdims=True)
    cl2 = jnp.where(idx8 == i1, -jnp.inf, cl)
    v2 = jnp.max(cl2, axis=0, keepdims=True)
    i2 = jnp.min(jnp.where(cl2 == v2, idx8, 8), axis=0, keepdims=True)
    e21 = jnp.exp(v2 - v1)
    w1 = p_grp / (1.0 + e21)
    w2 = p_grp * e21 / (1.0 + e21)
    return gsel * EXPERTS_PER_GROUP + i1, gsel * EXPERTS_PER_GROUP + i2, w1, w2


def _pack_halves(v):
    half = v.shape[1] // 2
    lo = lax.bitcast_convert_type(v[:, :half].astype(BF16).astype(F32), U32)
    hi = lax.bitcast_convert_type(v[:, half:].astype(BF16).astype(F32), U32)
    return lax.bitcast_convert_type(lax.shift_right_logical(lo, U32(16)) | hi, I32)


def _unpack_halves(w):
    u = lax.bitcast_convert_type(w, U32)
    lo = lax.bitcast_convert_type(lax.shift_left(u, U32(16)), F32)
    hi = lax.bitcast_convert_type(u & U32(0xFFFF0000), F32)
    return lo, hi


def _merge_kernel(yp_ref, yr_ref, ya_ref, gl_ref, x_ref, wp_ref, wr_ref, wa_ref, wo_ref, lnw_ref, lnb_ref,
                  rw_ref, rb_ref, x1_ref, xp_ref, eid_ref, wtok_ref):
    merged = _sigmoid(gl_ref[:, 0:D_MODEL].astype(F32)) * _dot(yp_ref[...], wp_ref[...])
    merged = merged + _sigmoid(gl_ref[:, D_MODEL:2 * D_MODEL].astype(F32)) * _dot(yr_ref[...], wr_ref[...])
    merged = merged + _sigmoid(gl_ref[:, 2 * D_MODEL:3 * D_MODEL].astype(F32)) * _dot(ya_ref[...], wa_ref[...])
    h = ALPHA * x_ref[...] + _dot(merged.astype(BF16), wo_ref[...])
    x1 = _layer_norm(h, lnw_ref[...], lnb_ref[...])
    x1_ref[...] = x1
    xp_ref[...] = _pack_halves(x1)
    logits_t = _dot_nt(rw_ref[...], x1, precision=lax.Precision.HIGHEST) + rb_ref[...]
    e0, e1, w0, w1 = _route(logits_t)
    tile = e0.shape[1]
    eid_ref[...] = jnp.concatenate([e0, e1, jnp.zeros((8 - TOP_K, tile), I32)], axis=0)
    w_t = jnp.concatenate([w0, w1, jnp.zeros((ROUTER_ROWS - TOP_K, tile), F32)], axis=0)
    wtok_ref[...] = jnp.transpose(w_t)


def _merge(y_pool, y_ret, y_xa, proj, xf, wp, wr, wa, wo, lnw, lnb, rw, rb, tile=512):
    T = xf.shape[0]
    full = lambda a: pl.BlockSpec(a.shape, lambda i: (0,) * a.ndim)
    rowblk = lambda w: pl.BlockSpec((tile, w), lambda i: (i, 0))
    gw = 3 * D_MODEL
    return pl.pallas_call(
        _merge_kernel,
        grid=(T // tile,),
        in_specs=[rowblk(POOL_WIDTH), rowblk(D_MODEL), rowblk(XA_WIDTH),
                  pl.BlockSpec((tile, gw), lambda i: (i, COL_GATES // gw)),
                  rowblk(D_MODEL), full(wp), full(wr), full(wa), full(wo), full(lnw), full(lnb),
                  full(rw), full(rb)],
        out_specs=[rowblk(D_MODEL), rowblk(PACK_W), pl.BlockSpec((8, tile), lambda i: (0, i)),
                   rowblk(ROUTER_ROWS)],
        out_shape=[jax.ShapeDtypeStruct((T, D_MODEL), F32), jax.ShapeDtypeStruct((T, PACK_W), I32),
                   jax.ShapeDtypeStruct((8, T), I32), jax.ShapeDtypeStruct((T, ROUTER_ROWS), F32)],
        compiler_params=pltpu.CompilerParams(dimension_semantics=("arbitrary",), vmem_limit_bytes=VMEM_LIMIT),
        name="merge_ln1_router",
    )(y_pool, y_ret, y_xa, proj, xf, wp, wr, wa, wo, lnw, lnb, rw, rb)


META_LANES = 256


def _positions_kernel(eid_ref, tri_ref, low_ref, pos_ref, meta_ref, *, n_tok):
    n_chunks = n_tok // RANK_CHUNK
    erow = lax.broadcasted_iota(I32, (N_EXPERTS, RANK_CHUNK), 0)

    def onehot(c):
        sl = slice(c * RANK_CHUNK, (c + 1) * RANK_CHUNK)
        m0 = eid_ref[0:1, sl] == erow
        m1 = eid_ref[1:2, sl] == erow
        return m0, m1, jnp.where(m0, 1.0, 0.0) + jnp.where(m1, 1.0, 0.0)

    counts = jnp.zeros((N_EXPERTS, 1), F32)
    for c in range(n_chunks):
        counts = counts + jnp.sum(onehot(c)[2], axis=1, keepdims=True)
    ptiles = jnp.floor((counts + (MOE_TM - 1)) * (1.0 / MOE_TM))
    ptiles_b = jnp.broadcast_to(ptiles, (N_EXPERTS, 128)).astype(BF16)
    start = _dot(low_ref[...], ptiles_b)[:, 0:1] * MOE_TM
    end = start + ptiles * MOE_TM

    pos_ref[...] = jnp.zeros_like(pos_ref)
    carry = start - 1.0
    for c in range(n_chunks):
        sl = slice(c * RANK_CHUNK, (c + 1) * RANK_CHUNK)
        m0, m1, oh = onehot(c)
        rank = _dot(oh.astype(BF16), tri_ref[...]) + carry
        pos_ref[0:1, sl] = jnp.sum(jnp.where(m0, rank, 0.0), axis=0, keepdims=True).astype(I32)
        pos_ref[1:2, sl] = jnp.sum(jnp.where(m1, rank, 0.0), axis=0, keepdims=True).astype(I32)
        carry = carry + jnp.sum(oh, axis=1, keepdims=True)

    tile_start = lax.broadcasted_iota(I32, (N_EXPERTS, META_LANES), 1).astype(F32) * MOE_TM
    tile_e = jnp.sum(jnp.where(end <= tile_start, 1.0, 0.0), axis=0, keepdims=True)
    tile_e = jnp.minimum(tile_e, N_EXPERTS - 1.0)
    n_used = jnp.broadcast_to(jnp.sum(ptiles, axis=0, keepdims=True), (1, META_LANES))
    meta_ref[...] = jnp.concatenate([tile_e, n_used, jnp.zeros((6, META_LANES), F32)], axis=0).astype(I32)


def _positions(eid):
    T = eid.shape[1]
    r = np.arange(RANK_CHUNK)
    tri = jnp.asarray(r[:, None] <= r[None, :], BF16)
    e = np.arange(N_EXPERTS)
    low = jnp.asarray(e[None, :] < e[:, None], BF16)
    full = lambda a: pl.BlockSpec(a.shape, lambda i: (0,) * a.ndim)
    return pl.pallas_call(
        functools.partial(_positions_kernel, n_tok=T),
        grid=(1,),
        in_specs=[full(eid), full(tri), full(low)],
        out_specs=[pl.BlockSpec((8, T), lambda i: (0, 0)), pl.BlockSpec((8, META_LANES), lambda i: (0, 0))],
        out_shape=[jax.ShapeDtypeStruct((8, T), I32), jax.ShapeDtypeStruct((8, META_LANES), I32)],
        name="route_positions",
    )(eid, tri, low)


def _sc_workers():
    info = plsc.get_sparse_core_info()
    return info.num_cores, info.num_cores * info.num_subcores


def _sc_scatter_rows(xp, pos2d, n_out):
    T, W = xp.shape
    n_cores, n_workers = _sc_workers()
    cpw = T // SC_CHUNK // n_workers
    mesh = plsc.VectorSubcoreMesh(core_axis_name="c", subcore_axis_name="s")

    @functools.partial(
        pl.kernel, mesh=mesh, out_type=jax.ShapeDtypeStruct((n_out, W), I32),
        scratch_types=[pltpu.VMEM((TOP_K * cpw, SC_CHUNK), I32), pltpu.VMEM((SC_CHUNK, W), I32)],
        name="sc_scatter_rows")
    def k(x_hbm, pos_hbm, out_hbm, idx_v, rows_v):
        wid = lax.axis_index("s") * n_cores + lax.axis_index("c")
        for s in range(TOP_K):
            pltpu.sync_copy(pos_hbm.at[pl.ds(s * (T // SC_CHUNK) + wid * cpw, cpw)],
                            idx_v.at[pl.ds(s * cpw, cpw)])
        for j in range(cpw):
            pltpu.sync_copy(x_hbm.at[pl.ds((wid * cpw + j) * SC_CHUNK, SC_CHUNK)], rows_v)
            for s in range(TOP_K):
                pltpu.sync_copy(rows_v, out_hbm.at[idx_v.at[s * cpw + j]])

    return k(xp, pos2d)


def _sc_gather_rows(y, idx2d):
    W = y.shape[1]
    n = idx2d.shape[0] * SC_CHUNK
    n_cores, n_workers = _sc_workers()
    cpw = n // SC_CHUNK // n_workers
    mesh = plsc.VectorSubcoreMesh(core_axis_name="c", subcore_axis_name="s")

    @functools.partial(
        pl.kernel, mesh=mesh, out_type=jax.ShapeDtypeStruct((n, W), I32),
        scratch_types=[pltpu.VMEM((cpw, SC_CHUNK), I32), pltpu.VMEM((SC_CHUNK, W), I32)],
        name="sc_gather_rows")
    def k(y_hbm, idx_hbm, out_hbm, idx_v, rows_v):
        wid = lax.axis_index("s") * n_cores + lax.axis_index("c")
        pltpu.sync_copy(idx_hbm.at[pl.ds(wid * cpw, cpw)], idx_v)
        for j in range(cpw):
            pltpu.sync_copy(y_hbm.at[idx_v.at[j]], rows_v)
            pltpu.sync_copy(rows_v, out_hbm.at[pl.ds((wid * cpw + j) * SC_CHUNK, SC_CHUNK)])

    return k(y, idx2d)


def _routed_kernel(te_ref, nu_ref, xs_ref, wg_ref, wu_ref, wd_ref, ys_ref):
    used = pl.program_id(0) < nu_ref[0]

    @pl.when(used)
    def _():
        lo, hi = _unpack_halves(xs_ref[...])
        lo = lo.astype(BF16)
        hi = hi.astype(BF16)
        wg = wg_ref[0].astype(BF16)
        wu = wu_ref[0].astype(BF16)
        a = _dot(lo, wg[:PACK_W]) + _dot(hi, wg[PACK_W:])
        b = _dot(lo, wu[:PACK_W]) + _dot(hi, wu[PACK_W:])
        act = (a * _sigmoid(a) * b).astype(BF16)
        ys_ref[...] = _pack_halves(_dot(act, wd_ref[0].astype(BF16)))

    @pl.when(jnp.logical_not(used))
    def _():
        ys_ref[...] = jnp.zeros_like(ys_ref)


def _routed_mlp(tile_e, n_used, xs, wg, wu, wd):
    R = xs.shape[0]
    rows = pl.BlockSpec((MOE_TM, PACK_W), lambda i, te, nu: (i, 0))
    return pl.pallas_call(
        _routed_kernel,
        grid_spec=pltpu.PrefetchScalarGridSpec(
            num_scalar_prefetch=2,
            grid=(R // MOE_TM,),
            in_specs=[rows,
                      pl.BlockSpec((1, D_MODEL, D_EXPERT), lambda i, te, nu: (te[i], 0, 0)),
                      pl.BlockSpec((1, D_MODEL, D_EXPERT), lambda i, te, nu: (te[i], 0, 0)),
                      pl.BlockSpec((1, D_EXPERT, D_MODEL), lambda i, te, nu: (te[i], 0, 0))],
            out_specs=rows),
        out_shape=jax.ShapeDtypeStruct((R, PACK_W), I32),
        compiler_params=pltpu.CompilerParams(dimension_semantics=("arbitrary",)),
        name="routed_mlp",
    )(tile_e, n_used, xs, wg, wu, wd)


def _combine_kernel(x1_ref, y0_ref, y1_ref, wtok_ref, lnw_ref, lnb_ref, o_ref):
    w0 = wtok_ref[:, 0:1]
    w1 = wtok_ref[:, 1:2]
    y0lo, y0hi = _unpack_halves(y0_ref[...])
    y1lo, y1hi = _unpack_halves(y1_ref[...])
    moe = jnp.concatenate([w0 * y0lo + w1 * y1lo, w0 * y0hi + w1 * y1hi], axis=1)
    o_ref[...] = _layer_norm(ALPHA * x1_ref[...] + moe, lnw_ref[...], lnb_ref[...])


def _combine_ln2(x1, yg, wtok, lnw, lnb, tile=512):
    T = x1.shape[0]
    nt = T // tile
    full = lambda a: pl.BlockSpec(a.shape, lambda i: (0,) * a.ndim)
    return pl.pallas_call(
        _combine_kernel,
        grid=(nt,),
        in_specs=[pl.BlockSpec((tile, D_MODEL), lambda i: (i, 0)),
                  pl.BlockSpec((tile, PACK_W), lambda i: (i, 0)),
                  pl.BlockSpec((tile, PACK_W), lambda i: (i + nt, 0)),
                  pl.BlockSpec((tile, ROUTER_ROWS), lambda i: (i, 0)),
                  full(lnw), full(lnb)],
        out_specs=pl.BlockSpec((tile, D_MODEL), lambda i: (i, 0)),
        out_shape=jax.ShapeDtypeStruct((T, D_MODEL), F32),
        name="combine_ln2",
    )(x1, yg, yg, wtok, lnw, lnb)


def _reorder_in_cols(w):
    pool, q, k, v, g, xaq, gates = jnp.split(w, [512, 1024, 1536, 2560, 3584, 4096], axis=-1)
    return jnp.concatenate([gates, v, g, pool, q, k, xaq], axis=-1)


def _router_params(w_grp, b_grp, w_exp, b_exp):
    rw = jnp.zeros((ROUTER_ROWS, D_MODEL), F32)
    rw = rw.at[0:N_GROUPS].set(w_grp.T).at[EXP_ROW0:EXP_ROW0 + N_EXPERTS].set(w_exp.T)
    rb = jnp.zeros((ROUTER_ROWS,), F32).at[N_GROUPS:8].set(NEG_BIG)
    rb = rb.at[0:N_GROUPS].set(b_grp).at[EXP_ROW0:EXP_ROW0 + N_EXPERTS].set(b_exp)
    return rw, rb[:, None]


def kernel(x, mem, positions, w_in, w_pool_grp, pool_scale, ret_gn_w, w_mem_kv, w_br_pool, w_br_ret, w_br_xa,
           w_out, ln1_w, ln1_b, w_grp_router, b_grp_router, w_exp_router, b_exp_router, w_exp_gate, w_exp_up,
           w_exp_down, ln2_w, ln2_b):
    B, S, D = x.shape
    assert D == D_MODEL and w_in.shape[0] == DEPTH and S % 512 == 0
    T = B * S
    M = mem.shape[1]
    l = 0
    xf = x.reshape(T, D)

    cos, sin = _rope_table(positions.reshape(T, 1))
    proj = _in_proj(xf, _reorder_in_cols(w_in[l]).astype(BF16))
    y_pool = _pool(proj, w_pool_grp[l].astype(BF16), pool_scale[l][None, :], S)
    y_ret = _retention(proj, cos, sin, ret_gn_w[l].reshape(1, -1), B, S)
    kv = _mem_kv(mem.reshape(B * M, D), w_mem_kv[l].astype(BF16))
    y_xa = _cross_attention(proj, kv, B, S, M)
    rw, rb = _router_params(w_grp_router[l], b_grp_router[l], w_exp_router[l], b_exp_router[l])
    x1, xp, eid, wtok = _merge(y_pool, y_ret, y_xa, proj, xf, w_br_pool[l].astype(BF16),
                               w_br_ret[l].astype(BF16), w_br_xa[l].astype(BF16), w_out[l].astype(BF16),
                               ln1_w[l][None, :], ln1_b[l][None, :], rw, rb)

    pos, meta = _positions(eid)
    pos2d = pos[0:TOP_K].reshape(TOP_K * T // SC_CHUNK, SC_CHUNK)
    n_tiles = -(-((TOP_K * T + N_EXPERTS * (MOE_TM - 1)) // MOE_TM) // 8) * 8
    assert n_tiles <= META_LANES
    xs = _sc_scatter_rows(xp, pos2d, n_tiles * MOE_TM)
    ys = _routed_mlp(meta[0, :n_tiles], meta[1, :1], xs,
                     w_exp_gate[l].reshape(N_EXPERTS, D_MODEL, D_EXPERT),
                     w_exp_up[l].reshape(N_EXPERTS, D_MODEL, D_EXPERT),
                     w_exp_down[l].reshape(N_EXPERTS, D_EXPERT, D_MODEL))
    yg = _sc_gather_rows(ys, pos2d)
    out = _combine_ln2(x1, yg, wtok, ln2_w[l][None, :], ln2_b[l][None, :])
    return out.reshape(B, S, D)
```

```python
import functools

import numpy as np
import jax
import jax.numpy as jnp
from jax import lax
from jax.experimental import pallas as pl
from jax.experimental.pallas import tpu as pltpu
from jax.experimental.pallas import tpu_sc as plsc

F32 = jnp.float32
BF16 = jnp.bfloat16
I32 = jnp.int32
U32 = jnp.uint32

D_MODEL = 1024
POOL_WINDOWS = (2, 4, 8, 16)
POOL_GROUP_DIM = 128
POOL_WIDTH = 512
POOL_HALO = 16
RET_HEADS = 4
RET_QK_DIM = 128
RET_V_DIM = 256
RET_CHUNK = 128
ROPE_BASE = 10000.0
XA_HEADS = 4
XA_HEAD_DIM = 128
XA_WIDTH = 512
N_GROUPS = 4
EXPERTS_PER_GROUP = 8
N_EXPERTS = N_GROUPS * EXPERTS_PER_GROUP
D_EXPERT = 256
LN_EPS = 1e-5
DEPTH = 1
ALPHA = (2.0 * DEPTH) ** 0.25
NEG_BIG = -1e30

COL_GATES, COL_V, COL_G, COL_POOL, COL_Q, COL_K, COL_XAQ = 0, 3072, 4096, 5120, 5632, 6144, 6656
IN_TOTAL = 7168

VMEM_LIMIT = 56 * 1024 * 1024

TOP_K = 2
PACK_W = D_MODEL // 2
MOE_TM = 512
SC_CHUNK = 128
RANK_CHUNK = 512


def _dot(a, b):
    return jnp.dot(a, b, preferred_element_type=F32)


def _dot_nt(a, b, precision=None):
    return lax.dot_general(a, b, (((1,), (1,)), ((), ())), preferred_element_type=F32, precision=precision)


def _sigmoid(z):
    return 1.0 / (1.0 + jnp.exp(-z))


def _layer_norm(h, w, b):
    mu = jnp.mean(h, axis=-1, keepdims=True)
    hc = h - mu
    var = jnp.mean(hc * hc, axis=-1, keepdims=True)
    return hc * lax.rsqrt(var + LN_EPS) * w + b


def _rope_kernel(pos_ref, freq_ref, sign_ref, cos_ref, sin_ref):
    ang = pos_ref[...].astype(F32) * freq_ref[...]
    cos_ref[...] = jnp.cos(ang)
    sin_ref[...] = jnp.sin(ang) * sign_ref[...]


def _rope_table(pos_col, tile=2048):
    T = pos_col.shape[0]
    half = RET_QK_DIM // 2
    inv_freq = (ROPE_BASE ** (-np.arange(half, dtype=np.float64) / half)).astype(np.float32)
    freq = jnp.asarray(np.concatenate([inv_freq, inv_freq])[None, :])
    sign = jnp.asarray(np.concatenate([-np.ones(half, np.float32), np.ones(half, np.float32)])[None, :])
    row = pl.BlockSpec((1, RET_QK_DIM), lambda i: (0, 0))
    out = pl.BlockSpec((tile, RET_QK_DIM), lambda i: (i, 0))
    return pl.pallas_call(
        _rope_kernel,
        grid=(T // tile,),
        in_specs=[pl.BlockSpec((tile, 1), lambda i: (i, 0)), row, row],
        out_specs=[out, out],
        out_shape=[jax.ShapeDtypeStruct((T, RET_QK_DIM), F32)] * 2,
        name="rope_table",
    )(pos_col, freq, sign)


POOL_SUB = 256


def _pool_bands():
    r = np.arange(POOL_SUB)[:, None]
    c = np.arange(POOL_SUB)[None, :]
    ch = np.arange(POOL_HALO)[None, :] - POOL_HALO
    main = np.stack([((r - c >= 0) & (r - c < w)) for w in POOL_WINDOWS]).astype(np.float32)
    halo = np.stack([((r - ch >= 0) & (r - ch < w)) for w in POOL_WINDOWS]).astype(np.float32)
    return jnp.asarray(main, BF16), jnp.asarray(halo, BF16)


def _pool_branch(ub, j, halo_ref, bmain_ref, bhalo_ref, wg_ref, scale_ref, o_ref):
    tile = ub.shape[0]
    s0 = j * tile
    slot = lax.rem(j, 2)
    for sb in range(tile // POOL_SUB):
        r0 = sb * POOL_SUB
        u = ub[r0:r0 + POOL_SUB]
        prev = halo_ref[slot] if sb == 0 else ub[r0 - POOL_HALO:r0]
        pos = s0 + r0 + lax.broadcasted_iota(I32, (POOL_SUB, POOL_GROUP_DIM), 0)
        for g, w in enumerate(POOL_WINDOWS):
            cols = slice(g * POOL_GROUP_DIM, (g + 1) * POOL_GROUP_DIM)
            ug = u[:, cols]
            wsum = _dot(bmain_ref[g], ug) + _dot(bhalo_ref[g], prev[:, cols])
            cnt = jnp.minimum(pos + 1, w).astype(F32)
            pooled = wsum / cnt - ug.astype(F32)
            mixed = _dot(pooled.astype(BF16), wg_ref[g]) * scale_ref[:, cols]
            o_ref[r0:r0 + POOL_SUB, cols] = mixed.astype(BF16)
    halo_ref[1 - slot] = ub[tile - POOL_HALO:tile]


def _ret_consts():
    h = np.arange(RET_HEADS, dtype=np.float64)
    log_gamma = np.log1p(-np.exp2(-5.0 - h))
    pos = np.arange(RET_CHUNK, dtype=np.float64)
    diff = pos[:, None] - pos[None, :]
    kscale = RET_QK_DIM ** -0.5
    dmask = kscale * np.where(diff >= 0, np.exp(log_gamma[:, None, None] * np.maximum(diff, 0.0)), 0.0)
    qdec = np.exp(log_gamma[:, None] * (pos + 1.0)[None, :])
    kdec = kscale * np.exp(log_gamma[:, None] * (RET_CHUNK - 1.0 - pos)[None, :])
    cdec = np.exp(log_gamma * RET_CHUNK)
    lanes = lambda a: np.broadcast_to(a[:, :, None], (RET_HEADS, RET_CHUNK, RET_QK_DIM))
    return (jnp.asarray(dmask, F32), jnp.asarray(lanes(qdec), F32), jnp.asarray(lanes(kdec), F32),
            tuple(float(v) for v in cdec))


def _retention_branch(q, k, v, silu_g, cos_ref, sin_ref, dmask_ref, qdec_ref, kdec_ref, gnw_ref, state_ref,
                      o_ref, cdec):
    tile = q.shape[0]
    half = RET_QK_DIM // 2
    for c in range(tile // RET_CHUNK):
        rows = slice(c * RET_CHUNK, (c + 1) * RET_CHUNK)
        cos = cos_ref[rows, :]
        sin = sin_ref[rows, :]
        for h in range(RET_HEADS):
            qk_cols = slice(h * RET_QK_DIM, (h + 1) * RET_QK_DIM)
            v_cols = slice(h * RET_V_DIM, (h + 1) * RET_V_DIM)
            qh = q[rows, qk_cols]
            kh = k[rows, qk_cols]
            qr = qh * cos + pltpu.roll(qh, half, 1) * sin
            kr = kh * cos + pltpu.roll(kh, half, 1) * sin
            vh = v[rows, v_cols]
            scores = _dot_nt(qr.astype(BF16), kr.astype(BF16)) * dmask_ref[h]
            intra = _dot(scores.astype(BF16), vh)
            st = state_ref[h]
            cross = _dot((qr * qdec_ref[h]).astype(BF16), st.astype(BF16))
            kd_t = jnp.transpose(kr * kdec_ref[h]).astype(BF16)
            state_ref[h] = cdec[h] * st + _dot(kd_t, vh)
            y = intra + cross
            mu = jnp.mean(y, axis=-1, keepdims=True)
            yc = y - mu
            var = jnp.mean(yc * yc, axis=-1, keepdims=True)
            yn = yc * lax.rsqrt(var + LN_EPS) * gnw_ref[:, v_cols]
            o_ref[rows, v_cols] = (silu_g[h][rows] * yn).astype(BF16)


def _memkv_kernel(m_ref, w_ref, o_ref):
    o_ref[...] = _dot(m_ref[...].astype(BF16), w_ref[...]).astype(BF16)


def _mem_kv(memf, w_b):
    M, D = memf.shape
    N = w_b.shape[1]
    return pl.pallas_call(
        _memkv_kernel,
        grid=(1,),
        in_specs=[pl.BlockSpec((M, D), lambda i: (0, 0)), pl.BlockSpec((D, N), lambda i: (0, 0))],
        out_specs=pl.BlockSpec((M, N), lambda i: (0, 0)),
        out_shape=jax.ShapeDtypeStruct((M, N), BF16),
        name="mem_kv",
    )(memf, w_b)


def _cross_attention_branch(xq, k_ref, v_ref, o_ref):
    scale = XA_HEAD_DIM ** -0.5
    for h in range(XA_HEADS):
        cols = slice(h * XA_HEAD_DIM, (h + 1) * XA_HEAD_DIM)
        s = _dot_nt(xq[:, cols], k_ref[:, cols]) * scale
        m = jnp.max(s, axis=-1, keepdims=True)
        p = jnp.exp(s - m)
        l = jnp.sum(p, axis=-1, keepdims=True)
        o = _dot(p.astype(BF16), v_ref[:, cols]) / l
        o_ref[:, cols] = o.astype(BF16)


ROUTER_ROWS = 128
EXP_ROW0 = 8


def _route(logits_t):
    gl = logits_t[0:8]
    gmax = jnp.max(gl, axis=0, keepdims=True)
    p_grp = 1.0 / jnp.sum(jnp.exp(gl - gmax), axis=0, keepdims=True)
    idx8 = lax.broadcasted_iota(jnp.int32, gl.shape, 0)
    gsel = jnp.min(jnp.where(gl == gmax, idx8, 8), axis=0, keepdims=True)
    cl = jnp.zeros_like(gl)
    for g in range(N_GROUPS):
        r0 = EXP_ROW0 + g * EXPERTS_PER_GROUP
        cl = cl + jnp.where(gsel == g, logits_t[r0:r0 + EXPERTS_PER_GROUP], 0.0)
    v1 = jnp.max(cl, axis=0, keepdims=True)
    i1 = jnp.min(jnp.where(cl == v1, idx8, 8), axis=0, keepdims=True)
    cl2 = jnp.where(idx8 == i1, -jnp.inf, cl)
    v2 = jnp.max(cl2, axis=0, keepdims=True)
    i2 = jnp.min(jnp.where(cl2 == v2, idx8, 8), axis=0, keepdims=True)
    e21 = jnp.exp(v2 - v1)
    w1 = p_grp / (1.0 + e21)
    w2 = p_grp * e21 / (1.0 + e21)
    return gsel * EXPERTS_PER_GROUP + i1, gsel * EXPERTS_PER_GROUP + i2, w1, w2


def _pack_halves(v):
    half = v.shape[1] // 2
    lo = lax.bitcast_convert_type(v[:, :half].astype(BF16).astype(F32), U32)
    hi = lax.bitcast_convert_type(v[:, half:].astype(BF16).astype(F32), U32)
    return lax.bitcast_convert_type(lax.shift_right_logical(lo, U32(16)) | hi, I32)


def _unpack_halves(w):
    u = lax.bitcast_convert_type(w, U32)
    lo = lax.bitcast_convert_type(lax.shift_left(u, U32(16)), F32)
    hi = lax.bitcast_convert_type(u & U32(0xFFFF0000), F32)
    return lo, hi


def _mixer_kernel(x_ref, cos_ref, sin_ref, km_ref, vm_ref, win_ref, wgrp_ref, pscale_ref, bmain_ref, bhalo_ref,
                  dmask_ref, qdec_ref, kdec_ref, gnw_ref, wp_ref, wr_ref, wa_ref, wo_ref, lnw_ref, lnb_ref,
                  rw_ref, rb_ref, x1_ref, xp_ref, eid_ref, wtok_ref,
                  state_ref, halo_ref, ypool_ref, yret_ref, yxa_ref, *, tile, cdec):
    j = pl.program_id(1)

    @pl.when(j == 0)
    def _():
        state_ref[...] = jnp.zeros_like(state_ref)
        halo_ref[...] = jnp.zeros_like(halo_ref)

    x = x_ref[...]
    xb = x.astype(BF16)

    def proj(col, width):
        return _dot(xb, win_ref[:, col:col + width])

    _pool_branch(proj(COL_POOL, POOL_WIDTH).astype(BF16), j, halo_ref, bmain_ref, bhalo_ref, wgrp_ref,
                 pscale_ref, ypool_ref)
    merged = _sigmoid(proj(COL_GATES, D_MODEL)) * _dot(ypool_ref[...], wp_ref[...])

    silu_g = []
    for h in range(RET_HEADS):
        gh = proj(COL_G + h * RET_V_DIM, RET_V_DIM)
        silu_g.append(gh * _sigmoid(gh))
    _retention_branch(proj(COL_Q, RET_HEADS * RET_QK_DIM), proj(COL_K, RET_HEADS * RET_QK_DIM),
                      proj(COL_V, RET_HEADS * RET_V_DIM).astype(BF16), silu_g, cos_ref, sin_ref, dmask_ref,
                      qdec_ref, kdec_ref, gnw_ref, state_ref, yret_ref, cdec)
    merged = merged + _sigmoid(proj(COL_GATES + D_MODEL, D_MODEL)) * _dot(yret_ref[...], wr_ref[...])

    _cross_attention_branch(proj(COL_XAQ, XA_WIDTH).astype(BF16), km_ref, vm_ref, yxa_ref)
    merged = merged + _sigmoid(proj(COL_GATES + 2 * D_MODEL, D_MODEL)) * _dot(yxa_ref[...], wa_ref[...])

    h = ALPHA * x + _dot(merged.astype(BF16), wo_ref[...])
    x1 = _layer_norm(h, lnw_ref[...], lnb_ref[...])
    x1_ref[...] = x1
    xp_ref[...] = _pack_halves(x1)
    logits_t = _dot_nt(rw_ref[...], x1, precision=lax.Precision.HIGHEST) + rb_ref[...]
    e0, e1, w0, w1 = _route(logits_t)
    tile = e0.shape[1]
    eid_ref[...] = jnp.concatenate([e0, e1, jnp.zeros((8 - TOP_K, tile), I32)], axis=0)
    w_t = jnp.concatenate([w0, w1, jnp.zeros((ROUTER_ROWS - TOP_K, tile), F32)], axis=0)
    wtok_ref[...] = jnp.transpose(w_t)


def _mixer(xf, cos, sin, kv, win, wgrp, pscale, gnw, wp, wr, wa, wo, lnw, lnb, rw, rb, batch, seq, mem_len,
           tile=512):
    T = xf.shape[0]
    nj = seq // tile
    bmain, bhalo = _pool_bands()
    dmask, qdec, kdec, cdec = _ret_consts()
    resident = lambda a: pl.BlockSpec(a.shape, lambda b, j: (0,) * a.ndim, pipeline_mode=pl.Buffered(1))
    rowblk = lambda w: pl.BlockSpec((tile, w), lambda b, j: (b * nj + j, 0))
    consts = (win, wgrp, pscale, bmain, bhalo, dmask, qdec, kdec, gnw, wp, wr, wa, wo, lnw, lnb, rw, rb)
    return pl.pallas_call(
        functools.partial(_mixer_kernel, tile=tile, cdec=cdec),
        grid=(batch, nj),
        in_specs=[rowblk(D_MODEL), rowblk(RET_QK_DIM), rowblk(RET_QK_DIM),
                  pl.BlockSpec((mem_len, XA_WIDTH), lambda b, j: (b, 0)),
                  pl.BlockSpec((mem_len, XA_WIDTH), lambda b, j: (b, 1))] + [resident(a) for a in consts],
        out_specs=[rowblk(D_MODEL), rowblk(PACK_W), pl.BlockSpec((8, tile), lambda b, j: (0, b * nj + j)),
                   rowblk(ROUTER_ROWS)],
        out_shape=[jax.ShapeDtypeStruct((T, D_MODEL), F32), jax.ShapeDtypeStruct((T, PACK_W), I32),
                   jax.ShapeDtypeStruct((8, T), I32), jax.ShapeDtypeStruct((T, ROUTER_ROWS), F32)],
        scratch_shapes=[pltpu.VMEM((RET_HEADS, RET_QK_DIM, RET_V_DIM), F32),
                        pltpu.VMEM((2, POOL_HALO, POOL_WIDTH), BF16),
                        pltpu.VMEM((tile, POOL_WIDTH), BF16),
                        pltpu.VMEM((tile, RET_HEADS * RET_V_DIM), BF16),
                        pltpu.VMEM((tile, XA_WIDTH), BF16)],
        compiler_params=pltpu.CompilerParams(dimension_semantics=("arbitrary", "arbitrary"),
                                             vmem_limit_bytes=VMEM_LIMIT),
        name="mixer",
    )(xf, cos, sin, kv, kv, *consts)


META_LANES = 256


def _positions_kernel(eid_ref, tri_ref, low_ref, pos_ref, meta_ref, *, n_tok):
    n_chunks = n_tok // RANK_CHUNK
    erow = lax.broadcasted_iota(I32, (N_EXPERTS, RANK_CHUNK), 0)

    def onehot(c):
        sl = slice(c * RANK_CHUNK, (c + 1) * RANK_CHUNK)
        m0 = eid_ref[0:1, sl] == erow
        m1 = eid_ref[1:2, sl] == erow
        return m0, m1, jnp.where(m0, 1.0, 0.0) + jnp.where(m1, 1.0, 0.0)

    counts = jnp.zeros((N_EXPERTS, 1), F32)
    for c in range(n_chunks):
        counts = counts + jnp.sum(onehot(c)[2], axis=1, keepdims=True)
    ptiles = jnp.floor((counts + (MOE_TM - 1)) * (1.0 / MOE_TM))
    ptiles_b = jnp.broadcast_to(ptiles, (N_EXPERTS, 128)).astype(BF16)
    start = _dot(low_ref[...], ptiles_b)[:, 0:1] * MOE_TM
    end = start + ptiles * MOE_TM

    pos_ref[...] = jnp.zeros_like(pos_ref)
    carry = start - 1.0
    for c in range(n_chunks):
        sl = slice(c * RANK_CHUNK, (c + 1) * RANK_CHUNK)
        m0, m1, oh = onehot(c)
        rank = _dot(oh.astype(BF16), tri_ref[...]) + carry
        pos_ref[0:1, sl] = jnp.sum(jnp.where(m0, rank, 0.0), axis=0, keepdims=True).astype(I32)
        pos_ref[1:2, sl] = jnp.sum(jnp.where(m1, rank, 0.0), axis=0, keepdims=True).astype(I32)
        carry = carry + jnp.sum(oh, axis=1, keepdims=True)

    tile_start = lax.broadcasted_iota(I32, (N_EXPERTS, META_LANES), 1).astype(F32) * MOE_TM
    tile_e = jnp.sum(jnp.where(end <= tile_start, 1.0, 0.0), axis=0, keepdims=True)
    tile_e = jnp.minimum(tile_e, N_EXPERTS - 1.0)
    n_used = jnp.broadcast_to(jnp.sum(ptiles, axis=0, keepdims=True), (1, META_LANES))
    meta_ref[...] = jnp.concatenate([tile_e, n_used, jnp.zeros((6, META_LANES), F32)], axis=0).astype(I32)


def _positions(eid):
    T = eid.shape[1]
    r = np.arange(RANK_CHUNK)
    tri = jnp.asarray(r[:, None] <= r[None, :], BF16)
    e = np.arange(N_EXPERTS)
    low = jnp.asarray(e[None, :] < e[:, None], BF16)
    full = lambda a: pl.BlockSpec(a.shape, lambda i: (0,) * a.ndim)
    return pl.pallas_call(
        functools.partial(_positions_kernel, n_tok=T),
        grid=(1,),
        in_specs=[full(eid), full(tri), full(low)],
        out_specs=[pl.BlockSpec((8, T), lambda i: (0, 0)), pl.BlockSpec((8, META_LANES), lambda i: (0, 0))],
        out_shape=[jax.ShapeDtypeStruct((8, T), I32), jax.ShapeDtypeStruct((8, META_LANES), I32)],
        name="route_positions",
    )(eid, tri, low)


def _sc_workers():
    info = plsc.get_sparse_core_info()
    return info.num_cores, info.num_cores * info.num_subcores


def _sc_scatter_rows(xp, pos2d, n_out):
    T, W = xp.shape
    n_cores, n_workers = _sc_workers()
    cpw = T // SC_CHUNK // n_workers
    mesh = plsc.VectorSubcoreMesh(core_axis_name="c", subcore_axis_name="s")

    @functools.partial(
        pl.kernel, mesh=mesh, out_type=jax.ShapeDtypeStruct((n_out, W), I32),
        scratch_types=[pltpu.VMEM((TOP_K * cpw, SC_CHUNK), I32), pltpu.VMEM((SC_CHUNK, W), I32)],
        name="sc_scatter_rows")
    def k(x_hbm, pos_hbm, out_hbm, idx_v, rows_v):
        wid = lax.axis_index("s") * n_cores + lax.axis_index("c")
        for s in range(TOP_K):
            pltpu.sync_copy(pos_hbm.at[pl.ds(s * (T // SC_CHUNK) + wid * cpw, cpw)],
                            idx_v.at[pl.ds(s * cpw, cpw)])
        for j in range(cpw):
            pltpu.sync_copy(x_hbm.at[pl.ds((wid * cpw + j) * SC_CHUNK, SC_CHUNK)], rows_v)
            for s in range(TOP_K):
                pltpu.sync_copy(rows_v, out_hbm.at[idx_v.at[s * cpw + j]])

    return k(xp, pos2d)


def _sc_gather_rows(y, idx2d):
    W = y.shape[1]
    n = idx2d.shape[0] * SC_CHUNK
    n_cores, n_workers = _sc_workers()
    cpw = n // SC_CHUNK // n_workers
    mesh = plsc.VectorSubcoreMesh(core_axis_name="c", subcore_axis_name="s")

    @functools.partial(
        pl.kernel, mesh=mesh, out_type=jax.ShapeDtypeStruct((n, W), I32),
        scratch_types=[pltpu.VMEM((cpw, SC_CHUNK), I32), pltpu.VMEM((SC_CHUNK, W), I32)],
        name="sc_gather_rows")
    def k(y_hbm, idx_hbm, out_hbm, idx_v, rows_v):
        wid = lax.axis_index("s") * n_cores + lax.axis_index("c")
        pltpu.sync_copy(idx_hbm.at[pl.ds(wid * cpw, cpw)], idx_v)
        for j in range(cpw):
            pltpu.sync_copy(y_hbm.at[idx_v.at[j]], rows_v)
            pltpu.sync_copy(rows_v, out_hbm.at[pl.ds((wid * cpw + j) * SC_CHUNK, SC_CHUNK)])

    return k(y, idx2d)


def _routed_kernel(te_ref, nu_ref, xs_ref, wg_ref, wu_ref, wd_ref, ys_ref):
    used = pl.program_id(0) < nu_ref[0]

    @pl.when(used)
    def _():
        lo, hi = _unpack_halves(xs_ref[...])
        lo = lo.astype(BF16)
        hi = hi.astype(BF16)
        wg = wg_ref[0].astype(BF16)
        wu = wu_ref[0].astype(BF16)
        a = _dot(lo, wg[:PACK_W]) + _dot(hi, wg[PACK_W:])
        b = _dot(lo, wu[:PACK_W]) + _dot(hi, wu[PACK_W:])
        act = (a * _sigmoid(a) * b).astype(BF16)
        ys_ref[...] = _pack_halves(_dot(act, wd_ref[0].astype(BF16)))

    @pl.when(jnp.logical_not(used))
    def _():
        ys_ref[...] = jnp.zeros_like(ys_ref)


def _routed_mlp(tile_e, n_used, xs, wg, wu, wd):
    R = xs.shape[0]
    rows = pl.BlockSpec((MOE_TM, PACK_W), lambda i, te, nu: (i, 0))
    return pl.pallas_call(
        _routed_kernel,
        grid_spec=pltpu.PrefetchScalarGridSpec(
            num_scalar_prefetch=2,
            grid=(R // MOE_TM,),
            in_specs=[rows,
                      pl.BlockSpec((1, D_MODEL, D_EXPERT), lambda i, te, nu: (te[i], 0, 0)),
                      pl.BlockSpec((1, D_MODEL, D_EXPERT), lambda i, te, nu: (te[i], 0, 0)),
                      pl.BlockSpec((1, D_EXPERT, D_MODEL), lambda i, te, nu: (te[i], 0, 0))],
            out_specs=rows),
        out_shape=jax.ShapeDtypeStruct((R, PACK_W), I32),
        compiler_params=pltpu.CompilerParams(dimension_semantics=("arbitrary",)),
        name="routed_mlp",
    )(tile_e, n_used, xs, wg, wu, wd)


def _combine_kernel(x1_ref, y0_ref, y1_ref, wtok_ref, lnw_ref, lnb_ref, o_ref):
    w0 = wtok_ref[:, 0:1]
    w1 = wtok_ref[:, 1:2]
    y0lo, y0hi = _unpack_halves(y0_ref[...])
    y1lo, y1hi = _unpack_halves(y1_ref[...])
    moe = jnp.concatenate([w0 * y0lo + w1 * y1lo, w0 * y0hi + w1 * y1hi], axis=1)
    o_ref[...] = _layer_norm(ALPHA * x1_ref[...] + moe, lnw_ref[...], lnb_ref[...])


def _combine_ln2(x1, yg, wtok, lnw, lnb, tile=512):
    T = x1.shape[0]
    nt = T // tile
    full = lambda a: pl.BlockSpec(a.shape, lambda i: (0,) * a.ndim)
    return pl.pallas_call(
        _combine_kernel,
        grid=(nt,),
        in_specs=[pl.BlockSpec((tile, D_MODEL), lambda i: (i, 0)),
                  pl.BlockSpec((tile, PACK_W), lambda i: (i, 0)),
                  pl.BlockSpec((tile, PACK_W), lambda i: (i + nt, 0)),
                  pl.BlockSpec((tile, ROUTER_ROWS), lambda i: (i, 0)),
                  full(lnw), full(lnb)],
        out_specs=pl.BlockSpec((tile, D_MODEL), lambda i: (i, 0)),
        out_shape=jax.ShapeDtypeStruct((T, D_MODEL), F32),
        name="combine_ln2",
    )(x1, yg, yg, wtok, lnw, lnb)


def _reorder_in_cols(w):
    pool, q, k, v, g, xaq, gates = jnp.split(w, [512, 1024, 1536, 2560, 3584, 4096], axis=-1)
    return jnp.concatenate([gates, v, g, pool, q, k, xaq], axis=-1)


def _router_params(w_grp, b_grp, w_exp, b_exp):
    rw = jnp.zeros((ROUTER_ROWS, D_MODEL), F32)
    rw = rw.at[0:N_GROUPS].set(w_grp.T).at[EXP_ROW0:EXP_ROW0 + N_EXPERTS].set(w_exp.T)
    rb = jnp.zeros((ROUTER_ROWS,), F32).at[N_GROUPS:8].set(NEG_BIG)
    rb = rb.at[0:N_GROUPS].set(b_grp).at[EXP_ROW0:EXP_ROW0 + N_EXPERTS].set(b_exp)
    return rw, rb[:, None]


def kernel(x, mem, positions, w_in, w_pool_grp, pool_scale, ret_gn_w, w_mem_kv, w_br_pool, w_br_ret, w_br_xa,
           w_out, ln1_w, ln1_b, w_grp_router, b_grp_router, w_exp_router, b_exp_router, w_exp_gate, w_exp_up,
           w_exp_down, ln2_w, ln2_b):
    B, S, D = x.shape
    assert D == D_MODEL and w_in.shape[0] == DEPTH and S % 512 == 0
    T = B * S
    M = mem.shape[1]
    l = 0
    xf = x.reshape(T, D)

    cos, sin = _rope_table(positions.reshape(T, 1))
    kv = _mem_kv(mem.reshape(B * M, D), w_mem_kv[l].astype(BF16))
    rw, rb = _router_params(w_grp_router[l], b_grp_router[l], w_exp_router[l], b_exp_router[l])
    x1, xp, eid, wtok = _mixer(xf, cos, sin, kv, _reorder_in_cols(w_in[l]).astype(BF16),
                               w_pool_grp[l].astype(BF16), pool_scale[l][None, :], ret_gn_w[l].reshape(1, -1),
                               w_br_pool[l].astype(BF16), w_br_ret[l].astype(BF16), w_br_xa[l].astype(BF16),
                               w_out[l].astype(BF16), ln1_w[l][None, :], ln1_b[l][None, :], rw, rb, B, S, M)

    pos, meta = _positions(eid)
    pos2d = pos[0:TOP_K].reshape(TOP_K * T // SC_CHUNK, SC_CHUNK)
    n_tiles = -(-((TOP_K * T + N_EXPERTS * (MOE_TM - 1)) // MOE_TM) // 8) * 8
    assert n_tiles <= META_LANES
    xs = _sc_scatter_rows(xp, pos2d, n_tiles * MOE_TM)
    ys = _routed_mlp(meta[0, :n_tiles], meta[1, :1], xs,
                     w_exp_gate[l].reshape(N_EXPERTS, D_MODEL, D_EXPERT),
                     w_exp_up[l].reshape(N_EXPERTS, D_MODEL, D_EXPERT),
                     w_exp_down[l].reshape(N_EXPERTS, D_EXPERT, D_MODEL))
    yg = _sc_gather_rows(ys, pos2d)
    out = _combine_ln2(x1, yg, wtok, ln2_w[l][None, :], ln2_b[l][None, :])
    return out.reshape(B, S, D)
```

```python
import functools

import numpy as np
import jax
import jax.numpy as jnp
from jax import lax
from jax.experimental import pallas as pl
from jax.experimental.pallas import tpu as pltpu
from jax.experimental.pallas import tpu_sc as plsc

F32 = jnp.float32
BF16 = jnp.bfloat16
I32 = jnp.int32
U32 = jnp.uint32

D_MODEL = 1024
POOL_WINDOWS = (2, 4, 8, 16)
POOL_GROUP_DIM = 128
POOL_WIDTH = 512
POOL_HALO = 16
RET_HEADS = 4
RET_QK_DIM = 128
RET_V_DIM = 256
RET_CHUNK = 128
ROPE_BASE = 10000.0
XA_HEADS = 4
XA_HEAD_DIM = 128
XA_WIDTH = 512
N_GROUPS = 4
EXPERTS_PER_GROUP = 8
N_EXPERTS = N_GROUPS * EXPERTS_PER_GROUP
D_EXPERT = 256
LN_EPS = 1e-5
DEPTH = 1
ALPHA = (2.0 * DEPTH) ** 0.25
NEG_BIG = -1e30

COL_GATES, COL_V, COL_G, COL_POOL, COL_Q, COL_XAQ = 0, 3072, 4096, 5120, 5632, 6144

VMEM_LIMIT = 56 * 1024 * 1024

TOP_K = 2
PACK_W = D_MODEL // 2
MOE_TM = 512
SC_CHUNK = 128
RANK_CHUNK = 512


def _dot(a, b):
    return jnp.dot(a, b, preferred_element_type=F32)


def _dot_nt(a, b, precision=None):
    return lax.dot_general(a, b, (((1,), (1,)), ((), ())), preferred_element_type=F32, precision=precision)


def _sigmoid(z):
    return 1.0 / (1.0 + jnp.exp(-z))


def _layer_norm(h, w, b):
    mu = jnp.mean(h, axis=-1, keepdims=True)
    hc = h - mu
    var = jnp.mean(hc * hc, axis=-1, keepdims=True)
    return hc * lax.rsqrt(var + LN_EPS) * w + b


def _rope_kernel(pos_ref, freq_ref, sign_ref, cos_ref, sin_ref, cos_t_ref, sin_t_ref):
    half = RET_QK_DIM // 2
    ang = pos_ref[...].astype(F32) * freq_ref[...]
    cos = jnp.cos(ang)
    sin = jnp.sin(ang) * sign_ref[...]
    cos_ref[...] = cos
    sin_ref[...] = sin
    cos_t_ref[...] = jnp.transpose(cos)[:half]
    sin_t_ref[...] = jnp.transpose(sin)[half:]


def _rope_table(pos_col, tile=2048):
    T = pos_col.shape[0]
    half = RET_QK_DIM // 2
    inv_freq = (ROPE_BASE ** (-np.arange(half, dtype=np.float64) / half)).astype(np.float32)
    freq = jnp.asarray(np.concatenate([inv_freq, inv_freq])[None, :])
    sign = jnp.asarray(np.concatenate([-np.ones(half, np.float32), np.ones(half, np.float32)])[None, :])
    row = pl.BlockSpec((1, RET_QK_DIM), lambda i: (0, 0))
    out = pl.BlockSpec((tile, RET_QK_DIM), lambda i: (i, 0))
    out_t = pl.BlockSpec((half, tile), lambda i: (0, i))
    return pl.pallas_call(
        _rope_kernel,
        grid=(T // tile,),
        in_specs=[pl.BlockSpec((tile, 1), lambda i: (i, 0)), row, row],
        out_specs=[out, out, out_t, out_t],
        out_shape=[jax.ShapeDtypeStruct((T, RET_QK_DIM), F32)] * 2 + [jax.ShapeDtypeStruct((half, T), F32)] * 2,
        name="rope_table",
    )(pos_col, freq, sign)


POOL_SUB = 256


def _pool_bands():
    r = np.arange(POOL_SUB)[:, None]
    c = np.arange(POOL_SUB)[None, :]
    ch = np.arange(POOL_HALO)[None, :] - POOL_HALO
    main = np.stack([((r - c >= 0) & (r - c < w)) for w in POOL_WINDOWS]).astype(np.float32)
    halo = np.stack([((r - ch >= 0) & (r - ch < w)) for w in POOL_WINDOWS]).astype(np.float32)
    return jnp.asarray(main, BF16), jnp.asarray(halo, BF16)


def _pool_branch(ub, j, halo_ref, bmain_ref, bhalo_ref, wg_ref, scale_ref, o_ref):
    tile = ub.shape[0]
    s0 = j * tile
    slot = lax.rem(j, 2)
    for sb in range(tile // POOL_SUB):
        r0 = sb * POOL_SUB
        u = ub[r0:r0 + POOL_SUB]
        prev = halo_ref[slot] if sb == 0 else ub[r0 - POOL_HALO:r0]
        pos = s0 + r0 + lax.broadcasted_iota(I32, (POOL_SUB, POOL_GROUP_DIM), 0)
        for g, w in enumerate(POOL_WINDOWS):
            cols = slice(g * POOL_GROUP_DIM, (g + 1) * POOL_GROUP_DIM)
            ug = u[:, cols]
            wsum = _dot(bmain_ref[g], ug) + _dot(bhalo_ref[g], prev[:, cols])
            cnt = jnp.minimum(pos + 1, w).astype(F32)
            pooled = wsum / cnt - ug.astype(F32)
            mixed = _dot(pooled.astype(BF16), wg_ref[g]) * scale_ref[:, cols]
            o_ref[r0:r0 + POOL_SUB, cols] = mixed.astype(BF16)
    halo_ref[1 - slot] = ub[tile - POOL_HALO:tile]


def _ret_consts():
    h = np.arange(RET_HEADS, dtype=np.float64)
    log_gamma = np.log1p(-np.exp2(-5.0 - h))
    pos = np.arange(RET_CHUNK, dtype=np.float64)
    diff = pos[:, None] - pos[None, :]
    kscale = RET_QK_DIM ** -0.5
    dmask = kscale * np.where(diff >= 0, np.exp(log_gamma[:, None, None] * np.maximum(diff, 0.0)), 0.0)
    qdec = np.exp(log_gamma[:, None] * (pos + 1.0)[None, :])
    kdec = kscale * np.exp(log_gamma[:, None] * (RET_CHUNK - 1.0 - pos)[None, :])
    cdec = np.exp(log_gamma * RET_CHUNK)
    lanes = lambda a: np.broadcast_to(a[:, :, None], (RET_HEADS, RET_CHUNK, RET_QK_DIM))
    kdec_t = np.broadcast_to(kdec[:, None, :], (RET_HEADS, RET_QK_DIM, RET_CHUNK))
    return (jnp.asarray(dmask, F32), jnp.asarray(lanes(qdec), F32), jnp.asarray(kdec_t, F32),
            tuple(float(v) for v in cdec))


def _retention_branch(q, k_t, v, silu_g, cos_ref, sin_ref, cos_t_ref, sin_t_ref, dmask_ref, qdec_ref, kdec_ref,
                      gnw_ref, state_ref, rq_ref, rqd_ref, rkt_ref, rkdt_ref, o_ref, cdec, fillers):
    tile = q.shape[0]
    n_chunks = tile // RET_CHUNK
    half = RET_QK_DIM // 2
    cos = cos_ref[...]
    sin = sin_ref[...]
    cos_t = cos_t_ref[...]
    sin_t = sin_t_ref[...]
    for h in range(RET_HEADS):
        qk = slice(h * RET_QK_DIM, (h + 1) * RET_QK_DIM)
        qh = q[:, qk]
        qr = qh * cos + pltpu.roll(qh, half, 1) * sin
        rq_ref[:, qk] = qr.astype(BF16)
        rqd_ref[:, qk] = (qr * jnp.concatenate([qdec_ref[h]] * n_chunks, axis=0)).astype(BF16)
        k1 = k_t[h * RET_QK_DIM:h * RET_QK_DIM + half]
        k2 = k_t[h * RET_QK_DIM + half:(h + 1) * RET_QK_DIM]
        kr_t = jnp.concatenate([k1 * cos_t - k2 * sin_t, k2 * cos_t + k1 * sin_t], axis=0)
        rkt_ref[qk, :] = kr_t.astype(BF16)
        rkdt_ref[qk, :] = (kr_t * jnp.concatenate([kdec_ref[h]] * n_chunks, axis=1)).astype(BF16)

    for c in range(n_chunks):
        rows = slice(c * RET_CHUNK, (c + 1) * RET_CHUNK)
        heads = [slice(h * RET_QK_DIM, (h + 1) * RET_QK_DIM) for h in range(RET_HEADS)]
        raw = [_dot(rq_ref[rows, qk], rkt_ref[qk, rows]) for qk in heads]
        fillers[c]()
        for h in range(RET_HEADS):
            qk = heads[h]
            v_cols = slice(h * RET_V_DIM, (h + 1) * RET_V_DIM)
            vh = v[rows, v_cols]
            scores = raw[h] * dmask_ref[h]
            st = state_ref[h]
            lhs = jnp.concatenate([scores.astype(BF16), rqd_ref[rows, qk]], axis=1)
            y = _dot(lhs, jnp.concatenate([vh, st.astype(BF16)], axis=0))
            state_ref[h] = cdec[h] * st + _dot(rkdt_ref[qk, rows], vh)
            mu = jnp.mean(y, axis=-1, keepdims=True)
            yc = y - mu
            var = jnp.mean(yc * yc, axis=-1, keepdims=True)
            yn = yc * lax.rsqrt(var + LN_EPS) * gnw_ref[:, v_cols]
            o_ref[rows, v_cols] = (silu_g[h][rows] * yn).astype(BF16)


def _memkv_kernel(m_ref, w_ref, o_ref):
    o_ref[...] = _dot(m_ref[...].astype(BF16), w_ref[...]).astype(BF16)


def _mem_kv(memf, w_b):
    M, D = memf.shape
    N = w_b.shape[1]
    return pl.pallas_call(
        _memkv_kernel,
        grid=(1,),
        in_specs=[pl.BlockSpec((M, D), lambda i: (0, 0)), pl.BlockSpec((D, N), lambda i: (0, 0))],
        out_specs=pl.BlockSpec((M, N), lambda i: (0, 0)),
        out_shape=jax.ShapeDtypeStruct((M, N), BF16),
        name="mem_kv",
    )(memf, w_b)


def _cross_attention_branch(xq, k_ref, v_ref, o_ref):
    scale = XA_HEAD_DIM ** -0.5
    for h in range(XA_HEADS):
        cols = slice(h * XA_HEAD_DIM, (h + 1) * XA_HEAD_DIM)
        s = _dot_nt(xq[:, cols], k_ref[:, cols]) * scale
        m = jnp.max(s, axis=-1, keepdims=True)
        p = jnp.exp(s - m)
        l = jnp.sum(p, axis=-1, keepdims=True)
        o = _dot(p.astype(BF16), v_ref[:, cols]) / l
        o_ref[:, cols] = o.astype(BF16)


ROUTER_ROWS = 40
WTOK_LANES = 128
EXP_ROW0 = 8


def _route(logits_t):
    gl = logits_t[0:8]
    gmax = jnp.max(gl, axis=0, keepdims=True)
    p_grp = 1.0 / jnp.sum(jnp.exp(gl - gmax), axis=0, keepdims=True)
    idx8 = lax.broadcasted_iota(jnp.int32, gl.shape, 0)
    gsel = jnp.min(jnp.where(gl == gmax, idx8, 8), axis=0, keepdims=True)
    cl = jnp.zeros_like(gl)
    for g in range(N_GROUPS):
        r0 = EXP_ROW0 + g * EXPERTS_PER_GROUP
        cl = cl + jnp.where(gsel == g, logits_t[r0:r0 + EXPERTS_PER_GROUP], 0.0)
    v1 = jnp.max(cl, axis=0, keepdims=True)
    i1 = jnp.min(jnp.where(cl == v1, idx8, 8), axis=0, keepdims=True)
    cl2 = jnp.where(idx8 == i1, -jnp.inf, cl)
    v2 = jnp.max(cl2, axis=0, keepdims=True)
    i2 = jnp.min(jnp.where(cl2 == v2, idx8, 8), axis=0, keepdims=True)
    e21 = jnp.exp(v2 - v1)
    w1 = p_grp / (1.0 + e21)
    w2 = p_grp * e21 / (1.0 + e21)
    return gsel * EXPERTS_PER_GROUP + i1, gsel * EXPERTS_PER_GROUP + i2, w1, w2


def _pack_halves(v):
    half = v.shape[1] // 2
    lo = lax.bitcast_convert_type(v[:, :half].astype(BF16).astype(F32), U32)
    hi = lax.bitcast_convert_type(v[:, half:].astype(BF16).astype(F32), U32)
    return lax.bitcast_convert_type(lax.shift_right_logical(lo, U32(16)) | hi, I32)


def _unpack_halves(w):
    u = lax.bitcast_convert_type(w, U32)
    lo = lax.bitcast_convert_type(lax.shift_left(u, U32(16)), F32)
    hi = lax.bitcast_convert_type(u & U32(0xFFFF0000), F32)
    return lo, hi


def _mixer_kernel(x_ref, cos_ref, sin_ref, cos_t_ref, sin_t_ref, km_ref, vm_ref, win_ref, wkt_ref, wgrp_ref,
                  pscale_ref, bmain_ref, bhalo_ref, dmask_ref, qdec_ref, kdec_ref, gnw_ref, wp_ref, wr_ref,
                  wa_ref, wo_ref, lnw_ref, lnb_ref, rwh_ref, rwl_ref, rb_ref, x1_ref, xp_ref, eid_ref, wtok_ref,
                  state_ref, halo_ref, ypool_ref, yret_ref, yxa_ref, rq_ref, rqd_ref, rkt_ref, rkdt_ref,
                  *, tile, cdec):
    j = pl.program_id(1)

    @pl.when(j == 0)
    def _():
        state_ref[...] = jnp.zeros_like(state_ref)
        halo_ref[...] = jnp.zeros_like(halo_ref)

    x = x_ref[...]
    xb = x.astype(BF16)

    def proj(col, width):
        return _dot(xb, win_ref[:, col:col + width])

    part = {}

    def pool_part():
        _pool_branch(proj(COL_POOL, POOL_WIDTH).astype(BF16), j, halo_ref, bmain_ref, bhalo_ref, wgrp_ref,
                     pscale_ref, ypool_ref)
        part["pool"] = _sigmoid(proj(COL_GATES, D_MODEL)) * _dot(ypool_ref[...], wp_ref[...])

    def xa_part():
        _cross_attention_branch(proj(COL_XAQ, XA_WIDTH).astype(BF16), km_ref, vm_ref, yxa_ref)
        part["xa"] = _sigmoid(proj(COL_GATES + 2 * D_MODEL, D_MODEL)) * _dot(yxa_ref[...], wa_ref[...])

    def ret_gate_part():
        part["ret_gate"] = _sigmoid(proj(COL_GATES + D_MODEL, D_MODEL))

    silu_g = []
    for h in range(RET_HEADS):
        gh = proj(COL_G + h * RET_V_DIM, RET_V_DIM)
        silu_g.append(gh * _sigmoid(gh))
    fillers = [pool_part, xa_part, ret_gate_part] + [lambda: None] * (tile // RET_CHUNK - 3)
    _retention_branch(proj(COL_Q, RET_HEADS * RET_QK_DIM), _dot_nt(wkt_ref[...], xb),
                      proj(COL_V, RET_HEADS * RET_V_DIM).astype(BF16), silu_g, cos_ref, sin_ref, cos_t_ref,
                      sin_t_ref, dmask_ref, qdec_ref, kdec_ref, gnw_ref, state_ref, rq_ref, rqd_ref, rkt_ref,
                      rkdt_ref, yret_ref, cdec, fillers)
    merged = part["pool"] + part["ret_gate"] * _dot(yret_ref[...], wr_ref[...]) + part["xa"]

    h = ALPHA * x + _dot(merged.astype(BF16), wo_ref[...])
    x1 = _layer_norm(h, lnw_ref[...], lnb_ref[...])
    x1_ref[...] = x1
    xp_ref[...] = _pack_halves(x1)
    x1_hi = x1.astype(BF16)
    x1_lo = (x1 - x1_hi.astype(F32)).astype(BF16)
    logits_t = (_dot_nt(rwh_ref[...], x1_hi) + _dot_nt(rwh_ref[...], x1_lo) + _dot_nt(rwl_ref[...], x1_hi)
                + rb_ref[...])
    e0, e1, w0, w1 = _route(logits_t)
    tile = e0.shape[1]
    eid_ref[...] = jnp.concatenate([e0, e1, jnp.zeros((8 - TOP_K, tile), I32)], axis=0)
    w_t = jnp.concatenate([w0, w1, jnp.zeros((WTOK_LANES - TOP_K, tile), F32)], axis=0)
    wtok_ref[...] = jnp.transpose(w_t)


def _mixer(xf, rope, kv, win, wkt, wgrp, pscale, gnw, wp, wr, wa, wo, lnw, lnb, rw, rb, batch, seq, mem_len,
           tile=512):
    T = xf.shape[0]
    nj = seq // tile
    cos, sin, cos_t, sin_t = rope
    bmain, bhalo = _pool_bands()
    dmask, qdec, kdec, cdec = _ret_consts()
    resident = lambda a: pl.BlockSpec(a.shape, lambda b, j: (0,) * a.ndim, pipeline_mode=pl.Buffered(1))
    rowblk = lambda w: pl.BlockSpec((tile, w), lambda b, j: (b * nj + j, 0))
    colblk = lambda r: pl.BlockSpec((r, tile), lambda b, j: (0, b * nj + j))
    consts = (win, wkt, wgrp, pscale, bmain, bhalo, dmask, qdec, kdec, gnw, wp, wr, wa, wo, lnw, lnb, *rw, rb)
    return pl.pallas_call(
        functools.partial(_mixer_kernel, tile=tile, cdec=cdec),
        grid=(batch, nj),
        in_specs=[rowblk(D_MODEL), rowblk(RET_QK_DIM), rowblk(RET_QK_DIM),
                  colblk(RET_QK_DIM // 2), colblk(RET_QK_DIM // 2),
                  pl.BlockSpec((mem_len, XA_WIDTH), lambda b, j: (b, 0)),
                  pl.BlockSpec((mem_len, XA_WIDTH), lambda b, j: (b, 1))] + [resident(a) for a in consts],
        out_specs=[rowblk(D_MODEL), rowblk(PACK_W), colblk(8), rowblk(WTOK_LANES)],
        out_shape=[jax.ShapeDtypeStruct((T, D_MODEL), F32), jax.ShapeDtypeStruct((T, PACK_W), I32),
                   jax.ShapeDtypeStruct((8, T), I32), jax.ShapeDtypeStruct((T, WTOK_LANES), F32)],
        scratch_shapes=[pltpu.VMEM((RET_HEADS, RET_QK_DIM, RET_V_DIM), F32),
                        pltpu.VMEM((2, POOL_HALO, POOL_WIDTH), BF16),
                        pltpu.VMEM((tile, POOL_WIDTH), BF16),
                        pltpu.VMEM((tile, RET_HEADS * RET_V_DIM), BF16),
                        pltpu.VMEM((tile, XA_WIDTH), BF16),
                        pltpu.VMEM((tile, RET_HEADS * RET_QK_DIM), BF16),
                        pltpu.VMEM((tile, RET_HEADS * RET_QK_DIM), BF16),
                        pltpu.VMEM((RET_HEADS * RET_QK_DIM, tile), BF16),
                        pltpu.VMEM((RET_HEADS * RET_QK_DIM, tile), BF16)],
        compiler_params=pltpu.CompilerParams(dimension_semantics=("arbitrary", "arbitrary"),
                                             vmem_limit_bytes=VMEM_LIMIT),
        name="mixer",
    )(xf, cos, sin, cos_t, sin_t, kv, kv, *consts)


META_LANES = 256


def _positions_kernel(eid_ref, tri_ref, low_ref, pos_ref, meta_ref, *, n_tok):
    n_chunks = n_tok // RANK_CHUNK
    erow = lax.broadcasted_iota(I32, (N_EXPERTS, RANK_CHUNK), 0)

    def onehot(c):
        sl = slice(c * RANK_CHUNK, (c + 1) * RANK_CHUNK)
        m0 = eid_ref[0:1, sl] == erow
        m1 = eid_ref[1:2, sl] == erow
        return m0, m1, jnp.where(m0, 1.0, 0.0) + jnp.where(m1, 1.0, 0.0)

    counts = jnp.zeros((N_EXPERTS, 1), F32)
    for c in range(n_chunks):
        counts = counts + jnp.sum(onehot(c)[2], axis=1, keepdims=True)
    ptiles = jnp.floor((counts + (MOE_TM - 1)) * (1.0 / MOE_TM))
    ptiles_b = jnp.broadcast_to(ptiles, (N_EXPERTS, 128)).astype(BF16)
    start = _dot(low_ref[...], ptiles_b)[:, 0:1] * MOE_TM
    end = start + ptiles * MOE_TM

    pos_ref[...] = jnp.zeros_like(pos_ref)
    carry = start - 1.0
    for c in range(n_chunks):
        sl = slice(c * RANK_CHUNK, (c + 1) * RANK_CHUNK)
        m0, m1, oh = onehot(c)
        rank = _dot(oh.astype(BF16), tri_ref[...]) + carry
        pos_ref[0:1, sl] = jnp.sum(jnp.where(m0, rank, 0.0), axis=0, keepdims=True).astype(I32)
        pos_ref[1:2, sl] = jnp.sum(jnp.where(m1, rank, 0.0), axis=0, keepdims=True).astype(I32)
        carry = carry + jnp.sum(oh, axis=1, keepdims=True)

    tile_start = lax.broadcasted_iota(I32, (N_EXPERTS, META_LANES), 1).astype(F32) * MOE_TM
    tile_e = jnp.sum(jnp.where(end <= tile_start, 1.0, 0.0), axis=0, keepdims=True)
    tile_e = jnp.minimum(tile_e, N_EXPERTS - 1.0)
    n_used = jnp.broadcast_to(jnp.sum(ptiles, axis=0, keepdims=True), (1, META_LANES))
    meta_ref[...] = jnp.concatenate([tile_e, n_used, jnp.zeros((6, META_LANES), F32)], axis=0).astype(I32)


def _positions(eid):
    T = eid.shape[1]
    r = np.arange(RANK_CHUNK)
    tri = jnp.asarray(r[:, None] <= r[None, :], BF16)
    e = np.arange(N_EXPERTS)
    low = jnp.asarray(e[None, :] < e[:, None], BF16)
    full = lambda a: pl.BlockSpec(a.shape, lambda i: (0,) * a.ndim)
    return pl.pallas_call(
        functools.partial(_positions_kernel, n_tok=T),
        grid=(1,),
        in_specs=[full(eid), full(tri), full(low)],
        out_specs=[pl.BlockSpec((8, T), lambda i: (0, 0)), pl.BlockSpec((8, META_LANES), lambda i: (0, 0))],
        out_shape=[jax.ShapeDtypeStruct((8, T), I32), jax.ShapeDtypeStruct((8, META_LANES), I32)],
        name="route_positions",
    )(eid, tri, low)


def _sc_workers():
    info = plsc.get_sparse_core_info()
    return info.num_cores, info.num_cores * info.num_subcores


def _sc_scatter_rows(xp, pos2d, n_out):
    T, W = xp.shape
    n_cores, n_workers = _sc_workers()
    cpw = T // SC_CHUNK // n_workers
    mesh = plsc.VectorSubcoreMesh(core_axis_name="c", subcore_axis_name="s")

    @functools.partial(
        pl.kernel, mesh=mesh, out_type=jax.ShapeDtypeStruct((n_out, W), I32),
        scratch_types=[pltpu.VMEM((TOP_K * cpw, SC_CHUNK), I32), pltpu.VMEM((SC_CHUNK, W), I32)],
        name="sc_scatter_rows")
    def k(x_hbm, pos_hbm, out_hbm, idx_v, rows_v):
        wid = lax.axis_index("s") * n_cores + lax.axis_index("c")
        for s in range(TOP_K):
            pltpu.sync_copy(pos_hbm.at[pl.ds(s * (T // SC_CHUNK) + wid * cpw, cpw)],
                            idx_v.at[pl.ds(s * cpw, cpw)])
        for j in range(cpw):
            pltpu.sync_copy(x_hbm.at[pl.ds((wid * cpw + j) * SC_CHUNK, SC_CHUNK)], rows_v)
            for s in range(TOP_K):
                pltpu.sync_copy(rows_v, out_hbm.at[idx_v.at[s * cpw + j]])

    return k(xp, pos2d)


def _sc_gather_rows(y, idx2d):
    W = y.shape[1]
    n = idx2d.shape[0] * SC_CHUNK
    n_cores, n_workers = _sc_workers()
    cpw = n // SC_CHUNK // n_workers
    mesh = plsc.VectorSubcoreMesh(core_axis_name="c", subcore_axis_name="s")

    @functools.partial(
        pl.kernel, mesh=mesh, out_type=jax.ShapeDtypeStruct((n, W), I32),
        scratch_types=[pltpu.VMEM((cpw, SC_CHUNK), I32), pltpu.VMEM((SC_CHUNK, W), I32)],
        name="sc_gather_rows")
    def k(y_hbm, idx_hbm, out_hbm, idx_v, rows_v):
        wid = lax.axis_index("s") * n_cores + lax.axis_index("c")
        pltpu.sync_copy(idx_hbm.at[pl.ds(wid * cpw, cpw)], idx_v)
        for j in range(cpw):
            pltpu.sync_copy(y_hbm.at[idx_v.at[j]], rows_v)
            pltpu.sync_copy(rows_v, out_hbm.at[pl.ds((wid * cpw + j) * SC_CHUNK, SC_CHUNK)])

    return k(y, idx2d)


def _routed_kernel(te_ref, nu_ref, xs_ref, wg_ref, wu_ref, wd_ref, ys_ref):
    used = pl.program_id(0) < nu_ref[0]

    @pl.when(used)
    def _():
        lo, hi = _unpack_halves(xs_ref[...])
        lo = lo.astype(BF16)
        hi = hi.astype(BF16)
        wg = wg_ref[0].astype(BF16)
        wu = wu_ref[0].astype(BF16)
        a = _dot(lo, wg[:PACK_W]) + _dot(hi, wg[PACK_W:])
        b = _dot(lo, wu[:PACK_W]) + _dot(hi, wu[PACK_W:])
        act = (a * _sigmoid(a) * b).astype(BF16)
        ys_ref[...] = _pack_halves(_dot(act, wd_ref[0].astype(BF16)))

    @pl.when(jnp.logical_not(used))
    def _():
        ys_ref[...] = jnp.zeros_like(ys_ref)


def _routed_mlp(tile_e, n_used, xs, wg, wu, wd):
    R = xs.shape[0]
    rows = pl.BlockSpec((MOE_TM, PACK_W), lambda i, te, nu: (i, 0))
    return pl.pallas_call(
        _routed_kernel,
        grid_spec=pltpu.PrefetchScalarGridSpec(
            num_scalar_prefetch=2,
            grid=(R // MOE_TM,),
            in_specs=[rows,
                      pl.BlockSpec((1, D_MODEL, D_EXPERT), lambda i, te, nu: (te[i], 0, 0)),
                      pl.BlockSpec((1, D_MODEL, D_EXPERT), lambda i, te, nu: (te[i], 0, 0)),
                      pl.BlockSpec((1, D_EXPERT, D_MODEL), lambda i, te, nu: (te[i], 0, 0))],
            out_specs=rows),
        out_shape=jax.ShapeDtypeStruct((R, PACK_W), I32),
        compiler_params=pltpu.CompilerParams(dimension_semantics=("arbitrary",)),
        name="routed_mlp",
    )(tile_e, n_used, xs, wg, wu, wd)


def _combine_kernel(x1_ref, y0_ref, y1_ref, wtok_ref, lnw_ref, lnb_ref, o_ref):
    w0 = wtok_ref[:, 0:1]
    w1 = wtok_ref[:, 1:2]
    y0lo, y0hi = _unpack_halves(y0_ref[...])
    y1lo, y1hi = _unpack_halves(y1_ref[...])
    moe = jnp.concatenate([w0 * y0lo + w1 * y1lo, w0 * y0hi + w1 * y1hi], axis=1)
    o_ref[...] = _layer_norm(ALPHA * x1_ref[...] + moe, lnw_ref[...], lnb_ref[...])


def _combine_ln2(x1, yg, wtok, lnw, lnb, tile=512):
    T = x1.shape[0]
    nt = T // tile
    full = lambda a: pl.BlockSpec(a.shape, lambda i: (0,) * a.ndim)
    return pl.pallas_call(
        _combine_kernel,
        grid=(nt,),
        in_specs=[pl.BlockSpec((tile, D_MODEL), lambda i: (i, 0)),
                  pl.BlockSpec((tile, PACK_W), lambda i: (i, 0)),
                  pl.BlockSpec((tile, PACK_W), lambda i: (i + nt, 0)),
                  pl.BlockSpec((tile, WTOK_LANES), lambda i: (i, 0)),
                  full(lnw), full(lnb)],
        out_specs=pl.BlockSpec((tile, D_MODEL), lambda i: (i, 0)),
        out_shape=jax.ShapeDtypeStruct((T, D_MODEL), F32),
        name="combine_ln2",
    )(x1, yg, yg, wtok, lnw, lnb)


def _split_in_cols(w):
    pool, q, k, v, g, xaq, gates = jnp.split(w, [512, 1024, 1536, 2560, 3584, 4096], axis=-1)
    return jnp.concatenate([gates, v, g, pool, q, xaq], axis=-1).astype(BF16), k.T.astype(BF16)


def _router_params(w_grp, b_grp, w_exp, b_exp):
    rw = jnp.zeros((ROUTER_ROWS, D_MODEL), F32)
    rw = rw.at[0:N_GROUPS].set(w_grp.T).at[EXP_ROW0:EXP_ROW0 + N_EXPERTS].set(w_exp.T)
    rb = jnp.zeros((ROUTER_ROWS,), F32).at[N_GROUPS:8].set(NEG_BIG)
    rb = rb.at[0:N_GROUPS].set(b_grp).at[EXP_ROW0:EXP_ROW0 + N_EXPERTS].set(b_exp)
    rw_hi = rw.astype(BF16)
    rw_lo = (rw - rw_hi.astype(F32)).astype(BF16)
    return (rw_hi, rw_lo), rb[:, None]


def kernel(x, mem, positions, w_in, w_pool_grp, pool_scale, ret_gn_w, w_mem_kv, w_br_pool, w_br_ret, w_br_xa,
           w_out, ln1_w, ln1_b, w_grp_router, b_grp_router, w_exp_router, b_exp_router, w_exp_gate, w_exp_up,
           w_exp_down, ln2_w, ln2_b):
    B, S, D = x.shape
    assert D == D_MODEL and w_in.shape[0] == DEPTH and S % 512 == 0
    T = B * S
    M = mem.shape[1]
    l = 0
    xf = x.reshape(T, D)

    rope = _rope_table(positions.reshape(T, 1))
    kv = _mem_kv(mem.reshape(B * M, D), w_mem_kv[l].astype(BF16))
    rw, rb = _router_params(w_grp_router[l], b_grp_router[l], w_exp_router[l], b_exp_router[l])
    x1, xp, eid, wtok = _mixer(xf, rope, kv, *_split_in_cols(w_in[l]),
                               w_pool_grp[l].astype(BF16), pool_scale[l][None, :], ret_gn_w[l].reshape(1, -1),
                               w_br_pool[l].astype(BF16), w_br_ret[l].astype(BF16), w_br_xa[l].astype(BF16),
                               w_out[l].astype(BF16), ln1_w[l][None, :], ln1_b[l][None, :], rw, rb, B, S, M)

    pos, meta = _positions(eid)
    pos2d = pos[0:TOP_K].reshape(TOP_K * T // SC_CHUNK, SC_CHUNK)
    n_tiles = -(-((TOP_K * T + N_EXPERTS * (MOE_TM - 1)) // MOE_TM) // 8) * 8
    assert n_tiles <= META_LANES
    xs = _sc_scatter_rows(xp, pos2d, n_tiles * MOE_TM)
    ys = _routed_mlp(meta[0, :n_tiles], meta[1, :1], xs,
                     w_exp_gate[l].reshape(N_EXPERTS, D_MODEL, D_EXPERT),
                     w_exp_up[l].reshape(N_EXPERTS, D_MODEL, D_EXPERT),
                     w_exp_down[l].reshape(N_EXPERTS, D_EXPERT, D_MODEL))
    yg = _sc_gather_rows(ys, pos2d)
    out = _combine_ln2(x1, yg, wtok, ln2_w[l][None, :], ln2_b[l][None, :])
    return out.reshape(B, S, D)
```

```python
import functools

import numpy as np
import jax
import jax.numpy as jnp
from jax import lax
from jax.experimental import pallas as pl
from jax.experimental.pallas import tpu as pltpu
from jax.experimental.pallas import tpu_sc as plsc

F32 = jnp.float32
BF16 = jnp.bfloat16
I32 = jnp.int32
U32 = jnp.uint32

D_MODEL = 1024
POOL_WINDOWS = (2, 4, 8, 16)
POOL_GROUP_DIM = 128
POOL_WIDTH = 512
POOL_HALO = 16
RET_HEADS = 4
RET_QK_DIM = 128
RET_V_DIM = 256
RET_CHUNK = 128
ROPE_BASE = 10000.0
XA_HEADS = 4
XA_HEAD_DIM = 128
XA_WIDTH = 512
N_GROUPS = 4
EXPERTS_PER_GROUP = 8
N_EXPERTS = N_GROUPS * EXPERTS_PER_GROUP
D_EXPERT = 256
LN_EPS = 1e-5
DEPTH = 1
ALPHA = (2.0 * DEPTH) ** 0.25
NEG_BIG = -1e30

COL_POOL, COL_Q, COL_K, COL_V, COL_G, COL_XAQ, COL_GATES = 0, 512, 1024, 1536, 2560, 3584, 4096

VMEM_LIMIT = 56 * 1024 * 1024

TOP_K = 2
PACK_W = D_MODEL // 2
MOE_TM = 512
SC_CHUNK = 128
RANK_CHUNK = 512


def _dot(a, b):
    return jnp.dot(a, b, preferred_element_type=F32)


def _dot_nt(a, b, precision=None):
    return lax.dot_general(a, b, (((1,), (1,)), ((), ())), preferred_element_type=F32, precision=precision)


def _sigmoid(z):
    return 1.0 / (1.0 + jnp.exp(-z))


def _layer_norm(h, w, b):
    mu = jnp.mean(h, axis=-1, keepdims=True)
    hc = h - mu
    var = jnp.mean(hc * hc, axis=-1, keepdims=True)
    return hc * lax.rsqrt(var + LN_EPS) * w + b


def _rope_kernel(pos_ref, freq_ref, sign_ref, cos_ref, sin_ref, cos_t_ref, sin_t_ref):
    half = RET_QK_DIM // 2
    ang = pos_ref[...].astype(F32) * freq_ref[...]
    cos = jnp.cos(ang)
    sin = jnp.sin(ang) * sign_ref[...]
    cos_ref[...] = cos
    sin_ref[...] = sin
    cos_t_ref[...] = jnp.transpose(cos)[:half]
    sin_t_ref[...] = jnp.transpose(sin)[half:]


def _rope_table(pos_col, tile=2048):
    T = pos_col.shape[0]
    half = RET_QK_DIM // 2
    inv_freq = (ROPE_BASE ** (-np.arange(half, dtype=np.float64) / half)).astype(np.float32)
    freq = jnp.asarray(np.concatenate([inv_freq, inv_freq])[None, :])
    sign = jnp.asarray(np.concatenate([-np.ones(half, np.float32), np.ones(half, np.float32)])[None, :])
    row = pl.BlockSpec((1, RET_QK_DIM), lambda i: (0, 0))
    out = pl.BlockSpec((tile, RET_QK_DIM), lambda i: (i, 0))
    out_t = pl.BlockSpec((half, tile), lambda i: (0, i))
    return pl.pallas_call(
        _rope_kernel,
        grid=(T // tile,),
        in_specs=[pl.BlockSpec((tile, 1), lambda i: (i, 0)), row, row],
        out_specs=[out, out, out_t, out_t],
        out_shape=[jax.ShapeDtypeStruct((T, RET_QK_DIM), F32)] * 2 + [jax.ShapeDtypeStruct((half, T), F32)] * 2,
        name="rope_table",
    )(pos_col, freq, sign)


POOL_SUB = 256


def _pool_bands():
    r = np.arange(POOL_SUB)[:, None]
    c = np.arange(POOL_SUB)[None, :]
    ch = np.arange(POOL_HALO)[None, :] - POOL_HALO
    main = np.stack([((r - c >= 0) & (r - c < w)) for w in POOL_WINDOWS]).astype(np.float32)
    halo = np.stack([((r - ch >= 0) & (r - ch < w)) for w in POOL_WINDOWS]).astype(np.float32)
    return jnp.asarray(main, BF16), jnp.asarray(halo, BF16)


def _pool_branch(ub, j, halo_ref, bmain_ref, bhalo_ref, wg_ref, scale_ref, o_ref):
    tile = ub.shape[0]
    s0 = j * tile
    slot = lax.rem(j, 2)
    for sb in range(tile // POOL_SUB):
        r0 = sb * POOL_SUB
        u = ub[r0:r0 + POOL_SUB]
        prev = halo_ref[slot] if sb == 0 else ub[r0 - POOL_HALO:r0]
        pos = s0 + r0 + lax.broadcasted_iota(I32, (POOL_SUB, POOL_GROUP_DIM), 0)
        for g, w in enumerate(POOL_WINDOWS):
            cols = slice(g * POOL_GROUP_DIM, (g + 1) * POOL_GROUP_DIM)
            ug = u[:, cols]
            wsum = _dot(bmain_ref[g], ug) + _dot(bhalo_ref[g], prev[:, cols])
            cnt = jnp.minimum(pos + 1, w).astype(F32)
            pooled = wsum / cnt - ug.astype(F32)
            mixed = _dot(pooled.astype(BF16), wg_ref[g]) * scale_ref[:, cols]
            o_ref[r0:r0 + POOL_SUB, cols] = mixed.astype(BF16)
    halo_ref[1 - slot] = ub[tile - POOL_HALO:tile]


def _ret_consts():
    h = np.arange(RET_HEADS, dtype=np.float64)
    log_gamma = np.log1p(-np.exp2(-5.0 - h))
    pos = np.arange(RET_CHUNK, dtype=np.float64)
    diff = pos[:, None] - pos[None, :]
    kscale = RET_QK_DIM ** -0.5
    dmask = kscale * np.where(diff >= 0, np.exp(log_gamma[:, None, None] * np.maximum(diff, 0.0)), 0.0)
    qdec = np.exp(log_gamma[:, None] * (pos + 1.0)[None, :])
    kdec = kscale * np.exp(log_gamma[:, None] * (RET_CHUNK - 1.0 - pos)[None, :])
    cdec = np.exp(log_gamma * RET_CHUNK)
    lanes = lambda a: np.broadcast_to(a[:, :, None], (RET_HEADS, RET_CHUNK, RET_QK_DIM))
    kdec_t = np.broadcast_to(kdec[:, None, :], (RET_HEADS, RET_QK_DIM, RET_CHUNK))
    return (jnp.asarray(dmask, F32), jnp.asarray(lanes(qdec), F32), jnp.asarray(kdec_t, F32),
            tuple(float(v) for v in cdec))


def _retention_branch(q, k_t, v, silu_g, cos_ref, sin_ref, cos_t_ref, sin_t_ref, dmask_ref, qdec_ref, kdec_ref,
                      gnw_ref, state_ref, rq_ref, rqd_ref, rkt_ref, rkdt_ref, o_ref, cdec, fillers):
    tile = q.shape[0]
    n_chunks = tile // RET_CHUNK
    half = RET_QK_DIM // 2
    cos = cos_ref[...]
    sin = sin_ref[...]
    cos_t = cos_t_ref[...]
    sin_t = sin_t_ref[...]
    for h in range(RET_HEADS):
        qk = slice(h * RET_QK_DIM, (h + 1) * RET_QK_DIM)
        qh = q[:, qk]
        qr = qh * cos + pltpu.roll(qh, half, 1) * sin
        rq_ref[:, qk] = qr.astype(BF16)
        rqd_ref[:, qk] = (qr * jnp.concatenate([qdec_ref[h]] * n_chunks, axis=0)).astype(BF16)
        k1 = k_t[h * RET_QK_DIM:h * RET_QK_DIM + half]
        k2 = k_t[h * RET_QK_DIM + half:(h + 1) * RET_QK_DIM]
        kr_t = jnp.concatenate([k1 * cos_t - k2 * sin_t, k2 * cos_t + k1 * sin_t], axis=0)
        rkt_ref[qk, :] = kr_t.astype(BF16)
        rkdt_ref[qk, :] = (kr_t * jnp.concatenate([kdec_ref[h]] * n_chunks, axis=1)).astype(BF16)

    for c in range(n_chunks):
        rows = slice(c * RET_CHUNK, (c + 1) * RET_CHUNK)
        heads = [slice(h * RET_QK_DIM, (h + 1) * RET_QK_DIM) for h in range(RET_HEADS)]
        raw = [_dot(rq_ref[rows, qk], rkt_ref[qk, rows]) for qk in heads]
        fillers[c]()
        for h in range(RET_HEADS):
            qk = heads[h]
            v_cols = slice(h * RET_V_DIM, (h + 1) * RET_V_DIM)
            vh = v[rows, v_cols]
            scores = raw[h] * dmask_ref[h]
            st = state_ref[h]
            lhs = jnp.concatenate([scores.astype(BF16), rqd_ref[rows, qk]], axis=1)
            y = _dot(lhs, jnp.concatenate([vh, st.astype(BF16)], axis=0))
            state_ref[h] = cdec[h] * st + _dot(rkdt_ref[qk, rows], vh)
            mu = jnp.mean(y, axis=-1, keepdims=True)
            yc = y - mu
            var = jnp.mean(yc * yc, axis=-1, keepdims=True)
            yn = yc * lax.rsqrt(var + LN_EPS) * gnw_ref[:, v_cols]
            o_ref[rows, v_cols] = (silu_g[h][rows] * yn).astype(BF16)


def _memkv_kernel(m_ref, w_ref, o_ref):
    o_ref[...] = _dot(m_ref[...].astype(BF16), w_ref[...]).astype(BF16)


def _mem_kv(memf, w_b):
    M, D = memf.shape
    N = w_b.shape[1]
    return pl.pallas_call(
        _memkv_kernel,
        grid=(1,),
        in_specs=[pl.BlockSpec((M, D), lambda i: (0, 0)), pl.BlockSpec((D, N), lambda i: (0, 0))],
        out_specs=pl.BlockSpec((M, N), lambda i: (0, 0)),
        out_shape=jax.ShapeDtypeStruct((M, N), BF16),
        name="mem_kv",
    )(memf, w_b)


def _cross_attention_branch(xq, k_ref, v_ref, o_ref):
    scale = XA_HEAD_DIM ** -0.5
    for h in range(XA_HEADS):
        cols = slice(h * XA_HEAD_DIM, (h + 1) * XA_HEAD_DIM)
        s = _dot_nt(xq[:, cols], k_ref[:, cols]) * scale
        m = jnp.max(s, axis=-1, keepdims=True)
        p = jnp.exp(s - m)
        l = jnp.sum(p, axis=-1, keepdims=True)
        o = _dot(p.astype(BF16), v_ref[:, cols]) / l
        o_ref[:, cols] = o.astype(BF16)


ROUTER_ROWS = 40
WTOK_LANES = 128
EXP_ROW0 = 8


def _route(logits_t):
    gl = logits_t[0:8]
    gmax = jnp.max(gl, axis=0, keepdims=True)
    p_grp = 1.0 / jnp.sum(jnp.exp(gl - gmax), axis=0, keepdims=True)
    idx8 = lax.broadcasted_iota(jnp.int32, gl.shape, 0)
    gsel = jnp.min(jnp.where(gl == gmax, idx8, 8), axis=0, keepdims=True)
    cl = jnp.zeros_like(gl)
    for g in range(N_GROUPS):
        r0 = EXP_ROW0 + g * EXPERTS_PER_GROUP
        cl = cl + jnp.where(gsel == g, logits_t[r0:r0 + EXPERTS_PER_GROUP], 0.0)
    v1 = jnp.max(cl, axis=0, keepdims=True)
    i1 = jnp.min(jnp.where(cl == v1, idx8, 8), axis=0, keepdims=True)
    cl2 = jnp.where(idx8 == i1, -jnp.inf, cl)
    v2 = jnp.max(cl2, axis=0, keepdims=True)
    i2 = jnp.min(jnp.where(cl2 == v2, idx8, 8), axis=0, keepdims=True)
    e21 = jnp.exp(v2 - v1)
    w1 = p_grp / (1.0 + e21)
    w2 = p_grp * e21 / (1.0 + e21)
    return gsel * EXPERTS_PER_GROUP + i1, gsel * EXPERTS_PER_GROUP + i2, w1, w2


def _pack_halves(v):
    half = v.shape[1] // 2
    lo = lax.bitcast_convert_type(v[:, :half].astype(BF16).astype(F32), U32)
    hi = lax.bitcast_convert_type(v[:, half:].astype(BF16).astype(F32), U32)
    return lax.bitcast_convert_type(lax.shift_right_logical(lo, U32(16)) | hi, I32)


def _unpack_halves(w):
    u = lax.bitcast_convert_type(w, U32)
    lo = lax.bitcast_convert_type(lax.shift_left(u, U32(16)), F32)
    hi = lax.bitcast_convert_type(u & U32(0xFFFF0000), F32)
    return lo, hi


def _mixer_kernel(x_ref, cos_ref, sin_ref, cos_t_ref, sin_t_ref, km_ref, vm_ref, win_ref, wkt_ref, wgrp_ref,
                  pscale_ref, bmain_ref, bhalo_ref, dmask_ref, qdec_ref, kdec_ref, gnw_ref, wp_ref, wr_ref,
                  wa_ref, wo_ref, lnw_ref, lnb_ref, rwh_ref, rwl_ref, rb_ref, x1_ref, xp_ref, eid_ref, wtok_ref,
                  state_ref, halo_ref, ypool_ref, yret_ref, yxa_ref, rq_ref, rqd_ref, rkt_ref, rkdt_ref,
                  *, tile, cdec):
    j = pl.program_id(1)

    @pl.when(j == 0)
    def _():
        state_ref[...] = jnp.zeros_like(state_ref)
        halo_ref[...] = jnp.zeros_like(halo_ref)

    x = x_ref[...]
    xb = x.astype(BF16)

    def proj(col, width):
        return _dot(xb, win_ref[:, col:col + width])

    part = {}

    def pool_part():
        _pool_branch(proj(COL_POOL, POOL_WIDTH).astype(BF16), j, halo_ref, bmain_ref, bhalo_ref, wgrp_ref,
                     pscale_ref, ypool_ref)
        part["pool"] = _sigmoid(proj(COL_GATES, D_MODEL)) * _dot(ypool_ref[...], wp_ref[...])

    def xa_part():
        _cross_attention_branch(proj(COL_XAQ, XA_WIDTH).astype(BF16), km_ref, vm_ref, yxa_ref)
        part["xa"] = _sigmoid(proj(COL_GATES + 2 * D_MODEL, D_MODEL)) * _dot(yxa_ref[...], wa_ref[...])

    def ret_gate_part():
        part["ret_gate"] = _sigmoid(proj(COL_GATES + D_MODEL, D_MODEL))

    silu_g = []
    for h in range(RET_HEADS):
        gh = proj(COL_G + h * RET_V_DIM, RET_V_DIM)
        silu_g.append(gh * _sigmoid(gh))
    fillers = [pool_part, xa_part, ret_gate_part] + [lambda: None] * (tile // RET_CHUNK - 3)
    _retention_branch(proj(COL_Q, RET_HEADS * RET_QK_DIM), _dot_nt(wkt_ref[...], xb),
                      proj(COL_V, RET_HEADS * RET_V_DIM).astype(BF16), silu_g, cos_ref, sin_ref, cos_t_ref,
                      sin_t_ref, dmask_ref, qdec_ref, kdec_ref, gnw_ref, state_ref, rq_ref, rqd_ref, rkt_ref,
                      rkdt_ref, yret_ref, cdec, fillers)
    merged = part["pool"] + part["ret_gate"] * _dot(yret_ref[...], wr_ref[...]) + part["xa"]

    h = ALPHA * x + _dot(merged.astype(BF16), wo_ref[...])
    x1 = _layer_norm(h, lnw_ref[...], lnb_ref[...])
    x1_ref[...] = x1
    xp_ref[...] = _pack_halves(x1)
    x1_hi = x1.astype(BF16)
    x1_lo = (x1 - x1_hi.astype(F32)).astype(BF16)
    logits_t = (_dot_nt(rwh_ref[...], x1_hi) + _dot_nt(rwh_ref[...], x1_lo) + _dot_nt(rwl_ref[...], x1_hi)
                + rb_ref[...])
    e0, e1, w0, w1 = _route(logits_t)
    tile = e0.shape[1]
    eid_ref[...] = jnp.concatenate([e0, e1, jnp.zeros((8 - TOP_K, tile), I32)], axis=0)
    w_t = jnp.concatenate([w0, w1, jnp.zeros((WTOK_LANES - TOP_K, tile), F32)], axis=0)
    wtok_ref[...] = jnp.transpose(w_t)


def _mixer(xf, rope, kv, win, wkt, wgrp, pscale, gnw, wp, wr, wa, wo, lnw, lnb, rw, rb, batch, seq, mem_len,
           tile=512):
    T = xf.shape[0]
    nj = seq // tile
    cos, sin, cos_t, sin_t = rope
    bmain, bhalo = _pool_bands()
    dmask, qdec, kdec, cdec = _ret_consts()
    resident = lambda a: pl.BlockSpec(a.shape, lambda b, j: (0,) * a.ndim, pipeline_mode=pl.Buffered(1))
    rowblk = lambda w: pl.BlockSpec((tile, w), lambda b, j: (b * nj + j, 0))
    colblk = lambda r: pl.BlockSpec((r, tile), lambda b, j: (0, b * nj + j))
    consts = (win, wkt, wgrp, pscale, bmain, bhalo, dmask, qdec, kdec, gnw, wp, wr, wa, wo, lnw, lnb, *rw, rb)
    return pl.pallas_call(
        functools.partial(_mixer_kernel, tile=tile, cdec=cdec),
        grid=(batch, nj),
        in_specs=[rowblk(D_MODEL), rowblk(RET_QK_DIM), rowblk(RET_QK_DIM),
                  colblk(RET_QK_DIM // 2), colblk(RET_QK_DIM // 2),
                  pl.BlockSpec((mem_len, XA_WIDTH), lambda b, j: (b, 0)),
                  pl.BlockSpec((mem_len, XA_WIDTH), lambda b, j: (b, 1))] + [resident(a) for a in consts],
        out_specs=[rowblk(D_MODEL), rowblk(PACK_W), colblk(8), rowblk(WTOK_LANES)],
        out_shape=[jax.ShapeDtypeStruct((T, D_MODEL), F32), jax.ShapeDtypeStruct((T, PACK_W), I32),
                   jax.ShapeDtypeStruct((8, T), I32), jax.ShapeDtypeStruct((T, WTOK_LANES), F32)],
        scratch_shapes=[pltpu.VMEM((RET_HEADS, RET_QK_DIM, RET_V_DIM), F32),
                        pltpu.VMEM((2, POOL_HALO, POOL_WIDTH), BF16),
                        pltpu.VMEM((tile, POOL_WIDTH), BF16),
                        pltpu.VMEM((tile, RET_HEADS * RET_V_DIM), BF16),
                        pltpu.VMEM((tile, XA_WIDTH), BF16),
                        pltpu.VMEM((tile, RET_HEADS * RET_QK_DIM), BF16),
                        pltpu.VMEM((tile, RET_HEADS * RET_QK_DIM), BF16),
                        pltpu.VMEM((RET_HEADS * RET_QK_DIM, tile), BF16),
                        pltpu.VMEM((RET_HEADS * RET_QK_DIM, tile), BF16)],
        compiler_params=pltpu.CompilerParams(dimension_semantics=("arbitrary", "arbitrary"),
                                             vmem_limit_bytes=VMEM_LIMIT),
        name="mixer",
    )(xf, cos, sin, cos_t, sin_t, kv, kv, *consts)


META_LANES = 128


def _positions_kernel(eid_ref, tri_ref, low_ref, pos_ref, first_tile_ref, n_tiles_ref, *, n_tok):
    n_chunks = n_tok // RANK_CHUNK
    erow = lax.broadcasted_iota(I32, (N_EXPERTS, RANK_CHUNK), 0)

    def onehot(c):
        sl = slice(c * RANK_CHUNK, (c + 1) * RANK_CHUNK)
        m0 = eid_ref[0:1, sl] == erow
        m1 = eid_ref[1:2, sl] == erow
        return m0, m1, jnp.where(m0, 1.0, 0.0) + jnp.where(m1, 1.0, 0.0)

    counts = jnp.zeros((N_EXPERTS, 1), F32)
    for c in range(n_chunks):
        counts = counts + jnp.sum(onehot(c)[2], axis=1, keepdims=True)
    ptiles = jnp.floor((counts + (MOE_TM - 1)) * (1.0 / MOE_TM))
    ptiles_b = jnp.broadcast_to(ptiles, (N_EXPERTS, 128)).astype(BF16)
    start = _dot(low_ref[...], ptiles_b)[:, 0:1] * MOE_TM

    pos_ref[...] = jnp.zeros_like(pos_ref)
    carry = start - 1.0
    for c in range(n_chunks):
        sl = slice(c * RANK_CHUNK, (c + 1) * RANK_CHUNK)
        m0, m1, oh = onehot(c)
        rank = _dot(oh.astype(BF16), tri_ref[...]) + carry
        pos_ref[0:1, sl] = jnp.sum(jnp.where(m0, rank, 0.0), axis=0, keepdims=True).astype(I32)
        pos_ref[1:2, sl] = jnp.sum(jnp.where(m1, rank, 0.0), axis=0, keepdims=True).astype(I32)
        carry = carry + jnp.sum(oh, axis=1, keepdims=True)

    first_tile_ref[...] = jnp.broadcast_to(start * (1.0 / MOE_TM), first_tile_ref.shape).astype(I32)
    n_tiles_ref[...] = jnp.broadcast_to(ptiles, n_tiles_ref.shape).astype(I32)


def _positions(eid):
    T = eid.shape[1]
    r = np.arange(RANK_CHUNK)
    tri = jnp.asarray(r[:, None] <= r[None, :], BF16)
    e = np.arange(N_EXPERTS)
    low = jnp.asarray(e[None, :] < e[:, None], BF16)
    full = lambda a: pl.BlockSpec(a.shape, lambda i: (0,) * a.ndim)
    return pl.pallas_call(
        functools.partial(_positions_kernel, n_tok=T),
        grid=(1,),
        in_specs=[full(eid), full(tri), full(low)],
        out_specs=[pl.BlockSpec((8, T), lambda i: (0, 0)),
                   pl.BlockSpec((N_EXPERTS, META_LANES), lambda i: (0, 0)),
                   pl.BlockSpec((N_EXPERTS, META_LANES), lambda i: (0, 0))],
        out_shape=[jax.ShapeDtypeStruct((8, T), I32), jax.ShapeDtypeStruct((N_EXPERTS, META_LANES), I32),
                   jax.ShapeDtypeStruct((N_EXPERTS, META_LANES), I32)],
        name="route_positions",
    )(eid, tri, low)


def _sc_workers():
    info = plsc.get_sparse_core_info()
    return info.num_cores, info.num_cores * info.num_subcores


def _sc_scatter_rows(xp, pos2d, n_out):
    T, W = xp.shape
    n_cores, n_workers = _sc_workers()
    cpw = T // SC_CHUNK // n_workers
    mesh = plsc.VectorSubcoreMesh(core_axis_name="c", subcore_axis_name="s")

    @functools.partial(
        pl.kernel, mesh=mesh, out_type=jax.ShapeDtypeStruct((n_out, W), I32),
        scratch_types=[pltpu.VMEM((TOP_K * cpw, SC_CHUNK), I32), pltpu.VMEM((SC_CHUNK, W), I32)],
        name="sc_scatter_rows")
    def k(x_hbm, pos_hbm, out_hbm, idx_v, rows_v):
        wid = lax.axis_index("s") * n_cores + lax.axis_index("c")
        for s in range(TOP_K):
            pltpu.sync_copy(pos_hbm.at[pl.ds(s * (T // SC_CHUNK) + wid * cpw, cpw)],
                            idx_v.at[pl.ds(s * cpw, cpw)])
        for j in range(cpw):
            pltpu.sync_copy(x_hbm.at[pl.ds((wid * cpw + j) * SC_CHUNK, SC_CHUNK)], rows_v)
            for s in range(TOP_K):
                pltpu.sync_copy(rows_v, out_hbm.at[idx_v.at[s * cpw + j]])

    return k(xp, pos2d)


def _sc_gather_rows(y, idx2d):
    W = y.shape[1]
    n = idx2d.shape[0] * SC_CHUNK
    n_cores, n_workers = _sc_workers()
    cpw = n // SC_CHUNK // n_workers
    mesh = plsc.VectorSubcoreMesh(core_axis_name="c", subcore_axis_name="s")

    @functools.partial(
        pl.kernel, mesh=mesh, out_type=jax.ShapeDtypeStruct((n, W), I32),
        scratch_types=[pltpu.VMEM((cpw, SC_CHUNK), I32), pltpu.VMEM((SC_CHUNK, W), I32)],
        name="sc_gather_rows")
    def k(y_hbm, idx_hbm, out_hbm, idx_v, rows_v):
        wid = lax.axis_index("s") * n_cores + lax.axis_index("c")
        pltpu.sync_copy(idx_hbm.at[pl.ds(wid * cpw, cpw)], idx_v)
        for j in range(cpw):
            pltpu.sync_copy(y_hbm.at[idx_v.at[j]], rows_v)
            pltpu.sync_copy(rows_v, out_hbm.at[pl.ds((wid * cpw + j) * SC_CHUNK, SC_CHUNK)])

    return k(y, idx2d)


def _routed_kernel(first_ref, count_ref, xs_hbm, wg_ref, wu_ref, wd_ref, ys_hbm, xbuf, ybuf, in_sem, out_sem):
    e = pl.program_id(0)
    last = pl.num_programs(0) - 1
    total = first_ref[last] + count_ref[last]

    def in_copy(g, slot):
        return pltpu.make_async_copy(xs_hbm.at[pl.ds(g * MOE_TM, MOE_TM)], xbuf.at[slot], in_sem.at[slot])

    def out_copy(g, slot):
        return pltpu.make_async_copy(ybuf.at[slot], ys_hbm.at[pl.ds(g * MOE_TM, MOE_TM)], out_sem.at[slot])

    @pl.when(e == 0)
    def _():
        in_copy(0, 0).start()

    wg = wg_ref[0].astype(BF16)
    wu = wu_ref[0].astype(BF16)
    wd = wd_ref[0].astype(BF16)

    def tile_step(i, carry):
        g = first_ref[e] + i
        slot = lax.rem(g, 2)
        in_copy(g, slot).wait()

        @pl.when(g + 1 < total)
        def _():
            in_copy(g + 1, 1 - slot).start()

        @pl.when(g >= 2)
        def _():
            out_copy(g - 2, slot).wait()

        lo, hi = _unpack_halves(xbuf[slot])
        lo = lo.astype(BF16)
        hi = hi.astype(BF16)
        a = _dot(lo, wg[:PACK_W]) + _dot(hi, wg[PACK_W:])
        b = _dot(lo, wu[:PACK_W]) + _dot(hi, wu[PACK_W:])
        act = (a * _sigmoid(a) * b).astype(BF16)
        ybuf[slot] = _pack_halves(_dot(act, wd))
        out_copy(g, slot).start()
        return carry

    lax.fori_loop(0, count_ref[e], tile_step, 0)

    @pl.when(e == last)
    def _():
        @pl.when(total >= 2)
        def _():
            out_copy(total - 2, lax.rem(total, 2)).wait()

        out_copy(total - 1, lax.rem(total - 1, 2)).wait()


def _routed_mlp(first_tile, n_tiles, xs, wg, wu, wd):
    R = xs.shape[0]
    any_space = pl.BlockSpec(memory_space=pl.ANY)
    return pl.pallas_call(
        _routed_kernel,
        grid_spec=pltpu.PrefetchScalarGridSpec(
            num_scalar_prefetch=2,
            grid=(N_EXPERTS,),
            in_specs=[any_space,
                      pl.BlockSpec((1, D_MODEL, D_EXPERT), lambda e, ft, nt: (e, 0, 0)),
                      pl.BlockSpec((1, D_MODEL, D_EXPERT), lambda e, ft, nt: (e, 0, 0)),
                      pl.BlockSpec((1, D_EXPERT, D_MODEL), lambda e, ft, nt: (e, 0, 0))],
            out_specs=any_space,
            scratch_shapes=[pltpu.VMEM((2, MOE_TM, PACK_W), I32), pltpu.VMEM((2, MOE_TM, PACK_W), I32),
                            pltpu.SemaphoreType.DMA((2,)), pltpu.SemaphoreType.DMA((2,))]),
        out_shape=jax.ShapeDtypeStruct((R, PACK_W), I32),
        compiler_params=pltpu.CompilerParams(dimension_semantics=("arbitrary",)),
        name="routed_mlp",
    )(first_tile, n_tiles, xs, wg, wu, wd)


def _combine_kernel(x1_ref, y0_ref, y1_ref, wtok_ref, lnw_ref, lnb_ref, o_ref):
    w0 = wtok_ref[:, 0:1]
    w1 = wtok_ref[:, 1:2]
    y0lo, y0hi = _unpack_halves(y0_ref[...])
    y1lo, y1hi = _unpack_halves(y1_ref[...])
    moe = jnp.concatenate([w0 * y0lo + w1 * y1lo, w0 * y0hi + w1 * y1hi], axis=1)
    o_ref[...] = _layer_norm(ALPHA * x1_ref[...] + moe, lnw_ref[...], lnb_ref[...])


def _combine_ln2(x1, yg, wtok, lnw, lnb, tile=512):
    T = x1.shape[0]
    nt = T // tile
    full = lambda a: pl.BlockSpec(a.shape, lambda i: (0,) * a.ndim)
    return pl.pallas_call(
        _combine_kernel,
        grid=(nt,),
        in_specs=[pl.BlockSpec((tile, D_MODEL), lambda i: (i, 0)),
                  pl.BlockSpec((tile, PACK_W), lambda i: (i, 0)),
                  pl.BlockSpec((tile, PACK_W), lambda i: (i + nt, 0)),
                  pl.BlockSpec((tile, WTOK_LANES), lambda i: (i, 0)),
                  full(lnw), full(lnb)],
        out_specs=pl.BlockSpec((tile, D_MODEL), lambda i: (i, 0)),
        out_shape=jax.ShapeDtypeStruct((T, D_MODEL), F32),
        name="combine_ln2",
    )(x1, yg, yg, wtok, lnw, lnb)


def _split_in_cols(w):
    k_cols = w[:, COL_K:COL_K + RET_HEADS * RET_QK_DIM]
    return w.astype(BF16), k_cols.T.astype(BF16)


def _router_params(w_grp, b_grp, w_exp, b_exp):
    rw = jnp.zeros((ROUTER_ROWS, D_MODEL), F32)
    rw = rw.at[0:N_GROUPS].set(w_grp.T).at[EXP_ROW0:EXP_ROW0 + N_EXPERTS].set(w_exp.T)
    rb = jnp.zeros((ROUTER_ROWS,), F32).at[N_GROUPS:8].set(NEG_BIG)
    rb = rb.at[0:N_GROUPS].set(b_grp).at[EXP_ROW0:EXP_ROW0 + N_EXPERTS].set(b_exp)
    rw_hi = rw.astype(BF16)
    rw_lo = (rw - rw_hi.astype(F32)).astype(BF16)
    return (rw_hi, rw_lo), rb[:, None]


def kernel(x, mem, positions, w_in, w_pool_grp, pool_scale, ret_gn_w, w_mem_kv, w_br_pool, w_br_ret, w_br_xa,
           w_out, ln1_w, ln1_b, w_grp_router, b_grp_router, w_exp_router, b_exp_router, w_exp_gate, w_exp_up,
           w_exp_down, ln2_w, ln2_b):
    B, S, D = x.shape
    assert D == D_MODEL and w_in.shape[0] == DEPTH and S % 512 == 0
    T = B * S
    M = mem.shape[1]
    l = 0
    xf = x.reshape(T, D)

    rope = _rope_table(positions.reshape(T, 1))
    kv = _mem_kv(mem.reshape(B * M, D), w_mem_kv[l].astype(BF16))
    rw, rb = _router_params(w_grp_router[l], b_grp_router[l], w_exp_router[l], b_exp_router[l])
    x1, xp, eid, wtok = _mixer(xf, rope, kv, *_split_in_cols(w_in[l]),
                               w_pool_grp[l].astype(BF16), pool_scale[l][None, :], ret_gn_w[l].reshape(1, -1),
                               w_br_pool[l].astype(BF16), w_br_ret[l].astype(BF16), w_br_xa[l].astype(BF16),
                               w_out[l].astype(BF16), ln1_w[l][None, :], ln1_b[l][None, :], rw, rb, B, S, M)

    pos, first_tile, n_tiles = _positions(eid)
    pos2d = pos[0:TOP_K].reshape(TOP_K * T // SC_CHUNK, SC_CHUNK)
    max_tiles = (TOP_K * T + N_EXPERTS * (MOE_TM - 1)) // MOE_TM
    xs = _sc_scatter_rows(xp, pos2d, max_tiles * MOE_TM)
    ys = _routed_mlp(first_tile[:, 0], n_tiles[:, 0], xs,
                     w_exp_gate[l].reshape(N_EXPERTS, D_MODEL, D_EXPERT),
                     w_exp_up[l].reshape(N_EXPERTS, D_MODEL, D_EXPERT),
                     w_exp_down[l].reshape(N_EXPERTS, D_EXPERT, D_MODEL))
    yg = _sc_gather_rows(ys, pos2d)
    out = _combine_ln2(x1, yg, wtok, ln2_w[l][None, :], ln2_b[l][None, :])
    return out.reshape(B, S, D)
```

```python
import functools

import numpy as np
import jax
import jax.numpy as jnp
from jax import lax
from jax.experimental import pallas as pl
from jax.experimental.pallas import tpu as pltpu
from jax.experimental.pallas import tpu_sc as plsc

F32 = jnp.float32
BF16 = jnp.bfloat16
I32 = jnp.int32
U32 = jnp.uint32

D_MODEL = 1024
POOL_WINDOWS = (2, 4, 8, 16)
POOL_GROUP_DIM = 128
POOL_WIDTH = 512
POOL_HALO = 16
RET_HEADS = 4
RET_QK_DIM = 128
RET_V_DIM = 256
RET_CHUNK = 128
ROPE_BASE = 10000.0
XA_HEADS = 4
XA_HEAD_DIM = 128
XA_WIDTH = 512
N_GROUPS = 4
EXPERTS_PER_GROUP = 8
N_EXPERTS = N_GROUPS * EXPERTS_PER_GROUP
D_EXPERT = 256
LN_EPS = 1e-5
DEPTH = 1
ALPHA = (2.0 * DEPTH) ** 0.25
NEG_BIG = -1e30

COL_POOL, COL_Q, COL_K, COL_V, COL_G, COL_XAQ, COL_GATES = 0, 512, 1024, 1536, 2560, 3584, 4096

VMEM_LIMIT = 56 * 1024 * 1024

TOP_K = 2
PACK_W = D_MODEL // 2
MOE_TM = 512
SC_CHUNK = 128
RANK_CHUNK = 512


def _dot(a, b):
    return jnp.dot(a, b, preferred_element_type=F32)


def _dot_nt(a, b, precision=None):
    return lax.dot_general(a, b, (((1,), (1,)), ((), ())), preferred_element_type=F32, precision=precision)


def _sigmoid(z):
    return 1.0 / (1.0 + jnp.exp(-z))


def _layer_norm(h, w, b):
    mu = jnp.mean(h, axis=-1, keepdims=True)
    hc = h - mu
    var = jnp.mean(hc * hc, axis=-1, keepdims=True)
    return hc * lax.rsqrt(var + LN_EPS) * w + b


def _rope_kernel(pos_ref, freq_ref, cos_ref, sin_ref, cos_t_ref, sin_t_ref):
    ang = freq_ref[...] * pos_ref[...].astype(F32)
    cos_t = jnp.cos(ang)
    sin_t = jnp.sin(ang)
    cos_t_ref[...] = cos_t
    sin_t_ref[...] = sin_t
    cos_ref[...] = jnp.transpose(jnp.concatenate([cos_t, cos_t], axis=0))
    sin_ref[...] = jnp.transpose(jnp.concatenate([-sin_t, sin_t], axis=0))


def _rope_table(pos_row, tile=2048):
    T = pos_row.shape[1]
    half = RET_QK_DIM // 2
    inv_freq = (ROPE_BASE ** (-np.arange(half, dtype=np.float64) / half)).astype(np.float32)
    freq = jnp.asarray(inv_freq[:, None])
    out = pl.BlockSpec((tile, RET_QK_DIM), lambda i: (i, 0))
    out_t = pl.BlockSpec((half, tile), lambda i: (0, i))
    return pl.pallas_call(
        _rope_kernel,
        grid=(T // tile,),
        in_specs=[pl.BlockSpec((1, tile), lambda i: (0, i)), pl.BlockSpec((half, 1), lambda i: (0, 0))],
        out_specs=[out, out, out_t, out_t],
        out_shape=[jax.ShapeDtypeStruct((T, RET_QK_DIM), F32)] * 2 + [jax.ShapeDtypeStruct((half, T), F32)] * 2,
        name="rope_table",
    )(pos_row, freq)


POOL_SUB = 256


def _pool_bands():
    r = np.arange(POOL_SUB)[:, None]
    c = np.arange(POOL_SUB)[None, :]
    ch = np.arange(POOL_HALO)[None, :] - POOL_HALO
    main = np.stack([((r - c >= 0) & (r - c < w)) for w in POOL_WINDOWS]).astype(np.float32)
    halo = np.stack([((r - ch >= 0) & (r - ch < w)) for w in POOL_WINDOWS]).astype(np.float32)
    return jnp.asarray(main, BF16), jnp.asarray(halo, BF16)


def _pool_branch(ub, j, halo_ref, bmain_ref, bhalo_ref, wg_ref, scale_ref, o_ref):
    tile = ub.shape[0]
    s0 = j * tile
    slot = lax.rem(j, 2)
    for sb in range(tile // POOL_SUB):
        r0 = sb * POOL_SUB
        u = ub[r0:r0 + POOL_SUB]
        prev = halo_ref[slot] if sb == 0 else ub[r0 - POOL_HALO:r0]
        pos = s0 + r0 + lax.broadcasted_iota(I32, (POOL_SUB, POOL_GROUP_DIM), 0)
        for g, w in enumerate(POOL_WINDOWS):
            cols = slice(g * POOL_GROUP_DIM, (g + 1) * POOL_GROUP_DIM)
            ug = u[:, cols]
            wsum = _dot(bmain_ref[g], ug) + _dot(bhalo_ref[g], prev[:, cols])
            cnt = jnp.minimum(pos + 1, w).astype(F32)
            pooled = wsum / cnt - ug.astype(F32)
            mixed = _dot(pooled.astype(BF16), wg_ref[g]) * scale_ref[:, cols]
            o_ref[r0:r0 + POOL_SUB, cols] = mixed.astype(BF16)
    halo_ref[1 - slot] = ub[tile - POOL_HALO:tile]


def _ret_consts():
    h = np.arange(RET_HEADS, dtype=np.float64)
    log_gamma = np.log1p(-np.exp2(-5.0 - h))
    pos = np.arange(RET_CHUNK, dtype=np.float64)
    diff = pos[:, None] - pos[None, :]
    kscale = RET_QK_DIM ** -0.5
    dmask = kscale * np.where(diff >= 0, np.exp(log_gamma[:, None, None] * np.maximum(diff, 0.0)), 0.0)
    qdec = np.exp(log_gamma[:, None] * (pos + 1.0)[None, :])
    kdec = kscale * np.exp(log_gamma[:, None] * (RET_CHUNK - 1.0 - pos)[None, :])
    cdec = np.exp(log_gamma * RET_CHUNK)
    lanes = lambda a: np.broadcast_to(a[:, :, None], (RET_HEADS, RET_CHUNK, RET_QK_DIM))
    kdec_t = np.broadcast_to(kdec[:, None, :], (RET_HEADS, RET_QK_DIM, RET_CHUNK))
    return (jnp.asarray(dmask, F32), jnp.asarray(lanes(qdec), F32), jnp.asarray(kdec_t, F32),
            tuple(float(v) for v in cdec))


def _retention_branch(q, k_t, v, silu_g, cos_ref, sin_ref, cos_t_ref, sin_t_ref, dmask_ref, qdec_ref, kdec_ref,
                      gnw_ref, state_ref, rq_ref, rqd_ref, rkt_ref, rkdt_ref, o_ref, cdec, fillers):
    tile = q.shape[0]
    n_chunks = tile // RET_CHUNK
    half = RET_QK_DIM // 2
    cos = cos_ref[...]
    sin = sin_ref[...]
    cos_t = cos_t_ref[...]
    sin_t = sin_t_ref[...]
    for h in range(RET_HEADS):
        qk = slice(h * RET_QK_DIM, (h + 1) * RET_QK_DIM)
        qh = q[:, qk]
        qr = qh * cos + pltpu.roll(qh, half, 1) * sin
        rq_ref[:, qk] = qr.astype(BF16)
        rqd_ref[:, qk] = (qr * jnp.concatenate([qdec_ref[h]] * n_chunks, axis=0)).astype(BF16)
        k1 = k_t[h * RET_QK_DIM:h * RET_QK_DIM + half]
        k2 = k_t[h * RET_QK_DIM + half:(h + 1) * RET_QK_DIM]
        kr_t = jnp.concatenate([k1 * cos_t - k2 * sin_t, k2 * cos_t + k1 * sin_t], axis=0)
        rkt_ref[qk, :] = kr_t.astype(BF16)
        rkdt_ref[qk, :] = (kr_t * jnp.concatenate([kdec_ref[h]] * n_chunks, axis=1)).astype(BF16)

    for c in range(n_chunks):
        rows = slice(c * RET_CHUNK, (c + 1) * RET_CHUNK)
        heads = [slice(h * RET_QK_DIM, (h + 1) * RET_QK_DIM) for h in range(RET_HEADS)]
        raw = [_dot(rq_ref[rows, qk], rkt_ref[qk, rows]) for qk in heads]
        fillers[c]()
        for h in range(RET_HEADS):
            qk = heads[h]
            v_cols = slice(h * RET_V_DIM, (h + 1) * RET_V_DIM)
            vh = v[rows, v_cols]
            scores = raw[h] * dmask_ref[h]
            st = state_ref[h]
            lhs = jnp.concatenate([scores.astype(BF16), rqd_ref[rows, qk]], axis=1)
            y = _dot(lhs, jnp.concatenate([vh, st.astype(BF16)], axis=0))
            state_ref[h] = cdec[h] * st + _dot(rkdt_ref[qk, rows], vh)
            mu = jnp.mean(y, axis=-1, keepdims=True)
            yc = y - mu
            var = jnp.mean(yc * yc, axis=-1, keepdims=True)
            yn = yc * lax.rsqrt(var + LN_EPS) * gnw_ref[:, v_cols]
            o_ref[rows, v_cols] = (silu_g[h][rows] * yn).astype(BF16)


def _memkv_kernel(m_ref, w_ref, o_ref):
    o_ref[...] = _dot(m_ref[...].astype(BF16), w_ref[...]).astype(BF16)


def _mem_kv(memf, w_b):
    M, D = memf.shape
    N = w_b.shape[1]
    return pl.pallas_call(
        _memkv_kernel,
        grid=(1,),
        in_specs=[pl.BlockSpec((M, D), lambda i: (0, 0)), pl.BlockSpec((D, N), lambda i: (0, 0))],
        out_specs=pl.BlockSpec((M, N), lambda i: (0, 0)),
        out_shape=jax.ShapeDtypeStruct((M, N), BF16),
        name="mem_kv",
    )(memf, w_b)


def _cross_attention_branch(xq, k_ref, v_ref, o_ref):
    scale = XA_HEAD_DIM ** -0.5
    for h in range(XA_HEADS):
        cols = slice(h * XA_HEAD_DIM, (h + 1) * XA_HEAD_DIM)
        s = _dot_nt(xq[:, cols], k_ref[:, cols]) * scale
        m = jnp.max(s, axis=-1, keepdims=True)
        p = jnp.exp(s - m)
        l = jnp.sum(p, axis=-1, keepdims=True)
        o = _dot(p.astype(BF16), v_ref[:, cols]) / l
        o_ref[:, cols] = o.astype(BF16)


ROUTER_ROWS = 40
WTOK_LANES = 128
EXP_ROW0 = 8


def _route(logits_t):
    gl = logits_t[0:8]
    gmax = jnp.max(gl, axis=0, keepdims=True)
    p_grp = 1.0 / jnp.sum(jnp.exp(gl - gmax), axis=0, keepdims=True)
    idx8 = lax.broadcasted_iota(jnp.int32, gl.shape, 0)
    gsel = jnp.min(jnp.where(gl == gmax, idx8, 8), axis=0, keepdims=True)
    cl = jnp.zeros_like(gl)
    for g in range(N_GROUPS):
        r0 = EXP_ROW0 + g * EXPERTS_PER_GROUP
        cl = cl + jnp.where(gsel == g, logits_t[r0:r0 + EXPERTS_PER_GROUP], 0.0)
    v1 = jnp.max(cl, axis=0, keepdims=True)
    i1 = jnp.min(jnp.where(cl == v1, idx8, 8), axis=0, keepdims=True)
    cl2 = jnp.where(idx8 == i1, -jnp.inf, cl)
    v2 = jnp.max(cl2, axis=0, keepdims=True)
    i2 = jnp.min(jnp.where(cl2 == v2, idx8, 8), axis=0, keepdims=True)
    e21 = jnp.exp(v2 - v1)
    w1 = p_grp / (1.0 + e21)
    w2 = p_grp * e21 / (1.0 + e21)
    return gsel * EXPERTS_PER_GROUP + i1, gsel * EXPERTS_PER_GROUP + i2, w1, w2


def _pack_halves(v):
    half = v.shape[1] // 2
    lo = lax.bitcast_convert_type(v[:, :half].astype(BF16).astype(F32), U32)
    hi = lax.bitcast_convert_type(v[:, half:].astype(BF16).astype(F32), U32)
    return lax.bitcast_convert_type(lax.shift_right_logical(lo, U32(16)) | hi, I32)


def _unpack_halves(w):
    u = lax.bitcast_convert_type(w, U32)
    lo = lax.bitcast_convert_type(lax.shift_left(u, U32(16)), F32)
    hi = lax.bitcast_convert_type(u & U32(0xFFFF0000), F32)
    return lo, hi


def _mixer_kernel(x_ref, cos_ref, sin_ref, cos_t_ref, sin_t_ref, km_ref, vm_ref, win_ref, wkt_ref, wgrp_ref,
                  pscale_ref, bmain_ref, bhalo_ref, dmask_ref, qdec_ref, kdec_ref, gnw_ref, wp_ref, wr_ref,
                  wa_ref, wo_ref, lnw_ref, lnb_ref, rwh_ref, rwl_ref, rb_ref, xp_ref, eid_ref, wtok_ref,
                  state_ref, halo_ref, ypool_ref, yret_ref, yxa_ref, rq_ref, rqd_ref, rkt_ref, rkdt_ref,
                  *, tile, cdec):
    j = pl.program_id(1)

    @pl.when(j == 0)
    def _():
        state_ref[...] = jnp.zeros_like(state_ref)
        halo_ref[...] = jnp.zeros_like(halo_ref)

    x = x_ref[...]
    xb = x.astype(BF16)

    def proj(col, width):
        return _dot(xb, win_ref[:, col:col + width])

    part = {}

    def pool_part():
        _pool_branch(proj(COL_POOL, POOL_WIDTH).astype(BF16), j, halo_ref, bmain_ref, bhalo_ref, wgrp_ref,
                     pscale_ref, ypool_ref)
        part["pool"] = _sigmoid(proj(COL_GATES, D_MODEL)) * _dot(ypool_ref[...], wp_ref[...])

    def xa_part():
        _cross_attention_branch(proj(COL_XAQ, XA_WIDTH).astype(BF16), km_ref, vm_ref, yxa_ref)
        part["xa"] = _sigmoid(proj(COL_GATES + 2 * D_MODEL, D_MODEL)) * _dot(yxa_ref[...], wa_ref[...])

    def ret_gate_part():
        part["ret_gate"] = _sigmoid(proj(COL_GATES + D_MODEL, D_MODEL))

    silu_g = []
    for h in range(RET_HEADS):
        gh = proj(COL_G + h * RET_V_DIM, RET_V_DIM)
        silu_g.append(gh * _sigmoid(gh))
    fillers = [pool_part, xa_part, ret_gate_part] + [lambda: None] * (tile // RET_CHUNK - 3)
    _retention_branch(proj(COL_Q, RET_HEADS * RET_QK_DIM), _dot_nt(wkt_ref[...], xb),
                      proj(COL_V, RET_HEADS * RET_V_DIM).astype(BF16), silu_g, cos_ref, sin_ref, cos_t_ref,
                      sin_t_ref, dmask_ref, qdec_ref, kdec_ref, gnw_ref, state_ref, rq_ref, rqd_ref, rkt_ref,
                      rkdt_ref, yret_ref, cdec, fillers)
    merged = part["pool"] + part["ret_gate"] * _dot(yret_ref[...], wr_ref[...]) + part["xa"]

    h = ALPHA * x + _dot(merged.astype(BF16), wo_ref[...])
    x1 = _layer_norm(h, lnw_ref[...], lnb_ref[...])
    xp_ref[...] = _pack_halves(x1)
    x1_hi = x1.astype(BF16)
    x1_lo = (x1 - x1_hi.astype(F32)).astype(BF16)
    logits_t = (_dot_nt(rwh_ref[...], x1_hi) + _dot_nt(rwh_ref[...], x1_lo) + _dot_nt(rwl_ref[...], x1_hi)
                + rb_ref[...])
    e0, e1, w0, w1 = _route(logits_t)
    tile = e0.shape[1]
    eid_ref[...] = jnp.concatenate([e0, e1, jnp.zeros((8 - TOP_K, tile), I32)], axis=0)
    w_t = jnp.concatenate([w0, w1, jnp.zeros((WTOK_LANES - TOP_K, tile), F32)], axis=0)
    wtok_ref[...] = jnp.transpose(w_t)


def _mixer(xf, rope, kv, win, wkt, wgrp, pscale, gnw, wp, wr, wa, wo, lnw, lnb, rw, rb, batch, seq, mem_len,
           tile=512):
    T = xf.shape[0]
    nj = seq // tile
    cos, sin, cos_t, sin_t = rope
    bmain, bhalo = _pool_bands()
    dmask, qdec, kdec, cdec = _ret_consts()
    resident = lambda a: pl.BlockSpec(a.shape, lambda b, j: (0,) * a.ndim, pipeline_mode=pl.Buffered(1))
    rowblk = lambda w: pl.BlockSpec((tile, w), lambda b, j: (b * nj + j, 0))
    colblk = lambda r: pl.BlockSpec((r, tile), lambda b, j: (0, b * nj + j))
    consts = (win, wkt, wgrp, pscale, bmain, bhalo, dmask, qdec, kdec, gnw, wp, wr, wa, wo, lnw, lnb, *rw, rb)
    return pl.pallas_call(
        functools.partial(_mixer_kernel, tile=tile, cdec=cdec),
        grid=(batch, nj),
        in_specs=[rowblk(D_MODEL), rowblk(RET_QK_DIM), rowblk(RET_QK_DIM),
                  colblk(RET_QK_DIM // 2), colblk(RET_QK_DIM // 2),
                  pl.BlockSpec((mem_len, XA_WIDTH), lambda b, j: (b, 0)),
                  pl.BlockSpec((mem_len, XA_WIDTH), lambda b, j: (b, 1))] + [resident(a) for a in consts],
        out_specs=[rowblk(PACK_W), colblk(8), rowblk(WTOK_LANES)],
        out_shape=[jax.ShapeDtypeStruct((T, PACK_W), I32), jax.ShapeDtypeStruct((8, T), I32),
                   jax.ShapeDtypeStruct((T, WTOK_LANES), F32)],
        scratch_shapes=[pltpu.VMEM((RET_HEADS, RET_QK_DIM, RET_V_DIM), F32),
                        pltpu.VMEM((2, POOL_HALO, POOL_WIDTH), BF16),
                        pltpu.VMEM((tile, POOL_WIDTH), BF16),
                        pltpu.VMEM((tile, RET_HEADS * RET_V_DIM), BF16),
                        pltpu.VMEM((tile, XA_WIDTH), BF16),
                        pltpu.VMEM((tile, RET_HEADS * RET_QK_DIM), BF16),
                        pltpu.VMEM((tile, RET_HEADS * RET_QK_DIM), BF16),
                        pltpu.VMEM((RET_HEADS * RET_QK_DIM, tile), BF16),
                        pltpu.VMEM((RET_HEADS * RET_QK_DIM, tile), BF16)],
        compiler_params=pltpu.CompilerParams(dimension_semantics=("arbitrary", "arbitrary"),
                                             vmem_limit_bytes=VMEM_LIMIT),
        name="mixer",
    )(xf, cos, sin, cos_t, sin_t, kv, kv, *consts)


META_LANES = 128


def _positions_kernel(eid_ref, tri_ref, low_ref, pos_ref, first_tile_ref, n_tiles_ref, *, n_tok):
    n_chunks = n_tok // RANK_CHUNK
    erow = lax.broadcasted_iota(I32, (N_EXPERTS, RANK_CHUNK), 0)

    def onehot(c):
        sl = slice(c * RANK_CHUNK, (c + 1) * RANK_CHUNK)
        m0 = eid_ref[0:1, sl] == erow
        m1 = eid_ref[1:2, sl] == erow
        return m0, m1, jnp.where(m0, 1.0, 0.0) + jnp.where(m1, 1.0, 0.0)

    counts = jnp.zeros((N_EXPERTS, 1), F32)
    for c in range(n_chunks):
        counts = counts + jnp.sum(onehot(c)[2], axis=1, keepdims=True)
    ptiles = jnp.floor((counts + (MOE_TM - 1)) * (1.0 / MOE_TM))
    ptiles_b = jnp.broadcast_to(ptiles, (N_EXPERTS, 128)).astype(BF16)
    start = _dot(low_ref[...], ptiles_b)[:, 0:1] * MOE_TM

    pos_ref[...] = jnp.zeros_like(pos_ref)
    carry = start - 1.0
    for c in range(n_chunks):
        sl = slice(c * RANK_CHUNK, (c + 1) * RANK_CHUNK)
        m0, m1, oh = onehot(c)
        rank = _dot(oh.astype(BF16), tri_ref[...]) + carry
        pos_ref[0:1, sl] = jnp.sum(jnp.where(m0, rank, 0.0), axis=0, keepdims=True).astype(I32)
        pos_ref[1:2, sl] = jnp.sum(jnp.where(m1, rank, 0.0), axis=0, keepdims=True).astype(I32)
        carry = carry + jnp.sum(oh, axis=1, keepdims=True)

    first_tile_ref[...] = jnp.broadcast_to(start * (1.0 / MOE_TM), first_tile_ref.shape).astype(I32)
    n_tiles_ref[...] = jnp.broadcast_to(ptiles, n_tiles_ref.shape).astype(I32)


def _positions(eid):
    T = eid.shape[1]
    r = np.arange(RANK_CHUNK)
    tri = jnp.asarray(r[:, None] <= r[None, :], BF16)
    e = np.arange(N_EXPERTS)
    low = jnp.asarray(e[None, :] < e[:, None], BF16)
    full = lambda a: pl.BlockSpec(a.shape, lambda i: (0,) * a.ndim)
    return pl.pallas_call(
        functools.partial(_positions_kernel, n_tok=T),
        grid=(1,),
        in_specs=[full(eid), full(tri), full(low)],
        out_specs=[pl.BlockSpec((8, T), lambda i: (0, 0)),
                   pl.BlockSpec((N_EXPERTS, META_LANES), lambda i: (0, 0)),
                   pl.BlockSpec((N_EXPERTS, META_LANES), lambda i: (0, 0))],
        out_shape=[jax.ShapeDtypeStruct((8, T), I32), jax.ShapeDtypeStruct((N_EXPERTS, META_LANES), I32),
                   jax.ShapeDtypeStruct((N_EXPERTS, META_LANES), I32)],
        name="route_positions",
    )(eid, tri, low)


def _sc_workers():
    info = plsc.get_sparse_core_info()
    return info.num_cores, info.num_cores * info.num_subcores


def _sc_scatter_rows(xp, pos2d, n_out):
    T, W = xp.shape
    n_cores, n_workers = _sc_workers()
    cpw = T // SC_CHUNK // n_workers
    mesh = plsc.VectorSubcoreMesh(core_axis_name="c", subcore_axis_name="s")

    @functools.partial(
        pl.kernel, mesh=mesh, out_type=jax.ShapeDtypeStruct((n_out, W), I32),
        scratch_types=[pltpu.VMEM((TOP_K * cpw, SC_CHUNK), I32), pltpu.VMEM((SC_CHUNK, W), I32)],
        name="sc_scatter_rows")
    def k(x_hbm, pos_hbm, out_hbm, idx_v, rows_v):
        wid = lax.axis_index("s") * n_cores + lax.axis_index("c")
        for s in range(TOP_K):
            pltpu.sync_copy(pos_hbm.at[pl.ds(s * (T // SC_CHUNK) + wid * cpw, cpw)],
                            idx_v.at[pl.ds(s * cpw, cpw)])
        for j in range(cpw):
            pltpu.sync_copy(x_hbm.at[pl.ds((wid * cpw + j) * SC_CHUNK, SC_CHUNK)], rows_v)
            for s in range(TOP_K):
                pltpu.sync_copy(rows_v, out_hbm.at[idx_v.at[s * cpw + j]])

    return k(xp, pos2d)


def _sc_gather_rows(y, idx2d):
    W = y.shape[1]
    n = idx2d.shape[0] * SC_CHUNK
    n_cores, n_workers = _sc_workers()
    cpw = n // SC_CHUNK // n_workers
    mesh = plsc.VectorSubcoreMesh(core_axis_name="c", subcore_axis_name="s")

    @functools.partial(
        pl.kernel, mesh=mesh, out_type=jax.ShapeDtypeStruct((n, W), I32),
        scratch_types=[pltpu.VMEM((cpw, SC_CHUNK), I32), pltpu.VMEM((SC_CHUNK, W), I32)],
        name="sc_gather_rows")
    def k(y_hbm, idx_hbm, out_hbm, idx_v, rows_v):
        wid = lax.axis_index("s") * n_cores + lax.axis_index("c")
        pltpu.sync_copy(idx_hbm.at[pl.ds(wid * cpw, cpw)], idx_v)
        for j in range(cpw):
            pltpu.sync_copy(y_hbm.at[idx_v.at[j]], rows_v)
            pltpu.sync_copy(rows_v, out_hbm.at[pl.ds((wid * cpw + j) * SC_CHUNK, SC_CHUNK)])

    return k(y, idx2d)


def _routed_kernel(first_ref, count_ref, xs_hbm, wg_ref, wu_ref, wd_ref, ys_hbm, xbuf, ybuf, in_sem, out_sem):
    e = pl.program_id(0)
    last = pl.num_programs(0) - 1
    total = first_ref[last] + count_ref[last]

    def in_copy(g, slot):
        return pltpu.make_async_copy(xs_hbm.at[pl.ds(g * MOE_TM, MOE_TM)], xbuf.at[slot], in_sem.at[slot])

    def out_copy(g, slot):
        return pltpu.make_async_copy(ybuf.at[slot], ys_hbm.at[pl.ds(g * MOE_TM, MOE_TM)], out_sem.at[slot])

    @pl.when(e == 0)
    def _():
        in_copy(0, 0).start()

    wg = wg_ref[0].astype(BF16)
    wu = wu_ref[0].astype(BF16)
    wd = wd_ref[0].astype(BF16)

    def tile_step(i, carry):
        g = first_ref[e] + i
        slot = lax.rem(g, 2)
        in_copy(g, slot).wait()

        @pl.when(g + 1 < total)
        def _():
            in_copy(g + 1, 1 - slot).start()

        @pl.when(g >= 2)
        def _():
            out_copy(g - 2, slot).wait()

        lo, hi = _unpack_halves(xbuf[slot])
        lo = lo.astype(BF16)
        hi = hi.astype(BF16)
        a = _dot(lo, wg[:PACK_W]) + _dot(hi, wg[PACK_W:])
        b = _dot(lo, wu[:PACK_W]) + _dot(hi, wu[PACK_W:])
        act = (a * _sigmoid(a) * b).astype(BF16)
        ybuf[slot] = _pack_halves(_dot(act, wd))
        out_copy(g, slot).start()
        return carry

    lax.fori_loop(0, count_ref[e], tile_step, 0)

    @pl.when(e == last)
    def _():
        @pl.when(total >= 2)
        def _():
            out_copy(total - 2, lax.rem(total, 2)).wait()

        out_copy(total - 1, lax.rem(total - 1, 2)).wait()


def _routed_mlp(first_tile, n_tiles, xs, wg, wu, wd):
    R = xs.shape[0]
    any_space = pl.BlockSpec(memory_space=pl.ANY)
    return pl.pallas_call(
        _routed_kernel,
        grid_spec=pltpu.PrefetchScalarGridSpec(
            num_scalar_prefetch=2,
            grid=(N_EXPERTS,),
            in_specs=[any_space,
                      pl.BlockSpec((1, D_MODEL, D_EXPERT), lambda e, ft, nt: (e, 0, 0)),
                      pl.BlockSpec((1, D_MODEL, D_EXPERT), lambda e, ft, nt: (e, 0, 0)),
                      pl.BlockSpec((1, D_EXPERT, D_MODEL), lambda e, ft, nt: (e, 0, 0))],
            out_specs=any_space,
            scratch_shapes=[pltpu.VMEM((2, MOE_TM, PACK_W), I32), pltpu.VMEM((2, MOE_TM, PACK_W), I32),
                            pltpu.SemaphoreType.DMA((2,)), pltpu.SemaphoreType.DMA((2,))]),
        out_shape=jax.ShapeDtypeStruct((R, PACK_W), I32),
        compiler_params=pltpu.CompilerParams(dimension_semantics=("arbitrary",)),
        name="routed_mlp",
    )(first_tile, n_tiles, xs, wg, wu, wd)


def _combine_kernel(xp_ref, y0_ref, y1_ref, wtok_ref, lnw_ref, lnb_ref, o_ref):
    w0 = wtok_ref[:, 0:1]
    w1 = wtok_ref[:, 1:2]
    xlo, xhi = _unpack_halves(xp_ref[...])
    y0lo, y0hi = _unpack_halves(y0_ref[...])
    y1lo, y1hi = _unpack_halves(y1_ref[...])
    h = jnp.concatenate([ALPHA * xlo + (w0 * y0lo + w1 * y1lo), ALPHA * xhi + (w0 * y0hi + w1 * y1hi)], axis=1)
    o_ref[...] = _layer_norm(h, lnw_ref[...], lnb_ref[...])


def _combine_ln2(xp, yg, wtok, lnw, lnb, tile=512):
    T = xp.shape[0]
    nt = T // tile
    full = lambda a: pl.BlockSpec(a.shape, lambda i: (0,) * a.ndim)
    return pl.pallas_call(
        _combine_kernel,
        grid=(nt,),
        in_specs=[pl.BlockSpec((tile, PACK_W), lambda i: (i, 0)),
                  pl.BlockSpec((tile, PACK_W), lambda i: (i, 0)),
                  pl.BlockSpec((tile, PACK_W), lambda i: (i + nt, 0)),
                  pl.BlockSpec((tile, WTOK_LANES), lambda i: (i, 0)),
                  full(lnw), full(lnb)],
        out_specs=pl.BlockSpec((tile, D_MODEL), lambda i: (i, 0)),
        out_shape=jax.ShapeDtypeStruct((T, D_MODEL), F32),
        name="combine_ln2",
    )(xp, yg, yg, wtok, lnw, lnb)


def _split_in_cols(w):
    k_cols = lax.optimization_barrier(w[:, COL_K:COL_K + RET_HEADS * RET_QK_DIM])
    return w.astype(BF16), k_cols.T.astype(BF16)


def _router_params(w_grp, b_grp, w_exp, b_exp):
    rw = jnp.zeros((ROUTER_ROWS, D_MODEL), F32)
    rw = rw.at[0:N_GROUPS].set(w_grp.T).at[EXP_ROW0:EXP_ROW0 + N_EXPERTS].set(w_exp.T)
    rb = jnp.zeros((ROUTER_ROWS,), F32).at[N_GROUPS:8].set(NEG_BIG)
    rb = rb.at[0:N_GROUPS].set(b_grp).at[EXP_ROW0:EXP_ROW0 + N_EXPERTS].set(b_exp)
    rw_hi = rw.astype(BF16)
    rw_lo = (rw - rw_hi.astype(F32)).astype(BF16)
    return (rw_hi, rw_lo), rb[:, None]


def kernel(x, mem, positions, w_in, w_pool_grp, pool_scale, ret_gn_w, w_mem_kv, w_br_pool, w_br_ret, w_br_xa,
           w_out, ln1_w, ln1_b, w_grp_router, b_grp_router, w_exp_router, b_exp_router, w_exp_gate, w_exp_up,
           w_exp_down, ln2_w, ln2_b):
    B, S, D = x.shape
    assert D == D_MODEL and w_in.shape[0] == DEPTH and S % 512 == 0
    T = B * S
    M = mem.shape[1]
    l = 0
    xf = x.reshape(T, D)

    rope = _rope_table(positions.reshape(1, T))
    kv = _mem_kv(mem.reshape(B * M, D), w_mem_kv[l].astype(BF16))
    rw, rb = _router_params(w_grp_router[l], b_grp_router[l], w_exp_router[l], b_exp_router[l])
    xp, eid, wtok = _mixer(xf, rope, kv, *_split_in_cols(w_in[l]),
                           w_pool_grp[l].astype(BF16), pool_scale[l][None, :], ret_gn_w[l].reshape(1, -1),
                           w_br_pool[l].astype(BF16), w_br_ret[l].astype(BF16), w_br_xa[l].astype(BF16),
                           w_out[l].astype(BF16), ln1_w[l][None, :], ln1_b[l][None, :], rw, rb, B, S, M)

    pos, first_tile, n_tiles = _positions(eid)
    pos2d = pos[0:TOP_K].reshape(TOP_K * T // SC_CHUNK, SC_CHUNK)
    max_tiles = (TOP_K * T + N_EXPERTS * (MOE_TM - 1)) // MOE_TM
    xs = _sc_scatter_rows(xp, pos2d, max_tiles * MOE_TM)
    ys = _routed_mlp(first_tile[:, 0], n_tiles[:, 0], xs,
                     w_exp_gate[l].reshape(N_EXPERTS, D_MODEL, D_EXPERT),
                     w_exp_up[l].reshape(N_EXPERTS, D_MODEL, D_EXPERT),
                     w_exp_down[l].reshape(N_EXPERTS, D_EXPERT, D_MODEL))
    yg = _sc_gather_rows(ys, pos2d)
    out = _combine_ln2(xp, yg, wtok, ln2_w[l][None, :], ln2_b[l][None, :])
    return out.reshape(B, S, D)
```

```python
import functools

import numpy as np
import jax
import jax.numpy as jnp
from jax import lax
from jax.experimental import pallas as pl
from jax.experimental.pallas import tpu as pltpu
from jax.experimental.pallas import tpu_sc as plsc

F32 = jnp.float32
BF16 = jnp.bfloat16
I32 = jnp.int32
U32 = jnp.uint32

D_MODEL = 1024
POOL_WINDOWS = (2, 4, 8, 16)
POOL_GROUP_DIM = 128
POOL_WIDTH = 512
POOL_HALO = 16
RET_HEADS = 4
RET_QK_DIM = 128
RET_V_DIM = 256
RET_CHUNK = 128
ROPE_BASE = 10000.0
XA_HEADS = 4
XA_HEAD_DIM = 128
XA_WIDTH = 512
N_GROUPS = 4
EXPERTS_PER_GROUP = 8
N_EXPERTS = N_GROUPS * EXPERTS_PER_GROUP
D_EXPERT = 256
LN_EPS = 1e-5
DEPTH = 1
ALPHA = (2.0 * DEPTH) ** 0.25
NEG_BIG = -1e30

COL_POOL, COL_Q, COL_K, COL_V, COL_G, COL_XAQ, COL_GATES = 0, 512, 1024, 1536, 2560, 3584, 4096

VMEM_LIMIT = 56 * 1024 * 1024

TOP_K = 2
PACK_W = D_MODEL // 2
MOE_TM = 512
SC_CHUNK = 128
RANK_CHUNK = 512


def _dot(a, b):
    return jnp.dot(a, b, preferred_element_type=F32)


def _dot_nt(a, b, precision=None):
    return lax.dot_general(a, b, (((1,), (1,)), ((), ())), preferred_element_type=F32, precision=precision)


def _sigmoid(z):
    return 1.0 / (1.0 + jnp.exp(-z))


def _layer_norm(h, w, b):
    mu = jnp.mean(h, axis=-1, keepdims=True)
    hc = h - mu
    var = jnp.mean(hc * hc, axis=-1, keepdims=True)
    return hc * lax.rsqrt(var + LN_EPS) * w + b


def _rope_kernel(pos_ref, freq_ref, cos_ref, sin_ref, cos_t_ref, sin_t_ref):
    ang = freq_ref[...] * pos_ref[...].astype(F32)
    cos_t = jnp.cos(ang)
    sin_t = jnp.sin(ang)
    cos_t_ref[...] = cos_t
    sin_t_ref[...] = sin_t
    cos_ref[...] = jnp.transpose(jnp.concatenate([cos_t, cos_t], axis=0))
    sin_ref[...] = jnp.transpose(jnp.concatenate([-sin_t, sin_t], axis=0))


def _rope_table(pos_row, tile=2048):
    T = pos_row.shape[1]
    half = RET_QK_DIM // 2
    inv_freq = (ROPE_BASE ** (-np.arange(half, dtype=np.float64) / half)).astype(np.float32)
    freq = jnp.asarray(inv_freq[:, None])
    out = pl.BlockSpec((tile, RET_QK_DIM), lambda i: (i, 0))
    out_t = pl.BlockSpec((half, tile), lambda i: (0, i))
    return pl.pallas_call(
        _rope_kernel,
        grid=(T // tile,),
        in_specs=[pl.BlockSpec((1, tile), lambda i: (0, i)), pl.BlockSpec((half, 1), lambda i: (0, 0))],
        out_specs=[out, out, out_t, out_t],
        out_shape=[jax.ShapeDtypeStruct((T, RET_QK_DIM), F32)] * 2 + [jax.ShapeDtypeStruct((half, T), F32)] * 2,
        name="rope_table",
    )(pos_row, freq)


POOL_SUB = 256


def _pool_bands():
    r = np.arange(POOL_SUB)[:, None]
    c = np.arange(POOL_SUB)[None, :]
    ch = np.arange(POOL_HALO)[None, :] - POOL_HALO
    main = np.stack([((r - c >= 0) & (r - c < w)) for w in POOL_WINDOWS]).astype(np.float32)
    halo = np.stack([((r - ch >= 0) & (r - ch < w)) for w in POOL_WINDOWS]).astype(np.float32)
    return jnp.asarray(main, BF16), jnp.asarray(halo, BF16)


def _pool_branch(ub, j, halo_ref, bmain_ref, bhalo_ref, wg_ref, scale_ref, o_ref):
    tile = ub.shape[0]
    s0 = j * tile
    slot = lax.rem(j, 2)
    for sb in range(tile // POOL_SUB):
        r0 = sb * POOL_SUB
        u = ub[r0:r0 + POOL_SUB]
        prev = halo_ref[slot] if sb == 0 else ub[r0 - POOL_HALO:r0]
        pos = s0 + r0 + lax.broadcasted_iota(I32, (POOL_SUB, POOL_GROUP_DIM), 0)
        for g, w in enumerate(POOL_WINDOWS):
            cols = slice(g * POOL_GROUP_DIM, (g + 1) * POOL_GROUP_DIM)
            ug = u[:, cols]
            wsum = _dot(bmain_ref[g], ug) + _dot(bhalo_ref[g], prev[:, cols])
            cnt = jnp.minimum(pos + 1, w).astype(F32)
            pooled = wsum / cnt - ug.astype(F32)
            mixed = _dot(pooled.astype(BF16), wg_ref[g]) * scale_ref[:, cols]
            o_ref[r0:r0 + POOL_SUB, cols] = mixed.astype(BF16)
    halo_ref[1 - slot] = ub[tile - POOL_HALO:tile]


def _ret_consts():
    h = np.arange(RET_HEADS, dtype=np.float64)
    log_gamma = np.log1p(-np.exp2(-5.0 - h))
    pos = np.arange(RET_CHUNK, dtype=np.float64)
    diff = pos[:, None] - pos[None, :]
    kscale = RET_QK_DIM ** -0.5
    dmask = kscale * np.where(diff >= 0, np.exp(log_gamma[:, None, None] * np.maximum(diff, 0.0)), 0.0)
    qdec = np.exp(log_gamma[:, None] * (pos + 1.0)[None, :])
    kdec = kscale * np.exp(log_gamma[:, None] * (RET_CHUNK - 1.0 - pos)[None, :])
    cdec = np.exp(log_gamma * RET_CHUNK)
    lanes = lambda a: np.broadcast_to(a[:, :, None], (RET_HEADS, RET_CHUNK, RET_QK_DIM))
    kdec_t = np.broadcast_to(kdec[:, None, :], (RET_HEADS, RET_QK_DIM, RET_CHUNK))
    return (jnp.asarray(dmask, F32), jnp.asarray(lanes(qdec), F32), jnp.asarray(kdec_t, F32),
            tuple(float(v) for v in cdec))


def _retention_branch(q, k_t, v, silu_g, cos_ref, sin_ref, cos_t_ref, sin_t_ref, dmask_ref, qdec_ref, kdec_ref,
                      gnw_ref, state_ref, rq_ref, rqd_ref, rkt_ref, rkdt_ref, o_ref, cdec, fillers):
    tile = q.shape[0]
    n_chunks = tile // RET_CHUNK
    half = RET_QK_DIM // 2
    cos = cos_ref[...]
    sin = sin_ref[...]
    cos_t = cos_t_ref[...]
    sin_t = sin_t_ref[...]
    for h in range(RET_HEADS):
        qk = slice(h * RET_QK_DIM, (h + 1) * RET_QK_DIM)
        qh = q[:, qk]
        qr = qh * cos + pltpu.roll(qh, half, 1) * sin
        rq_ref[:, qk] = qr.astype(BF16)
        rqd_ref[:, qk] = (qr * jnp.concatenate([qdec_ref[h]] * n_chunks, axis=0)).astype(BF16)
        k1 = k_t[h * RET_QK_DIM:h * RET_QK_DIM + half]
        k2 = k_t[h * RET_QK_DIM + half:(h + 1) * RET_QK_DIM]
        kr_t = jnp.concatenate([k1 * cos_t - k2 * sin_t, k2 * cos_t + k1 * sin_t], axis=0)
        rkt_ref[qk, :] = kr_t.astype(BF16)
        rkdt_ref[qk, :] = (kr_t * jnp.concatenate([kdec_ref[h]] * n_chunks, axis=1)).astype(BF16)

    for c in range(n_chunks):
        rows = slice(c * RET_CHUNK, (c + 1) * RET_CHUNK)
        heads = [slice(h * RET_QK_DIM, (h + 1) * RET_QK_DIM) for h in range(RET_HEADS)]
        raw = [_dot(rq_ref[rows, qk], rkt_ref[qk, rows]) for qk in heads]
        fillers[c]()
        for h in range(RET_HEADS):
            qk = heads[h]
            v_cols = slice(h * RET_V_DIM, (h + 1) * RET_V_DIM)
            vh = v[rows, v_cols]
            scores = raw[h] * dmask_ref[h]
            st = state_ref[h]
            lhs = jnp.concatenate([scores.astype(BF16), rqd_ref[rows, qk]], axis=1)
            y = _dot(lhs, jnp.concatenate([vh, st.astype(BF16)], axis=0))
            state_ref[h] = cdec[h] * st + _dot(rkdt_ref[qk, rows], vh)
            mu = jnp.mean(y, axis=-1, keepdims=True)
            yc = y - mu
            var = jnp.mean(yc * yc, axis=-1, keepdims=True)
            yn = yc * lax.rsqrt(var + LN_EPS) * gnw_ref[:, v_cols]
            o_ref[rows, v_cols] = (silu_g[h][rows] * yn).astype(BF16)


def _memkv_kernel(m_ref, w_ref, o_ref):
    o_ref[...] = _dot(m_ref[...].astype(BF16), w_ref[...]).astype(BF16)


def _mem_kv(memf, w_b):
    M, D = memf.shape
    N = w_b.shape[1]
    return pl.pallas_call(
        _memkv_kernel,
        grid=(1,),
        in_specs=[pl.BlockSpec((M, D), lambda i: (0, 0)), pl.BlockSpec((D, N), lambda i: (0, 0))],
        out_specs=pl.BlockSpec((M, N), lambda i: (0, 0)),
        out_shape=jax.ShapeDtypeStruct((M, N), BF16),
        name="mem_kv",
    )(memf, w_b)


def _cross_attention_branch(xq, k_ref, v_ref, o_ref):
    scale = XA_HEAD_DIM ** -0.5
    for h in range(XA_HEADS):
        cols = slice(h * XA_HEAD_DIM, (h + 1) * XA_HEAD_DIM)
        s = _dot_nt(xq[:, cols], k_ref[:, cols]) * scale
        m = jnp.max(s, axis=-1, keepdims=True)
        p = jnp.exp(s - m)
        l = jnp.sum(p, axis=-1, keepdims=True)
        o = _dot(p.astype(BF16), v_ref[:, cols]) / l
        o_ref[:, cols] = o.astype(BF16)


ROUTER_ROWS = 40
WTOK_LANES = 128
EXP_ROW0 = 8


def _route(logits_t):
    gl = logits_t[0:8]
    gmax = jnp.max(gl, axis=0, keepdims=True)
    p_grp = 1.0 / jnp.sum(jnp.exp(gl - gmax), axis=0, keepdims=True)
    idx8 = lax.broadcasted_iota(jnp.int32, gl.shape, 0)
    gsel = jnp.min(jnp.where(gl == gmax, idx8, 8), axis=0, keepdims=True)
    cl = jnp.zeros_like(gl)
    for g in range(N_GROUPS):
        r0 = EXP_ROW0 + g * EXPERTS_PER_GROUP
        cl = cl + jnp.where(gsel == g, logits_t[r0:r0 + EXPERTS_PER_GROUP], 0.0)
    v1 = jnp.max(cl, axis=0, keepdims=True)
    i1 = jnp.min(jnp.where(cl == v1, idx8, 8), axis=0, keepdims=True)
    cl2 = jnp.where(idx8 == i1, -jnp.inf, cl)
    v2 = jnp.max(cl2, axis=0, keepdims=True)
    i2 = jnp.min(jnp.where(cl2 == v2, idx8, 8), axis=0, keepdims=True)
    e21 = jnp.exp(v2 - v1)
    w1 = p_grp / (1.0 + e21)
    w2 = p_grp * e21 / (1.0 + e21)
    return gsel * EXPERTS_PER_GROUP + i1, gsel * EXPERTS_PER_GROUP + i2, w1, w2


def _pack_halves(v):
    half = v.shape[1] // 2
    lo = lax.bitcast_convert_type(v[:, :half].astype(BF16).astype(F32), U32)
    hi = lax.bitcast_convert_type(v[:, half:].astype(BF16).astype(F32), U32)
    return lax.bitcast_convert_type(lax.shift_right_logical(lo, U32(16)) | hi, I32)


def _unpack_halves(w):
    u = lax.bitcast_convert_type(w, U32)
    lo = lax.bitcast_convert_type(lax.shift_left(u, U32(16)), F32)
    hi = lax.bitcast_convert_type(u & U32(0xFFFF0000), F32)
    return lo, hi


def _mixer_kernel(x_ref, cos_ref, sin_ref, cos_t_ref, sin_t_ref, km_ref, vm_ref, win_ref, wgrp_ref,
                  pscale_ref, bmain_ref, bhalo_ref, dmask_ref, qdec_ref, kdec_ref, gnw_ref, wp_ref, wr_ref,
                  wa_ref, wo_ref, lnw_ref, lnb_ref, rwh_ref, rwl_ref, rb_ref, xp_ref, eid_ref, wtok_ref,
                  state_ref, halo_ref, ypool_ref, yret_ref, yxa_ref, rq_ref, rqd_ref, rkt_ref, rkdt_ref, wkt_ref,
                  *, tile, cdec):
    j = pl.program_id(1)

    @pl.when(jnp.logical_and(pl.program_id(0) == 0, j == 0))
    def _():
        wk = win_ref[:, COL_K:COL_K + RET_HEADS * RET_QK_DIM].astype(F32)
        wkt_ref[...] = jnp.transpose(wk).astype(BF16)

    @pl.when(j == 0)
    def _():
        state_ref[...] = jnp.zeros_like(state_ref)
        halo_ref[...] = jnp.zeros_like(halo_ref)

    x = x_ref[...]
    xb = x.astype(BF16)

    def proj(col, width):
        return _dot(xb, win_ref[:, col:col + width])

    part = {}

    def pool_part():
        _pool_branch(proj(COL_POOL, POOL_WIDTH).astype(BF16), j, halo_ref, bmain_ref, bhalo_ref, wgrp_ref,
                     pscale_ref, ypool_ref)
        part["pool"] = _sigmoid(proj(COL_GATES, D_MODEL)) * _dot(ypool_ref[...], wp_ref[...])

    def xa_part():
        _cross_attention_branch(proj(COL_XAQ, XA_WIDTH).astype(BF16), km_ref, vm_ref, yxa_ref)
        part["xa"] = _sigmoid(proj(COL_GATES + 2 * D_MODEL, D_MODEL)) * _dot(yxa_ref[...], wa_ref[...])

    def ret_gate_part():
        part["ret_gate"] = _sigmoid(proj(COL_GATES + D_MODEL, D_MODEL))

    silu_g = []
    for h in range(RET_HEADS):
        gh = proj(COL_G + h * RET_V_DIM, RET_V_DIM)
        silu_g.append(gh * _sigmoid(gh))
    fillers = [pool_part, xa_part, ret_gate_part] + [lambda: None] * (tile // RET_CHUNK - 3)
    _retention_branch(proj(COL_Q, RET_HEADS * RET_QK_DIM), _dot_nt(wkt_ref[...], xb),
                      proj(COL_V, RET_HEADS * RET_V_DIM).astype(BF16), silu_g, cos_ref, sin_ref, cos_t_ref,
                      sin_t_ref, dmask_ref, qdec_ref, kdec_ref, gnw_ref, state_ref, rq_ref, rqd_ref, rkt_ref,
                      rkdt_ref, yret_ref, cdec, fillers)
    merged = part["pool"] + part["ret_gate"] * _dot(yret_ref[...], wr_ref[...]) + part["xa"]

    h = ALPHA * x + _dot(merged.astype(BF16), wo_ref[...])
    x1 = _layer_norm(h, lnw_ref[...], lnb_ref[...])
    xp_ref[...] = _pack_halves(x1)
    x1_hi = x1.astype(BF16)
    x1_lo = (x1 - x1_hi.astype(F32)).astype(BF16)
    logits_t = (_dot_nt(rwh_ref[...], x1_hi) + _dot_nt(rwh_ref[...], x1_lo) + _dot_nt(rwl_ref[...], x1_hi)
                + rb_ref[...])
    e0, e1, w0, w1 = _route(logits_t)
    tile = e0.shape[1]
    eid_ref[...] = jnp.concatenate([e0, e1, jnp.zeros((8 - TOP_K, tile), I32)], axis=0)
    w_t = jnp.concatenate([w0, w1, jnp.zeros((WTOK_LANES - TOP_K, tile), F32)], axis=0)
    wtok_ref[...] = jnp.transpose(w_t)


def _mixer(xf, rope, kv, win, wgrp, pscale, gnw, wp, wr, wa, wo, lnw, lnb, rw, rb, batch, seq, mem_len,
           tile=512):
    T = xf.shape[0]
    nj = seq // tile
    cos, sin, cos_t, sin_t = rope
    bmain, bhalo = _pool_bands()
    dmask, qdec, kdec, cdec = _ret_consts()
    resident = lambda a: pl.BlockSpec(a.shape, lambda b, j: (0,) * a.ndim, pipeline_mode=pl.Buffered(1))
    rowblk = lambda w: pl.BlockSpec((tile, w), lambda b, j: (b * nj + j, 0))
    colblk = lambda r: pl.BlockSpec((r, tile), lambda b, j: (0, b * nj + j))
    consts = (win, wgrp, pscale, bmain, bhalo, dmask, qdec, kdec, gnw, wp, wr, wa, wo, lnw, lnb, *rw, rb)
    return pl.pallas_call(
        functools.partial(_mixer_kernel, tile=tile, cdec=cdec),
        grid=(batch, nj),
        in_specs=[rowblk(D_MODEL), rowblk(RET_QK_DIM), rowblk(RET_QK_DIM),
                  colblk(RET_QK_DIM // 2), colblk(RET_QK_DIM // 2),
                  pl.BlockSpec((mem_len, XA_WIDTH), lambda b, j: (b, 0)),
                  pl.BlockSpec((mem_len, XA_WIDTH), lambda b, j: (b, 1))] + [resident(a) for a in consts],
        out_specs=[rowblk(PACK_W), colblk(8), rowblk(WTOK_LANES)],
        out_shape=[jax.ShapeDtypeStruct((T, PACK_W), I32), jax.ShapeDtypeStruct((8, T), I32),
                   jax.ShapeDtypeStruct((T, WTOK_LANES), F32)],
        scratch_shapes=[pltpu.VMEM((RET_HEADS, RET_QK_DIM, RET_V_DIM), F32),
                        pltpu.VMEM((2, POOL_HALO, POOL_WIDTH), BF16),
                        pltpu.VMEM((tile, POOL_WIDTH), BF16),
                        pltpu.VMEM((tile, RET_HEADS * RET_V_DIM), BF16),
                        pltpu.VMEM((tile, XA_WIDTH), BF16),
                        pltpu.VMEM((tile, RET_HEADS * RET_QK_DIM), BF16),
                        pltpu.VMEM((tile, RET_HEADS * RET_QK_DIM), BF16),
                        pltpu.VMEM((RET_HEADS * RET_QK_DIM, tile), BF16),
                        pltpu.VMEM((RET_HEADS * RET_QK_DIM, tile), BF16),
                        pltpu.VMEM((RET_HEADS * RET_QK_DIM, D_MODEL), BF16)],
        compiler_params=pltpu.CompilerParams(dimension_semantics=("arbitrary", "arbitrary"),
                                             vmem_limit_bytes=VMEM_LIMIT),
        name="mixer",
    )(xf, cos, sin, cos_t, sin_t, kv, kv, *consts)


META_LANES = 128


def _positions_kernel(eid_ref, tri_ref, low_ref, pos_ref, first_tile_ref, n_tiles_ref, *, n_tok):
    n_chunks = n_tok // RANK_CHUNK
    erow = lax.broadcasted_iota(I32, (N_EXPERTS, RANK_CHUNK), 0)

    def onehot(c):
        sl = slice(c * RANK_CHUNK, (c + 1) * RANK_CHUNK)
        m0 = eid_ref[0:1, sl] == erow
        m1 = eid_ref[1:2, sl] == erow
        return m0, m1, jnp.where(m0, 1.0, 0.0) + jnp.where(m1, 1.0, 0.0)

    counts = jnp.zeros((N_EXPERTS, 1), F32)
    for c in range(n_chunks):
        counts = counts + jnp.sum(onehot(c)[2], axis=1, keepdims=True)
    ptiles = jnp.floor((counts + (MOE_TM - 1)) * (1.0 / MOE_TM))
    ptiles_b = jnp.broadcast_to(ptiles, (N_EXPERTS, 128)).astype(BF16)
    start = _dot(low_ref[...], ptiles_b)[:, 0:1] * MOE_TM

    pos_ref[...] = jnp.zeros_like(pos_ref)
    carry = start - 1.0
    for c in range(n_chunks):
        sl = slice(c * RANK_CHUNK, (c + 1) * RANK_CHUNK)
        m0, m1, oh = onehot(c)
        rank = _dot(oh.astype(BF16), tri_ref[...]) + carry
        pos_ref[0:1, sl] = jnp.sum(jnp.where(m0, rank, 0.0), axis=0, keepdims=True).astype(I32)
        pos_ref[1:2, sl] = jnp.sum(jnp.where(m1, rank, 0.0), axis=0, keepdims=True).astype(I32)
        carry = carry + jnp.sum(oh, axis=1, keepdims=True)

    first_tile_ref[...] = jnp.broadcast_to(start * (1.0 / MOE_TM), first_tile_ref.shape).astype(I32)
    n_tiles_ref[...] = jnp.broadcast_to(ptiles, n_tiles_ref.shape).astype(I32)


def _positions(eid):
    T = eid.shape[1]
    r = np.arange(RANK_CHUNK)
    tri = jnp.asarray(r[:, None] <= r[None, :], BF16)
    e = np.arange(N_EXPERTS)
    low = jnp.asarray(e[None, :] < e[:, None], BF16)
    full = lambda a: pl.BlockSpec(a.shape, lambda i: (0,) * a.ndim)
    return pl.pallas_call(
        functools.partial(_positions_kernel, n_tok=T),
        grid=(1,),
        in_specs=[full(eid), full(tri), full(low)],
        out_specs=[pl.BlockSpec((8, T), lambda i: (0, 0)),
                   pl.BlockSpec((N_EXPERTS, META_LANES), lambda i: (0, 0)),
                   pl.BlockSpec((N_EXPERTS, META_LANES), lambda i: (0, 0))],
        out_shape=[jax.ShapeDtypeStruct((8, T), I32), jax.ShapeDtypeStruct((N_EXPERTS, META_LANES), I32),
                   jax.ShapeDtypeStruct((N_EXPERTS, META_LANES), I32)],
        name="route_positions",
    )(eid, tri, low)


def _sc_workers():
    info = plsc.get_sparse_core_info()
    return info.num_cores, info.num_cores * info.num_subcores


def _sc_scatter_rows(xp, pos2d, n_out):
    T, W = xp.shape
    n_cores, n_workers = _sc_workers()
    cpw = T // SC_CHUNK // n_workers
    mesh = plsc.VectorSubcoreMesh(core_axis_name="c", subcore_axis_name="s")

    @functools.partial(
        pl.kernel, mesh=mesh, out_type=jax.ShapeDtypeStruct((n_out, W), I32),
        scratch_types=[pltpu.VMEM((TOP_K * cpw, SC_CHUNK), I32), pltpu.VMEM((SC_CHUNK, W), I32)],
        name="sc_scatter_rows")
    def k(x_hbm, pos_hbm, out_hbm, idx_v, rows_v):
        wid = lax.axis_index("s") * n_cores + lax.axis_index("c")
        for s in range(TOP_K):
            pltpu.sync_copy(pos_hbm.at[pl.ds(s * (T // SC_CHUNK) + wid * cpw, cpw)],
                            idx_v.at[pl.ds(s * cpw, cpw)])
        for j in range(cpw):
            pltpu.sync_copy(x_hbm.at[pl.ds((wid * cpw + j) * SC_CHUNK, SC_CHUNK)], rows_v)
            for s in range(TOP_K):
                pltpu.sync_copy(rows_v, out_hbm.at[idx_v.at[s * cpw + j]])

    return k(xp, pos2d)


def _sc_gather_rows(y, idx2d):
    W = y.shape[1]
    n = idx2d.shape[0] * SC_CHUNK
    n_cores, n_workers = _sc_workers()
    cpw = n // SC_CHUNK // n_workers
    mesh = plsc.VectorSubcoreMesh(core_axis_name="c", subcore_axis_name="s")

    @functools.partial(
        pl.kernel, mesh=mesh, out_type=jax.ShapeDtypeStruct((n, W), I32),
        scratch_types=[pltpu.VMEM((cpw, SC_CHUNK), I32), pltpu.VMEM((SC_CHUNK, W), I32)],
        name="sc_gather_rows")
    def k(y_hbm, idx_hbm, out_hbm, idx_v, rows_v):
        wid = lax.axis_index("s") * n_cores + lax.axis_index("c")
        pltpu.sync_copy(idx_hbm.at[pl.ds(wid * cpw, cpw)], idx_v)
        for j in range(cpw):
            pltpu.sync_copy(y_hbm.at[idx_v.at[j]], rows_v)
            pltpu.sync_copy(rows_v, out_hbm.at[pl.ds((wid * cpw + j) * SC_CHUNK, SC_CHUNK)])

    return k(y, idx2d)


def _routed_kernel(first_ref, count_ref, xs_hbm, wg_ref, wu_ref, wd_ref, ys_hbm, xbuf, ybuf, in_sem, out_sem):
    e = pl.program_id(0)
    last = pl.num_programs(0) - 1
    total = first_ref[last] + count_ref[last]

    def in_copy(g, slot):
        return pltpu.make_async_copy(xs_hbm.at[pl.ds(g * MOE_TM, MOE_TM)], xbuf.at[slot], in_sem.at[slot])

    def out_copy(g, slot):
        return pltpu.make_async_copy(ybuf.at[slot], ys_hbm.at[pl.ds(g * MOE_TM, MOE_TM)], out_sem.at[slot])

    @pl.when(e == 0)
    def _():
        in_copy(0, 0).start()

    wg = wg_ref[0].astype(BF16)
    wu = wu_ref[0].astype(BF16)
    wd = wd_ref[0].astype(BF16)

    def tile_step(i, carry):
        g = first_ref[e] + i
        slot = lax.rem(g, 2)
        in_copy(g, slot).wait()

        @pl.when(g + 1 < total)
        def _():
            in_copy(g + 1, 1 - slot).start()

        @pl.when(g >= 2)
        def _():
            out_copy(g - 2, slot).wait()

        lo, hi = _unpack_halves(xbuf[slot])
        lo = lo.astype(BF16)
        hi = hi.astype(BF16)
        a = _dot(lo, wg[:PACK_W]) + _dot(hi, wg[PACK_W:])
        b = _dot(lo, wu[:PACK_W]) + _dot(hi, wu[PACK_W:])
        act = (a * _sigmoid(a) * b).astype(BF16)
        ybuf[slot] = _pack_halves(_dot(act, wd))
        out_copy(g, slot).start()
        return carry

    lax.fori_loop(0, count_ref[e], tile_step, 0)

    @pl.when(e == last)
    def _():
        @pl.when(total >= 2)
        def _():
            out_copy(total - 2, lax.rem(total, 2)).wait()

        out_copy(total - 1, lax.rem(total - 1, 2)).wait()


def _routed_mlp(first_tile, n_tiles, xs, wg, wu, wd):
    R = xs.shape[0]
    any_space = pl.BlockSpec(memory_space=pl.ANY)
    return pl.pallas_call(
        _routed_kernel,
        grid_spec=pltpu.PrefetchScalarGridSpec(
            num_scalar_prefetch=2,
            grid=(N_EXPERTS,),
            in_specs=[any_space,
                      pl.BlockSpec((1, D_MODEL, D_EXPERT), lambda e, ft, nt: (e, 0, 0)),
                      pl.BlockSpec((1, D_MODEL, D_EXPERT), lambda e, ft, nt: (e, 0, 0)),
                      pl.BlockSpec((1, D_EXPERT, D_MODEL), lambda e, ft, nt: (e, 0, 0))],
            out_specs=any_space,
            scratch_shapes=[pltpu.VMEM((2, MOE_TM, PACK_W), I32), pltpu.VMEM((2, MOE_TM, PACK_W), I32),
                            pltpu.SemaphoreType.DMA((2,)), pltpu.SemaphoreType.DMA((2,))]),
        out_shape=jax.ShapeDtypeStruct((R, PACK_W), I32),
        compiler_params=pltpu.CompilerParams(dimension_semantics=("arbitrary",)),
        name="routed_mlp",
    )(first_tile, n_tiles, xs, wg, wu, wd)


def _combine_kernel(xp_ref, y0_ref, y1_ref, wtok_ref, lnw_ref, lnb_ref, o_ref):
    w0 = wtok_ref[:, 0:1]
    w1 = wtok_ref[:, 1:2]
    xlo, xhi = _unpack_halves(xp_ref[...])
    y0lo, y0hi = _unpack_halves(y0_ref[...])
    y1lo, y1hi = _unpack_halves(y1_ref[...])
    h = jnp.concatenate([ALPHA * xlo + (w0 * y0lo + w1 * y1lo), ALPHA * xhi + (w0 * y0hi + w1 * y1hi)], axis=1)
    o_ref[...] = _layer_norm(h, lnw_ref[...], lnb_ref[...])


def _combine_ln2(xp, yg, wtok, lnw, lnb, tile=512):
    T = xp.shape[0]
    nt = T // tile
    full = lambda a: pl.BlockSpec(a.shape, lambda i: (0,) * a.ndim)
    return pl.pallas_call(
        _combine_kernel,
        grid=(nt,),
        in_specs=[pl.BlockSpec((tile, PACK_W), lambda i: (i, 0)),
                  pl.BlockSpec((tile, PACK_W), lambda i: (i, 0)),
                  pl.BlockSpec((tile, PACK_W), lambda i: (i + nt, 0)),
                  pl.BlockSpec((tile, WTOK_LANES), lambda i: (i, 0)),
                  full(lnw), full(lnb)],
        out_specs=pl.BlockSpec((tile, D_MODEL), lambda i: (i, 0)),
        out_shape=jax.ShapeDtypeStruct((T, D_MODEL), F32),
        name="combine_ln2",
    )(xp, yg, yg, wtok, lnw, lnb)


def _router_params(w_grp, b_grp, w_exp, b_exp):
    rw = jnp.zeros((ROUTER_ROWS, D_MODEL), F32)
    rw = rw.at[0:N_GROUPS].set(w_grp.T).at[EXP_ROW0:EXP_ROW0 + N_EXPERTS].set(w_exp.T)
    rb = jnp.zeros((ROUTER_ROWS,), F32).at[N_GROUPS:8].set(NEG_BIG)
    rb = rb.at[0:N_GROUPS].set(b_grp).at[EXP_ROW0:EXP_ROW0 + N_EXPERTS].set(b_exp)
    rw_hi = rw.astype(BF16)
    rw_lo = (rw - rw_hi.astype(F32)).astype(BF16)
    return (rw_hi, rw_lo), rb[:, None]


def kernel(x, mem, positions, w_in, w_pool_grp, pool_scale, ret_gn_w, w_mem_kv, w_br_pool, w_br_ret, w_br_xa,
           w_out, ln1_w, ln1_b, w_grp_router, b_grp_router, w_exp_router, b_exp_router, w_exp_gate, w_exp_up,
           w_exp_down, ln2_w, ln2_b):
    B, S, D = x.shape
    assert D == D_MODEL and w_in.shape[0] == DEPTH and S % 512 == 0
    T = B * S
    M = mem.shape[1]
    l = 0
    xf = x.reshape(T, D)

    rope = _rope_table(positions.reshape(1, T))
    kv = _mem_kv(mem.reshape(B * M, D), w_mem_kv[l].astype(BF16))
    rw, rb = _router_params(w_grp_router[l], b_grp_router[l], w_exp_router[l], b_exp_router[l])
    xp, eid, wtok = _mixer(xf, rope, kv, w_in[l].astype(BF16),
                           w_pool_grp[l].astype(BF16), pool_scale[l][None, :], ret_gn_w[l].reshape(1, -1),
                           w_br_pool[l].astype(BF16), w_br_ret[l].astype(BF16), w_br_xa[l].astype(BF16),
                           w_out[l].astype(BF16), ln1_w[l][None, :], ln1_b[l][None, :], rw, rb, B, S, M)

    pos, first_tile, n_tiles = _positions(eid)
    pos2d = pos[0:TOP_K].reshape(TOP_K * T // SC_CHUNK, SC_CHUNK)
    max_tiles = (TOP_K * T + N_EXPERTS * (MOE_TM - 1)) // MOE_TM
    xs = _sc_scatter_rows(xp, pos2d, max_tiles * MOE_TM)
    ys = _routed_mlp(first_tile[:, 0], n_tiles[:, 0], xs,
                     w_exp_gate[l].reshape(N_EXPERTS, D_MODEL, D_EXPERT),
                     w_exp_up[l].reshape(N_EXPERTS, D_MODEL, D_EXPERT),
                     w_exp_down[l].reshape(N_EXPERTS, D_EXPERT, D_MODEL))
    yg = _sc_gather_rows(ys, pos2d)
    out = _combine_ln2(xp, yg, wtok, ln2_w[l][None, :], ln2_b[l][None, :])
    return out.reshape(B, S, D)
```

```python
import functools

import numpy as np
import jax
import jax.numpy as jnp
from jax import lax
from jax.experimental import pallas as pl
from jax.experimental.pallas import tpu as pltpu
from jax.experimental.pallas import tpu_sc as plsc

F32 = jnp.float32
BF16 = jnp.bfloat16
I32 = jnp.int32
U32 = jnp.uint32

D_MODEL = 1024
POOL_WINDOWS = (2, 4, 8, 16)
POOL_GROUP_DIM = 128
POOL_WIDTH = 512
POOL_HALO = 16
RET_HEADS = 4
RET_QK_DIM = 128
RET_V_DIM = 256
RET_CHUNK = 128
ROPE_BASE = 10000.0
XA_HEADS = 4
XA_HEAD_DIM = 128
XA_WIDTH = 512
N_GROUPS = 4
EXPERTS_PER_GROUP = 8
N_EXPERTS = N_GROUPS * EXPERTS_PER_GROUP
D_EXPERT = 256
LN_EPS = 1e-5
DEPTH = 1
ALPHA = (2.0 * DEPTH) ** 0.25
NEG_BIG = -1e30

COL_POOL, COL_Q, COL_K, COL_V, COL_G, COL_XAQ, COL_GATES = 0, 512, 1024, 1536, 2560, 3584, 4096

VMEM_LIMIT = 56 * 1024 * 1024

TOP_K = 2
PACK_W = D_MODEL // 2
MOE_TM = 512
SC_CHUNK = 128
RANK_CHUNK = 512
STRIP = 256
LN_ROWS = 32


def _dot(a, b):
    return jnp.dot(a, b, preferred_element_type=F32)


def _dot_nt(a, b, precision=None):
    return lax.dot_general(a, b, (((1,), (1,)), ((), ())), preferred_element_type=F32, precision=precision)


def _sigmoid(z):
    return 1.0 / (1.0 + jnp.exp2(z * (-1.0 / np.log(2.0))))


def _layer_norm(h, w, b):
    mu = jnp.mean(h, axis=-1, keepdims=True)
    hc = h - mu
    var = jnp.mean(hc * hc, axis=-1, keepdims=True)
    return hc * lax.rsqrt(var + LN_EPS) * w + b


def _rope_kernel(pos_ref, freq_ref, cos_ref, sin_ref, cos_t_ref, sin_t_ref):
    ang = freq_ref[...] * pos_ref[...].astype(F32)
    cos_t = jnp.cos(ang)
    sin_t = jnp.sin(ang)
    cos_t_ref[...] = cos_t
    sin_t_ref[...] = sin_t
    cos_ref[...] = jnp.transpose(jnp.concatenate([cos_t, cos_t], axis=0))
    sin_ref[...] = jnp.transpose(jnp.concatenate([-sin_t, sin_t], axis=0))


def _rope_table(pos_row, tile=2048):
    T = pos_row.shape[1]
    half = RET_QK_DIM // 2
    inv_freq = (ROPE_BASE ** (-np.arange(half, dtype=np.float64) / half)).astype(np.float32)
    freq = jnp.asarray(inv_freq[:, None])
    out = pl.BlockSpec((tile, RET_QK_DIM), lambda i: (i, 0))
    out_t = pl.BlockSpec((half, tile), lambda i: (0, i))
    return pl.pallas_call(
        _rope_kernel,
        grid=(T // tile,),
        in_specs=[pl.BlockSpec((1, tile), lambda i: (0, i)), pl.BlockSpec((half, 1), lambda i: (0, 0))],
        out_specs=[out, out, out_t, out_t],
        out_shape=[jax.ShapeDtypeStruct((T, RET_QK_DIM), F32)] * 2 + [jax.ShapeDtypeStruct((half, T), F32)] * 2,
        name="rope_table",
    )(pos_row, freq)


POOL_SUB = 256


def _pool_bands():
    r = np.arange(POOL_SUB)[:, None]
    c = np.arange(POOL_SUB)[None, :]
    ch = np.arange(POOL_HALO)[None, :] - POOL_HALO
    main = np.stack([((r - c >= 0) & (r - c < w)) for w in POOL_WINDOWS]).astype(np.float32)
    halo = np.stack([((r - ch >= 0) & (r - ch < w)) for w in POOL_WINDOWS]).astype(np.float32)
    return jnp.asarray(main, BF16), jnp.asarray(halo, BF16)


def _pool_branch(ub, j, halo_ref, bmain_ref, bhalo_ref, wg_ref, scale_ref, o_ref):
    tile = ub.shape[0]
    s0 = j * tile
    slot = lax.rem(j, 2)
    for sb in range(tile // POOL_SUB):
        r0 = sb * POOL_SUB
        u = ub[r0:r0 + POOL_SUB]
        prev = halo_ref[slot] if sb == 0 else ub[r0 - POOL_HALO:r0]
        pos = s0 + r0 + lax.broadcasted_iota(I32, (POOL_SUB, POOL_GROUP_DIM), 0)
        for g, w in enumerate(POOL_WINDOWS):
            cols = slice(g * POOL_GROUP_DIM, (g + 1) * POOL_GROUP_DIM)
            ug = u[:, cols]
            wsum = _dot(bmain_ref[g], ug) + _dot(bhalo_ref[g], prev[:, cols])
            cnt = jnp.minimum(pos + 1, w).astype(F32)
            pooled = wsum / cnt - ug.astype(F32)
            mixed = _dot(pooled.astype(BF16), wg_ref[g]) * scale_ref[:, cols]
            o_ref[r0:r0 + POOL_SUB, cols] = mixed.astype(BF16)
    halo_ref[1 - slot] = ub[tile - POOL_HALO:tile]


def _ret_consts():
    h = np.arange(RET_HEADS, dtype=np.float64)
    log_gamma = np.log1p(-np.exp2(-5.0 - h))
    pos = np.arange(RET_CHUNK, dtype=np.float64)
    diff = pos[:, None] - pos[None, :]
    kscale = RET_QK_DIM ** -0.5
    dmask = kscale * np.where(diff >= 0, np.exp(log_gamma[:, None, None] * np.maximum(diff, 0.0)), 0.0)
    qdec = np.exp(log_gamma[:, None] * (pos + 1.0)[None, :])
    kdec = kscale * np.exp(log_gamma[:, None] * (RET_CHUNK - 1.0 - pos)[None, :])
    cdec = np.exp(log_gamma * RET_CHUNK)
    lanes = lambda a: np.broadcast_to(a[:, :, None], (RET_HEADS, RET_CHUNK, RET_QK_DIM))
    kdec_t = np.broadcast_to(kdec[:, None, :], (RET_HEADS, RET_QK_DIM, RET_CHUNK))
    return (jnp.asarray(dmask, F32), jnp.asarray(lanes(qdec), F32), jnp.asarray(kdec_t, F32),
            tuple(float(v) for v in cdec))


def _retention_branch(q, k_t, v, silu_g, cos_ref, sin_ref, cos_t_ref, sin_t_ref, dmask_ref, qdec_ref, kdec_ref,
                      gnw_ref, state_ref, rq_ref, rqd_ref, rkt_ref, rkdt_ref, o_ref, cdec, fillers):
    tile = q.shape[0]
    n_chunks = tile // RET_CHUNK
    half = RET_QK_DIM // 2
    cos = cos_ref[...]
    sin = sin_ref[...]
    cos_t = cos_t_ref[...]
    sin_t = sin_t_ref[...]
    for h in range(RET_HEADS):
        qk = slice(h * RET_QK_DIM, (h + 1) * RET_QK_DIM)
        qh = q[:, qk]
        qr = qh * cos + pltpu.roll(qh, half, 1) * sin
        rq_ref[:, qk] = qr.astype(BF16)
        rqd_ref[:, qk] = (qr * jnp.concatenate([qdec_ref[h]] * n_chunks, axis=0)).astype(BF16)
        k1 = k_t[h * RET_QK_DIM:h * RET_QK_DIM + half]
        k2 = k_t[h * RET_QK_DIM + half:(h + 1) * RET_QK_DIM]
        kr_t = jnp.concatenate([k1 * cos_t - k2 * sin_t, k2 * cos_t + k1 * sin_t], axis=0)
        rkt_ref[qk, :] = kr_t.astype(BF16)
        rkdt_ref[qk, :] = (kr_t * jnp.concatenate([kdec_ref[h]] * n_chunks, axis=1)).astype(BF16)

    for c in range(n_chunks):
        rows = slice(c * RET_CHUNK, (c + 1) * RET_CHUNK)
        heads = [slice(h * RET_QK_DIM, (h + 1) * RET_QK_DIM) for h in range(RET_HEADS)]
        raw = [_dot(rq_ref[rows, qk], rkt_ref[qk, rows]) for qk in heads]
        fillers[c]()
        for h in range(RET_HEADS):
            qk = heads[h]
            v_cols = slice(h * RET_V_DIM, (h + 1) * RET_V_DIM)
            vh = v[rows, v_cols]
            scores = raw[h] * dmask_ref[h]
            st = state_ref[h]
            lhs = jnp.concatenate([scores.astype(BF16), rqd_ref[rows, qk]], axis=1)
            y = _dot(lhs, jnp.concatenate([vh, st.astype(BF16)], axis=0))
            state_ref[h] = cdec[h] * st + _dot(rkdt_ref[qk, rows], vh)
            mu = jnp.mean(y, axis=-1, keepdims=True)
            yc = y - mu
            var = jnp.mean(yc * yc, axis=-1, keepdims=True)
            yn = yc * lax.rsqrt(var + LN_EPS) * gnw_ref[:, v_cols]
            o_ref[rows, v_cols] = (silu_g[h][rows] * yn).astype(BF16)


def _memkv_kernel(m_ref, w_ref, o_ref):
    o_ref[...] = _dot(m_ref[...].astype(BF16), w_ref[...]).astype(BF16)


def _mem_kv(memf, w_b):
    M, D = memf.shape
    N = w_b.shape[1]
    return pl.pallas_call(
        _memkv_kernel,
        grid=(1,),
        in_specs=[pl.BlockSpec((M, D), lambda i: (0, 0)), pl.BlockSpec((D, N), lambda i: (0, 0))],
        out_specs=pl.BlockSpec((M, N), lambda i: (0, 0)),
        out_shape=jax.ShapeDtypeStruct((M, N), BF16),
        name="mem_kv",
    )(memf, w_b)


def _cross_attention_branch(xq, k_ref, v_ref, o_ref):
    scale = XA_HEAD_DIM ** -0.5
    for h in range(XA_HEADS):
        cols = slice(h * XA_HEAD_DIM, (h + 1) * XA_HEAD_DIM)
        s = _dot_nt(xq[:, cols], k_ref[:, cols]) * scale
        m = jnp.max(s, axis=-1, keepdims=True)
        p = jnp.exp(s - m)
        l = jnp.sum(p, axis=-1, keepdims=True)
        o = _dot(p.astype(BF16), v_ref[:, cols]) / l
        o_ref[:, cols] = o.astype(BF16)


ROUTER_ROWS = 40
WTOK_LANES = 128
EXP_ROW0 = 8


def _route(logits_t):
    gl = logits_t[0:8]
    gmax = jnp.max(gl, axis=0, keepdims=True)
    p_grp = 1.0 / jnp.sum(jnp.exp(gl - gmax), axis=0, keepdims=True)
    idx8 = lax.broadcasted_iota(jnp.int32, gl.shape, 0)
    gsel = jnp.min(jnp.where(gl == gmax, idx8, 8), axis=0, keepdims=True)
    cl = jnp.zeros_like(gl)
    for g in range(N_GROUPS):
        r0 = EXP_ROW0 + g * EXPERTS_PER_GROUP
        cl = cl + jnp.where(gsel == g, logits_t[r0:r0 + EXPERTS_PER_GROUP], 0.0)
    v1 = jnp.max(cl, axis=0, keepdims=True)
    i1 = jnp.min(jnp.where(cl == v1, idx8, 8), axis=0, keepdims=True)
    cl2 = jnp.where(idx8 == i1, -jnp.inf, cl)
    v2 = jnp.max(cl2, axis=0, keepdims=True)
    i2 = jnp.min(jnp.where(cl2 == v2, idx8, 8), axis=0, keepdims=True)
    e21 = jnp.exp(v2 - v1)
    w1 = p_grp / (1.0 + e21)
    w2 = p_grp * e21 / (1.0 + e21)
    return gsel * EXPERTS_PER_GROUP + i1, gsel * EXPERTS_PER_GROUP + i2, w1, w2


def _pack_halves(v):
    half = v.shape[1] // 2
    lo = lax.bitcast_convert_type(v[:, :half].astype(BF16).astype(F32), U32)
    hi = lax.bitcast_convert_type(v[:, half:].astype(BF16).astype(F32), U32)
    return lax.bitcast_convert_type(lax.shift_right_logical(lo, U32(16)) | hi, I32)


def _unpack_halves(w):
    u = lax.bitcast_convert_type(w, U32)
    lo = lax.bitcast_convert_type(lax.shift_left(u, U32(16)), F32)
    hi = lax.bitcast_convert_type(u & U32(0xFFFF0000), F32)
    return lo, hi


def _mixer_kernel(x_ref, cos_ref, sin_ref, cos_t_ref, sin_t_ref, km_ref, vm_ref, win_ref, wgrp_ref,
                  pscale_ref, bmain_ref, bhalo_ref, dmask_ref, qdec_ref, kdec_ref, gnw_ref, wp_ref, wr_ref,
                  wa_ref, wo_ref, lnw_ref, lnb_ref, rwh_ref, rwl_ref, rb_ref, xp_ref, eid_ref, wtok_ref,
                  state_ref, halo_ref, ypool_ref, yret_ref, yxa_ref, rq_ref, rqd_ref, rkt_ref, rkdt_ref, wkt_ref,
                  *, tile, cdec):
    j = pl.program_id(1)

    @pl.when(jnp.logical_and(pl.program_id(0) == 0, j == 0))
    def _():
        wk = win_ref[:, COL_K:COL_K + RET_HEADS * RET_QK_DIM].astype(F32)
        wkt_ref[...] = jnp.transpose(wk).astype(BF16)

    @pl.when(j == 0)
    def _():
        state_ref[...] = jnp.zeros_like(state_ref)
        halo_ref[...] = jnp.zeros_like(halo_ref)

    x = x_ref[...]
    xb = x.astype(BF16)

    def proj(col, width):
        return _dot(xb, win_ref[:, col:col + width])

    part = {}
    strips = [slice(c, c + STRIP) for c in range(0, D_MODEL, STRIP)]

    def gate(branch, cols):
        return _sigmoid(proj(COL_GATES + branch * D_MODEL + cols.start, STRIP))

    def pool_part():
        _pool_branch(proj(COL_POOL, POOL_WIDTH).astype(BF16), j, halo_ref, bmain_ref, bhalo_ref, wgrp_ref,
                     pscale_ref, ypool_ref)
        part["pool"] = [gate(0, c) * _dot(ypool_ref[...], wp_ref[:, c]) for c in strips]

    def xa_part():
        _cross_attention_branch(proj(COL_XAQ, XA_WIDTH).astype(BF16), km_ref, vm_ref, yxa_ref)
        part["xa"] = [gate(2, c) * _dot(yxa_ref[...], wa_ref[:, c]) for c in strips]

    def ret_gate_part():
        part["ret_gate"] = [gate(1, c) for c in strips]

    silu_g = []
    for h in range(RET_HEADS):
        gh = proj(COL_G + h * RET_V_DIM, RET_V_DIM)
        silu_g.append(gh * _sigmoid(gh))
    fillers = [pool_part, xa_part, ret_gate_part] + [lambda: None] * (tile // RET_CHUNK - 3)
    _retention_branch(proj(COL_Q, RET_HEADS * RET_QK_DIM), _dot_nt(wkt_ref[...], xb),
                      proj(COL_V, RET_HEADS * RET_V_DIM).astype(BF16), silu_g, cos_ref, sin_ref, cos_t_ref,
                      sin_t_ref, dmask_ref, qdec_ref, kdec_ref, gnw_ref, state_ref, rq_ref, rqd_ref, rkt_ref,
                      rkdt_ref, yret_ref, cdec, fillers)
    merged = jnp.concatenate(
        [(part["pool"][i] + part["ret_gate"][i] * _dot(yret_ref[...], wr_ref[:, c]) + part["xa"][i]).astype(BF16)
         for i, c in enumerate(strips)], axis=1)
    h = jnp.concatenate([ALPHA * x[:, c] + _dot(merged, wo_ref[:, c]) for c in strips], axis=1)
    x1 = jnp.concatenate([_layer_norm(h[r:r + LN_ROWS], lnw_ref[...], lnb_ref[...])
                          for r in range(0, tile, LN_ROWS)], axis=0)
    xp_ref[...] = _pack_halves(x1)
    x1_hi = x1.astype(BF16)
    x1_lo = (x1 - x1_hi.astype(F32)).astype(BF16)
    logits_t = (_dot_nt(rwh_ref[...], x1_hi) + _dot_nt(rwh_ref[...], x1_lo) + _dot_nt(rwl_ref[...], x1_hi)
                + rb_ref[...])
    e0, e1, w0, w1 = _route(logits_t)
    eid_ref[...] = jnp.concatenate([e0, e1, jnp.zeros((8 - TOP_K, tile), I32)], axis=0)
    w_t = jnp.concatenate([w0, w1, jnp.zeros((WTOK_LANES - TOP_K, tile), F32)], axis=0)
    wtok_ref[...] = jnp.transpose(w_t)


def _mixer(xf, rope, kv, win, wgrp, pscale, gnw, wp, wr, wa, wo, lnw, lnb, rw, rb, batch, seq, mem_len,
           tile=512):
    T = xf.shape[0]
    nj = seq // tile
    cos, sin, cos_t, sin_t = rope
    bmain, bhalo = _pool_bands()
    dmask, qdec, kdec, cdec = _ret_consts()
    resident = lambda a: pl.BlockSpec(a.shape, lambda b, j: (0,) * a.ndim, pipeline_mode=pl.Buffered(1))
    rowblk = lambda w: pl.BlockSpec((tile, w), lambda b, j: (b * nj + j, 0))
    colblk = lambda r: pl.BlockSpec((r, tile), lambda b, j: (0, b * nj + j))
    consts = (win, wgrp, pscale, bmain, bhalo, dmask, qdec, kdec, gnw, wp, wr, wa, wo, lnw, lnb, *rw, rb)
    return pl.pallas_call(
        functools.partial(_mixer_kernel, tile=tile, cdec=cdec),
        grid=(batch, nj),
        in_specs=[rowblk(D_MODEL), rowblk(RET_QK_DIM), rowblk(RET_QK_DIM),
                  colblk(RET_QK_DIM // 2), colblk(RET_QK_DIM // 2),
                  pl.BlockSpec((mem_len, XA_WIDTH), lambda b, j: (b, 0)),
                  pl.BlockSpec((mem_len, XA_WIDTH), lambda b, j: (b, 1))] + [resident(a) for a in consts],
        out_specs=[rowblk(PACK_W), colblk(8), rowblk(WTOK_LANES)],
        out_shape=[jax.ShapeDtypeStruct((T, PACK_W), I32), jax.ShapeDtypeStruct((8, T), I32),
                   jax.ShapeDtypeStruct((T, WTOK_LANES), F32)],
        scratch_shapes=[pltpu.VMEM((RET_HEADS, RET_QK_DIM, RET_V_DIM), F32),
                        pltpu.VMEM((2, POOL_HALO, POOL_WIDTH), BF16),
                        pltpu.VMEM((tile, POOL_WIDTH), BF16),
                        pltpu.VMEM((tile, RET_HEADS * RET_V_DIM), BF16),
                        pltpu.VMEM((tile, XA_WIDTH), BF16),
                        pltpu.VMEM((tile, RET_HEADS * RET_QK_DIM), BF16),
                        pltpu.VMEM((tile, RET_HEADS * RET_QK_DIM), BF16),
                        pltpu.VMEM((RET_HEADS * RET_QK_DIM, tile), BF16),
                        pltpu.VMEM((RET_HEADS * RET_QK_DIM, tile), BF16),
                        pltpu.VMEM((RET_HEADS * RET_QK_DIM, D_MODEL), BF16)],
        compiler_params=pltpu.CompilerParams(dimension_semantics=("arbitrary", "arbitrary"),
                                             vmem_limit_bytes=VMEM_LIMIT),
        name="mixer",
    )(xf, cos, sin, cos_t, sin_t, kv, kv, *consts)


META_LANES = 128


def _positions_kernel(eid_ref, tri_ref, low_ref, pos_ref, first_tile_ref, n_tiles_ref, *, n_tok):
    n_chunks = n_tok // RANK_CHUNK
    erow = lax.broadcasted_iota(I32, (N_EXPERTS, RANK_CHUNK), 0)

    def onehot(c):
        sl = slice(c * RANK_CHUNK, (c + 1) * RANK_CHUNK)
        m0 = eid_ref[0:1, sl] == erow
        m1 = eid_ref[1:2, sl] == erow
        return m0, m1, jnp.where(m0, 1.0, 0.0) + jnp.where(m1, 1.0, 0.0)

    counts = jnp.zeros((N_EXPERTS, 1), F32)
    for c in range(n_chunks):
        counts = counts + jnp.sum(onehot(c)[2], axis=1, keepdims=True)
    ptiles = jnp.floor((counts + (MOE_TM - 1)) * (1.0 / MOE_TM))
    ptiles_b = jnp.broadcast_to(ptiles, (N_EXPERTS, 128)).astype(BF16)
    start = _dot(low_ref[...], ptiles_b)[:, 0:1] * MOE_TM

    pos_ref[...] = jnp.zeros_like(pos_ref)
    carry = start - 1.0
    for c in range(n_chunks):
        sl = slice(c * RANK_CHUNK, (c + 1) * RANK_CHUNK)
        m0, m1, oh = onehot(c)
        rank = _dot(oh.astype(BF16), tri_ref[...]) + carry
        pos_ref[0:1, sl] = jnp.sum(jnp.where(m0, rank, 0.0), axis=0, keepdims=True).astype(I32)
        pos_ref[1:2, sl] = jnp.sum(jnp.where(m1, rank, 0.0), axis=0, keepdims=True).astype(I32)
        carry = carry + jnp.sum(oh, axis=1, keepdims=True)

    first_tile_ref[...] = jnp.broadcast_to(start * (1.0 / MOE_TM), first_tile_ref.shape).astype(I32)
    n_tiles_ref[...] = jnp.broadcast_to(ptiles, n_tiles_ref.shape).astype(I32)


def _positions(eid):
    T = eid.shape[1]
    r = np.arange(RANK_CHUNK)
    tri = jnp.asarray(r[:, None] <= r[None, :], BF16)
    e = np.arange(N_EXPERTS)
    low = jnp.asarray(e[None, :] < e[:, None], BF16)
    full = lambda a: pl.BlockSpec(a.shape, lambda i: (0,) * a.ndim)
    return pl.pallas_call(
        functools.partial(_positions_kernel, n_tok=T),
        grid=(1,),
        in_specs=[full(eid), full(tri), full(low)],
        out_specs=[pl.BlockSpec((8, T), lambda i: (0, 0)),
                   pl.BlockSpec((N_EXPERTS, META_LANES), lambda i: (0, 0)),
                   pl.BlockSpec((N_EXPERTS, META_LANES), lambda i: (0, 0))],
        out_shape=[jax.ShapeDtypeStruct((8, T), I32), jax.ShapeDtypeStruct((N_EXPERTS, META_LANES), I32),
                   jax.ShapeDtypeStruct((N_EXPERTS, META_LANES), I32)],
        name="route_positions",
    )(eid, tri, low)


def _sc_workers():
    info = plsc.get_sparse_core_info()
    return info.num_cores, info.num_cores * info.num_subcores


def _sc_scatter_rows(xp, pos2d, n_out):
    T, W = xp.shape
    n_cores, n_workers = _sc_workers()
    cpw = T // SC_CHUNK // n_workers
    mesh = plsc.VectorSubcoreMesh(core_axis_name="c", subcore_axis_name="s")

    @functools.partial(
        pl.kernel, mesh=mesh, out_type=jax.ShapeDtypeStruct((n_out, W), I32),
        scratch_types=[pltpu.VMEM((TOP_K * cpw, SC_CHUNK), I32), pltpu.VMEM((SC_CHUNK, W), I32)],
        name="sc_scatter_rows")
    def k(x_hbm, pos_hbm, out_hbm, idx_v, rows_v):
        wid = lax.axis_index("s") * n_cores + lax.axis_index("c")
        for s in range(TOP_K):
            pltpu.sync_copy(pos_hbm.at[pl.ds(s * (T // SC_CHUNK) + wid * cpw, cpw)],
                            idx_v.at[pl.ds(s * cpw, cpw)])
        for j in range(cpw):
            pltpu.sync_copy(x_hbm.at[pl.ds((wid * cpw + j) * SC_CHUNK, SC_CHUNK)], rows_v)
            for s in range(TOP_K):
                pltpu.sync_copy(rows_v, out_hbm.at[idx_v.at[s * cpw + j]])

    return k(xp, pos2d)


def _sc_gather_rows(y, idx2d):
    W = y.shape[1]
    n = idx2d.shape[0] * SC_CHUNK
    n_cores, n_workers = _sc_workers()
    cpw = n // SC_CHUNK // n_workers
    mesh = plsc.VectorSubcoreMesh(core_axis_name="c", subcore_axis_name="s")

    @functools.partial(
        pl.kernel, mesh=mesh, out_type=jax.ShapeDtypeStruct((n, W), I32),
        scratch_types=[pltpu.VMEM((cpw, SC_CHUNK), I32), pltpu.VMEM((SC_CHUNK, W), I32)],
        name="sc_gather_rows")
    def k(y_hbm, idx_hbm, out_hbm, idx_v, rows_v):
        wid = lax.axis_index("s") * n_cores + lax.axis_index("c")
        pltpu.sync_copy(idx_hbm.at[pl.ds(wid * cpw, cpw)], idx_v)
        for j in range(cpw):
            pltpu.sync_copy(y_hbm.at[idx_v.at[j]], rows_v)
            pltpu.sync_copy(rows_v, out_hbm.at[pl.ds((wid * cpw + j) * SC_CHUNK, SC_CHUNK)])

    return k(y, idx2d)


def _routed_kernel(first_ref, count_ref, xs_hbm, wg_ref, wu_ref, wd_ref, ys_hbm, xbuf, ybuf, in_sem, out_sem):
    e = pl.program_id(0)
    last = pl.num_programs(0) - 1
    total = first_ref[last] + count_ref[last]

    def in_copy(g, slot):
        return pltpu.make_async_copy(xs_hbm.at[pl.ds(g * MOE_TM, MOE_TM)], xbuf.at[slot], in_sem.at[slot])

    def out_copy(g, slot):
        return pltpu.make_async_copy(ybuf.at[slot], ys_hbm.at[pl.ds(g * MOE_TM, MOE_TM)], out_sem.at[slot])

    @pl.when(e == 0)
    def _():
        in_copy(0, 0).start()

    wg = wg_ref[0].astype(BF16)
    wu = wu_ref[0].astype(BF16)
    wd = wd_ref[0].astype(BF16)

    def tile_step(i, carry):
        g = first_ref[e] + i
        slot = lax.rem(g, 2)
        in_copy(g, slot).wait()

        @pl.when(g + 1 < total)
        def _():
            in_copy(g + 1, 1 - slot).start()

        @pl.when(g >= 2)
        def _():
            out_copy(g - 2, slot).wait()

        lo, hi = _unpack_halves(xbuf[slot])
        lo = lo.astype(BF16)
        hi = hi.astype(BF16)
        a = _dot(lo, wg[:PACK_W]) + _dot(hi, wg[PACK_W:])
        b = _dot(lo, wu[:PACK_W]) + _dot(hi, wu[PACK_W:])
        act = (a * _sigmoid(a) * b).astype(BF16)
        ybuf[slot] = _pack_halves(_dot(act, wd))
        out_copy(g, slot).start()
        return carry

    lax.fori_loop(0, count_ref[e], tile_step, 0)

    @pl.when(e == last)
    def _():
        @pl.when(total >= 2)
        def _():
            out_copy(total - 2, lax.rem(total, 2)).wait()

        out_copy(total - 1, lax.rem(total - 1, 2)).wait()


def _routed_mlp(first_tile, n_tiles, xs, wg, wu, wd):
    R = xs.shape[0]
    any_space = pl.BlockSpec(memory_space=pl.ANY)
    return pl.pallas_call(
        _routed_kernel,
        grid_spec=pltpu.PrefetchScalarGridSpec(
            num_scalar_prefetch=2,
            grid=(N_EXPERTS,),
            in_specs=[any_space,
                      pl.BlockSpec((1, D_MODEL, D_EXPERT), lambda e, ft, nt: (e, 0, 0)),
                      pl.BlockSpec((1, D_MODEL, D_EXPERT), lambda e, ft, nt: (e, 0, 0)),
                      pl.BlockSpec((1, D_EXPERT, D_MODEL), lambda e, ft, nt: (e, 0, 0))],
            out_specs=any_space,
            scratch_shapes=[pltpu.VMEM((2, MOE_TM, PACK_W), I32), pltpu.VMEM((2, MOE_TM, PACK_W), I32),
                            pltpu.SemaphoreType.DMA((2,)), pltpu.SemaphoreType.DMA((2,))]),
        out_shape=jax.ShapeDtypeStruct((R, PACK_W), I32),
        compiler_params=pltpu.CompilerParams(dimension_semantics=("arbitrary",)),
        name="routed_mlp",
    )(first_tile, n_tiles, xs, wg, wu, wd)


def _combine_kernel(xp_ref, y0_ref, y1_ref, wtok_ref, lnw_ref, lnb_ref, o_ref):
    for r in range(0, xp_ref.shape[0], LN_ROWS):
        rows = slice(r, r + LN_ROWS)
        w0 = wtok_ref[rows, 0:1]
        w1 = wtok_ref[rows, 1:2]
        xlo, xhi = _unpack_halves(xp_ref[rows, :])
        y0lo, y0hi = _unpack_halves(y0_ref[rows, :])
        y1lo, y1hi = _unpack_halves(y1_ref[rows, :])
        h = jnp.concatenate([ALPHA * xlo + (w0 * y0lo + w1 * y1lo), ALPHA * xhi + (w0 * y0hi + w1 * y1hi)],
                            axis=1)
        o_ref[rows, :] = _layer_norm(h, lnw_ref[...], lnb_ref[...])


def _combine_ln2(xp, yg, wtok, lnw, lnb, tile=512):
    T = xp.shape[0]
    nt = T // tile
    full = lambda a: pl.BlockSpec(a.shape, lambda i: (0,) * a.ndim)
    return pl.pallas_call(
        _combine_kernel,
        grid=(nt,),
        in_specs=[pl.BlockSpec((tile, PACK_W), lambda i: (i, 0)),
                  pl.BlockSpec((tile, PACK_W), lambda i: (i, 0)),
                  pl.BlockSpec((tile, PACK_W), lambda i: (i + nt, 0)),
                  pl.BlockSpec((tile, WTOK_LANES), lambda i: (i, 0)),
                  full(lnw), full(lnb)],
        out_specs=pl.BlockSpec((tile, D_MODEL), lambda i: (i, 0)),
        out_shape=jax.ShapeDtypeStruct((T, D_MODEL), F32),
        name="combine_ln2",
    )(xp, yg, yg, wtok, lnw, lnb)


def _router_params(w_grp, b_grp, w_exp, b_exp):
    rw = jnp.zeros((ROUTER_ROWS, D_MODEL), F32)
    rw = rw.at[0:N_GROUPS].set(w_grp.T).at[EXP_ROW0:EXP_ROW0 + N_EXPERTS].set(w_exp.T)
    rb = jnp.zeros((ROUTER_ROWS,), F32).at[N_GROUPS:8].set(NEG_BIG)
    rb = rb.at[0:N_GROUPS].set(b_grp).at[EXP_ROW0:EXP_ROW0 + N_EXPERTS].set(b_exp)
    rw_hi = rw.astype(BF16)
    rw_lo = (rw - rw_hi.astype(F32)).astype(BF16)
    return (rw_hi, rw_lo), rb[:, None]


def kernel(x, mem, positions, w_in, w_pool_grp, pool_scale, ret_gn_w, w_mem_kv, w_br_pool, w_br_ret, w_br_xa,
           w_out, ln1_w, ln1_b, w_grp_router, b_grp_router, w_exp_router, b_exp_router, w_exp_gate, w_exp_up,
           w_exp_down, ln2_w, ln2_b):
    B, S, D = x.shape
    assert D == D_MODEL and w_in.shape[0] == DEPTH and S % 512 == 0
    T = B * S
    M = mem.shape[1]
    l = 0
    xf = x.reshape(T, D)

    rope = _rope_table(positions.reshape(1, T))
    kv = _mem_kv(mem.reshape(B * M, D), w_mem_kv[l].astype(BF16))
    rw, rb = _router_params(w_grp_router[l], b_grp_router[l], w_exp_router[l], b_exp_router[l])
    xp, eid, wtok = _mixer(xf, rope, kv, w_in[l].astype(BF16),
                           w_pool_grp[l].astype(BF16), pool_scale[l][None, :], ret_gn_w[l].reshape(1, -1),
                           w_br_pool[l].astype(BF16), w_br_ret[l].astype(BF16), w_br_xa[l].astype(BF16),
                           w_out[l].astype(BF16), ln1_w[l][None, :], ln1_b[l][None, :], rw, rb, B, S, M)

    pos, first_tile, n_tiles = _positions(eid)
    pos2d = pos[0:TOP_K].reshape(TOP_K * T // SC_CHUNK, SC_CHUNK)
    max_tiles = (TOP_K * T + N_EXPERTS * (MOE_TM - 1)) // MOE_TM
    xs = _sc_scatter_rows(xp, pos2d, max_tiles * MOE_TM)
    ys = _routed_mlp(first_tile[:, 0], n_tiles[:, 0], xs,
                     w_exp_gate[l].reshape(N_EXPERTS, D_MODEL, D_EXPERT),
                     w_exp_up[l].reshape(N_EXPERTS, D_MODEL, D_EXPERT),
                     w_exp_down[l].reshape(N_EXPERTS, D_EXPERT, D_MODEL))
    yg = _sc_gather_rows(ys, pos2d)
    out = _combine_ln2(xp, yg, wtok, ln2_w[l][None, :], ln2_b[l][None, :])
    return out.reshape(B, S, D)
```

```python
import functools

import numpy as np
import jax
import jax.numpy as jnp
from jax import lax
from jax.experimental import pallas as pl
from jax.experimental.pallas import tpu as pltpu
from jax.experimental.pallas import tpu_sc as plsc

F32 = jnp.float32
BF16 = jnp.bfloat16
I32 = jnp.int32
U32 = jnp.uint32

D_MODEL = 1024
POOL_WINDOWS = (2, 4, 8, 16)
POOL_GROUP_DIM = 128
POOL_WIDTH = 512
POOL_HALO = 16
RET_HEADS = 4
RET_QK_DIM = 128
RET_V_DIM = 256
RET_CHUNK = 128
ROPE_BASE = 10000.0
XA_HEADS = 4
XA_HEAD_DIM = 128
XA_WIDTH = 512
N_GROUPS = 4
EXPERTS_PER_GROUP = 8
N_EXPERTS = N_GROUPS * EXPERTS_PER_GROUP
D_EXPERT = 256
LN_EPS = 1e-5
DEPTH = 1
ALPHA = (2.0 * DEPTH) ** 0.25
NEG_BIG = -1e30

COL_POOL, COL_Q, COL_K, COL_V, COL_G, COL_XAQ, COL_GATES = 0, 512, 1024, 1536, 2560, 3584, 4096

VMEM_LIMIT = 56 * 1024 * 1024

TOP_K = 2
PACK_W = D_MODEL // 2
MOE_TM = 512
SC_CHUNK = 128
RANK_CHUNK = 512
COMBINE_PARTS = 2
STRIP = 256
LN_ROWS = 32


def _dot(a, b):
    return jnp.dot(a, b, preferred_element_type=F32)


def _dot_nt(a, b, precision=None):
    return lax.dot_general(a, b, (((1,), (1,)), ((), ())), preferred_element_type=F32, precision=precision)


def _sigmoid(z):
    return 1.0 / (1.0 + jnp.exp2(z * (-1.0 / np.log(2.0))))


def _layer_norm(h, w, b):
    mu = jnp.mean(h, axis=-1, keepdims=True)
    hc = h - mu
    var = jnp.mean(hc * hc, axis=-1, keepdims=True)
    return hc * lax.rsqrt(var + LN_EPS) * w + b


def _rope_kernel(pos_ref, freq_ref, cos_ref, sin_ref, cos_t_ref, sin_t_ref):
    ang = freq_ref[...] * pos_ref[...].astype(F32)
    cos_t = jnp.cos(ang)
    sin_t = jnp.sin(ang)
    cos_t_ref[...] = cos_t
    sin_t_ref[...] = sin_t
    cos_ref[...] = jnp.transpose(jnp.concatenate([cos_t, cos_t], axis=0))
    sin_ref[...] = jnp.transpose(jnp.concatenate([-sin_t, sin_t], axis=0))


def _rope_table(pos_row, tile=2048):
    T = pos_row.shape[1]
    half = RET_QK_DIM // 2
    inv_freq = (ROPE_BASE ** (-np.arange(half, dtype=np.float64) / half)).astype(np.float32)
    freq = jnp.asarray(inv_freq[:, None])
    out = pl.BlockSpec((tile, RET_QK_DIM), lambda i: (i, 0))
    out_t = pl.BlockSpec((half, tile), lambda i: (0, i))
    return pl.pallas_call(
        _rope_kernel,
        grid=(T // tile,),
        in_specs=[pl.BlockSpec((1, tile), lambda i: (0, i)), pl.BlockSpec((half, 1), lambda i: (0, 0))],
        out_specs=[out, out, out_t, out_t],
        out_shape=[jax.ShapeDtypeStruct((T, RET_QK_DIM), F32)] * 2 + [jax.ShapeDtypeStruct((half, T), F32)] * 2,
        name="rope_table",
    )(pos_row, freq)


POOL_SUB = 256


def _pool_bands():
    r = np.arange(POOL_SUB)[:, None]
    c = np.arange(POOL_SUB)[None, :]
    ch = np.arange(POOL_HALO)[None, :] - POOL_HALO
    main = np.stack([((r - c >= 0) & (r - c < w)) for w in POOL_WINDOWS]).astype(np.float32)
    halo = np.stack([((r - ch >= 0) & (r - ch < w)) for w in POOL_WINDOWS]).astype(np.float32)
    return jnp.asarray(main, BF16), jnp.asarray(halo, BF16)


def _pool_branch(ub, j, halo_ref, bmain_ref, bhalo_ref, wg_ref, scale_ref, o_ref):
    tile = ub.shape[0]
    s0 = j * tile
    slot = lax.rem(j, 2)
    for sb in range(tile // POOL_SUB):
        r0 = sb * POOL_SUB
        u = ub[r0:r0 + POOL_SUB]
        prev = halo_ref[slot] if sb == 0 else ub[r0 - POOL_HALO:r0]
        pos = s0 + r0 + lax.broadcasted_iota(I32, (POOL_SUB, POOL_GROUP_DIM), 0)
        for g, w in enumerate(POOL_WINDOWS):
            cols = slice(g * POOL_GROUP_DIM, (g + 1) * POOL_GROUP_DIM)
            ug = u[:, cols]
            wsum = _dot(bmain_ref[g], ug) + _dot(bhalo_ref[g], prev[:, cols])
            cnt = jnp.minimum(pos + 1, w).astype(F32)
            pooled = wsum / cnt - ug.astype(F32)
            mixed = _dot(pooled.astype(BF16), wg_ref[g]) * scale_ref[:, cols]
            o_ref[r0:r0 + POOL_SUB, cols] = mixed.astype(BF16)
    halo_ref[1 - slot] = ub[tile - POOL_HALO:tile]


def _ret_consts():
    h = np.arange(RET_HEADS, dtype=np.float64)
    log_gamma = np.log1p(-np.exp2(-5.0 - h))
    pos = np.arange(RET_CHUNK, dtype=np.float64)
    diff = pos[:, None] - pos[None, :]
    kscale = RET_QK_DIM ** -0.5
    dmask = kscale * np.where(diff >= 0, np.exp(log_gamma[:, None, None] * np.maximum(diff, 0.0)), 0.0)
    qdec = np.exp(log_gamma[:, None] * (pos + 1.0)[None, :])
    kdec = kscale * np.exp(log_gamma[:, None] * (RET_CHUNK - 1.0 - pos)[None, :])
    cdec = np.exp(log_gamma * RET_CHUNK)
    lanes = lambda a: np.broadcast_to(a[:, :, None], (RET_HEADS, RET_CHUNK, RET_QK_DIM))
    kdec_t = np.broadcast_to(kdec[:, None, :], (RET_HEADS, RET_QK_DIM, RET_CHUNK))
    return (jnp.asarray(dmask, F32), jnp.asarray(lanes(qdec), F32), jnp.asarray(kdec_t, F32),
            tuple(float(v) for v in cdec))


def _retention_branch(q, k_t, v, silu_g, cos_ref, sin_ref, cos_t_ref, sin_t_ref, dmask_ref, qdec_ref, kdec_ref,
                      gnw_ref, state_ref, rq_ref, rqd_ref, rkt_ref, rkdt_ref, o_ref, cdec, fillers):
    tile = q.shape[0]
    n_chunks = tile // RET_CHUNK
    half = RET_QK_DIM // 2
    cos = cos_ref[...]
    sin = sin_ref[...]
    cos_t = cos_t_ref[...]
    sin_t = sin_t_ref[...]
    for h in range(RET_HEADS):
        qk = slice(h * RET_QK_DIM, (h + 1) * RET_QK_DIM)
        qh = q[:, qk]
        qr = qh * cos + pltpu.roll(qh, half, 1) * sin
        rq_ref[:, qk] = qr.astype(BF16)
        rqd_ref[:, qk] = (qr * jnp.concatenate([qdec_ref[h]] * n_chunks, axis=0)).astype(BF16)
        k1 = k_t[h * RET_QK_DIM:h * RET_QK_DIM + half]
        k2 = k_t[h * RET_QK_DIM + half:(h + 1) * RET_QK_DIM]
        kr_t = jnp.concatenate([k1 * cos_t - k2 * sin_t, k2 * cos_t + k1 * sin_t], axis=0)
        rkt_ref[qk, :] = kr_t.astype(BF16)
        rkdt_ref[qk, :] = (kr_t * jnp.concatenate([kdec_ref[h]] * n_chunks, axis=1)).astype(BF16)

    for c in range(n_chunks):
        rows = slice(c * RET_CHUNK, (c + 1) * RET_CHUNK)
        heads = [slice(h * RET_QK_DIM, (h + 1) * RET_QK_DIM) for h in range(RET_HEADS)]
        raw = [_dot(rq_ref[rows, qk], rkt_ref[qk, rows]) for qk in heads]
        fillers[c]()
        for h in range(RET_HEADS):
            qk = heads[h]
            v_cols = slice(h * RET_V_DIM, (h + 1) * RET_V_DIM)
            vh = v[rows, v_cols]
            scores = raw[h] * dmask_ref[h]
            st = state_ref[h]
            lhs = jnp.concatenate([scores.astype(BF16), rqd_ref[rows, qk]], axis=1)
            y = _dot(lhs, jnp.concatenate([vh, st.astype(BF16)], axis=0))
            state_ref[h] = cdec[h] * st + _dot(rkdt_ref[qk, rows], vh)
            mu = jnp.mean(y, axis=-1, keepdims=True)
            yc = y - mu
            var = jnp.mean(yc * yc, axis=-1, keepdims=True)
            yn = yc * lax.rsqrt(var + LN_EPS) * gnw_ref[:, v_cols]
            o_ref[rows, v_cols] = (silu_g[h][rows] * yn).astype(BF16)


def _memkv_kernel(m_ref, w_ref, o_ref):
    o_ref[...] = _dot(m_ref[...].astype(BF16), w_ref[...]).astype(BF16)


def _mem_kv(memf, w_b):
    M, D = memf.shape
    N = w_b.shape[1]
    return pl.pallas_call(
        _memkv_kernel,
        grid=(1,),
        in_specs=[pl.BlockSpec((M, D), lambda i: (0, 0)), pl.BlockSpec((D, N), lambda i: (0, 0))],
        out_specs=pl.BlockSpec((M, N), lambda i: (0, 0)),
        out_shape=jax.ShapeDtypeStruct((M, N), BF16),
        name="mem_kv",
    )(memf, w_b)


def _cross_attention_branch(xq, k_ref, v_ref, o_ref):
    scale = XA_HEAD_DIM ** -0.5
    for h in range(XA_HEADS):
        cols = slice(h * XA_HEAD_DIM, (h + 1) * XA_HEAD_DIM)
        s = _dot_nt(xq[:, cols], k_ref[:, cols]) * scale
        m = jnp.max(s, axis=-1, keepdims=True)
        p = jnp.exp(s - m)
        l = jnp.sum(p, axis=-1, keepdims=True)
        o = _dot(p.astype(BF16), v_ref[:, cols]) / l
        o_ref[:, cols] = o.astype(BF16)


ROUTER_ROWS = 40
WTOK_LANES = 128
EXP_ROW0 = 8


def _route(logits_t):
    gl = logits_t[0:8]
    gmax = jnp.max(gl, axis=0, keepdims=True)
    p_grp = 1.0 / jnp.sum(jnp.exp(gl - gmax), axis=0, keepdims=True)
    idx8 = lax.broadcasted_iota(jnp.int32, gl.shape, 0)
    gsel = jnp.min(jnp.where(gl == gmax, idx8, 8), axis=0, keepdims=True)
    cl = jnp.zeros_like(gl)
    for g in range(N_GROUPS):
        r0 = EXP_ROW0 + g * EXPERTS_PER_GROUP
        cl = cl + jnp.where(gsel == g, logits_t[r0:r0 + EXPERTS_PER_GROUP], 0.0)
    v1 = jnp.max(cl, axis=0, keepdims=True)
    i1 = jnp.min(jnp.where(cl == v1, idx8, 8), axis=0, keepdims=True)
    cl2 = jnp.where(idx8 == i1, -jnp.inf, cl)
    v2 = jnp.max(cl2, axis=0, keepdims=True)
    i2 = jnp.min(jnp.where(cl2 == v2, idx8, 8), axis=0, keepdims=True)
    e21 = jnp.exp(v2 - v1)
    w1 = p_grp / (1.0 + e21)
    w2 = p_grp * e21 / (1.0 + e21)
    return gsel * EXPERTS_PER_GROUP + i1, gsel * EXPERTS_PER_GROUP + i2, w1, w2


def _pack_halves(v):
    half = v.shape[1] // 2
    lo = lax.bitcast_convert_type(v[:, :half].astype(BF16).astype(F32), U32)
    hi = lax.bitcast_convert_type(v[:, half:].astype(BF16).astype(F32), U32)
    return lax.bitcast_convert_type(lax.shift_right_logical(lo, U32(16)) | hi, I32)


def _unpack_halves(w):
    u = lax.bitcast_convert_type(w, U32)
    lo = lax.bitcast_convert_type(lax.shift_left(u, U32(16)), F32)
    hi = lax.bitcast_convert_type(u & U32(0xFFFF0000), F32)
    return lo, hi


def _mixer_kernel(x_ref, cos_ref, sin_ref, cos_t_ref, sin_t_ref, km_ref, vm_ref, win_ref, wgrp_ref,
                  pscale_ref, bmain_ref, bhalo_ref, dmask_ref, qdec_ref, kdec_ref, gnw_ref, wp_ref, wr_ref,
                  wa_ref, wo_ref, lnw_ref, lnb_ref, rwh_ref, rwl_ref, rb_ref, xp_ref, eid_ref, wtok_ref,
                  state_ref, halo_ref, ypool_ref, yret_ref, yxa_ref, rq_ref, rqd_ref, rkt_ref, rkdt_ref, wkt_ref,
                  *, tile, cdec):
    j = pl.program_id(1)

    @pl.when(jnp.logical_and(pl.program_id(0) == 0, j == 0))
    def _():
        wk = win_ref[:, COL_K:COL_K + RET_HEADS * RET_QK_DIM].astype(F32)
        wkt_ref[...] = jnp.transpose(wk).astype(BF16)

    @pl.when(j == 0)
    def _():
        state_ref[...] = jnp.zeros_like(state_ref)
        halo_ref[...] = jnp.zeros_like(halo_ref)

    x = x_ref[...]
    xb = x.astype(BF16)

    def proj(col, width):
        return _dot(xb, win_ref[:, col:col + width])

    part = {}
    strips = [slice(c, c + STRIP) for c in range(0, D_MODEL, STRIP)]

    def gate(branch, cols):
        return _sigmoid(proj(COL_GATES + branch * D_MODEL + cols.start, STRIP))

    def pool_part():
        _pool_branch(proj(COL_POOL, POOL_WIDTH).astype(BF16), j, halo_ref, bmain_ref, bhalo_ref, wgrp_ref,
                     pscale_ref, ypool_ref)
        part["pool"] = [gate(0, c) * _dot(ypool_ref[...], wp_ref[:, c]) for c in strips]

    def xa_part():
        _cross_attention_branch(proj(COL_XAQ, XA_WIDTH).astype(BF16), km_ref, vm_ref, yxa_ref)
        part["xa"] = [gate(2, c) * _dot(yxa_ref[...], wa_ref[:, c]) for c in strips]

    def ret_gate_part():
        part["ret_gate"] = [gate(1, c) for c in strips]

    silu_g = []
    for h in range(RET_HEADS):
        gh = proj(COL_G + h * RET_V_DIM, RET_V_DIM)
        silu_g.append(gh * _sigmoid(gh))
    fillers = [pool_part, xa_part, ret_gate_part] + [lambda: None] * (tile // RET_CHUNK - 3)
    _retention_branch(proj(COL_Q, RET_HEADS * RET_QK_DIM), _dot_nt(wkt_ref[...], xb),
                      proj(COL_V, RET_HEADS * RET_V_DIM).astype(BF16), silu_g, cos_ref, sin_ref, cos_t_ref,
                      sin_t_ref, dmask_ref, qdec_ref, kdec_ref, gnw_ref, state_ref, rq_ref, rqd_ref, rkt_ref,
                      rkdt_ref, yret_ref, cdec, fillers)
    merged = jnp.concatenate(
        [(part["pool"][i] + part["ret_gate"][i] * _dot(yret_ref[...], wr_ref[:, c]) + part["xa"][i]).astype(BF16)
         for i, c in enumerate(strips)], axis=1)
    h = jnp.concatenate([ALPHA * x[:, c] + _dot(merged, wo_ref[:, c]) for c in strips], axis=1)
    x1 = jnp.concatenate([_layer_norm(h[r:r + LN_ROWS], lnw_ref[...], lnb_ref[...])
                          for r in range(0, tile, LN_ROWS)], axis=0)
    xp_ref[...] = _pack_halves(x1)
    x1_hi = x1.astype(BF16)
    x1_lo = (x1 - x1_hi.astype(F32)).astype(BF16)
    logits_t = (_dot_nt(rwh_ref[...], x1_hi) + _dot_nt(rwh_ref[...], x1_lo) + _dot_nt(rwl_ref[...], x1_hi)
                + rb_ref[...])
    e0, e1, w0, w1 = _route(logits_t)
    eid_ref[...] = jnp.concatenate([e0, e1, jnp.zeros((8 - TOP_K, tile), I32)], axis=0)
    w_t = jnp.concatenate([w0, w1, jnp.zeros((WTOK_LANES - TOP_K, tile), F32)], axis=0)
    wtok_ref[...] = jnp.transpose(w_t)


def _mixer(xf, rope, kv, win, wgrp, pscale, gnw, wp, wr, wa, wo, lnw, lnb, rw, rb, batch, seq, mem_len,
           tile=512):
    T = xf.shape[0]
    nj = seq // tile
    cos, sin, cos_t, sin_t = rope
    bmain, bhalo = _pool_bands()
    dmask, qdec, kdec, cdec = _ret_consts()
    resident = lambda a: pl.BlockSpec(a.shape, lambda b, j: (0,) * a.ndim, pipeline_mode=pl.Buffered(1))
    rowblk = lambda w: pl.BlockSpec((tile, w), lambda b, j: (b * nj + j, 0))
    colblk = lambda r: pl.BlockSpec((r, tile), lambda b, j: (0, b * nj + j))
    consts = (win, wgrp, pscale, bmain, bhalo, dmask, qdec, kdec, gnw, wp, wr, wa, wo, lnw, lnb, *rw, rb)
    return pl.pallas_call(
        functools.partial(_mixer_kernel, tile=tile, cdec=cdec),
        grid=(batch, nj),
        in_specs=[rowblk(D_MODEL), rowblk(RET_QK_DIM), rowblk(RET_QK_DIM),
                  colblk(RET_QK_DIM // 2), colblk(RET_QK_DIM // 2),
                  pl.BlockSpec((mem_len, XA_WIDTH), lambda b, j: (b, 0)),
                  pl.BlockSpec((mem_len, XA_WIDTH), lambda b, j: (b, 1))] + [resident(a) for a in consts],
        out_specs=[rowblk(PACK_W), colblk(8), rowblk(WTOK_LANES)],
        out_shape=[jax.ShapeDtypeStruct((T, PACK_W), I32), jax.ShapeDtypeStruct((8, T), I32),
                   jax.ShapeDtypeStruct((T, WTOK_LANES), F32)],
        scratch_shapes=[pltpu.VMEM((RET_HEADS, RET_QK_DIM, RET_V_DIM), F32),
                        pltpu.VMEM((2, POOL_HALO, POOL_WIDTH), BF16),
                        pltpu.VMEM((tile, POOL_WIDTH), BF16),
                        pltpu.VMEM((tile, RET_HEADS * RET_V_DIM), BF16),
                        pltpu.VMEM((tile, XA_WIDTH), BF16),
                        pltpu.VMEM((tile, RET_HEADS * RET_QK_DIM), BF16),
                        pltpu.VMEM((tile, RET_HEADS * RET_QK_DIM), BF16),
                        pltpu.VMEM((RET_HEADS * RET_QK_DIM, tile), BF16),
                        pltpu.VMEM((RET_HEADS * RET_QK_DIM, tile), BF16),
                        pltpu.VMEM((RET_HEADS * RET_QK_DIM, D_MODEL), BF16)],
        compiler_params=pltpu.CompilerParams(dimension_semantics=("arbitrary", "arbitrary"),
                                             vmem_limit_bytes=VMEM_LIMIT),
        name="mixer",
    )(xf, cos, sin, cos_t, sin_t, kv, kv, *consts)


META_LANES = 128


def _positions_kernel(eid_ref, tri_ref, low_ref, pos_ref, first_tile_ref, n_tiles_ref, *, n_tok):
    n_chunks = n_tok // RANK_CHUNK
    erow = lax.broadcasted_iota(I32, (N_EXPERTS, RANK_CHUNK), 0)

    def onehot(c):
        sl = slice(c * RANK_CHUNK, (c + 1) * RANK_CHUNK)
        m0 = eid_ref[0:1, sl] == erow
        m1 = eid_ref[1:2, sl] == erow
        return m0, m1, jnp.where(m0, 1.0, 0.0) + jnp.where(m1, 1.0, 0.0)

    counts = jnp.zeros((N_EXPERTS, 1), F32)
    for c in range(n_chunks):
        counts = counts + jnp.sum(onehot(c)[2], axis=1, keepdims=True)
    ptiles = jnp.floor((counts + (MOE_TM - 1)) * (1.0 / MOE_TM))
    ptiles_b = jnp.broadcast_to(ptiles, (N_EXPERTS, 128)).astype(BF16)
    start = _dot(low_ref[...], ptiles_b)[:, 0:1] * MOE_TM

    pos_ref[...] = jnp.zeros_like(pos_ref)
    carry = start - 1.0
    for c in range(n_chunks):
        sl = slice(c * RANK_CHUNK, (c + 1) * RANK_CHUNK)
        m0, m1, oh = onehot(c)
        rank = _dot(oh.astype(BF16), tri_ref[...]) + carry
        pos_ref[0:1, sl] = jnp.sum(jnp.where(m0, rank, 0.0), axis=0, keepdims=True).astype(I32)
        pos_ref[1:2, sl] = jnp.sum(jnp.where(m1, rank, 0.0), axis=0, keepdims=True).astype(I32)
        carry = carry + jnp.sum(oh, axis=1, keepdims=True)

    first_tile_ref[...] = jnp.broadcast_to(start * (1.0 / MOE_TM), first_tile_ref.shape).astype(I32)
    n_tiles_ref[...] = jnp.broadcast_to(ptiles, n_tiles_ref.shape).astype(I32)


def _positions(eid):
    T = eid.shape[1]
    r = np.arange(RANK_CHUNK)
    tri = jnp.asarray(r[:, None] <= r[None, :], BF16)
    e = np.arange(N_EXPERTS)
    low = jnp.asarray(e[None, :] < e[:, None], BF16)
    full = lambda a: pl.BlockSpec(a.shape, lambda i: (0,) * a.ndim)
    return pl.pallas_call(
        functools.partial(_positions_kernel, n_tok=T),
        grid=(1,),
        in_specs=[full(eid), full(tri), full(low)],
        out_specs=[pl.BlockSpec((8, T), lambda i: (0, 0)),
                   pl.BlockSpec((N_EXPERTS, META_LANES), lambda i: (0, 0)),
                   pl.BlockSpec((N_EXPERTS, META_LANES), lambda i: (0, 0))],
        out_shape=[jax.ShapeDtypeStruct((8, T), I32), jax.ShapeDtypeStruct((N_EXPERTS, META_LANES), I32),
                   jax.ShapeDtypeStruct((N_EXPERTS, META_LANES), I32)],
        name="route_positions",
    )(eid, tri, low)


def _sc_workers():
    info = plsc.get_sparse_core_info()
    return info.num_cores, info.num_cores * info.num_subcores


def _sc_scatter_rows(xp, pos2d, n_out):
    T, W = xp.shape
    n_cores, n_workers = _sc_workers()
    cpw = T // SC_CHUNK // n_workers
    mesh = plsc.VectorSubcoreMesh(core_axis_name="c", subcore_axis_name="s")

    @functools.partial(
        pl.kernel, mesh=mesh, out_type=jax.ShapeDtypeStruct((n_out, W), I32),
        scratch_types=[pltpu.VMEM((TOP_K * cpw, SC_CHUNK), I32), pltpu.VMEM((SC_CHUNK, W), I32)],
        name="sc_scatter_rows")
    def k(x_hbm, pos_hbm, out_hbm, idx_v, rows_v):
        wid = lax.axis_index("s") * n_cores + lax.axis_index("c")
        for s in range(TOP_K):
            pltpu.sync_copy(pos_hbm.at[pl.ds(s * (T // SC_CHUNK) + wid * cpw, cpw)],
                            idx_v.at[pl.ds(s * cpw, cpw)])
        for j in range(cpw):
            pltpu.sync_copy(x_hbm.at[pl.ds((wid * cpw + j) * SC_CHUNK, SC_CHUNK)], rows_v)
            for s in range(TOP_K):
                pltpu.sync_copy(rows_v, out_hbm.at[idx_v.at[s * cpw + j]])

    return k(xp, pos2d)


def _sc_gather_rows(y, idx2d):
    W = y.shape[1]
    n = idx2d.shape[0] * SC_CHUNK
    n_cores, n_workers = _sc_workers()
    cpw = n // SC_CHUNK // n_workers
    mesh = plsc.VectorSubcoreMesh(core_axis_name="c", subcore_axis_name="s")

    @functools.partial(
        pl.kernel, mesh=mesh, out_type=jax.ShapeDtypeStruct((n, W), I32),
        scratch_types=[pltpu.VMEM((cpw, SC_CHUNK), I32), pltpu.VMEM((SC_CHUNK, W), I32)],
        name="sc_gather_rows")
    def k(y_hbm, idx_hbm, out_hbm, idx_v, rows_v):
        wid = lax.axis_index("s") * n_cores + lax.axis_index("c")
        pltpu.sync_copy(idx_hbm.at[pl.ds(wid * cpw, cpw)], idx_v)
        for j in range(cpw):
            pltpu.sync_copy(y_hbm.at[idx_v.at[j]], rows_v)
            pltpu.sync_copy(rows_v, out_hbm.at[pl.ds((wid * cpw + j) * SC_CHUNK, SC_CHUNK)])

    return k(y, idx2d)


def _routed_kernel(first_ref, count_ref, xs_hbm, wg_ref, wu_ref, wd_ref, ys_hbm, xbuf, ybuf, in_sem, out_sem):
    e = pl.program_id(0)
    last = pl.num_programs(0) - 1
    total = first_ref[last] + count_ref[last]

    def in_copy(g, slot):
        return pltpu.make_async_copy(xs_hbm.at[pl.ds(g * MOE_TM, MOE_TM)], xbuf.at[slot], in_sem.at[slot])

    def out_copy(g, slot):
        return pltpu.make_async_copy(ybuf.at[slot], ys_hbm.at[pl.ds(g * MOE_TM, MOE_TM)], out_sem.at[slot])

    @pl.when(e == 0)
    def _():
        in_copy(0, 0).start()

    wg = wg_ref[0].astype(BF16)
    wu = wu_ref[0].astype(BF16)
    wd = wd_ref[0].astype(BF16)

    def tile_step(i, carry):
        g = first_ref[e] + i
        slot = lax.rem(g, 2)
        in_copy(g, slot).wait()

        @pl.when(g + 1 < total)
        def _():
            in_copy(g + 1, 1 - slot).start()

        @pl.when(g >= 2)
        def _():
            out_copy(g - 2, slot).wait()

        lo, hi = _unpack_halves(xbuf[slot])
        lo = lo.astype(BF16)
        hi = hi.astype(BF16)
        a = _dot(lo, wg[:PACK_W]) + _dot(hi, wg[PACK_W:])
        b = _dot(lo, wu[:PACK_W]) + _dot(hi, wu[PACK_W:])
        act = (a * _sigmoid(a) * b).astype(BF16)
        ybuf[slot] = _pack_halves(_dot(act, wd))
        out_copy(g, slot).start()
        return carry

    lax.fori_loop(0, count_ref[e], tile_step, 0)

    @pl.when(e == last)
    def _():
        @pl.when(total >= 2)
        def _():
            out_copy(total - 2, lax.rem(total, 2)).wait()

        out_copy(total - 1, lax.rem(total - 1, 2)).wait()


def _routed_mlp(first_tile, n_tiles, xs, wg, wu, wd):
    R = xs.shape[0]
    any_space = pl.BlockSpec(memory_space=pl.ANY)
    return pl.pallas_call(
        _routed_kernel,
        grid_spec=pltpu.PrefetchScalarGridSpec(
            num_scalar_prefetch=2,
            grid=(N_EXPERTS,),
            in_specs=[any_space,
                      pl.BlockSpec((1, D_MODEL, D_EXPERT), lambda e, ft, nt: (e, 0, 0)),
                      pl.BlockSpec((1, D_MODEL, D_EXPERT), lambda e, ft, nt: (e, 0, 0)),
                      pl.BlockSpec((1, D_EXPERT, D_MODEL), lambda e, ft, nt: (e, 0, 0))],
            out_specs=any_space,
            scratch_shapes=[pltpu.VMEM((2, MOE_TM, PACK_W), I32), pltpu.VMEM((2, MOE_TM, PACK_W), I32),
                            pltpu.SemaphoreType.DMA((2,)), pltpu.SemaphoreType.DMA((2,))]),
        out_shape=jax.ShapeDtypeStruct((R, PACK_W), I32),
        compiler_params=pltpu.CompilerParams(dimension_semantics=("arbitrary",)),
        name="routed_mlp",
    )(first_tile, n_tiles, xs, wg, wu, wd)


def _combine_kernel(xp_ref, y0_ref, y1_ref, wtok_ref, lnw_ref, lnb_ref, *out_refs):
    o_ref = out_refs[-1]
    w0 = wtok_ref[:, 0:1]
    w1 = wtok_ref[:, 1:2]
    xlo, xhi = _unpack_halves(xp_ref[...])
    y0lo, y0hi = _unpack_halves(y0_ref[...])
    y1lo, y1hi = _unpack_halves(y1_ref[...])
    h = jnp.concatenate([ALPHA * xlo + (w0 * y0lo + w1 * y1lo), ALPHA * xhi + (w0 * y0hi + w1 * y1hi)], axis=1)
    o_ref[...] = _layer_norm(h, lnw_ref[...], lnb_ref[...])


def _combine_ln2(xp, yg_parts, wtok, lnw, lnb, tile=512):
    T = xp.shape[0]
    n_parts = len(yg_parts)
    nt = T // tile // n_parts
    full = lambda a: pl.BlockSpec(a.shape, lambda i: (0,) * a.ndim)
    out = None
    for p, yg in enumerate(yg_parts):
        rows = lambda w, p=p: pl.BlockSpec((tile, w), lambda i: (i + p * nt, 0))
        in_specs = [rows(PACK_W),
                    pl.BlockSpec((tile, PACK_W), lambda i: (i, 0)),
                    pl.BlockSpec((tile, PACK_W), lambda i: (i + nt, 0)),
                    rows(WTOK_LANES), full(lnw), full(lnb)]
        args = [xp, yg, yg, wtok, lnw, lnb]
        aliases = {}
        if out is not None:
            in_specs.append(pl.BlockSpec(memory_space=pl.ANY))
            args.append(out)
            aliases = {len(args) - 1: 0}
        out = pl.pallas_call(
            _combine_kernel,
            grid=(nt,),
            in_specs=in_specs,
            out_specs=rows(D_MODEL),
            out_shape=jax.ShapeDtypeStruct((T, D_MODEL), F32),
            input_output_aliases=aliases,
            name="combine_ln2",
        )(*args)
    return out


def _router_params(w_grp, b_grp, w_exp, b_exp):
    rw = jnp.zeros((ROUTER_ROWS, D_MODEL), F32)
    rw = rw.at[0:N_GROUPS].set(w_grp.T).at[EXP_ROW0:EXP_ROW0 + N_EXPERTS].set(w_exp.T)
    rb = jnp.zeros((ROUTER_ROWS,), F32).at[N_GROUPS:8].set(NEG_BIG)
    rb = rb.at[0:N_GROUPS].set(b_grp).at[EXP_ROW0:EXP_ROW0 + N_EXPERTS].set(b_exp)
    rw_hi = rw.astype(BF16)
    rw_lo = (rw - rw_hi.astype(F32)).astype(BF16)
    return (rw_hi, rw_lo), rb[:, None]


def kernel(x, mem, positions, w_in, w_pool_grp, pool_scale, ret_gn_w, w_mem_kv, w_br_pool, w_br_ret, w_br_xa,
           w_out, ln1_w, ln1_b, w_grp_router, b_grp_router, w_exp_router, b_exp_router, w_exp_gate, w_exp_up,
           w_exp_down, ln2_w, ln2_b):
    B, S, D = x.shape
    assert D == D_MODEL and w_in.shape[0] == DEPTH and S % 512 == 0
    T = B * S
    M = mem.shape[1]
    l = 0
    xf = x.reshape(T, D)

    rope = _rope_table(positions.reshape(1, T))
    kv = _mem_kv(mem.reshape(B * M, D), w_mem_kv[l].astype(BF16))
    rw, rb = _router_params(w_grp_router[l], b_grp_router[l], w_exp_router[l], b_exp_router[l])
    xp, eid, wtok = _mixer(xf, rope, kv, w_in[l].astype(BF16),
                           w_pool_grp[l].astype(BF16), pool_scale[l][None, :], ret_gn_w[l].reshape(1, -1),
                           w_br_pool[l].astype(BF16), w_br_ret[l].astype(BF16), w_br_xa[l].astype(BF16),
                           w_out[l].astype(BF16), ln1_w[l][None, :], ln1_b[l][None, :], rw, rb, B, S, M)

    pos, first_tile, n_tiles = _positions(eid)
    pos2d = pos[0:TOP_K].reshape(TOP_K * T // SC_CHUNK, SC_CHUNK)
    max_tiles = (TOP_K * T + N_EXPERTS * (MOE_TM - 1)) // MOE_TM
    xs = _sc_scatter_rows(xp, pos2d, max_tiles * MOE_TM)
    ys = _routed_mlp(first_tile[:, 0], n_tiles[:, 0], xs,
                     w_exp_gate[l].reshape(N_EXPERTS, D_MODEL, D_EXPERT),
                     w_exp_up[l].reshape(N_EXPERTS, D_MODEL, D_EXPERT),
                     w_exp_down[l].reshape(N_EXPERTS, D_EXPERT, D_MODEL))
    rng = T // COMBINE_PARTS
    yg_parts = [_sc_gather_rows(ys, pos[0:TOP_K, p * rng:(p + 1) * rng].reshape(TOP_K * rng // SC_CHUNK, SC_CHUNK))
                for p in range(COMBINE_PARTS)]
    out = _combine_ln2(xp, yg_parts, wtok, ln2_w[l][None, :], ln2_b[l][None, :])
    return out.reshape(B, S, D)
```

```python
import functools

import numpy as np
import jax
import jax.numpy as jnp
from jax import lax
from jax.experimental import pallas as pl
from jax.experimental.pallas import tpu as pltpu
from jax.experimental.pallas import tpu_sc as plsc

F32 = jnp.float32
BF16 = jnp.bfloat16
I32 = jnp.int32
U32 = jnp.uint32

D_MODEL = 1024
POOL_WINDOWS = (2, 4, 8, 16)
POOL_GROUP_DIM = 128
POOL_WIDTH = 512
POOL_HALO = 16
RET_HEADS = 4
RET_QK_DIM = 128
RET_V_DIM = 256
RET_CHUNK = 128
ROPE_BASE = 10000.0
XA_HEADS = 4
XA_HEAD_DIM = 128
XA_WIDTH = 512
N_GROUPS = 4
EXPERTS_PER_GROUP = 8
N_EXPERTS = N_GROUPS * EXPERTS_PER_GROUP
D_EXPERT = 256
LN_EPS = 1e-5
DEPTH = 1
ALPHA = (2.0 * DEPTH) ** 0.25
NEG_BIG = -1e30

COL_POOL, COL_Q, COL_K, COL_V, COL_G, COL_XAQ, COL_GATES = 0, 512, 1024, 1536, 2560, 3584, 4096

VMEM_LIMIT = 56 * 1024 * 1024

TOP_K = 2
PACK_W = D_MODEL // 2
MOE_TM = 256
SC_CHUNK = 128
RANK_CHUNK = 512
COMBINE_PARTS = 2
STRIP = 256
LN_ROWS = 32


def _dot(a, b):
    return jnp.dot(a, b, preferred_element_type=F32)


def _dot_nt(a, b, precision=None):
    return lax.dot_general(a, b, (((1,), (1,)), ((), ())), preferred_element_type=F32, precision=precision)


def _sigmoid(z):
    return 1.0 / (1.0 + jnp.exp2(z * (-1.0 / np.log(2.0))))


def _layer_norm(h, w, b):
    mu = jnp.mean(h, axis=-1, keepdims=True)
    hc = h - mu
    var = jnp.mean(hc * hc, axis=-1, keepdims=True)
    return hc * lax.rsqrt(var + LN_EPS) * w + b


def _rope_kernel(pos_ref, freq_ref, cos_ref, sin_ref, cos_t_ref, sin_t_ref):
    ang = freq_ref[...] * pos_ref[...].astype(F32)
    cos_t = jnp.cos(ang)
    sin_t = jnp.sin(ang)
    cos_t_ref[...] = cos_t
    sin_t_ref[...] = sin_t
    cos_ref[...] = jnp.transpose(jnp.concatenate([cos_t, cos_t], axis=0))
    sin_ref[...] = jnp.transpose(jnp.concatenate([-sin_t, sin_t], axis=0))


def _rope_table(pos_row, tile=2048):
    T = pos_row.shape[1]
    half = RET_QK_DIM // 2
    inv_freq = (ROPE_BASE ** (-np.arange(half, dtype=np.float64) / half)).astype(np.float32)
    freq = jnp.asarray(inv_freq[:, None])
    out = pl.BlockSpec((tile, RET_QK_DIM), lambda i: (i, 0))
    out_t = pl.BlockSpec((half, tile), lambda i: (0, i))
    return pl.pallas_call(
        _rope_kernel,
        grid=(T // tile,),
        in_specs=[pl.BlockSpec((1, tile), lambda i: (0, i)), pl.BlockSpec((half, 1), lambda i: (0, 0))],
        out_specs=[out, out, out_t, out_t],
        out_shape=[jax.ShapeDtypeStruct((T, RET_QK_DIM), F32)] * 2 + [jax.ShapeDtypeStruct((half, T), F32)] * 2,
        name="rope_table",
    )(pos_row, freq)


POOL_SUB = 256


def _pool_bands():
    r = np.arange(POOL_SUB)[:, None]
    c = np.arange(POOL_SUB)[None, :]
    ch = np.arange(POOL_HALO)[None, :] - POOL_HALO
    main = np.stack([((r - c >= 0) & (r - c < w)) for w in POOL_WINDOWS]).astype(np.float32)
    halo = np.stack([((r - ch >= 0) & (r - ch < w)) for w in POOL_WINDOWS]).astype(np.float32)
    return jnp.asarray(main, BF16), jnp.asarray(halo, BF16)


def _pool_branch(ub, j, halo_ref, bmain_ref, bhalo_ref, wg_ref, scale_ref, o_ref):
    tile = ub.shape[0]
    s0 = j * tile
    slot = lax.rem(j, 2)
    for sb in range(tile // POOL_SUB):
        r0 = sb * POOL_SUB
        u = ub[r0:r0 + POOL_SUB]
        prev = halo_ref[slot] if sb == 0 else ub[r0 - POOL_HALO:r0]
        pos = s0 + r0 + lax.broadcasted_iota(I32, (POOL_SUB, POOL_GROUP_DIM), 0)
        for g, w in enumerate(POOL_WINDOWS):
            cols = slice(g * POOL_GROUP_DIM, (g + 1) * POOL_GROUP_DIM)
            ug = u[:, cols]
            wsum = _dot(bmain_ref[g], ug) + _dot(bhalo_ref[g], prev[:, cols])
            cnt = jnp.minimum(pos + 1, w).astype(F32)
            pooled = wsum / cnt - ug.astype(F32)
            mixed = _dot(pooled.astype(BF16), wg_ref[g]) * scale_ref[:, cols]
            o_ref[r0:r0 + POOL_SUB, cols] = mixed.astype(BF16)
    halo_ref[1 - slot] = ub[tile - POOL_HALO:tile]


def _ret_consts():
    h = np.arange(RET_HEADS, dtype=np.float64)
    log_gamma = np.log1p(-np.exp2(-5.0 - h))
    pos = np.arange(RET_CHUNK, dtype=np.float64)
    diff = pos[:, None] - pos[None, :]
    kscale = RET_QK_DIM ** -0.5
    dmask = kscale * np.where(diff >= 0, np.exp(log_gamma[:, None, None] * np.maximum(diff, 0.0)), 0.0)
    qdec = np.exp(log_gamma[:, None] * (pos + 1.0)[None, :])
    kdec = kscale * np.exp(log_gamma[:, None] * (RET_CHUNK - 1.0 - pos)[None, :])
    cdec = np.exp(log_gamma * RET_CHUNK)
    lanes = lambda a: np.broadcast_to(a[:, :, None], (RET_HEADS, RET_CHUNK, RET_QK_DIM))
    kdec_t = np.broadcast_to(kdec[:, None, :], (RET_HEADS, RET_QK_DIM, RET_CHUNK))
    return (jnp.asarray(dmask, F32), jnp.asarray(lanes(qdec), F32), jnp.asarray(kdec_t, F32),
            tuple(float(v) for v in cdec))


def _retention_branch(q, k_t, v, silu_g, cos_ref, sin_ref, cos_t_ref, sin_t_ref, dmask_ref, qdec_ref, kdec_ref,
                      gnw_ref, state_ref, rq_ref, rqd_ref, rkt_ref, rkdt_ref, o_ref, cdec, fillers):
    tile = q.shape[0]
    n_chunks = tile // RET_CHUNK
    half = RET_QK_DIM // 2
    cos = cos_ref[...]
    sin = sin_ref[...]
    cos_t = cos_t_ref[...]
    sin_t = sin_t_ref[...]
    for h in range(RET_HEADS):
        qk = slice(h * RET_QK_DIM, (h + 1) * RET_QK_DIM)
        qh = q[:, qk]
        qr = qh * cos + pltpu.roll(qh, half, 1) * sin
        rq_ref[:, qk] = qr.astype(BF16)
        rqd_ref[:, qk] = (qr * jnp.concatenate([qdec_ref[h]] * n_chunks, axis=0)).astype(BF16)
        k1 = k_t[h * RET_QK_DIM:h * RET_QK_DIM + half]
        k2 = k_t[h * RET_QK_DIM + half:(h + 1) * RET_QK_DIM]
        kr_t = jnp.concatenate([k1 * cos_t - k2 * sin_t, k2 * cos_t + k1 * sin_t], axis=0)
        rkt_ref[qk, :] = kr_t.astype(BF16)
        rkdt_ref[qk, :] = (kr_t * jnp.concatenate([kdec_ref[h]] * n_chunks, axis=1)).astype(BF16)

    for c in range(n_chunks):
        rows = slice(c * RET_CHUNK, (c + 1) * RET_CHUNK)
        heads = [slice(h * RET_QK_DIM, (h + 1) * RET_QK_DIM) for h in range(RET_HEADS)]
        raw = [_dot(rq_ref[rows, qk], rkt_ref[qk, rows]) for qk in heads]
        fillers[c]()
        for h in range(RET_HEADS):
            qk = heads[h]
            v_cols = slice(h * RET_V_DIM, (h + 1) * RET_V_DIM)
            vh = v[rows, v_cols]
            scores = raw[h] * dmask_ref[h]
            st = state_ref[h]
            lhs = jnp.concatenate([scores.astype(BF16), rqd_ref[rows, qk]], axis=1)
            y = _dot(lhs, jnp.concatenate([vh, st.astype(BF16)], axis=0))
            state_ref[h] = cdec[h] * st + _dot(rkdt_ref[qk, rows], vh)
            mu = jnp.mean(y, axis=-1, keepdims=True)
            yc = y - mu
            var = jnp.mean(yc * yc, axis=-1, keepdims=True)
            yn = yc * lax.rsqrt(var + LN_EPS) * gnw_ref[:, v_cols]
            o_ref[rows, v_cols] = (silu_g[h][rows] * yn).astype(BF16)


def _memkv_kernel(m_ref, w_ref, o_ref):
    o_ref[...] = _dot(m_ref[...].astype(BF16), w_ref[...]).astype(BF16)


def _mem_kv(memf, w_b):
    M, D = memf.shape
    N = w_b.shape[1]
    return pl.pallas_call(
        _memkv_kernel,
        grid=(1,),
        in_specs=[pl.BlockSpec((M, D), lambda i: (0, 0)), pl.BlockSpec((D, N), lambda i: (0, 0))],
        out_specs=pl.BlockSpec((M, N), lambda i: (0, 0)),
        out_shape=jax.ShapeDtypeStruct((M, N), BF16),
        name="mem_kv",
    )(memf, w_b)


def _cross_attention_branch(xq, k_ref, v_ref, o_ref):
    scale = XA_HEAD_DIM ** -0.5
    for h in range(XA_HEADS):
        cols = slice(h * XA_HEAD_DIM, (h + 1) * XA_HEAD_DIM)
        s = _dot_nt(xq[:, cols], k_ref[:, cols]) * scale
        m = jnp.max(s, axis=-1, keepdims=True)
        p = jnp.exp(s - m)
        l = jnp.sum(p, axis=-1, keepdims=True)
        o = _dot(p.astype(BF16), v_ref[:, cols]) / l
        o_ref[:, cols] = o.astype(BF16)


ROUTER_ROWS = 40
WTOK_LANES = 128
EXP_ROW0 = 8


def _route(logits_t):
    gl = logits_t[0:8]
    gmax = jnp.max(gl, axis=0, keepdims=True)
    p_grp = 1.0 / jnp.sum(jnp.exp(gl - gmax), axis=0, keepdims=True)
    idx8 = lax.broadcasted_iota(jnp.int32, gl.shape, 0)
    gsel = jnp.min(jnp.where(gl == gmax, idx8, 8), axis=0, keepdims=True)
    cl = jnp.zeros_like(gl)
    for g in range(N_GROUPS):
        r0 = EXP_ROW0 + g * EXPERTS_PER_GROUP
        cl = cl + jnp.where(gsel == g, logits_t[r0:r0 + EXPERTS_PER_GROUP], 0.0)
    v1 = jnp.max(cl, axis=0, keepdims=True)
    i1 = jnp.min(jnp.where(cl == v1, idx8, 8), axis=0, keepdims=True)
    cl2 = jnp.where(idx8 == i1, -jnp.inf, cl)
    v2 = jnp.max(cl2, axis=0, keepdims=True)
    i2 = jnp.min(jnp.where(cl2 == v2, idx8, 8), axis=0, keepdims=True)
    e21 = jnp.exp(v2 - v1)
    w1 = p_grp / (1.0 + e21)
    w2 = p_grp * e21 / (1.0 + e21)
    return gsel * EXPERTS_PER_GROUP + i1, gsel * EXPERTS_PER_GROUP + i2, w1, w2


def _pack_halves(v):
    half = v.shape[1] // 2
    lo = lax.bitcast_convert_type(v[:, :half].astype(BF16).astype(F32), U32)
    hi = lax.bitcast_convert_type(v[:, half:].astype(BF16).astype(F32), U32)
    return lax.bitcast_convert_type(lax.shift_right_logical(lo, U32(16)) | hi, I32)


def _unpack_halves(w):
    u = lax.bitcast_convert_type(w, U32)
    lo = lax.bitcast_convert_type(lax.shift_left(u, U32(16)), F32)
    hi = lax.bitcast_convert_type(u & U32(0xFFFF0000), F32)
    return lo, hi


def _mixer_kernel(x_ref, cos_ref, sin_ref, cos_t_ref, sin_t_ref, km_ref, vm_ref, win_ref, wgrp_ref,
                  pscale_ref, bmain_ref, bhalo_ref, dmask_ref, qdec_ref, kdec_ref, gnw_ref, wp_ref, wr_ref,
                  wa_ref, wo_ref, lnw_ref, lnb_ref, rwh_ref, rwl_ref, rb_ref, xp_ref, eid_ref, wtok_ref,
                  state_ref, halo_ref, ypool_ref, yret_ref, yxa_ref, rq_ref, rqd_ref, rkt_ref, rkdt_ref, wkt_ref,
                  *, tile, cdec):
    j = pl.program_id(1)

    @pl.when(jnp.logical_and(pl.program_id(0) == 0, j == 0))
    def _():
        wk = win_ref[:, COL_K:COL_K + RET_HEADS * RET_QK_DIM].astype(F32)
        wkt_ref[...] = jnp.transpose(wk).astype(BF16)

    @pl.when(j == 0)
    def _():
        state_ref[...] = jnp.zeros_like(state_ref)
        halo_ref[...] = jnp.zeros_like(halo_ref)

    x = x_ref[...]
    xb = x.astype(BF16)

    def proj(col, width):
        return _dot(xb, win_ref[:, col:col + width])

    part = {}
    strips = [slice(c, c + STRIP) for c in range(0, D_MODEL, STRIP)]

    def gate(branch, cols):
        return _sigmoid(proj(COL_GATES + branch * D_MODEL + cols.start, STRIP))

    def pool_part():
        _pool_branch(proj(COL_POOL, POOL_WIDTH).astype(BF16), j, halo_ref, bmain_ref, bhalo_ref, wgrp_ref,
                     pscale_ref, ypool_ref)
        part["pool"] = [gate(0, c) * _dot(ypool_ref[...], wp_ref[:, c]) for c in strips]

    def xa_part():
        _cross_attention_branch(proj(COL_XAQ, XA_WIDTH).astype(BF16), km_ref, vm_ref, yxa_ref)
        part["xa"] = [gate(2, c) * _dot(yxa_ref[...], wa_ref[:, c]) for c in strips]

    def ret_gate_part():
        part["ret_gate"] = [gate(1, c) for c in strips]

    silu_g = []
    for h in range(RET_HEADS):
        gh = proj(COL_G + h * RET_V_DIM, RET_V_DIM)
        silu_g.append(gh * _sigmoid(gh))
    fillers = [pool_part, xa_part, ret_gate_part] + [lambda: None] * (tile // RET_CHUNK - 3)
    _retention_branch(proj(COL_Q, RET_HEADS * RET_QK_DIM), _dot_nt(wkt_ref[...], xb),
                      proj(COL_V, RET_HEADS * RET_V_DIM).astype(BF16), silu_g, cos_ref, sin_ref, cos_t_ref,
                      sin_t_ref, dmask_ref, qdec_ref, kdec_ref, gnw_ref, state_ref, rq_ref, rqd_ref, rkt_ref,
                      rkdt_ref, yret_ref, cdec, fillers)
    merged = jnp.concatenate(
        [(part["pool"][i] + part["ret_gate"][i] * _dot(yret_ref[...], wr_ref[:, c]) + part["xa"][i]).astype(BF16)
         for i, c in enumerate(strips)], axis=1)
    h = jnp.concatenate([ALPHA * x[:, c] + _dot(merged, wo_ref[:, c]) for c in strips], axis=1)
    x1 = jnp.concatenate([_layer_norm(h[r:r + LN_ROWS], lnw_ref[...], lnb_ref[...])
                          for r in range(0, tile, LN_ROWS)], axis=0)
    xp_ref[...] = _pack_halves(x1)
    x1_hi = x1.astype(BF16)
    x1_lo = (x1 - x1_hi.astype(F32)).astype(BF16)
    logits_t = (_dot_nt(rwh_ref[...], x1_hi) + _dot_nt(rwh_ref[...], x1_lo) + _dot_nt(rwl_ref[...], x1_hi)
                + rb_ref[...])
    e0, e1, w0, w1 = _route(logits_t)
    eid_ref[...] = jnp.concatenate([e0, e1, jnp.zeros((8 - TOP_K, tile), I32)], axis=0)
    w_t = jnp.concatenate([w0, w1, jnp.zeros((WTOK_LANES - TOP_K, tile), F32)], axis=0)
    wtok_ref[...] = jnp.transpose(w_t)


def _mixer(xf, rope, kv, win, wgrp, pscale, gnw, wp, wr, wa, wo, lnw, lnb, rw, rb, batch, seq, mem_len,
           tile=512):
    T = xf.shape[0]
    nj = seq // tile
    cos, sin, cos_t, sin_t = rope
    bmain, bhalo = _pool_bands()
    dmask, qdec, kdec, cdec = _ret_consts()
    resident = lambda a: pl.BlockSpec(a.shape, lambda b, j: (0,) * a.ndim, pipeline_mode=pl.Buffered(1))
    rowblk = lambda w: pl.BlockSpec((tile, w), lambda b, j: (b * nj + j, 0))
    colblk = lambda r: pl.BlockSpec((r, tile), lambda b, j: (0, b * nj + j))
    consts = (win, wgrp, pscale, bmain, bhalo, dmask, qdec, kdec, gnw, wp, wr, wa, wo, lnw, lnb, *rw, rb)
    return pl.pallas_call(
        functools.partial(_mixer_kernel, tile=tile, cdec=cdec),
        grid=(batch, nj),
        in_specs=[rowblk(D_MODEL), rowblk(RET_QK_DIM), rowblk(RET_QK_DIM),
                  colblk(RET_QK_DIM // 2), colblk(RET_QK_DIM // 2),
                  pl.BlockSpec((mem_len, XA_WIDTH), lambda b, j: (b, 0)),
                  pl.BlockSpec((mem_len, XA_WIDTH), lambda b, j: (b, 1))] + [resident(a) for a in consts],
        out_specs=[rowblk(PACK_W), colblk(8), rowblk(WTOK_LANES)],
        out_shape=[jax.ShapeDtypeStruct((T, PACK_W), I32), jax.ShapeDtypeStruct((8, T), I32),
                   jax.ShapeDtypeStruct((T, WTOK_LANES), F32)],
        scratch_shapes=[pltpu.VMEM((RET_HEADS, RET_QK_DIM, RET_V_DIM), F32),
                        pltpu.VMEM((2, POOL_HALO, POOL_WIDTH), BF16),
                        pltpu.VMEM((tile, POOL_WIDTH), BF16),
                        pltpu.VMEM((tile, RET_HEADS * RET_V_DIM), BF16),
                        pltpu.VMEM((tile, XA_WIDTH), BF16),
                        pltpu.VMEM((tile, RET_HEADS * RET_QK_DIM), BF16),
                        pltpu.VMEM((tile, RET_HEADS * RET_QK_DIM), BF16),
                        pltpu.VMEM((RET_HEADS * RET_QK_DIM, tile), BF16),
                        pltpu.VMEM((RET_HEADS * RET_QK_DIM, tile), BF16),
                        pltpu.VMEM((RET_HEADS * RET_QK_DIM, D_MODEL), BF16)],
        compiler_params=pltpu.CompilerParams(dimension_semantics=("arbitrary", "arbitrary"),
                                             vmem_limit_bytes=VMEM_LIMIT),
        name="mixer",
    )(xf, cos, sin, cos_t, sin_t, kv, kv, *consts)


META_LANES = 128


def _positions_kernel(eid_ref, tri_ref, low_ref, pos_ref, first_tile_ref, n_tiles_ref, *, n_tok):
    n_chunks = n_tok // RANK_CHUNK
    erow = lax.broadcasted_iota(I32, (N_EXPERTS, RANK_CHUNK), 0)

    def onehot(c):
        sl = slice(c * RANK_CHUNK, (c + 1) * RANK_CHUNK)
        m0 = eid_ref[0:1, sl] == erow
        m1 = eid_ref[1:2, sl] == erow
        return m0, m1, jnp.where(m0, 1.0, 0.0) + jnp.where(m1, 1.0, 0.0)

    counts = jnp.zeros((N_EXPERTS, 1), F32)
    for c in range(n_chunks):
        counts = counts + jnp.sum(onehot(c)[2], axis=1, keepdims=True)
    ptiles = jnp.floor((counts + (MOE_TM - 1)) * (1.0 / MOE_TM))
    ptiles_b = jnp.broadcast_to(ptiles, (N_EXPERTS, 128)).astype(BF16)
    start = _dot(low_ref[...], ptiles_b)[:, 0:1] * MOE_TM

    pos_ref[...] = jnp.zeros_like(pos_ref)
    carry = start - 1.0
    for c in range(n_chunks):
        sl = slice(c * RANK_CHUNK, (c + 1) * RANK_CHUNK)
        m0, m1, oh = onehot(c)
        rank = _dot(oh.astype(BF16), tri_ref[...]) + carry
        pos_ref[0:1, sl] = jnp.sum(jnp.where(m0, rank, 0.0), axis=0, keepdims=True).astype(I32)
        pos_ref[1:2, sl] = jnp.sum(jnp.where(m1, rank, 0.0), axis=0, keepdims=True).astype(I32)
        carry = carry + jnp.sum(oh, axis=1, keepdims=True)

    first_tile_ref[...] = jnp.broadcast_to(start * (1.0 / MOE_TM), first_tile_ref.shape).astype(I32)
    n_tiles_ref[...] = jnp.broadcast_to(ptiles, n_tiles_ref.shape).astype(I32)


def _positions(eid):
    T = eid.shape[1]
    r = np.arange(RANK_CHUNK)
    tri = jnp.asarray(r[:, None] <= r[None, :], BF16)
    e = np.arange(N_EXPERTS)
    low = jnp.asarray(e[None, :] < e[:, None], BF16)
    full = lambda a: pl.BlockSpec(a.shape, lambda i: (0,) * a.ndim)
    return pl.pallas_call(
        functools.partial(_positions_kernel, n_tok=T),
        grid=(1,),
        in_specs=[full(eid), full(tri), full(low)],
        out_specs=[pl.BlockSpec((8, T), lambda i: (0, 0)),
                   pl.BlockSpec((N_EXPERTS, META_LANES), lambda i: (0, 0)),
                   pl.BlockSpec((N_EXPERTS, META_LANES), lambda i: (0, 0))],
        out_shape=[jax.ShapeDtypeStruct((8, T), I32), jax.ShapeDtypeStruct((N_EXPERTS, META_LANES), I32),
                   jax.ShapeDtypeStruct((N_EXPERTS, META_LANES), I32)],
        name="route_positions",
    )(eid, tri, low)


def _sc_workers():
    info = plsc.get_sparse_core_info()
    return info.num_cores, info.num_cores * info.num_subcores


def _sc_scatter_rows(xp, pos2d, n_out):
    T, W = xp.shape
    n_cores, n_workers = _sc_workers()
    cpw = T // SC_CHUNK // n_workers
    mesh = plsc.VectorSubcoreMesh(core_axis_name="c", subcore_axis_name="s")

    @functools.partial(
        pl.kernel, mesh=mesh, out_type=jax.ShapeDtypeStruct((n_out, W), I32),
        scratch_types=[pltpu.VMEM((TOP_K * cpw, SC_CHUNK), I32), pltpu.VMEM((SC_CHUNK, W), I32)],
        name="sc_scatter_rows")
    def k(x_hbm, pos_hbm, out_hbm, idx_v, rows_v):
        wid = lax.axis_index("s") * n_cores + lax.axis_index("c")
        for s in range(TOP_K):
            pltpu.sync_copy(pos_hbm.at[pl.ds(s * (T // SC_CHUNK) + wid * cpw, cpw)],
                            idx_v.at[pl.ds(s * cpw, cpw)])
        for j in range(cpw):
            pltpu.sync_copy(x_hbm.at[pl.ds((wid * cpw + j) * SC_CHUNK, SC_CHUNK)], rows_v)
            for s in range(TOP_K):
                pltpu.sync_copy(rows_v, out_hbm.at[idx_v.at[s * cpw + j]])

    return k(xp, pos2d)


def _sc_gather_rows(y, idx2d):
    W = y.shape[1]
    n = idx2d.shape[0] * SC_CHUNK
    n_cores, n_workers = _sc_workers()
    cpw = n // SC_CHUNK // n_workers
    mesh = plsc.VectorSubcoreMesh(core_axis_name="c", subcore_axis_name="s")

    @functools.partial(
        pl.kernel, mesh=mesh, out_type=jax.ShapeDtypeStruct((n, W), I32),
        scratch_types=[pltpu.VMEM((cpw, SC_CHUNK), I32), pltpu.VMEM((SC_CHUNK, W), I32)],
        name="sc_gather_rows")
    def k(y_hbm, idx_hbm, out_hbm, idx_v, rows_v):
        wid = lax.axis_index("s") * n_cores + lax.axis_index("c")
        pltpu.sync_copy(idx_hbm.at[pl.ds(wid * cpw, cpw)], idx_v)
        for j in range(cpw):
            pltpu.sync_copy(y_hbm.at[idx_v.at[j]], rows_v)
            pltpu.sync_copy(rows_v, out_hbm.at[pl.ds((wid * cpw + j) * SC_CHUNK, SC_CHUNK)])

    return k(y, idx2d)


def _routed_kernel(first_ref, count_ref, xs_hbm, wg_ref, wu_ref, wd_ref, ys_hbm, xbuf, ybuf, in_sem, out_sem):
    e = pl.program_id(0)
    last = pl.num_programs(0) - 1
    total = first_ref[last] + count_ref[last]

    def in_copy(g, slot):
        return pltpu.make_async_copy(xs_hbm.at[pl.ds(g * MOE_TM, MOE_TM)], xbuf.at[slot], in_sem.at[slot])

    def out_copy(g, slot):
        return pltpu.make_async_copy(ybuf.at[slot], ys_hbm.at[pl.ds(g * MOE_TM, MOE_TM)], out_sem.at[slot])

    @pl.when(e == 0)
    def _():
        in_copy(0, 0).start()

    wg = wg_ref[0].astype(BF16)
    wu = wu_ref[0].astype(BF16)
    wd = wd_ref[0].astype(BF16)

    def tile_step(i, carry):
        g = first_ref[e] + i
        slot = lax.rem(g, 2)
        in_copy(g, slot).wait()

        @pl.when(g + 1 < total)
        def _():
            in_copy(g + 1, 1 - slot).start()

        @pl.when(g >= 2)
        def _():
            out_copy(g - 2, slot).wait()

        lo, hi = _unpack_halves(xbuf[slot])
        lo = lo.astype(BF16)
        hi = hi.astype(BF16)
        a = _dot(lo, wg[:PACK_W]) + _dot(hi, wg[PACK_W:])
        b = _dot(lo, wu[:PACK_W]) + _dot(hi, wu[PACK_W:])
        act = (a * _sigmoid(a) * b).astype(BF16)
        ybuf[slot] = _pack_halves(_dot(act, wd))
        out_copy(g, slot).start()
        return carry

    lax.fori_loop(0, count_ref[e], tile_step, 0)

    @pl.when(e == last)
    def _():
        @pl.when(total >= 2)
        def _():
            out_copy(total - 2, lax.rem(total, 2)).wait()

        out_copy(total - 1, lax.rem(total - 1, 2)).wait()


def _routed_mlp(first_tile, n_tiles, xs, wg, wu, wd):
    R = xs.shape[0]
    any_space = pl.BlockSpec(memory_space=pl.ANY)
    return pl.pallas_call(
        _routed_kernel,
        grid_spec=pltpu.PrefetchScalarGridSpec(
            num_scalar_prefetch=2,
            grid=(N_EXPERTS,),
            in_specs=[any_space,
                      pl.BlockSpec((1, D_MODEL, D_EXPERT), lambda e, ft, nt: (e, 0, 0)),
                      pl.BlockSpec((1, D_MODEL, D_EXPERT), lambda e, ft, nt: (e, 0, 0)),
                      pl.BlockSpec((1, D_EXPERT, D_MODEL), lambda e, ft, nt: (e, 0, 0))],
            out_specs=any_space,
            scratch_shapes=[pltpu.VMEM((2, MOE_TM, PACK_W), I32), pltpu.VMEM((2, MOE_TM, PACK_W), I32),
                            pltpu.SemaphoreType.DMA((2,)), pltpu.SemaphoreType.DMA((2,))]),
        out_shape=jax.ShapeDtypeStruct((R, PACK_W), I32),
        compiler_params=pltpu.CompilerParams(dimension_semantics=("arbitrary",)),
        name="routed_mlp",
    )(first_tile, n_tiles, xs, wg, wu, wd)


def _combine_kernel(xp_ref, y0_ref, y1_ref, wtok_ref, lnw_ref, lnb_ref, *out_refs):
    o_ref = out_refs[-1]
    w0 = wtok_ref[:, 0:1]
    w1 = wtok_ref[:, 1:2]
    xlo, xhi = _unpack_halves(xp_ref[...])
    y0lo, y0hi = _unpack_halves(y0_ref[...])
    y1lo, y1hi = _unpack_halves(y1_ref[...])
    h = jnp.concatenate([ALPHA * xlo + (w0 * y0lo + w1 * y1lo), ALPHA * xhi + (w0 * y0hi + w1 * y1hi)], axis=1)
    o_ref[...] = _layer_norm(h, lnw_ref[...], lnb_ref[...])


def _combine_ln2(xp, yg_parts, wtok, lnw, lnb, tile=512):
    T = xp.shape[0]
    n_parts = len(yg_parts)
    nt = T // tile // n_parts
    full = lambda a: pl.BlockSpec(a.shape, lambda i: (0,) * a.ndim)
    out = None
    for p, yg in enumerate(yg_parts):
        rows = lambda w, p=p: pl.BlockSpec((tile, w), lambda i: (i + p * nt, 0))
        in_specs = [rows(PACK_W),
                    pl.BlockSpec((tile, PACK_W), lambda i: (i, 0)),
                    pl.BlockSpec((tile, PACK_W), lambda i: (i + nt, 0)),
                    rows(WTOK_LANES), full(lnw), full(lnb)]
        args = [xp, yg, yg, wtok, lnw, lnb]
        aliases = {}
        if out is not None:
            in_specs.append(pl.BlockSpec(memory_space=pl.ANY))
            args.append(out)
            aliases = {len(args) - 1: 0}
        out = pl.pallas_call(
            _combine_kernel,
            grid=(nt,),
            in_specs=in_specs,
            out_specs=rows(D_MODEL),
            out_shape=jax.ShapeDtypeStruct((T, D_MODEL), F32),
            input_output_aliases=aliases,
            name="combine_ln2",
        )(*args)
    return out


def _router_params(w_grp, b_grp, w_exp, b_exp):
    rw = jnp.zeros((ROUTER_ROWS, D_MODEL), F32)
    rw = rw.at[0:N_GROUPS].set(w_grp.T).at[EXP_ROW0:EXP_ROW0 + N_EXPERTS].set(w_exp.T)
    rb = jnp.zeros((ROUTER_ROWS,), F32).at[N_GROUPS:8].set(NEG_BIG)
    rb = rb.at[0:N_GROUPS].set(b_grp).at[EXP_ROW0:EXP_ROW0 + N_EXPERTS].set(b_exp)
    rw_hi = rw.astype(BF16)
    rw_lo = (rw - rw_hi.astype(F32)).astype(BF16)
    return (rw_hi, rw_lo), rb[:, None]


def kernel(x, mem, positions, w_in, w_pool_grp, pool_scale, ret_gn_w, w_mem_kv, w_br_pool, w_br_ret, w_br_xa,
           w_out, ln1_w, ln1_b, w_grp_router, b_grp_router, w_exp_router, b_exp_router, w_exp_gate, w_exp_up,
           w_exp_down, ln2_w, ln2_b):
    B, S, D = x.shape
    assert D == D_MODEL and w_in.shape[0] == DEPTH and S % 512 == 0
    T = B * S
    M = mem.shape[1]
    l = 0
    xf = x.reshape(T, D)

    rope = _rope_table(positions.reshape(1, T))
    kv = _mem_kv(mem.reshape(B * M, D), w_mem_kv[l].astype(BF16))
    rw, rb = _router_params(w_grp_router[l], b_grp_router[l], w_exp_router[l], b_exp_router[l])
    xp, eid, wtok = _mixer(xf, rope, kv, w_in[l].astype(BF16),
                           w_pool_grp[l].astype(BF16), pool_scale[l][None, :], ret_gn_w[l].reshape(1, -1),
                           w_br_pool[l].astype(BF16), w_br_ret[l].astype(BF16), w_br_xa[l].astype(BF16),
                           w_out[l].astype(BF16), ln1_w[l][None, :], ln1_b[l][None, :], rw, rb, B, S, M)

    pos, first_tile, n_tiles = _positions(eid)
    pos2d = pos[0:TOP_K].reshape(TOP_K * T // SC_CHUNK, SC_CHUNK)
    max_tiles = (TOP_K * T + N_EXPERTS * (MOE_TM - 1)) // MOE_TM
    xs = _sc_scatter_rows(xp, pos2d, max_tiles * MOE_TM)
    ys = _routed_mlp(first_tile[:, 0], n_tiles[:, 0], xs,
                     w_exp_gate[l].reshape(N_EXPERTS, D_MODEL, D_EXPERT),
                     w_exp_up[l].reshape(N_EXPERTS, D_MODEL, D_EXPERT),
                     w_exp_down[l].reshape(N_EXPERTS, D_EXPERT, D_MODEL))
    rng = T // COMBINE_PARTS
    yg_parts = [_sc_gather_rows(ys, pos[0:TOP_K, p * rng:(p + 1) * rng].reshape(TOP_K * rng // SC_CHUNK, SC_CHUNK))
                for p in range(COMBINE_PARTS)]
    out = _combine_ln2(xp, yg_parts, wtok, ln2_w[l][None, :], ln2_b[l][None, :])
    return out.reshape(B, S, D)
```

```python
import functools

import numpy as np
import jax
import jax.numpy as jnp
from jax import lax
from jax.experimental import pallas as pl
from jax.experimental.pallas import tpu as pltpu
from jax.experimental.pallas import tpu_sc as plsc

F32 = jnp.float32
BF16 = jnp.bfloat16
I32 = jnp.int32
U32 = jnp.uint32

D_MODEL = 1024
POOL_WINDOWS = (2, 4, 8, 16)
POOL_GROUP_DIM = 128
POOL_WIDTH = 512
POOL_HALO = 16
RET_HEADS = 4
RET_QK_DIM = 128
RET_V_DIM = 256
RET_CHUNK = 128
ROPE_BASE = 10000.0
XA_HEADS = 4
XA_HEAD_DIM = 128
XA_WIDTH = 512
N_GROUPS = 4
EXPERTS_PER_GROUP = 8
N_EXPERTS = N_GROUPS * EXPERTS_PER_GROUP
D_EXPERT = 256
LN_EPS = 1e-5
DEPTH = 1
ALPHA = (2.0 * DEPTH) ** 0.25
NEG_BIG = -1e30

COL_POOL, COL_Q, COL_K, COL_V, COL_G, COL_XAQ, COL_GATES = 0, 512, 1024, 1536, 2560, 3584, 4096

VMEM_LIMIT = 56 * 1024 * 1024

TOP_K = 2
PACK_W = D_MODEL // 2
MOE_TM = 512
SC_CHUNK = 128
RANK_CHUNK = 512
COMBINE_PARTS = 2
RING = 4
STRIP = 256
LN_ROWS = 32


def _dot(a, b):
    return jnp.dot(a, b, preferred_element_type=F32)


def _dot_nt(a, b, precision=None):
    return lax.dot_general(a, b, (((1,), (1,)), ((), ())), preferred_element_type=F32, precision=precision)


def _sigmoid(z):
    return 1.0 / (1.0 + jnp.exp2(z * (-1.0 / np.log(2.0))))


def _layer_norm(h, w, b):
    mu = jnp.mean(h, axis=-1, keepdims=True)
    hc = h - mu
    var = jnp.mean(hc * hc, axis=-1, keepdims=True)
    return hc * lax.rsqrt(var + LN_EPS) * w + b


def _rope_kernel(pos_ref, freq_ref, cos_ref, sin_ref, cos_t_ref, sin_t_ref):
    ang = freq_ref[...] * pos_ref[...].astype(F32)
    cos_t = jnp.cos(ang)
    sin_t = jnp.sin(ang)
    cos_t_ref[...] = cos_t
    sin_t_ref[...] = sin_t
    cos_ref[...] = jnp.transpose(jnp.concatenate([cos_t, cos_t], axis=0))
    sin_ref[...] = jnp.transpose(jnp.concatenate([-sin_t, sin_t], axis=0))


def _rope_table(pos_row, tile=2048):
    T = pos_row.shape[1]
    half = RET_QK_DIM // 2
    inv_freq = (ROPE_BASE ** (-np.arange(half, dtype=np.float64) / half)).astype(np.float32)
    freq = jnp.asarray(inv_freq[:, None])
    out = pl.BlockSpec((tile, RET_QK_DIM), lambda i: (i, 0))
    out_t = pl.BlockSpec((half, tile), lambda i: (0, i))
    return pl.pallas_call(
        _rope_kernel,
        grid=(T // tile,),
        in_specs=[pl.BlockSpec((1, tile), lambda i: (0, i)), pl.BlockSpec((half, 1), lambda i: (0, 0))],
        out_specs=[out, out, out_t, out_t],
        out_shape=[jax.ShapeDtypeStruct((T, RET_QK_DIM), F32)] * 2 + [jax.ShapeDtypeStruct((half, T), F32)] * 2,
        name="rope_table",
    )(pos_row, freq)


POOL_SUB = 256


def _pool_bands():
    r = np.arange(POOL_SUB)[:, None]
    c = np.arange(POOL_SUB)[None, :]
    ch = np.arange(POOL_HALO)[None, :] - POOL_HALO
    main = np.stack([((r - c >= 0) & (r - c < w)) for w in POOL_WINDOWS]).astype(np.float32)
    halo = np.stack([((r - ch >= 0) & (r - ch < w)) for w in POOL_WINDOWS]).astype(np.float32)
    return jnp.asarray(main, BF16), jnp.asarray(halo, BF16)


def _pool_branch(ub, j, halo_ref, bmain_ref, bhalo_ref, wg_ref, scale_ref, o_ref):
    tile = ub.shape[0]
    s0 = j * tile
    slot = lax.rem(j, 2)
    for sb in range(tile // POOL_SUB):
        r0 = sb * POOL_SUB
        u = ub[r0:r0 + POOL_SUB]
        prev = halo_ref[slot] if sb == 0 else ub[r0 - POOL_HALO:r0]
        pos = s0 + r0 + lax.broadcasted_iota(I32, (POOL_SUB, POOL_GROUP_DIM), 0)
        for g, w in enumerate(POOL_WINDOWS):
            cols = slice(g * POOL_GROUP_DIM, (g + 1) * POOL_GROUP_DIM)
            ug = u[:, cols]
            wsum = _dot(bmain_ref[g], ug) + _dot(bhalo_ref[g], prev[:, cols])
            cnt = jnp.minimum(pos + 1, w).astype(F32)
            pooled = wsum / cnt - ug.astype(F32)
            mixed = _dot(pooled.astype(BF16), wg_ref[g]) * scale_ref[:, cols]
            o_ref[r0:r0 + POOL_SUB, cols] = mixed.astype(BF16)
    halo_ref[1 - slot] = ub[tile - POOL_HALO:tile]


def _ret_consts():
    h = np.arange(RET_HEADS, dtype=np.float64)
    log_gamma = np.log1p(-np.exp2(-5.0 - h))
    pos = np.arange(RET_CHUNK, dtype=np.float64)
    diff = pos[:, None] - pos[None, :]
    kscale = RET_QK_DIM ** -0.5
    dmask = kscale * np.where(diff >= 0, np.exp(log_gamma[:, None, None] * np.maximum(diff, 0.0)), 0.0)
    qdec = np.exp(log_gamma[:, None] * (pos + 1.0)[None, :])
    kdec = kscale * np.exp(log_gamma[:, None] * (RET_CHUNK - 1.0 - pos)[None, :])
    cdec = np.exp(log_gamma * RET_CHUNK)
    lanes = lambda a: np.broadcast_to(a[:, :, None], (RET_HEADS, RET_CHUNK, RET_QK_DIM))
    kdec_t = np.broadcast_to(kdec[:, None, :], (RET_HEADS, RET_QK_DIM, RET_CHUNK))
    return (jnp.asarray(dmask, F32), jnp.asarray(lanes(qdec), F32), jnp.asarray(kdec_t, F32),
            tuple(float(v) for v in cdec))


def _retention_branch(q, k_t, v, silu_g, cos_ref, sin_ref, cos_t_ref, sin_t_ref, dmask_ref, qdec_ref, kdec_ref,
                      gnw_ref, state_ref, rq_ref, rqd_ref, rkt_ref, rkdt_ref, o_ref, cdec, fillers):
    tile = q.shape[0]
    n_chunks = tile // RET_CHUNK
    half = RET_QK_DIM // 2
    cos = cos_ref[...]
    sin = sin_ref[...]
    cos_t = cos_t_ref[...]
    sin_t = sin_t_ref[...]
    for h in range(RET_HEADS):
        qk = slice(h * RET_QK_DIM, (h + 1) * RET_QK_DIM)
        qh = q[:, qk]
        qr = qh * cos + pltpu.roll(qh, half, 1) * sin
        rq_ref[:, qk] = qr.astype(BF16)
        rqd_ref[:, qk] = (qr * jnp.concatenate([qdec_ref[h]] * n_chunks, axis=0)).astype(BF16)
        k1 = k_t[h * RET_QK_DIM:h * RET_QK_DIM + half]
        k2 = k_t[h * RET_QK_DIM + half:(h + 1) * RET_QK_DIM]
        kr_t = jnp.concatenate([k1 * cos_t - k2 * sin_t, k2 * cos_t + k1 * sin_t], axis=0)
        rkt_ref[qk, :] = kr_t.astype(BF16)
        rkdt_ref[qk, :] = (kr_t * jnp.concatenate([kdec_ref[h]] * n_chunks, axis=1)).astype(BF16)

    for c in range(n_chunks):
        rows = slice(c * RET_CHUNK, (c + 1) * RET_CHUNK)
        heads = [slice(h * RET_QK_DIM, (h + 1) * RET_QK_DIM) for h in range(RET_HEADS)]
        raw = [_dot(rq_ref[rows, qk], rkt_ref[qk, rows]) for qk in heads]
        fillers[c]()
        for h in range(RET_HEADS):
            qk = heads[h]
            v_cols = slice(h * RET_V_DIM, (h + 1) * RET_V_DIM)
            vh = v[rows, v_cols]
            scores = raw[h] * dmask_ref[h]
            st = state_ref[h]
            lhs = jnp.concatenate([scores.astype(BF16), rqd_ref[rows, qk]], axis=1)
            y = _dot(lhs, jnp.concatenate([vh, st.astype(BF16)], axis=0))
            state_ref[h] = cdec[h] * st + _dot(rkdt_ref[qk, rows], vh)
            mu = jnp.mean(y, axis=-1, keepdims=True)
            yc = y - mu
            var = jnp.mean(yc * yc, axis=-1, keepdims=True)
            yn = yc * lax.rsqrt(var + LN_EPS) * gnw_ref[:, v_cols]
            o_ref[rows, v_cols] = (silu_g[h][rows] * yn).astype(BF16)


def _memkv_kernel(m_ref, w_ref, o_ref):
    o_ref[...] = _dot(m_ref[...].astype(BF16), w_ref[...]).astype(BF16)


def _mem_kv(memf, w_b):
    M, D = memf.shape
    N = w_b.shape[1]
    return pl.pallas_call(
        _memkv_kernel,
        grid=(1,),
        in_specs=[pl.BlockSpec((M, D), lambda i: (0, 0)), pl.BlockSpec((D, N), lambda i: (0, 0))],
        out_specs=pl.BlockSpec((M, N), lambda i: (0, 0)),
        out_shape=jax.ShapeDtypeStruct((M, N), BF16),
        name="mem_kv",
    )(memf, w_b)


def _cross_attention_branch(xq, k_ref, v_ref, o_ref):
    scale = XA_HEAD_DIM ** -0.5
    for h in range(XA_HEADS):
        cols = slice(h * XA_HEAD_DIM, (h + 1) * XA_HEAD_DIM)
        s = _dot_nt(xq[:, cols], k_ref[:, cols]) * scale
        m = jnp.max(s, axis=-1, keepdims=True)
        p = jnp.exp(s - m)
        l = jnp.sum(p, axis=-1, keepdims=True)
        o = _dot(p.astype(BF16), v_ref[:, cols]) / l
        o_ref[:, cols] = o.astype(BF16)


ROUTER_ROWS = 40
WTOK_LANES = 128
EXP_ROW0 = 8


def _route(logits_t):
    gl = logits_t[0:8]
    gmax = jnp.max(gl, axis=0, keepdims=True)
    p_grp = 1.0 / jnp.sum(jnp.exp(gl - gmax), axis=0, keepdims=True)
    idx8 = lax.broadcasted_iota(jnp.int32, gl.shape, 0)
    gsel = jnp.min(jnp.where(gl == gmax, idx8, 8), axis=0, keepdims=True)
    cl = jnp.zeros_like(gl)
    for g in range(N_GROUPS):
        r0 = EXP_ROW0 + g * EXPERTS_PER_GROUP
        cl = cl + jnp.where(gsel == g, logits_t[r0:r0 + EXPERTS_PER_GROUP], 0.0)
    v1 = jnp.max(cl, axis=0, keepdims=True)
    i1 = jnp.min(jnp.where(cl == v1, idx8, 8), axis=0, keepdims=True)
    cl2 = jnp.where(idx8 == i1, -jnp.inf, cl)
    v2 = jnp.max(cl2, axis=0, keepdims=True)
    i2 = jnp.min(jnp.where(cl2 == v2, idx8, 8), axis=0, keepdims=True)
    e21 = jnp.exp(v2 - v1)
    w1 = p_grp / (1.0 + e21)
    w2 = p_grp * e21 / (1.0 + e21)
    return gsel * EXPERTS_PER_GROUP + i1, gsel * EXPERTS_PER_GROUP + i2, w1, w2


def _pack_halves(v):
    half = v.shape[1] // 2
    lo = lax.bitcast_convert_type(v[:, :half].astype(BF16).astype(F32), U32)
    hi = lax.bitcast_convert_type(v[:, half:].astype(BF16).astype(F32), U32)
    return lax.bitcast_convert_type(lax.shift_right_logical(lo, U32(16)) | hi, I32)


def _unpack_halves(w):
    u = lax.bitcast_convert_type(w, U32)
    lo = lax.bitcast_convert_type(lax.shift_left(u, U32(16)), F32)
    hi = lax.bitcast_convert_type(u & U32(0xFFFF0000), F32)
    return lo, hi


def _mixer_kernel(x_ref, cos_ref, sin_ref, cos_t_ref, sin_t_ref, km_ref, vm_ref, win_ref, wgrp_ref,
                  pscale_ref, bmain_ref, bhalo_ref, dmask_ref, qdec_ref, kdec_ref, gnw_ref, wp_ref, wr_ref,
                  wa_ref, wo_ref, lnw_ref, lnb_ref, rwh_ref, rwl_ref, rb_ref, xp_ref, eid_ref, wtok_ref,
                  state_ref, halo_ref, ypool_ref, yret_ref, yxa_ref, rq_ref, rqd_ref, rkt_ref, rkdt_ref, wkt_ref,
                  *, tile, cdec):
    j = pl.program_id(1)

    @pl.when(jnp.logical_and(pl.program_id(0) == 0, j == 0))
    def _():
        wk = win_ref[:, COL_K:COL_K + RET_HEADS * RET_QK_DIM].astype(F32)
        wkt_ref[...] = jnp.transpose(wk).astype(BF16)

    @pl.when(j == 0)
    def _():
        state_ref[...] = jnp.zeros_like(state_ref)
        halo_ref[...] = jnp.zeros_like(halo_ref)

    x = x_ref[...]
    xb = x.astype(BF16)

    def proj(col, width):
        return _dot(xb, win_ref[:, col:col + width])

    part = {}
    strips = [slice(c, c + STRIP) for c in range(0, D_MODEL, STRIP)]

    def gate(branch, cols):
        return _sigmoid(proj(COL_GATES + branch * D_MODEL + cols.start, STRIP))

    def pool_part():
        _pool_branch(proj(COL_POOL, POOL_WIDTH).astype(BF16), j, halo_ref, bmain_ref, bhalo_ref, wgrp_ref,
                     pscale_ref, ypool_ref)
        part["pool"] = [gate(0, c) * _dot(ypool_ref[...], wp_ref[:, c]) for c in strips]

    def xa_part():
        _cross_attention_branch(proj(COL_XAQ, XA_WIDTH).astype(BF16), km_ref, vm_ref, yxa_ref)
        part["xa"] = [gate(2, c) * _dot(yxa_ref[...], wa_ref[:, c]) for c in strips]

    def ret_gate_part():
        part["ret_gate"] = [gate(1, c) for c in strips]

    silu_g = []
    for h in range(RET_HEADS):
        gh = proj(COL_G + h * RET_V_DIM, RET_V_DIM)
        silu_g.append(gh * _sigmoid(gh))
    fillers = [pool_part, xa_part, ret_gate_part] + [lambda: None] * (tile // RET_CHUNK - 3)
    _retention_branch(proj(COL_Q, RET_HEADS * RET_QK_DIM), _dot_nt(wkt_ref[...], xb),
                      proj(COL_V, RET_HEADS * RET_V_DIM).astype(BF16), silu_g, cos_ref, sin_ref, cos_t_ref,
                      sin_t_ref, dmask_ref, qdec_ref, kdec_ref, gnw_ref, state_ref, rq_ref, rqd_ref, rkt_ref,
                      rkdt_ref, yret_ref, cdec, fillers)
    merged = jnp.concatenate(
        [(part["pool"][i] + part["ret_gate"][i] * _dot(yret_ref[...], wr_ref[:, c]) + part["xa"][i]).astype(BF16)
         for i, c in enumerate(strips)], axis=1)
    h = jnp.concatenate([ALPHA * x[:, c] + _dot(merged, wo_ref[:, c]) for c in strips], axis=1)
    x1 = jnp.concatenate([_layer_norm(h[r:r + LN_ROWS], lnw_ref[...], lnb_ref[...])
                          for r in range(0, tile, LN_ROWS)], axis=0)
    xp_ref[...] = _pack_halves(x1)
    x1_hi = x1.astype(BF16)
    x1_lo = (x1 - x1_hi.astype(F32)).astype(BF16)
    logits_t = (_dot_nt(rwh_ref[...], x1_hi) + _dot_nt(rwh_ref[...], x1_lo) + _dot_nt(rwl_ref[...], x1_hi)
                + rb_ref[...])
    e0, e1, w0, w1 = _route(logits_t)
    eid_ref[...] = jnp.concatenate([e0, e1, jnp.zeros((8 - TOP_K, tile), I32)], axis=0)
    w_t = jnp.concatenate([w0, w1, jnp.zeros((WTOK_LANES - TOP_K, tile), F32)], axis=0)
    wtok_ref[...] = jnp.transpose(w_t)


def _mixer(xf, rope, kv, win, wgrp, pscale, gnw, wp, wr, wa, wo, lnw, lnb, rw, rb, batch, seq, mem_len,
           tile=512):
    T = xf.shape[0]
    nj = seq // tile
    cos, sin, cos_t, sin_t = rope
    bmain, bhalo = _pool_bands()
    dmask, qdec, kdec, cdec = _ret_consts()
    resident = lambda a: pl.BlockSpec(a.shape, lambda b, j: (0,) * a.ndim, pipeline_mode=pl.Buffered(1))
    rowblk = lambda w: pl.BlockSpec((tile, w), lambda b, j: (b * nj + j, 0))
    colblk = lambda r: pl.BlockSpec((r, tile), lambda b, j: (0, b * nj + j))
    consts = (win, wgrp, pscale, bmain, bhalo, dmask, qdec, kdec, gnw, wp, wr, wa, wo, lnw, lnb, *rw, rb)
    return pl.pallas_call(
        functools.partial(_mixer_kernel, tile=tile, cdec=cdec),
        grid=(batch, nj),
        in_specs=[rowblk(D_MODEL), rowblk(RET_QK_DIM), rowblk(RET_QK_DIM),
                  colblk(RET_QK_DIM // 2), colblk(RET_QK_DIM // 2),
                  pl.BlockSpec((mem_len, XA_WIDTH), lambda b, j: (b, 0)),
                  pl.BlockSpec((mem_len, XA_WIDTH), lambda b, j: (b, 1))] + [resident(a) for a in consts],
        out_specs=[rowblk(PACK_W), colblk(8), rowblk(WTOK_LANES)],
        out_shape=[jax.ShapeDtypeStruct((T, PACK_W), I32), jax.ShapeDtypeStruct((8, T), I32),
                   jax.ShapeDtypeStruct((T, WTOK_LANES), F32)],
        scratch_shapes=[pltpu.VMEM((RET_HEADS, RET_QK_DIM, RET_V_DIM), F32),
                        pltpu.VMEM((2, POOL_HALO, POOL_WIDTH), BF16),
                        pltpu.VMEM((tile, POOL_WIDTH), BF16),
                        pltpu.VMEM((tile, RET_HEADS * RET_V_DIM), BF16),
                        pltpu.VMEM((tile, XA_WIDTH), BF16),
                        pltpu.VMEM((tile, RET_HEADS * RET_QK_DIM), BF16),
                        pltpu.VMEM((tile, RET_HEADS * RET_QK_DIM), BF16),
                        pltpu.VMEM((RET_HEADS * RET_QK_DIM, tile), BF16),
                        pltpu.VMEM((RET_HEADS * RET_QK_DIM, tile), BF16),
                        pltpu.VMEM((RET_HEADS * RET_QK_DIM, D_MODEL), BF16)],
        compiler_params=pltpu.CompilerParams(dimension_semantics=("arbitrary", "arbitrary"),
                                             vmem_limit_bytes=VMEM_LIMIT),
        name="mixer",
    )(xf, cos, sin, cos_t, sin_t, kv, kv, *consts)


META_LANES = 128


def _positions_kernel(eid_ref, tri_ref, low_ref, pos_ref, first_tile_ref, n_tiles_ref, *, n_tok):
    n_chunks = n_tok // RANK_CHUNK
    erow = lax.broadcasted_iota(I32, (N_EXPERTS, RANK_CHUNK), 0)

    def onehot(c):
        sl = slice(c * RANK_CHUNK, (c + 1) * RANK_CHUNK)
        m0 = eid_ref[0:1, sl] == erow
        m1 = eid_ref[1:2, sl] == erow
        return m0, m1, jnp.where(m0, 1.0, 0.0) + jnp.where(m1, 1.0, 0.0)

    counts = jnp.zeros((N_EXPERTS, 1), F32)
    for c in range(n_chunks):
        counts = counts + jnp.sum(onehot(c)[2], axis=1, keepdims=True)
    ptiles = jnp.floor((counts + (MOE_TM - 1)) * (1.0 / MOE_TM))
    ptiles_b = jnp.broadcast_to(ptiles, (N_EXPERTS, 128)).astype(BF16)
    start = _dot(low_ref[...], ptiles_b)[:, 0:1] * MOE_TM

    pos_ref[...] = jnp.zeros_like(pos_ref)
    carry = start - 1.0
    for c in range(n_chunks):
        sl = slice(c * RANK_CHUNK, (c + 1) * RANK_CHUNK)
        m0, m1, oh = onehot(c)
        rank = _dot(oh.astype(BF16), tri_ref[...]) + carry
        pos_ref[0:1, sl] = jnp.sum(jnp.where(m0, rank, 0.0), axis=0, keepdims=True).astype(I32)
        pos_ref[1:2, sl] = jnp.sum(jnp.where(m1, rank, 0.0), axis=0, keepdims=True).astype(I32)
        carry = carry + jnp.sum(oh, axis=1, keepdims=True)

    first_tile_ref[...] = jnp.broadcast_to(start * (1.0 / MOE_TM), first_tile_ref.shape).astype(I32)
    n_tiles_ref[...] = jnp.broadcast_to(ptiles, n_tiles_ref.shape).astype(I32)


def _positions(eid):
    T = eid.shape[1]
    r = np.arange(RANK_CHUNK)
    tri = jnp.asarray(r[:, None] <= r[None, :], BF16)
    e = np.arange(N_EXPERTS)
    low = jnp.asarray(e[None, :] < e[:, None], BF16)
    full = lambda a: pl.BlockSpec(a.shape, lambda i: (0,) * a.ndim)
    return pl.pallas_call(
        functools.partial(_positions_kernel, n_tok=T),
        grid=(1,),
        in_specs=[full(eid), full(tri), full(low)],
        out_specs=[pl.BlockSpec((8, T), lambda i: (0, 0)),
                   pl.BlockSpec((N_EXPERTS, META_LANES), lambda i: (0, 0)),
                   pl.BlockSpec((N_EXPERTS, META_LANES), lambda i: (0, 0))],
        out_shape=[jax.ShapeDtypeStruct((8, T), I32), jax.ShapeDtypeStruct((N_EXPERTS, META_LANES), I32),
                   jax.ShapeDtypeStruct((N_EXPERTS, META_LANES), I32)],
        name="route_positions",
    )(eid, tri, low)


def _sc_workers():
    info = plsc.get_sparse_core_info()
    return info.num_cores, info.num_cores * info.num_subcores


def _sc_scatter_rows(xp, pos2d, n_out):
    T, W = xp.shape
    n_cores, n_workers = _sc_workers()
    cpw = T // SC_CHUNK // n_workers
    mesh = plsc.VectorSubcoreMesh(core_axis_name="c", subcore_axis_name="s")

    @functools.partial(
        pl.kernel, mesh=mesh, out_type=jax.ShapeDtypeStruct((n_out, W), I32),
        scratch_types=[pltpu.VMEM((TOP_K * cpw, SC_CHUNK), I32), pltpu.VMEM((SC_CHUNK, W), I32)],
        name="sc_scatter_rows")
    def k(x_hbm, pos_hbm, out_hbm, idx_v, rows_v):
        wid = lax.axis_index("s") * n_cores + lax.axis_index("c")
        for s in range(TOP_K):
            pltpu.sync_copy(pos_hbm.at[pl.ds(s * (T // SC_CHUNK) + wid * cpw, cpw)],
                            idx_v.at[pl.ds(s * cpw, cpw)])
        for j in range(cpw):
            pltpu.sync_copy(x_hbm.at[pl.ds((wid * cpw + j) * SC_CHUNK, SC_CHUNK)], rows_v)
            for s in range(TOP_K):
                pltpu.sync_copy(rows_v, out_hbm.at[idx_v.at[s * cpw + j]])

    return k(xp, pos2d)


def _sc_gather_rows(y, idx2d):
    W = y.shape[1]
    n = idx2d.shape[0] * SC_CHUNK
    n_cores, n_workers = _sc_workers()
    cpw = n // SC_CHUNK // n_workers
    mesh = plsc.VectorSubcoreMesh(core_axis_name="c", subcore_axis_name="s")

    @functools.partial(
        pl.kernel, mesh=mesh, out_type=jax.ShapeDtypeStruct((n, W), I32),
        scratch_types=[pltpu.VMEM((cpw, SC_CHUNK), I32), pltpu.VMEM((SC_CHUNK, W), I32)],
        name="sc_gather_rows")
    def k(y_hbm, idx_hbm, out_hbm, idx_v, rows_v):
        wid = lax.axis_index("s") * n_cores + lax.axis_index("c")
        pltpu.sync_copy(idx_hbm.at[pl.ds(wid * cpw, cpw)], idx_v)
        for j in range(cpw):
            pltpu.sync_copy(y_hbm.at[idx_v.at[j]], rows_v)
            pltpu.sync_copy(rows_v, out_hbm.at[pl.ds((wid * cpw + j) * SC_CHUNK, SC_CHUNK)])

    return k(y, idx2d)


def _routed_kernel(first_ref, count_ref, xs_hbm, wg_ref, wu_ref, wd_ref, ys_hbm, xbuf, ybuf, in_sem, out_sem):
    e = pl.program_id(0)
    last = pl.num_programs(0) - 1
    total = first_ref[last] + count_ref[last]

    def in_copy(g):
        slot = lax.rem(g, RING)
        return pltpu.make_async_copy(xs_hbm.at[pl.ds(g * MOE_TM, MOE_TM)], xbuf.at[slot], in_sem.at[slot])

    def out_copy(g):
        slot = lax.rem(g, RING)
        return pltpu.make_async_copy(ybuf.at[slot], ys_hbm.at[pl.ds(g * MOE_TM, MOE_TM)], out_sem.at[slot])

    @pl.when(e == 0)
    def _():
        for g0 in range(RING - 1):
            @pl.when(g0 < total)
            def _():
                in_copy(g0).start()

    wg = wg_ref[0].astype(BF16)
    wu = wu_ref[0].astype(BF16)
    wd = wd_ref[0].astype(BF16)

    def tile_step(i, carry):
        g = first_ref[e] + i
        slot = lax.rem(g, RING)
        in_copy(g).wait()

        @pl.when(g + RING - 1 < total)
        def _():
            in_copy(g + RING - 1).start()

        @pl.when(g >= RING)
        def _():
            out_copy(g - RING).wait()

        lo, hi = _unpack_halves(xbuf[slot])
        lo = lo.astype(BF16)
        hi = hi.astype(BF16)
        a = _dot(lo, wg[:PACK_W]) + _dot(hi, wg[PACK_W:])
        b = _dot(lo, wu[:PACK_W]) + _dot(hi, wu[PACK_W:])
        act = (a * _sigmoid(a) * b).astype(BF16)
        ybuf[slot] = _pack_halves(_dot(act, wd))
        out_copy(g).start()
        return carry

    lax.fori_loop(0, count_ref[e], tile_step, 0)

    @pl.when(e == last)
    def _():
        for back in range(RING, 0, -1):
            @pl.when(total >= back)
            def _():
                out_copy(total - back).wait()


def _routed_mlp(first_tile, n_tiles, xs, wg, wu, wd):
    R = xs.shape[0]
    any_space = pl.BlockSpec(memory_space=pl.ANY)
    return pl.pallas_call(
        _routed_kernel,
        grid_spec=pltpu.PrefetchScalarGridSpec(
            num_scalar_prefetch=2,
            grid=(N_EXPERTS,),
            in_specs=[any_space,
                      pl.BlockSpec((1, D_MODEL, D_EXPERT), lambda e, ft, nt: (e, 0, 0)),
                      pl.BlockSpec((1, D_MODEL, D_EXPERT), lambda e, ft, nt: (e, 0, 0)),
                      pl.BlockSpec((1, D_EXPERT, D_MODEL), lambda e, ft, nt: (e, 0, 0))],
            out_specs=any_space,
            scratch_shapes=[pltpu.VMEM((RING, MOE_TM, PACK_W), I32), pltpu.VMEM((RING, MOE_TM, PACK_W), I32),
                            pltpu.SemaphoreType.DMA((RING,)), pltpu.SemaphoreType.DMA((RING,))]),
        out_shape=jax.ShapeDtypeStruct((R, PACK_W), I32),
        compiler_params=pltpu.CompilerParams(dimension_semantics=("arbitrary",)),
        name="routed_mlp",
    )(first_tile, n_tiles, xs, wg, wu, wd)


def _combine_kernel(xp_ref, y0_ref, y1_ref, wtok_ref, lnw_ref, lnb_ref, *out_refs):
    o_ref = out_refs[-1]
    w0 = wtok_ref[:, 0:1]
    w1 = wtok_ref[:, 1:2]
    xlo, xhi = _unpack_halves(xp_ref[...])
    y0lo, y0hi = _unpack_halves(y0_ref[...])
    y1lo, y1hi = _unpack_halves(y1_ref[...])
    h = jnp.concatenate([ALPHA * xlo + (w0 * y0lo + w1 * y1lo), ALPHA * xhi + (w0 * y0hi + w1 * y1hi)], axis=1)
    o_ref[...] = _layer_norm(h, lnw_ref[...], lnb_ref[...])


def _combine_ln2(xp, yg_parts, wtok, lnw, lnb, tile=512):
    T = xp.shape[0]
    n_parts = len(yg_parts)
    nt = T // tile // n_parts
    full = lambda a: pl.BlockSpec(a.shape, lambda i: (0,) * a.ndim)
    out = None
    for p, yg in enumerate(yg_parts):
        rows = lambda w, p=p: pl.BlockSpec((tile, w), lambda i: (i + p * nt, 0))
        in_specs = [rows(PACK_W),
                    pl.BlockSpec((tile, PACK_W), lambda i: (i, 0)),
                    pl.BlockSpec((tile, PACK_W), lambda i: (i + nt, 0)),
                    rows(WTOK_LANES), full(lnw), full(lnb)]
        args = [xp, yg, yg, wtok, lnw, lnb]
        aliases = {}
        if out is not None:
            in_specs.append(pl.BlockSpec(memory_space=pl.ANY))
            args.append(out)
            aliases = {len(args) - 1: 0}
        out = pl.pallas_call(
            _combine_kernel,
            grid=(nt,),
            in_specs=in_specs,
            out_specs=rows(D_MODEL),
            out_shape=jax.ShapeDtypeStruct((T, D_MODEL), F32),
            input_output_aliases=aliases,
            name="combine_ln2",
        )(*args)
    return out


def _router_params(w_grp, b_grp, w_exp, b_exp):
    rw = jnp.zeros((ROUTER_ROWS, D_MODEL), F32)
    rw = rw.at[0:N_GROUPS].set(w_grp.T).at[EXP_ROW0:EXP_ROW0 + N_EXPERTS].set(w_exp.T)
    rb = jnp.zeros((ROUTER_ROWS,), F32).at[N_GROUPS:8].set(NEG_BIG)
    rb = rb.at[0:N_GROUPS].set(b_grp).at[EXP_ROW0:EXP_ROW0 + N_EXPERTS].set(b_exp)
    rw_hi = rw.astype(BF16)
    rw_lo = (rw - rw_hi.astype(F32)).astype(BF16)
    return (rw_hi, rw_lo), rb[:, None]


def kernel(x, mem, positions, w_in, w_pool_grp, pool_scale, ret_gn_w, w_mem_kv, w_br_pool, w_br_ret, w_br_xa,
           w_out, ln1_w, ln1_b, w_grp_router, b_grp_router, w_exp_router, b_exp_router, w_exp_gate, w_exp_up,
           w_exp_down, ln2_w, ln2_b):
    B, S, D = x.shape
    assert D == D_MODEL and w_in.shape[0] == DEPTH and S % 512 == 0
    T = B * S
    M = mem.shape[1]
    l = 0
    xf = x.reshape(T, D)

    rope = _rope_table(positions.reshape(1, T))
    kv = _mem_kv(mem.reshape(B * M, D), w_mem_kv[l].astype(BF16))
    rw, rb = _router_params(w_grp_router[l], b_grp_router[l], w_exp_router[l], b_exp_router[l])
    xp, eid, wtok = _mixer(xf, rope, kv, w_in[l].astype(BF16),
                           w_pool_grp[l].astype(BF16), pool_scale[l][None, :], ret_gn_w[l].reshape(1, -1),
                           w_br_pool[l].astype(BF16), w_br_ret[l].astype(BF16), w_br_xa[l].astype(BF16),
                           w_out[l].astype(BF16), ln1_w[l][None, :], ln1_b[l][None, :], rw, rb, B, S, M)

    pos, first_tile, n_tiles = _positions(eid)
    pos2d = pos[0:TOP_K].reshape(TOP_K * T // SC_CHUNK, SC_CHUNK)
    max_tiles = (TOP_K * T + N_EXPERTS * (MOE_TM - 1)) // MOE_TM
    xs = _sc_scatter_rows(xp, pos2d, max_tiles * MOE_TM)
    ys = _routed_mlp(first_tile[:, 0], n_tiles[:, 0], xs,
                     w_exp_gate[l].reshape(N_EXPERTS, D_MODEL, D_EXPERT),
                     w_exp_up[l].reshape(N_EXPERTS, D_MODEL, D_EXPERT),
                     w_exp_down[l].reshape(N_EXPERTS, D_EXPERT, D_MODEL))
    rng = T // COMBINE_PARTS
    yg_parts = [_sc_gather_rows(ys, pos[0:TOP_K, p * rng:(p + 1) * rng].reshape(TOP_K * rng // SC_CHUNK, SC_CHUNK))
                for p in range(COMBINE_PARTS)]
    out = _combine_ln2(xp, yg_parts, wtok, ln2_w[l][None, :], ln2_b[l][None, :])
    return out.reshape(B, S, D)
```

```python
import functools

import numpy as np
import jax
import jax.numpy as jnp
from jax import lax
from jax.experimental import pallas as pl
from jax.experimental.pallas import tpu as pltpu
from jax.experimental.pallas import tpu_sc as plsc

F32 = jnp.float32
BF16 = jnp.bfloat16
I32 = jnp.int32
U32 = jnp.uint32

D_MODEL = 1024
POOL_WINDOWS = (2, 4, 8, 16)
POOL_GROUP_DIM = 128
POOL_WIDTH = 512
POOL_HALO = 16
RET_HEADS = 4
RET_QK_DIM = 128
RET_V_DIM = 256
RET_CHUNK = 128
ROPE_BASE = 10000.0
XA_HEADS = 4
XA_HEAD_DIM = 128
XA_WIDTH = 512
N_GROUPS = 4
EXPERTS_PER_GROUP = 8
N_EXPERTS = N_GROUPS * EXPERTS_PER_GROUP
D_EXPERT = 256
LN_EPS = 1e-5
DEPTH = 1
ALPHA = (2.0 * DEPTH) ** 0.25
NEG_BIG = -1e30

COL_POOL, COL_Q, COL_K, COL_V, COL_G, COL_XAQ, COL_GATES = 0, 512, 1024, 1536, 2560, 3584, 4096

VMEM_LIMIT = 56 * 1024 * 1024

TOP_K = 2
PACK_W = D_MODEL // 2
MOE_TM = 512
SC_CHUNK = 64
RANK_CHUNK = 512
COMBINE_PARTS = 2
RING = 4
STRIP = 256
LN_ROWS = 32


def _dot(a, b):
    return jnp.dot(a, b, preferred_element_type=F32)


def _dot_nt(a, b, precision=None):
    return lax.dot_general(a, b, (((1,), (1,)), ((), ())), preferred_element_type=F32, precision=precision)


def _sigmoid(z):
    return 1.0 / (1.0 + jnp.exp2(z * (-1.0 / np.log(2.0))))


def _layer_norm(h, w, b):
    mu = jnp.mean(h, axis=-1, keepdims=True)
    hc = h - mu
    var = jnp.mean(hc * hc, axis=-1, keepdims=True)
    return hc * lax.rsqrt(var + LN_EPS) * w + b


def _rope_kernel(pos_ref, freq_ref, cos_ref, sin_ref, cos_t_ref, sin_t_ref):
    ang = freq_ref[...] * pos_ref[...].astype(F32)
    cos_t = jnp.cos(ang)
    sin_t = jnp.sin(ang)
    cos_t_ref[...] = cos_t
    sin_t_ref[...] = sin_t
    cos_ref[...] = jnp.transpose(jnp.concatenate([cos_t, cos_t], axis=0))
    sin_ref[...] = jnp.transpose(jnp.concatenate([-sin_t, sin_t], axis=0))


def _rope_table(pos_row, tile=2048):
    T = pos_row.shape[1]
    half = RET_QK_DIM // 2
    inv_freq = (ROPE_BASE ** (-np.arange(half, dtype=np.float64) / half)).astype(np.float32)
    freq = jnp.asarray(inv_freq[:, None])
    out = pl.BlockSpec((tile, RET_QK_DIM), lambda i: (i, 0))
    out_t = pl.BlockSpec((half, tile), lambda i: (0, i))
    return pl.pallas_call(
        _rope_kernel,
        grid=(T // tile,),
        in_specs=[pl.BlockSpec((1, tile), lambda i: (0, i)), pl.BlockSpec((half, 1), lambda i: (0, 0))],
        out_specs=[out, out, out_t, out_t],
        out_shape=[jax.ShapeDtypeStruct((T, RET_QK_DIM), F32)] * 2 + [jax.ShapeDtypeStruct((half, T), F32)] * 2,
        name="rope_table",
    )(pos_row, freq)


POOL_SUB = 256


def _pool_bands():
    r = np.arange(POOL_SUB)[:, None]
    c = np.arange(POOL_SUB)[None, :]
    ch = np.arange(POOL_HALO)[None, :] - POOL_HALO
    main = np.stack([((r - c >= 0) & (r - c < w)) for w in POOL_WINDOWS]).astype(np.float32)
    halo = np.stack([((r - ch >= 0) & (r - ch < w)) for w in POOL_WINDOWS]).astype(np.float32)
    return jnp.asarray(main, BF16), jnp.asarray(halo, BF16)


def _pool_branch(ub, j, halo_ref, bmain_ref, bhalo_ref, wg_ref, scale_ref, o_ref):
    tile = ub.shape[0]
    s0 = j * tile
    slot = lax.rem(j, 2)
    for sb in range(tile // POOL_SUB):
        r0 = sb * POOL_SUB
        u = ub[r0:r0 + POOL_SUB]
        prev = halo_ref[slot] if sb == 0 else ub[r0 - POOL_HALO:r0]
        pos = s0 + r0 + lax.broadcasted_iota(I32, (POOL_SUB, POOL_GROUP_DIM), 0)
        for g, w in enumerate(POOL_WINDOWS):
            cols = slice(g * POOL_GROUP_DIM, (g + 1) * POOL_GROUP_DIM)
            ug = u[:, cols]
            wsum = _dot(bmain_ref[g], ug) + _dot(bhalo_ref[g], prev[:, cols])
            cnt = jnp.minimum(pos + 1, w).astype(F32)
            pooled = wsum / cnt - ug.astype(F32)
            mixed = _dot(pooled.astype(BF16), wg_ref[g]) * scale_ref[:, cols]
            o_ref[r0:r0 + POOL_SUB, cols] = mixed.astype(BF16)
    halo_ref[1 - slot] = ub[tile - POOL_HALO:tile]


def _ret_consts():
    h = np.arange(RET_HEADS, dtype=np.float64)
    log_gamma = np.log1p(-np.exp2(-5.0 - h))
    pos = np.arange(RET_CHUNK, dtype=np.float64)
    diff = pos[:, None] - pos[None, :]
    kscale = RET_QK_DIM ** -0.5
    dmask = kscale * np.where(diff >= 0, np.exp(log_gamma[:, None, None] * np.maximum(diff, 0.0)), 0.0)
    qdec = np.exp(log_gamma[:, None] * (pos + 1.0)[None, :])
    kdec = kscale * np.exp(log_gamma[:, None] * (RET_CHUNK - 1.0 - pos)[None, :])
    cdec = np.exp(log_gamma * RET_CHUNK)
    lanes = lambda a: np.broadcast_to(a[:, :, None], (RET_HEADS, RET_CHUNK, RET_QK_DIM))
    kdec_t = np.broadcast_to(kdec[:, None, :], (RET_HEADS, RET_QK_DIM, RET_CHUNK))
    return (jnp.asarray(dmask, F32), jnp.asarray(lanes(qdec), F32), jnp.asarray(kdec_t, F32),
            tuple(float(v) for v in cdec))


def _retention_branch(q, k_t, v, silu_g, cos_ref, sin_ref, cos_t_ref, sin_t_ref, dmask_ref, qdec_ref, kdec_ref,
                      gnw_ref, state_ref, rq_ref, rqd_ref, rkt_ref, rkdt_ref, o_ref, cdec, fillers):
    tile = q.shape[0]
    n_chunks = tile // RET_CHUNK
    half = RET_QK_DIM // 2
    cos = cos_ref[...]
    sin = sin_ref[...]
    cos_t = cos_t_ref[...]
    sin_t = sin_t_ref[...]
    for h in range(RET_HEADS):
        qk = slice(h * RET_QK_DIM, (h + 1) * RET_QK_DIM)
        qh = q[:, qk]
        qr = qh * cos + pltpu.roll(qh, half, 1) * sin
        rq_ref[:, qk] = qr.astype(BF16)
        rqd_ref[:, qk] = (qr * jnp.concatenate([qdec_ref[h]] * n_chunks, axis=0)).astype(BF16)
        k1 = k_t[h * RET_QK_DIM:h * RET_QK_DIM + half]
        k2 = k_t[h * RET_QK_DIM + half:(h + 1) * RET_QK_DIM]
        kr_t = jnp.concatenate([k1 * cos_t - k2 * sin_t, k2 * cos_t + k1 * sin_t], axis=0)
        rkt_ref[qk, :] = kr_t.astype(BF16)
        rkdt_ref[qk, :] = (kr_t * jnp.concatenate([kdec_ref[h]] * n_chunks, axis=1)).astype(BF16)

    for c in range(n_chunks):
        rows = slice(c * RET_CHUNK, (c + 1) * RET_CHUNK)
        heads = [slice(h * RET_QK_DIM, (h + 1) * RET_QK_DIM) for h in range(RET_HEADS)]
        raw = [_dot(rq_ref[rows, qk], rkt_ref[qk, rows]) for qk in heads]
        fillers[c]()
        for h in range(RET_HEADS):
            qk = heads[h]
            v_cols = slice(h * RET_V_DIM, (h + 1) * RET_V_DIM)
            vh = v[rows, v_cols]
            scores = raw[h] * dmask_ref[h]
            st = state_ref[h]
            lhs = jnp.concatenate([scores.astype(BF16), rqd_ref[rows, qk]], axis=1)
            y = _dot(lhs, jnp.concatenate([vh, st.astype(BF16)], axis=0))
            state_ref[h] = cdec[h] * st + _dot(rkdt_ref[qk, rows], vh)
            mu = jnp.mean(y, axis=-1, keepdims=True)
            yc = y - mu
            var = jnp.mean(yc * yc, axis=-1, keepdims=True)
            yn = yc * lax.rsqrt(var + LN_EPS) * gnw_ref[:, v_cols]
            o_ref[rows, v_cols] = (silu_g[h][rows] * yn).astype(BF16)


def _memkv_kernel(m_ref, w_ref, o_ref):
    o_ref[...] = _dot(m_ref[...].astype(BF16), w_ref[...]).astype(BF16)


def _mem_kv(memf, w_b):
    M, D = memf.shape
    N = w_b.shape[1]
    return pl.pallas_call(
        _memkv_kernel,
        grid=(1,),
        in_specs=[pl.BlockSpec((M, D), lambda i: (0, 0)), pl.BlockSpec((D, N), lambda i: (0, 0))],
        out_specs=pl.BlockSpec((M, N), lambda i: (0, 0)),
        out_shape=jax.ShapeDtypeStruct((M, N), BF16),
        name="mem_kv",
    )(memf, w_b)


def _cross_attention_branch(xq, k_ref, v_ref, o_ref):
    scale = XA_HEAD_DIM ** -0.5
    for h in range(XA_HEADS):
        cols = slice(h * XA_HEAD_DIM, (h + 1) * XA_HEAD_DIM)
        s = _dot_nt(xq[:, cols], k_ref[:, cols]) * scale
        m = jnp.max(s, axis=-1, keepdims=True)
        p = jnp.exp(s - m)
        l = jnp.sum(p, axis=-1, keepdims=True)
        o = _dot(p.astype(BF16), v_ref[:, cols]) / l
        o_ref[:, cols] = o.astype(BF16)


ROUTER_ROWS = 40
WTOK_LANES = 128
EXP_ROW0 = 8


def _route(logits_t):
    gl = logits_t[0:8]
    gmax = jnp.max(gl, axis=0, keepdims=True)
    p_grp = 1.0 / jnp.sum(jnp.exp(gl - gmax), axis=0, keepdims=True)
    idx8 = lax.broadcasted_iota(jnp.int32, gl.shape, 0)
    gsel = jnp.min(jnp.where(gl == gmax, idx8, 8), axis=0, keepdims=True)
    cl = jnp.zeros_like(gl)
    for g in range(N_GROUPS):
        r0 = EXP_ROW0 + g * EXPERTS_PER_GROUP
        cl = cl + jnp.where(gsel == g, logits_t[r0:r0 + EXPERTS_PER_GROUP], 0.0)
    v1 = jnp.max(cl, axis=0, keepdims=True)
    i1 = jnp.min(jnp.where(cl == v1, idx8, 8), axis=0, keepdims=True)
    cl2 = jnp.where(idx8 == i1, -jnp.inf, cl)
    v2 = jnp.max(cl2, axis=0, keepdims=True)
    i2 = jnp.min(jnp.where(cl2 == v2, idx8, 8), axis=0, keepdims=True)
    e21 = jnp.exp(v2 - v1)
    w1 = p_grp / (1.0 + e21)
    w2 = p_grp * e21 / (1.0 + e21)
    return gsel * EXPERTS_PER_GROUP + i1, gsel * EXPERTS_PER_GROUP + i2, w1, w2


def _pack_halves(v):
    half = v.shape[1] // 2
    lo = lax.bitcast_convert_type(v[:, :half].astype(BF16).astype(F32), U32)
    hi = lax.bitcast_convert_type(v[:, half:].astype(BF16).astype(F32), U32)
    return lax.bitcast_convert_type(lax.shift_right_logical(lo, U32(16)) | hi, I32)


def _unpack_halves(w):
    u = lax.bitcast_convert_type(w, U32)
    lo = lax.bitcast_convert_type(lax.shift_left(u, U32(16)), F32)
    hi = lax.bitcast_convert_type(u & U32(0xFFFF0000), F32)
    return lo, hi


def _mixer_kernel(x_ref, cos_ref, sin_ref, cos_t_ref, sin_t_ref, km_ref, vm_ref, win_ref, wgrp_ref,
                  pscale_ref, bmain_ref, bhalo_ref, dmask_ref, qdec_ref, kdec_ref, gnw_ref, wp_ref, wr_ref,
                  wa_ref, wo_ref, lnw_ref, lnb_ref, rwh_ref, rwl_ref, rb_ref, xp_ref, eid_ref, wtok_ref,
                  state_ref, halo_ref, ypool_ref, yret_ref, yxa_ref, rq_ref, rqd_ref, rkt_ref, rkdt_ref, wkt_ref,
                  *, tile, cdec):
    j = pl.program_id(1)

    @pl.when(jnp.logical_and(pl.program_id(0) == 0, j == 0))
    def _():
        wk = win_ref[:, COL_K:COL_K + RET_HEADS * RET_QK_DIM].astype(F32)
        wkt_ref[...] = jnp.transpose(wk).astype(BF16)

    @pl.when(j == 0)
    def _():
        state_ref[...] = jnp.zeros_like(state_ref)
        halo_ref[...] = jnp.zeros_like(halo_ref)

    x = x_ref[...]
    xb = x.astype(BF16)

    def proj(col, width):
        return _dot(xb, win_ref[:, col:col + width])

    part = {}
    strips = [slice(c, c + STRIP) for c in range(0, D_MODEL, STRIP)]

    def gate(branch, cols):
        return _sigmoid(proj(COL_GATES + branch * D_MODEL + cols.start, STRIP))

    def pool_part():
        _pool_branch(proj(COL_POOL, POOL_WIDTH).astype(BF16), j, halo_ref, bmain_ref, bhalo_ref, wgrp_ref,
                     pscale_ref, ypool_ref)
        part["pool"] = [gate(0, c) * _dot(ypool_ref[...], wp_ref[:, c]) for c in strips]

    def xa_part():
        _cross_attention_branch(proj(COL_XAQ, XA_WIDTH).astype(BF16), km_ref, vm_ref, yxa_ref)
        part["xa"] = [gate(2, c) * _dot(yxa_ref[...], wa_ref[:, c]) for c in strips]

    def ret_gate_part():
        part["ret_gate"] = [gate(1, c) for c in strips]

    silu_g = []
    for h in range(RET_HEADS):
        gh = proj(COL_G + h * RET_V_DIM, RET_V_DIM)
        silu_g.append(gh * _sigmoid(gh))
    fillers = [pool_part, xa_part, ret_gate_part] + [lambda: None] * (tile // RET_CHUNK - 3)
    _retention_branch(proj(COL_Q, RET_HEADS * RET_QK_DIM), _dot_nt(wkt_ref[...], xb),
                      proj(COL_V, RET_HEADS * RET_V_DIM).astype(BF16), silu_g, cos_ref, sin_ref, cos_t_ref,
                      sin_t_ref, dmask_ref, qdec_ref, kdec_ref, gnw_ref, state_ref, rq_ref, rqd_ref, rkt_ref,
                      rkdt_ref, yret_ref, cdec, fillers)
    merged = jnp.concatenate(
        [(part["pool"][i] + part["ret_gate"][i] * _dot(yret_ref[...], wr_ref[:, c]) + part["xa"][i]).astype(BF16)
         for i, c in enumerate(strips)], axis=1)
    h = jnp.concatenate([ALPHA * x[:, c] + _dot(merged, wo_ref[:, c]) for c in strips], axis=1)
    x1 = jnp.concatenate([_layer_norm(h[r:r + LN_ROWS], lnw_ref[...], lnb_ref[...])
                          for r in range(0, tile, LN_ROWS)], axis=0)
    xp_ref[...] = _pack_halves(x1)
    x1_hi = x1.astype(BF16)
    x1_lo = (x1 - x1_hi.astype(F32)).astype(BF16)
    logits_t = (_dot_nt(rwh_ref[...], x1_hi) + _dot_nt(rwh_ref[...], x1_lo) + _dot_nt(rwl_ref[...], x1_hi)
                + rb_ref[...])
    e0, e1, w0, w1 = _route(logits_t)
    eid_ref[...] = jnp.concatenate([e0, e1, jnp.zeros((8 - TOP_K, tile), I32)], axis=0)
    w_t = jnp.concatenate([w0, w1, jnp.zeros((WTOK_LANES - TOP_K, tile), F32)], axis=0)
    wtok_ref[...] = jnp.transpose(w_t)


def _mixer(xf, rope, kv, win, wgrp, pscale, gnw, wp, wr, wa, wo, lnw, lnb, rw, rb, batch, seq, mem_len,
           tile=512):
    T = xf.shape[0]
    nj = seq // tile
    cos, sin, cos_t, sin_t = rope
    bmain, bhalo = _pool_bands()
    dmask, qdec, kdec, cdec = _ret_consts()
    resident = lambda a: pl.BlockSpec(a.shape, lambda b, j: (0,) * a.ndim, pipeline_mode=pl.Buffered(1))
    rowblk = lambda w: pl.BlockSpec((tile, w), lambda b, j: (b * nj + j, 0))
    colblk = lambda r: pl.BlockSpec((r, tile), lambda b, j: (0, b * nj + j))
    consts = (win, wgrp, pscale, bmain, bhalo, dmask, qdec, kdec, gnw, wp, wr, wa, wo, lnw, lnb, *rw, rb)
    return pl.pallas_call(
        functools.partial(_mixer_kernel, tile=tile, cdec=cdec),
        grid=(batch, nj),
        in_specs=[rowblk(D_MODEL), rowblk(RET_QK_DIM), rowblk(RET_QK_DIM),
                  colblk(RET_QK_DIM // 2), colblk(RET_QK_DIM // 2),
                  pl.BlockSpec((mem_len, XA_WIDTH), lambda b, j: (b, 0)),
                  pl.BlockSpec((mem_len, XA_WIDTH), lambda b, j: (b, 1))] + [resident(a) for a in consts],
        out_specs=[rowblk(PACK_W), colblk(8), rowblk(WTOK_LANES)],
        out_shape=[jax.ShapeDtypeStruct((T, PACK_W), I32), jax.ShapeDtypeStruct((8, T), I32),
                   jax.ShapeDtypeStruct((T, WTOK_LANES), F32)],
        scratch_shapes=[pltpu.VMEM((RET_HEADS, RET_QK_DIM, RET_V_DIM), F32),
                        pltpu.VMEM((2, POOL_HALO, POOL_WIDTH), BF16),
                        pltpu.VMEM((tile, POOL_WIDTH), BF16),
                        pltpu.VMEM((tile, RET_HEADS * RET_V_DIM), BF16),
                        pltpu.VMEM((tile, XA_WIDTH), BF16),
                        pltpu.VMEM((tile, RET_HEADS * RET_QK_DIM), BF16),
                        pltpu.VMEM((tile, RET_HEADS * RET_QK_DIM), BF16),
                        pltpu.VMEM((RET_HEADS * RET_QK_DIM, tile), BF16),
                        pltpu.VMEM((RET_HEADS * RET_QK_DIM, tile), BF16),
                        pltpu.VMEM((RET_HEADS * RET_QK_DIM, D_MODEL), BF16)],
        compiler_params=pltpu.CompilerParams(dimension_semantics=("arbitrary", "arbitrary"),
                                             vmem_limit_bytes=VMEM_LIMIT),
        name="mixer",
    )(xf, cos, sin, cos_t, sin_t, kv, kv, *consts)


META_LANES = 128


def _positions_kernel(eid_ref, tri_ref, low_ref, pos_ref, first_tile_ref, n_tiles_ref, *, n_tok):
    n_chunks = n_tok // RANK_CHUNK
    erow = lax.broadcasted_iota(I32, (N_EXPERTS, RANK_CHUNK), 0)

    def onehot(c):
        sl = slice(c * RANK_CHUNK, (c + 1) * RANK_CHUNK)
        m0 = eid_ref[0:1, sl] == erow
        m1 = eid_ref[1:2, sl] == erow
        return m0, m1, jnp.where(m0, 1.0, 0.0) + jnp.where(m1, 1.0, 0.0)

    counts = jnp.zeros((N_EXPERTS, 1), F32)
    for c in range(n_chunks):
        counts = counts + jnp.sum(onehot(c)[2], axis=1, keepdims=True)
    ptiles = jnp.floor((counts + (MOE_TM - 1)) * (1.0 / MOE_TM))
    ptiles_b = jnp.broadcast_to(ptiles, (N_EXPERTS, 128)).astype(BF16)
    start = _dot(low_ref[...], ptiles_b)[:, 0:1] * MOE_TM

    pos_ref[...] = jnp.zeros_like(pos_ref)
    carry = start - 1.0
    for c in range(n_chunks):
        sl = slice(c * RANK_CHUNK, (c + 1) * RANK_CHUNK)
        m0, m1, oh = onehot(c)
        rank = _dot(oh.astype(BF16), tri_ref[...]) + carry
        pos_ref[0:1, sl] = jnp.sum(jnp.where(m0, rank, 0.0), axis=0, keepdims=True).astype(I32)
        pos_ref[1:2, sl] = jnp.sum(jnp.where(m1, rank, 0.0), axis=0, keepdims=True).astype(I32)
        carry = carry + jnp.sum(oh, axis=1, keepdims=True)

    first_tile_ref[...] = jnp.broadcast_to(start * (1.0 / MOE_TM), first_tile_ref.shape).astype(I32)
    n_tiles_ref[...] = jnp.broadcast_to(ptiles, n_tiles_ref.shape).astype(I32)


def _positions(eid):
    T = eid.shape[1]
    r = np.arange(RANK_CHUNK)
    tri = jnp.asarray(r[:, None] <= r[None, :], BF16)
    e = np.arange(N_EXPERTS)
    low = jnp.asarray(e[None, :] < e[:, None], BF16)
    full = lambda a: pl.BlockSpec(a.shape, lambda i: (0,) * a.ndim)
    return pl.pallas_call(
        functools.partial(_positions_kernel, n_tok=T),
        grid=(1,),
        in_specs=[full(eid), full(tri), full(low)],
        out_specs=[pl.BlockSpec((8, T), lambda i: (0, 0)),
                   pl.BlockSpec((N_EXPERTS, META_LANES), lambda i: (0, 0)),
                   pl.BlockSpec((N_EXPERTS, META_LANES), lambda i: (0, 0))],
        out_shape=[jax.ShapeDtypeStruct((8, T), I32), jax.ShapeDtypeStruct((N_EXPERTS, META_LANES), I32),
                   jax.ShapeDtypeStruct((N_EXPERTS, META_LANES), I32)],
        name="route_positions",
    )(eid, tri, low)


def _sc_workers():
    info = plsc.get_sparse_core_info()
    return info.num_cores, info.num_cores * info.num_subcores


def _sc_scatter_rows(xp, pos2d, n_out):
    T, W = xp.shape
    n_cores, n_workers = _sc_workers()
    cpw = T // SC_CHUNK // n_workers
    mesh = plsc.VectorSubcoreMesh(core_axis_name="c", subcore_axis_name="s")

    @functools.partial(
        pl.kernel, mesh=mesh, out_type=jax.ShapeDtypeStruct((n_out, W), I32),
        scratch_types=[pltpu.VMEM((TOP_K * cpw, SC_CHUNK), I32), pltpu.VMEM((2, SC_CHUNK, W), I32),
                       pltpu.SemaphoreType.DMA((2,)), pltpu.SemaphoreType.DMA((2,))],
        name="sc_scatter_rows")
    def k(x_hbm, pos_hbm, out_hbm, idx_v, rows_v, rd_sem, wr_sem):
        wid = lax.axis_index("s") * n_cores + lax.axis_index("c")
        for s in range(TOP_K):
            pltpu.sync_copy(pos_hbm.at[pl.ds(s * (T // SC_CHUNK) + wid * cpw, cpw)],
                            idx_v.at[pl.ds(s * cpw, cpw)])

        def read(j):
            return pltpu.make_async_copy(x_hbm.at[pl.ds((wid * cpw + j) * SC_CHUNK, SC_CHUNK)],
                                         rows_v.at[j % 2], rd_sem.at[j % 2])

        def write(j, s):
            return pltpu.make_async_copy(rows_v.at[j % 2], out_hbm.at[idx_v.at[s * cpw + j]], wr_sem.at[j % 2])

        _sc_two_buffer_stream(cpw, read, lambda j: [write(j, s) for s in range(TOP_K)])

    return k(xp, pos2d)


def _sc_two_buffer_stream(n, read, writes):
    read(0).start()
    for j in range(n):
        read(j).wait()
        if j + 1 < n:
            if j >= 1:
                for w in writes(j - 1):
                    w.wait()
            read(j + 1).start()
        for w in writes(j):
            w.start()
    for j in range(max(n - 2, 0), n):
        for w in writes(j):
            w.wait()


def _sc_gather_rows(y, idx2d):
    W = y.shape[1]
    n = idx2d.shape[0] * SC_CHUNK
    n_cores, n_workers = _sc_workers()
    cpw = n // SC_CHUNK // n_workers
    mesh = plsc.VectorSubcoreMesh(core_axis_name="c", subcore_axis_name="s")

    @functools.partial(
        pl.kernel, mesh=mesh, out_type=jax.ShapeDtypeStruct((n, W), I32),
        scratch_types=[pltpu.VMEM((cpw, SC_CHUNK), I32), pltpu.VMEM((2, SC_CHUNK, W), I32),
                       pltpu.SemaphoreType.DMA((2,)), pltpu.SemaphoreType.DMA((2,))],
        name="sc_gather_rows")
    def k(y_hbm, idx_hbm, out_hbm, idx_v, rows_v, rd_sem, wr_sem):
        wid = lax.axis_index("s") * n_cores + lax.axis_index("c")
        pltpu.sync_copy(idx_hbm.at[pl.ds(wid * cpw, cpw)], idx_v)

        def read(j):
            return pltpu.make_async_copy(y_hbm.at[idx_v.at[j]], rows_v.at[j % 2], rd_sem.at[j % 2])

        def write(j):
            return pltpu.make_async_copy(rows_v.at[j % 2], out_hbm.at[pl.ds((wid * cpw + j) * SC_CHUNK, SC_CHUNK)],
                                         wr_sem.at[j % 2])

        _sc_two_buffer_stream(cpw, read, lambda j: [write(j)])

    return k(y, idx2d)


def _routed_kernel(first_ref, count_ref, xs_hbm, wg_ref, wu_ref, wd_ref, ys_hbm, xbuf, ybuf, in_sem, out_sem):
    e = pl.program_id(0)
    last = pl.num_programs(0) - 1
    total = first_ref[last] + count_ref[last]

    def in_copy(g):
        slot = lax.rem(g, RING)
        return pltpu.make_async_copy(xs_hbm.at[pl.ds(g * MOE_TM, MOE_TM)], xbuf.at[slot], in_sem.at[slot])

    def out_copy(g):
        slot = lax.rem(g, RING)
        return pltpu.make_async_copy(ybuf.at[slot], ys_hbm.at[pl.ds(g * MOE_TM, MOE_TM)], out_sem.at[slot])

    @pl.when(e == 0)
    def _():
        for g0 in range(RING - 1):
            @pl.when(g0 < total)
            def _():
                in_copy(g0).start()

    wg = wg_ref[0].astype(BF16)
    wu = wu_ref[0].astype(BF16)
    wd = wd_ref[0].astype(BF16)

    def tile_step(i, carry):
        g = first_ref[e] + i
        slot = lax.rem(g, RING)
        in_copy(g).wait()

        @pl.when(g + RING - 1 < total)
        def _():
            in_copy(g + RING - 1).start()

        @pl.when(g >= RING)
        def _():
            out_copy(g - RING).wait()

        lo, hi = _unpack_halves(xbuf[slot])
        lo = lo.astype(BF16)
        hi = hi.astype(BF16)
        a = _dot(lo, wg[:PACK_W]) + _dot(hi, wg[PACK_W:])
        b = _dot(lo, wu[:PACK_W]) + _dot(hi, wu[PACK_W:])
        act = (a * _sigmoid(a) * b).astype(BF16)
        ybuf[slot] = _pack_halves(_dot(act, wd))
        out_copy(g).start()
        return carry

    lax.fori_loop(0, count_ref[e], tile_step, 0)

    @pl.when(e == last)
    def _():
        for back in range(RING, 0, -1):
            @pl.when(total >= back)
            def _():
                out_copy(total - back).wait()


def _routed_mlp(first_tile, n_tiles, xs, wg, wu, wd):
    R = xs.shape[0]
    any_space = pl.BlockSpec(memory_space=pl.ANY)
    return pl.pallas_call(
        _routed_kernel,
        grid_spec=pltpu.PrefetchScalarGridSpec(
            num_scalar_prefetch=2,
            grid=(N_EXPERTS,),
            in_specs=[any_space,
                      pl.BlockSpec((1, D_MODEL, D_EXPERT), lambda e, ft, nt: (e, 0, 0)),
                      pl.BlockSpec((1, D_MODEL, D_EXPERT), lambda e, ft, nt: (e, 0, 0)),
                      pl.BlockSpec((1, D_EXPERT, D_MODEL), lambda e, ft, nt: (e, 0, 0))],
            out_specs=any_space,
            scratch_shapes=[pltpu.VMEM((RING, MOE_TM, PACK_W), I32), pltpu.VMEM((RING, MOE_TM, PACK_W), I32),
                            pltpu.SemaphoreType.DMA((RING,)), pltpu.SemaphoreType.DMA((RING,))]),
        out_shape=jax.ShapeDtypeStruct((R, PACK_W), I32),
        compiler_params=pltpu.CompilerParams(dimension_semantics=("arbitrary",)),
        name="routed_mlp",
    )(first_tile, n_tiles, xs, wg, wu, wd)


def _combine_kernel(xp_ref, y0_ref, y1_ref, wtok_ref, lnw_ref, lnb_ref, *out_refs):
    o_ref = out_refs[-1]
    w0 = wtok_ref[:, 0:1]
    w1 = wtok_ref[:, 1:2]
    xlo, xhi = _unpack_halves(xp_ref[...])
    y0lo, y0hi = _unpack_halves(y0_ref[...])
    y1lo, y1hi = _unpack_halves(y1_ref[...])
    h = jnp.concatenate([ALPHA * xlo + (w0 * y0lo + w1 * y1lo), ALPHA * xhi + (w0 * y0hi + w1 * y1hi)], axis=1)
    o_ref[...] = _layer_norm(h, lnw_ref[...], lnb_ref[...])


def _combine_ln2(xp, yg_parts, wtok, lnw, lnb, tile=1024):
    T = xp.shape[0]
    n_parts = len(yg_parts)
    nt = T // tile // n_parts
    full = lambda a: pl.BlockSpec(a.shape, lambda i: (0,) * a.ndim)
    out = None
    for p, yg in enumerate(yg_parts):
        rows = lambda w, p=p: pl.BlockSpec((tile, w), lambda i: (i + p * nt, 0))
        in_specs = [rows(PACK_W),
                    pl.BlockSpec((tile, PACK_W), lambda i: (i, 0)),
                    pl.BlockSpec((tile, PACK_W), lambda i: (i + nt, 0)),
                    rows(WTOK_LANES), full(lnw), full(lnb)]
        args = [xp, yg, yg, wtok, lnw, lnb]
        aliases = {}
        if out is not None:
            in_specs.append(pl.BlockSpec(memory_space=pl.ANY))
            args.append(out)
            aliases = {len(args) - 1: 0}
        out = pl.pallas_call(
            _combine_kernel,
            grid=(nt,),
            in_specs=in_specs,
            out_specs=rows(D_MODEL),
            out_shape=jax.ShapeDtypeStruct((T, D_MODEL), F32),
            input_output_aliases=aliases,
            compiler_params=pltpu.CompilerParams(dimension_semantics=("arbitrary",), vmem_limit_bytes=VMEM_LIMIT),
            name="combine_ln2",
        )(*args)
    return out


def _router_params(w_grp, b_grp, w_exp, b_exp):
    rw = jnp.zeros((ROUTER_ROWS, D_MODEL), F32)
    rw = rw.at[0:N_GROUPS].set(w_grp.T).at[EXP_ROW0:EXP_ROW0 + N_EXPERTS].set(w_exp.T)
    rb = jnp.zeros((ROUTER_ROWS,), F32).at[N_GROUPS:8].set(NEG_BIG)
    rb = rb.at[0:N_GROUPS].set(b_grp).at[EXP_ROW0:EXP_ROW0 + N_EXPERTS].set(b_exp)
    rw_hi = rw.astype(BF16)
    rw_lo = (rw - rw_hi.astype(F32)).astype(BF16)
    return (rw_hi, rw_lo), rb[:, None]


def kernel(x, mem, positions, w_in, w_pool_grp, pool_scale, ret_gn_w, w_mem_kv, w_br_pool, w_br_ret, w_br_xa,
           w_out, ln1_w, ln1_b, w_grp_router, b_grp_router, w_exp_router, b_exp_router, w_exp_gate, w_exp_up,
           w_exp_down, ln2_w, ln2_b):
    B, S, D = x.shape
    assert D == D_MODEL and w_in.shape[0] == DEPTH and S % 512 == 0
    T = B * S
    M = mem.shape[1]
    l = 0
    xf = x.reshape(T, D)

    rope = _rope_table(positions.reshape(1, T))
    kv = _mem_kv(mem.reshape(B * M, D), w_mem_kv[l].astype(BF16))
    rw, rb = _router_params(w_grp_router[l], b_grp_router[l], w_exp_router[l], b_exp_router[l])
    xp, eid, wtok = _mixer(xf, rope, kv, w_in[l].astype(BF16),
                           w_pool_grp[l].astype(BF16), pool_scale[l][None, :], ret_gn_w[l].reshape(1, -1),
                           w_br_pool[l].astype(BF16), w_br_ret[l].astype(BF16), w_br_xa[l].astype(BF16),
                           w_out[l].astype(BF16), ln1_w[l][None, :], ln1_b[l][None, :], rw, rb, B, S, M)

    pos, first_tile, n_tiles = _positions(eid)
    pos2d = pos[0:TOP_K].reshape(TOP_K * T // SC_CHUNK, SC_CHUNK)
    max_tiles = (TOP_K * T + N_EXPERTS * (MOE_TM - 1)) // MOE_TM
    xs = _sc_scatter_rows(xp, pos2d, max_tiles * MOE_TM)
    ys = _routed_mlp(first_tile[:, 0], n_tiles[:, 0], xs,
                     w_exp_gate[l].reshape(N_EXPERTS, D_MODEL, D_EXPERT),
                     w_exp_up[l].reshape(N_EXPERTS, D_MODEL, D_EXPERT),
                     w_exp_down[l].reshape(N_EXPERTS, D_EXPERT, D_MODEL))
    rng = T // COMBINE_PARTS
    yg_parts = [_sc_gather_rows(ys, pos[0:TOP_K, p * rng:(p + 1) * rng].reshape(TOP_K * rng // SC_CHUNK, SC_CHUNK))
                for p in range(COMBINE_PARTS)]
    out = _combine_ln2(xp, yg_parts, wtok, ln2_w[l][None, :], ln2_b[l][None, :])
    return out.reshape(B, S, D)
```

```python
import functools

import numpy as np
import jax
import jax.numpy as jnp
from jax import lax
from jax.experimental import pallas as pl
from jax.experimental.pallas import tpu as pltpu
from jax.experimental.pallas import tpu_sc as plsc

F32 = jnp.float32
BF16 = jnp.bfloat16
I32 = jnp.int32
U32 = jnp.uint32

D_MODEL = 1024
POOL_WINDOWS = (2, 4, 8, 16)
POOL_GROUP_DIM = 128
POOL_WIDTH = 512
POOL_HALO = 16
RET_HEADS = 4
RET_QK_DIM = 128
RET_V_DIM = 256
RET_CHUNK = 128
ROPE_BASE = 10000.0
XA_HEADS = 4
XA_HEAD_DIM = 128
XA_WIDTH = 512
N_GROUPS = 4
EXPERTS_PER_GROUP = 8
N_EXPERTS = N_GROUPS * EXPERTS_PER_GROUP
D_EXPERT = 256
LN_EPS = 1e-5
DEPTH = 1
ALPHA = (2.0 * DEPTH) ** 0.25
NEG_BIG = -1e30

COL_POOL, COL_Q, COL_K, COL_V, COL_G, COL_XAQ, COL_GATES = 0, 512, 1024, 1536, 2560, 3584, 4096

VMEM_LIMIT = 56 * 1024 * 1024

TOP_K = 2
PACK_W = D_MODEL // 2
MOE_TM = 512
SC_CHUNK = 64
RANK_CHUNK = 512
COMBINE_PARTS = 2
RING = 4
STRIP = 256
LN_ROWS = 32


def _dot(a, b):
    return jnp.dot(a, b, preferred_element_type=F32)


def _dot_nt(a, b, precision=None):
    return lax.dot_general(a, b, (((1,), (1,)), ((), ())), preferred_element_type=F32, precision=precision)


def _sigmoid(z):
    return 1.0 / (1.0 + jnp.exp2(z * (-1.0 / np.log(2.0))))


def _layer_norm(h, w, b):
    mu = jnp.mean(h, axis=-1, keepdims=True)
    hc = h - mu
    var = jnp.mean(hc * hc, axis=-1, keepdims=True)
    return hc * lax.rsqrt(var + LN_EPS) * w + b


def _rope_kernel(pos_ref, freq_ref, cos_ref, sin_ref, cos_t_ref, sin_t_ref):
    ang = freq_ref[...] * pos_ref[...].astype(F32)
    cos_t = jnp.cos(ang)
    sin_t = jnp.sin(ang)
    cos_t_ref[...] = cos_t
    sin_t_ref[...] = sin_t
    cos_ref[...] = jnp.transpose(jnp.concatenate([cos_t, cos_t], axis=0))
    sin_ref[...] = jnp.transpose(jnp.concatenate([-sin_t, sin_t], axis=0))


def _rope_table(pos_row, tile=2048):
    T = pos_row.shape[1]
    half = RET_QK_DIM // 2
    inv_freq = (ROPE_BASE ** (-np.arange(half, dtype=np.float64) / half)).astype(np.float32)
    freq = jnp.asarray(inv_freq[:, None])
    out = pl.BlockSpec((tile, RET_QK_DIM), lambda i: (i, 0))
    out_t = pl.BlockSpec((half, tile), lambda i: (0, i))
    return pl.pallas_call(
        _rope_kernel,
        grid=(T // tile,),
        in_specs=[pl.BlockSpec((1, tile), lambda i: (0, i)), pl.BlockSpec((half, 1), lambda i: (0, 0))],
        out_specs=[out, out, out_t, out_t],
        out_shape=[jax.ShapeDtypeStruct((T, RET_QK_DIM), F32)] * 2 + [jax.ShapeDtypeStruct((half, T), F32)] * 2,
        name="rope_table",
    )(pos_row, freq)


POOL_SUB = 256


def _pool_bands():
    r = np.arange(POOL_SUB)[:, None]
    c = np.arange(POOL_SUB)[None, :]
    ch = np.arange(POOL_HALO)[None, :] - POOL_HALO
    main = np.stack([((r - c >= 0) & (r - c < w)) for w in POOL_WINDOWS]).astype(np.float32)
    halo = np.stack([((r - ch >= 0) & (r - ch < w)) for w in POOL_WINDOWS]).astype(np.float32)
    return jnp.asarray(main, BF16), jnp.asarray(halo, BF16)


def _pool_branch(ub, j, halo_ref, bmain_ref, bhalo_ref, wg_ref, scale_ref, o_ref, filler):
    tile = ub.shape[0]
    s0 = j * tile
    slot = lax.rem(j, 2)
    blocks = [(sb * POOL_SUB, g) for sb in range(tile // POOL_SUB) for g in range(len(POOL_WINDOWS))]
    wsum = {}
    for r0, g in blocks:
        cols = slice(g * POOL_GROUP_DIM, (g + 1) * POOL_GROUP_DIM)
        prev = halo_ref[slot] if r0 == 0 else ub[r0 - POOL_HALO:r0]
        wsum[r0, g] = _dot(bmain_ref[g], ub[r0:r0 + POOL_SUB, cols]) + _dot(bhalo_ref[g], prev[:, cols])
    filler()
    for r0, g in blocks:
        cols = slice(g * POOL_GROUP_DIM, (g + 1) * POOL_GROUP_DIM)
        pos = s0 + r0 + lax.broadcasted_iota(I32, (POOL_SUB, POOL_GROUP_DIM), 0)
        cnt = jnp.minimum(pos + 1, POOL_WINDOWS[g]).astype(F32)
        pooled = wsum[r0, g] / cnt - ub[r0:r0 + POOL_SUB, cols].astype(F32)
        mixed = _dot(pooled.astype(BF16), wg_ref[g]) * scale_ref[:, cols]
        o_ref[r0:r0 + POOL_SUB, cols] = mixed.astype(BF16)
    halo_ref[1 - slot] = ub[tile - POOL_HALO:tile]


def _ret_consts():
    h = np.arange(RET_HEADS, dtype=np.float64)
    log_gamma = np.log1p(-np.exp2(-5.0 - h))
    pos = np.arange(RET_CHUNK, dtype=np.float64)
    diff = pos[:, None] - pos[None, :]
    kscale = RET_QK_DIM ** -0.5
    dmask = kscale * np.where(diff >= 0, np.exp(log_gamma[:, None, None] * np.maximum(diff, 0.0)), 0.0)
    qdec = np.exp(log_gamma[:, None] * (pos + 1.0)[None, :])
    kdec = kscale * np.exp(log_gamma[:, None] * (RET_CHUNK - 1.0 - pos)[None, :])
    cdec = np.exp(log_gamma * RET_CHUNK)
    lanes = lambda a: np.broadcast_to(a[:, :, None], (RET_HEADS, RET_CHUNK, RET_QK_DIM))
    kdec_t = np.broadcast_to(kdec[:, None, :], (RET_HEADS, RET_QK_DIM, RET_CHUNK))
    return (jnp.asarray(dmask, F32), jnp.asarray(lanes(qdec), F32), jnp.asarray(kdec_t, F32),
            tuple(float(v) for v in cdec))


def _retention_branch(q, k_t, v, silu_g, cos_ref, sin_ref, cos_t_ref, sin_t_ref, dmask_ref, qdec_ref, kdec_ref,
                      gnw_ref, state_ref, rq_ref, rqd_ref, rkt_ref, rkdt_ref, o_ref, cdec, fillers):
    tile = q.shape[0]
    n_chunks = tile // RET_CHUNK
    half = RET_QK_DIM // 2
    cos = cos_ref[...]
    sin = sin_ref[...]
    cos_t = cos_t_ref[...]
    sin_t = sin_t_ref[...]
    for h in range(RET_HEADS):
        qk = slice(h * RET_QK_DIM, (h + 1) * RET_QK_DIM)
        qh = q[:, qk]
        qr = qh * cos + pltpu.roll(qh, half, 1) * sin
        rq_ref[:, qk] = qr.astype(BF16)
        rqd_ref[:, qk] = (qr * jnp.concatenate([qdec_ref[h]] * n_chunks, axis=0)).astype(BF16)
        k1 = k_t[h * RET_QK_DIM:h * RET_QK_DIM + half]
        k2 = k_t[h * RET_QK_DIM + half:(h + 1) * RET_QK_DIM]
        kr_t = jnp.concatenate([k1 * cos_t - k2 * sin_t, k2 * cos_t + k1 * sin_t], axis=0)
        rkt_ref[qk, :] = kr_t.astype(BF16)
        rkdt_ref[qk, :] = (kr_t * jnp.concatenate([kdec_ref[h]] * n_chunks, axis=1)).astype(BF16)

    for c in range(n_chunks):
        rows = slice(c * RET_CHUNK, (c + 1) * RET_CHUNK)
        heads = [slice(h * RET_QK_DIM, (h + 1) * RET_QK_DIM) for h in range(RET_HEADS)]
        raw = [_dot(rq_ref[rows, qk], rkt_ref[qk, rows]) for qk in heads]
        fillers[c]()
        for h in range(RET_HEADS):
            qk = heads[h]
            v_cols = slice(h * RET_V_DIM, (h + 1) * RET_V_DIM)
            vh = v[rows, v_cols]
            scores = raw[h] * dmask_ref[h]
            st = state_ref[h]
            lhs = jnp.concatenate([scores.astype(BF16), rqd_ref[rows, qk]], axis=1)
            y = _dot(lhs, jnp.concatenate([vh, st.astype(BF16)], axis=0))
            state_ref[h] = cdec[h] * st + _dot(rkdt_ref[qk, rows], vh)
            mu = jnp.mean(y, axis=-1, keepdims=True)
            yc = y - mu
            var = jnp.mean(yc * yc, axis=-1, keepdims=True)
            yn = yc * lax.rsqrt(var + LN_EPS) * gnw_ref[:, v_cols]
            o_ref[rows, v_cols] = (silu_g[h][rows] * yn).astype(BF16)


def _memkv_kernel(m_ref, w_ref, o_ref):
    o_ref[...] = _dot(m_ref[...].astype(BF16), w_ref[...]).astype(BF16)


def _mem_kv(memf, w_b):
    M, D = memf.shape
    N = w_b.shape[1]
    return pl.pallas_call(
        _memkv_kernel,
        grid=(1,),
        in_specs=[pl.BlockSpec((M, D), lambda i: (0, 0)), pl.BlockSpec((D, N), lambda i: (0, 0))],
        out_specs=pl.BlockSpec((M, N), lambda i: (0, 0)),
        out_shape=jax.ShapeDtypeStruct((M, N), BF16),
        name="mem_kv",
    )(memf, w_b)


def _cross_attention_branch(xq, k_ref, v_ref, o_ref):
    scale = XA_HEAD_DIM ** -0.5
    for h in range(XA_HEADS):
        cols = slice(h * XA_HEAD_DIM, (h + 1) * XA_HEAD_DIM)
        s = _dot_nt(xq[:, cols], k_ref[:, cols]) * scale
        m = jnp.max(s, axis=-1, keepdims=True)
        p = jnp.exp(s - m)
        l = jnp.sum(p, axis=-1, keepdims=True)
        o = _dot(p.astype(BF16), v_ref[:, cols]) / l
        o_ref[:, cols] = o.astype(BF16)


ROUTER_ROWS = 128
WTOK_LANES = 128
EXP_ROW0 = 8


def _route(logits_t):
    gl = logits_t[0:8]
    gmax = jnp.max(gl, axis=0, keepdims=True)
    p_grp = 1.0 / jnp.sum(jnp.exp(gl - gmax), axis=0, keepdims=True)
    idx8 = lax.broadcasted_iota(jnp.int32, gl.shape, 0)
    gsel = jnp.min(jnp.where(gl == gmax, idx8, 8), axis=0, keepdims=True)
    cl = jnp.zeros_like(gl)
    for g in range(N_GROUPS):
        r0 = EXP_ROW0 + g * EXPERTS_PER_GROUP
        cl = cl + jnp.where(gsel == g, logits_t[r0:r0 + EXPERTS_PER_GROUP], 0.0)
    v1 = jnp.max(cl, axis=0, keepdims=True)
    i1 = jnp.min(jnp.where(cl == v1, idx8, 8), axis=0, keepdims=True)
    cl2 = jnp.where(idx8 == i1, -jnp.inf, cl)
    v2 = jnp.max(cl2, axis=0, keepdims=True)
    i2 = jnp.min(jnp.where(cl2 == v2, idx8, 8), axis=0, keepdims=True)
    e21 = jnp.exp(v2 - v1)
    w1 = p_grp / (1.0 + e21)
    w2 = p_grp * e21 / (1.0 + e21)
    return gsel * EXPERTS_PER_GROUP + i1, gsel * EXPERTS_PER_GROUP + i2, w1, w2


def _pack_halves(v):
    half = v.shape[1] // 2
    lo = lax.bitcast_convert_type(v[:, :half].astype(BF16).astype(F32), U32)
    hi = lax.bitcast_convert_type(v[:, half:].astype(BF16).astype(F32), U32)
    return lax.bitcast_convert_type(lax.shift_right_logical(lo, U32(16)) | hi, I32)


def _unpack_halves(w):
    u = lax.bitcast_convert_type(w, U32)
    lo = lax.bitcast_convert_type(lax.shift_left(u, U32(16)), F32)
    hi = lax.bitcast_convert_type(u & U32(0xFFFF0000), F32)
    return lo, hi


def _mixer_kernel(x_ref, cos_ref, sin_ref, cos_t_ref, sin_t_ref, km_ref, vm_ref, win_ref, wgrp_ref,
                  pscale_ref, bmain_ref, bhalo_ref, dmask_ref, qdec_ref, kdec_ref, gnw_ref, wp_ref, wr_ref,
                  wa_ref, wo_ref, lnw_ref, lnb_ref, rw_ref, rb_ref, xp_ref, eid_ref, wtok_ref,
                  state_ref, halo_ref, ypool_ref, yret_ref, yxa_ref, rq_ref, rqd_ref, rkt_ref, rkdt_ref, wkt_ref,
                  *, tile, cdec):
    j = pl.program_id(1)

    @pl.when(jnp.logical_and(pl.program_id(0) == 0, j == 0))
    def _():
        wk = win_ref[:, COL_K:COL_K + RET_HEADS * RET_QK_DIM].astype(F32)
        wkt_ref[...] = jnp.transpose(wk).astype(BF16)

    @pl.when(j == 0)
    def _():
        state_ref[...] = jnp.zeros_like(state_ref)
        halo_ref[...] = jnp.zeros_like(halo_ref)

    x = x_ref[...]
    xb = x.astype(BF16)

    def proj(col, width):
        return _dot(xb, win_ref[:, col:col + width])

    part = {}
    strips = [slice(c, c + STRIP) for c in range(0, D_MODEL, STRIP)]

    def gate(branch, cols):
        return _sigmoid(proj(COL_GATES + branch * D_MODEL + cols.start, STRIP))

    def pool_part():
        def pool_gates():
            part["pool_gate"] = [gate(0, c) for c in strips]

        _pool_branch(proj(COL_POOL, POOL_WIDTH).astype(BF16), j, halo_ref, bmain_ref, bhalo_ref, wgrp_ref,
                     pscale_ref, ypool_ref, pool_gates)
        part["pool"] = [part["pool_gate"][i] * _dot(ypool_ref[...], wp_ref[:, c]) for i, c in enumerate(strips)]

    def xa_part():
        _cross_attention_branch(proj(COL_XAQ, XA_WIDTH).astype(BF16), km_ref, vm_ref, yxa_ref)
        part["xa"] = [gate(2, c) * _dot(yxa_ref[...], wa_ref[:, c]) for c in strips]

    def ret_gate_part():
        part["ret_gate"] = [gate(1, c) for c in strips]

    silu_g = []
    for h in range(RET_HEADS):
        gh = proj(COL_G + h * RET_V_DIM, RET_V_DIM)
        silu_g.append(gh * _sigmoid(gh))
    fillers = [pool_part, xa_part, ret_gate_part] + [lambda: None] * (tile // RET_CHUNK - 3)
    _retention_branch(proj(COL_Q, RET_HEADS * RET_QK_DIM), _dot_nt(wkt_ref[...], xb),
                      proj(COL_V, RET_HEADS * RET_V_DIM).astype(BF16), silu_g, cos_ref, sin_ref, cos_t_ref,
                      sin_t_ref, dmask_ref, qdec_ref, kdec_ref, gnw_ref, state_ref, rq_ref, rqd_ref, rkt_ref,
                      rkdt_ref, yret_ref, cdec, fillers)
    merged = jnp.concatenate(
        [(part["pool"][i] + part["ret_gate"][i] * _dot(yret_ref[...], wr_ref[:, c]) + part["xa"][i]).astype(BF16)
         for i, c in enumerate(strips)], axis=1)
    h = jnp.concatenate([ALPHA * x[:, c] + _dot(merged, wo_ref[:, c]) for c in strips], axis=1)
    x1 = jnp.concatenate([_layer_norm(h[r:r + LN_ROWS], lnw_ref[...], lnb_ref[...])
                          for r in range(0, tile, LN_ROWS)], axis=0)
    xp_ref[...] = _pack_halves(x1)
    x1_hi = x1.astype(BF16)
    x1_lo = (x1 - x1_hi.astype(F32)).astype(BF16)
    p4 = _dot(jnp.concatenate([x1_hi, x1_lo], axis=0), rw_ref[...])
    logits = (p4[:tile, :ROUTER_ROWS] + p4[:tile, ROUTER_ROWS:]) + (p4[tile:, :ROUTER_ROWS] + p4[tile:, ROUTER_ROWS:])
    e0, e1, w0, w1 = _route(jnp.transpose(logits) + rb_ref[...])
    eid_ref[...] = jnp.concatenate([e0, e1, jnp.zeros((8 - TOP_K, tile), I32)], axis=0)
    w_t = jnp.concatenate([w0, w1, jnp.zeros((WTOK_LANES - TOP_K, tile), F32)], axis=0)
    wtok_ref[...] = jnp.transpose(w_t)


def _mixer(xf, rope, kv, win, wgrp, pscale, gnw, wp, wr, wa, wo, lnw, lnb, rw, rb, batch, seq, mem_len,
           tile=512):
    T = xf.shape[0]
    nj = seq // tile
    cos, sin, cos_t, sin_t = rope
    bmain, bhalo = _pool_bands()
    dmask, qdec, kdec, cdec = _ret_consts()
    resident = lambda a: pl.BlockSpec(a.shape, lambda b, j: (0,) * a.ndim, pipeline_mode=pl.Buffered(1))
    rowblk = lambda w: pl.BlockSpec((tile, w), lambda b, j: (b * nj + j, 0))
    colblk = lambda r: pl.BlockSpec((r, tile), lambda b, j: (0, b * nj + j))
    consts = (win, wgrp, pscale, bmain, bhalo, dmask, qdec, kdec, gnw, wp, wr, wa, wo, lnw, lnb, rw, rb)
    return pl.pallas_call(
        functools.partial(_mixer_kernel, tile=tile, cdec=cdec),
        grid=(batch, nj),
        in_specs=[rowblk(D_MODEL), rowblk(RET_QK_DIM), rowblk(RET_QK_DIM),
                  colblk(RET_QK_DIM // 2), colblk(RET_QK_DIM // 2),
                  pl.BlockSpec((mem_len, XA_WIDTH), lambda b, j: (b, 0)),
                  pl.BlockSpec((mem_len, XA_WIDTH), lambda b, j: (b, 1))] + [resident(a) for a in consts],
        out_specs=[rowblk(PACK_W), colblk(8), rowblk(WTOK_LANES)],
        out_shape=[jax.ShapeDtypeStruct((T, PACK_W), I32), jax.ShapeDtypeStruct((8, T), I32),
                   jax.ShapeDtypeStruct((T, WTOK_LANES), F32)],
        scratch_shapes=[pltpu.VMEM((RET_HEADS, RET_QK_DIM, RET_V_DIM), F32),
                        pltpu.VMEM((2, POOL_HALO, POOL_WIDTH), BF16),
                        pltpu.VMEM((tile, POOL_WIDTH), BF16),
                        pltpu.VMEM((tile, RET_HEADS * RET_V_DIM), BF16),
                        pltpu.VMEM((tile, XA_WIDTH), BF16),
                        pltpu.VMEM((tile, RET_HEADS * RET_QK_DIM), BF16),
                        pltpu.VMEM((tile, RET_HEADS * RET_QK_DIM), BF16),
                        pltpu.VMEM((RET_HEADS * RET_QK_DIM, tile), BF16),
                        pltpu.VMEM((RET_HEADS * RET_QK_DIM, tile), BF16),
                        pltpu.VMEM((RET_HEADS * RET_QK_DIM, D_MODEL), BF16)],
        compiler_params=pltpu.CompilerParams(dimension_semantics=("arbitrary", "arbitrary"),
                                             vmem_limit_bytes=VMEM_LIMIT),
        name="mixer",
    )(xf, cos, sin, cos_t, sin_t, kv, kv, *consts)


META_LANES = 128


def _positions_kernel(eid_ref, tri_ref, low_ref, pos_ref, first_tile_ref, n_tiles_ref, *, n_tok):
    n_chunks = n_tok // RANK_CHUNK
    erow = lax.broadcasted_iota(I32, (N_EXPERTS, RANK_CHUNK), 0)

    def onehot(c):
        sl = slice(c * RANK_CHUNK, (c + 1) * RANK_CHUNK)
        m0 = eid_ref[0:1, sl] == erow
        m1 = eid_ref[1:2, sl] == erow
        return m0, m1, jnp.where(m0, 1.0, 0.0) + jnp.where(m1, 1.0, 0.0)

    counts = jnp.zeros((N_EXPERTS, 1), F32)
    for c in range(n_chunks):
        counts = counts + jnp.sum(onehot(c)[2], axis=1, keepdims=True)
    ptiles = jnp.floor((counts + (MOE_TM - 1)) * (1.0 / MOE_TM))
    ptiles_b = jnp.broadcast_to(ptiles, (N_EXPERTS, 128)).astype(BF16)
    start = _dot(low_ref[...], ptiles_b)[:, 0:1] * MOE_TM

    pos_ref[...] = jnp.zeros_like(pos_ref)
    carry = start - 1.0
    for c in range(n_chunks):
        sl = slice(c * RANK_CHUNK, (c + 1) * RANK_CHUNK)
        m0, m1, oh = onehot(c)
        rank = _dot(oh.astype(BF16), tri_ref[...]) + carry
        pos_ref[0:1, sl] = jnp.sum(jnp.where(m0, rank, 0.0), axis=0, keepdims=True).astype(I32)
        pos_ref[1:2, sl] = jnp.sum(jnp.where(m1, rank, 0.0), axis=0, keepdims=True).astype(I32)
        carry = carry + jnp.sum(oh, axis=1, keepdims=True)

    first_tile_ref[...] = jnp.broadcast_to(start * (1.0 / MOE_TM), first_tile_ref.shape).astype(I32)
    n_tiles_ref[...] = jnp.broadcast_to(ptiles, n_tiles_ref.shape).astype(I32)


def _positions(eid):
    T = eid.shape[1]
    r = np.arange(RANK_CHUNK)
    tri = jnp.asarray(r[:, None] <= r[None, :], BF16)
    e = np.arange(N_EXPERTS)
    low = jnp.asarray(e[None, :] < e[:, None], BF16)
    full = lambda a: pl.BlockSpec(a.shape, lambda i: (0,) * a.ndim)
    return pl.pallas_call(
        functools.partial(_positions_kernel, n_tok=T),
        grid=(1,),
        in_specs=[full(eid), full(tri), full(low)],
        out_specs=[pl.BlockSpec((8, T), lambda i: (0, 0)),
                   pl.BlockSpec((N_EXPERTS, META_LANES), lambda i: (0, 0)),
                   pl.BlockSpec((N_EXPERTS, META_LANES), lambda i: (0, 0))],
        out_shape=[jax.ShapeDtypeStruct((8, T), I32), jax.ShapeDtypeStruct((N_EXPERTS, META_LANES), I32),
                   jax.ShapeDtypeStruct((N_EXPERTS, META_LANES), I32)],
        name="route_positions",
    )(eid, tri, low)


def _sc_workers():
    info = plsc.get_sparse_core_info()
    return info.num_cores, info.num_cores * info.num_subcores


def _sc_scatter_rows(xp, pos2d, n_out):
    T, W = xp.shape
    n_cores, n_workers = _sc_workers()
    cpw = T // SC_CHUNK // n_workers
    mesh = plsc.VectorSubcoreMesh(core_axis_name="c", subcore_axis_name="s")

    @functools.partial(
        pl.kernel, mesh=mesh, out_type=jax.ShapeDtypeStruct((n_out, W), I32),
        scratch_types=[pltpu.VMEM((TOP_K * cpw, SC_CHUNK), I32), pltpu.VMEM((2, SC_CHUNK, W), I32),
                       pltpu.SemaphoreType.DMA((2,)), pltpu.SemaphoreType.DMA((2,))],
        name="sc_scatter_rows")
    def k(x_hbm, pos_hbm, out_hbm, idx_v, rows_v, rd_sem, wr_sem):
        wid = lax.axis_index("s") * n_cores + lax.axis_index("c")
        for s in range(TOP_K):
            pltpu.sync_copy(pos_hbm.at[pl.ds(s * (T // SC_CHUNK) + wid * cpw, cpw)],
                            idx_v.at[pl.ds(s * cpw, cpw)])

        def read(j):
            return pltpu.make_async_copy(x_hbm.at[pl.ds((wid * cpw + j) * SC_CHUNK, SC_CHUNK)],
                                         rows_v.at[j % 2], rd_sem.at[j % 2])

        def write(j, s):
            return pltpu.make_async_copy(rows_v.at[j % 2], out_hbm.at[idx_v.at[s * cpw + j]], wr_sem.at[j % 2])

        _sc_two_buffer_stream(cpw, read, lambda j: [write(j, s) for s in range(TOP_K)])

    return k(xp, pos2d)


def _sc_two_buffer_stream(n, read, writes):
    read(0).start()
    for j in range(n):
        read(j).wait()
        if j + 1 < n:
            if j >= 1:
                for w in writes(j - 1):
                    w.wait()
            read(j + 1).start()
        for w in writes(j):
            w.start()
    for j in range(max(n - 2, 0), n):
        for w in writes(j):
            w.wait()


def _sc_gather_rows(y, idx2d):
    W = y.shape[1]
    n = idx2d.shape[0] * SC_CHUNK
    n_cores, n_workers = _sc_workers()
    cpw = n // SC_CHUNK // n_workers
    mesh = plsc.VectorSubcoreMesh(core_axis_name="c", subcore_axis_name="s")

    @functools.partial(
        pl.kernel, mesh=mesh, out_type=jax.ShapeDtypeStruct((n, W), I32),
        scratch_types=[pltpu.VMEM((cpw, SC_CHUNK), I32), pltpu.VMEM((2, SC_CHUNK, W), I32),
                       pltpu.SemaphoreType.DMA((2,)), pltpu.SemaphoreType.DMA((2,))],
        name="sc_gather_rows")
    def k(y_hbm, idx_hbm, out_hbm, idx_v, rows_v, rd_sem, wr_sem):
        wid = lax.axis_index("s") * n_cores + lax.axis_index("c")
        pltpu.sync_copy(idx_hbm.at[pl.ds(wid * cpw, cpw)], idx_v)

        def read(j):
            return pltpu.make_async_copy(y_hbm.at[idx_v.at[j]], rows_v.at[j % 2], rd_sem.at[j % 2])

        def write(j):
            return pltpu.make_async_copy(rows_v.at[j % 2], out_hbm.at[pl.ds((wid * cpw + j) * SC_CHUNK, SC_CHUNK)],
                                         wr_sem.at[j % 2])

        _sc_two_buffer_stream(cpw, read, lambda j: [write(j)])

    return k(y, idx2d)


def _routed_kernel(first_ref, count_ref, xs_hbm, wg_ref, wu_ref, wd_ref, ys_hbm, xbuf, ybuf, in_sem, out_sem):
    e = pl.program_id(0)
    last = pl.num_programs(0) - 1
    total = first_ref[last] + count_ref[last]

    def in_copy(g):
        slot = lax.rem(g, RING)
        return pltpu.make_async_copy(xs_hbm.at[pl.ds(g * MOE_TM, MOE_TM)], xbuf.at[slot], in_sem.at[slot])

    def out_copy(g):
        slot = lax.rem(g, RING)
        return pltpu.make_async_copy(ybuf.at[slot], ys_hbm.at[pl.ds(g * MOE_TM, MOE_TM)], out_sem.at[slot])

    @pl.when(e == 0)
    def _():
        for g0 in range(RING - 1):
            @pl.when(g0 < total)
            def _():
                in_copy(g0).start()

    wg = wg_ref[0].astype(BF16)
    wu = wu_ref[0].astype(BF16)
    wd = wd_ref[0].astype(BF16)

    def tile_step(i, carry):
        g = first_ref[e] + i
        slot = lax.rem(g, RING)
        in_copy(g).wait()

        @pl.when(g + RING - 1 < total)
        def _():
            in_copy(g + RING - 1).start()

        @pl.when(g >= RING)
        def _():
            out_copy(g - RING).wait()

        lo, hi = _unpack_halves(xbuf[slot])
        lo = lo.astype(BF16)
        hi = hi.astype(BF16)
        a = _dot(lo, wg[:PACK_W]) + _dot(hi, wg[PACK_W:])
        b = _dot(lo, wu[:PACK_W]) + _dot(hi, wu[PACK_W:])
        act = (a * _sigmoid(a) * b).astype(BF16)
        ybuf[slot] = _pack_halves(_dot(act, wd))
        out_copy(g).start()
        return carry

    lax.fori_loop(0, count_ref[e], tile_step, 0)

    @pl.when(e == last)
    def _():
        for back in range(RING, 0, -1):
            @pl.when(total >= back)
            def _():
                out_copy(total - back).wait()


def _routed_mlp(first_tile, n_tiles, xs, wg, wu, wd):
    R = xs.shape[0]
    any_space = pl.BlockSpec(memory_space=pl.ANY)
    return pl.pallas_call(
        _routed_kernel,
        grid_spec=pltpu.PrefetchScalarGridSpec(
            num_scalar_prefetch=2,
            grid=(N_EXPERTS,),
            in_specs=[any_space,
                      pl.BlockSpec((1, D_MODEL, D_EXPERT), lambda e, ft, nt: (e, 0, 0)),
                      pl.BlockSpec((1, D_MODEL, D_EXPERT), lambda e, ft, nt: (e, 0, 0)),
                      pl.BlockSpec((1, D_EXPERT, D_MODEL), lambda e, ft, nt: (e, 0, 0))],
            out_specs=any_space,
            scratch_shapes=[pltpu.VMEM((RING, MOE_TM, PACK_W), I32), pltpu.VMEM((RING, MOE_TM, PACK_W), I32),
                            pltpu.SemaphoreType.DMA((RING,)), pltpu.SemaphoreType.DMA((RING,))]),
        out_shape=jax.ShapeDtypeStruct((R, PACK_W), I32),
        compiler_params=pltpu.CompilerParams(dimension_semantics=("arbitrary",)),
        name="routed_mlp",
    )(first_tile, n_tiles, xs, wg, wu, wd)


def _combine_kernel(xp_ref, y0_ref, y1_ref, wtok_ref, lnw_ref, lnb_ref, *out_refs):
    o_ref = out_refs[-1]
    w0 = wtok_ref[:, 0:1]
    w1 = wtok_ref[:, 1:2]
    xlo, xhi = _unpack_halves(xp_ref[...])
    y0lo, y0hi = _unpack_halves(y0_ref[...])
    y1lo, y1hi = _unpack_halves(y1_ref[...])
    h = jnp.concatenate([ALPHA * xlo + (w0 * y0lo + w1 * y1lo), ALPHA * xhi + (w0 * y0hi + w1 * y1hi)], axis=1)
    o_ref[...] = _layer_norm(h, lnw_ref[...], lnb_ref[...])


def _combine_ln2(xp, yg_parts, wtok, lnw, lnb, tile=1024):
    T = xp.shape[0]
    n_parts = len(yg_parts)
    nt = T // tile // n_parts
    full = lambda a: pl.BlockSpec(a.shape, lambda i: (0,) * a.ndim)
    out = None
    for p, yg in enumerate(yg_parts):
        rows = lambda w, p=p: pl.BlockSpec((tile, w), lambda i: (i + p * nt, 0))
        in_specs = [rows(PACK_W),
                    pl.BlockSpec((tile, PACK_W), lambda i: (i, 0)),
                    pl.BlockSpec((tile, PACK_W), lambda i: (i + nt, 0)),
                    rows(WTOK_LANES), full(lnw), full(lnb)]
        args = [xp, yg, yg, wtok, lnw, lnb]
        aliases = {}
        if out is not None:
            in_specs.append(pl.BlockSpec(memory_space=pl.ANY))
            args.append(out)
            aliases = {len(args) - 1: 0}
        out = pl.pallas_call(
            _combine_kernel,
            grid=(nt,),
            in_specs=in_specs,
            out_specs=rows(D_MODEL),
            out_shape=jax.ShapeDtypeStruct((T, D_MODEL), F32),
            input_output_aliases=aliases,
            compiler_params=pltpu.CompilerParams(dimension_semantics=("arbitrary",), vmem_limit_bytes=VMEM_LIMIT),
            name="combine_ln2",
        )(*args)
    return out


def _router_params(w_grp, b_grp, w_exp, b_exp):
    rw = jnp.zeros((D_MODEL, ROUTER_ROWS), F32)
    rw = rw.at[:, 0:N_GROUPS].set(w_grp).at[:, EXP_ROW0:EXP_ROW0 + N_EXPERTS].set(w_exp)
    rb = jnp.zeros((ROUTER_ROWS,), F32).at[N_GROUPS:8].set(NEG_BIG)
    rb = rb.at[0:N_GROUPS].set(b_grp).at[EXP_ROW0:EXP_ROW0 + N_EXPERTS].set(b_exp)
    rw_hi = rw.astype(BF16)
    rw_lo = (rw - rw_hi.astype(F32)).astype(BF16)
    return jnp.concatenate([rw_hi, rw_lo], axis=1), rb[:, None]


def kernel(x, mem, positions, w_in, w_pool_grp, pool_scale, ret_gn_w, w_mem_kv, w_br_pool, w_br_ret, w_br_xa,
           w_out, ln1_w, ln1_b, w_grp_router, b_grp_router, w_exp_router, b_exp_router, w_exp_gate, w_exp_up,
           w_exp_down, ln2_w, ln2_b):
    B, S, D = x.shape
    assert D == D_MODEL and w_in.shape[0] == DEPTH and S % 512 == 0
    T = B * S
    M = mem.shape[1]
    l = 0
    xf = x.reshape(T, D)

    rope = _rope_table(positions.reshape(1, T))
    kv = _mem_kv(mem.reshape(B * M, D), w_mem_kv[l].astype(BF16))
    rw, rb = _router_params(w_grp_router[l], b_grp_router[l], w_exp_router[l], b_exp_router[l])
    xp, eid, wtok = _mixer(xf, rope, kv, w_in[l].astype(BF16),
                           w_pool_grp[l].astype(BF16), pool_scale[l][None, :], ret_gn_w[l].reshape(1, -1),
                           w_br_pool[l].astype(BF16), w_br_ret[l].astype(BF16), w_br_xa[l].astype(BF16),
                           w_out[l].astype(BF16), ln1_w[l][None, :], ln1_b[l][None, :], rw, rb, B, S, M)

    pos, first_tile, n_tiles = _positions(eid)
    pos2d = pos[0:TOP_K].reshape(TOP_K * T // SC_CHUNK, SC_CHUNK)
    max_tiles = (TOP_K * T + N_EXPERTS * (MOE_TM - 1)) // MOE_TM
    xs = _sc_scatter_rows(xp, pos2d, max_tiles * MOE_TM)
    ys = _routed_mlp(first_tile[:, 0], n_tiles[:, 0], xs,
                     w_exp_gate[l].reshape(N_EXPERTS, D_MODEL, D_EXPERT),
                     w_exp_up[l].reshape(N_EXPERTS, D_MODEL, D_EXPERT),
                     w_exp_down[l].reshape(N_EXPERTS, D_EXPERT, D_MODEL))
    rng = T // COMBINE_PARTS
    yg_parts = [_sc_gather_rows(ys, pos[0:TOP_K, p * rng:(p + 1) * rng].reshape(TOP_K * rng // SC_CHUNK, SC_CHUNK))
                for p in range(COMBINE_PARTS)]
    out = _combine_ln2(xp, yg_parts, wtok, ln2_w[l][None, :], ln2_b[l][None, :])
    return out.reshape(B, S, D)
```

```python
import functools

import numpy as np
import jax
import jax.numpy as jnp
from jax import lax
from jax.experimental import pallas as pl
from jax.experimental.pallas import tpu as pltpu
from jax.experimental.pallas import tpu_sc as plsc

F32 = jnp.float32
BF16 = jnp.bfloat16
I32 = jnp.int32
U32 = jnp.uint32

D_MODEL = 1024
POOL_WINDOWS = (2, 4, 8, 16)
POOL_GROUP_DIM = 128
POOL_WIDTH = 512
POOL_HALO = 16
RET_HEADS = 4
RET_QK_DIM = 128
RET_V_DIM = 256
RET_CHUNK = 128
ROPE_BASE = 10000.0
XA_HEADS = 4
XA_HEAD_DIM = 128
XA_WIDTH = 512
N_GROUPS = 4
EXPERTS_PER_GROUP = 8
N_EXPERTS = N_GROUPS * EXPERTS_PER_GROUP
D_EXPERT = 256
LN_EPS = 1e-5
DEPTH = 1
ALPHA = (2.0 * DEPTH) ** 0.25
NEG_BIG = -1e30

COL_POOL, COL_Q, COL_K, COL_V, COL_G, COL_XAQ, COL_GATES = 0, 512, 1024, 1536, 2560, 3584, 4096

VMEM_LIMIT = 56 * 1024 * 1024

TOP_K = 2
PACK_W = D_MODEL // 2
MOE_TM = 512
SC_CHUNK = 64
RANK_CHUNK = 512
COMBINE_PARTS = 2
RING = 6
STRIP = 256
LN_ROWS = 32


def _dot(a, b):
    return jnp.dot(a, b, preferred_element_type=F32)


def _dot_nt(a, b, precision=None):
    return lax.dot_general(a, b, (((1,), (1,)), ((), ())), preferred_element_type=F32, precision=precision)


def _sigmoid(z):
    return 1.0 / (1.0 + jnp.exp2(z * (-1.0 / np.log(2.0))))


def _layer_norm(h, w, b):
    mu = jnp.mean(h, axis=-1, keepdims=True)
    hc = h - mu
    var = jnp.mean(hc * hc, axis=-1, keepdims=True)
    return hc * lax.rsqrt(var + LN_EPS) * w + b


def _rope_kernel(pos_ref, freq_ref, cos_ref, sin_ref, cos_t_ref, sin_t_ref):
    ang = freq_ref[...] * pos_ref[...].astype(F32)
    cos_t = jnp.cos(ang)
    sin_t = jnp.sin(ang)
    cos_t_ref[...] = cos_t
    sin_t_ref[...] = sin_t
    cos_ref[...] = jnp.transpose(jnp.concatenate([cos_t, cos_t], axis=0))
    sin_ref[...] = jnp.transpose(jnp.concatenate([-sin_t, sin_t], axis=0))


def _rope_table(pos_row, tile=2048):
    T = pos_row.shape[1]
    half = RET_QK_DIM // 2
    inv_freq = (ROPE_BASE ** (-np.arange(half, dtype=np.float64) / half)).astype(np.float32)
    freq = jnp.asarray(inv_freq[:, None])
    out = pl.BlockSpec((tile, RET_QK_DIM), lambda i: (i, 0))
    out_t = pl.BlockSpec((half, tile), lambda i: (0, i))
    return pl.pallas_call(
        _rope_kernel,
        grid=(T // tile,),
        in_specs=[pl.BlockSpec((1, tile), lambda i: (0, i)), pl.BlockSpec((half, 1), lambda i: (0, 0))],
        out_specs=[out, out, out_t, out_t],
        out_shape=[jax.ShapeDtypeStruct((T, RET_QK_DIM), F32)] * 2 + [jax.ShapeDtypeStruct((half, T), F32)] * 2,
        name="rope_table",
    )(pos_row, freq)


POOL_SUB = 256


def _pool_bands():
    r = np.arange(POOL_SUB)[:, None]
    c = np.arange(POOL_SUB)[None, :]
    ch = np.arange(POOL_HALO)[None, :] - POOL_HALO
    main = np.stack([((r - c >= 0) & (r - c < w)) for w in POOL_WINDOWS]).astype(np.float32)
    halo = np.stack([((r - ch >= 0) & (r - ch < w)) for w in POOL_WINDOWS]).astype(np.float32)
    return jnp.asarray(main, BF16), jnp.asarray(halo, BF16)


def _pool_branch(ub, j, halo_ref, bmain_ref, bhalo_ref, wg_ref, scale_ref, o_ref, filler):
    tile = ub.shape[0]
    s0 = j * tile
    slot = lax.rem(j, 2)
    blocks = [(sb * POOL_SUB, g) for sb in range(tile // POOL_SUB) for g in range(len(POOL_WINDOWS))]
    wsum = {}
    for r0, g in blocks:
        cols = slice(g * POOL_GROUP_DIM, (g + 1) * POOL_GROUP_DIM)
        prev = halo_ref[slot] if r0 == 0 else ub[r0 - POOL_HALO:r0]
        wsum[r0, g] = _dot(bmain_ref[g], ub[r0:r0 + POOL_SUB, cols]) + _dot(bhalo_ref[g], prev[:, cols])
    filler()
    for r0, g in blocks:
        cols = slice(g * POOL_GROUP_DIM, (g + 1) * POOL_GROUP_DIM)
        pos = s0 + r0 + lax.broadcasted_iota(I32, (POOL_SUB, POOL_GROUP_DIM), 0)
        cnt = jnp.minimum(pos + 1, POOL_WINDOWS[g]).astype(F32)
        pooled = wsum[r0, g] / cnt - ub[r0:r0 + POOL_SUB, cols].astype(F32)
        mixed = _dot(pooled.astype(BF16), wg_ref[g]) * scale_ref[:, cols]
        o_ref[r0:r0 + POOL_SUB, cols] = mixed.astype(BF16)
    halo_ref[1 - slot] = ub[tile - POOL_HALO:tile]


def _ret_consts():
    h = np.arange(RET_HEADS, dtype=np.float64)
    log_gamma = np.log1p(-np.exp2(-5.0 - h))
    pos = np.arange(RET_CHUNK, dtype=np.float64)
    diff = pos[:, None] - pos[None, :]
    kscale = RET_QK_DIM ** -0.5
    dmask = kscale * np.where(diff >= 0, np.exp(log_gamma[:, None, None] * np.maximum(diff, 0.0)), 0.0)
    qdec = np.exp(log_gamma[:, None] * (pos + 1.0)[None, :])
    kdec = kscale * np.exp(log_gamma[:, None] * (RET_CHUNK - 1.0 - pos)[None, :])
    cdec = np.exp(log_gamma * RET_CHUNK)
    lanes = lambda a: np.broadcast_to(a[:, :, None], (RET_HEADS, RET_CHUNK, RET_QK_DIM))
    kdec_t = np.broadcast_to(kdec[:, None, :], (RET_HEADS, RET_QK_DIM, RET_CHUNK))
    return (jnp.asarray(dmask, F32), jnp.asarray(lanes(qdec), F32), jnp.asarray(kdec_t, F32),
            tuple(float(v) for v in cdec))


def _retention_branch(q, k_t, v, silu_g, cos_ref, sin_ref, cos_t_ref, sin_t_ref, dmask_ref, qdec_ref, kdec_ref,
                      gnw_ref, state_ref, rq_ref, rqd_ref, rkt_ref, rkdt_ref, o_ref, cdec, fillers):
    tile = q.shape[0]
    n_chunks = tile // RET_CHUNK
    half = RET_QK_DIM // 2
    cos = cos_ref[...]
    sin = sin_ref[...]
    cos_t = cos_t_ref[...]
    sin_t = sin_t_ref[...]
    for h in range(RET_HEADS):
        qk = slice(h * RET_QK_DIM, (h + 1) * RET_QK_DIM)
        qh = q[:, qk]
        qr = qh * cos + pltpu.roll(qh, half, 1) * sin
        rq_ref[:, qk] = qr.astype(BF16)
        rqd_ref[:, qk] = (qr * jnp.concatenate([qdec_ref[h]] * n_chunks, axis=0)).astype(BF16)
        k1 = k_t[h * RET_QK_DIM:h * RET_QK_DIM + half]
        k2 = k_t[h * RET_QK_DIM + half:(h + 1) * RET_QK_DIM]
        kr_t = jnp.concatenate([k1 * cos_t - k2 * sin_t, k2 * cos_t + k1 * sin_t], axis=0)
        rkt_ref[qk, :] = kr_t.astype(BF16)
        rkdt_ref[qk, :] = (kr_t * jnp.concatenate([kdec_ref[h]] * n_chunks, axis=1)).astype(BF16)

    for c in range(n_chunks):
        rows = slice(c * RET_CHUNK, (c + 1) * RET_CHUNK)
        heads = [slice(h * RET_QK_DIM, (h + 1) * RET_QK_DIM) for h in range(RET_HEADS)]
        raw = [_dot(rq_ref[rows, qk], rkt_ref[qk, rows]) for qk in heads]
        fillers[c]()
        for h in range(RET_HEADS):
            qk = heads[h]
            v_cols = slice(h * RET_V_DIM, (h + 1) * RET_V_DIM)
            vh = v[rows, v_cols]
            scores = raw[h] * dmask_ref[h]
            st = state_ref[h]
            lhs = jnp.concatenate([scores.astype(BF16), rqd_ref[rows, qk]], axis=1)
            y = _dot(lhs, jnp.concatenate([vh, st.astype(BF16)], axis=0))
            state_ref[h] = cdec[h] * st + _dot(rkdt_ref[qk, rows], vh)
            mu = jnp.mean(y, axis=-1, keepdims=True)
            yc = y - mu
            var = jnp.mean(yc * yc, axis=-1, keepdims=True)
            yn = yc * lax.rsqrt(var + LN_EPS) * gnw_ref[:, v_cols]
            o_ref[rows, v_cols] = (silu_g[h][rows] * yn).astype(BF16)


def _memkv_kernel(m_ref, w_ref, o_ref):
    o_ref[...] = _dot(m_ref[...].astype(BF16), w_ref[...]).astype(BF16)


def _mem_kv(memf, w_b):
    M, D = memf.shape
    N = w_b.shape[1]
    return pl.pallas_call(
        _memkv_kernel,
        grid=(1,),
        in_specs=[pl.BlockSpec((M, D), lambda i: (0, 0)), pl.BlockSpec((D, N), lambda i: (0, 0))],
        out_specs=pl.BlockSpec((M, N), lambda i: (0, 0)),
        out_shape=jax.ShapeDtypeStruct((M, N), BF16),
        name="mem_kv",
    )(memf, w_b)


def _cross_attention_branch(xq, k_ref, v_ref, o_ref):
    scale = XA_HEAD_DIM ** -0.5
    for h in range(XA_HEADS):
        cols = slice(h * XA_HEAD_DIM, (h + 1) * XA_HEAD_DIM)
        s = _dot_nt(xq[:, cols], k_ref[:, cols]) * scale
        m = jnp.max(s, axis=-1, keepdims=True)
        p = jnp.exp(s - m)
        l = jnp.sum(p, axis=-1, keepdims=True)
        o = _dot(p.astype(BF16), v_ref[:, cols]) / l
        o_ref[:, cols] = o.astype(BF16)


ROUTER_ROWS = 128
WTOK_LANES = 128
EXP_ROW0 = 8


def _route(logits_t):
    gl = logits_t[0:8]
    gmax = jnp.max(gl, axis=0, keepdims=True)
    p_grp = 1.0 / jnp.sum(jnp.exp(gl - gmax), axis=0, keepdims=True)
    idx8 = lax.broadcasted_iota(jnp.int32, gl.shape, 0)
    gsel = jnp.min(jnp.where(gl == gmax, idx8, 8), axis=0, keepdims=True)
    cl = jnp.zeros_like(gl)
    for g in range(N_GROUPS):
        r0 = EXP_ROW0 + g * EXPERTS_PER_GROUP
        cl = cl + jnp.where(gsel == g, logits_t[r0:r0 + EXPERTS_PER_GROUP], 0.0)
    v1 = jnp.max(cl, axis=0, keepdims=True)
    i1 = jnp.min(jnp.where(cl == v1, idx8, 8), axis=0, keepdims=True)
    cl2 = jnp.where(idx8 == i1, -jnp.inf, cl)
    v2 = jnp.max(cl2, axis=0, keepdims=True)
    i2 = jnp.min(jnp.where(cl2 == v2, idx8, 8), axis=0, keepdims=True)
    e21 = jnp.exp(v2 - v1)
    w1 = p_grp / (1.0 + e21)
    w2 = p_grp * e21 / (1.0 + e21)
    return gsel * EXPERTS_PER_GROUP + i1, gsel * EXPERTS_PER_GROUP + i2, w1, w2


def _pack_halves(v):
    half = v.shape[1] // 2
    lo = lax.bitcast_convert_type(v[:, :half].astype(BF16).astype(F32), U32)
    hi = lax.bitcast_convert_type(v[:, half:].astype(BF16).astype(F32), U32)
    return lax.bitcast_convert_type(lax.shift_right_logical(lo, U32(16)) | hi, I32)


def _unpack_halves(w):
    u = lax.bitcast_convert_type(w, U32)
    lo = lax.bitcast_convert_type(lax.shift_left(u, U32(16)), F32)
    hi = lax.bitcast_convert_type(u & U32(0xFFFF0000), F32)
    return lo, hi


def _mixer_kernel(x_ref, cos_ref, sin_ref, cos_t_ref, sin_t_ref, km_ref, vm_ref, win_ref, wgrp_ref,
                  pscale_ref, bmain_ref, bhalo_ref, dmask_ref, qdec_ref, kdec_ref, gnw_ref, wp_ref, wr_ref,
                  wa_ref, wo_ref, lnw_ref, lnb_ref, rw_ref, rb_ref, xp_ref, eid_ref, wtok_ref,
                  state_ref, halo_ref, ypool_ref, yret_ref, yxa_ref, rq_ref, rqd_ref, rkt_ref, rkdt_ref, wkt_ref,
                  *, tile, cdec):
    j = pl.program_id(1)

    @pl.when(jnp.logical_and(pl.program_id(0) == 0, j == 0))
    def _():
        wk = win_ref[:, COL_K:COL_K + RET_HEADS * RET_QK_DIM].astype(F32)
        wkt_ref[...] = jnp.transpose(wk).astype(BF16)

    @pl.when(j == 0)
    def _():
        state_ref[...] = jnp.zeros_like(state_ref)
        halo_ref[...] = jnp.zeros_like(halo_ref)

    x = x_ref[...]
    xb = x.astype(BF16)

    def proj(col, width):
        return _dot(xb, win_ref[:, col:col + width])

    part = {}
    strips = [slice(c, c + STRIP) for c in range(0, D_MODEL, STRIP)]

    def gate(branch, cols):
        return _sigmoid(proj(COL_GATES + branch * D_MODEL + cols.start, STRIP))

    def pool_part():
        def pool_gates():
            part["pool_gate"] = [gate(0, c) for c in strips]

        _pool_branch(proj(COL_POOL, POOL_WIDTH).astype(BF16), j, halo_ref, bmain_ref, bhalo_ref, wgrp_ref,
                     pscale_ref, ypool_ref, pool_gates)
        part["pool"] = [part["pool_gate"][i] * _dot(ypool_ref[...], wp_ref[:, c]) for i, c in enumerate(strips)]

    def xa_part():
        _cross_attention_branch(proj(COL_XAQ, XA_WIDTH).astype(BF16), km_ref, vm_ref, yxa_ref)
        part["xa"] = [gate(2, c) * _dot(yxa_ref[...], wa_ref[:, c]) for c in strips]

    def ret_gate_part():
        part["ret_gate"] = [gate(1, c) for c in strips]

    silu_g = []
    for h in range(RET_HEADS):
        gh = proj(COL_G + h * RET_V_DIM, RET_V_DIM)
        silu_g.append(gh * _sigmoid(gh))
    fillers = [pool_part, xa_part, ret_gate_part] + [lambda: None] * (tile // RET_CHUNK - 3)
    _retention_branch(proj(COL_Q, RET_HEADS * RET_QK_DIM), _dot_nt(wkt_ref[...], xb),
                      proj(COL_V, RET_HEADS * RET_V_DIM).astype(BF16), silu_g, cos_ref, sin_ref, cos_t_ref,
                      sin_t_ref, dmask_ref, qdec_ref, kdec_ref, gnw_ref, state_ref, rq_ref, rqd_ref, rkt_ref,
                      rkdt_ref, yret_ref, cdec, fillers)
    merged = jnp.concatenate(
        [(part["pool"][i] + part["ret_gate"][i] * _dot(yret_ref[...], wr_ref[:, c]) + part["xa"][i]).astype(BF16)
         for i, c in enumerate(strips)], axis=1)
    h = jnp.concatenate([ALPHA * x[:, c] + _dot(merged, wo_ref[:, c]) for c in strips], axis=1)
    x1 = jnp.concatenate([_layer_norm(h[r:r + LN_ROWS], lnw_ref[...], lnb_ref[...])
                          for r in range(0, tile, LN_ROWS)], axis=0)
    xp_ref[...] = _pack_halves(x1)
    x1_hi = x1.astype(BF16)
    x1_lo = (x1 - x1_hi.astype(F32)).astype(BF16)
    p4 = _dot(jnp.concatenate([x1_hi, x1_lo], axis=0), rw_ref[...])
    logits = (p4[:tile, :ROUTER_ROWS] + p4[:tile, ROUTER_ROWS:]) + (p4[tile:, :ROUTER_ROWS] + p4[tile:, ROUTER_ROWS:])
    e0, e1, w0, w1 = _route(jnp.transpose(logits) + rb_ref[...])
    eid_ref[...] = jnp.concatenate([e0, e1, jnp.zeros((8 - TOP_K, tile), I32)], axis=0)
    w_t = jnp.concatenate([w0, w1, jnp.zeros((WTOK_LANES - TOP_K, tile), F32)], axis=0)
    wtok_ref[...] = jnp.transpose(w_t)


def _mixer(xf, rope, kv, win, wgrp, pscale, gnw, wp, wr, wa, wo, lnw, lnb, rw, rb, batch, seq, mem_len,
           tile=512):
    T = xf.shape[0]
    nj = seq // tile
    cos, sin, cos_t, sin_t = rope
    bmain, bhalo = _pool_bands()
    dmask, qdec, kdec, cdec = _ret_consts()
    resident = lambda a: pl.BlockSpec(a.shape, lambda b, j: (0,) * a.ndim, pipeline_mode=pl.Buffered(1))
    rowblk = lambda w: pl.BlockSpec((tile, w), lambda b, j: (b * nj + j, 0))
    colblk = lambda r: pl.BlockSpec((r, tile), lambda b, j: (0, b * nj + j))
    consts = (win, wgrp, pscale, bmain, bhalo, dmask, qdec, kdec, gnw, wp, wr, wa, wo, lnw, lnb, rw, rb)
    return pl.pallas_call(
        functools.partial(_mixer_kernel, tile=tile, cdec=cdec),
        grid=(batch, nj),
        in_specs=[rowblk(D_MODEL), rowblk(RET_QK_DIM), rowblk(RET_QK_DIM),
                  colblk(RET_QK_DIM // 2), colblk(RET_QK_DIM // 2),
                  pl.BlockSpec((mem_len, XA_WIDTH), lambda b, j: (b, 0)),
                  pl.BlockSpec((mem_len, XA_WIDTH), lambda b, j: (b, 1))] + [resident(a) for a in consts],
        out_specs=[rowblk(PACK_W), colblk(8), rowblk(WTOK_LANES)],
        out_shape=[jax.ShapeDtypeStruct((T, PACK_W), I32), jax.ShapeDtypeStruct((8, T), I32),
                   jax.ShapeDtypeStruct((T, WTOK_LANES), F32)],
        scratch_shapes=[pltpu.VMEM((RET_HEADS, RET_QK_DIM, RET_V_DIM), F32),
                        pltpu.VMEM((2, POOL_HALO, POOL_WIDTH), BF16),
                        pltpu.VMEM((tile, POOL_WIDTH), BF16),
                        pltpu.VMEM((tile, RET_HEADS * RET_V_DIM), BF16),
                        pltpu.VMEM((tile, XA_WIDTH), BF16),
                        pltpu.VMEM((tile, RET_HEADS * RET_QK_DIM), BF16),
                        pltpu.VMEM((tile, RET_HEADS * RET_QK_DIM), BF16),
                        pltpu.VMEM((RET_HEADS * RET_QK_DIM, tile), BF16),
                        pltpu.VMEM((RET_HEADS * RET_QK_DIM, tile), BF16),
                        pltpu.VMEM((RET_HEADS * RET_QK_DIM, D_MODEL), BF16)],
        compiler_params=pltpu.CompilerParams(dimension_semantics=("arbitrary", "arbitrary"),
                                             vmem_limit_bytes=VMEM_LIMIT),
        name="mixer",
    )(xf, cos, sin, cos_t, sin_t, kv, kv, *consts)


META_LANES = 128


def _positions_kernel(eid_ref, tri_ref, low_ref, pos_ref, first_tile_ref, n_tiles_ref, *, n_tok):
    n_chunks = n_tok // RANK_CHUNK
    erow = lax.broadcasted_iota(I32, (N_EXPERTS, RANK_CHUNK), 0)

    def onehot(c):
        sl = slice(c * RANK_CHUNK, (c + 1) * RANK_CHUNK)
        m0 = eid_ref[0:1, sl] == erow
        m1 = eid_ref[1:2, sl] == erow
        return m0, m1, jnp.where(m0, 1.0, 0.0) + jnp.where(m1, 1.0, 0.0)

    counts = jnp.zeros((N_EXPERTS, 1), F32)
    for c in range(n_chunks):
        counts = counts + jnp.sum(onehot(c)[2], axis=1, keepdims=True)
    ptiles = jnp.floor((counts + (MOE_TM - 1)) * (1.0 / MOE_TM))
    ptiles_b = jnp.broadcast_to(ptiles, (N_EXPERTS, 128)).astype(BF16)
    start = _dot(low_ref[...], ptiles_b)[:, 0:1] * MOE_TM

    pos_ref[...] = jnp.zeros_like(pos_ref)
    carry = start - 1.0
    for c in range(n_chunks):
        sl = slice(c * RANK_CHUNK, (c + 1) * RANK_CHUNK)
        m0, m1, oh = onehot(c)
        rank = _dot(oh.astype(BF16), tri_ref[...]) + carry
        pos_ref[0:1, sl] = jnp.sum(jnp.where(m0, rank, 0.0), axis=0, keepdims=True).astype(I32)
        pos_ref[1:2, sl] = jnp.sum(jnp.where(m1, rank, 0.0), axis=0, keepdims=True).astype(I32)
        carry = carry + jnp.sum(oh, axis=1, keepdims=True)

    first_tile_ref[...] = jnp.broadcast_to(start * (1.0 / MOE_TM), first_tile_ref.shape).astype(I32)
    n_tiles_ref[...] = jnp.broadcast_to(ptiles, n_tiles_ref.shape).astype(I32)


def _positions(eid):
    T = eid.shape[1]
    r = np.arange(RANK_CHUNK)
    tri = jnp.asarray(r[:, None] <= r[None, :], BF16)
    e = np.arange(N_EXPERTS)
    low = jnp.asarray(e[None, :] < e[:, None], BF16)
    full = lambda a: pl.BlockSpec(a.shape, lambda i: (0,) * a.ndim)
    return pl.pallas_call(
        functools.partial(_positions_kernel, n_tok=T),
        grid=(1,),
        in_specs=[full(eid), full(tri), full(low)],
        out_specs=[pl.BlockSpec((8, T), lambda i: (0, 0)),
                   pl.BlockSpec((N_EXPERTS, META_LANES), lambda i: (0, 0)),
                   pl.BlockSpec((N_EXPERTS, META_LANES), lambda i: (0, 0))],
        out_shape=[jax.ShapeDtypeStruct((8, T), I32), jax.ShapeDtypeStruct((N_EXPERTS, META_LANES), I32),
                   jax.ShapeDtypeStruct((N_EXPERTS, META_LANES), I32)],
        name="route_positions",
    )(eid, tri, low)


def _sc_workers():
    info = plsc.get_sparse_core_info()
    return info.num_cores, info.num_cores * info.num_subcores


def _sc_scatter_rows(xp, pos2d, n_out):
    T, W = xp.shape
    n_cores, n_workers = _sc_workers()
    cpw = T // SC_CHUNK // n_workers
    mesh = plsc.VectorSubcoreMesh(core_axis_name="c", subcore_axis_name="s")

    @functools.partial(
        pl.kernel, mesh=mesh, out_type=jax.ShapeDtypeStruct((n_out, W), I32),
        scratch_types=[pltpu.VMEM((TOP_K * cpw, SC_CHUNK), I32), pltpu.VMEM((2, SC_CHUNK, W), I32),
                       pltpu.SemaphoreType.DMA((2,)), pltpu.SemaphoreType.DMA((2,))],
        name="sc_scatter_rows")
    def k(x_hbm, pos_hbm, out_hbm, idx_v, rows_v, rd_sem, wr_sem):
        wid = lax.axis_index("s") * n_cores + lax.axis_index("c")
        for s in range(TOP_K):
            pltpu.sync_copy(pos_hbm.at[pl.ds(s * (T // SC_CHUNK) + wid * cpw, cpw)],
                            idx_v.at[pl.ds(s * cpw, cpw)])

        def read(j):
            return pltpu.make_async_copy(x_hbm.at[pl.ds((wid * cpw + j) * SC_CHUNK, SC_CHUNK)],
                                         rows_v.at[j % 2], rd_sem.at[j % 2])

        def write(j, s):
            return pltpu.make_async_copy(rows_v.at[j % 2], out_hbm.at[idx_v.at[s * cpw + j]], wr_sem.at[j % 2])

        _sc_two_buffer_stream(cpw, read, lambda j: [write(j, s) for s in range(TOP_K)])

    return k(xp, pos2d)


def _sc_two_buffer_stream(n, read, writes):
    read(0).start()
    for j in range(n):
        read(j).wait()
        if j + 1 < n:
            if j >= 1:
                for w in writes(j - 1):
                    w.wait()
            read(j + 1).start()
        for w in writes(j):
            w.start()
    for j in range(max(n - 2, 0), n):
        for w in writes(j):
            w.wait()


def _sc_gather_rows(y, idx2d):
    W = y.shape[1]
    n = idx2d.shape[0] * SC_CHUNK
    n_cores, n_workers = _sc_workers()
    cpw = n // SC_CHUNK // n_workers
    mesh = plsc.VectorSubcoreMesh(core_axis_name="c", subcore_axis_name="s")

    @functools.partial(
        pl.kernel, mesh=mesh, out_type=jax.ShapeDtypeStruct((n, W), I32),
        scratch_types=[pltpu.VMEM((cpw, SC_CHUNK), I32), pltpu.VMEM((2, SC_CHUNK, W), I32),
                       pltpu.SemaphoreType.DMA((2,)), pltpu.SemaphoreType.DMA((2,))],
        name="sc_gather_rows")
    def k(y_hbm, idx_hbm, out_hbm, idx_v, rows_v, rd_sem, wr_sem):
        wid = lax.axis_index("s") * n_cores + lax.axis_index("c")
        pltpu.sync_copy(idx_hbm.at[pl.ds(wid * cpw, cpw)], idx_v)

        def read(j):
            return pltpu.make_async_copy(y_hbm.at[idx_v.at[j]], rows_v.at[j % 2], rd_sem.at[j % 2])

        def write(j):
            return pltpu.make_async_copy(rows_v.at[j % 2], out_hbm.at[pl.ds((wid * cpw + j) * SC_CHUNK, SC_CHUNK)],
                                         wr_sem.at[j % 2])

        _sc_two_buffer_stream(cpw, read, lambda j: [write(j)])

    return k(y, idx2d)


def _routed_kernel(first_ref, count_ref, xs_hbm, wg_ref, wu_ref, wd_ref, ys_hbm, xbuf, ybuf, in_sem, out_sem):
    e = pl.program_id(0)
    last = pl.num_programs(0) - 1
    total = first_ref[last] + count_ref[last]

    def in_copy(g):
        slot = lax.rem(g, RING)
        return pltpu.make_async_copy(xs_hbm.at[pl.ds(g * MOE_TM, MOE_TM)], xbuf.at[slot], in_sem.at[slot])

    def out_copy(g):
        slot = lax.rem(g, RING)
        return pltpu.make_async_copy(ybuf.at[slot], ys_hbm.at[pl.ds(g * MOE_TM, MOE_TM)], out_sem.at[slot])

    @pl.when(e == 0)
    def _():
        for g0 in range(RING - 1):
            @pl.when(g0 < total)
            def _():
                in_copy(g0).start()

    wg = wg_ref[0].astype(BF16)
    wu = wu_ref[0].astype(BF16)
    wd = wd_ref[0].astype(BF16)

    def tile_step(i, carry):
        g = first_ref[e] + i
        slot = lax.rem(g, RING)
        in_copy(g).wait()

        @pl.when(g + RING - 1 < total)
        def _():
            in_copy(g + RING - 1).start()

        @pl.when(g >= RING)
        def _():
            out_copy(g - RING).wait()

        lo, hi = _unpack_halves(xbuf[slot])
        lo = lo.astype(BF16)
        hi = hi.astype(BF16)
        a = _dot(lo, wg[:PACK_W]) + _dot(hi, wg[PACK_W:])
        b = _dot(lo, wu[:PACK_W]) + _dot(hi, wu[PACK_W:])
        act = (a * _sigmoid(a) * b).astype(BF16)
        ybuf[slot] = _pack_halves(_dot(act, wd))
        out_copy(g).start()
        return carry

    lax.fori_loop(0, count_ref[e], tile_step, 0)

    @pl.when(e == last)
    def _():
        for back in range(RING, 0, -1):
            @pl.when(total >= back)
            def _():
                out_copy(total - back).wait()


def _routed_mlp(first_tile, n_tiles, xs, wg, wu, wd):
    R = xs.shape[0]
    any_space = pl.BlockSpec(memory_space=pl.ANY)
    return pl.pallas_call(
        _routed_kernel,
        grid_spec=pltpu.PrefetchScalarGridSpec(
            num_scalar_prefetch=2,
            grid=(N_EXPERTS,),
            in_specs=[any_space,
                      pl.BlockSpec((1, D_MODEL, D_EXPERT), lambda e, ft, nt: (e, 0, 0)),
                      pl.BlockSpec((1, D_MODEL, D_EXPERT), lambda e, ft, nt: (e, 0, 0)),
                      pl.BlockSpec((1, D_EXPERT, D_MODEL), lambda e, ft, nt: (e, 0, 0))],
            out_specs=any_space,
            scratch_shapes=[pltpu.VMEM((RING, MOE_TM, PACK_W), I32), pltpu.VMEM((RING, MOE_TM, PACK_W), I32),
                            pltpu.SemaphoreType.DMA((RING,)), pltpu.SemaphoreType.DMA((RING,))]),
        out_shape=jax.ShapeDtypeStruct((R, PACK_W), I32),
        compiler_params=pltpu.CompilerParams(dimension_semantics=("arbitrary",)),
        name="routed_mlp",
    )(first_tile, n_tiles, xs, wg, wu, wd)


def _combine_kernel(xp_ref, y0_ref, y1_ref, wtok_ref, lnw_ref, lnb_ref, *out_refs):
    o_ref = out_refs[-1]
    w0 = wtok_ref[:, 0:1]
    w1 = wtok_ref[:, 1:2]
    xlo, xhi = _unpack_halves(xp_ref[...])
    y0lo, y0hi = _unpack_halves(y0_ref[...])
    y1lo, y1hi = _unpack_halves(y1_ref[...])
    h = jnp.concatenate([ALPHA * xlo + (w0 * y0lo + w1 * y1lo), ALPHA * xhi + (w0 * y0hi + w1 * y1hi)], axis=1)
    o_ref[...] = _layer_norm(h, lnw_ref[...], lnb_ref[...])


def _combine_ln2(xp, yg_parts, wtok, lnw, lnb, tile=1024):
    T = xp.shape[0]
    n_parts = len(yg_parts)
    nt = T // tile // n_parts
    full = lambda a: pl.BlockSpec(a.shape, lambda i: (0,) * a.ndim)
    out = None
    for p, yg in enumerate(yg_parts):
        rows = lambda w, p=p: pl.BlockSpec((tile, w), lambda i: (i + p * nt, 0))
        in_specs = [rows(PACK_W),
                    pl.BlockSpec((tile, PACK_W), lambda i: (i, 0)),
                    pl.BlockSpec((tile, PACK_W), lambda i: (i + nt, 0)),
                    rows(WTOK_LANES), full(lnw), full(lnb)]
        args = [xp, yg, yg, wtok, lnw, lnb]
        aliases = {}
        if out is not None:
            in_specs.append(pl.BlockSpec(memory_space=pl.ANY))
            args.append(out)
            aliases = {len(args) - 1: 0}
        out = pl.pallas_call(
            _combine_kernel,
            grid=(nt,),
            in_specs=in_specs,
            out_specs=rows(D_MODEL),
            out_shape=jax.ShapeDtypeStruct((T, D_MODEL), F32),
            input_output_aliases=aliases,
            compiler_params=pltpu.CompilerParams(dimension_semantics=("arbitrary",), vmem_limit_bytes=VMEM_LIMIT),
            name="combine_ln2",
        )(*args)
    return out


def _router_params(w_grp, b_grp, w_exp, b_exp):
    rw = jnp.zeros((D_MODEL, ROUTER_ROWS), F32)
    rw = rw.at[:, 0:N_GROUPS].set(w_grp).at[:, EXP_ROW0:EXP_ROW0 + N_EXPERTS].set(w_exp)
    rb = jnp.zeros((ROUTER_ROWS,), F32).at[N_GROUPS:8].set(NEG_BIG)
    rb = rb.at[0:N_GROUPS].set(b_grp).at[EXP_ROW0:EXP_ROW0 + N_EXPERTS].set(b_exp)
    rw_hi = rw.astype(BF16)
    rw_lo = (rw - rw_hi.astype(F32)).astype(BF16)
    return jnp.concatenate([rw_hi, rw_lo], axis=1), rb[:, None]


def kernel(x, mem, positions, w_in, w_pool_grp, pool_scale, ret_gn_w, w_mem_kv, w_br_pool, w_br_ret, w_br_xa,
           w_out, ln1_w, ln1_b, w_grp_router, b_grp_router, w_exp_router, b_exp_router, w_exp_gate, w_exp_up,
           w_exp_down, ln2_w, ln2_b):
    B, S, D = x.shape
    assert D == D_MODEL and w_in.shape[0] == DEPTH and S % 512 == 0
    T = B * S
    M = mem.shape[1]
    l = 0
    xf = x.reshape(T, D)

    rope = _rope_table(positions.reshape(1, T))
    kv = _mem_kv(mem.reshape(B * M, D), w_mem_kv[l].astype(BF16))
    rw, rb = _router_params(w_grp_router[l], b_grp_router[l], w_exp_router[l], b_exp_router[l])
    xp, eid, wtok = _mixer(xf, rope, kv, w_in[l].astype(BF16),
                           w_pool_grp[l].astype(BF16), pool_scale[l][None, :], ret_gn_w[l].reshape(1, -1),
                           w_br_pool[l].astype(BF16), w_br_ret[l].astype(BF16), w_br_xa[l].astype(BF16),
                           w_out[l].astype(BF16), ln1_w[l][None, :], ln1_b[l][None, :], rw, rb, B, S, M)

    pos, first_tile, n_tiles = _positions(eid)
    pos2d = pos[0:TOP_K].reshape(TOP_K * T // SC_CHUNK, SC_CHUNK)
    max_tiles = (TOP_K * T + N_EXPERTS * (MOE_TM - 1)) // MOE_TM
    xs = _sc_scatter_rows(xp, pos2d, max_tiles * MOE_TM)
    ys = _routed_mlp(first_tile[:, 0], n_tiles[:, 0], xs,
                     w_exp_gate[l].reshape(N_EXPERTS, D_MODEL, D_EXPERT),
                     w_exp_up[l].reshape(N_EXPERTS, D_MODEL, D_EXPERT),
                     w_exp_down[l].reshape(N_EXPERTS, D_EXPERT, D_MODEL))
    rng = T // COMBINE_PARTS
    yg_parts = [_sc_gather_rows(ys, pos[0:TOP_K, p * rng:(p + 1) * rng].reshape(TOP_K * rng // SC_CHUNK, SC_CHUNK))
                for p in range(COMBINE_PARTS)]
    out = _combine_ln2(xp, yg_parts, wtok, ln2_w[l][None, :], ln2_b[l][None, :])
    return out.reshape(B, S, D)
```

```python
import functools

import numpy as np
import jax
import jax.numpy as jnp
from jax import lax
from jax.experimental import pallas as pl
from jax.experimental.pallas import tpu as pltpu
from jax.experimental.pallas import tpu_sc as plsc

F32 = jnp.float32
BF16 = jnp.bfloat16
I32 = jnp.int32
U32 = jnp.uint32

D_MODEL = 1024
POOL_WINDOWS = (2, 4, 8, 16)
POOL_GROUP_DIM = 128
POOL_WIDTH = 512
POOL_HALO = 16
RET_HEADS = 4
RET_QK_DIM = 128
RET_V_DIM = 256
RET_CHUNK = 128
ROPE_BASE = 10000.0
XA_HEADS = 4
XA_HEAD_DIM = 128
XA_WIDTH = 512
N_GROUPS = 4
EXPERTS_PER_GROUP = 8
N_EXPERTS = N_GROUPS * EXPERTS_PER_GROUP
D_EXPERT = 256
LN_EPS = 1e-5
DEPTH = 1
ALPHA = (2.0 * DEPTH) ** 0.25
NEG_BIG = -1e30

COL_POOL, COL_Q, COL_K, COL_V, COL_G, COL_XAQ, COL_GATES = 0, 512, 1024, 1536, 2560, 3584, 4096

VMEM_LIMIT = 56 * 1024 * 1024

TOP_K = 2
PACK_W = D_MODEL // 2
MOE_TM = 256
SC_CHUNK = 64
RANK_CHUNK = 512
COMBINE_PARTS = 2
RING = 8
STRIP = 256
LN_ROWS = 32


def _dot(a, b):
    return jnp.dot(a, b, preferred_element_type=F32)


def _dot_nt(a, b, precision=None):
    return lax.dot_general(a, b, (((1,), (1,)), ((), ())), preferred_element_type=F32, precision=precision)


def _sigmoid(z):
    return 1.0 / (1.0 + jnp.exp2(z * (-1.0 / np.log(2.0))))


def _layer_norm(h, w, b):
    mu = jnp.mean(h, axis=-1, keepdims=True)
    hc = h - mu
    var = jnp.mean(hc * hc, axis=-1, keepdims=True)
    return hc * lax.rsqrt(var + LN_EPS) * w + b


def _rope_kernel(pos_ref, freq_ref, cos_ref, sin_ref, cos_t_ref, sin_t_ref):
    ang = freq_ref[...] * pos_ref[...].astype(F32)
    cos_t = jnp.cos(ang)
    sin_t = jnp.sin(ang)
    cos_t_ref[...] = cos_t
    sin_t_ref[...] = sin_t
    cos_ref[...] = jnp.transpose(jnp.concatenate([cos_t, cos_t], axis=0))
    sin_ref[...] = jnp.transpose(jnp.concatenate([-sin_t, sin_t], axis=0))


def _rope_table(pos_row, tile=2048):
    T = pos_row.shape[1]
    half = RET_QK_DIM // 2
    inv_freq = (ROPE_BASE ** (-np.arange(half, dtype=np.float64) / half)).astype(np.float32)
    freq = jnp.asarray(inv_freq[:, None])
    out = pl.BlockSpec((tile, RET_QK_DIM), lambda i: (i, 0))
    out_t = pl.BlockSpec((half, tile), lambda i: (0, i))
    return pl.pallas_call(
        _rope_kernel,
        grid=(T // tile,),
        in_specs=[pl.BlockSpec((1, tile), lambda i: (0, i)), pl.BlockSpec((half, 1), lambda i: (0, 0))],
        out_specs=[out, out, out_t, out_t],
        out_shape=[jax.ShapeDtypeStruct((T, RET_QK_DIM), F32)] * 2 + [jax.ShapeDtypeStruct((half, T), F32)] * 2,
        name="rope_table",
    )(pos_row, freq)


POOL_SUB = 256


def _pool_bands():
    r = np.arange(POOL_SUB)[:, None]
    c = np.arange(POOL_SUB)[None, :]
    ch = np.arange(POOL_HALO)[None, :] - POOL_HALO
    main = np.stack([((r - c >= 0) & (r - c < w)) for w in POOL_WINDOWS]).astype(np.float32)
    halo = np.stack([((r - ch >= 0) & (r - ch < w)) for w in POOL_WINDOWS]).astype(np.float32)
    return jnp.asarray(main, BF16), jnp.asarray(halo, BF16)


def _pool_branch(ub, j, halo_ref, bmain_ref, bhalo_ref, wg_ref, scale_ref, o_ref, filler):
    tile = ub.shape[0]
    s0 = j * tile
    slot = lax.rem(j, 2)
    blocks = [(sb * POOL_SUB, g) for sb in range(tile // POOL_SUB) for g in range(len(POOL_WINDOWS))]
    wsum = {}
    for r0, g in blocks:
        cols = slice(g * POOL_GROUP_DIM, (g + 1) * POOL_GROUP_DIM)
        prev = halo_ref[slot] if r0 == 0 else ub[r0 - POOL_HALO:r0]
        wsum[r0, g] = _dot(bmain_ref[g], ub[r0:r0 + POOL_SUB, cols]) + _dot(bhalo_ref[g], prev[:, cols])
    filler()
    for r0, g in blocks:
        cols = slice(g * POOL_GROUP_DIM, (g + 1) * POOL_GROUP_DIM)
        pos = s0 + r0 + lax.broadcasted_iota(I32, (POOL_SUB, POOL_GROUP_DIM), 0)
        cnt = jnp.minimum(pos + 1, POOL_WINDOWS[g]).astype(F32)
        pooled = wsum[r0, g] / cnt - ub[r0:r0 + POOL_SUB, cols].astype(F32)
        mixed = _dot(pooled.astype(BF16), wg_ref[g]) * scale_ref[:, cols]
        o_ref[r0:r0 + POOL_SUB, cols] = mixed.astype(BF16)
    halo_ref[1 - slot] = ub[tile - POOL_HALO:tile]


def _ret_consts():
    h = np.arange(RET_HEADS, dtype=np.float64)
    log_gamma = np.log1p(-np.exp2(-5.0 - h))
    pos = np.arange(RET_CHUNK, dtype=np.float64)
    diff = pos[:, None] - pos[None, :]
    kscale = RET_QK_DIM ** -0.5
    dmask = kscale * np.where(diff >= 0, np.exp(log_gamma[:, None, None] * np.maximum(diff, 0.0)), 0.0)
    qdec = np.exp(log_gamma[:, None] * (pos + 1.0)[None, :])
    kdec = kscale * np.exp(log_gamma[:, None] * (RET_CHUNK - 1.0 - pos)[None, :])
    cdec = np.exp(log_gamma * RET_CHUNK)
    lanes = lambda a: np.broadcast_to(a[:, :, None], (RET_HEADS, RET_CHUNK, RET_QK_DIM))
    kdec_t = np.broadcast_to(kdec[:, None, :], (RET_HEADS, RET_QK_DIM, RET_CHUNK))
    return (jnp.asarray(dmask, F32), jnp.asarray(lanes(qdec), F32), jnp.asarray(kdec_t, F32),
            tuple(float(v) for v in cdec))


def _retention_branch(q, k_t, v, silu_g, cos_ref, sin_ref, cos_t_ref, sin_t_ref, dmask_ref, qdec_ref, kdec_ref,
                      gnw_ref, state_ref, rq_ref, rqd_ref, rkt_ref, rkdt_ref, o_ref, cdec, fillers):
    tile = q.shape[0]
    n_chunks = tile // RET_CHUNK
    half = RET_QK_DIM // 2
    cos = cos_ref[...]
    sin = sin_ref[...]
    cos_t = cos_t_ref[...]
    sin_t = sin_t_ref[...]
    for h in range(RET_HEADS):
        qk = slice(h * RET_QK_DIM, (h + 1) * RET_QK_DIM)
        qh = q[:, qk]
        qr = qh * cos + pltpu.roll(qh, half, 1) * sin
        rq_ref[:, qk] = qr.astype(BF16)
        rqd_ref[:, qk] = (qr * jnp.concatenate([qdec_ref[h]] * n_chunks, axis=0)).astype(BF16)
        k1 = k_t[h * RET_QK_DIM:h * RET_QK_DIM + half]
        k2 = k_t[h * RET_QK_DIM + half:(h + 1) * RET_QK_DIM]
        kr_t = jnp.concatenate([k1 * cos_t - k2 * sin_t, k2 * cos_t + k1 * sin_t], axis=0)
        rkt_ref[qk, :] = kr_t.astype(BF16)
        rkdt_ref[qk, :] = (kr_t * jnp.concatenate([kdec_ref[h]] * n_chunks, axis=1)).astype(BF16)

    for c in range(n_chunks):
        rows = slice(c * RET_CHUNK, (c + 1) * RET_CHUNK)
        heads = [slice(h * RET_QK_DIM, (h + 1) * RET_QK_DIM) for h in range(RET_HEADS)]
        raw = [_dot(rq_ref[rows, qk], rkt_ref[qk, rows]) for qk in heads]
        fillers[c]()
        for h in range(RET_HEADS):
            qk = heads[h]
            v_cols = slice(h * RET_V_DIM, (h + 1) * RET_V_DIM)
            vh = v[rows, v_cols]
            scores = raw[h] * dmask_ref[h]
            st = state_ref[h]
            lhs = jnp.concatenate([scores.astype(BF16), rqd_ref[rows, qk]], axis=1)
            y = _dot(lhs, jnp.concatenate([vh, st.astype(BF16)], axis=0))
            state_ref[h] = cdec[h] * st + _dot(rkdt_ref[qk, rows], vh)
            mu = jnp.mean(y, axis=-1, keepdims=True)
            yc = y - mu
            var = jnp.mean(yc * yc, axis=-1, keepdims=True)
            yn = yc * lax.rsqrt(var + LN_EPS) * gnw_ref[:, v_cols]
            o_ref[rows, v_cols] = (silu_g[h][rows] * yn).astype(BF16)


def _memkv_kernel(m_ref, w_ref, o_ref):
    o_ref[...] = _dot(m_ref[...].astype(BF16), w_ref[...]).astype(BF16)


def _mem_kv(memf, w_b):
    M, D = memf.shape
    N = w_b.shape[1]
    return pl.pallas_call(
        _memkv_kernel,
        grid=(1,),
        in_specs=[pl.BlockSpec((M, D), lambda i: (0, 0)), pl.BlockSpec((D, N), lambda i: (0, 0))],
        out_specs=pl.BlockSpec((M, N), lambda i: (0, 0)),
        out_shape=jax.ShapeDtypeStruct((M, N), BF16),
        name="mem_kv",
    )(memf, w_b)


def _cross_attention_branch(xq, k_ref, v_ref, o_ref):
    scale = XA_HEAD_DIM ** -0.5
    for h in range(XA_HEADS):
        cols = slice(h * XA_HEAD_DIM, (h + 1) * XA_HEAD_DIM)
        s = _dot_nt(xq[:, cols], k_ref[:, cols]) * scale
        m = jnp.max(s, axis=-1, keepdims=True)
        p = jnp.exp(s - m)
        l = jnp.sum(p, axis=-1, keepdims=True)
        o = _dot(p.astype(BF16), v_ref[:, cols]) / l
        o_ref[:, cols] = o.astype(BF16)


ROUTER_ROWS = 128
WTOK_LANES = 128
EXP_ROW0 = 8


def _route(logits_t):
    gl = logits_t[0:8]
    gmax = jnp.max(gl, axis=0, keepdims=True)
    p_grp = 1.0 / jnp.sum(jnp.exp(gl - gmax), axis=0, keepdims=True)
    idx8 = lax.broadcasted_iota(jnp.int32, gl.shape, 0)
    gsel = jnp.min(jnp.where(gl == gmax, idx8, 8), axis=0, keepdims=True)
    cl = jnp.zeros_like(gl)
    for g in range(N_GROUPS):
        r0 = EXP_ROW0 + g * EXPERTS_PER_GROUP
        cl = cl + jnp.where(gsel == g, logits_t[r0:r0 + EXPERTS_PER_GROUP], 0.0)
    v1 = jnp.max(cl, axis=0, keepdims=True)
    i1 = jnp.min(jnp.where(cl == v1, idx8, 8), axis=0, keepdims=True)
    cl2 = jnp.where(idx8 == i1, -jnp.inf, cl)
    v2 = jnp.max(cl2, axis=0, keepdims=True)
    i2 = jnp.min(jnp.where(cl2 == v2, idx8, 8), axis=0, keepdims=True)
    e21 = jnp.exp(v2 - v1)
    w1 = p_grp / (1.0 + e21)
    w2 = p_grp * e21 / (1.0 + e21)
    return gsel * EXPERTS_PER_GROUP + i1, gsel * EXPERTS_PER_GROUP + i2, w1, w2


def _pack_halves(v):
    half = v.shape[1] // 2
    lo = lax.bitcast_convert_type(v[:, :half].astype(BF16).astype(F32), U32)
    hi = lax.bitcast_convert_type(v[:, half:].astype(BF16).astype(F32), U32)
    return lax.bitcast_convert_type(lax.shift_right_logical(lo, U32(16)) | hi, I32)


def _unpack_halves(w):
    u = lax.bitcast_convert_type(w, U32)
    lo = lax.bitcast_convert_type(lax.shift_left(u, U32(16)), F32)
    hi = lax.bitcast_convert_type(u & U32(0xFFFF0000), F32)
    return lo, hi


def _mixer_kernel(x_ref, cos_ref, sin_ref, cos_t_ref, sin_t_ref, km_ref, vm_ref, win_ref, wgrp_ref,
                  pscale_ref, bmain_ref, bhalo_ref, dmask_ref, qdec_ref, kdec_ref, gnw_ref, wp_ref, wr_ref,
                  wa_ref, wo_ref, lnw_ref, lnb_ref, rw_ref, rb_ref, xp_ref, eid_ref, wtok_ref,
                  state_ref, halo_ref, ypool_ref, yret_ref, yxa_ref, rq_ref, rqd_ref, rkt_ref, rkdt_ref, wkt_ref,
                  *, tile, cdec):
    j = pl.program_id(1)

    @pl.when(jnp.logical_and(pl.program_id(0) == 0, j == 0))
    def _():
        wk = win_ref[:, COL_K:COL_K + RET_HEADS * RET_QK_DIM].astype(F32)
        wkt_ref[...] = jnp.transpose(wk).astype(BF16)

    @pl.when(j == 0)
    def _():
        state_ref[...] = jnp.zeros_like(state_ref)
        halo_ref[...] = jnp.zeros_like(halo_ref)

    x = x_ref[...]
    xb = x.astype(BF16)

    def proj(col, width):
        return _dot(xb, win_ref[:, col:col + width])

    part = {}
    strips = [slice(c, c + STRIP) for c in range(0, D_MODEL, STRIP)]

    def gate(branch, cols):
        return _sigmoid(proj(COL_GATES + branch * D_MODEL + cols.start, STRIP))

    def pool_part():
        def pool_gates():
            part["pool_gate"] = [gate(0, c) for c in strips]

        _pool_branch(proj(COL_POOL, POOL_WIDTH).astype(BF16), j, halo_ref, bmain_ref, bhalo_ref, wgrp_ref,
                     pscale_ref, ypool_ref, pool_gates)
        part["pool"] = [part["pool_gate"][i] * _dot(ypool_ref[...], wp_ref[:, c]) for i, c in enumerate(strips)]

    def xa_part():
        _cross_attention_branch(proj(COL_XAQ, XA_WIDTH).astype(BF16), km_ref, vm_ref, yxa_ref)
        part["xa"] = [gate(2, c) * _dot(yxa_ref[...], wa_ref[:, c]) for c in strips]

    def ret_gate_part():
        part["ret_gate"] = [gate(1, c) for c in strips]

    silu_g = []
    for h in range(RET_HEADS):
        gh = proj(COL_G + h * RET_V_DIM, RET_V_DIM)
        silu_g.append(gh * _sigmoid(gh))
    fillers = [pool_part, xa_part, ret_gate_part] + [lambda: None] * (tile // RET_CHUNK - 3)
    _retention_branch(proj(COL_Q, RET_HEADS * RET_QK_DIM), _dot_nt(wkt_ref[...], xb),
                      proj(COL_V, RET_HEADS * RET_V_DIM).astype(BF16), silu_g, cos_ref, sin_ref, cos_t_ref,
                      sin_t_ref, dmask_ref, qdec_ref, kdec_ref, gnw_ref, state_ref, rq_ref, rqd_ref, rkt_ref,
                      rkdt_ref, yret_ref, cdec, fillers)
    merged = jnp.concatenate(
        [(part["pool"][i] + part["ret_gate"][i] * _dot(yret_ref[...], wr_ref[:, c]) + part["xa"][i]).astype(BF16)
         for i, c in enumerate(strips)], axis=1)
    h = jnp.concatenate([ALPHA * x[:, c] + _dot(merged, wo_ref[:, c]) for c in strips], axis=1)
    x1 = jnp.concatenate([_layer_norm(h[r:r + LN_ROWS], lnw_ref[...], lnb_ref[...])
                          for r in range(0, tile, LN_ROWS)], axis=0)
    xp_ref[...] = _pack_halves(x1)
    x1_hi = x1.astype(BF16)
    x1_lo = (x1 - x1_hi.astype(F32)).astype(BF16)
    p4 = _dot(jnp.concatenate([x1_hi, x1_lo], axis=0), rw_ref[...])
    logits = (p4[:tile, :ROUTER_ROWS] + p4[:tile, ROUTER_ROWS:]) + (p4[tile:, :ROUTER_ROWS] + p4[tile:, ROUTER_ROWS:])
    e0, e1, w0, w1 = _route(jnp.transpose(logits) + rb_ref[...])
    eid_ref[...] = jnp.concatenate([e0, e1, jnp.zeros((8 - TOP_K, tile), I32)], axis=0)
    w_t = jnp.concatenate([w0, w1, jnp.zeros((WTOK_LANES - TOP_K, tile), F32)], axis=0)
    wtok_ref[...] = jnp.transpose(w_t)


def _mixer(xf, rope, kv, win, wgrp, pscale, gnw, wp, wr, wa, wo, lnw, lnb, rw, rb, batch, seq, mem_len,
           tile=512):
    T = xf.shape[0]
    nj = seq // tile
    cos, sin, cos_t, sin_t = rope
    bmain, bhalo = _pool_bands()
    dmask, qdec, kdec, cdec = _ret_consts()
    resident = lambda a: pl.BlockSpec(a.shape, lambda b, j: (0,) * a.ndim, pipeline_mode=pl.Buffered(1))
    rowblk = lambda w: pl.BlockSpec((tile, w), lambda b, j: (b * nj + j, 0))
    colblk = lambda r: pl.BlockSpec((r, tile), lambda b, j: (0, b * nj + j))
    consts = (win, wgrp, pscale, bmain, bhalo, dmask, qdec, kdec, gnw, wp, wr, wa, wo, lnw, lnb, rw, rb)
    return pl.pallas_call(
        functools.partial(_mixer_kernel, tile=tile, cdec=cdec),
        grid=(batch, nj),
        in_specs=[rowblk(D_MODEL), rowblk(RET_QK_DIM), rowblk(RET_QK_DIM),
                  colblk(RET_QK_DIM // 2), colblk(RET_QK_DIM // 2),
                  pl.BlockSpec((mem_len, XA_WIDTH), lambda b, j: (b, 0)),
                  pl.BlockSpec((mem_len, XA_WIDTH), lambda b, j: (b, 1))] + [resident(a) for a in consts],
        out_specs=[rowblk(PACK_W), colblk(8), rowblk(WTOK_LANES)],
        out_shape=[jax.ShapeDtypeStruct((T, PACK_W), I32), jax.ShapeDtypeStruct((8, T), I32),
                   jax.ShapeDtypeStruct((T, WTOK_LANES), F32)],
        scratch_shapes=[pltpu.VMEM((RET_HEADS, RET_QK_DIM, RET_V_DIM), F32),
                        pltpu.VMEM((2, POOL_HALO, POOL_WIDTH), BF16),
                        pltpu.VMEM((tile, POOL_WIDTH), BF16),
                        pltpu.VMEM((tile, RET_HEADS * RET_V_DIM), BF16),
                        pltpu.VMEM((tile, XA_WIDTH), BF16),
                        pltpu.VMEM((tile, RET_HEADS * RET_QK_DIM), BF16),
                        pltpu.VMEM((tile, RET_HEADS * RET_QK_DIM), BF16),
                        pltpu.VMEM((RET_HEADS * RET_QK_DIM, tile), BF16),
                        pltpu.VMEM((RET_HEADS * RET_QK_DIM, tile), BF16),
                        pltpu.VMEM((RET_HEADS * RET_QK_DIM, D_MODEL), BF16)],
        compiler_params=pltpu.CompilerParams(dimension_semantics=("arbitrary", "arbitrary"),
                                             vmem_limit_bytes=VMEM_LIMIT),
        name="mixer",
    )(xf, cos, sin, cos_t, sin_t, kv, kv, *consts)


META_LANES = 128


def _positions_kernel(eid_ref, tri_ref, low_ref, pos_ref, first_tile_ref, n_tiles_ref, *, n_tok):
    n_chunks = n_tok // RANK_CHUNK
    erow = lax.broadcasted_iota(I32, (N_EXPERTS, RANK_CHUNK), 0)

    def onehot(c):
        sl = slice(c * RANK_CHUNK, (c + 1) * RANK_CHUNK)
        m0 = eid_ref[0:1, sl] == erow
        m1 = eid_ref[1:2, sl] == erow
        return m0, m1, jnp.where(m0, 1.0, 0.0) + jnp.where(m1, 1.0, 0.0)

    counts = jnp.zeros((N_EXPERTS, 1), F32)
    for c in range(n_chunks):
        counts = counts + jnp.sum(onehot(c)[2], axis=1, keepdims=True)
    ptiles = jnp.floor((counts + (MOE_TM - 1)) * (1.0 / MOE_TM))
    ptiles_b = jnp.broadcast_to(ptiles, (N_EXPERTS, 128)).astype(BF16)
    start = _dot(low_ref[...], ptiles_b)[:, 0:1] * MOE_TM

    pos_ref[...] = jnp.zeros_like(pos_ref)
    carry = start - 1.0
    for c in range(n_chunks):
        sl = slice(c * RANK_CHUNK, (c + 1) * RANK_CHUNK)
        m0, m1, oh = onehot(c)
        rank = _dot(oh.astype(BF16), tri_ref[...]) + carry
        pos_ref[0:1, sl] = jnp.sum(jnp.where(m0, rank, 0.0), axis=0, keepdims=True).astype(I32)
        pos_ref[1:2, sl] = jnp.sum(jnp.where(m1, rank, 0.0), axis=0, keepdims=True).astype(I32)
        carry = carry + jnp.sum(oh, axis=1, keepdims=True)

    first_tile_ref[...] = jnp.broadcast_to(start * (1.0 / MOE_TM), first_tile_ref.shape).astype(I32)
    n_tiles_ref[...] = jnp.broadcast_to(ptiles, n_tiles_ref.shape).astype(I32)


def _positions(eid):
    T = eid.shape[1]
    r = np.arange(RANK_CHUNK)
    tri = jnp.asarray(r[:, None] <= r[None, :], BF16)
    e = np.arange(N_EXPERTS)
    low = jnp.asarray(e[None, :] < e[:, None], BF16)
    full = lambda a: pl.BlockSpec(a.shape, lambda i: (0,) * a.ndim)
    return pl.pallas_call(
        functools.partial(_positions_kernel, n_tok=T),
        grid=(1,),
        in_specs=[full(eid), full(tri), full(low)],
        out_specs=[pl.BlockSpec((8, T), lambda i: (0, 0)),
                   pl.BlockSpec((N_EXPERTS, META_LANES), lambda i: (0, 0)),
                   pl.BlockSpec((N_EXPERTS, META_LANES), lambda i: (0, 0))],
        out_shape=[jax.ShapeDtypeStruct((8, T), I32), jax.ShapeDtypeStruct((N_EXPERTS, META_LANES), I32),
                   jax.ShapeDtypeStruct((N_EXPERTS, META_LANES), I32)],
        name="route_positions",
    )(eid, tri, low)


def _sc_workers():
    info = plsc.get_sparse_core_info()
    return info.num_cores, info.num_cores * info.num_subcores


def _sc_scatter_rows(xp, pos2d, n_out):
    T, W = xp.shape
    n_cores, n_workers = _sc_workers()
    cpw = T // SC_CHUNK // n_workers
    mesh = plsc.VectorSubcoreMesh(core_axis_name="c", subcore_axis_name="s")

    @functools.partial(
        pl.kernel, mesh=mesh, out_type=jax.ShapeDtypeStruct((n_out, W), I32),
        scratch_types=[pltpu.VMEM((TOP_K * cpw, SC_CHUNK), I32), pltpu.VMEM((2, SC_CHUNK, W), I32),
                       pltpu.SemaphoreType.DMA((2,)), pltpu.SemaphoreType.DMA((2,))],
        name="sc_scatter_rows")
    def k(x_hbm, pos_hbm, out_hbm, idx_v, rows_v, rd_sem, wr_sem):
        wid = lax.axis_index("s") * n_cores + lax.axis_index("c")
        for s in range(TOP_K):
            pltpu.sync_copy(pos_hbm.at[pl.ds(s * (T // SC_CHUNK) + wid * cpw, cpw)],
                            idx_v.at[pl.ds(s * cpw, cpw)])

        def read(j):
            return pltpu.make_async_copy(x_hbm.at[pl.ds((wid * cpw + j) * SC_CHUNK, SC_CHUNK)],
                                         rows_v.at[j % 2], rd_sem.at[j % 2])

        def write(j, s):
            return pltpu.make_async_copy(rows_v.at[j % 2], out_hbm.at[idx_v.at[s * cpw + j]], wr_sem.at[j % 2])

        _sc_two_buffer_stream(cpw, read, lambda j: [write(j, s) for s in range(TOP_K)])

    return k(xp, pos2d)


def _sc_two_buffer_stream(n, read, writes):
    read(0).start()
    for j in range(n):
        read(j).wait()
        if j + 1 < n:
            if j >= 1:
                for w in writes(j - 1):
                    w.wait()
            read(j + 1).start()
        for w in writes(j):
            w.start()
    for j in range(max(n - 2, 0), n):
        for w in writes(j):
            w.wait()


def _sc_gather_rows(y, idx2d):
    W = y.shape[1]
    n = idx2d.shape[0] * SC_CHUNK
    n_cores, n_workers = _sc_workers()
    cpw = n // SC_CHUNK // n_workers
    mesh = plsc.VectorSubcoreMesh(core_axis_name="c", subcore_axis_name="s")

    @functools.partial(
        pl.kernel, mesh=mesh, out_type=jax.ShapeDtypeStruct((n, W), I32),
        scratch_types=[pltpu.VMEM((cpw, SC_CHUNK), I32), pltpu.VMEM((2, SC_CHUNK, W), I32),
                       pltpu.SemaphoreType.DMA((2,)), pltpu.SemaphoreType.DMA((2,))],
        name="sc_gather_rows")
    def k(y_hbm, idx_hbm, out_hbm, idx_v, rows_v, rd_sem, wr_sem):
        wid = lax.axis_index("s") * n_cores + lax.axis_index("c")
        pltpu.sync_copy(idx_hbm.at[pl.ds(wid * cpw, cpw)], idx_v)

        def read(j):
            return pltpu.make_async_copy(y_hbm.at[idx_v.at[j]], rows_v.at[j % 2], rd_sem.at[j % 2])

        def write(j):
            return pltpu.make_async_copy(rows_v.at[j % 2], out_hbm.at[pl.ds((wid * cpw + j) * SC_CHUNK, SC_CHUNK)],
                                         wr_sem.at[j % 2])

        _sc_two_buffer_stream(cpw, read, lambda j: [write(j)])

    return k(y, idx2d)


def _routed_kernel(first_ref, count_ref, xs_hbm, wg_ref, wu_ref, wd_ref, ys_hbm, xbuf, ybuf, in_sem, out_sem):
    e = pl.program_id(0)
    last = pl.num_programs(0) - 1
    total = first_ref[last] + count_ref[last]

    def in_copy(g):
        slot = lax.rem(g, RING)
        return pltpu.make_async_copy(xs_hbm.at[pl.ds(g * MOE_TM, MOE_TM)], xbuf.at[slot], in_sem.at[slot])

    def out_copy(g):
        slot = lax.rem(g, RING)
        return pltpu.make_async_copy(ybuf.at[slot], ys_hbm.at[pl.ds(g * MOE_TM, MOE_TM)], out_sem.at[slot])

    @pl.when(e == 0)
    def _():
        for g0 in range(RING - 1):
            @pl.when(g0 < total)
            def _():
                in_copy(g0).start()

    wg = wg_ref[0].astype(BF16)
    wu = wu_ref[0].astype(BF16)
    wd = wd_ref[0].astype(BF16)

    def tile_step(i, carry):
        g = first_ref[e] + i
        slot = lax.rem(g, RING)
        in_copy(g).wait()

        @pl.when(g + RING - 1 < total)
        def _():
            in_copy(g + RING - 1).start()

        @pl.when(g >= RING)
        def _():
            out_copy(g - RING).wait()

        lo, hi = _unpack_halves(xbuf[slot])
        lo = lo.astype(BF16)
        hi = hi.astype(BF16)
        a = _dot(lo, wg[:PACK_W]) + _dot(hi, wg[PACK_W:])
        b = _dot(lo, wu[:PACK_W]) + _dot(hi, wu[PACK_W:])
        act = (a * _sigmoid(a) * b).astype(BF16)
        ybuf[slot] = _pack_halves(_dot(act, wd))
        out_copy(g).start()
        return carry

    lax.fori_loop(0, count_ref[e], tile_step, 0)

    @pl.when(e == last)
    def _():
        for back in range(RING, 0, -1):
            @pl.when(total >= back)
            def _():
                out_copy(total - back).wait()


def _routed_mlp(first_tile, n_tiles, xs, wg, wu, wd):
    R = xs.shape[0]
    any_space = pl.BlockSpec(memory_space=pl.ANY)
    return pl.pallas_call(
        _routed_kernel,
        grid_spec=pltpu.PrefetchScalarGridSpec(
            num_scalar_prefetch=2,
            grid=(N_EXPERTS,),
            in_specs=[any_space,
                      pl.BlockSpec((1, D_MODEL, D_EXPERT), lambda e, ft, nt: (e, 0, 0)),
                      pl.BlockSpec((1, D_MODEL, D_EXPERT), lambda e, ft, nt: (e, 0, 0)),
                      pl.BlockSpec((1, D_EXPERT, D_MODEL), lambda e, ft, nt: (e, 0, 0))],
            out_specs=any_space,
            scratch_shapes=[pltpu.VMEM((RING, MOE_TM, PACK_W), I32), pltpu.VMEM((RING, MOE_TM, PACK_W), I32),
                            pltpu.SemaphoreType.DMA((RING,)), pltpu.SemaphoreType.DMA((RING,))]),
        out_shape=jax.ShapeDtypeStruct((R, PACK_W), I32),
        compiler_params=pltpu.CompilerParams(dimension_semantics=("arbitrary",)),
        name="routed_mlp",
    )(first_tile, n_tiles, xs, wg, wu, wd)


def _combine_kernel(xp_ref, y0_ref, y1_ref, wtok_ref, lnw_ref, lnb_ref, *out_refs):
    o_ref = out_refs[-1]
    w0 = wtok_ref[:, 0:1]
    w1 = wtok_ref[:, 1:2]
    xlo, xhi = _unpack_halves(xp_ref[...])
    y0lo, y0hi = _unpack_halves(y0_ref[...])
    y1lo, y1hi = _unpack_halves(y1_ref[...])
    h = jnp.concatenate([ALPHA * xlo + (w0 * y0lo + w1 * y1lo), ALPHA * xhi + (w0 * y0hi + w1 * y1hi)], axis=1)
    o_ref[...] = _layer_norm(h, lnw_ref[...], lnb_ref[...])


def _combine_ln2(xp, yg_parts, wtok, lnw, lnb, tile=1024):
    T = xp.shape[0]
    n_parts = len(yg_parts)
    nt = T // tile // n_parts
    full = lambda a: pl.BlockSpec(a.shape, lambda i: (0,) * a.ndim)
    out = None
    for p, yg in enumerate(yg_parts):
        rows = lambda w, p=p: pl.BlockSpec((tile, w), lambda i: (i + p * nt, 0))
        in_specs = [rows(PACK_W),
                    pl.BlockSpec((tile, PACK_W), lambda i: (i, 0)),
                    pl.BlockSpec((tile, PACK_W), lambda i: (i + nt, 0)),
                    rows(WTOK_LANES), full(lnw), full(lnb)]
        args = [xp, yg, yg, wtok, lnw, lnb]
        aliases = {}
        if out is not None:
            in_specs.append(pl.BlockSpec(memory_space=pl.ANY))
            args.append(out)
            aliases = {len(args) - 1: 0}
        out = pl.pallas_call(
            _combine_kernel,
            grid=(nt,),
            in_specs=in_specs,
            out_specs=rows(D_MODEL),
            out_shape=jax.ShapeDtypeStruct((T, D_MODEL), F32),
            input_output_aliases=aliases,
            compiler_params=pltpu.CompilerParams(dimension_semantics=("arbitrary",), vmem_limit_bytes=VMEM_LIMIT),
            name="combine_ln2",
        )(*args)
    return out


def _router_params(w_grp, b_grp, w_exp, b_exp):
    rw = jnp.zeros((D_MODEL, ROUTER_ROWS), F32)
    rw = rw.at[:, 0:N_GROUPS].set(w_grp).at[:, EXP_ROW0:EXP_ROW0 + N_EXPERTS].set(w_exp)
    rb = jnp.zeros((ROUTER_ROWS,), F32).at[N_GROUPS:8].set(NEG_BIG)
    rb = rb.at[0:N_GROUPS].set(b_grp).at[EXP_ROW0:EXP_ROW0 + N_EXPERTS].set(b_exp)
    rw_hi = rw.astype(BF16)
    rw_lo = (rw - rw_hi.astype(F32)).astype(BF16)
    return jnp.concatenate([rw_hi, rw_lo], axis=1), rb[:, None]


def kernel(x, mem, positions, w_in, w_pool_grp, pool_scale, ret_gn_w, w_mem_kv, w_br_pool, w_br_ret, w_br_xa,
           w_out, ln1_w, ln1_b, w_grp_router, b_grp_router, w_exp_router, b_exp_router, w_exp_gate, w_exp_up,
           w_exp_down, ln2_w, ln2_b):
    B, S, D = x.shape
    assert D == D_MODEL and w_in.shape[0] == DEPTH and S % 512 == 0
    T = B * S
    M = mem.shape[1]
    l = 0
    xf = x.reshape(T, D)

    rope = _rope_table(positions.reshape(1, T))
    kv = _mem_kv(mem.reshape(B * M, D), w_mem_kv[l].astype(BF16))
    rw, rb = _router_params(w_grp_router[l], b_grp_router[l], w_exp_router[l], b_exp_router[l])
    xp, eid, wtok = _mixer(xf, rope, kv, w_in[l].astype(BF16),
                           w_pool_grp[l].astype(BF16), pool_scale[l][None, :], ret_gn_w[l].reshape(1, -1),
                           w_br_pool[l].astype(BF16), w_br_ret[l].astype(BF16), w_br_xa[l].astype(BF16),
                           w_out[l].astype(BF16), ln1_w[l][None, :], ln1_b[l][None, :], rw, rb, B, S, M)

    pos, first_tile, n_tiles = _positions(eid)
    pos2d = pos[0:TOP_K].reshape(TOP_K * T // SC_CHUNK, SC_CHUNK)
    max_tiles = (TOP_K * T + N_EXPERTS * (MOE_TM - 1)) // MOE_TM
    xs = _sc_scatter_rows(xp, pos2d, max_tiles * MOE_TM)
    ys = _routed_mlp(first_tile[:, 0], n_tiles[:, 0], xs,
                     w_exp_gate[l].reshape(N_EXPERTS, D_MODEL, D_EXPERT),
                     w_exp_up[l].reshape(N_EXPERTS, D_MODEL, D_EXPERT),
                     w_exp_down[l].reshape(N_EXPERTS, D_EXPERT, D_MODEL))
    rng = T // COMBINE_PARTS
    yg_parts = [_sc_gather_rows(ys, pos[0:TOP_K, p * rng:(p + 1) * rng].reshape(TOP_K * rng // SC_CHUNK, SC_CHUNK))
                for p in range(COMBINE_PARTS)]
    out = _combine_ln2(xp, yg_parts, wtok, ln2_w[l][None, :], ln2_b[l][None, :])
    return out.reshape(B, S, D)
```

```python
import functools

import numpy as np
import jax
import jax.numpy as jnp
from jax import lax
from jax.experimental import pallas as pl
from jax.experimental.pallas import tpu as pltpu
from jax.experimental.pallas import tpu_sc as plsc

F32 = jnp.float32
BF16 = jnp.bfloat16
I32 = jnp.int32
U32 = jnp.uint32

D_MODEL = 1024
POOL_WINDOWS = (2, 4, 8, 16)
POOL_GROUP_DIM = 128
POOL_WIDTH = 512
POOL_HALO = 16
RET_HEADS = 4
RET_QK_DIM = 128
RET_V_DIM = 256
RET_CHUNK = 128
ROPE_BASE = 10000.0
XA_HEADS = 4
XA_HEAD_DIM = 128
XA_WIDTH = 512
N_GROUPS = 4
EXPERTS_PER_GROUP = 8
N_EXPERTS = N_GROUPS * EXPERTS_PER_GROUP
D_EXPERT = 256
LN_EPS = 1e-5
DEPTH = 1
ALPHA = (2.0 * DEPTH) ** 0.25
NEG_BIG = -1e30

COL_POOL, COL_Q, COL_K, COL_V, COL_G, COL_XAQ, COL_GATES = 0, 512, 1024, 1536, 2560, 3584, 4096

VMEM_LIMIT = 56 * 1024 * 1024

TOP_K = 2
PACK_W = D_MODEL // 2
MOE_TM = 512
SC_CHUNK = 64
RANK_CHUNK = 512
COMBINE_PARTS = 2
RING = 4
STRIP = 256
LN_ROWS = 32


def _dot(a, b):
    return jnp.dot(a, b, preferred_element_type=F32)


def _dot_nt(a, b, precision=None):
    return lax.dot_general(a, b, (((1,), (1,)), ((), ())), preferred_element_type=F32, precision=precision)


def _sigmoid(z):
    return 1.0 / (1.0 + jnp.exp2(z * (-1.0 / np.log(2.0))))


def _layer_norm(h, w, b):
    mu = jnp.mean(h, axis=-1, keepdims=True)
    hc = h - mu
    var = jnp.mean(hc * hc, axis=-1, keepdims=True)
    return hc * lax.rsqrt(var + LN_EPS) * w + b


def _rope_kernel(pos_ref, freq_ref, cos_ref, sin_ref, cos_t_ref, sin_t_ref):
    ang = freq_ref[...] * pos_ref[...].astype(F32)
    cos_t = jnp.cos(ang)
    sin_t = jnp.sin(ang)
    cos_t_ref[...] = cos_t
    sin_t_ref[...] = sin_t
    cos_ref[...] = jnp.transpose(jnp.concatenate([cos_t, cos_t], axis=0))
    sin_ref[...] = jnp.transpose(jnp.concatenate([-sin_t, sin_t], axis=0))


def _rope_table(pos_row, tile=2048):
    T = pos_row.shape[1]
    half = RET_QK_DIM // 2
    inv_freq = (ROPE_BASE ** (-np.arange(half, dtype=np.float64) / half)).astype(np.float32)
    freq = jnp.asarray(inv_freq[:, None])
    out = pl.BlockSpec((tile, RET_QK_DIM), lambda i: (i, 0))
    out_t = pl.BlockSpec((half, tile), lambda i: (0, i))
    return pl.pallas_call(
        _rope_kernel,
        grid=(T // tile,),
        in_specs=[pl.BlockSpec((1, tile), lambda i: (0, i)), pl.BlockSpec((half, 1), lambda i: (0, 0))],
        out_specs=[out, out, out_t, out_t],
        out_shape=[jax.ShapeDtypeStruct((T, RET_QK_DIM), F32)] * 2 + [jax.ShapeDtypeStruct((half, T), F32)] * 2,
        name="rope_table",
    )(pos_row, freq)


POOL_SUB = 256


def _pool_bands():
    r = np.arange(POOL_SUB)[:, None]
    c = np.arange(POOL_SUB)[None, :]
    ch = np.arange(POOL_HALO)[None, :] - POOL_HALO
    main = np.stack([((r - c >= 0) & (r - c < w)) for w in POOL_WINDOWS]).astype(np.float32)
    halo = np.stack([((r - ch >= 0) & (r - ch < w)) for w in POOL_WINDOWS]).astype(np.float32)
    return jnp.asarray(main, BF16), jnp.asarray(halo, BF16)


def _pool_branch(ub, j, halo_ref, bmain_ref, bhalo_ref, wg_ref, scale_ref, o_ref, filler):
    tile = ub.shape[0]
    s0 = j * tile
    slot = lax.rem(j, 2)
    blocks = [(sb * POOL_SUB, g) for sb in range(tile // POOL_SUB) for g in range(len(POOL_WINDOWS))]
    wsum = {}
    for r0, g in blocks:
        cols = slice(g * POOL_GROUP_DIM, (g + 1) * POOL_GROUP_DIM)
        prev = halo_ref[slot] if r0 == 0 else ub[r0 - POOL_HALO:r0]
        wsum[r0, g] = _dot(bmain_ref[g], ub[r0:r0 + POOL_SUB, cols]) + _dot(bhalo_ref[g], prev[:, cols])
    filler()
    for r0, g in blocks:
        cols = slice(g * POOL_GROUP_DIM, (g + 1) * POOL_GROUP_DIM)
        pos = s0 + r0 + lax.broadcasted_iota(I32, (POOL_SUB, POOL_GROUP_DIM), 0)
        cnt = jnp.minimum(pos + 1, POOL_WINDOWS[g]).astype(F32)
        pooled = wsum[r0, g] / cnt - ub[r0:r0 + POOL_SUB, cols].astype(F32)
        mixed = _dot(pooled.astype(BF16), wg_ref[g]) * scale_ref[:, cols]
        o_ref[r0:r0 + POOL_SUB, cols] = mixed.astype(BF16)
    halo_ref[1 - slot] = ub[tile - POOL_HALO:tile]


def _ret_consts():
    h = np.arange(RET_HEADS, dtype=np.float64)
    log_gamma = np.log1p(-np.exp2(-5.0 - h))
    pos = np.arange(RET_CHUNK, dtype=np.float64)
    diff = pos[:, None] - pos[None, :]
    kscale = RET_QK_DIM ** -0.5
    dmask = kscale * np.where(diff >= 0, np.exp(log_gamma[:, None, None] * np.maximum(diff, 0.0)), 0.0)
    qdec = np.exp(log_gamma[:, None] * (pos + 1.0)[None, :])
    kdec = kscale * np.exp(log_gamma[:, None] * (RET_CHUNK - 1.0 - pos)[None, :])
    cdec = np.exp(log_gamma * RET_CHUNK)
    lanes = lambda a: np.broadcast_to(a[:, :, None], (RET_HEADS, RET_CHUNK, RET_QK_DIM))
    kdec_t = np.broadcast_to(kdec[:, None, :], (RET_HEADS, RET_QK_DIM, RET_CHUNK))
    return (jnp.asarray(dmask, F32), jnp.asarray(lanes(qdec), F32), jnp.asarray(kdec_t, F32),
            tuple(float(v) for v in cdec))


def _retention_branch(q, k_t, v, silu_g, cos_ref, sin_ref, cos_t_ref, sin_t_ref, dmask_ref, qdec_ref, kdec_ref,
                      gnw_ref, state_ref, rq_ref, rqd_ref, rkt_ref, rkdt_ref, o_ref, cdec, fillers):
    tile = q.shape[0]
    n_chunks = tile // RET_CHUNK
    half = RET_QK_DIM // 2
    cos = cos_ref[...]
    sin = sin_ref[...]
    cos_t = cos_t_ref[...]
    sin_t = sin_t_ref[...]
    for h in range(RET_HEADS):
        qk = slice(h * RET_QK_DIM, (h + 1) * RET_QK_DIM)
        qh = q[:, qk]
        qr = qh * cos + pltpu.roll(qh, half, 1) * sin
        rq_ref[:, qk] = qr.astype(BF16)
        rqd_ref[:, qk] = (qr * jnp.concatenate([qdec_ref[h]] * n_chunks, axis=0)).astype(BF16)
        k1 = k_t[h * RET_QK_DIM:h * RET_QK_DIM + half]
        k2 = k_t[h * RET_QK_DIM + half:(h + 1) * RET_QK_DIM]
        kr_t = jnp.concatenate([k1 * cos_t - k2 * sin_t, k2 * cos_t + k1 * sin_t], axis=0)
        rkt_ref[qk, :] = kr_t.astype(BF16)
        rkdt_ref[qk, :] = (kr_t * jnp.concatenate([kdec_ref[h]] * n_chunks, axis=1)).astype(BF16)

    chunks = [slice(c * RET_CHUNK, (c + 1) * RET_CHUNK) for c in range(n_chunks)]
    heads = [slice(h * RET_QK_DIM, (h + 1) * RET_QK_DIM) for h in range(RET_HEADS)]
    v_heads = [slice(h * RET_V_DIM, (h + 1) * RET_V_DIM) for h in range(RET_HEADS)]
    raw = {(c, h): _dot(rq_ref[chunks[c], heads[h]], rkt_ref[heads[h], chunks[c]])
           for c in range(n_chunks) for h in range(RET_HEADS)}
    incr = {(c, h): _dot(rkdt_ref[heads[h], chunks[c]], v[chunks[c], v_heads[h]])
            for c in range(n_chunks) for h in range(RET_HEADS)}
    state_in = {}
    for h in range(RET_HEADS):
        st = state_ref[h]
        for c in range(n_chunks):
            state_in[c, h] = st.astype(BF16)
            st = cdec[h] * st + incr[c, h]
        state_ref[h] = st

    for c in range(n_chunks):
        rows = chunks[c]
        fillers[c]()
        for h in range(RET_HEADS):
            qk = heads[h]
            v_cols = v_heads[h]
            scores = raw[c, h] * dmask_ref[h]
            lhs = jnp.concatenate([scores.astype(BF16), rqd_ref[rows, qk]], axis=1)
            y = _dot(lhs, jnp.concatenate([v[rows, v_cols], state_in[c, h]], axis=0))
            mu = jnp.mean(y, axis=-1, keepdims=True)
            yc = y - mu
            var = jnp.mean(yc * yc, axis=-1, keepdims=True)
            yn = yc * lax.rsqrt(var + LN_EPS) * gnw_ref[:, v_cols]
            o_ref[rows, v_cols] = (silu_g[h][rows] * yn).astype(BF16)


def _memkv_kernel(m_ref, w_ref, o_ref):
    o_ref[...] = _dot(m_ref[...].astype(BF16), w_ref[...]).astype(BF16)


def _mem_kv(memf, w_b):
    M, D = memf.shape
    N = w_b.shape[1]
    return pl.pallas_call(
        _memkv_kernel,
        grid=(1,),
        in_specs=[pl.BlockSpec((M, D), lambda i: (0, 0)), pl.BlockSpec((D, N), lambda i: (0, 0))],
        out_specs=pl.BlockSpec((M, N), lambda i: (0, 0)),
        out_shape=jax.ShapeDtypeStruct((M, N), BF16),
        name="mem_kv",
    )(memf, w_b)


def _cross_attention_branch(xq, k_ref, v_ref, o_ref):
    scale = XA_HEAD_DIM ** -0.5
    for h in range(XA_HEADS):
        cols = slice(h * XA_HEAD_DIM, (h + 1) * XA_HEAD_DIM)
        s = _dot_nt(xq[:, cols], k_ref[:, cols]) * scale
        m = jnp.max(s, axis=-1, keepdims=True)
        p = jnp.exp(s - m)
        l = jnp.sum(p, axis=-1, keepdims=True)
        o = _dot(p.astype(BF16), v_ref[:, cols]) / l
        o_ref[:, cols] = o.astype(BF16)


ROUTER_ROWS = 128
WTOK_LANES = 128
EXP_ROW0 = 8


def _route(logits_t):
    gl = logits_t[0:8]
    gmax = jnp.max(gl, axis=0, keepdims=True)
    p_grp = 1.0 / jnp.sum(jnp.exp(gl - gmax), axis=0, keepdims=True)
    idx8 = lax.broadcasted_iota(jnp.int32, gl.shape, 0)
    gsel = jnp.min(jnp.where(gl == gmax, idx8, 8), axis=0, keepdims=True)
    cl = jnp.zeros_like(gl)
    for g in range(N_GROUPS):
        r0 = EXP_ROW0 + g * EXPERTS_PER_GROUP
        cl = cl + jnp.where(gsel == g, logits_t[r0:r0 + EXPERTS_PER_GROUP], 0.0)
    v1 = jnp.max(cl, axis=0, keepdims=True)
    i1 = jnp.min(jnp.where(cl == v1, idx8, 8), axis=0, keepdims=True)
    cl2 = jnp.where(idx8 == i1, -jnp.inf, cl)
    v2 = jnp.max(cl2, axis=0, keepdims=True)
    i2 = jnp.min(jnp.where(cl2 == v2, idx8, 8), axis=0, keepdims=True)
    e21 = jnp.exp(v2 - v1)
    w1 = p_grp / (1.0 + e21)
    w2 = p_grp * e21 / (1.0 + e21)
    return gsel * EXPERTS_PER_GROUP + i1, gsel * EXPERTS_PER_GROUP + i2, w1, w2


def _pack_halves(v):
    half = v.shape[1] // 2
    lo = lax.bitcast_convert_type(v[:, :half].astype(BF16).astype(F32), U32)
    hi = lax.bitcast_convert_type(v[:, half:].astype(BF16).astype(F32), U32)
    return lax.bitcast_convert_type(lax.shift_right_logical(lo, U32(16)) | hi, I32)


def _unpack_halves(w):
    u = lax.bitcast_convert_type(w, U32)
    lo = lax.bitcast_convert_type(lax.shift_left(u, U32(16)), F32)
    hi = lax.bitcast_convert_type(u & U32(0xFFFF0000), F32)
    return lo, hi


def _mixer_kernel(x_ref, cos_ref, sin_ref, cos_t_ref, sin_t_ref, km_ref, vm_ref, win_ref, wgrp_ref,
                  pscale_ref, bmain_ref, bhalo_ref, dmask_ref, qdec_ref, kdec_ref, gnw_ref, wp_ref, wr_ref,
                  wa_ref, wo_ref, lnw_ref, lnb_ref, rw_ref, rb_ref, xp_ref, eid_ref, wtok_ref,
                  state_ref, halo_ref, ypool_ref, yret_ref, yxa_ref, rq_ref, rqd_ref, rkt_ref, rkdt_ref, wkt_ref,
                  *, tile, cdec):
    j = pl.program_id(1)

    @pl.when(jnp.logical_and(pl.program_id(0) == 0, j == 0))
    def _():
        wk = win_ref[:, COL_K:COL_K + RET_HEADS * RET_QK_DIM].astype(F32)
        wkt_ref[...] = jnp.transpose(wk).astype(BF16)

    @pl.when(j == 0)
    def _():
        state_ref[...] = jnp.zeros_like(state_ref)
        halo_ref[...] = jnp.zeros_like(halo_ref)

    x = x_ref[...]
    xb = x.astype(BF16)

    def proj(col, width):
        return _dot(xb, win_ref[:, col:col + width])

    part = {}
    strips = [slice(c, c + STRIP) for c in range(0, D_MODEL, STRIP)]

    def gate(branch, cols):
        return _sigmoid(proj(COL_GATES + branch * D_MODEL + cols.start, STRIP))

    def pool_part():
        def pool_gates():
            part["pool_gate"] = [gate(0, c) for c in strips]

        _pool_branch(proj(COL_POOL, POOL_WIDTH).astype(BF16), j, halo_ref, bmain_ref, bhalo_ref, wgrp_ref,
                     pscale_ref, ypool_ref, pool_gates)
        part["pool"] = [part["pool_gate"][i] * _dot(ypool_ref[...], wp_ref[:, c]) for i, c in enumerate(strips)]

    def xa_part():
        _cross_attention_branch(proj(COL_XAQ, XA_WIDTH).astype(BF16), km_ref, vm_ref, yxa_ref)
        part["xa"] = [gate(2, c) * _dot(yxa_ref[...], wa_ref[:, c]) for c in strips]

    def ret_gate_part():
        part["ret_gate"] = [gate(1, c) for c in strips]

    silu_g = []
    for h in range(RET_HEADS):
        gh = proj(COL_G + h * RET_V_DIM, RET_V_DIM)
        silu_g.append(gh * _sigmoid(gh))
    fillers = [pool_part, xa_part, ret_gate_part] + [lambda: None] * (tile // RET_CHUNK - 3)
    _retention_branch(proj(COL_Q, RET_HEADS * RET_QK_DIM), _dot_nt(wkt_ref[...], xb),
                      proj(COL_V, RET_HEADS * RET_V_DIM).astype(BF16), silu_g, cos_ref, sin_ref, cos_t_ref,
                      sin_t_ref, dmask_ref, qdec_ref, kdec_ref, gnw_ref, state_ref, rq_ref, rqd_ref, rkt_ref,
                      rkdt_ref, yret_ref, cdec, fillers)
    merged = jnp.concatenate(
        [(part["pool"][i] + part["ret_gate"][i] * _dot(yret_ref[...], wr_ref[:, c]) + part["xa"][i]).astype(BF16)
         for i, c in enumerate(strips)], axis=1)
    h = jnp.concatenate([ALPHA * x[:, c] + _dot(merged, wo_ref[:, c]) for c in strips], axis=1)
    x1 = jnp.concatenate([_layer_norm(h[r:r + LN_ROWS], lnw_ref[...], lnb_ref[...])
                          for r in range(0, tile, LN_ROWS)], axis=0)
    xp_ref[...] = _pack_halves(x1)
    x1_hi = x1.astype(BF16)
    x1_lo = (x1 - x1_hi.astype(F32)).astype(BF16)
    p4 = _dot(jnp.concatenate([x1_hi, x1_lo], axis=0), rw_ref[...])
    logits = (p4[:tile, :ROUTER_ROWS] + p4[:tile, ROUTER_ROWS:]) + (p4[tile:, :ROUTER_ROWS] + p4[tile:, ROUTER_ROWS:])
    e0, e1, w0, w1 = _route(jnp.transpose(logits) + rb_ref[...])
    eid_ref[...] = jnp.concatenate([e0, e1, jnp.zeros((8 - TOP_K, tile), I32)], axis=0)
    w_t = jnp.concatenate([w0, w1, jnp.zeros((WTOK_LANES - TOP_K, tile), F32)], axis=0)
    wtok_ref[...] = jnp.transpose(w_t)


def _mixer(xf, rope, kv, win, wgrp, pscale, gnw, wp, wr, wa, wo, lnw, lnb, rw, rb, batch, seq, mem_len,
           tile=512):
    T = xf.shape[0]
    nj = seq // tile
    cos, sin, cos_t, sin_t = rope
    bmain, bhalo = _pool_bands()
    dmask, qdec, kdec, cdec = _ret_consts()
    resident = lambda a: pl.BlockSpec(a.shape, lambda b, j: (0,) * a.ndim, pipeline_mode=pl.Buffered(1))
    rowblk = lambda w: pl.BlockSpec((tile, w), lambda b, j: (b * nj + j, 0))
    colblk = lambda r: pl.BlockSpec((r, tile), lambda b, j: (0, b * nj + j))
    consts = (win, wgrp, pscale, bmain, bhalo, dmask, qdec, kdec, gnw, wp, wr, wa, wo, lnw, lnb, rw, rb)
    return pl.pallas_call(
        functools.partial(_mixer_kernel, tile=tile, cdec=cdec),
        grid=(batch, nj),
        in_specs=[rowblk(D_MODEL), rowblk(RET_QK_DIM), rowblk(RET_QK_DIM),
                  colblk(RET_QK_DIM // 2), colblk(RET_QK_DIM // 2),
                  pl.BlockSpec((mem_len, XA_WIDTH), lambda b, j: (b, 0)),
                  pl.BlockSpec((mem_len, XA_WIDTH), lambda b, j: (b, 1))] + [resident(a) for a in consts],
        out_specs=[rowblk(PACK_W), colblk(8), rowblk(WTOK_LANES)],
        out_shape=[jax.ShapeDtypeStruct((T, PACK_W), I32), jax.ShapeDtypeStruct((8, T), I32),
                   jax.ShapeDtypeStruct((T, WTOK_LANES), F32)],
        scratch_shapes=[pltpu.VMEM((RET_HEADS, RET_QK_DIM, RET_V_DIM), F32),
                        pltpu.VMEM((2, POOL_HALO, POOL_WIDTH), BF16),
                        pltpu.VMEM((tile, POOL_WIDTH), BF16),
                        pltpu.VMEM((tile, RET_HEADS * RET_V_DIM), BF16),
                        pltpu.VMEM((tile, XA_WIDTH), BF16),
                        pltpu.VMEM((tile, RET_HEADS * RET_QK_DIM), BF16),
                        pltpu.VMEM((tile, RET_HEADS * RET_QK_DIM), BF16),
                        pltpu.VMEM((RET_HEADS * RET_QK_DIM, tile), BF16),
                        pltpu.VMEM((RET_HEADS * RET_QK_DIM, tile), BF16),
                        pltpu.VMEM((RET_HEADS * RET_QK_DIM, D_MODEL), BF16)],
        compiler_params=pltpu.CompilerParams(dimension_semantics=("arbitrary", "arbitrary"),
                                             vmem_limit_bytes=VMEM_LIMIT),
        name="mixer",
    )(xf, cos, sin, cos_t, sin_t, kv, kv, *consts)


META_LANES = 128


def _positions_kernel(eid_ref, tri_ref, low_ref, pos_ref, first_tile_ref, n_tiles_ref, *, n_tok):
    n_chunks = n_tok // RANK_CHUNK
    erow = lax.broadcasted_iota(I32, (N_EXPERTS, RANK_CHUNK), 0)

    def onehot(c):
        sl = slice(c * RANK_CHUNK, (c + 1) * RANK_CHUNK)
        m0 = eid_ref[0:1, sl] == erow
        m1 = eid_ref[1:2, sl] == erow
        return m0, m1, jnp.where(m0, 1.0, 0.0) + jnp.where(m1, 1.0, 0.0)

    counts = jnp.zeros((N_EXPERTS, 1), F32)
    for c in range(n_chunks):
        counts = counts + jnp.sum(onehot(c)[2], axis=1, keepdims=True)
    ptiles = jnp.floor((counts + (MOE_TM - 1)) * (1.0 / MOE_TM))
    ptiles_b = jnp.broadcast_to(ptiles, (N_EXPERTS, 128)).astype(BF16)
    start = _dot(low_ref[...], ptiles_b)[:, 0:1] * MOE_TM

    pos_ref[...] = jnp.zeros_like(pos_ref)
    carry = start - 1.0
    for c in range(n_chunks):
        sl = slice(c * RANK_CHUNK, (c + 1) * RANK_CHUNK)
        m0, m1, oh = onehot(c)
        rank = _dot(oh.astype(BF16), tri_ref[...]) + carry
        pos_ref[0:1, sl] = jnp.sum(jnp.where(m0, rank, 0.0), axis=0, keepdims=True).astype(I32)
        pos_ref[1:2, sl] = jnp.sum(jnp.where(m1, rank, 0.0), axis=0, keepdims=True).astype(I32)
        carry = carry + jnp.sum(oh, axis=1, keepdims=True)

    first_tile_ref[...] = jnp.broadcast_to(start * (1.0 / MOE_TM), first_tile_ref.shape).astype(I32)
    n_tiles_ref[...] = jnp.broadcast_to(ptiles, n_tiles_ref.shape).astype(I32)


def _positions(eid):
    T = eid.shape[1]
    r = np.arange(RANK_CHUNK)
    tri = jnp.asarray(r[:, None] <= r[None, :], BF16)
    e = np.arange(N_EXPERTS)
    low = jnp.asarray(e[None, :] < e[:, None], BF16)
    full = lambda a: pl.BlockSpec(a.shape, lambda i: (0,) * a.ndim)
    return pl.pallas_call(
        functools.partial(_positions_kernel, n_tok=T),
        grid=(1,),
        in_specs=[full(eid), full(tri), full(low)],
        out_specs=[pl.BlockSpec((8, T), lambda i: (0, 0)),
                   pl.BlockSpec((N_EXPERTS, META_LANES), lambda i: (0, 0)),
                   pl.BlockSpec((N_EXPERTS, META_LANES), lambda i: (0, 0))],
        out_shape=[jax.ShapeDtypeStruct((8, T), I32), jax.ShapeDtypeStruct((N_EXPERTS, META_LANES), I32),
                   jax.ShapeDtypeStruct((N_EXPERTS, META_LANES), I32)],
        name="route_positions",
    )(eid, tri, low)


def _sc_workers():
    info = plsc.get_sparse_core_info()
    return info.num_cores, info.num_cores * info.num_subcores


def _sc_scatter_rows(xp, pos2d, n_out):
    T, W = xp.shape
    n_cores, n_workers = _sc_workers()
    cpw = T // SC_CHUNK // n_workers
    mesh = plsc.VectorSubcoreMesh(core_axis_name="c", subcore_axis_name="s")

    @functools.partial(
        pl.kernel, mesh=mesh, out_type=jax.ShapeDtypeStruct((n_out, W), I32),
        scratch_types=[pltpu.VMEM((TOP_K * cpw, SC_CHUNK), I32), pltpu.VMEM((2, SC_CHUNK, W), I32),
                       pltpu.SemaphoreType.DMA((2,)), pltpu.SemaphoreType.DMA((2,))],
        name="sc_scatter_rows")
    def k(x_hbm, pos_hbm, out_hbm, idx_v, rows_v, rd_sem, wr_sem):
        wid = lax.axis_index("s") * n_cores + lax.axis_index("c")
        for s in range(TOP_K):
            pltpu.sync_copy(pos_hbm.at[pl.ds(s * (T // SC_CHUNK) + wid * cpw, cpw)],
                            idx_v.at[pl.ds(s * cpw, cpw)])

        def read(j):
            return pltpu.make_async_copy(x_hbm.at[pl.ds((wid * cpw + j) * SC_CHUNK, SC_CHUNK)],
                                         rows_v.at[j % 2], rd_sem.at[j % 2])

        def write(j, s):
            return pltpu.make_async_copy(rows_v.at[j % 2], out_hbm.at[idx_v.at[s * cpw + j]], wr_sem.at[j % 2])

        _sc_two_buffer_stream(cpw, read, lambda j: [write(j, s) for s in range(TOP_K)])

    return k(xp, pos2d)


def _sc_two_buffer_stream(n, read, writes):
    read(0).start()
    for j in range(n):
        read(j).wait()
        if j + 1 < n:
            if j >= 1:
                for w in writes(j - 1):
                    w.wait()
            read(j + 1).start()
        for w in writes(j):
            w.start()
    for j in range(max(n - 2, 0), n):
        for w in writes(j):
            w.wait()


def _sc_gather_rows(y, idx2d):
    W = y.shape[1]
    n = idx2d.shape[0] * SC_CHUNK
    n_cores, n_workers = _sc_workers()
    cpw = n // SC_CHUNK // n_workers
    mesh = plsc.VectorSubcoreMesh(core_axis_name="c", subcore_axis_name="s")

    @functools.partial(
        pl.kernel, mesh=mesh, out_type=jax.ShapeDtypeStruct((n, W), I32),
        scratch_types=[pltpu.VMEM((cpw, SC_CHUNK), I32), pltpu.VMEM((2, SC_CHUNK, W), I32),
                       pltpu.SemaphoreType.DMA((2,)), pltpu.SemaphoreType.DMA((2,))],
        name="sc_gather_rows")
    def k(y_hbm, idx_hbm, out_hbm, idx_v, rows_v, rd_sem, wr_sem):
        wid = lax.axis_index("s") * n_cores + lax.axis_index("c")
        pltpu.sync_copy(idx_hbm.at[pl.ds(wid * cpw, cpw)], idx_v)

        def read(j):
            return pltpu.make_async_copy(y_hbm.at[idx_v.at[j]], rows_v.at[j % 2], rd_sem.at[j % 2])

        def write(j):
            return pltpu.make_async_copy(rows_v.at[j % 2], out_hbm.at[pl.ds((wid * cpw + j) * SC_CHUNK, SC_CHUNK)],
                                         wr_sem.at[j % 2])

        _sc_two_buffer_stream(cpw, read, lambda j: [write(j)])

    return k(y, idx2d)


def _routed_kernel(first_ref, count_ref, xs_hbm, wg_ref, wu_ref, wd_ref, ys_hbm, xbuf, ybuf, in_sem, out_sem):
    e = pl.program_id(0)
    last = pl.num_programs(0) - 1
    total = first_ref[last] + count_ref[last]

    def in_copy(g):
        slot = lax.rem(g, RING)
        return pltpu.make_async_copy(xs_hbm.at[pl.ds(g * MOE_TM, MOE_TM)], xbuf.at[slot], in_sem.at[slot])

    def out_copy(g):
        slot = lax.rem(g, RING)
        return pltpu.make_async_copy(ybuf.at[slot], ys_hbm.at[pl.ds(g * MOE_TM, MOE_TM)], out_sem.at[slot])

    @pl.when(e == 0)
    def _():
        for g0 in range(RING - 1):
            @pl.when(g0 < total)
            def _():
                in_copy(g0).start()

    wg = wg_ref[0].astype(BF16)
    wu = wu_ref[0].astype(BF16)
    wd = wd_ref[0].astype(BF16)

    def tile_step(i, carry):
        g = first_ref[e] + i
        slot = lax.rem(g, RING)
        in_copy(g).wait()

        @pl.when(g + RING - 1 < total)
        def _():
            in_copy(g + RING - 1).start()

        @pl.when(g >= RING)
        def _():
            out_copy(g - RING).wait()

        lo, hi = _unpack_halves(xbuf[slot])
        lo = lo.astype(BF16)
        hi = hi.astype(BF16)
        a = _dot(lo, wg[:PACK_W]) + _dot(hi, wg[PACK_W:])
        b = _dot(lo, wu[:PACK_W]) + _dot(hi, wu[PACK_W:])
        act = (a * _sigmoid(a) * b).astype(BF16)
        ybuf[slot] = _pack_halves(_dot(act, wd))
        out_copy(g).start()
        return carry

    lax.fori_loop(0, count_ref[e], tile_step, 0)

    @pl.when(e == last)
    def _():
        for back in range(RING, 0, -1):
            @pl.when(total >= back)
            def _():
                out_copy(total - back).wait()


def _routed_mlp(first_tile, n_tiles, xs, wg, wu, wd):
    R = xs.shape[0]
    any_space = pl.BlockSpec(memory_space=pl.ANY)
    return pl.pallas_call(
        _routed_kernel,
        grid_spec=pltpu.PrefetchScalarGridSpec(
            num_scalar_prefetch=2,
            grid=(N_EXPERTS,),
            in_specs=[any_space,
                      pl.BlockSpec((1, D_MODEL, D_EXPERT), lambda e, ft, nt: (e, 0, 0)),
                      pl.BlockSpec((1, D_MODEL, D_EXPERT), lambda e, ft, nt: (e, 0, 0)),
                      pl.BlockSpec((1, D_EXPERT, D_MODEL), lambda e, ft, nt: (e, 0, 0))],
            out_specs=any_space,
            scratch_shapes=[pltpu.VMEM((RING, MOE_TM, PACK_W), I32), pltpu.VMEM((RING, MOE_TM, PACK_W), I32),
                            pltpu.SemaphoreType.DMA((RING,)), pltpu.SemaphoreType.DMA((RING,))]),
        out_shape=jax.ShapeDtypeStruct((R, PACK_W), I32),
        compiler_params=pltpu.CompilerParams(dimension_semantics=("arbitrary",)),
        name="routed_mlp",
    )(first_tile, n_tiles, xs, wg, wu, wd)


def _combine_kernel(xp_ref, y0_ref, y1_ref, wtok_ref, lnw_ref, lnb_ref, *out_refs):
    o_ref = out_refs[-1]
    w0 = wtok_ref[:, 0:1]
    w1 = wtok_ref[:, 1:2]
    xlo, xhi = _unpack_halves(xp_ref[...])
    y0lo, y0hi = _unpack_halves(y0_ref[...])
    y1lo, y1hi = _unpack_halves(y1_ref[...])
    h = jnp.concatenate([ALPHA * xlo + (w0 * y0lo + w1 * y1lo), ALPHA * xhi + (w0 * y0hi + w1 * y1hi)], axis=1)
    o_ref[...] = _layer_norm(h, lnw_ref[...], lnb_ref[...])


def _combine_ln2(xp, yg_parts, wtok, lnw, lnb, tile=1024):
    T = xp.shape[0]
    n_parts = len(yg_parts)
    nt = T // tile // n_parts
    full = lambda a: pl.BlockSpec(a.shape, lambda i: (0,) * a.ndim)
    out = None
    for p, yg in enumerate(yg_parts):
        rows = lambda w, p=p: pl.BlockSpec((tile, w), lambda i: (i + p * nt, 0))
        in_specs = [rows(PACK_W),
                    pl.BlockSpec((tile, PACK_W), lambda i: (i, 0)),
                    pl.BlockSpec((tile, PACK_W), lambda i: (i + nt, 0)),
                    rows(WTOK_LANES), full(lnw), full(lnb)]
        args = [xp, yg, yg, wtok, lnw, lnb]
        aliases = {}
        if out is not None:
            in_specs.append(pl.BlockSpec(memory_space=pl.ANY))
            args.append(out)
            aliases = {len(args) - 1: 0}
        out = pl.pallas_call(
            _combine_kernel,
            grid=(nt,),
            in_specs=in_specs,
            out_specs=rows(D_MODEL),
            out_shape=jax.ShapeDtypeStruct((T, D_MODEL), F32),
            input_output_aliases=aliases,
            compiler_params=pltpu.CompilerParams(dimension_semantics=("arbitrary",), vmem_limit_bytes=VMEM_LIMIT),
            name="combine_ln2",
        )(*args)
    return out


def _router_params(w_grp, b_grp, w_exp, b_exp):
    rw = jnp.zeros((D_MODEL, ROUTER_ROWS), F32)
    rw = rw.at[:, 0:N_GROUPS].set(w_grp).at[:, EXP_ROW0:EXP_ROW0 + N_EXPERTS].set(w_exp)
    rb = jnp.zeros((ROUTER_ROWS,), F32).at[N_GROUPS:8].set(NEG_BIG)
    rb = rb.at[0:N_GROUPS].set(b_grp).at[EXP_ROW0:EXP_ROW0 + N_EXPERTS].set(b_exp)
    rw_hi = rw.astype(BF16)
    rw_lo = (rw - rw_hi.astype(F32)).astype(BF16)
    return jnp.concatenate([rw_hi, rw_lo], axis=1), rb[:, None]


def kernel(x, mem, positions, w_in, w_pool_grp, pool_scale, ret_gn_w, w_mem_kv, w_br_pool, w_br_ret, w_br_xa,
           w_out, ln1_w, ln1_b, w_grp_router, b_grp_router, w_exp_router, b_exp_router, w_exp_gate, w_exp_up,
           w_exp_down, ln2_w, ln2_b):
    B, S, D = x.shape
    assert D == D_MODEL and w_in.shape[0] == DEPTH and S % 512 == 0
    T = B * S
    M = mem.shape[1]
    l = 0
    xf = x.reshape(T, D)

    rope = _rope_table(positions.reshape(1, T))
    kv = _mem_kv(mem.reshape(B * M, D), w_mem_kv[l].astype(BF16))
    rw, rb = _router_params(w_grp_router[l], b_grp_router[l], w_exp_router[l], b_exp_router[l])
    xp, eid, wtok = _mixer(xf, rope, kv, w_in[l].astype(BF16),
                           w_pool_grp[l].astype(BF16), pool_scale[l][None, :], ret_gn_w[l].reshape(1, -1),
                           w_br_pool[l].astype(BF16), w_br_ret[l].astype(BF16), w_br_xa[l].astype(BF16),
                           w_out[l].astype(BF16), ln1_w[l][None, :], ln1_b[l][None, :], rw, rb, B, S, M)

    pos, first_tile, n_tiles = _positions(eid)
    pos2d = pos[0:TOP_K].reshape(TOP_K * T // SC_CHUNK, SC_CHUNK)
    max_tiles = (TOP_K * T + N_EXPERTS * (MOE_TM - 1)) // MOE_TM
    xs = _sc_scatter_rows(xp, pos2d, max_tiles * MOE_TM)
    ys = _routed_mlp(first_tile[:, 0], n_tiles[:, 0], xs,
                     w_exp_gate[l].reshape(N_EXPERTS, D_MODEL, D_EXPERT),
                     w_exp_up[l].reshape(N_EXPERTS, D_MODEL, D_EXPERT),
                     w_exp_down[l].reshape(N_EXPERTS, D_EXPERT, D_MODEL))
    rng = T // COMBINE_PARTS
    yg_parts = [_sc_gather_rows(ys, pos[0:TOP_K, p * rng:(p + 1) * rng].reshape(TOP_K * rng // SC_CHUNK, SC_CHUNK))
                for p in range(COMBINE_PARTS)]
    out = _combine_ln2(xp, yg_parts, wtok, ln2_w[l][None, :], ln2_b[l][None, :])
    return out.reshape(B, S, D)
```

```python
import functools

import numpy as np
import jax
import jax.numpy as jnp
from jax import lax
from jax.experimental import pallas as pl
from jax.experimental.pallas import tpu as pltpu
from jax.experimental.pallas import tpu_sc as plsc

F32 = jnp.float32
BF16 = jnp.bfloat16
I32 = jnp.int32
U32 = jnp.uint32

D_MODEL = 1024
POOL_WINDOWS = (2, 4, 8, 16)
POOL_GROUP_DIM = 128
POOL_WIDTH = 512
POOL_HALO = 16
RET_HEADS = 4
RET_QK_DIM = 128
RET_V_DIM = 256
RET_CHUNK = 128
ROPE_BASE = 10000.0
XA_HEADS = 4
XA_HEAD_DIM = 128
XA_WIDTH = 512
N_GROUPS = 4
EXPERTS_PER_GROUP = 8
N_EXPERTS = N_GROUPS * EXPERTS_PER_GROUP
D_EXPERT = 256
LN_EPS = 1e-5
DEPTH = 1
ALPHA = (2.0 * DEPTH) ** 0.25
NEG_BIG = -1e30

COL_POOL, COL_Q, COL_K, COL_V, COL_G, COL_XAQ, COL_GATES = 0, 512, 1024, 1536, 2560, 3584, 4096

VMEM_LIMIT = 56 * 1024 * 1024

TOP_K = 2
PACK_W = D_MODEL // 2
MOE_TM = 512
SC_CHUNK = 64
RANK_CHUNK = 512
COMBINE_PARTS = 2
W_STAGE = 512
RING = 4
STRIP = 256
LN_ROWS = 32


def _dot(a, b):
    return jnp.dot(a, b, preferred_element_type=F32)


def _dot_nt(a, b, precision=None):
    return lax.dot_general(a, b, (((1,), (1,)), ((), ())), preferred_element_type=F32, precision=precision)


def _sigmoid(z):
    return 1.0 / (1.0 + jnp.exp2(z * (-1.0 / np.log(2.0))))


def _layer_norm(h, w, b):
    mu = jnp.mean(h, axis=-1, keepdims=True)
    hc = h - mu
    var = jnp.mean(hc * hc, axis=-1, keepdims=True)
    return hc * lax.rsqrt(var + LN_EPS) * w + b


def _rope_kernel(pos_ref, freq_ref, cos_ref, sin_ref, cos_t_ref, sin_t_ref):
    ang = freq_ref[...] * pos_ref[...].astype(F32)
    cos_t = jnp.cos(ang)
    sin_t = jnp.sin(ang)
    cos_t_ref[...] = cos_t
    sin_t_ref[...] = sin_t
    cos_ref[...] = jnp.transpose(jnp.concatenate([cos_t, cos_t], axis=0))
    sin_ref[...] = jnp.transpose(jnp.concatenate([-sin_t, sin_t], axis=0))


def _rope_table(pos_row, tile=2048):
    T = pos_row.shape[1]
    half = RET_QK_DIM // 2
    inv_freq = (ROPE_BASE ** (-np.arange(half, dtype=np.float64) / half)).astype(np.float32)
    freq = jnp.asarray(inv_freq[:, None])
    out = pl.BlockSpec((tile, RET_QK_DIM), lambda i: (i, 0))
    out_t = pl.BlockSpec((half, tile), lambda i: (0, i))
    return pl.pallas_call(
        _rope_kernel,
        grid=(T // tile,),
        in_specs=[pl.BlockSpec((1, tile), lambda i: (0, i)), pl.BlockSpec((half, 1), lambda i: (0, 0))],
        out_specs=[out, out, out_t, out_t],
        out_shape=[jax.ShapeDtypeStruct((T, RET_QK_DIM), F32)] * 2 + [jax.ShapeDtypeStruct((half, T), F32)] * 2,
        name="rope_table",
    )(pos_row, freq)


POOL_SUB = 256


def _pool_bands():
    r = np.arange(POOL_SUB)[:, None]
    c = np.arange(POOL_SUB)[None, :]
    ch = np.arange(POOL_HALO)[None, :] - POOL_HALO
    main = np.stack([((r - c >= 0) & (r - c < w)) for w in POOL_WINDOWS]).astype(np.float32)
    halo = np.stack([((r - ch >= 0) & (r - ch < w)) for w in POOL_WINDOWS]).astype(np.float32)
    return jnp.asarray(main, BF16), jnp.asarray(halo, BF16)


def _pool_branch(ub, j, halo_ref, bmain_ref, bhalo_ref, wg_ref, scale_ref, o_ref, filler):
    tile = ub.shape[0]
    s0 = j * tile
    slot = lax.rem(j, 2)
    blocks = [(sb * POOL_SUB, g) for sb in range(tile // POOL_SUB) for g in range(len(POOL_WINDOWS))]
    wsum = {}
    for r0, g in blocks:
        cols = slice(g * POOL_GROUP_DIM, (g + 1) * POOL_GROUP_DIM)
        prev = halo_ref[slot] if r0 == 0 else ub[r0 - POOL_HALO:r0]
        wsum[r0, g] = _dot(bmain_ref[g], ub[r0:r0 + POOL_SUB, cols]) + _dot(bhalo_ref[g], prev[:, cols])
    filler()
    for r0, g in blocks:
        cols = slice(g * POOL_GROUP_DIM, (g + 1) * POOL_GROUP_DIM)
        pos = s0 + r0 + lax.broadcasted_iota(I32, (POOL_SUB, POOL_GROUP_DIM), 0)
        cnt = jnp.minimum(pos + 1, POOL_WINDOWS[g]).astype(F32)
        pooled = wsum[r0, g] / cnt - ub[r0:r0 + POOL_SUB, cols].astype(F32)
        mixed = _dot(pooled.astype(BF16), wg_ref[g]) * scale_ref[:, cols]
        o_ref[r0:r0 + POOL_SUB, cols] = mixed.astype(BF16)
    halo_ref[1 - slot] = ub[tile - POOL_HALO:tile]


def _ret_consts():
    h = np.arange(RET_HEADS, dtype=np.float64)
    log_gamma = np.log1p(-np.exp2(-5.0 - h))
    pos = np.arange(RET_CHUNK, dtype=np.float64)
    diff = pos[:, None] - pos[None, :]
    kscale = RET_QK_DIM ** -0.5
    dmask = kscale * np.where(diff >= 0, np.exp(log_gamma[:, None, None] * np.maximum(diff, 0.0)), 0.0)
    qdec = np.exp(log_gamma[:, None] * (pos + 1.0)[None, :])
    kdec = kscale * np.exp(log_gamma[:, None] * (RET_CHUNK - 1.0 - pos)[None, :])
    cdec = np.exp(log_gamma * RET_CHUNK)
    lanes = lambda a: np.broadcast_to(a[:, :, None], (RET_HEADS, RET_CHUNK, RET_QK_DIM))
    kdec_t = np.broadcast_to(kdec[:, None, :], (RET_HEADS, RET_QK_DIM, RET_CHUNK))
    return (jnp.asarray(dmask, F32), jnp.asarray(lanes(qdec), F32), jnp.asarray(kdec_t, F32),
            tuple(float(v) for v in cdec))


def _retention_branch(q, k_t, v, silu_g, cos_ref, sin_ref, cos_t_ref, sin_t_ref, dmask_ref, qdec_ref, kdec_ref,
                      gnw_ref, state_ref, rq_ref, rqd_ref, rkt_ref, rkdt_ref, o_ref, cdec, fillers):
    tile = q.shape[0]
    n_chunks = tile // RET_CHUNK
    half = RET_QK_DIM // 2
    cos = cos_ref[...]
    sin = sin_ref[...]
    cos_t = cos_t_ref[...]
    sin_t = sin_t_ref[...]
    for h in range(RET_HEADS):
        qk = slice(h * RET_QK_DIM, (h + 1) * RET_QK_DIM)
        qh = q[:, qk]
        qr = qh * cos + pltpu.roll(qh, half, 1) * sin
        rq_ref[:, qk] = qr.astype(BF16)
        rqd_ref[:, qk] = (qr * jnp.concatenate([qdec_ref[h]] * n_chunks, axis=0)).astype(BF16)
        k1 = k_t[h * RET_QK_DIM:h * RET_QK_DIM + half]
        k2 = k_t[h * RET_QK_DIM + half:(h + 1) * RET_QK_DIM]
        kr_t = jnp.concatenate([k1 * cos_t - k2 * sin_t, k2 * cos_t + k1 * sin_t], axis=0)
        rkt_ref[qk, :] = kr_t.astype(BF16)
        rkdt_ref[qk, :] = (kr_t * jnp.concatenate([kdec_ref[h]] * n_chunks, axis=1)).astype(BF16)

    chunks = [slice(c * RET_CHUNK, (c + 1) * RET_CHUNK) for c in range(n_chunks)]
    heads = [slice(h * RET_QK_DIM, (h + 1) * RET_QK_DIM) for h in range(RET_HEADS)]
    v_heads = [slice(h * RET_V_DIM, (h + 1) * RET_V_DIM) for h in range(RET_HEADS)]
    raw = {(c, h): _dot(rq_ref[chunks[c], heads[h]], rkt_ref[heads[h], chunks[c]])
           for c in range(n_chunks) for h in range(RET_HEADS)}
    incr = {(c, h): _dot(rkdt_ref[heads[h], chunks[c]], v[chunks[c], v_heads[h]])
            for c in range(n_chunks) for h in range(RET_HEADS)}
    state_in = {}
    for h in range(RET_HEADS):
        st = state_ref[h]
        for c in range(n_chunks):
            state_in[c, h] = st.astype(BF16)
            st = cdec[h] * st + incr[c, h]
        state_ref[h] = st

    for c in range(n_chunks):
        rows = chunks[c]
        fillers[c]()
        for h in range(RET_HEADS):
            qk = heads[h]
            v_cols = v_heads[h]
            scores = raw[c, h] * dmask_ref[h]
            lhs = jnp.concatenate([scores.astype(BF16), rqd_ref[rows, qk]], axis=1)
            y = _dot(lhs, jnp.concatenate([v[rows, v_cols], state_in[c, h]], axis=0))
            mu = jnp.mean(y, axis=-1, keepdims=True)
            yc = y - mu
            var = jnp.mean(yc * yc, axis=-1, keepdims=True)
            yn = yc * lax.rsqrt(var + LN_EPS) * gnw_ref[:, v_cols]
            o_ref[rows, v_cols] = (silu_g[h][rows] * yn).astype(BF16)


def _memkv_kernel(m_ref, w_ref, o_ref):
    o_ref[...] = _dot(m_ref[...].astype(BF16), w_ref[...]).astype(BF16)


def _mem_kv(memf, w_b):
    M, D = memf.shape
    N = w_b.shape[1]
    return pl.pallas_call(
        _memkv_kernel,
        grid=(1,),
        in_specs=[pl.BlockSpec((M, D), lambda i: (0, 0)), pl.BlockSpec((D, N), lambda i: (0, 0))],
        out_specs=pl.BlockSpec((M, N), lambda i: (0, 0)),
        out_shape=jax.ShapeDtypeStruct((M, N), BF16),
        name="mem_kv",
    )(memf, w_b)


def _cross_attention_branch(xq, k_ref, v_ref, o_ref):
    scale = XA_HEAD_DIM ** -0.5
    for h in range(XA_HEADS):
        cols = slice(h * XA_HEAD_DIM, (h + 1) * XA_HEAD_DIM)
        s = _dot_nt(xq[:, cols], k_ref[:, cols]) * scale
        m = jnp.max(s, axis=-1, keepdims=True)
        p = jnp.exp(s - m)
        l = jnp.sum(p, axis=-1, keepdims=True)
        o = _dot(p.astype(BF16), v_ref[:, cols]) / l
        o_ref[:, cols] = o.astype(BF16)


ROUTER_ROWS = 128
WTOK_LANES = 128
EXP_ROW0 = 8


def _route(logits_t):
    gl = logits_t[0:8]
    gmax = jnp.max(gl, axis=0, keepdims=True)
    p_grp = 1.0 / jnp.sum(jnp.exp(gl - gmax), axis=0, keepdims=True)
    idx8 = lax.broadcasted_iota(jnp.int32, gl.shape, 0)
    gsel = jnp.min(jnp.where(gl == gmax, idx8, 8), axis=0, keepdims=True)
    cl = jnp.zeros_like(gl)
    for g in range(N_GROUPS):
        r0 = EXP_ROW0 + g * EXPERTS_PER_GROUP
        cl = cl + jnp.where(gsel == g, logits_t[r0:r0 + EXPERTS_PER_GROUP], 0.0)
    v1 = jnp.max(cl, axis=0, keepdims=True)
    i1 = jnp.min(jnp.where(cl == v1, idx8, 8), axis=0, keepdims=True)
    cl2 = jnp.where(idx8 == i1, -jnp.inf, cl)
    v2 = jnp.max(cl2, axis=0, keepdims=True)
    i2 = jnp.min(jnp.where(cl2 == v2, idx8, 8), axis=0, keepdims=True)
    e21 = jnp.exp(v2 - v1)
    w1 = p_grp / (1.0 + e21)
    w2 = p_grp * e21 / (1.0 + e21)
    return gsel * EXPERTS_PER_GROUP + i1, gsel * EXPERTS_PER_GROUP + i2, w1, w2


def _pack_halves(v):
    half = v.shape[1] // 2
    lo = lax.bitcast_convert_type(v[:, :half].astype(BF16).astype(F32), U32)
    hi = lax.bitcast_convert_type(v[:, half:].astype(BF16).astype(F32), U32)
    return lax.bitcast_convert_type(lax.shift_right_logical(lo, U32(16)) | hi, I32)


def _unpack_halves(w):
    u = lax.bitcast_convert_type(w, U32)
    lo = lax.bitcast_convert_type(lax.shift_left(u, U32(16)), F32)
    hi = lax.bitcast_convert_type(u & U32(0xFFFF0000), F32)
    return lo, hi


def _mixer_kernel(x_ref, cos_ref, sin_ref, cos_t_ref, sin_t_ref, km_ref, vm_ref, win_hbm, wgrp_ref,
                  pscale_ref, bmain_ref, bhalo_ref, dmask_ref, qdec_ref, kdec_ref, gnw_ref, wp_ref, wr_ref,
                  wa_ref, wo_ref, lnw_ref, lnb_ref, rw_ref, rb_ref, xp_ref, eid_ref, wtok_ref,
                  state_ref, halo_ref, ypool_ref, yret_ref, yxa_ref, rq_ref, rqd_ref, rkt_ref, rkdt_ref, wkt_ref,
                  win_ref, stage_ref, stage_sem, *, tile, cdec):
    j = pl.program_id(1)

    @pl.when(jnp.logical_and(pl.program_id(0) == 0, j == 0))
    def _():
        n_stage = win_ref.shape[1] // W_STAGE

        def fetch(i):
            return pltpu.make_async_copy(win_hbm.at[:, pl.ds(i * W_STAGE, W_STAGE)], stage_ref.at[i % 2],
                                         stage_sem.at[i % 2])

        fetch(0).start()
        for i in range(n_stage):
            fetch(i).wait()
            if i + 1 < n_stage:
                fetch(i + 1).start()
            win_ref[:, i * W_STAGE:(i + 1) * W_STAGE] = stage_ref[i % 2].astype(BF16)
        wk = win_ref[:, COL_K:COL_K + RET_HEADS * RET_QK_DIM].astype(F32)
        wkt_ref[...] = jnp.transpose(wk).astype(BF16)

    @pl.when(j == 0)
    def _():
        state_ref[...] = jnp.zeros_like(state_ref)
        halo_ref[...] = jnp.zeros_like(halo_ref)

    x = x_ref[...]
    xb = x.astype(BF16)

    def proj(col, width):
        return _dot(xb, win_ref[:, col:col + width])

    part = {}
    strips = [slice(c, c + STRIP) for c in range(0, D_MODEL, STRIP)]

    def gate(branch, cols):
        return _sigmoid(proj(COL_GATES + branch * D_MODEL + cols.start, STRIP))

    def pool_part():
        def pool_gates():
            part["pool_gate"] = [gate(0, c) for c in strips]

        _pool_branch(proj(COL_POOL, POOL_WIDTH).astype(BF16), j, halo_ref, bmain_ref, bhalo_ref, wgrp_ref,
                     pscale_ref, ypool_ref, pool_gates)
        part["pool"] = [part["pool_gate"][i] * _dot(ypool_ref[...], wp_ref[:, c]) for i, c in enumerate(strips)]

    def xa_part():
        _cross_attention_branch(proj(COL_XAQ, XA_WIDTH).astype(BF16), km_ref, vm_ref, yxa_ref)
        part["xa"] = [gate(2, c) * _dot(yxa_ref[...], wa_ref[:, c]) for c in strips]

    def ret_gate_part():
        part["ret_gate"] = [gate(1, c) for c in strips]

    silu_g = []
    for h in range(RET_HEADS):
        gh = proj(COL_G + h * RET_V_DIM, RET_V_DIM)
        silu_g.append(gh * _sigmoid(gh))
    fillers = [pool_part, xa_part, ret_gate_part] + [lambda: None] * (tile // RET_CHUNK - 3)
    _retention_branch(proj(COL_Q, RET_HEADS * RET_QK_DIM), _dot_nt(wkt_ref[...], xb),
                      proj(COL_V, RET_HEADS * RET_V_DIM).astype(BF16), silu_g, cos_ref, sin_ref, cos_t_ref,
                      sin_t_ref, dmask_ref, qdec_ref, kdec_ref, gnw_ref, state_ref, rq_ref, rqd_ref, rkt_ref,
                      rkdt_ref, yret_ref, cdec, fillers)
    merged = jnp.concatenate(
        [(part["pool"][i] + part["ret_gate"][i] * _dot(yret_ref[...], wr_ref[:, c]) + part["xa"][i]).astype(BF16)
         for i, c in enumerate(strips)], axis=1)
    h = jnp.concatenate([ALPHA * x[:, c] + _dot(merged, wo_ref[:, c]) for c in strips], axis=1)
    x1 = jnp.concatenate([_layer_norm(h[r:r + LN_ROWS], lnw_ref[...], lnb_ref[...])
                          for r in range(0, tile, LN_ROWS)], axis=0)
    xp_ref[...] = _pack_halves(x1)
    x1_hi = x1.astype(BF16)
    x1_lo = (x1 - x1_hi.astype(F32)).astype(BF16)
    p4 = _dot(jnp.concatenate([x1_hi, x1_lo], axis=0), rw_ref[...])
    logits = (p4[:tile, :ROUTER_ROWS] + p4[:tile, ROUTER_ROWS:]) + (p4[tile:, :ROUTER_ROWS] + p4[tile:, ROUTER_ROWS:])
    e0, e1, w0, w1 = _route(jnp.transpose(logits) + rb_ref[...])
    eid_ref[...] = jnp.concatenate([e0, e1, jnp.zeros((8 - TOP_K, tile), I32)], axis=0)
    w_t = jnp.concatenate([w0, w1, jnp.zeros((WTOK_LANES - TOP_K, tile), F32)], axis=0)
    wtok_ref[...] = jnp.transpose(w_t)


def _mixer(xf, rope, kv, win, wgrp, pscale, gnw, wp, wr, wa, wo, lnw, lnb, rw, rb, batch, seq, mem_len,
           tile=512):
    T = xf.shape[0]
    nj = seq // tile
    cos, sin, cos_t, sin_t = rope
    bmain, bhalo = _pool_bands()
    dmask, qdec, kdec, cdec = _ret_consts()
    resident = lambda a: pl.BlockSpec(a.shape, lambda b, j: (0,) * a.ndim, pipeline_mode=pl.Buffered(1))
    rowblk = lambda w: pl.BlockSpec((tile, w), lambda b, j: (b * nj + j, 0))
    colblk = lambda r: pl.BlockSpec((r, tile), lambda b, j: (0, b * nj + j))
    consts = (wgrp, pscale, bmain, bhalo, dmask, qdec, kdec, gnw, wp, wr, wa, wo, lnw, lnb, rw, rb)
    return pl.pallas_call(
        functools.partial(_mixer_kernel, tile=tile, cdec=cdec),
        grid=(batch, nj),
        in_specs=[rowblk(D_MODEL), rowblk(RET_QK_DIM), rowblk(RET_QK_DIM),
                  colblk(RET_QK_DIM // 2), colblk(RET_QK_DIM // 2),
                  pl.BlockSpec((mem_len, XA_WIDTH), lambda b, j: (b, 0)),
                  pl.BlockSpec((mem_len, XA_WIDTH), lambda b, j: (b, 1)),
                  pl.BlockSpec(memory_space=pl.ANY)] + [resident(a) for a in consts],
        out_specs=[rowblk(PACK_W), colblk(8), rowblk(WTOK_LANES)],
        out_shape=[jax.ShapeDtypeStruct((T, PACK_W), I32), jax.ShapeDtypeStruct((8, T), I32),
                   jax.ShapeDtypeStruct((T, WTOK_LANES), F32)],
        scratch_shapes=[pltpu.VMEM((RET_HEADS, RET_QK_DIM, RET_V_DIM), F32),
                        pltpu.VMEM((2, POOL_HALO, POOL_WIDTH), BF16),
                        pltpu.VMEM((tile, POOL_WIDTH), BF16),
                        pltpu.VMEM((tile, RET_HEADS * RET_V_DIM), BF16),
                        pltpu.VMEM((tile, XA_WIDTH), BF16),
                        pltpu.VMEM((tile, RET_HEADS * RET_QK_DIM), BF16),
                        pltpu.VMEM((tile, RET_HEADS * RET_QK_DIM), BF16),
                        pltpu.VMEM((RET_HEADS * RET_QK_DIM, tile), BF16),
                        pltpu.VMEM((RET_HEADS * RET_QK_DIM, tile), BF16),
                        pltpu.VMEM((RET_HEADS * RET_QK_DIM, D_MODEL), BF16),
                        pltpu.VMEM(win.shape, BF16),
                        pltpu.VMEM((2, win.shape[0], W_STAGE), F32),
                        pltpu.SemaphoreType.DMA((2,))],
        compiler_params=pltpu.CompilerParams(dimension_semantics=("arbitrary", "arbitrary"),
                                             vmem_limit_bytes=VMEM_LIMIT),
        name="mixer",
    )(xf, cos, sin, cos_t, sin_t, kv, kv, win, *consts)


META_LANES = 128


def _positions_kernel(eid_ref, tri_ref, low_ref, pos_ref, first_tile_ref, n_tiles_ref, *, n_tok):
    n_chunks = n_tok // RANK_CHUNK
    erow = lax.broadcasted_iota(I32, (N_EXPERTS, RANK_CHUNK), 0)

    def onehot(c):
        sl = slice(c * RANK_CHUNK, (c + 1) * RANK_CHUNK)
        m0 = eid_ref[0:1, sl] == erow
        m1 = eid_ref[1:2, sl] == erow
        return m0, m1, jnp.where(m0, 1.0, 0.0) + jnp.where(m1, 1.0, 0.0)

    counts = jnp.zeros((N_EXPERTS, 1), F32)
    for c in range(n_chunks):
        counts = counts + jnp.sum(onehot(c)[2], axis=1, keepdims=True)
    ptiles = jnp.floor((counts + (MOE_TM - 1)) * (1.0 / MOE_TM))
    ptiles_b = jnp.broadcast_to(ptiles, (N_EXPERTS, 128)).astype(BF16)
    start = _dot(low_ref[...], ptiles_b)[:, 0:1] * MOE_TM

    pos_ref[...] = jnp.zeros_like(pos_ref)
    carry = start - 1.0
    for c in range(n_chunks):
        sl = slice(c * RANK_CHUNK, (c + 1) * RANK_CHUNK)
        m0, m1, oh = onehot(c)
        rank = _dot(oh.astype(BF16), tri_ref[...]) + carry
        pos_ref[0:1, sl] = jnp.sum(jnp.where(m0, rank, 0.0), axis=0, keepdims=True).astype(I32)
        pos_ref[1:2, sl] = jnp.sum(jnp.where(m1, rank, 0.0), axis=0, keepdims=True).astype(I32)
        carry = carry + jnp.sum(oh, axis=1, keepdims=True)

    first_tile_ref[...] = jnp.broadcast_to(start * (1.0 / MOE_TM), first_tile_ref.shape).astype(I32)
    n_tiles_ref[...] = jnp.broadcast_to(ptiles, n_tiles_ref.shape).astype(I32)


def _positions(eid):
    T = eid.shape[1]
    r = np.arange(RANK_CHUNK)
    tri = jnp.asarray(r[:, None] <= r[None, :], BF16)
    e = np.arange(N_EXPERTS)
    low = jnp.asarray(e[None, :] < e[:, None], BF16)
    full = lambda a: pl.BlockSpec(a.shape, lambda i: (0,) * a.ndim)
    return pl.pallas_call(
        functools.partial(_positions_kernel, n_tok=T),
        grid=(1,),
        in_specs=[full(eid), full(tri), full(low)],
        out_specs=[pl.BlockSpec((8, T), lambda i: (0, 0)),
                   pl.BlockSpec((N_EXPERTS, META_LANES), lambda i: (0, 0)),
                   pl.BlockSpec((N_EXPERTS, META_LANES), lambda i: (0, 0))],
        out_shape=[jax.ShapeDtypeStruct((8, T), I32), jax.ShapeDtypeStruct((N_EXPERTS, META_LANES), I32),
                   jax.ShapeDtypeStruct((N_EXPERTS, META_LANES), I32)],
        name="route_positions",
    )(eid, tri, low)


def _sc_workers():
    info = plsc.get_sparse_core_info()
    return info.num_cores, info.num_cores * info.num_subcores


def _sc_scatter_rows(xp, pos2d, n_out):
    T, W = xp.shape
    n_cores, n_workers = _sc_workers()
    cpw = T // SC_CHUNK // n_workers
    mesh = plsc.VectorSubcoreMesh(core_axis_name="c", subcore_axis_name="s")

    @functools.partial(
        pl.kernel, mesh=mesh, out_type=jax.ShapeDtypeStruct((n_out, W), I32),
        scratch_types=[pltpu.VMEM((TOP_K * cpw, SC_CHUNK), I32), pltpu.VMEM((2, SC_CHUNK, W), I32),
                       pltpu.SemaphoreType.DMA((2,)), pltpu.SemaphoreType.DMA((2,))],
        name="sc_scatter_rows")
    def k(x_hbm, pos_hbm, out_hbm, idx_v, rows_v, rd_sem, wr_sem):
        wid = lax.axis_index("s") * n_cores + lax.axis_index("c")
        for s in range(TOP_K):
            pltpu.sync_copy(pos_hbm.at[pl.ds(s * (T // SC_CHUNK) + wid * cpw, cpw)],
                            idx_v.at[pl.ds(s * cpw, cpw)])

        def read(j):
            return pltpu.make_async_copy(x_hbm.at[pl.ds((wid * cpw + j) * SC_CHUNK, SC_CHUNK)],
                                         rows_v.at[j % 2], rd_sem.at[j % 2])

        def write(j, s):
            return pltpu.make_async_copy(rows_v.at[j % 2], out_hbm.at[idx_v.at[s * cpw + j]], wr_sem.at[j % 2])

        _sc_two_buffer_stream(cpw, read, lambda j: [write(j, s) for s in range(TOP_K)])

    return k(xp, pos2d)


def _sc_two_buffer_stream(n, read, writes):
    read(0).start()
    for j in range(n):
        read(j).wait()
        if j + 1 < n:
            if j >= 1:
                for w in writes(j - 1):
                    w.wait()
            read(j + 1).start()
        for w in writes(j):
            w.start()
    for j in range(max(n - 2, 0), n):
        for w in writes(j):
            w.wait()


def _sc_gather_rows(y, idx2d):
    W = y.shape[1]
    n = idx2d.shape[0] * SC_CHUNK
    n_cores, n_workers = _sc_workers()
    cpw = n // SC_CHUNK // n_workers
    mesh = plsc.VectorSubcoreMesh(core_axis_name="c", subcore_axis_name="s")

    @functools.partial(
        pl.kernel, mesh=mesh, out_type=jax.ShapeDtypeStruct((n, W), I32),
        scratch_types=[pltpu.VMEM((cpw, SC_CHUNK), I32), pltpu.VMEM((2, SC_CHUNK, W), I32),
                       pltpu.SemaphoreType.DMA((2,)), pltpu.SemaphoreType.DMA((2,))],
        name="sc_gather_rows")
    def k(y_hbm, idx_hbm, out_hbm, idx_v, rows_v, rd_sem, wr_sem):
        wid = lax.axis_index("s") * n_cores + lax.axis_index("c")
        pltpu.sync_copy(idx_hbm.at[pl.ds(wid * cpw, cpw)], idx_v)

        def read(j):
            return pltpu.make_async_copy(y_hbm.at[idx_v.at[j]], rows_v.at[j % 2], rd_sem.at[j % 2])

        def write(j):
            return pltpu.make_async_copy(rows_v.at[j % 2], out_hbm.at[pl.ds((wid * cpw + j) * SC_CHUNK, SC_CHUNK)],
                                         wr_sem.at[j % 2])

        _sc_two_buffer_stream(cpw, read, lambda j: [write(j)])

    return k(y, idx2d)


def _routed_kernel(first_ref, count_ref, xs_hbm, wg_ref, wu_ref, wd_ref, ys_hbm, xbuf, ybuf, in_sem, out_sem):
    e = pl.program_id(0)
    last = pl.num_programs(0) - 1
    total = first_ref[last] + count_ref[last]

    def in_copy(g):
        slot = lax.rem(g, RING)
        return pltpu.make_async_copy(xs_hbm.at[pl.ds(g * MOE_TM, MOE_TM)], xbuf.at[slot], in_sem.at[slot])

    def out_copy(g):
        slot = lax.rem(g, RING)
        return pltpu.make_async_copy(ybuf.at[slot], ys_hbm.at[pl.ds(g * MOE_TM, MOE_TM)], out_sem.at[slot])

    @pl.when(e == 0)
    def _():
        for g0 in range(RING - 1):
            @pl.when(g0 < total)
            def _():
                in_copy(g0).start()

    wg = wg_ref[0].astype(BF16)
    wu = wu_ref[0].astype(BF16)
    wd = wd_ref[0].astype(BF16)

    def tile_step(i, carry):
        g = first_ref[e] + i
        slot = lax.rem(g, RING)
        in_copy(g).wait()

        @pl.when(g + RING - 1 < total)
        def _():
            in_copy(g + RING - 1).start()

        @pl.when(g >= RING)
        def _():
            out_copy(g - RING).wait()

        lo, hi = _unpack_halves(xbuf[slot])
        lo = lo.astype(BF16)
        hi = hi.astype(BF16)
        a = _dot(lo, wg[:PACK_W]) + _dot(hi, wg[PACK_W:])
        b = _dot(lo, wu[:PACK_W]) + _dot(hi, wu[PACK_W:])
        act = (a * _sigmoid(a) * b).astype(BF16)
        ybuf[slot] = _pack_halves(_dot(act, wd))
        out_copy(g).start()
        return carry

    lax.fori_loop(0, count_ref[e], tile_step, 0)

    @pl.when(e == last)
    def _():
        for back in range(RING, 0, -1):
            @pl.when(total >= back)
            def _():
                out_copy(total - back).wait()


def _routed_mlp(first_tile, n_tiles, xs, wg, wu, wd):
    R = xs.shape[0]
    any_space = pl.BlockSpec(memory_space=pl.ANY)
    return pl.pallas_call(
        _routed_kernel,
        grid_spec=pltpu.PrefetchScalarGridSpec(
            num_scalar_prefetch=2,
            grid=(N_EXPERTS,),
            in_specs=[any_space,
                      pl.BlockSpec((1, D_MODEL, D_EXPERT), lambda e, ft, nt: (e, 0, 0)),
                      pl.BlockSpec((1, D_MODEL, D_EXPERT), lambda e, ft, nt: (e, 0, 0)),
                      pl.BlockSpec((1, D_EXPERT, D_MODEL), lambda e, ft, nt: (e, 0, 0))],
            out_specs=any_space,
            scratch_shapes=[pltpu.VMEM((RING, MOE_TM, PACK_W), I32), pltpu.VMEM((RING, MOE_TM, PACK_W), I32),
                            pltpu.SemaphoreType.DMA((RING,)), pltpu.SemaphoreType.DMA((RING,))]),
        out_shape=jax.ShapeDtypeStruct((R, PACK_W), I32),
        compiler_params=pltpu.CompilerParams(dimension_semantics=("arbitrary",)),
        name="routed_mlp",
    )(first_tile, n_tiles, xs, wg, wu, wd)


def _combine_kernel(xp_ref, y0_ref, y1_ref, wtok_ref, lnw_ref, lnb_ref, *out_refs):
    o_ref = out_refs[-1]
    w0 = wtok_ref[:, 0:1]
    w1 = wtok_ref[:, 1:2]
    xlo, xhi = _unpack_halves(xp_ref[...])
    y0lo, y0hi = _unpack_halves(y0_ref[...])
    y1lo, y1hi = _unpack_halves(y1_ref[...])
    h = jnp.concatenate([ALPHA * xlo + (w0 * y0lo + w1 * y1lo), ALPHA * xhi + (w0 * y0hi + w1 * y1hi)], axis=1)
    o_ref[...] = _layer_norm(h, lnw_ref[...], lnb_ref[...])


def _combine_ln2(xp, yg_parts, wtok, lnw, lnb, tile=1024):
    T = xp.shape[0]
    n_parts = len(yg_parts)
    nt = T // tile // n_parts
    full = lambda a: pl.BlockSpec(a.shape, lambda i: (0,) * a.ndim)
    out = None
    for p, yg in enumerate(yg_parts):
        rows = lambda w, p=p: pl.BlockSpec((tile, w), lambda i: (i + p * nt, 0))
        in_specs = [rows(PACK_W),
                    pl.BlockSpec((tile, PACK_W), lambda i: (i, 0)),
                    pl.BlockSpec((tile, PACK_W), lambda i: (i + nt, 0)),
                    rows(WTOK_LANES), full(lnw), full(lnb)]
        args = [xp, yg, yg, wtok, lnw, lnb]
        aliases = {}
        if out is not None:
            in_specs.append(pl.BlockSpec(memory_space=pl.ANY))
            args.append(out)
            aliases = {len(args) - 1: 0}
        out = pl.pallas_call(
            _combine_kernel,
            grid=(nt,),
            in_specs=in_specs,
            out_specs=rows(D_MODEL),
            out_shape=jax.ShapeDtypeStruct((T, D_MODEL), F32),
            input_output_aliases=aliases,
            compiler_params=pltpu.CompilerParams(dimension_semantics=("arbitrary",), vmem_limit_bytes=VMEM_LIMIT),
            name="combine_ln2",
        )(*args)
    return out


def _router_params(w_grp, b_grp, w_exp, b_exp):
    rw = jnp.zeros((D_MODEL, ROUTER_ROWS), F32)
    rw = rw.at[:, 0:N_GROUPS].set(w_grp).at[:, EXP_ROW0:EXP_ROW0 + N_EXPERTS].set(w_exp)
    rb = jnp.zeros((ROUTER_ROWS,), F32).at[N_GROUPS:8].set(NEG_BIG)
    rb = rb.at[0:N_GROUPS].set(b_grp).at[EXP_ROW0:EXP_ROW0 + N_EXPERTS].set(b_exp)
    rw_hi = rw.astype(BF16)
    rw_lo = (rw - rw_hi.astype(F32)).astype(BF16)
    return jnp.concatenate([rw_hi, rw_lo], axis=1), rb[:, None]


def kernel(x, mem, positions, w_in, w_pool_grp, pool_scale, ret_gn_w, w_mem_kv, w_br_pool, w_br_ret, w_br_xa,
           w_out, ln1_w, ln1_b, w_grp_router, b_grp_router, w_exp_router, b_exp_router, w_exp_gate, w_exp_up,
           w_exp_down, ln2_w, ln2_b):
    B, S, D = x.shape
    assert D == D_MODEL and w_in.shape[0] == DEPTH and S % 512 == 0
    T = B * S
    M = mem.shape[1]
    l = 0
    xf = x.reshape(T, D)

    rope = _rope_table(positions.reshape(1, T))
    kv = _mem_kv(mem.reshape(B * M, D), w_mem_kv[l].astype(BF16))
    rw, rb = _router_params(w_grp_router[l], b_grp_router[l], w_exp_router[l], b_exp_router[l])
    xp, eid, wtok = _mixer(xf, rope, kv, w_in[l],
                           w_pool_grp[l].astype(BF16), pool_scale[l][None, :], ret_gn_w[l].reshape(1, -1),
                           w_br_pool[l].astype(BF16), w_br_ret[l].astype(BF16), w_br_xa[l].astype(BF16),
                           w_out[l].astype(BF16), ln1_w[l][None, :], ln1_b[l][None, :], rw, rb, B, S, M)

    pos, first_tile, n_tiles = _positions(eid)
    pos2d = pos[0:TOP_K].reshape(TOP_K * T // SC_CHUNK, SC_CHUNK)
    max_tiles = (TOP_K * T + N_EXPERTS * (MOE_TM - 1)) // MOE_TM
    xs = _sc_scatter_rows(xp, pos2d, max_tiles * MOE_TM)
    ys = _routed_mlp(first_tile[:, 0], n_tiles[:, 0], xs,
                     w_exp_gate[l].reshape(N_EXPERTS, D_MODEL, D_EXPERT),
                     w_exp_up[l].reshape(N_EXPERTS, D_MODEL, D_EXPERT),
                     w_exp_down[l].reshape(N_EXPERTS, D_EXPERT, D_MODEL))
    rng = T // COMBINE_PARTS
    yg_parts = [_sc_gather_rows(ys, pos[0:TOP_K, p * rng:(p + 1) * rng].reshape(TOP_K * rng // SC_CHUNK, SC_CHUNK))
                for p in range(COMBINE_PARTS)]
    out = _combine_ln2(xp, yg_parts, wtok, ln2_w[l][None, :], ln2_b[l][None, :])
    return out.reshape(B, S, D)
```

```python
import functools

import numpy as np
import jax
import jax.numpy as jnp
from jax import lax
from jax.experimental import pallas as pl
from jax.experimental.pallas import tpu as pltpu
from jax.experimental.pallas import tpu_sc as plsc

F32 = jnp.float32
BF16 = jnp.bfloat16
I32 = jnp.int32
U32 = jnp.uint32

D_MODEL = 1024
POOL_WINDOWS = (2, 4, 8, 16)
POOL_GROUP_DIM = 128
POOL_WIDTH = 512
POOL_HALO = 16
RET_HEADS = 4
RET_QK_DIM = 128
RET_V_DIM = 256
RET_CHUNK = 128
ROPE_BASE = 10000.0
XA_HEADS = 4
XA_HEAD_DIM = 128
XA_WIDTH = 512
N_GROUPS = 4
EXPERTS_PER_GROUP = 8
N_EXPERTS = N_GROUPS * EXPERTS_PER_GROUP
D_EXPERT = 256
LN_EPS = 1e-5
DEPTH = 1
ALPHA = (2.0 * DEPTH) ** 0.25
NEG_BIG = -1e30

COL_POOL, COL_Q, COL_K, COL_V, COL_G, COL_XAQ, COL_GATES = 0, 512, 1024, 1536, 2560, 3584, 4096

V7X_VMEM_BYTES = 64 * 1024 * 1024
VMEM_LIMIT = V7X_VMEM_BYTES * 7 // 8
SUBLANES = 8
LANES = 128

TOP_K = 2
PACK_W = D_MODEL // 2
MOE_TM = 512
SC_CHUNK = 64
RANK_CHUNK = 512
COMBINE_PARTS = 2
RING = 4
STRIP = 256
LN_ROWS = 32


def _dot(a, b):
    return jnp.dot(a, b, preferred_element_type=F32)


def _dot_nt(a, b):
    return lax.dot_general(a, b, (((1,), (1,)), ((), ())), preferred_element_type=F32)


def _sigmoid(z):
    return 1.0 / (1.0 + jnp.exp2(z * (-1.0 / np.log(2.0))))


def _layer_norm(h, w, b):
    mu = jnp.mean(h, axis=-1, keepdims=True)
    hc = h - mu
    var = jnp.mean(hc * hc, axis=-1, keepdims=True)
    return hc * lax.rsqrt(var + LN_EPS) * w + b


def _rope_kernel(pos_ref, freq_ref, cos_ref, sin_ref, cos_t_ref, sin_t_ref):
    ang = freq_ref[...] * pos_ref[...].astype(F32)
    cos_t = jnp.cos(ang)
    sin_t = jnp.sin(ang)
    cos_t_ref[...] = cos_t
    sin_t_ref[...] = sin_t
    cos_ref[...] = jnp.transpose(jnp.concatenate([cos_t, cos_t], axis=0))
    sin_ref[...] = jnp.transpose(jnp.concatenate([-sin_t, sin_t], axis=0))


def _rope_table(pos_row, tile=2048):
    T = pos_row.shape[1]
    half = RET_QK_DIM // 2
    inv_freq = (ROPE_BASE ** (-np.arange(half, dtype=np.float64) / half)).astype(np.float32)
    freq = jnp.asarray(inv_freq[:, None])
    out = pl.BlockSpec((tile, RET_QK_DIM), lambda i: (i, 0))
    out_t = pl.BlockSpec((half, tile), lambda i: (0, i))
    return pl.pallas_call(
        _rope_kernel,
        grid=(T // tile,),
        in_specs=[pl.BlockSpec((1, tile), lambda i: (0, i)), pl.BlockSpec((half, 1), lambda i: (0, 0))],
        out_specs=[out, out, out_t, out_t],
        out_shape=[jax.ShapeDtypeStruct((T, RET_QK_DIM), F32)] * 2 + [jax.ShapeDtypeStruct((half, T), F32)] * 2,
        name="rope_table",
    )(pos_row, freq)


POOL_SUB = 256


def _pool_bands():
    r = np.arange(POOL_SUB)[:, None]
    c = np.arange(POOL_SUB)[None, :]
    ch = np.arange(POOL_HALO)[None, :] - POOL_HALO
    main = np.stack([((r - c >= 0) & (r - c < w)) for w in POOL_WINDOWS]).astype(np.float32)
    halo = np.stack([((r - ch >= 0) & (r - ch < w)) for w in POOL_WINDOWS]).astype(np.float32)
    return jnp.asarray(main, BF16), jnp.asarray(halo, BF16)


def _pool_branch(ub, j, halo_ref, bmain_ref, bhalo_ref, wg_ref, scale_ref, o_ref, filler):
    tile = ub.shape[0]
    s0 = j * tile
    slot = lax.rem(j, 2)
    blocks = [(sb * POOL_SUB, g) for sb in range(tile // POOL_SUB) for g in range(len(POOL_WINDOWS))]
    wsum = {}
    for r0, g in blocks:
        cols = slice(g * POOL_GROUP_DIM, (g + 1) * POOL_GROUP_DIM)
        prev = halo_ref[slot] if r0 == 0 else ub[r0 - POOL_HALO:r0]
        wsum[r0, g] = _dot(bmain_ref[g], ub[r0:r0 + POOL_SUB, cols]) + _dot(bhalo_ref[g], prev[:, cols])
    filler()
    for r0, g in blocks:
        cols = slice(g * POOL_GROUP_DIM, (g + 1) * POOL_GROUP_DIM)
        pos = s0 + r0 + lax.broadcasted_iota(I32, (POOL_SUB, POOL_GROUP_DIM), 0)
        cnt = jnp.minimum(pos + 1, POOL_WINDOWS[g]).astype(F32)
        pooled = wsum[r0, g] / cnt - ub[r0:r0 + POOL_SUB, cols].astype(F32)
        mixed = _dot(pooled.astype(BF16), wg_ref[g]) * scale_ref[:, cols]
        o_ref[r0:r0 + POOL_SUB, cols] = mixed.astype(BF16)
    halo_ref[1 - slot] = ub[tile - POOL_HALO:tile]


def _ret_consts():
    h = np.arange(RET_HEADS, dtype=np.float64)
    log_gamma = np.log1p(-np.exp2(-5.0 - h))
    pos = np.arange(RET_CHUNK, dtype=np.float64)
    diff = pos[:, None] - pos[None, :]
    kscale = RET_QK_DIM ** -0.5
    dmask = kscale * np.where(diff >= 0, np.exp(log_gamma[:, None, None] * np.maximum(diff, 0.0)), 0.0)
    qdec = np.exp(log_gamma[:, None] * (pos + 1.0)[None, :])
    kdec = kscale * np.exp(log_gamma[:, None] * (RET_CHUNK - 1.0 - pos)[None, :])
    cdec = np.exp(log_gamma * RET_CHUNK)
    lanes = lambda a: np.broadcast_to(a[:, :, None], (RET_HEADS, RET_CHUNK, RET_QK_DIM))
    kdec_t = np.broadcast_to(kdec[:, None, :], (RET_HEADS, RET_QK_DIM, RET_CHUNK))
    return (jnp.asarray(dmask, F32), jnp.asarray(lanes(qdec), F32), jnp.asarray(kdec_t, F32),
            tuple(float(v) for v in cdec))


def _retention_branch(q, k_t, v, silu_g, cos_ref, sin_ref, cos_t_ref, sin_t_ref, dmask_ref, qdec_ref, kdec_ref,
                      gnw_ref, state_ref, rq_ref, rqd_ref, rkt_ref, rkdt_ref, o_ref, cdec, fillers):
    tile = q.shape[0]
    n_chunks = tile // RET_CHUNK
    half = RET_QK_DIM // 2
    cos = cos_ref[...]
    sin = sin_ref[...]
    cos_t = cos_t_ref[...]
    sin_t = sin_t_ref[...]
    for h in range(RET_HEADS):
        qk = slice(h * RET_QK_DIM, (h + 1) * RET_QK_DIM)
        qh = q[:, qk]
        qr = qh * cos + pltpu.roll(qh, half, 1) * sin
        rq_ref[:, qk] = qr.astype(BF16)
        rqd_ref[:, qk] = (qr * jnp.concatenate([qdec_ref[h]] * n_chunks, axis=0)).astype(BF16)
        k1 = k_t[h * RET_QK_DIM:h * RET_QK_DIM + half]
        k2 = k_t[h * RET_QK_DIM + half:(h + 1) * RET_QK_DIM]
        kr_t = jnp.concatenate([k1 * cos_t - k2 * sin_t, k2 * cos_t + k1 * sin_t], axis=0)
        rkt_ref[qk, :] = kr_t.astype(BF16)
        rkdt_ref[qk, :] = (kr_t * jnp.concatenate([kdec_ref[h]] * n_chunks, axis=1)).astype(BF16)

    chunks = [slice(c * RET_CHUNK, (c + 1) * RET_CHUNK) for c in range(n_chunks)]
    heads = [slice(h * RET_QK_DIM, (h + 1) * RET_QK_DIM) for h in range(RET_HEADS)]
    v_heads = [slice(h * RET_V_DIM, (h + 1) * RET_V_DIM) for h in range(RET_HEADS)]
    raw = {(c, h): _dot(rq_ref[chunks[c], heads[h]], rkt_ref[heads[h], chunks[c]])
           for c in range(n_chunks) for h in range(RET_HEADS)}
    incr = {(c, h): _dot(rkdt_ref[heads[h], chunks[c]], v[chunks[c], v_heads[h]])
            for c in range(n_chunks) for h in range(RET_HEADS)}
    state_in = {}
    for h in range(RET_HEADS):
        st = state_ref[h]
        for c in range(n_chunks):
            state_in[c, h] = st.astype(BF16)
            st = cdec[h] * st + incr[c, h]
        state_ref[h] = st

    for c in range(n_chunks):
        rows = chunks[c]
        fillers[c]()
        for h in range(RET_HEADS):
            qk = heads[h]
            v_cols = v_heads[h]
            scores = raw[c, h] * dmask_ref[h]
            lhs = jnp.concatenate([scores.astype(BF16), rqd_ref[rows, qk]], axis=1)
            y = _dot(lhs, jnp.concatenate([v[rows, v_cols], state_in[c, h]], axis=0))
            mu = jnp.mean(y, axis=-1, keepdims=True)
            yc = y - mu
            var = jnp.mean(yc * yc, axis=-1, keepdims=True)
            yn = yc * lax.rsqrt(var + LN_EPS) * gnw_ref[:, v_cols]
            o_ref[rows, v_cols] = (silu_g[h][rows] * yn).astype(BF16)


def _memkv_kernel(m_ref, w_ref, o_ref):
    o_ref[...] = _dot(m_ref[...].astype(BF16), w_ref[...]).astype(BF16)


def _mem_kv(memf, w_b):
    M, D = memf.shape
    N = w_b.shape[1]
    return pl.pallas_call(
        _memkv_kernel,
        grid=(1,),
        in_specs=[pl.BlockSpec((M, D), lambda i: (0, 0)), pl.BlockSpec((D, N), lambda i: (0, 0))],
        out_specs=pl.BlockSpec((M, N), lambda i: (0, 0)),
        out_shape=jax.ShapeDtypeStruct((M, N), BF16),
        name="mem_kv",
    )(memf, w_b)


def _cross_attention_branch(xq, k_ref, v_ref, o_ref):
    scale = XA_HEAD_DIM ** -0.5
    for h in range(XA_HEADS):
        cols = slice(h * XA_HEAD_DIM, (h + 1) * XA_HEAD_DIM)
        s = _dot_nt(xq[:, cols], k_ref[:, cols]) * scale
        m = jnp.max(s, axis=-1, keepdims=True)
        p = jnp.exp(s - m)
        l = jnp.sum(p, axis=-1, keepdims=True)
        o = _dot(p.astype(BF16), v_ref[:, cols]) / l
        o_ref[:, cols] = o.astype(BF16)


ROUTER_ROWS = 128
WTOK_LANES = 128
EXP_ROW0 = 8


def _route(logits_t):
    gl = logits_t[0:8]
    gmax = jnp.max(gl, axis=0, keepdims=True)
    p_grp = 1.0 / jnp.sum(jnp.exp(gl - gmax), axis=0, keepdims=True)
    idx8 = lax.broadcasted_iota(jnp.int32, gl.shape, 0)
    gsel = jnp.min(jnp.where(gl == gmax, idx8, 8), axis=0, keepdims=True)
    cl = jnp.zeros_like(gl)
    for g in range(N_GROUPS):
        r0 = EXP_ROW0 + g * EXPERTS_PER_GROUP
        cl = cl + jnp.where(gsel == g, logits_t[r0:r0 + EXPERTS_PER_GROUP], 0.0)
    v1 = jnp.max(cl, axis=0, keepdims=True)
    i1 = jnp.min(jnp.where(cl == v1, idx8, 8), axis=0, keepdims=True)
    cl2 = jnp.where(idx8 == i1, -jnp.inf, cl)
    v2 = jnp.max(cl2, axis=0, keepdims=True)
    i2 = jnp.min(jnp.where(cl2 == v2, idx8, 8), axis=0, keepdims=True)
    e21 = jnp.exp(v2 - v1)
    w1 = p_grp / (1.0 + e21)
    w2 = p_grp * e21 / (1.0 + e21)
    return gsel * EXPERTS_PER_GROUP + i1, gsel * EXPERTS_PER_GROUP + i2, w1, w2


def _pack_halves(v):
    half = v.shape[1] // 2
    lo = lax.bitcast_convert_type(v[:, :half].astype(BF16).astype(F32), U32)
    hi = lax.bitcast_convert_type(v[:, half:].astype(BF16).astype(F32), U32)
    return lax.bitcast_convert_type(lax.shift_right_logical(lo, U32(16)) | hi, I32)


def _unpack_halves(w):
    u = lax.bitcast_convert_type(w, U32)
    lo = lax.bitcast_convert_type(lax.shift_left(u, U32(16)), F32)
    hi = lax.bitcast_convert_type(u & U32(0xFFFF0000), F32)
    return lo, hi


def _mixer_kernel(x_ref, cos_ref, sin_ref, cos_t_ref, sin_t_ref, km_ref, vm_ref, win_ref, wgrp_ref,
                  pscale_ref, bmain_ref, bhalo_ref, dmask_ref, qdec_ref, kdec_ref, gnw_ref, wp_ref, wr_ref,
                  wa_ref, wo_ref, lnw_ref, lnb_ref, rw_ref, rb_ref, xp_ref, eid_ref, wtok_ref,
                  state_ref, halo_ref, ypool_ref, yret_ref, yxa_ref, rq_ref, rqd_ref, rkt_ref, rkdt_ref, wkt_ref,
                  *, tile, cdec):
    j = pl.program_id(1)

    @pl.when(jnp.logical_and(pl.program_id(0) == 0, j == 0))
    def _():
        wk = win_ref[:, COL_K:COL_K + RET_HEADS * RET_QK_DIM].astype(F32)
        wkt_ref[...] = jnp.transpose(wk).astype(BF16)

    @pl.when(j == 0)
    def _():
        state_ref[...] = jnp.zeros_like(state_ref)
        halo_ref[...] = jnp.zeros_like(halo_ref)

    x = x_ref[...]
    xb = x.astype(BF16)

    def proj(col, width):
        return _dot(xb, win_ref[:, col:col + width])

    part = {}
    strips = [slice(c, c + STRIP) for c in range(0, D_MODEL, STRIP)]

    def gate(branch, cols):
        return _sigmoid(proj(COL_GATES + branch * D_MODEL + cols.start, STRIP))

    def pool_part():
        def pool_gates():
            part["pool_gate"] = [gate(0, c) for c in strips]

        _pool_branch(proj(COL_POOL, POOL_WIDTH).astype(BF16), j, halo_ref, bmain_ref, bhalo_ref, wgrp_ref,
                     pscale_ref, ypool_ref, pool_gates)
        part["pool"] = [part["pool_gate"][i] * _dot(ypool_ref[...], wp_ref[:, c]) for i, c in enumerate(strips)]

    def xa_part():
        _cross_attention_branch(proj(COL_XAQ, XA_WIDTH).astype(BF16), km_ref, vm_ref, yxa_ref)
        part["xa"] = [gate(2, c) * _dot(yxa_ref[...], wa_ref[:, c]) for c in strips]

    def ret_gate_part():
        part["ret_gate"] = [gate(1, c) for c in strips]

    silu_g = []
    for h in range(RET_HEADS):
        gh = proj(COL_G + h * RET_V_DIM, RET_V_DIM)
        silu_g.append(gh * _sigmoid(gh))
    fillers = [pool_part, xa_part, ret_gate_part] + [lambda: None] * (tile // RET_CHUNK - 3)
    _retention_branch(proj(COL_Q, RET_HEADS * RET_QK_DIM), _dot_nt(wkt_ref[...], xb),
                      proj(COL_V, RET_HEADS * RET_V_DIM).astype(BF16), silu_g, cos_ref, sin_ref, cos_t_ref,
                      sin_t_ref, dmask_ref, qdec_ref, kdec_ref, gnw_ref, state_ref, rq_ref, rqd_ref, rkt_ref,
                      rkdt_ref, yret_ref, cdec, fillers)
    merged = jnp.concatenate(
        [(part["pool"][i] + part["ret_gate"][i] * _dot(yret_ref[...], wr_ref[:, c]) + part["xa"][i]).astype(BF16)
         for i, c in enumerate(strips)], axis=1)
    h = jnp.concatenate([ALPHA * x[:, c] + _dot(merged, wo_ref[:, c]) for c in strips], axis=1)
    x1 = jnp.concatenate([_layer_norm(h[r:r + LN_ROWS], lnw_ref[...], lnb_ref[...])
                          for r in range(0, tile, LN_ROWS)], axis=0)
    xp_ref[...] = _pack_halves(x1)
    x1_hi = x1.astype(BF16)
    x1_lo = (x1 - x1_hi.astype(F32)).astype(BF16)
    p4 = _dot(jnp.concatenate([x1_hi, x1_lo], axis=0), rw_ref[...])
    logits = (p4[:tile, :ROUTER_ROWS] + p4[:tile, ROUTER_ROWS:]) + (p4[tile:, :ROUTER_ROWS] + p4[tile:, ROUTER_ROWS:])
    e0, e1, w0, w1 = _route(jnp.transpose(logits) + rb_ref[...])
    eid_ref[...] = jnp.concatenate([e0, e1, jnp.zeros((SUBLANES - TOP_K, tile), I32)], axis=0)
    w_t = jnp.concatenate([w0, w1, jnp.zeros((WTOK_LANES - TOP_K, tile), F32)], axis=0)
    wtok_ref[...] = jnp.transpose(w_t)


def _mixer(xf, rope, kv, win, wgrp, pscale, gnw, wp, wr, wa, wo, lnw, lnb, rw, rb, batch, seq, mem_len,
           tile=512):
    T = xf.shape[0]
    nj = seq // tile
    cos, sin, cos_t, sin_t = rope
    bmain, bhalo = _pool_bands()
    dmask, qdec, kdec, cdec = _ret_consts()
    resident = lambda a: pl.BlockSpec(a.shape, lambda b, j: (0,) * a.ndim, pipeline_mode=pl.Buffered(1))
    rowblk = lambda w: pl.BlockSpec((tile, w), lambda b, j: (b * nj + j, 0))
    colblk = lambda r: pl.BlockSpec((r, tile), lambda b, j: (0, b * nj + j))
    consts = (win, wgrp, pscale, bmain, bhalo, dmask, qdec, kdec, gnw, wp, wr, wa, wo, lnw, lnb, rw, rb)
    return pl.pallas_call(
        functools.partial(_mixer_kernel, tile=tile, cdec=cdec),
        grid=(batch, nj),
        in_specs=[rowblk(D_MODEL), rowblk(RET_QK_DIM), rowblk(RET_QK_DIM),
                  colblk(RET_QK_DIM // 2), colblk(RET_QK_DIM // 2),
                  pl.BlockSpec((mem_len, XA_WIDTH), lambda b, j: (b, 0)),
                  pl.BlockSpec((mem_len, XA_WIDTH), lambda b, j: (b, 1))] + [resident(a) for a in consts],
        out_specs=[rowblk(PACK_W), colblk(SUBLANES), rowblk(WTOK_LANES)],
        out_shape=[jax.ShapeDtypeStruct((T, PACK_W), I32), jax.ShapeDtypeStruct((SUBLANES, T), I32),
                   jax.ShapeDtypeStruct((T, WTOK_LANES), F32)],
        scratch_shapes=[pltpu.VMEM((RET_HEADS, RET_QK_DIM, RET_V_DIM), F32),
                        pltpu.VMEM((2, POOL_HALO, POOL_WIDTH), BF16),
                        pltpu.VMEM((tile, POOL_WIDTH), BF16),
                        pltpu.VMEM((tile, RET_HEADS * RET_V_DIM), BF16),
                        pltpu.VMEM((tile, XA_WIDTH), BF16),
                        pltpu.VMEM((tile, RET_HEADS * RET_QK_DIM), BF16),
                        pltpu.VMEM((tile, RET_HEADS * RET_QK_DIM), BF16),
                        pltpu.VMEM((RET_HEADS * RET_QK_DIM, tile), BF16),
                        pltpu.VMEM((RET_HEADS * RET_QK_DIM, tile), BF16),
                        pltpu.VMEM((RET_HEADS * RET_QK_DIM, D_MODEL), BF16)],
        compiler_params=pltpu.CompilerParams(dimension_semantics=("arbitrary", "arbitrary"),
                                             vmem_limit_bytes=VMEM_LIMIT),
        name="mixer",
    )(xf, cos, sin, cos_t, sin_t, kv, kv, *consts)


META_LANES = LANES


def _positions_kernel(eid_ref, tri_ref, low_ref, pos_ref, first_tile_ref, n_tiles_ref, *, n_tok):
    n_chunks = n_tok // RANK_CHUNK
    erow = lax.broadcasted_iota(I32, (N_EXPERTS, RANK_CHUNK), 0)

    def onehot(c):
        sl = slice(c * RANK_CHUNK, (c + 1) * RANK_CHUNK)
        m0 = eid_ref[0:1, sl] == erow
        m1 = eid_ref[1:2, sl] == erow
        return m0, m1, jnp.where(m0, 1.0, 0.0) + jnp.where(m1, 1.0, 0.0)

    counts = jnp.zeros((N_EXPERTS, 1), F32)
    for c in range(n_chunks):
        counts = counts + jnp.sum(onehot(c)[2], axis=1, keepdims=True)
    ptiles = jnp.floor((counts + (MOE_TM - 1)) * (1.0 / MOE_TM))
    ptiles_b = jnp.broadcast_to(ptiles, (N_EXPERTS, LANES)).astype(BF16)
    start = _dot(low_ref[...], ptiles_b)[:, 0:1] * MOE_TM

    pos_ref[...] = jnp.zeros_like(pos_ref)
    carry = start - 1.0
    for c in range(n_chunks):
        sl = slice(c * RANK_CHUNK, (c + 1) * RANK_CHUNK)
        m0, m1, oh = onehot(c)
        rank = _dot(oh.astype(BF16), tri_ref[...]) + carry
        pos_ref[0:1, sl] = jnp.sum(jnp.where(m0, rank, 0.0), axis=0, keepdims=True).astype(I32)
        pos_ref[1:2, sl] = jnp.sum(jnp.where(m1, rank, 0.0), axis=0, keepdims=True).astype(I32)
        carry = carry + jnp.sum(oh, axis=1, keepdims=True)

    first_tile_ref[...] = jnp.broadcast_to(start * (1.0 / MOE_TM), first_tile_ref.shape).astype(I32)
    n_tiles_ref[...] = jnp.broadcast_to(ptiles, n_tiles_ref.shape).astype(I32)


def _positions(eid):
    T = eid.shape[1]
    r = np.arange(RANK_CHUNK)
    tri = jnp.asarray(r[:, None] <= r[None, :], BF16)
    e = np.arange(N_EXPERTS)
    low = jnp.asarray(e[None, :] < e[:, None], BF16)
    full = lambda a: pl.BlockSpec(a.shape, lambda i: (0,) * a.ndim)
    return pl.pallas_call(
        functools.partial(_positions_kernel, n_tok=T),
        grid=(1,),
        in_specs=[full(eid), full(tri), full(low)],
        out_specs=[pl.BlockSpec((SUBLANES, T), lambda i: (0, 0)),
                   pl.BlockSpec((N_EXPERTS, META_LANES), lambda i: (0, 0)),
                   pl.BlockSpec((N_EXPERTS, META_LANES), lambda i: (0, 0))],
        out_shape=[jax.ShapeDtypeStruct((SUBLANES, T), I32), jax.ShapeDtypeStruct((N_EXPERTS, META_LANES), I32),
                   jax.ShapeDtypeStruct((N_EXPERTS, META_LANES), I32)],
        name="route_positions",
    )(eid, tri, low)


def _sc_workers():
    info = plsc.get_sparse_core_info()
    return info.num_cores, info.num_cores * info.num_subcores


def _sc_scatter_rows(xp, pos2d, n_out):
    T, W = xp.shape
    n_cores, n_workers = _sc_workers()
    cpw = T // SC_CHUNK // n_workers
    mesh = plsc.VectorSubcoreMesh(core_axis_name="c", subcore_axis_name="s")

    @functools.partial(
        pl.kernel, mesh=mesh, out_type=jax.ShapeDtypeStruct((n_out, W), I32),
        scratch_types=[pltpu.VMEM((TOP_K * cpw, SC_CHUNK), I32), pltpu.VMEM((2, SC_CHUNK, W), I32),
                       pltpu.SemaphoreType.DMA((2,)), pltpu.SemaphoreType.DMA((2,))],
        name="sc_scatter_rows")
    def k(x_hbm, pos_hbm, out_hbm, idx_v, rows_v, rd_sem, wr_sem):
        wid = lax.axis_index("s") * n_cores + lax.axis_index("c")
        for s in range(TOP_K):
            pltpu.sync_copy(pos_hbm.at[pl.ds(s * (T // SC_CHUNK) + wid * cpw, cpw)],
                            idx_v.at[pl.ds(s * cpw, cpw)])

        def read(j):
            return pltpu.make_async_copy(x_hbm.at[pl.ds((wid * cpw + j) * SC_CHUNK, SC_CHUNK)],
                                         rows_v.at[j % 2], rd_sem.at[j % 2])

        def write(j, s):
            return pltpu.make_async_copy(rows_v.at[j % 2], out_hbm.at[idx_v.at[s * cpw + j]], wr_sem.at[j % 2])

        _sc_two_buffer_stream(cpw, read, lambda j: [write(j, s) for s in range(TOP_K)])

    return k(xp, pos2d)


def _sc_two_buffer_stream(n, read, writes):
    read(0).start()
    for j in range(n):
        read(j).wait()
        if j + 1 < n:
            if j >= 1:
                for w in writes(j - 1):
                    w.wait()
            read(j + 1).start()
        for w in writes(j):
            w.start()
    for j in range(max(n - 2, 0), n):
        for w in writes(j):
            w.wait()


def _sc_gather_rows(y, idx2d):
    W = y.shape[1]
    n = idx2d.shape[0] * SC_CHUNK
    n_cores, n_workers = _sc_workers()
    cpw = n // SC_CHUNK // n_workers
    mesh = plsc.VectorSubcoreMesh(core_axis_name="c", subcore_axis_name="s")

    @functools.partial(
        pl.kernel, mesh=mesh, out_type=jax.ShapeDtypeStruct((n, W), I32),
        scratch_types=[pltpu.VMEM((cpw, SC_CHUNK), I32), pltpu.VMEM((2, SC_CHUNK, W), I32),
                       pltpu.SemaphoreType.DMA((2,)), pltpu.SemaphoreType.DMA((2,))],
        name="sc_gather_rows")
    def k(y_hbm, idx_hbm, out_hbm, idx_v, rows_v, rd_sem, wr_sem):
        wid = lax.axis_index("s") * n_cores + lax.axis_index("c")
        pltpu.sync_copy(idx_hbm.at[pl.ds(wid * cpw, cpw)], idx_v)

        def read(j):
            return pltpu.make_async_copy(y_hbm.at[idx_v.at[j]], rows_v.at[j % 2], rd_sem.at[j % 2])

        def write(j):
            return pltpu.make_async_copy(rows_v.at[j % 2], out_hbm.at[pl.ds((wid * cpw + j) * SC_CHUNK, SC_CHUNK)],
                                         wr_sem.at[j % 2])

        _sc_two_buffer_stream(cpw, read, lambda j: [write(j)])

    return k(y, idx2d)


def _routed_kernel(first_ref, count_ref, xs_hbm, wg_ref, wu_ref, wd_ref, ys_hbm, xbuf, ybuf, in_sem, out_sem):
    e = pl.program_id(0)
    last = pl.num_programs(0) - 1
    total = first_ref[last] + count_ref[last]

    def in_copy(g):
        slot = lax.rem(g, RING)
        return pltpu.make_async_copy(xs_hbm.at[pl.ds(g * MOE_TM, MOE_TM)], xbuf.at[slot], in_sem.at[slot])

    def out_copy(g):
        slot = lax.rem(g, RING)
        return pltpu.make_async_copy(ybuf.at[slot], ys_hbm.at[pl.ds(g * MOE_TM, MOE_TM)], out_sem.at[slot])

    @pl.when(e == 0)
    def _():
        for g0 in range(RING - 1):
            @pl.when(g0 < total)
            def _():
                in_copy(g0).start()

    wg = wg_ref[0].astype(BF16)
    wu = wu_ref[0].astype(BF16)
    wd = wd_ref[0].astype(BF16)

    def tile_step(i, carry):
        g = first_ref[e] + i
        slot = lax.rem(g, RING)
        in_copy(g).wait()

        @pl.when(g + RING - 1 < total)
        def _():
            in_copy(g + RING - 1).start()

        @pl.when(g >= RING)
        def _():
            out_copy(g - RING).wait()

        lo, hi = _unpack_halves(xbuf[slot])
        lo = lo.astype(BF16)
        hi = hi.astype(BF16)
        a = _dot(lo, wg[:PACK_W]) + _dot(hi, wg[PACK_W:])
        b = _dot(lo, wu[:PACK_W]) + _dot(hi, wu[PACK_W:])
        act = (a * _sigmoid(a) * b).astype(BF16)
        ybuf[slot] = _pack_halves(_dot(act, wd))
        out_copy(g).start()
        return carry

    lax.fori_loop(0, count_ref[e], tile_step, 0)

    @pl.when(e == last)
    def _():
        for back in range(RING, 0, -1):
            @pl.when(total >= back)
            def _():
                out_copy(total - back).wait()


def _routed_mlp(first_tile, n_tiles, xs, wg, wu, wd):
    R = xs.shape[0]
    any_space = pl.BlockSpec(memory_space=pl.ANY)
    return pl.pallas_call(
        _routed_kernel,
        grid_spec=pltpu.PrefetchScalarGridSpec(
            num_scalar_prefetch=2,
            grid=(N_EXPERTS,),
            in_specs=[any_space,
                      pl.BlockSpec((1, D_MODEL, D_EXPERT), lambda e, ft, nt: (e, 0, 0)),
                      pl.BlockSpec((1, D_MODEL, D_EXPERT), lambda e, ft, nt: (e, 0, 0)),
                      pl.BlockSpec((1, D_EXPERT, D_MODEL), lambda e, ft, nt: (e, 0, 0))],
            out_specs=any_space,
            scratch_shapes=[pltpu.VMEM((RING, MOE_TM, PACK_W), I32), pltpu.VMEM((RING, MOE_TM, PACK_W), I32),
                            pltpu.SemaphoreType.DMA((RING,)), pltpu.SemaphoreType.DMA((RING,))]),
        out_shape=jax.ShapeDtypeStruct((R, PACK_W), I32),
        compiler_params=pltpu.CompilerParams(dimension_semantics=("arbitrary",)),
        name="routed_mlp",
    )(first_tile, n_tiles, xs, wg, wu, wd)


def _combine_kernel(xp_ref, y0_ref, y1_ref, wtok_ref, lnw_ref, lnb_ref, *out_refs):
    o_ref = out_refs[-1]
    w0 = wtok_ref[:, 0:1]
    w1 = wtok_ref[:, 1:2]
    xlo, xhi = _unpack_halves(xp_ref[...])
    y0lo, y0hi = _unpack_halves(y0_ref[...])
    y1lo, y1hi = _unpack_halves(y1_ref[...])
    h = jnp.concatenate([ALPHA * xlo + (w0 * y0lo + w1 * y1lo), ALPHA * xhi + (w0 * y0hi + w1 * y1hi)], axis=1)
    o_ref[...] = _layer_norm(h, lnw_ref[...], lnb_ref[...])


def _combine_ln2(xp, yg_parts, wtok, lnw, lnb, tile=1024):
    T = xp.shape[0]
    n_parts = len(yg_parts)
    nt = T // tile // n_parts
    full = lambda a: pl.BlockSpec(a.shape, lambda i: (0,) * a.ndim)
    out = None
    for p, yg in enumerate(yg_parts):
        rows = lambda w, p=p: pl.BlockSpec((tile, w), lambda i: (i + p * nt, 0))
        in_specs = [rows(PACK_W),
                    pl.BlockSpec((tile, PACK_W), lambda i: (i, 0)),
                    pl.BlockSpec((tile, PACK_W), lambda i: (i + nt, 0)),
                    rows(WTOK_LANES), full(lnw), full(lnb)]
        args = [xp, yg, yg, wtok, lnw, lnb]
        aliases = {}
        if out is not None:
            in_specs.append(pl.BlockSpec(memory_space=pl.ANY))
            args.append(out)
            aliases = {len(args) - 1: 0}
        out = pl.pallas_call(
            _combine_kernel,
            grid=(nt,),
            in_specs=in_specs,
            out_specs=rows(D_MODEL),
            out_shape=jax.ShapeDtypeStruct((T, D_MODEL), F32),
            input_output_aliases=aliases,
            compiler_params=pltpu.CompilerParams(dimension_semantics=("arbitrary",), vmem_limit_bytes=VMEM_LIMIT),
            name="combine_ln2",
        )(*args)
    return out


def _router_params(w_grp, b_grp, w_exp, b_exp):
    rw = jnp.zeros((D_MODEL, ROUTER_ROWS), F32)
    rw = rw.at[:, 0:N_GROUPS].set(w_grp).at[:, EXP_ROW0:EXP_ROW0 + N_EXPERTS].set(w_exp)
    rb = jnp.zeros((ROUTER_ROWS,), F32).at[N_GROUPS:8].set(NEG_BIG)
    rb = rb.at[0:N_GROUPS].set(b_grp).at[EXP_ROW0:EXP_ROW0 + N_EXPERTS].set(b_exp)
    rw_hi = rw.astype(BF16)
    rw_lo = (rw - rw_hi.astype(F32)).astype(BF16)
    return jnp.concatenate([rw_hi, rw_lo], axis=1), rb[:, None]


def kernel(x, mem, positions, w_in, w_pool_grp, pool_scale, ret_gn_w, w_mem_kv, w_br_pool, w_br_ret, w_br_xa,
           w_out, ln1_w, ln1_b, w_grp_router, b_grp_router, w_exp_router, b_exp_router, w_exp_gate, w_exp_up,
           w_exp_down, ln2_w, ln2_b):
    B, S, D = x.shape
    assert D == D_MODEL and w_in.shape[0] == DEPTH and S % 512 == 0
    T = B * S
    M = mem.shape[1]
    l = 0
    xf = x.reshape(T, D)

    rope = _rope_table(positions.reshape(1, T))
    kv = _mem_kv(mem.reshape(B * M, D), w_mem_kv[l].astype(BF16))
    rw, rb = _router_params(w_grp_router[l], b_grp_router[l], w_exp_router[l], b_exp_router[l])
    xp, eid, wtok = _mixer(xf, rope, kv, w_in[l].astype(BF16),
                           w_pool_grp[l].astype(BF16), pool_scale[l][None, :], ret_gn_w[l].reshape(1, -1),
                           w_br_pool[l].astype(BF16), w_br_ret[l].astype(BF16), w_br_xa[l].astype(BF16),
                           w_out[l].astype(BF16), ln1_w[l][None, :], ln1_b[l][None, :], rw, rb, B, S, M)

    pos, first_tile, n_tiles = _positions(eid)
    pos2d = pos[0:TOP_K].reshape(TOP_K * T // SC_CHUNK, SC_CHUNK)
    max_tiles = (TOP_K * T + N_EXPERTS * (MOE_TM - 1)) // MOE_TM
    xs = _sc_scatter_rows(xp, pos2d, max_tiles * MOE_TM)
    ys = _routed_mlp(first_tile[:, 0], n_tiles[:, 0], xs,
                     w_exp_gate[l].reshape(N_EXPERTS, D_MODEL, D_EXPERT),
                     w_exp_up[l].reshape(N_EXPERTS, D_MODEL, D_EXPERT),
                     w_exp_down[l].reshape(N_EXPERTS, D_EXPERT, D_MODEL))
    rng = T // COMBINE_PARTS
    yg_parts = [_sc_gather_rows(ys, pos[0:TOP_K, p * rng:(p + 1) * rng].reshape(TOP_K * rng // SC_CHUNK, SC_CHUNK))
                for p in range(COMBINE_PARTS)]
    out = _combine_ln2(xp, yg_parts, wtok, ln2_w[l][None, :], ln2_b[l][None, :])
    return out.reshape(B, S, D)
```

```python
import functools

import numpy as np
import jax
import jax.numpy as jnp
from jax import lax
from jax.experimental import pallas as pl
from jax.experimental.pallas import tpu as pltpu
from jax.experimental.pallas import tpu_sc as plsc

F32 = jnp.float32
BF16 = jnp.bfloat16
I32 = jnp.int32
U32 = jnp.uint32

D_MODEL = 1024
POOL_WINDOWS = (2, 4, 8, 16)
POOL_GROUP_DIM = 128
POOL_WIDTH = 512
POOL_HALO = 16
RET_HEADS = 4
RET_QK_DIM = 128
RET_V_DIM = 256
RET_CHUNK = 128
ROPE_BASE = 10000.0
XA_HEADS = 4
XA_HEAD_DIM = 128
XA_WIDTH = 512
N_GROUPS = 4
EXPERTS_PER_GROUP = 8
N_EXPERTS = N_GROUPS * EXPERTS_PER_GROUP
D_EXPERT = 256
LN_EPS = 1e-5
DEPTH = 1
ALPHA = (2.0 * DEPTH) ** 0.25
NEG_BIG = -1e30

COL_POOL, COL_Q, COL_K, COL_V, COL_G, COL_XAQ, COL_GATES = 0, 512, 1024, 1536, 2560, 3584, 4096

V7X_VMEM_BYTES = 64 * 1024 * 1024
VMEM_LIMIT = V7X_VMEM_BYTES * 7 // 8
SUBLANES = 8
LANES = 128

TOP_K = 2
PACK_W = D_MODEL // 2
MOE_TM = 512
SC_CHUNK = 64
RANK_CHUNK = 512
COMBINE_PARTS = 2
RING = 4
STRIP = 256
LN_ROWS = 32


def _dot(a, b):
    return jnp.dot(a, b, preferred_element_type=F32)


def _dot_nt(a, b):
    return lax.dot_general(a, b, (((1,), (1,)), ((), ())), preferred_element_type=F32)


def _sigmoid(z):
    return 1.0 / (1.0 + jnp.exp2(z * (-1.0 / np.log(2.0))))


def _layer_norm(h, w, b):
    mu = jnp.mean(h, axis=-1, keepdims=True)
    hc = h - mu
    var = jnp.mean(hc * hc, axis=-1, keepdims=True)
    return hc * lax.rsqrt(var + LN_EPS) * w + b


ROPE_LO = 64
ROPE_PARTS = 3


def _rope_kernel(pos_ref, freq_ref, tab_ref, cos_ref, sin_ref, cos_t_ref, sin_t_ref):
    pos = pos_ref[...]

    def emit(cos_t, sin_t):
        cos_t_ref[...] = cos_t
        sin_t_ref[...] = sin_t
        cos_ref[...] = jnp.transpose(jnp.concatenate([cos_t, cos_t], axis=0))
        sin_ref[...] = jnp.transpose(jnp.concatenate([-sin_t, sin_t], axis=0))

    in_table = jnp.logical_and(jnp.min(pos) >= 0, jnp.max(pos) < ROPE_LO * LANES)

    @pl.when(in_table)
    def _():
        idx = lax.broadcasted_iota(I32, (LANES, pos.shape[1]), 0)
        pick_hi = jnp.where(idx == jnp.right_shift(pos, ROPE_LO.bit_length() - 1), 1.0, 0.0).astype(BF16)
        pick_lo = jnp.where(idx == (pos & (ROPE_LO - 1)), 1.0, 0.0).astype(BF16)

        def look(k, pick):
            return sum(_dot(tab_ref[k * ROPE_PARTS + p], pick) for p in range(ROPE_PARTS))

        cos_a, sin_a, cos_b, sin_b = look(0, pick_hi), look(1, pick_hi), look(2, pick_lo), look(3, pick_lo)
        emit(cos_a * cos_b - sin_a * sin_b, sin_a * cos_b + cos_a * sin_b)

    @pl.when(jnp.logical_not(in_table))
    def _():
        ang = freq_ref[...] * pos.astype(F32)
        emit(jnp.cos(ang), jnp.sin(ang))


def _rope_tables(inv_freq):
    f = inv_freq.astype(np.float64)[:, None]
    idx = np.arange(LANES, dtype=np.float64)[None, :]
    tabs = [np.cos(ROPE_LO * idx * f), np.sin(ROPE_LO * idx * f), np.cos(idx * f), np.sin(idx * f)]
    pieces = []
    for t in tabs:
        rest = t.astype(np.float32)
        for _ in range(ROPE_PARTS):
            piece = rest.astype(BF16)
            pieces.append(piece)
            rest = rest - piece.astype(np.float32)
    return jnp.asarray(np.stack(pieces))


def _rope_table(pos_row, tile=2048):
    T = pos_row.shape[1]
    half = RET_QK_DIM // 2
    inv_freq = (ROPE_BASE ** (-np.arange(half, dtype=np.float64) / half)).astype(np.float32)
    freq = jnp.asarray(inv_freq[:, None])
    tabs = _rope_tables(inv_freq)
    out = pl.BlockSpec((tile, RET_QK_DIM), lambda i: (i, 0))
    out_t = pl.BlockSpec((half, tile), lambda i: (0, i))
    return pl.pallas_call(
        _rope_kernel,
        grid=(T // tile,),
        in_specs=[pl.BlockSpec((1, tile), lambda i: (0, i)), pl.BlockSpec((half, 1), lambda i: (0, 0)),
                  pl.BlockSpec(tabs.shape, lambda i: (0, 0, 0))],
        out_specs=[out, out, out_t, out_t],
        out_shape=[jax.ShapeDtypeStruct((T, RET_QK_DIM), F32)] * 2 + [jax.ShapeDtypeStruct((half, T), F32)] * 2,
        name="rope_table",
    )(pos_row, freq, tabs)


POOL_SUB = 256


def _pool_bands():
    r = np.arange(POOL_SUB)[:, None]
    c = np.arange(POOL_SUB)[None, :]
    ch = np.arange(POOL_HALO)[None, :] - POOL_HALO
    main = np.stack([((r - c >= 0) & (r - c < w)) for w in POOL_WINDOWS]).astype(np.float32)
    halo = np.stack([((r - ch >= 0) & (r - ch < w)) for w in POOL_WINDOWS]).astype(np.float32)
    return jnp.asarray(main, BF16), jnp.asarray(halo, BF16)


def _pool_branch(ub, j, halo_ref, bmain_ref, bhalo_ref, wg_ref, scale_ref, o_ref, filler):
    tile = ub.shape[0]
    s0 = j * tile
    slot = lax.rem(j, 2)
    blocks = [(sb * POOL_SUB, g) for sb in range(tile // POOL_SUB) for g in range(len(POOL_WINDOWS))]
    wsum = {}
    for r0, g in blocks:
        cols = slice(g * POOL_GROUP_DIM, (g + 1) * POOL_GROUP_DIM)
        prev = halo_ref[slot] if r0 == 0 else ub[r0 - POOL_HALO:r0]
        wsum[r0, g] = _dot(bmain_ref[g], ub[r0:r0 + POOL_SUB, cols]) + _dot(bhalo_ref[g], prev[:, cols])
    filler()
    for r0, g in blocks:
        cols = slice(g * POOL_GROUP_DIM, (g + 1) * POOL_GROUP_DIM)
        pos = s0 + r0 + lax.broadcasted_iota(I32, (POOL_SUB, POOL_GROUP_DIM), 0)
        cnt = jnp.minimum(pos + 1, POOL_WINDOWS[g]).astype(F32)
        pooled = wsum[r0, g] / cnt - ub[r0:r0 + POOL_SUB, cols].astype(F32)
        mixed = _dot(pooled.astype(BF16), wg_ref[g]) * scale_ref[:, cols]
        o_ref[r0:r0 + POOL_SUB, cols] = mixed.astype(BF16)
    halo_ref[1 - slot] = ub[tile - POOL_HALO:tile]


def _ret_consts():
    h = np.arange(RET_HEADS, dtype=np.float64)
    log_gamma = np.log1p(-np.exp2(-5.0 - h))
    pos = np.arange(RET_CHUNK, dtype=np.float64)
    diff = pos[:, None] - pos[None, :]
    kscale = RET_QK_DIM ** -0.5
    dmask = kscale * np.where(diff >= 0, np.exp(log_gamma[:, None, None] * np.maximum(diff, 0.0)), 0.0)
    qdec = np.exp(log_gamma[:, None] * (pos + 1.0)[None, :])
    kdec = kscale * np.exp(log_gamma[:, None] * (RET_CHUNK - 1.0 - pos)[None, :])
    cdec = np.exp(log_gamma * RET_CHUNK)
    lanes = lambda a: np.broadcast_to(a[:, :, None], (RET_HEADS, RET_CHUNK, RET_QK_DIM))
    kdec_t = np.broadcast_to(kdec[:, None, :], (RET_HEADS, RET_QK_DIM, RET_CHUNK))
    return (jnp.asarray(dmask, F32), jnp.asarray(lanes(qdec), F32), jnp.asarray(kdec_t, F32),
            tuple(float(v) for v in cdec))


def _retention_branch(q, k_t, v, silu_g, cos_ref, sin_ref, cos_t_ref, sin_t_ref, dmask_ref, qdec_ref, kdec_ref,
                      gnw_ref, state_ref, rq_ref, rqd_ref, rkt_ref, rkdt_ref, o_ref, cdec, fillers):
    tile = q.shape[0]
    n_chunks = tile // RET_CHUNK
    half = RET_QK_DIM // 2
    cos = cos_ref[...]
    sin = sin_ref[...]
    cos_t = cos_t_ref[...]
    sin_t = sin_t_ref[...]
    for h in range(RET_HEADS):
        qk = slice(h * RET_QK_DIM, (h + 1) * RET_QK_DIM)
        qh = q[:, qk]
        qr = qh * cos + pltpu.roll(qh, half, 1) * sin
        rq_ref[:, qk] = qr.astype(BF16)
        rqd_ref[:, qk] = (qr * jnp.concatenate([qdec_ref[h]] * n_chunks, axis=0)).astype(BF16)
        k1 = k_t[h * RET_QK_DIM:h * RET_QK_DIM + half]
        k2 = k_t[h * RET_QK_DIM + half:(h + 1) * RET_QK_DIM]
        kr_t = jnp.concatenate([k1 * cos_t - k2 * sin_t, k2 * cos_t + k1 * sin_t], axis=0)
        rkt_ref[qk, :] = kr_t.astype(BF16)
        rkdt_ref[qk, :] = (kr_t * jnp.concatenate([kdec_ref[h]] * n_chunks, axis=1)).astype(BF16)

    chunks = [slice(c * RET_CHUNK, (c + 1) * RET_CHUNK) for c in range(n_chunks)]
    heads = [slice(h * RET_QK_DIM, (h + 1) * RET_QK_DIM) for h in range(RET_HEADS)]
    v_heads = [slice(h * RET_V_DIM, (h + 1) * RET_V_DIM) for h in range(RET_HEADS)]
    raw = {(c, h): _dot(rq_ref[chunks[c], heads[h]], rkt_ref[heads[h], chunks[c]])
           for c in range(n_chunks) for h in range(RET_HEADS)}
    incr = {(c, h): _dot(rkdt_ref[heads[h], chunks[c]], v[chunks[c], v_heads[h]])
            for c in range(n_chunks) for h in range(RET_HEADS)}
    state_in = {}
    for h in range(RET_HEADS):
        st = state_ref[h]
        for c in range(n_chunks):
            state_in[c, h] = st.astype(BF16)
            st = cdec[h] * st + incr[c, h]
        state_ref[h] = st

    for c in range(n_chunks):
        rows = chunks[c]
        fillers[c]()
        for h in range(RET_HEADS):
            qk = heads[h]
            v_cols = v_heads[h]
            scores = raw[c, h] * dmask_ref[h]
            lhs = jnp.concatenate([scores.astype(BF16), rqd_ref[rows, qk]], axis=1)
            y = _dot(lhs, jnp.concatenate([v[rows, v_cols], state_in[c, h]], axis=0))
            mu = jnp.mean(y, axis=-1, keepdims=True)
            yc = y - mu
            var = jnp.mean(yc * yc, axis=-1, keepdims=True)
            yn = yc * lax.rsqrt(var + LN_EPS) * gnw_ref[:, v_cols]
            o_ref[rows, v_cols] = (silu_g[h][rows] * yn).astype(BF16)


def _memkv_kernel(m_ref, w_ref, o_ref):
    o_ref[...] = _dot(m_ref[...].astype(BF16), w_ref[...]).astype(BF16)


def _mem_kv(memf, w_b):
    M, D = memf.shape
    N = w_b.shape[1]
    return pl.pallas_call(
        _memkv_kernel,
        grid=(1,),
        in_specs=[pl.BlockSpec((M, D), lambda i: (0, 0)), pl.BlockSpec((D, N), lambda i: (0, 0))],
        out_specs=pl.BlockSpec((M, N), lambda i: (0, 0)),
        out_shape=jax.ShapeDtypeStruct((M, N), BF16),
        name="mem_kv",
    )(memf, w_b)


def _cross_attention_branch(xq, k_ref, v_ref, o_ref):
    scale = XA_HEAD_DIM ** -0.5
    for h in range(XA_HEADS):
        cols = slice(h * XA_HEAD_DIM, (h + 1) * XA_HEAD_DIM)
        s = _dot_nt(xq[:, cols], k_ref[:, cols]) * scale
        m = jnp.max(s, axis=-1, keepdims=True)
        p = jnp.exp(s - m)
        l = jnp.sum(p, axis=-1, keepdims=True)
        o = _dot(p.astype(BF16), v_ref[:, cols]) / l
        o_ref[:, cols] = o.astype(BF16)


ROUTER_ROWS = 128
WTOK_LANES = 128
EXP_ROW0 = 8


def _route(logits_t):
    gl = logits_t[0:8]
    gmax = jnp.max(gl, axis=0, keepdims=True)
    p_grp = 1.0 / jnp.sum(jnp.exp(gl - gmax), axis=0, keepdims=True)
    idx8 = lax.broadcasted_iota(jnp.int32, gl.shape, 0)
    gsel = jnp.min(jnp.where(gl == gmax, idx8, 8), axis=0, keepdims=True)
    cl = jnp.zeros_like(gl)
    for g in range(N_GROUPS):
        r0 = EXP_ROW0 + g * EXPERTS_PER_GROUP
        cl = cl + jnp.where(gsel == g, logits_t[r0:r0 + EXPERTS_PER_GROUP], 0.0)
    v1 = jnp.max(cl, axis=0, keepdims=True)
    i1 = jnp.min(jnp.where(cl == v1, idx8, 8), axis=0, keepdims=True)
    cl2 = jnp.where(idx8 == i1, -jnp.inf, cl)
    v2 = jnp.max(cl2, axis=0, keepdims=True)
    i2 = jnp.min(jnp.where(cl2 == v2, idx8, 8), axis=0, keepdims=True)
    e21 = jnp.exp(v2 - v1)
    w1 = p_grp / (1.0 + e21)
    w2 = p_grp * e21 / (1.0 + e21)
    return gsel * EXPERTS_PER_GROUP + i1, gsel * EXPERTS_PER_GROUP + i2, w1, w2


def _pack_halves(v):
    half = v.shape[1] // 2
    lo = lax.bitcast_convert_type(v[:, :half].astype(BF16).astype(F32), U32)
    hi = lax.bitcast_convert_type(v[:, half:].astype(BF16).astype(F32), U32)
    return lax.bitcast_convert_type(lax.shift_right_logical(lo, U32(16)) | hi, I32)


def _unpack_halves(w):
    u = lax.bitcast_convert_type(w, U32)
    lo = lax.bitcast_convert_type(lax.shift_left(u, U32(16)), F32)
    hi = lax.bitcast_convert_type(u & U32(0xFFFF0000), F32)
    return lo, hi


def _mixer_kernel(x_ref, cos_ref, sin_ref, cos_t_ref, sin_t_ref, km_ref, vm_ref, win_ref, wgrp_ref,
                  pscale_ref, bmain_ref, bhalo_ref, dmask_ref, qdec_ref, kdec_ref, gnw_ref, wp_ref, wr_ref,
                  wa_ref, wo_ref, lnw_ref, lnb_ref, rw_ref, rb_ref, xp_ref, eid_ref, wtok_ref,
                  state_ref, halo_ref, ypool_ref, yret_ref, yxa_ref, rq_ref, rqd_ref, rkt_ref, rkdt_ref, wkt_ref,
                  *, tile, cdec):
    j = pl.program_id(1)

    @pl.when(jnp.logical_and(pl.program_id(0) == 0, j == 0))
    def _():
        wk = win_ref[:, COL_K:COL_K + RET_HEADS * RET_QK_DIM].astype(F32)
        wkt_ref[...] = jnp.transpose(wk).astype(BF16)

    @pl.when(j == 0)
    def _():
        state_ref[...] = jnp.zeros_like(state_ref)
        halo_ref[...] = jnp.zeros_like(halo_ref)

    x = x_ref[...]
    xb = x.astype(BF16)

    def proj(col, width):
        return _dot(xb, win_ref[:, col:col + width])

    part = {}
    strips = [slice(c, c + STRIP) for c in range(0, D_MODEL, STRIP)]

    def gate(branch, cols):
        return _sigmoid(proj(COL_GATES + branch * D_MODEL + cols.start, STRIP))

    def pool_part():
        def pool_gates():
            part["pool_gate"] = [gate(0, c) for c in strips]

        _pool_branch(proj(COL_POOL, POOL_WIDTH).astype(BF16), j, halo_ref, bmain_ref, bhalo_ref, wgrp_ref,
                     pscale_ref, ypool_ref, pool_gates)
        part["pool"] = [part["pool_gate"][i] * _dot(ypool_ref[...], wp_ref[:, c]) for i, c in enumerate(strips)]

    def xa_part():
        _cross_attention_branch(proj(COL_XAQ, XA_WIDTH).astype(BF16), km_ref, vm_ref, yxa_ref)
        part["xa"] = [gate(2, c) * _dot(yxa_ref[...], wa_ref[:, c]) for c in strips]

    def ret_gate_part():
        part["ret_gate"] = [gate(1, c) for c in strips]

    silu_g = []
    for h in range(RET_HEADS):
        gh = proj(COL_G + h * RET_V_DIM, RET_V_DIM)
        silu_g.append(gh * _sigmoid(gh))
    fillers = [pool_part, xa_part, ret_gate_part] + [lambda: None] * (tile // RET_CHUNK - 3)
    _retention_branch(proj(COL_Q, RET_HEADS * RET_QK_DIM), _dot_nt(wkt_ref[...], xb),
                      proj(COL_V, RET_HEADS * RET_V_DIM).astype(BF16), silu_g, cos_ref, sin_ref, cos_t_ref,
                      sin_t_ref, dmask_ref, qdec_ref, kdec_ref, gnw_ref, state_ref, rq_ref, rqd_ref, rkt_ref,
                      rkdt_ref, yret_ref, cdec, fillers)
    merged = jnp.concatenate(
        [(part["pool"][i] + part["ret_gate"][i] * _dot(yret_ref[...], wr_ref[:, c]) + part["xa"][i]).astype(BF16)
         for i, c in enumerate(strips)], axis=1)
    h = jnp.concatenate([ALPHA * x[:, c] + _dot(merged, wo_ref[:, c]) for c in strips], axis=1)
    x1 = jnp.concatenate([_layer_norm(h[r:r + LN_ROWS], lnw_ref[...], lnb_ref[...])
                          for r in range(0, tile, LN_ROWS)], axis=0)
    xp_ref[...] = _pack_halves(x1)
    x1_hi = x1.astype(BF16)
    x1_lo = (x1 - x1_hi.astype(F32)).astype(BF16)
    p4 = _dot(jnp.concatenate([x1_hi, x1_lo], axis=0), rw_ref[...])
    logits = (p4[:tile, :ROUTER_ROWS] + p4[:tile, ROUTER_ROWS:]) + (p4[tile:, :ROUTER_ROWS] + p4[tile:, ROUTER_ROWS:])
    e0, e1, w0, w1 = _route(jnp.transpose(logits) + rb_ref[...])
    eid_ref[...] = jnp.concatenate([e0, e1, jnp.zeros((SUBLANES - TOP_K, tile), I32)], axis=0)
    w_t = jnp.concatenate([w0, w1, jnp.zeros((WTOK_LANES - TOP_K, tile), F32)], axis=0)
    wtok_ref[...] = jnp.transpose(w_t)


def _mixer(xf, rope, kv, win, wgrp, pscale, gnw, wp, wr, wa, wo, lnw, lnb, rw, rb, batch, seq, mem_len,
           tile=512):
    T = xf.shape[0]
    nj = seq // tile
    cos, sin, cos_t, sin_t = rope
    bmain, bhalo = _pool_bands()
    dmask, qdec, kdec, cdec = _ret_consts()
    resident = lambda a: pl.BlockSpec(a.shape, lambda b, j: (0,) * a.ndim, pipeline_mode=pl.Buffered(1))
    rowblk = lambda w: pl.BlockSpec((tile, w), lambda b, j: (b * nj + j, 0))
    colblk = lambda r: pl.BlockSpec((r, tile), lambda b, j: (0, b * nj + j))
    consts = (win, wgrp, pscale, bmain, bhalo, dmask, qdec, kdec, gnw, wp, wr, wa, wo, lnw, lnb, rw, rb)
    return pl.pallas_call(
        functools.partial(_mixer_kernel, tile=tile, cdec=cdec),
        grid=(batch, nj),
        in_specs=[rowblk(D_MODEL), rowblk(RET_QK_DIM), rowblk(RET_QK_DIM),
                  colblk(RET_QK_DIM // 2), colblk(RET_QK_DIM // 2),
                  pl.BlockSpec((mem_len, XA_WIDTH), lambda b, j: (b, 0)),
                  pl.BlockSpec((mem_len, XA_WIDTH), lambda b, j: (b, 1))] + [resident(a) for a in consts],
        out_specs=[rowblk(PACK_W), colblk(SUBLANES), rowblk(WTOK_LANES)],
        out_shape=[jax.ShapeDtypeStruct((T, PACK_W), I32), jax.ShapeDtypeStruct((SUBLANES, T), I32),
                   jax.ShapeDtypeStruct((T, WTOK_LANES), F32)],
        scratch_shapes=[pltpu.VMEM((RET_HEADS, RET_QK_DIM, RET_V_DIM), F32),
                        pltpu.VMEM((2, POOL_HALO, POOL_WIDTH), BF16),
                        pltpu.VMEM((tile, POOL_WIDTH), BF16),
                        pltpu.VMEM((tile, RET_HEADS * RET_V_DIM), BF16),
                        pltpu.VMEM((tile, XA_WIDTH), BF16),
                        pltpu.VMEM((tile, RET_HEADS * RET_QK_DIM), BF16),
                        pltpu.VMEM((tile, RET_HEADS * RET_QK_DIM), BF16),
                        pltpu.VMEM((RET_HEADS * RET_QK_DIM, tile), BF16),
                        pltpu.VMEM((RET_HEADS * RET_QK_DIM, tile), BF16),
                        pltpu.VMEM((RET_HEADS * RET_QK_DIM, D_MODEL), BF16)],
        compiler_params=pltpu.CompilerParams(dimension_semantics=("arbitrary", "arbitrary"),
                                             vmem_limit_bytes=VMEM_LIMIT),
        name="mixer",
    )(xf, cos, sin, cos_t, sin_t, kv, kv, *consts)


META_LANES = LANES


def _positions_kernel(eid_ref, tri_ref, low_ref, pos_ref, first_tile_ref, n_tiles_ref, *, n_tok):
    n_chunks = n_tok // RANK_CHUNK
    erow = lax.broadcasted_iota(I32, (N_EXPERTS, RANK_CHUNK), 0)

    def onehot(c):
        sl = slice(c * RANK_CHUNK, (c + 1) * RANK_CHUNK)
        m0 = eid_ref[0:1, sl] == erow
        m1 = eid_ref[1:2, sl] == erow
        return m0, m1, jnp.where(m0, 1.0, 0.0) + jnp.where(m1, 1.0, 0.0)

    counts = jnp.zeros((N_EXPERTS, 1), F32)
    for c in range(n_chunks):
        counts = counts + jnp.sum(onehot(c)[2], axis=1, keepdims=True)
    ptiles = jnp.floor((counts + (MOE_TM - 1)) * (1.0 / MOE_TM))
    ptiles_b = jnp.broadcast_to(ptiles, (N_EXPERTS, LANES)).astype(BF16)
    start = _dot(low_ref[...], ptiles_b)[:, 0:1] * MOE_TM

    pos_ref[...] = jnp.zeros_like(pos_ref)
    carry = start - 1.0
    for c in range(n_chunks):
        sl = slice(c * RANK_CHUNK, (c + 1) * RANK_CHUNK)
        m0, m1, oh = onehot(c)
        rank = _dot(oh.astype(BF16), tri_ref[...]) + carry
        pos_ref[0:1, sl] = jnp.sum(jnp.where(m0, rank, 0.0), axis=0, keepdims=True).astype(I32)
        pos_ref[1:2, sl] = jnp.sum(jnp.where(m1, rank, 0.0), axis=0, keepdims=True).astype(I32)
        carry = carry + jnp.sum(oh, axis=1, keepdims=True)

    first_tile_ref[...] = jnp.broadcast_to(start * (1.0 / MOE_TM), first_tile_ref.shape).astype(I32)
    n_tiles_ref[...] = jnp.broadcast_to(ptiles, n_tiles_ref.shape).astype(I32)


def _positions(eid):
    T = eid.shape[1]
    r = np.arange(RANK_CHUNK)
    tri = jnp.asarray(r[:, None] <= r[None, :], BF16)
    e = np.arange(N_EXPERTS)
    low = jnp.asarray(e[None, :] < e[:, None], BF16)
    full = lambda a: pl.BlockSpec(a.shape, lambda i: (0,) * a.ndim)
    return pl.pallas_call(
        functools.partial(_positions_kernel, n_tok=T),
        grid=(1,),
        in_specs=[full(eid), full(tri), full(low)],
        out_specs=[pl.BlockSpec((SUBLANES, T), lambda i: (0, 0)),
                   pl.BlockSpec((N_EXPERTS, META_LANES), lambda i: (0, 0)),
                   pl.BlockSpec((N_EXPERTS, META_LANES), lambda i: (0, 0))],
        out_shape=[jax.ShapeDtypeStruct((SUBLANES, T), I32), jax.ShapeDtypeStruct((N_EXPERTS, META_LANES), I32),
                   jax.ShapeDtypeStruct((N_EXPERTS, META_LANES), I32)],
        name="route_positions",
    )(eid, tri, low)


def _sc_workers():
    info = plsc.get_sparse_core_info()
    return info.num_cores, info.num_cores * info.num_subcores


def _sc_scatter_rows(xp, pos2d, n_out):
    T, W = xp.shape
    n_cores, n_workers = _sc_workers()
    cpw = T // SC_CHUNK // n_workers
    mesh = plsc.VectorSubcoreMesh(core_axis_name="c", subcore_axis_name="s")

    @functools.partial(
        pl.kernel, mesh=mesh, out_type=jax.ShapeDtypeStruct((n_out, W), I32),
        scratch_types=[pltpu.VMEM((TOP_K * cpw, SC_CHUNK), I32), pltpu.VMEM((2, SC_CHUNK, W), I32),
                       pltpu.SemaphoreType.DMA((2,)), pltpu.SemaphoreType.DMA((2,))],
        name="sc_scatter_rows")
    def k(x_hbm, pos_hbm, out_hbm, idx_v, rows_v, rd_sem, wr_sem):
        wid = lax.axis_index("s") * n_cores + lax.axis_index("c")
        for s in range(TOP_K):
            pltpu.sync_copy(pos_hbm.at[pl.ds(s * (T // SC_CHUNK) + wid * cpw, cpw)],
                            idx_v.at[pl.ds(s * cpw, cpw)])

        def read(j):
            return pltpu.make_async_copy(x_hbm.at[pl.ds((wid * cpw + j) * SC_CHUNK, SC_CHUNK)],
                                         rows_v.at[j % 2], rd_sem.at[j % 2])

        def write(j, s):
            return pltpu.make_async_copy(rows_v.at[j % 2], out_hbm.at[idx_v.at[s * cpw + j]], wr_sem.at[j % 2])

        _sc_two_buffer_stream(cpw, read, lambda j: [write(j, s) for s in range(TOP_K)])

    return k(xp, pos2d)


def _sc_two_buffer_stream(n, read, writes):
    read(0).start()
    for j in range(n):
        read(j).wait()
        if j + 1 < n:
            if j >= 1:
                for w in writes(j - 1):
                    w.wait()
            read(j + 1).start()
        for w in writes(j):
            w.start()
    for j in range(max(n - 2, 0), n):
        for w in writes(j):
            w.wait()


def _sc_gather_rows(y, idx2d):
    W = y.shape[1]
    n = idx2d.shape[0] * SC_CHUNK
    n_cores, n_workers = _sc_workers()
    cpw = n // SC_CHUNK // n_workers
    mesh = plsc.VectorSubcoreMesh(core_axis_name="c", subcore_axis_name="s")

    @functools.partial(
        pl.kernel, mesh=mesh, out_type=jax.ShapeDtypeStruct((n, W), I32),
        scratch_types=[pltpu.VMEM((cpw, SC_CHUNK), I32), pltpu.VMEM((2, SC_CHUNK, W), I32),
                       pltpu.SemaphoreType.DMA((2,)), pltpu.SemaphoreType.DMA((2,))],
        name="sc_gather_rows")
    def k(y_hbm, idx_hbm, out_hbm, idx_v, rows_v, rd_sem, wr_sem):
        wid = lax.axis_index("s") * n_cores + lax.axis_index("c")
        pltpu.sync_copy(idx_hbm.at[pl.ds(wid * cpw, cpw)], idx_v)

        def read(j):
            return pltpu.make_async_copy(y_hbm.at[idx_v.at[j]], rows_v.at[j % 2], rd_sem.at[j % 2])

        def write(j):
            return pltpu.make_async_copy(rows_v.at[j % 2], out_hbm.at[pl.ds((wid * cpw + j) * SC_CHUNK, SC_CHUNK)],
                                         wr_sem.at[j % 2])

        _sc_two_buffer_stream(cpw, read, lambda j: [write(j)])

    return k(y, idx2d)


def _routed_kernel(first_ref, count_ref, xs_hbm, wg_ref, wu_ref, wd_ref, ys_hbm, xbuf, ybuf, in_sem, out_sem):
    e = pl.program_id(0)
    last = pl.num_programs(0) - 1
    total = first_ref[last] + count_ref[last]

    def in_copy(g):
        slot = lax.rem(g, RING)
        return pltpu.make_async_copy(xs_hbm.at[pl.ds(g * MOE_TM, MOE_TM)], xbuf.at[slot], in_sem.at[slot])

    def out_copy(g):
        slot = lax.rem(g, RING)
        return pltpu.make_async_copy(ybuf.at[slot], ys_hbm.at[pl.ds(g * MOE_TM, MOE_TM)], out_sem.at[slot])

    @pl.when(e == 0)
    def _():
        for g0 in range(RING - 1):
            @pl.when(g0 < total)
            def _():
                in_copy(g0).start()

    wg = wg_ref[0].astype(BF16)
    wu = wu_ref[0].astype(BF16)
    wd = wd_ref[0].astype(BF16)

    def tile_step(i, carry):
        g = first_ref[e] + i
        slot = lax.rem(g, RING)
        in_copy(g).wait()

        @pl.when(g + RING - 1 < total)
        def _():
            in_copy(g + RING - 1).start()

        @pl.when(g >= RING)
        def _():
            out_copy(g - RING).wait()

        lo, hi = _unpack_halves(xbuf[slot])
        lo = lo.astype(BF16)
        hi = hi.astype(BF16)
        a = _dot(lo, wg[:PACK_W]) + _dot(hi, wg[PACK_W:])
        b = _dot(lo, wu[:PACK_W]) + _dot(hi, wu[PACK_W:])
        act = (a * _sigmoid(a) * b).astype(BF16)
        ybuf[slot] = _pack_halves(_dot(act, wd))
        out_copy(g).start()
        return carry

    lax.fori_loop(0, count_ref[e], tile_step, 0)

    @pl.when(e == last)
    def _():
        for back in range(RING, 0, -1):
            @pl.when(total >= back)
            def _():
                out_copy(total - back).wait()


def _routed_mlp(first_tile, n_tiles, xs, wg, wu, wd):
    R = xs.shape[0]
    any_space = pl.BlockSpec(memory_space=pl.ANY)
    return pl.pallas_call(
        _routed_kernel,
        grid_spec=pltpu.PrefetchScalarGridSpec(
            num_scalar_prefetch=2,
            grid=(N_EXPERTS,),
            in_specs=[any_space,
                      pl.BlockSpec((1, D_MODEL, D_EXPERT), lambda e, ft, nt: (e, 0, 0)),
                      pl.BlockSpec((1, D_MODEL, D_EXPERT), lambda e, ft, nt: (e, 0, 0)),
                      pl.BlockSpec((1, D_EXPERT, D_MODEL), lambda e, ft, nt: (e, 0, 0))],
            out_specs=any_space,
            scratch_shapes=[pltpu.VMEM((RING, MOE_TM, PACK_W), I32), pltpu.VMEM((RING, MOE_TM, PACK_W), I32),
                            pltpu.SemaphoreType.DMA((RING,)), pltpu.SemaphoreType.DMA((RING,))]),
        out_shape=jax.ShapeDtypeStruct((R, PACK_W), I32),
        compiler_params=pltpu.CompilerParams(dimension_semantics=("arbitrary",)),
        name="routed_mlp",
    )(first_tile, n_tiles, xs, wg, wu, wd)


def _combine_kernel(xp_ref, y0_ref, y1_ref, wtok_ref, lnw_ref, lnb_ref, *out_refs):
    o_ref = out_refs[-1]
    w0 = wtok_ref[:, 0:1]
    w1 = wtok_ref[:, 1:2]
    xlo, xhi = _unpack_halves(xp_ref[...])
    y0lo, y0hi = _unpack_halves(y0_ref[...])
    y1lo, y1hi = _unpack_halves(y1_ref[...])
    h = jnp.concatenate([ALPHA * xlo + (w0 * y0lo + w1 * y1lo), ALPHA * xhi + (w0 * y0hi + w1 * y1hi)], axis=1)
    o_ref[...] = _layer_norm(h, lnw_ref[...], lnb_ref[...])


def _combine_ln2(xp, yg_parts, wtok, lnw, lnb, tile=1024):
    T = xp.shape[0]
    n_parts = len(yg_parts)
    nt = T // tile // n_parts
    full = lambda a: pl.BlockSpec(a.shape, lambda i: (0,) * a.ndim)
    out = None
    for p, yg in enumerate(yg_parts):
        rows = lambda w, p=p: pl.BlockSpec((tile, w), lambda i: (i + p * nt, 0))
        in_specs = [rows(PACK_W),
                    pl.BlockSpec((tile, PACK_W), lambda i: (i, 0)),
                    pl.BlockSpec((tile, PACK_W), lambda i: (i + nt, 0)),
                    rows(WTOK_LANES), full(lnw), full(lnb)]
        args = [xp, yg, yg, wtok, lnw, lnb]
        aliases = {}
        if out is not None:
            in_specs.append(pl.BlockSpec(memory_space=pl.ANY))
            args.append(out)
            aliases = {len(args) - 1: 0}
        out = pl.pallas_call(
            _combine_kernel,
            grid=(nt,),
            in_specs=in_specs,
            out_specs=rows(D_MODEL),
            out_shape=jax.ShapeDtypeStruct((T, D_MODEL), F32),
            input_output_aliases=aliases,
            compiler_params=pltpu.CompilerParams(dimension_semantics=("arbitrary",), vmem_limit_bytes=VMEM_LIMIT),
            name="combine_ln2",
        )(*args)
    return out


def _router_params(w_grp, b_grp, w_exp, b_exp):
    rw = jnp.zeros((D_MODEL, ROUTER_ROWS), F32)
    rw = rw.at[:, 0:N_GROUPS].set(w_grp).at[:, EXP_ROW0:EXP_ROW0 + N_EXPERTS].set(w_exp)
    rb = jnp.zeros((ROUTER_ROWS,), F32).at[N_GROUPS:8].set(NEG_BIG)
    rb = rb.at[0:N_GROUPS].set(b_grp).at[EXP_ROW0:EXP_ROW0 + N_EXPERTS].set(b_exp)
    rw_hi = rw.astype(BF16)
    rw_lo = (rw - rw_hi.astype(F32)).astype(BF16)
    return jnp.concatenate([rw_hi, rw_lo], axis=1), rb[:, None]


def kernel(x, mem, positions, w_in, w_pool_grp, pool_scale, ret_gn_w, w_mem_kv, w_br_pool, w_br_ret, w_br_xa,
           w_out, ln1_w, ln1_b, w_grp_router, b_grp_router, w_exp_router, b_exp_router, w_exp_gate, w_exp_up,
           w_exp_down, ln2_w, ln2_b):
    B, S, D = x.shape
    assert D == D_MODEL and w_in.shape[0] == DEPTH and S % 512 == 0
    T = B * S
    M = mem.shape[1]
    l = 0
    xf = x.reshape(T, D)

    rope = _rope_table(positions.reshape(1, T))
    kv = _mem_kv(mem.reshape(B * M, D), w_mem_kv[l].astype(BF16))
    rw, rb = _router_params(w_grp_router[l], b_grp_router[l], w_exp_router[l], b_exp_router[l])
    xp, eid, wtok = _mixer(xf, rope, kv, w_in[l].astype(BF16),
                           w_pool_grp[l].astype(BF16), pool_scale[l][None, :], ret_gn_w[l].reshape(1, -1),
                           w_br_pool[l].astype(BF16), w_br_ret[l].astype(BF16), w_br_xa[l].astype(BF16),
                           w_out[l].astype(BF16), ln1_w[l][None, :], ln1_b[l][None, :], rw, rb, B, S, M)

    pos, first_tile, n_tiles = _positions(eid)
    pos2d = pos[0:TOP_K].reshape(TOP_K * T // SC_CHUNK, SC_CHUNK)
    max_tiles = (TOP_K * T + N_EXPERTS * (MOE_TM - 1)) // MOE_TM
    xs = _sc_scatter_rows(xp, pos2d, max_tiles * MOE_TM)
    ys = _routed_mlp(first_tile[:, 0], n_tiles[:, 0], xs,
                     w_exp_gate[l].reshape(N_EXPERTS, D_MODEL, D_EXPERT),
                     w_exp_up[l].reshape(N_EXPERTS, D_MODEL, D_EXPERT),
                     w_exp_down[l].reshape(N_EXPERTS, D_EXPERT, D_MODEL))
    rng = T // COMBINE_PARTS
    yg_parts = [_sc_gather_rows(ys, pos[0:TOP_K, p * rng:(p + 1) * rng].reshape(TOP_K * rng // SC_CHUNK, SC_CHUNK))
                for p in range(COMBINE_PARTS)]
    out = _combine_ln2(xp, yg_parts, wtok, ln2_w[l][None, :], ln2_b[l][None, :])
    return out.reshape(B, S, D)
```

```python
import functools

import numpy as np
import jax
import jax.numpy as jnp
from jax import lax
from jax.experimental import pallas as pl
from jax.experimental.pallas import tpu as pltpu
from jax.experimental.pallas import tpu_sc as plsc

F32 = jnp.float32
BF16 = jnp.bfloat16
I32 = jnp.int32
U32 = jnp.uint32

D_MODEL = 1024
POOL_WINDOWS = (2, 4, 8, 16)
POOL_GROUP_DIM = 128
POOL_WIDTH = 512
POOL_HALO = 16
RET_HEADS = 4
RET_QK_DIM = 128
RET_V_DIM = 256
RET_CHUNK = 128
ROPE_BASE = 10000.0
XA_HEADS = 4
XA_HEAD_DIM = 128
XA_WIDTH = 512
N_GROUPS = 4
EXPERTS_PER_GROUP = 8
N_EXPERTS = N_GROUPS * EXPERTS_PER_GROUP
D_EXPERT = 256
LN_EPS = 1e-5
DEPTH = 1
ALPHA = (2.0 * DEPTH) ** 0.25
NEG_BIG = -1e30

COL_POOL, COL_Q, COL_K, COL_V, COL_G, COL_XAQ, COL_GATES = 0, 512, 1024, 1536, 2560, 3584, 4096

V7X_VMEM_BYTES = 64 * 1024 * 1024
VMEM_LIMIT = V7X_VMEM_BYTES * 7 // 8
SUBLANES = 8
LANES = 128

TOP_K = 2
PACK_W = D_MODEL // 2
MOE_TM = 512
SC_CHUNK = 64
RANK_CHUNK = 512
COMBINE_PARTS = 2
RING = 4
STRIP = 256
LN_ROWS = 32


def _dot(a, b):
    return jnp.dot(a, b, preferred_element_type=F32)


def _dot_nt(a, b):
    return lax.dot_general(a, b, (((1,), (1,)), ((), ())), preferred_element_type=F32)


def _sigmoid(z):
    return 1.0 / (1.0 + jnp.exp2(z * (-1.0 / np.log(2.0))))


def _layer_norm(h, w, b):
    mu = jnp.mean(h, axis=-1, keepdims=True)
    hc = h - mu
    var = jnp.mean(hc * hc, axis=-1, keepdims=True)
    return hc * lax.rsqrt(var + LN_EPS) * w + b


ROPE_LO = 64
ROPE_PARTS = 3


def _rope_kernel(pos_ref, freq_ref, tab_ref, cos_ref, sin_ref, cos_t_ref, sin_t_ref):
    pos = pos_ref[...]

    def emit(cos_t, sin_t):
        cos_t_ref[...] = cos_t
        sin_t_ref[...] = sin_t
        cos_ref[...] = jnp.transpose(jnp.concatenate([cos_t, cos_t], axis=0))
        sin_ref[...] = jnp.transpose(jnp.concatenate([-sin_t, sin_t], axis=0))

    in_table = jnp.logical_and(jnp.min(pos) >= 0, jnp.max(pos) < ROPE_LO * LANES)

    @pl.when(in_table)
    def _():
        idx = lax.broadcasted_iota(I32, (LANES, pos.shape[1]), 0)
        pick_hi = jnp.where(idx == jnp.right_shift(pos, ROPE_LO.bit_length() - 1), 1.0, 0.0).astype(BF16)
        pick_lo = jnp.where(idx == (pos & (ROPE_LO - 1)), 1.0, 0.0).astype(BF16)

        def look(k, pick):
            return sum(_dot(tab_ref[k * ROPE_PARTS + p], pick) for p in range(ROPE_PARTS))

        cos_a, sin_a, cos_b, sin_b = look(0, pick_hi), look(1, pick_hi), look(2, pick_lo), look(3, pick_lo)
        emit(cos_a * cos_b - sin_a * sin_b, sin_a * cos_b + cos_a * sin_b)

    @pl.when(jnp.logical_not(in_table))
    def _():
        ang = freq_ref[...] * pos.astype(F32)
        emit(jnp.cos(ang), jnp.sin(ang))


def _rope_tables(inv_freq):
    f = inv_freq.astype(np.float64)[:, None]
    idx = np.arange(LANES, dtype=np.float64)[None, :]
    tabs = [np.cos(ROPE_LO * idx * f), np.sin(ROPE_LO * idx * f), np.cos(idx * f), np.sin(idx * f)]
    pieces = []
    for t in tabs:
        rest = t.astype(np.float32)
        for _ in range(ROPE_PARTS):
            piece = rest.astype(BF16)
            pieces.append(piece)
            rest = rest - piece.astype(np.float32)
    return jnp.asarray(np.stack(pieces))


def _rope_table(pos_row, tile=2048):
    T = pos_row.shape[1]
    half = RET_QK_DIM // 2
    inv_freq = (ROPE_BASE ** (-np.arange(half, dtype=np.float64) / half)).astype(np.float32)
    freq = jnp.asarray(inv_freq[:, None])
    tabs = _rope_tables(inv_freq)
    out = pl.BlockSpec((tile, RET_QK_DIM), lambda i: (i, 0))
    out_t = pl.BlockSpec((half, tile), lambda i: (0, i))
    return pl.pallas_call(
        _rope_kernel,
        grid=(T // tile,),
        in_specs=[pl.BlockSpec((1, tile), lambda i: (0, i)), pl.BlockSpec((half, 1), lambda i: (0, 0)),
                  pl.BlockSpec(tabs.shape, lambda i: (0, 0, 0))],
        out_specs=[out, out, out_t, out_t],
        out_shape=[jax.ShapeDtypeStruct((T, RET_QK_DIM), F32)] * 2 + [jax.ShapeDtypeStruct((half, T), F32)] * 2,
        name="rope_table",
    )(pos_row, freq, tabs)


POOL_SUB = 256


def _pool_bands():
    r = np.arange(POOL_SUB)[:, None]
    c = np.arange(POOL_SUB)[None, :]
    ch = np.arange(POOL_HALO)[None, :] - POOL_HALO
    main = np.stack([((r - c >= 0) & (r - c < w)) for w in POOL_WINDOWS]).astype(np.float32)
    halo = np.stack([((r - ch >= 0) & (r - ch < w)) for w in POOL_WINDOWS]).astype(np.float32)
    return jnp.asarray(main, BF16), jnp.asarray(halo, BF16)


def _pool_branch(ub, j, halo_ref, bmain_ref, bhalo_ref, o_ref, filler):
    tile = ub.shape[0]
    s0 = j * tile
    slot = lax.rem(j, 2)
    blocks = [(sb * POOL_SUB, g) for sb in range(tile // POOL_SUB) for g in range(len(POOL_WINDOWS))]
    wsum = {}
    for r0, g in blocks:
        cols = slice(g * POOL_GROUP_DIM, (g + 1) * POOL_GROUP_DIM)
        prev = halo_ref[slot] if r0 == 0 else ub[r0 - POOL_HALO:r0]
        wsum[r0, g] = _dot(bmain_ref[g], ub[r0:r0 + POOL_SUB, cols]) + _dot(bhalo_ref[g], prev[:, cols])
    filler()
    for r0, g in blocks:
        cols = slice(g * POOL_GROUP_DIM, (g + 1) * POOL_GROUP_DIM)
        pos = s0 + r0 + lax.broadcasted_iota(I32, (POOL_SUB, POOL_GROUP_DIM), 0)
        cnt = jnp.minimum(pos + 1, POOL_WINDOWS[g]).astype(F32)
        pooled = wsum[r0, g] / cnt - ub[r0:r0 + POOL_SUB, cols].astype(F32)
        o_ref[r0:r0 + POOL_SUB, cols] = pooled.astype(BF16)
    halo_ref[1 - slot] = ub[tile - POOL_HALO:tile]


def _ret_consts():
    h = np.arange(RET_HEADS, dtype=np.float64)
    log_gamma = np.log1p(-np.exp2(-5.0 - h))
    pos = np.arange(RET_CHUNK, dtype=np.float64)
    diff = pos[:, None] - pos[None, :]
    kscale = RET_QK_DIM ** -0.5
    dmask = kscale * np.where(diff >= 0, np.exp(log_gamma[:, None, None] * np.maximum(diff, 0.0)), 0.0)
    qdec = np.exp(log_gamma[:, None] * (pos + 1.0)[None, :])
    kdec = kscale * np.exp(log_gamma[:, None] * (RET_CHUNK - 1.0 - pos)[None, :])
    cdec = np.exp(log_gamma * RET_CHUNK)
    lanes = lambda a: np.broadcast_to(a[:, :, None], (RET_HEADS, RET_CHUNK, RET_QK_DIM))
    kdec_t = np.broadcast_to(kdec[:, None, :], (RET_HEADS, RET_QK_DIM, RET_CHUNK))
    return (jnp.asarray(dmask, F32), jnp.asarray(lanes(qdec), F32), jnp.asarray(kdec_t, F32),
            tuple(float(v) for v in cdec))


def _retention_branch(q, k_t, v, silu_g, cos_ref, sin_ref, cos_t_ref, sin_t_ref, dmask_ref, qdec_ref, kdec_ref,
                      gnw_ref, state_ref, rq_ref, rqd_ref, rkt_ref, rkdt_ref, o_ref, cdec, fillers):
    tile = q.shape[0]
    n_chunks = tile // RET_CHUNK
    half = RET_QK_DIM // 2
    cos = cos_ref[...]
    sin = sin_ref[...]
    cos_t = cos_t_ref[...]
    sin_t = sin_t_ref[...]
    for h in range(RET_HEADS):
        qk = slice(h * RET_QK_DIM, (h + 1) * RET_QK_DIM)
        qh = q[:, qk]
        qr = qh * cos + pltpu.roll(qh, half, 1) * sin
        rq_ref[:, qk] = qr.astype(BF16)
        rqd_ref[:, qk] = (qr * jnp.concatenate([qdec_ref[h]] * n_chunks, axis=0)).astype(BF16)
        k1 = k_t[h * RET_QK_DIM:h * RET_QK_DIM + half]
        k2 = k_t[h * RET_QK_DIM + half:(h + 1) * RET_QK_DIM]
        kr_t = jnp.concatenate([k1 * cos_t - k2 * sin_t, k2 * cos_t + k1 * sin_t], axis=0)
        rkt_ref[qk, :] = kr_t.astype(BF16)
        rkdt_ref[qk, :] = (kr_t * jnp.concatenate([kdec_ref[h]] * n_chunks, axis=1)).astype(BF16)

    chunks = [slice(c * RET_CHUNK, (c + 1) * RET_CHUNK) for c in range(n_chunks)]
    heads = [slice(h * RET_QK_DIM, (h + 1) * RET_QK_DIM) for h in range(RET_HEADS)]
    v_heads = [slice(h * RET_V_DIM, (h + 1) * RET_V_DIM) for h in range(RET_HEADS)]
    raw = {(c, h): _dot(rq_ref[chunks[c], heads[h]], rkt_ref[heads[h], chunks[c]])
           for c in range(n_chunks) for h in range(RET_HEADS)}
    incr = {(c, h): _dot(rkdt_ref[heads[h], chunks[c]], v[chunks[c], v_heads[h]])
            for c in range(n_chunks) for h in range(RET_HEADS)}
    state_in = {}
    for h in range(RET_HEADS):
        st = state_ref[h]
        for c in range(n_chunks):
            state_in[c, h] = st.astype(BF16)
            st = cdec[h] * st + incr[c, h]
        state_ref[h] = st

    for c in range(n_chunks):
        rows = chunks[c]
        fillers[c]()
        for h in range(RET_HEADS):
            qk = heads[h]
            v_cols = v_heads[h]
            scores = raw[c, h] * dmask_ref[h]
            lhs = jnp.concatenate([scores.astype(BF16), rqd_ref[rows, qk]], axis=1)
            y = _dot(lhs, jnp.concatenate([v[rows, v_cols], state_in[c, h]], axis=0))
            mu = jnp.mean(y, axis=-1, keepdims=True)
            yc = y - mu
            var = jnp.mean(yc * yc, axis=-1, keepdims=True)
            yn = yc * lax.rsqrt(var + LN_EPS) * gnw_ref[:, v_cols]
            o_ref[rows, v_cols] = (silu_g[h][rows] * yn).astype(BF16)


def _memkv_kernel(m_ref, w_ref, o_ref):
    o_ref[...] = _dot(m_ref[...].astype(BF16), w_ref[...]).astype(BF16)


def _mem_kv(memf, w_b):
    M, D = memf.shape
    N = w_b.shape[1]
    return pl.pallas_call(
        _memkv_kernel,
        grid=(1,),
        in_specs=[pl.BlockSpec((M, D), lambda i: (0, 0)), pl.BlockSpec((D, N), lambda i: (0, 0))],
        out_specs=pl.BlockSpec((M, N), lambda i: (0, 0)),
        out_shape=jax.ShapeDtypeStruct((M, N), BF16),
        name="mem_kv",
    )(memf, w_b)


def _cross_attention_branch(xq, k_ref, v_ref, o_ref):
    scale = XA_HEAD_DIM ** -0.5
    for h in range(XA_HEADS):
        cols = slice(h * XA_HEAD_DIM, (h + 1) * XA_HEAD_DIM)
        s = _dot_nt(xq[:, cols], k_ref[:, cols]) * scale
        m = jnp.max(s, axis=-1, keepdims=True)
        p = jnp.exp(s - m)
        l = jnp.sum(p, axis=-1, keepdims=True)
        o = _dot(p.astype(BF16), v_ref[:, cols]) / l
        o_ref[:, cols] = o.astype(BF16)


ROUTER_ROWS = 128
WTOK_LANES = 128
EXP_ROW0 = 8


def _route(logits_t):
    gl = logits_t[0:8]
    gmax = jnp.max(gl, axis=0, keepdims=True)
    p_grp = 1.0 / jnp.sum(jnp.exp(gl - gmax), axis=0, keepdims=True)
    idx8 = lax.broadcasted_iota(jnp.int32, gl.shape, 0)
    gsel = jnp.min(jnp.where(gl == gmax, idx8, 8), axis=0, keepdims=True)
    cl = jnp.zeros_like(gl)
    for g in range(N_GROUPS):
        r0 = EXP_ROW0 + g * EXPERTS_PER_GROUP
        cl = cl + jnp.where(gsel == g, logits_t[r0:r0 + EXPERTS_PER_GROUP], 0.0)
    v1 = jnp.max(cl, axis=0, keepdims=True)
    i1 = jnp.min(jnp.where(cl == v1, idx8, 8), axis=0, keepdims=True)
    cl2 = jnp.where(idx8 == i1, -jnp.inf, cl)
    v2 = jnp.max(cl2, axis=0, keepdims=True)
    i2 = jnp.min(jnp.where(cl2 == v2, idx8, 8), axis=0, keepdims=True)
    e21 = jnp.exp(v2 - v1)
    w1 = p_grp / (1.0 + e21)
    w2 = p_grp * e21 / (1.0 + e21)
    return gsel * EXPERTS_PER_GROUP + i1, gsel * EXPERTS_PER_GROUP + i2, w1, w2


def _pack_halves(v):
    half = v.shape[1] // 2
    lo = lax.bitcast_convert_type(v[:, :half].astype(BF16).astype(F32), U32)
    hi = lax.bitcast_convert_type(v[:, half:].astype(BF16).astype(F32), U32)
    return lax.bitcast_convert_type(lax.shift_right_logical(lo, U32(16)) | hi, I32)


def _unpack_halves(w):
    u = lax.bitcast_convert_type(w, U32)
    lo = lax.bitcast_convert_type(lax.shift_left(u, U32(16)), F32)
    hi = lax.bitcast_convert_type(u & U32(0xFFFF0000), F32)
    return lo, hi


def _mixer_kernel(x_ref, cos_ref, sin_ref, cos_t_ref, sin_t_ref, km_ref, vm_ref, win_ref, wgrp_ref,
                  pscale_ref, bmain_ref, bhalo_ref, dmask_ref, qdec_ref, kdec_ref, gnw_ref, wp_ref, wr_ref,
                  wa_ref, wo_ref, lnw_ref, lnb_ref, rw_ref, rb_ref, xp_ref, eid_ref, wtok_ref,
                  state_ref, halo_ref, ypool_ref, yret_ref, yxa_ref, rq_ref, rqd_ref, rkt_ref, rkdt_ref, wkt_ref,
                  wpool_ref, *, tile, cdec):
    j = pl.program_id(1)

    @pl.when(jnp.logical_and(pl.program_id(0) == 0, j == 0))
    def _():
        wk = win_ref[:, COL_K:COL_K + RET_HEADS * RET_QK_DIM].astype(F32)
        wkt_ref[...] = jnp.transpose(wk).astype(BF16)
        for g in range(len(POOL_WINDOWS)):
            rows = slice(g * POOL_GROUP_DIM, (g + 1) * POOL_GROUP_DIM)
            wg = (wgrp_ref[g].astype(F32) * pscale_ref[:, rows]).astype(BF16)
            wpool_ref[rows, :] = _dot(wg, wp_ref[rows, :]).astype(BF16)

    @pl.when(j == 0)
    def _():
        state_ref[...] = jnp.zeros_like(state_ref)
        halo_ref[...] = jnp.zeros_like(halo_ref)

    x = x_ref[...]
    xb = x.astype(BF16)

    def proj(col, width):
        return _dot(xb, win_ref[:, col:col + width])

    part = {}
    strips = [slice(c, c + STRIP) for c in range(0, D_MODEL, STRIP)]

    def gate(branch, cols):
        return _sigmoid(proj(COL_GATES + branch * D_MODEL + cols.start, STRIP))

    def pool_part():
        def pool_gates():
            part["pool_gate"] = [gate(0, c) for c in strips]

        _pool_branch(proj(COL_POOL, POOL_WIDTH).astype(BF16), j, halo_ref, bmain_ref, bhalo_ref, ypool_ref,
                     pool_gates)
        part["pool"] = [part["pool_gate"][i] * _dot(ypool_ref[...], wpool_ref[:, c]) for i, c in enumerate(strips)]

    def xa_part():
        _cross_attention_branch(proj(COL_XAQ, XA_WIDTH).astype(BF16), km_ref, vm_ref, yxa_ref)
        part["xa"] = [gate(2, c) * _dot(yxa_ref[...], wa_ref[:, c]) for c in strips]

    def ret_gate_part():
        part["ret_gate"] = [gate(1, c) for c in strips]

    silu_g = []
    for h in range(RET_HEADS):
        gh = proj(COL_G + h * RET_V_DIM, RET_V_DIM)
        silu_g.append(gh * _sigmoid(gh))
    fillers = [pool_part, xa_part, ret_gate_part] + [lambda: None] * (tile // RET_CHUNK - 3)
    _retention_branch(proj(COL_Q, RET_HEADS * RET_QK_DIM), _dot_nt(wkt_ref[...], xb),
                      proj(COL_V, RET_HEADS * RET_V_DIM).astype(BF16), silu_g, cos_ref, sin_ref, cos_t_ref,
                      sin_t_ref, dmask_ref, qdec_ref, kdec_ref, gnw_ref, state_ref, rq_ref, rqd_ref, rkt_ref,
                      rkdt_ref, yret_ref, cdec, fillers)
    merged = jnp.concatenate(
        [(part["pool"][i] + part["ret_gate"][i] * _dot(yret_ref[...], wr_ref[:, c]) + part["xa"][i]).astype(BF16)
         for i, c in enumerate(strips)], axis=1)
    h = jnp.concatenate([ALPHA * x[:, c] + _dot(merged, wo_ref[:, c]) for c in strips], axis=1)
    x1 = jnp.concatenate([_layer_norm(h[r:r + LN_ROWS], lnw_ref[...], lnb_ref[...])
                          for r in range(0, tile, LN_ROWS)], axis=0)
    xp_ref[...] = _pack_halves(x1)
    x1_hi = x1.astype(BF16)
    x1_lo = (x1 - x1_hi.astype(F32)).astype(BF16)
    p4 = _dot(jnp.concatenate([x1_hi, x1_lo], axis=0), rw_ref[...])
    logits = (p4[:tile, :ROUTER_ROWS] + p4[:tile, ROUTER_ROWS:]) + (p4[tile:, :ROUTER_ROWS] + p4[tile:, ROUTER_ROWS:])
    e0, e1, w0, w1 = _route(jnp.transpose(logits) + rb_ref[...])
    eid_ref[...] = jnp.concatenate([e0, e1, jnp.zeros((SUBLANES - TOP_K, tile), I32)], axis=0)
    w_t = jnp.concatenate([w0, w1, jnp.zeros((WTOK_LANES - TOP_K, tile), F32)], axis=0)
    wtok_ref[...] = jnp.transpose(w_t)


def _mixer(xf, rope, kv, win, wgrp, pscale, gnw, wp, wr, wa, wo, lnw, lnb, rw, rb, batch, seq, mem_len,
           tile=512):
    T = xf.shape[0]
    nj = seq // tile
    cos, sin, cos_t, sin_t = rope
    bmain, bhalo = _pool_bands()
    dmask, qdec, kdec, cdec = _ret_consts()
    resident = lambda a: pl.BlockSpec(a.shape, lambda b, j: (0,) * a.ndim, pipeline_mode=pl.Buffered(1))
    rowblk = lambda w: pl.BlockSpec((tile, w), lambda b, j: (b * nj + j, 0))
    colblk = lambda r: pl.BlockSpec((r, tile), lambda b, j: (0, b * nj + j))
    consts = (win, wgrp, pscale, bmain, bhalo, dmask, qdec, kdec, gnw, wp, wr, wa, wo, lnw, lnb, rw, rb)
    return pl.pallas_call(
        functools.partial(_mixer_kernel, tile=tile, cdec=cdec),
        grid=(batch, nj),
        in_specs=[rowblk(D_MODEL), rowblk(RET_QK_DIM), rowblk(RET_QK_DIM),
                  colblk(RET_QK_DIM // 2), colblk(RET_QK_DIM // 2),
                  pl.BlockSpec((mem_len, XA_WIDTH), lambda b, j: (b, 0)),
                  pl.BlockSpec((mem_len, XA_WIDTH), lambda b, j: (b, 1))] + [resident(a) for a in consts],
        out_specs=[rowblk(PACK_W), colblk(SUBLANES), rowblk(WTOK_LANES)],
        out_shape=[jax.ShapeDtypeStruct((T, PACK_W), I32), jax.ShapeDtypeStruct((SUBLANES, T), I32),
                   jax.ShapeDtypeStruct((T, WTOK_LANES), F32)],
        scratch_shapes=[pltpu.VMEM((RET_HEADS, RET_QK_DIM, RET_V_DIM), F32),
                        pltpu.VMEM((2, POOL_HALO, POOL_WIDTH), BF16),
                        pltpu.VMEM((tile, POOL_WIDTH), BF16),
                        pltpu.VMEM((tile, RET_HEADS * RET_V_DIM), BF16),
                        pltpu.VMEM((tile, XA_WIDTH), BF16),
                        pltpu.VMEM((tile, RET_HEADS * RET_QK_DIM), BF16),
                        pltpu.VMEM((tile, RET_HEADS * RET_QK_DIM), BF16),
                        pltpu.VMEM((RET_HEADS * RET_QK_DIM, tile), BF16),
                        pltpu.VMEM((RET_HEADS * RET_QK_DIM, tile), BF16),
                        pltpu.VMEM((RET_HEADS * RET_QK_DIM, D_MODEL), BF16),
                        pltpu.VMEM((POOL_WIDTH, D_MODEL), BF16)],
        compiler_params=pltpu.CompilerParams(dimension_semantics=("arbitrary", "arbitrary"),
                                             vmem_limit_bytes=VMEM_LIMIT),
        name="mixer",
    )(xf, cos, sin, cos_t, sin_t, kv, kv, *consts)


META_LANES = LANES


def _positions_kernel(eid_ref, tri_ref, low_ref, pos_ref, first_tile_ref, n_tiles_ref, *, n_tok):
    n_chunks = n_tok // RANK_CHUNK
    erow = lax.broadcasted_iota(I32, (N_EXPERTS, RANK_CHUNK), 0)

    def onehot(c):
        sl = slice(c * RANK_CHUNK, (c + 1) * RANK_CHUNK)
        m0 = eid_ref[0:1, sl] == erow
        m1 = eid_ref[1:2, sl] == erow
        return m0, m1, jnp.where(m0, 1.0, 0.0) + jnp.where(m1, 1.0, 0.0)

    counts = jnp.zeros((N_EXPERTS, 1), F32)
    for c in range(n_chunks):
        counts = counts + jnp.sum(onehot(c)[2], axis=1, keepdims=True)
    ptiles = jnp.floor((counts + (MOE_TM - 1)) * (1.0 / MOE_TM))
    ptiles_b = jnp.broadcast_to(ptiles, (N_EXPERTS, LANES)).astype(BF16)
    start = _dot(low_ref[...], ptiles_b)[:, 0:1] * MOE_TM

    pos_ref[...] = jnp.zeros_like(pos_ref)
    carry = start - 1.0
    for c in range(n_chunks):
        sl = slice(c * RANK_CHUNK, (c + 1) * RANK_CHUNK)
        m0, m1, oh = onehot(c)
        rank = _dot(oh.astype(BF16), tri_ref[...]) + carry
        pos_ref[0:1, sl] = jnp.sum(jnp.where(m0, rank, 0.0), axis=0, keepdims=True).astype(I32)
        pos_ref[1:2, sl] = jnp.sum(jnp.where(m1, rank, 0.0), axis=0, keepdims=True).astype(I32)
        carry = carry + jnp.sum(oh, axis=1, keepdims=True)

    first_tile_ref[...] = jnp.broadcast_to(start * (1.0 / MOE_TM), first_tile_ref.shape).astype(I32)
    n_tiles_ref[...] = jnp.broadcast_to(ptiles, n_tiles_ref.shape).astype(I32)


def _positions(eid):
    T = eid.shape[1]
    r = np.arange(RANK_CHUNK)
    tri = jnp.asarray(r[:, None] <= r[None, :], BF16)
    e = np.arange(N_EXPERTS)
    low = jnp.asarray(e[None, :] < e[:, None], BF16)
    full = lambda a: pl.BlockSpec(a.shape, lambda i: (0,) * a.ndim)
    return pl.pallas_call(
        functools.partial(_positions_kernel, n_tok=T),
        grid=(1,),
        in_specs=[full(eid), full(tri), full(low)],
        out_specs=[pl.BlockSpec((SUBLANES, T), lambda i: (0, 0)),
                   pl.BlockSpec((N_EXPERTS, META_LANES), lambda i: (0, 0)),
                   pl.BlockSpec((N_EXPERTS, META_LANES), lambda i: (0, 0))],
        out_shape=[jax.ShapeDtypeStruct((SUBLANES, T), I32), jax.ShapeDtypeStruct((N_EXPERTS, META_LANES), I32),
                   jax.ShapeDtypeStruct((N_EXPERTS, META_LANES), I32)],
        name="route_positions",
    )(eid, tri, low)


def _sc_workers():
    info = plsc.get_sparse_core_info()
    return info.num_cores, info.num_cores * info.num_subcores


def _sc_scatter_rows(xp, pos2d, n_out):
    T, W = xp.shape
    n_cores, n_workers = _sc_workers()
    cpw = T // SC_CHUNK // n_workers
    mesh = plsc.VectorSubcoreMesh(core_axis_name="c", subcore_axis_name="s")

    @functools.partial(
        pl.kernel, mesh=mesh, out_type=jax.ShapeDtypeStruct((n_out, W), I32),
        scratch_types=[pltpu.VMEM((TOP_K * cpw, SC_CHUNK), I32), pltpu.VMEM((2, SC_CHUNK, W), I32),
                       pltpu.SemaphoreType.DMA((2,)), pltpu.SemaphoreType.DMA((2,))],
        name="sc_scatter_rows")
    def k(x_hbm, pos_hbm, out_hbm, idx_v, rows_v, rd_sem, wr_sem):
        wid = lax.axis_index("s") * n_cores + lax.axis_index("c")
        for s in range(TOP_K):
            pltpu.sync_copy(pos_hbm.at[pl.ds(s * (T // SC_CHUNK) + wid * cpw, cpw)],
                            idx_v.at[pl.ds(s * cpw, cpw)])

        def read(j):
            return pltpu.make_async_copy(x_hbm.at[pl.ds((wid * cpw + j) * SC_CHUNK, SC_CHUNK)],
                                         rows_v.at[j % 2], rd_sem.at[j % 2])

        def write(j, s):
            return pltpu.make_async_copy(rows_v.at[j % 2], out_hbm.at[idx_v.at[s * cpw + j]], wr_sem.at[j % 2])

        _sc_two_buffer_stream(cpw, read, lambda j: [write(j, s) for s in range(TOP_K)])

    return k(xp, pos2d)


def _sc_two_buffer_stream(n, read, writes):
    read(0).start()
    for j in range(n):
        read(j).wait()
        if j + 1 < n:
            if j >= 1:
                for w in writes(j - 1):
                    w.wait()
            read(j + 1).start()
        for w in writes(j):
            w.start()
    for j in range(max(n - 2, 0), n):
        for w in writes(j):
            w.wait()


def _sc_gather_rows(y, idx2d):
    W = y.shape[1]
    n = idx2d.shape[0] * SC_CHUNK
    n_cores, n_workers = _sc_workers()
    cpw = n // SC_CHUNK // n_workers
    mesh = plsc.VectorSubcoreMesh(core_axis_name="c", subcore_axis_name="s")

    @functools.partial(
        pl.kernel, mesh=mesh, out_type=jax.ShapeDtypeStruct((n, W), I32),
        scratch_types=[pltpu.VMEM((cpw, SC_CHUNK), I32), pltpu.VMEM((2, SC_CHUNK, W), I32),
                       pltpu.SemaphoreType.DMA((2,)), pltpu.SemaphoreType.DMA((2,))],
        name="sc_gather_rows")
    def k(y_hbm, idx_hbm, out_hbm, idx_v, rows_v, rd_sem, wr_sem):
        wid = lax.axis_index("s") * n_cores + lax.axis_index("c")
        pltpu.sync_copy(idx_hbm.at[pl.ds(wid * cpw, cpw)], idx_v)

        def read(j):
            return pltpu.make_async_copy(y_hbm.at[idx_v.at[j]], rows_v.at[j % 2], rd_sem.at[j % 2])

        def write(j):
            return pltpu.make_async_copy(rows_v.at[j % 2], out_hbm.at[pl.ds((wid * cpw + j) * SC_CHUNK, SC_CHUNK)],
                                         wr_sem.at[j % 2])

        _sc_two_buffer_stream(cpw, read, lambda j: [write(j)])

    return k(y, idx2d)


def _routed_kernel(first_ref, count_ref, xs_hbm, wg_ref, wu_ref, wd_ref, ys_hbm, xbuf, ybuf, in_sem, out_sem):
    e = pl.program_id(0)
    last = pl.num_programs(0) - 1
    total = first_ref[last] + count_ref[last]

    def in_copy(g):
        slot = lax.rem(g, RING)
        return pltpu.make_async_copy(xs_hbm.at[pl.ds(g * MOE_TM, MOE_TM)], xbuf.at[slot], in_sem.at[slot])

    def out_copy(g):
        slot = lax.rem(g, RING)
        return pltpu.make_async_copy(ybuf.at[slot], ys_hbm.at[pl.ds(g * MOE_TM, MOE_TM)], out_sem.at[slot])

    @pl.when(e == 0)
    def _():
        for g0 in range(RING - 1):
            @pl.when(g0 < total)
            def _():
                in_copy(g0).start()

    wg = wg_ref[0].astype(BF16)
    wu = wu_ref[0].astype(BF16)
    wd = wd_ref[0].astype(BF16)

    def tile_step(i, carry):
        g = first_ref[e] + i
        slot = lax.rem(g, RING)
        in_copy(g).wait()

        @pl.when(g + RING - 1 < total)
        def _():
            in_copy(g + RING - 1).start()

        @pl.when(g >= RING)
        def _():
            out_copy(g - RING).wait()

        lo, hi = _unpack_halves(xbuf[slot])
        lo = lo.astype(BF16)
        hi = hi.astype(BF16)
        a = _dot(lo, wg[:PACK_W]) + _dot(hi, wg[PACK_W:])
        b = _dot(lo, wu[:PACK_W]) + _dot(hi, wu[PACK_W:])
        act = (a * _sigmoid(a) * b).astype(BF16)
        ybuf[slot] = _pack_halves(_dot(act, wd))
        out_copy(g).start()
        return carry

    lax.fori_loop(0, count_ref[e], tile_step, 0)

    @pl.when(e == last)
    def _():
        for back in range(RING, 0, -1):
            @pl.when(total >= back)
            def _():
                out_copy(total - back).wait()


def _routed_mlp(first_tile, n_tiles, xs, wg, wu, wd):
    R = xs.shape[0]
    any_space = pl.BlockSpec(memory_space=pl.ANY)
    return pl.pallas_call(
        _routed_kernel,
        grid_spec=pltpu.PrefetchScalarGridSpec(
            num_scalar_prefetch=2,
            grid=(N_EXPERTS,),
            in_specs=[any_space,
                      pl.BlockSpec((1, D_MODEL, D_EXPERT), lambda e, ft, nt: (e, 0, 0)),
                      pl.BlockSpec((1, D_MODEL, D_EXPERT), lambda e, ft, nt: (e, 0, 0)),
                      pl.BlockSpec((1, D_EXPERT, D_MODEL), lambda e, ft, nt: (e, 0, 0))],
            out_specs=any_space,
            scratch_shapes=[pltpu.VMEM((RING, MOE_TM, PACK_W), I32), pltpu.VMEM((RING, MOE_TM, PACK_W), I32),
                            pltpu.SemaphoreType.DMA((RING,)), pltpu.SemaphoreType.DMA((RING,))]),
        out_shape=jax.ShapeDtypeStruct((R, PACK_W), I32),
        compiler_params=pltpu.CompilerParams(dimension_semantics=("arbitrary",)),
        name="routed_mlp",
    )(first_tile, n_tiles, xs, wg, wu, wd)


def _combine_kernel(xp_ref, y0_ref, y1_ref, wtok_ref, lnw_ref, lnb_ref, *out_refs):
    o_ref = out_refs[-1]
    w0 = wtok_ref[:, 0:1]
    w1 = wtok_ref[:, 1:2]
    xlo, xhi = _unpack_halves(xp_ref[...])
    y0lo, y0hi = _unpack_halves(y0_ref[...])
    y1lo, y1hi = _unpack_halves(y1_ref[...])
    h = jnp.concatenate([ALPHA * xlo + (w0 * y0lo + w1 * y1lo), ALPHA * xhi + (w0 * y0hi + w1 * y1hi)], axis=1)
    o_ref[...] = _layer_norm(h, lnw_ref[...], lnb_ref[...])


def _combine_ln2(xp, yg_parts, wtok, lnw, lnb, tile=1024):
    T = xp.shape[0]
    n_parts = len(yg_parts)
    nt = T // tile // n_parts
    full = lambda a: pl.BlockSpec(a.shape, lambda i: (0,) * a.ndim)
    out = None
    for p, yg in enumerate(yg_parts):
        rows = lambda w, p=p: pl.BlockSpec((tile, w), lambda i: (i + p * nt, 0))
        in_specs = [rows(PACK_W),
                    pl.BlockSpec((tile, PACK_W), lambda i: (i, 0)),
                    pl.BlockSpec((tile, PACK_W), lambda i: (i + nt, 0)),
                    rows(WTOK_LANES), full(lnw), full(lnb)]
        args = [xp, yg, yg, wtok, lnw, lnb]
        aliases = {}
        if out is not None:
            in_specs.append(pl.BlockSpec(memory_space=pl.ANY))
            args.append(out)
            aliases = {len(args) - 1: 0}
        out = pl.pallas_call(
            _combine_kernel,
            grid=(nt,),
            in_specs=in_specs,
            out_specs=rows(D_MODEL),
            out_shape=jax.ShapeDtypeStruct((T, D_MODEL), F32),
            input_output_aliases=aliases,
            compiler_params=pltpu.CompilerParams(dimension_semantics=("arbitrary",), vmem_limit_bytes=VMEM_LIMIT),
            name="combine_ln2",
        )(*args)
    return out


def _router_params(w_grp, b_grp, w_exp, b_exp):
    rw = jnp.zeros((D_MODEL, ROUTER_ROWS), F32)
    rw = rw.at[:, 0:N_GROUPS].set(w_grp).at[:, EXP_ROW0:EXP_ROW0 + N_EXPERTS].set(w_exp)
    rb = jnp.zeros((ROUTER_ROWS,), F32).at[N_GROUPS:8].set(NEG_BIG)
    rb = rb.at[0:N_GROUPS].set(b_grp).at[EXP_ROW0:EXP_ROW0 + N_EXPERTS].set(b_exp)
    rw_hi = rw.astype(BF16)
    rw_lo = (rw - rw_hi.astype(F32)).astype(BF16)
    return jnp.concatenate([rw_hi, rw_lo], axis=1), rb[:, None]


def kernel(x, mem, positions, w_in, w_pool_grp, pool_scale, ret_gn_w, w_mem_kv, w_br_pool, w_br_ret, w_br_xa,
           w_out, ln1_w, ln1_b, w_grp_router, b_grp_router, w_exp_router, b_exp_router, w_exp_gate, w_exp_up,
           w_exp_down, ln2_w, ln2_b):
    B, S, D = x.shape
    assert D == D_MODEL and w_in.shape[0] == DEPTH and S % 512 == 0
    T = B * S
    M = mem.shape[1]
    l = 0
    xf = x.reshape(T, D)

    rope = _rope_table(positions.reshape(1, T))
    kv = _mem_kv(mem.reshape(B * M, D), w_mem_kv[l].astype(BF16))
    rw, rb = _router_params(w_grp_router[l], b_grp_router[l], w_exp_router[l], b_exp_router[l])
    xp, eid, wtok = _mixer(xf, rope, kv, w_in[l].astype(BF16),
                           w_pool_grp[l], pool_scale[l][None, :], ret_gn_w[l].reshape(1, -1),
                           w_br_pool[l].astype(BF16), w_br_ret[l].astype(BF16), w_br_xa[l].astype(BF16),
                           w_out[l].astype(BF16), ln1_w[l][None, :], ln1_b[l][None, :], rw, rb, B, S, M)

    pos, first_tile, n_tiles = _positions(eid)
    pos2d = pos[0:TOP_K].reshape(TOP_K * T // SC_CHUNK, SC_CHUNK)
    max_tiles = (TOP_K * T + N_EXPERTS * (MOE_TM - 1)) // MOE_TM
    xs = _sc_scatter_rows(xp, pos2d, max_tiles * MOE_TM)
    ys = _routed_mlp(first_tile[:, 0], n_tiles[:, 0], xs,
                     w_exp_gate[l].reshape(N_EXPERTS, D_MODEL, D_EXPERT),
                     w_exp_up[l].reshape(N_EXPERTS, D_MODEL, D_EXPERT),
                     w_exp_down[l].reshape(N_EXPERTS, D_EXPERT, D_MODEL))
    rng = T // COMBINE_PARTS
    yg_parts = [_sc_gather_rows(ys, pos[0:TOP_K, p * rng:(p + 1) * rng].reshape(TOP_K * rng // SC_CHUNK, SC_CHUNK))
                for p in range(COMBINE_PARTS)]
    out = _combine_ln2(xp, yg_parts, wtok, ln2_w[l][None, :], ln2_b[l][None, :])
    return out.reshape(B, S, D)
```

```python
import functools

import numpy as np
import jax
import jax.numpy as jnp
from jax import lax
from jax.experimental import pallas as pl
from jax.experimental.pallas import tpu as pltpu
from jax.experimental.pallas import tpu_sc as plsc

F32 = jnp.float32
BF16 = jnp.bfloat16
I32 = jnp.int32
U32 = jnp.uint32

D_MODEL = 1024
POOL_WINDOWS = (2, 4, 8, 16)
POOL_GROUP_DIM = 128
POOL_WIDTH = 512
POOL_HALO = 16
RET_HEADS = 4
RET_QK_DIM = 128
RET_V_DIM = 256
RET_CHUNK = 128
ROPE_BASE = 10000.0
XA_HEADS = 4
XA_HEAD_DIM = 128
XA_WIDTH = 512
N_GROUPS = 4
EXPERTS_PER_GROUP = 8
N_EXPERTS = N_GROUPS * EXPERTS_PER_GROUP
D_EXPERT = 256
LN_EPS = 1e-5
DEPTH = 1
ALPHA = (2.0 * DEPTH) ** 0.25
NEG_BIG = -1e30

COL_POOL, COL_Q, COL_K, COL_V, COL_G, COL_XAQ, COL_GATES = 0, 512, 1024, 1536, 2560, 3584, 4096

V7X_VMEM_BYTES = 64 * 1024 * 1024
VMEM_LIMIT = V7X_VMEM_BYTES * 7 // 8
SUBLANES = 8
LANES = 128

TOP_K = 2
PACK_W = D_MODEL // 2
MOE_TM = 512
SC_CHUNK = 64
RANK_CHUNK = 512
COMBINE_PARTS = 4
RING = 4
STRIP = 256
LN_ROWS = 32


def _dot(a, b):
    return jnp.dot(a, b, preferred_element_type=F32)


def _dot_nt(a, b):
    return lax.dot_general(a, b, (((1,), (1,)), ((), ())), preferred_element_type=F32)


def _sigmoid(z):
    return 1.0 / (1.0 + jnp.exp2(z * (-1.0 / np.log(2.0))))


def _layer_norm(h, w, b):
    mu = jnp.mean(h, axis=-1, keepdims=True)
    hc = h - mu
    var = jnp.mean(hc * hc, axis=-1, keepdims=True)
    return hc * lax.rsqrt(var + LN_EPS) * w + b


ROPE_LO = 64
ROPE_PARTS = 3


def _rope_kernel(pos_ref, freq_ref, tab_ref, cos_ref, sin_ref, cos_t_ref, sin_t_ref):
    pos = pos_ref[...]

    def emit(cos_t, sin_t):
        cos_t_ref[...] = cos_t
        sin_t_ref[...] = sin_t
        cos_ref[...] = jnp.transpose(jnp.concatenate([cos_t, cos_t], axis=0))
        sin_ref[...] = jnp.transpose(jnp.concatenate([-sin_t, sin_t], axis=0))

    in_table = jnp.logical_and(jnp.min(pos) >= 0, jnp.max(pos) < ROPE_LO * LANES)

    @pl.when(in_table)
    def _():
        idx = lax.broadcasted_iota(I32, (LANES, pos.shape[1]), 0)
        pick_hi = jnp.where(idx == jnp.right_shift(pos, ROPE_LO.bit_length() - 1), 1.0, 0.0).astype(BF16)
        pick_lo = jnp.where(idx == (pos & (ROPE_LO - 1)), 1.0, 0.0).astype(BF16)

        def look(k, pick):
            return sum(_dot(tab_ref[k * ROPE_PARTS + p], pick) for p in range(ROPE_PARTS))

        cos_a, sin_a, cos_b, sin_b = look(0, pick_hi), look(1, pick_hi), look(2, pick_lo), look(3, pick_lo)
        emit(cos_a * cos_b - sin_a * sin_b, sin_a * cos_b + cos_a * sin_b)

    @pl.when(jnp.logical_not(in_table))
    def _():
        ang = freq_ref[...] * pos.astype(F32)
        emit(jnp.cos(ang), jnp.sin(ang))


def _rope_tables(inv_freq):
    f = inv_freq.astype(np.float64)[:, None]
    idx = np.arange(LANES, dtype=np.float64)[None, :]
    tabs = [np.cos(ROPE_LO * idx * f), np.sin(ROPE_LO * idx * f), np.cos(idx * f), np.sin(idx * f)]
    pieces = []
    for t in tabs:
        rest = t.astype(np.float32)
        for _ in range(ROPE_PARTS):
            piece = rest.astype(BF16)
            pieces.append(piece)
            rest = rest - piece.astype(np.float32)
    return jnp.asarray(np.stack(pieces))


def _rope_table(pos_row, tile=2048):
    T = pos_row.shape[1]
    half = RET_QK_DIM // 2
    inv_freq = (ROPE_BASE ** (-np.arange(half, dtype=np.float64) / half)).astype(np.float32)
    freq = jnp.asarray(inv_freq[:, None])
    tabs = _rope_tables(inv_freq)
    out = pl.BlockSpec((tile, RET_QK_DIM), lambda i: (i, 0))
    out_t = pl.BlockSpec((half, tile), lambda i: (0, i))
    return pl.pallas_call(
        _rope_kernel,
        grid=(T // tile,),
        in_specs=[pl.BlockSpec((1, tile), lambda i: (0, i)), pl.BlockSpec((half, 1), lambda i: (0, 0)),
                  pl.BlockSpec(tabs.shape, lambda i: (0, 0, 0))],
        out_specs=[out, out, out_t, out_t],
        out_shape=[jax.ShapeDtypeStruct((T, RET_QK_DIM), F32)] * 2 + [jax.ShapeDtypeStruct((half, T), F32)] * 2,
        name="rope_table",
    )(pos_row, freq, tabs)


POOL_SUB = 256


def _pool_bands():
    r = np.arange(POOL_SUB)[:, None]
    c = np.arange(POOL_SUB)[None, :]
    ch = np.arange(POOL_HALO)[None, :] - POOL_HALO
    main = np.stack([((r - c >= 0) & (r - c < w)) for w in POOL_WINDOWS]).astype(np.float32)
    halo = np.stack([((r - ch >= 0) & (r - ch < w)) for w in POOL_WINDOWS]).astype(np.float32)
    return jnp.asarray(main, BF16), jnp.asarray(halo, BF16)


def _pool_branch(ub, j, halo_ref, bmain_ref, bhalo_ref, o_ref, filler):
    tile = ub.shape[0]
    s0 = j * tile
    slot = lax.rem(j, 2)
    blocks = [(sb * POOL_SUB, g) for sb in range(tile // POOL_SUB) for g in range(len(POOL_WINDOWS))]
    wsum = {}
    for r0, g in blocks:
        cols = slice(g * POOL_GROUP_DIM, (g + 1) * POOL_GROUP_DIM)
        prev = halo_ref[slot] if r0 == 0 else ub[r0 - POOL_HALO:r0]
        wsum[r0, g] = _dot(bmain_ref[g], ub[r0:r0 + POOL_SUB, cols]) + _dot(bhalo_ref[g], prev[:, cols])
    filler()
    for r0, g in blocks:
        cols = slice(g * POOL_GROUP_DIM, (g + 1) * POOL_GROUP_DIM)
        pos = s0 + r0 + lax.broadcasted_iota(I32, (POOL_SUB, POOL_GROUP_DIM), 0)
        cnt = jnp.minimum(pos + 1, POOL_WINDOWS[g]).astype(F32)
        pooled = wsum[r0, g] / cnt - ub[r0:r0 + POOL_SUB, cols].astype(F32)
        o_ref[r0:r0 + POOL_SUB, cols] = pooled.astype(BF16)
    halo_ref[1 - slot] = ub[tile - POOL_HALO:tile]


def _ret_consts():
    h = np.arange(RET_HEADS, dtype=np.float64)
    log_gamma = np.log1p(-np.exp2(-5.0 - h))
    pos = np.arange(RET_CHUNK, dtype=np.float64)
    diff = pos[:, None] - pos[None, :]
    kscale = RET_QK_DIM ** -0.5
    dmask = kscale * np.where(diff >= 0, np.exp(log_gamma[:, None, None] * np.maximum(diff, 0.0)), 0.0)
    qdec = np.exp(log_gamma[:, None] * (pos + 1.0)[None, :])
    kdec = kscale * np.exp(log_gamma[:, None] * (RET_CHUNK - 1.0 - pos)[None, :])
    cdec = np.exp(log_gamma * RET_CHUNK)
    lanes = lambda a: np.broadcast_to(a[:, :, None], (RET_HEADS, RET_CHUNK, RET_QK_DIM))
    kdec_t = np.broadcast_to(kdec[:, None, :], (RET_HEADS, RET_QK_DIM, RET_CHUNK))
    return (jnp.asarray(dmask, F32), jnp.asarray(lanes(qdec), F32), jnp.asarray(kdec_t, F32),
            tuple(float(v) for v in cdec))


def _retention_branch(q, k_t, v, silu_g, cos_ref, sin_ref, cos_t_ref, sin_t_ref, dmask_ref, qdec_ref, kdec_ref,
                      gnw_ref, state_ref, rq_ref, rqd_ref, rkt_ref, rkdt_ref, o_ref, cdec, fillers):
    tile = q.shape[0]
    n_chunks = tile // RET_CHUNK
    half = RET_QK_DIM // 2
    cos = cos_ref[...]
    sin = sin_ref[...]
    cos_t = cos_t_ref[...]
    sin_t = sin_t_ref[...]
    for h in range(RET_HEADS):
        qk = slice(h * RET_QK_DIM, (h + 1) * RET_QK_DIM)
        qh = q[:, qk]
        qr = qh * cos + pltpu.roll(qh, half, 1) * sin
        rq_ref[:, qk] = qr.astype(BF16)
        rqd_ref[:, qk] = (qr * jnp.concatenate([qdec_ref[h]] * n_chunks, axis=0)).astype(BF16)
        k1 = k_t[h * RET_QK_DIM:h * RET_QK_DIM + half]
        k2 = k_t[h * RET_QK_DIM + half:(h + 1) * RET_QK_DIM]
        kr_t = jnp.concatenate([k1 * cos_t - k2 * sin_t, k2 * cos_t + k1 * sin_t], axis=0)
        rkt_ref[qk, :] = kr_t.astype(BF16)
        rkdt_ref[qk, :] = (kr_t * jnp.concatenate([kdec_ref[h]] * n_chunks, axis=1)).astype(BF16)

    chunks = [slice(c * RET_CHUNK, (c + 1) * RET_CHUNK) for c in range(n_chunks)]
    heads = [slice(h * RET_QK_DIM, (h + 1) * RET_QK_DIM) for h in range(RET_HEADS)]
    v_heads = [slice(h * RET_V_DIM, (h + 1) * RET_V_DIM) for h in range(RET_HEADS)]
    raw = {(c, h): _dot(rq_ref[chunks[c], heads[h]], rkt_ref[heads[h], chunks[c]])
           for c in range(n_chunks) for h in range(RET_HEADS)}
    incr = {(c, h): _dot(rkdt_ref[heads[h], chunks[c]], v[chunks[c], v_heads[h]])
            for c in range(n_chunks) for h in range(RET_HEADS)}
    state_in = {}
    for h in range(RET_HEADS):
        st = state_ref[h]
        for c in range(n_chunks):
            state_in[c, h] = st.astype(BF16)
            st = cdec[h] * st + incr[c, h]
        state_ref[h] = st

    for c in range(n_chunks):
        rows = chunks[c]
        fillers[c]()
        for h in range(RET_HEADS):
            qk = heads[h]
            v_cols = v_heads[h]
            scores = raw[c, h] * dmask_ref[h]
            lhs = jnp.concatenate([scores.astype(BF16), rqd_ref[rows, qk]], axis=1)
            y = _dot(lhs, jnp.concatenate([v[rows, v_cols], state_in[c, h]], axis=0))
            mu = jnp.mean(y, axis=-1, keepdims=True)
            yc = y - mu
            var = jnp.mean(yc * yc, axis=-1, keepdims=True)
            yn = yc * lax.rsqrt(var + LN_EPS) * gnw_ref[:, v_cols]
            o_ref[rows, v_cols] = (silu_g[h][rows] * yn).astype(BF16)


def _memkv_kernel(m_ref, w_ref, o_ref):
    o_ref[...] = _dot(m_ref[...].astype(BF16), w_ref[...]).astype(BF16)


def _mem_kv(memf, w_b):
    M, D = memf.shape
    N = w_b.shape[1]
    return pl.pallas_call(
        _memkv_kernel,
        grid=(1,),
        in_specs=[pl.BlockSpec((M, D), lambda i: (0, 0)), pl.BlockSpec((D, N), lambda i: (0, 0))],
        out_specs=pl.BlockSpec((M, N), lambda i: (0, 0)),
        out_shape=jax.ShapeDtypeStruct((M, N), BF16),
        name="mem_kv",
    )(memf, w_b)


def _cross_attention_branch(xq, k_ref, v_ref, o_ref):
    scale = XA_HEAD_DIM ** -0.5
    for h in range(XA_HEADS):
        cols = slice(h * XA_HEAD_DIM, (h + 1) * XA_HEAD_DIM)
        s = _dot_nt(xq[:, cols], k_ref[:, cols]) * scale
        m = jnp.max(s, axis=-1, keepdims=True)
        p = jnp.exp(s - m)
        l = jnp.sum(p, axis=-1, keepdims=True)
        o = _dot(p.astype(BF16), v_ref[:, cols]) / l
        o_ref[:, cols] = o.astype(BF16)


ROUTER_ROWS = 128
WTOK_LANES = 128
EXP_ROW0 = 8


def _route(logits_t):
    gl = logits_t[0:8]
    gmax = jnp.max(gl, axis=0, keepdims=True)
    p_grp = 1.0 / jnp.sum(jnp.exp(gl - gmax), axis=0, keepdims=True)
    idx8 = lax.broadcasted_iota(jnp.int32, gl.shape, 0)
    gsel = jnp.min(jnp.where(gl == gmax, idx8, 8), axis=0, keepdims=True)
    cl = jnp.zeros_like(gl)
    for g in range(N_GROUPS):
        r0 = EXP_ROW0 + g * EXPERTS_PER_GROUP
        cl = cl + jnp.where(gsel == g, logits_t[r0:r0 + EXPERTS_PER_GROUP], 0.0)
    v1 = jnp.max(cl, axis=0, keepdims=True)
    i1 = jnp.min(jnp.where(cl == v1, idx8, 8), axis=0, keepdims=True)
    cl2 = jnp.where(idx8 == i1, -jnp.inf, cl)
    v2 = jnp.max(cl2, axis=0, keepdims=True)
    i2 = jnp.min(jnp.where(cl2 == v2, idx8, 8), axis=0, keepdims=True)
    e21 = jnp.exp(v2 - v1)
    w1 = p_grp / (1.0 + e21)
    w2 = p_grp * e21 / (1.0 + e21)
    return gsel * EXPERTS_PER_GROUP + i1, gsel * EXPERTS_PER_GROUP + i2, w1, w2


def _pack_halves(v):
    half = v.shape[1] // 2
    lo = lax.bitcast_convert_type(v[:, :half].astype(BF16).astype(F32), U32)
    hi = lax.bitcast_convert_type(v[:, half:].astype(BF16).astype(F32), U32)
    return lax.bitcast_convert_type(lax.shift_right_logical(lo, U32(16)) | hi, I32)


def _unpack_halves(w):
    u = lax.bitcast_convert_type(w, U32)
    lo = lax.bitcast_convert_type(lax.shift_left(u, U32(16)), F32)
    hi = lax.bitcast_convert_type(u & U32(0xFFFF0000), F32)
    return lo, hi


def _mixer_kernel(x_ref, cos_ref, sin_ref, cos_t_ref, sin_t_ref, km_ref, vm_ref, win_ref, wgrp_ref,
                  pscale_ref, bmain_ref, bhalo_ref, dmask_ref, qdec_ref, kdec_ref, gnw_ref, wp_ref, wr_ref,
                  wa_ref, wo_ref, lnw_ref, lnb_ref, rw_ref, rb_ref, xp_ref, eid_ref, wtok_ref,
                  state_ref, halo_ref, ypool_ref, yret_ref, yxa_ref, rq_ref, rqd_ref, rkt_ref, rkdt_ref, wkt_ref,
                  wpool_ref, *, tile, cdec):
    j = pl.program_id(1)

    @pl.when(jnp.logical_and(pl.program_id(0) == 0, j == 0))
    def _():
        wk = win_ref[:, COL_K:COL_K + RET_HEADS * RET_QK_DIM].astype(F32)
        wkt_ref[...] = jnp.transpose(wk).astype(BF16)
        for g in range(len(POOL_WINDOWS)):
            rows = slice(g * POOL_GROUP_DIM, (g + 1) * POOL_GROUP_DIM)
            wg = (wgrp_ref[g].astype(F32) * pscale_ref[:, rows]).astype(BF16)
            wpool_ref[rows, :] = _dot(wg, wp_ref[rows, :]).astype(BF16)

    @pl.when(j == 0)
    def _():
        state_ref[...] = jnp.zeros_like(state_ref)
        halo_ref[...] = jnp.zeros_like(halo_ref)

    x = x_ref[...]
    xb = x.astype(BF16)

    def proj(col, width):
        return _dot(xb, win_ref[:, col:col + width])

    part = {}
    strips = [slice(c, c + STRIP) for c in range(0, D_MODEL, STRIP)]

    def gate(branch, cols):
        return _sigmoid(proj(COL_GATES + branch * D_MODEL + cols.start, STRIP))

    def pool_part():
        def pool_gates():
            part["pool_gate"] = [gate(0, c) for c in strips]

        _pool_branch(proj(COL_POOL, POOL_WIDTH).astype(BF16), j, halo_ref, bmain_ref, bhalo_ref, ypool_ref,
                     pool_gates)
        part["pool"] = [part["pool_gate"][i] * _dot(ypool_ref[...], wpool_ref[:, c]) for i, c in enumerate(strips)]

    def xa_part():
        _cross_attention_branch(proj(COL_XAQ, XA_WIDTH).astype(BF16), km_ref, vm_ref, yxa_ref)
        part["xa"] = [gate(2, c) * _dot(yxa_ref[...], wa_ref[:, c]) for c in strips]

    def ret_gate_part():
        part["ret_gate"] = [gate(1, c) for c in strips]

    silu_g = []
    for h in range(RET_HEADS):
        gh = proj(COL_G + h * RET_V_DIM, RET_V_DIM)
        silu_g.append(gh * _sigmoid(gh))
    fillers = [pool_part, xa_part, ret_gate_part] + [lambda: None] * (tile // RET_CHUNK - 3)
    _retention_branch(proj(COL_Q, RET_HEADS * RET_QK_DIM), _dot_nt(wkt_ref[...], xb),
                      proj(COL_V, RET_HEADS * RET_V_DIM).astype(BF16), silu_g, cos_ref, sin_ref, cos_t_ref,
                      sin_t_ref, dmask_ref, qdec_ref, kdec_ref, gnw_ref, state_ref, rq_ref, rqd_ref, rkt_ref,
                      rkdt_ref, yret_ref, cdec, fillers)
    merged = jnp.concatenate(
        [(part["pool"][i] + part["ret_gate"][i] * _dot(yret_ref[...], wr_ref[:, c]) + part["xa"][i]).astype(BF16)
         for i, c in enumerate(strips)], axis=1)
    h = jnp.concatenate([ALPHA * x[:, c] + _dot(merged, wo_ref[:, c]) for c in strips], axis=1)
    x1 = jnp.concatenate([_layer_norm(h[r:r + LN_ROWS], lnw_ref[...], lnb_ref[...])
                          for r in range(0, tile, LN_ROWS)], axis=0)
    xp_ref[...] = _pack_halves(x1)
    x1_hi = x1.astype(BF16)
    x1_lo = (x1 - x1_hi.astype(F32)).astype(BF16)
    p4 = _dot(jnp.concatenate([x1_hi, x1_lo], axis=0), rw_ref[...])
    logits = (p4[:tile, :ROUTER_ROWS] + p4[:tile, ROUTER_ROWS:]) + (p4[tile:, :ROUTER_ROWS] + p4[tile:, ROUTER_ROWS:])
    e0, e1, w0, w1 = _route(jnp.transpose(logits) + rb_ref[...])
    eid_ref[...] = jnp.concatenate([e0, e1, jnp.zeros((SUBLANES - TOP_K, tile), I32)], axis=0)
    w_t = jnp.concatenate([w0, w1, jnp.zeros((WTOK_LANES - TOP_K, tile), F32)], axis=0)
    wtok_ref[...] = jnp.transpose(w_t)


def _mixer(xf, rope, kv, win, wgrp, pscale, gnw, wp, wr, wa, wo, lnw, lnb, rw, rb, batch, seq, mem_len,
           tile=512):
    T = xf.shape[0]
    nj = seq // tile
    cos, sin, cos_t, sin_t = rope
    bmain, bhalo = _pool_bands()
    dmask, qdec, kdec, cdec = _ret_consts()
    resident = lambda a: pl.BlockSpec(a.shape, lambda b, j: (0,) * a.ndim, pipeline_mode=pl.Buffered(1))
    rowblk = lambda w: pl.BlockSpec((tile, w), lambda b, j: (b * nj + j, 0))
    colblk = lambda r: pl.BlockSpec((r, tile), lambda b, j: (0, b * nj + j))
    consts = (win, wgrp, pscale, bmain, bhalo, dmask, qdec, kdec, gnw, wp, wr, wa, wo, lnw, lnb, rw, rb)
    return pl.pallas_call(
        functools.partial(_mixer_kernel, tile=tile, cdec=cdec),
        grid=(batch, nj),
        in_specs=[rowblk(D_MODEL), rowblk(RET_QK_DIM), rowblk(RET_QK_DIM),
                  colblk(RET_QK_DIM // 2), colblk(RET_QK_DIM // 2),
                  pl.BlockSpec((mem_len, XA_WIDTH), lambda b, j: (b, 0)),
                  pl.BlockSpec((mem_len, XA_WIDTH), lambda b, j: (b, 1))] + [resident(a) for a in consts],
        out_specs=[rowblk(PACK_W), colblk(SUBLANES), rowblk(WTOK_LANES)],
        out_shape=[jax.ShapeDtypeStruct((T, PACK_W), I32), jax.ShapeDtypeStruct((SUBLANES, T), I32),
                   jax.ShapeDtypeStruct((T, WTOK_LANES), F32)],
        scratch_shapes=[pltpu.VMEM((RET_HEADS, RET_QK_DIM, RET_V_DIM), F32),
                        pltpu.VMEM((2, POOL_HALO, POOL_WIDTH), BF16),
                        pltpu.VMEM((tile, POOL_WIDTH), BF16),
                        pltpu.VMEM((tile, RET_HEADS * RET_V_DIM), BF16),
                        pltpu.VMEM((tile, XA_WIDTH), BF16),
                        pltpu.VMEM((tile, RET_HEADS * RET_QK_DIM), BF16),
                        pltpu.VMEM((tile, RET_HEADS * RET_QK_DIM), BF16),
                        pltpu.VMEM((RET_HEADS * RET_QK_DIM, tile), BF16),
                        pltpu.VMEM((RET_HEADS * RET_QK_DIM, tile), BF16),
                        pltpu.VMEM((RET_HEADS * RET_QK_DIM, D_MODEL), BF16),
                        pltpu.VMEM((POOL_WIDTH, D_MODEL), BF16)],
        compiler_params=pltpu.CompilerParams(dimension_semantics=("arbitrary", "arbitrary"),
                                             vmem_limit_bytes=VMEM_LIMIT),
        name="mixer",
    )(xf, cos, sin, cos_t, sin_t, kv, kv, *consts)


META_LANES = LANES


def _positions_kernel(eid_ref, tri_ref, low_ref, pos_ref, first_tile_ref, n_tiles_ref, *, n_tok):
    n_chunks = n_tok // RANK_CHUNK
    erow = lax.broadcasted_iota(I32, (N_EXPERTS, RANK_CHUNK), 0)

    def onehot(c):
        sl = slice(c * RANK_CHUNK, (c + 1) * RANK_CHUNK)
        m0 = eid_ref[0:1, sl] == erow
        m1 = eid_ref[1:2, sl] == erow
        return m0, m1, jnp.where(m0, 1.0, 0.0) + jnp.where(m1, 1.0, 0.0)

    counts = jnp.zeros((N_EXPERTS, 1), F32)
    for c in range(n_chunks):
        counts = counts + jnp.sum(onehot(c)[2], axis=1, keepdims=True)
    ptiles = jnp.floor((counts + (MOE_TM - 1)) * (1.0 / MOE_TM))
    ptiles_b = jnp.broadcast_to(ptiles, (N_EXPERTS, LANES)).astype(BF16)
    start = _dot(low_ref[...], ptiles_b)[:, 0:1] * MOE_TM

    pos_ref[...] = jnp.zeros_like(pos_ref)
    carry = start - 1.0
    for c in range(n_chunks):
        sl = slice(c * RANK_CHUNK, (c + 1) * RANK_CHUNK)
        m0, m1, oh = onehot(c)
        rank = _dot(oh.astype(BF16), tri_ref[...]) + carry
        pos_ref[0:1, sl] = jnp.sum(jnp.where(m0, rank, 0.0), axis=0, keepdims=True).astype(I32)
        pos_ref[1:2, sl] = jnp.sum(jnp.where(m1, rank, 0.0), axis=0, keepdims=True).astype(I32)
        carry = carry + jnp.sum(oh, axis=1, keepdims=True)

    first_tile_ref[...] = jnp.broadcast_to(start * (1.0 / MOE_TM), first_tile_ref.shape).astype(I32)
    n_tiles_ref[...] = jnp.broadcast_to(ptiles, n_tiles_ref.shape).astype(I32)


def _positions(eid):
    T = eid.shape[1]
    r = np.arange(RANK_CHUNK)
    tri = jnp.asarray(r[:, None] <= r[None, :], BF16)
    e = np.arange(N_EXPERTS)
    low = jnp.asarray(e[None, :] < e[:, None], BF16)
    full = lambda a: pl.BlockSpec(a.shape, lambda i: (0,) * a.ndim)
    return pl.pallas_call(
        functools.partial(_positions_kernel, n_tok=T),
        grid=(1,),
        in_specs=[full(eid), full(tri), full(low)],
        out_specs=[pl.BlockSpec((SUBLANES, T), lambda i: (0, 0)),
                   pl.BlockSpec((N_EXPERTS, META_LANES), lambda i: (0, 0)),
                   pl.BlockSpec((N_EXPERTS, META_LANES), lambda i: (0, 0))],
        out_shape=[jax.ShapeDtypeStruct((SUBLANES, T), I32), jax.ShapeDtypeStruct((N_EXPERTS, META_LANES), I32),
                   jax.ShapeDtypeStruct((N_EXPERTS, META_LANES), I32)],
        name="route_positions",
    )(eid, tri, low)


def _sc_workers():
    info = plsc.get_sparse_core_info()
    return info.num_cores, info.num_cores * info.num_subcores


def _sc_scatter_rows(xp, pos2d, n_out):
    T, W = xp.shape
    n_cores, n_workers = _sc_workers()
    cpw = T // SC_CHUNK // n_workers
    mesh = plsc.VectorSubcoreMesh(core_axis_name="c", subcore_axis_name="s")

    @functools.partial(
        pl.kernel, mesh=mesh, out_type=jax.ShapeDtypeStruct((n_out, W), I32),
        scratch_types=[pltpu.VMEM((TOP_K * cpw, SC_CHUNK), I32), pltpu.VMEM((2, SC_CHUNK, W), I32),
                       pltpu.SemaphoreType.DMA((2,)), pltpu.SemaphoreType.DMA((2,))],
        name="sc_scatter_rows")
    def k(x_hbm, pos_hbm, out_hbm, idx_v, rows_v, rd_sem, wr_sem):
        wid = lax.axis_index("s") * n_cores + lax.axis_index("c")
        for s in range(TOP_K):
            pltpu.sync_copy(pos_hbm.at[pl.ds(s * (T // SC_CHUNK) + wid * cpw, cpw)],
                            idx_v.at[pl.ds(s * cpw, cpw)])

        def read(j):
            return pltpu.make_async_copy(x_hbm.at[pl.ds((wid * cpw + j) * SC_CHUNK, SC_CHUNK)],
                                         rows_v.at[j % 2], rd_sem.at[j % 2])

        def write(j, s):
            return pltpu.make_async_copy(rows_v.at[j % 2], out_hbm.at[idx_v.at[s * cpw + j]], wr_sem.at[j % 2])

        _sc_two_buffer_stream(cpw, read, lambda j: [write(j, s) for s in range(TOP_K)])

    return k(xp, pos2d)


def _sc_two_buffer_stream(n, read, writes):
    read(0).start()
    for j in range(n):
        read(j).wait()
        if j + 1 < n:
            if j >= 1:
                for w in writes(j - 1):
                    w.wait()
            read(j + 1).start()
        for w in writes(j):
            w.start()
    for j in range(max(n - 2, 0), n):
        for w in writes(j):
            w.wait()


def _sc_gather_rows(y, idx2d):
    W = y.shape[1]
    n = idx2d.shape[0] * SC_CHUNK
    n_cores, n_workers = _sc_workers()
    cpw = n // SC_CHUNK // n_workers
    mesh = plsc.VectorSubcoreMesh(core_axis_name="c", subcore_axis_name="s")

    @functools.partial(
        pl.kernel, mesh=mesh, out_type=jax.ShapeDtypeStruct((n, W), I32),
        scratch_types=[pltpu.VMEM((cpw, SC_CHUNK), I32), pltpu.VMEM((2, SC_CHUNK, W), I32),
                       pltpu.SemaphoreType.DMA((2,)), pltpu.SemaphoreType.DMA((2,))],
        name="sc_gather_rows")
    def k(y_hbm, idx_hbm, out_hbm, idx_v, rows_v, rd_sem, wr_sem):
        wid = lax.axis_index("s") * n_cores + lax.axis_index("c")
        pltpu.sync_copy(idx_hbm.at[pl.ds(wid * cpw, cpw)], idx_v)

        def read(j):
            return pltpu.make_async_copy(y_hbm.at[idx_v.at[j]], rows_v.at[j % 2], rd_sem.at[j % 2])

        def write(j):
            return pltpu.make_async_copy(rows_v.at[j % 2], out_hbm.at[pl.ds((wid * cpw + j) * SC_CHUNK, SC_CHUNK)],
                                         wr_sem.at[j % 2])

        _sc_two_buffer_stream(cpw, read, lambda j: [write(j)])

    return k(y, idx2d)


def _routed_kernel(first_ref, count_ref, xs_hbm, wg_ref, wu_ref, wd_ref, ys_hbm, xbuf, ybuf, in_sem, out_sem):
    e = pl.program_id(0)
    last = pl.num_programs(0) - 1
    total = first_ref[last] + count_ref[last]

    def in_copy(g):
        slot = lax.rem(g, RING)
        return pltpu.make_async_copy(xs_hbm.at[pl.ds(g * MOE_TM, MOE_TM)], xbuf.at[slot], in_sem.at[slot])

    def out_copy(g):
        slot = lax.rem(g, RING)
        return pltpu.make_async_copy(ybuf.at[slot], ys_hbm.at[pl.ds(g * MOE_TM, MOE_TM)], out_sem.at[slot])

    @pl.when(e == 0)
    def _():
        for g0 in range(RING - 1):
            @pl.when(g0 < total)
            def _():
                in_copy(g0).start()

    wg = wg_ref[0].astype(BF16)
    wu = wu_ref[0].astype(BF16)
    wd = wd_ref[0].astype(BF16)

    def tile_step(i, carry):
        g = first_ref[e] + i
        slot = lax.rem(g, RING)
        in_copy(g).wait()

        @pl.when(g + RING - 1 < total)
        def _():
            in_copy(g + RING - 1).start()

        @pl.when(g >= RING)
        def _():
            out_copy(g - RING).wait()

        lo, hi = _unpack_halves(xbuf[slot])
        lo = lo.astype(BF16)
        hi = hi.astype(BF16)
        a = _dot(lo, wg[:PACK_W]) + _dot(hi, wg[PACK_W:])
        b = _dot(lo, wu[:PACK_W]) + _dot(hi, wu[PACK_W:])
        act = (a * _sigmoid(a) * b).astype(BF16)
        ybuf[slot] = _pack_halves(_dot(act, wd))
        out_copy(g).start()
        return carry

    lax.fori_loop(0, count_ref[e], tile_step, 0)

    @pl.when(e == last)
    def _():
        for back in range(RING, 0, -1):
            @pl.when(total >= back)
            def _():
                out_copy(total - back).wait()


def _routed_mlp(first_tile, n_tiles, xs, wg, wu, wd):
    R = xs.shape[0]
    any_space = pl.BlockSpec(memory_space=pl.ANY)
    return pl.pallas_call(
        _routed_kernel,
        grid_spec=pltpu.PrefetchScalarGridSpec(
            num_scalar_prefetch=2,
            grid=(N_EXPERTS,),
            in_specs=[any_space,
                      pl.BlockSpec((1, D_MODEL, D_EXPERT), lambda e, ft, nt: (e, 0, 0)),
                      pl.BlockSpec((1, D_MODEL, D_EXPERT), lambda e, ft, nt: (e, 0, 0)),
                      pl.BlockSpec((1, D_EXPERT, D_MODEL), lambda e, ft, nt: (e, 0, 0))],
            out_specs=any_space,
            scratch_shapes=[pltpu.VMEM((RING, MOE_TM, PACK_W), I32), pltpu.VMEM((RING, MOE_TM, PACK_W), I32),
                            pltpu.SemaphoreType.DMA((RING,)), pltpu.SemaphoreType.DMA((RING,))]),
        out_shape=jax.ShapeDtypeStruct((R, PACK_W), I32),
        compiler_params=pltpu.CompilerParams(dimension_semantics=("arbitrary",)),
        name="routed_mlp",
    )(first_tile, n_tiles, xs, wg, wu, wd)


def _combine_kernel(xp_ref, y0_ref, y1_ref, wtok_ref, lnw_ref, lnb_ref, *out_refs):
    o_ref = out_refs[-1]
    w0 = wtok_ref[:, 0:1]
    w1 = wtok_ref[:, 1:2]
    xlo, xhi = _unpack_halves(xp_ref[...])
    y0lo, y0hi = _unpack_halves(y0_ref[...])
    y1lo, y1hi = _unpack_halves(y1_ref[...])
    h = jnp.concatenate([ALPHA * xlo + (w0 * y0lo + w1 * y1lo), ALPHA * xhi + (w0 * y0hi + w1 * y1hi)], axis=1)
    o_ref[...] = _layer_norm(h, lnw_ref[...], lnb_ref[...])


def _combine_ln2(xp, yg_parts, wtok, lnw, lnb, tile=1024):
    T = xp.shape[0]
    n_parts = len(yg_parts)
    nt = T // tile // n_parts
    full = lambda a: pl.BlockSpec(a.shape, lambda i: (0,) * a.ndim)
    out = None
    for p, yg in enumerate(yg_parts):
        rows = lambda w, p=p: pl.BlockSpec((tile, w), lambda i: (i + p * nt, 0))
        in_specs = [rows(PACK_W),
                    pl.BlockSpec((tile, PACK_W), lambda i: (i, 0)),
                    pl.BlockSpec((tile, PACK_W), lambda i: (i + nt, 0)),
                    rows(WTOK_LANES), full(lnw), full(lnb)]
        args = [xp, yg, yg, wtok, lnw, lnb]
        aliases = {}
        if out is not None:
            in_specs.append(pl.BlockSpec(memory_space=pl.ANY))
            args.append(out)
            aliases = {len(args) - 1: 0}
        out = pl.pallas_call(
            _combine_kernel,
            grid=(nt,),
            in_specs=in_specs,
            out_specs=rows(D_MODEL),
            out_shape=jax.ShapeDtypeStruct((T, D_MODEL), F32),
            input_output_aliases=aliases,
            compiler_params=pltpu.CompilerParams(dimension_semantics=("arbitrary",), vmem_limit_bytes=VMEM_LIMIT),
            name="combine_ln2",
        )(*args)
    return out


def _router_params(w_grp, b_grp, w_exp, b_exp):
    rw = jnp.zeros((D_MODEL, ROUTER_ROWS), F32)
    rw = rw.at[:, 0:N_GROUPS].set(w_grp).at[:, EXP_ROW0:EXP_ROW0 + N_EXPERTS].set(w_exp)
    rb = jnp.zeros((ROUTER_ROWS,), F32).at[N_GROUPS:8].set(NEG_BIG)
    rb = rb.at[0:N_GROUPS].set(b_grp).at[EXP_ROW0:EXP_ROW0 + N_EXPERTS].set(b_exp)
    rw_hi = rw.astype(BF16)
    rw_lo = (rw - rw_hi.astype(F32)).astype(BF16)
    return jnp.concatenate([rw_hi, rw_lo], axis=1), rb[:, None]


def kernel(x, mem, positions, w_in, w_pool_grp, pool_scale, ret_gn_w, w_mem_kv, w_br_pool, w_br_ret, w_br_xa,
           w_out, ln1_w, ln1_b, w_grp_router, b_grp_router, w_exp_router, b_exp_router, w_exp_gate, w_exp_up,
           w_exp_down, ln2_w, ln2_b):
    B, S, D = x.shape
    assert D == D_MODEL and w_in.shape[0] == DEPTH and S % 512 == 0
    T = B * S
    M = mem.shape[1]
    l = 0
    xf = x.reshape(T, D)

    rope = _rope_table(positions.reshape(1, T))
    kv = _mem_kv(mem.reshape(B * M, D), w_mem_kv[l].astype(BF16))
    rw, rb = _router_params(w_grp_router[l], b_grp_router[l], w_exp_router[l], b_exp_router[l])
    xp, eid, wtok = _mixer(xf, rope, kv, w_in[l].astype(BF16),
                           w_pool_grp[l], pool_scale[l][None, :], ret_gn_w[l].reshape(1, -1),
                           w_br_pool[l].astype(BF16), w_br_ret[l].astype(BF16), w_br_xa[l].astype(BF16),
                           w_out[l].astype(BF16), ln1_w[l][None, :], ln1_b[l][None, :], rw, rb, B, S, M)

    pos, first_tile, n_tiles = _positions(eid)
    pos2d = pos[0:TOP_K].reshape(TOP_K * T // SC_CHUNK, SC_CHUNK)
    max_tiles = (TOP_K * T + N_EXPERTS * (MOE_TM - 1)) // MOE_TM
    xs = _sc_scatter_rows(xp, pos2d, max_tiles * MOE_TM)
    ys = _routed_mlp(first_tile[:, 0], n_tiles[:, 0], xs,
                     w_exp_gate[l].reshape(N_EXPERTS, D_MODEL, D_EXPERT),
                     w_exp_up[l].reshape(N_EXPERTS, D_MODEL, D_EXPERT),
                     w_exp_down[l].reshape(N_EXPERTS, D_EXPERT, D_MODEL))
    rng = T // COMBINE_PARTS
    yg_parts = [_sc_gather_rows(ys, pos[0:TOP_K, p * rng:(p + 1) * rng].reshape(TOP_K * rng // SC_CHUNK, SC_CHUNK))
                for p in range(COMBINE_PARTS)]
    out = _combine_ln2(xp, yg_parts, wtok, ln2_w[l][None, :], ln2_b[l][None, :])
    return out.reshape(B, S, D)
```

```python
import functools

import numpy as np
import jax
import jax.numpy as jnp
from jax import lax
from jax.experimental import pallas as pl
from jax.experimental.pallas import tpu as pltpu
from jax.experimental.pallas import tpu_sc as plsc

F32 = jnp.float32
BF16 = jnp.bfloat16
I32 = jnp.int32
U32 = jnp.uint32

D_MODEL = 1024
POOL_WINDOWS = (2, 4, 8, 16)
POOL_GROUP_DIM = 128
POOL_WIDTH = 512
POOL_HALO = 16
RET_HEADS = 4
RET_QK_DIM = 128
RET_V_DIM = 256
RET_CHUNK = 128
ROPE_BASE = 10000.0
XA_HEADS = 4
XA_HEAD_DIM = 128
XA_WIDTH = 512
N_GROUPS = 4
EXPERTS_PER_GROUP = 8
N_EXPERTS = N_GROUPS * EXPERTS_PER_GROUP
D_EXPERT = 256
LN_EPS = 1e-5
DEPTH = 1
ALPHA = (2.0 * DEPTH) ** 0.25
NEG_BIG = -1e30

COL_POOL, COL_Q, COL_K, COL_V, COL_G, COL_XAQ, COL_GATES = 0, 512, 1024, 1536, 2560, 3584, 4096

V7X_VMEM_BYTES = 64 * 1024 * 1024
VMEM_LIMIT = V7X_VMEM_BYTES * 7 // 8
SUBLANES = 8
LANES = 128

TOP_K = 2
PACK_W = D_MODEL // 2
MOE_TM = 512
SC_CHUNK = 64
RANK_CHUNK = 512
COMBINE_PARTS = 2
W_STAGE_ROWS = 64
RING = 4
STRIP = 256
LN_ROWS = 32


def _dot(a, b):
    return jnp.dot(a, b, preferred_element_type=F32)


def _dot_nt(a, b):
    return lax.dot_general(a, b, (((1,), (1,)), ((), ())), preferred_element_type=F32)


def _sigmoid(z):
    return 1.0 / (1.0 + jnp.exp2(z * (-1.0 / np.log(2.0))))


def _layer_norm(h, w, b):
    mu = jnp.mean(h, axis=-1, keepdims=True)
    hc = h - mu
    var = jnp.mean(hc * hc, axis=-1, keepdims=True)
    return hc * lax.rsqrt(var + LN_EPS) * w + b


ROPE_LO = 64
ROPE_PARTS = 3


def _rope_kernel(pos_ref, freq_ref, tab_ref, cos_ref, sin_ref, cos_t_ref, sin_t_ref):
    pos = pos_ref[...]

    def emit(cos_t, sin_t):
        cos_t_ref[...] = cos_t
        sin_t_ref[...] = sin_t
        cos_ref[...] = jnp.transpose(jnp.concatenate([cos_t, cos_t], axis=0))
        sin_ref[...] = jnp.transpose(jnp.concatenate([-sin_t, sin_t], axis=0))

    in_table = jnp.logical_and(jnp.min(pos) >= 0, jnp.max(pos) < ROPE_LO * LANES)

    @pl.when(in_table)
    def _():
        idx = lax.broadcasted_iota(I32, (LANES, pos.shape[1]), 0)
        pick_hi = jnp.where(idx == jnp.right_shift(pos, ROPE_LO.bit_length() - 1), 1.0, 0.0).astype(BF16)
        pick_lo = jnp.where(idx == (pos & (ROPE_LO - 1)), 1.0, 0.0).astype(BF16)

        def look(k, pick):
            return sum(_dot(tab_ref[k * ROPE_PARTS + p], pick) for p in range(ROPE_PARTS))

        cos_a, sin_a, cos_b, sin_b = look(0, pick_hi), look(1, pick_hi), look(2, pick_lo), look(3, pick_lo)
        emit(cos_a * cos_b - sin_a * sin_b, sin_a * cos_b + cos_a * sin_b)

    @pl.when(jnp.logical_not(in_table))
    def _():
        ang = freq_ref[...] * pos.astype(F32)
        emit(jnp.cos(ang), jnp.sin(ang))


def _rope_tables(inv_freq):
    f = inv_freq.astype(np.float64)[:, None]
    idx = np.arange(LANES, dtype=np.float64)[None, :]
    tabs = [np.cos(ROPE_LO * idx * f), np.sin(ROPE_LO * idx * f), np.cos(idx * f), np.sin(idx * f)]
    pieces = []
    for t in tabs:
        rest = t.astype(np.float32)
        for _ in range(ROPE_PARTS):
            piece = rest.astype(BF16)
            pieces.append(piece)
            rest = rest - piece.astype(np.float32)
    return jnp.asarray(np.stack(pieces))


def _rope_table(pos_row, tile=2048):
    T = pos_row.shape[1]
    half = RET_QK_DIM // 2
    inv_freq = (ROPE_BASE ** (-np.arange(half, dtype=np.float64) / half)).astype(np.float32)
    freq = jnp.asarray(inv_freq[:, None])
    tabs = _rope_tables(inv_freq)
    out = pl.BlockSpec((tile, RET_QK_DIM), lambda i: (i, 0))
    out_t = pl.BlockSpec((half, tile), lambda i: (0, i))
    return pl.pallas_call(
        _rope_kernel,
        grid=(T // tile,),
        in_specs=[pl.BlockSpec((1, tile), lambda i: (0, i)), pl.BlockSpec((half, 1), lambda i: (0, 0)),
                  pl.BlockSpec(tabs.shape, lambda i: (0, 0, 0))],
        out_specs=[out, out, out_t, out_t],
        out_shape=[jax.ShapeDtypeStruct((T, RET_QK_DIM), F32)] * 2 + [jax.ShapeDtypeStruct((half, T), F32)] * 2,
        name="rope_table",
    )(pos_row, freq, tabs)


POOL_SUB = 256


def _pool_bands():
    r = np.arange(POOL_SUB)[:, None]
    c = np.arange(POOL_SUB)[None, :]
    ch = np.arange(POOL_HALO)[None, :] - POOL_HALO
    main = np.stack([((r - c >= 0) & (r - c < w)) for w in POOL_WINDOWS]).astype(np.float32)
    halo = np.stack([((r - ch >= 0) & (r - ch < w)) for w in POOL_WINDOWS]).astype(np.float32)
    return jnp.asarray(main, BF16), jnp.asarray(halo, BF16)


def _pool_branch(ub, j, halo_ref, bmain_ref, bhalo_ref, o_ref, filler):
    tile = ub.shape[0]
    s0 = j * tile
    slot = lax.rem(j, 2)
    blocks = [(sb * POOL_SUB, g) for sb in range(tile // POOL_SUB) for g in range(len(POOL_WINDOWS))]
    wsum = {}
    for r0, g in blocks:
        cols = slice(g * POOL_GROUP_DIM, (g + 1) * POOL_GROUP_DIM)
        prev = halo_ref[slot] if r0 == 0 else ub[r0 - POOL_HALO:r0]
        wsum[r0, g] = _dot(bmain_ref[g], ub[r0:r0 + POOL_SUB, cols]) + _dot(bhalo_ref[g], prev[:, cols])
    filler()
    for r0, g in blocks:
        cols = slice(g * POOL_GROUP_DIM, (g + 1) * POOL_GROUP_DIM)
        pos = s0 + r0 + lax.broadcasted_iota(I32, (POOL_SUB, POOL_GROUP_DIM), 0)
        cnt = jnp.minimum(pos + 1, POOL_WINDOWS[g]).astype(F32)
        pooled = wsum[r0, g] / cnt - ub[r0:r0 + POOL_SUB, cols].astype(F32)
        o_ref[r0:r0 + POOL_SUB, cols] = pooled.astype(BF16)
    halo_ref[1 - slot] = ub[tile - POOL_HALO:tile]


def _ret_consts():
    h = np.arange(RET_HEADS, dtype=np.float64)
    log_gamma = np.log1p(-np.exp2(-5.0 - h))
    pos = np.arange(RET_CHUNK, dtype=np.float64)
    diff = pos[:, None] - pos[None, :]
    kscale = RET_QK_DIM ** -0.5
    dmask = kscale * np.where(diff >= 0, np.exp(log_gamma[:, None, None] * np.maximum(diff, 0.0)), 0.0)
    qdec = np.exp(log_gamma[:, None] * (pos + 1.0)[None, :])
    kdec = kscale * np.exp(log_gamma[:, None] * (RET_CHUNK - 1.0 - pos)[None, :])
    cdec = np.exp(log_gamma * RET_CHUNK)
    lanes = lambda a: np.broadcast_to(a[:, :, None], (RET_HEADS, RET_CHUNK, RET_QK_DIM))
    kdec_t = np.broadcast_to(kdec[:, None, :], (RET_HEADS, RET_QK_DIM, RET_CHUNK))
    return (jnp.asarray(dmask, F32), jnp.asarray(lanes(qdec), F32), jnp.asarray(kdec_t, F32),
            tuple(float(v) for v in cdec))


def _retention_branch(q, k_t, v, silu_g, cos_ref, sin_ref, cos_t_ref, sin_t_ref, dmask_ref, qdec_ref, kdec_ref,
                      gnw_ref, state_ref, rq_ref, rqd_ref, rkt_ref, rkdt_ref, o_ref, cdec, fillers):
    tile = q.shape[0]
    n_chunks = tile // RET_CHUNK
    half = RET_QK_DIM // 2
    cos = cos_ref[...]
    sin = sin_ref[...]
    cos_t = cos_t_ref[...]
    sin_t = sin_t_ref[...]
    for h in range(RET_HEADS):
        qk = slice(h * RET_QK_DIM, (h + 1) * RET_QK_DIM)
        qh = q[:, qk]
        qr = qh * cos + pltpu.roll(qh, half, 1) * sin
        rq_ref[:, qk] = qr.astype(BF16)
        rqd_ref[:, qk] = (qr * jnp.concatenate([qdec_ref[h]] * n_chunks, axis=0)).astype(BF16)
        k1 = k_t[h * RET_QK_DIM:h * RET_QK_DIM + half]
        k2 = k_t[h * RET_QK_DIM + half:(h + 1) * RET_QK_DIM]
        kr_t = jnp.concatenate([k1 * cos_t - k2 * sin_t, k2 * cos_t + k1 * sin_t], axis=0)
        rkt_ref[qk, :] = kr_t.astype(BF16)
        rkdt_ref[qk, :] = (kr_t * jnp.concatenate([kdec_ref[h]] * n_chunks, axis=1)).astype(BF16)

    chunks = [slice(c * RET_CHUNK, (c + 1) * RET_CHUNK) for c in range(n_chunks)]
    heads = [slice(h * RET_QK_DIM, (h + 1) * RET_QK_DIM) for h in range(RET_HEADS)]
    v_heads = [slice(h * RET_V_DIM, (h + 1) * RET_V_DIM) for h in range(RET_HEADS)]
    raw = {(c, h): _dot(rq_ref[chunks[c], heads[h]], rkt_ref[heads[h], chunks[c]])
           for c in range(n_chunks) for h in range(RET_HEADS)}
    incr = {(c, h): _dot(rkdt_ref[heads[h], chunks[c]], v[chunks[c], v_heads[h]])
            for c in range(n_chunks) for h in range(RET_HEADS)}
    state_in = {}
    for h in range(RET_HEADS):
        st = state_ref[h]
        for c in range(n_chunks):
            state_in[c, h] = st.astype(BF16)
            st = cdec[h] * st + incr[c, h]
        state_ref[h] = st

    for c in range(n_chunks):
        rows = chunks[c]
        fillers[c]()
        for h in range(RET_HEADS):
            qk = heads[h]
            v_cols = v_heads[h]
            scores = raw[c, h] * dmask_ref[h]
            lhs = jnp.concatenate([scores.astype(BF16), rqd_ref[rows, qk]], axis=1)
            y = _dot(lhs, jnp.concatenate([v[rows, v_cols], state_in[c, h]], axis=0))
            mu = jnp.mean(y, axis=-1, keepdims=True)
            yc = y - mu
            var = jnp.mean(yc * yc, axis=-1, keepdims=True)
            yn = yc * lax.rsqrt(var + LN_EPS) * gnw_ref[:, v_cols]
            o_ref[rows, v_cols] = (silu_g[h][rows] * yn).astype(BF16)


def _memkv_kernel(m_ref, w_ref, o_ref):
    o_ref[...] = _dot(m_ref[...].astype(BF16), w_ref[...]).astype(BF16)


def _mem_kv(memf, w_b):
    M, D = memf.shape
    N = w_b.shape[1]
    return pl.pallas_call(
        _memkv_kernel,
        grid=(1,),
        in_specs=[pl.BlockSpec((M, D), lambda i: (0, 0)), pl.BlockSpec((D, N), lambda i: (0, 0))],
        out_specs=pl.BlockSpec((M, N), lambda i: (0, 0)),
        out_shape=jax.ShapeDtypeStruct((M, N), BF16),
        name="mem_kv",
    )(memf, w_b)


def _cross_attention_branch(xq, k_ref, v_ref, o_ref):
    scale = XA_HEAD_DIM ** -0.5
    for h in range(XA_HEADS):
        cols = slice(h * XA_HEAD_DIM, (h + 1) * XA_HEAD_DIM)
        s = _dot_nt(xq[:, cols], k_ref[:, cols]) * scale
        m = jnp.max(s, axis=-1, keepdims=True)
        p = jnp.exp(s - m)
        l = jnp.sum(p, axis=-1, keepdims=True)
        o = _dot(p.astype(BF16), v_ref[:, cols]) / l
        o_ref[:, cols] = o.astype(BF16)


ROUTER_ROWS = 128
WTOK_LANES = 128
EXP_ROW0 = 8


def _route(logits_t):
    gl = logits_t[0:8]
    gmax = jnp.max(gl, axis=0, keepdims=True)
    p_grp = 1.0 / jnp.sum(jnp.exp(gl - gmax), axis=0, keepdims=True)
    idx8 = lax.broadcasted_iota(jnp.int32, gl.shape, 0)
    gsel = jnp.min(jnp.where(gl == gmax, idx8, 8), axis=0, keepdims=True)
    cl = jnp.zeros_like(gl)
    for g in range(N_GROUPS):
        r0 = EXP_ROW0 + g * EXPERTS_PER_GROUP
        cl = cl + jnp.where(gsel == g, logits_t[r0:r0 + EXPERTS_PER_GROUP], 0.0)
    v1 = jnp.max(cl, axis=0, keepdims=True)
    i1 = jnp.min(jnp.where(cl == v1, idx8, 8), axis=0, keepdims=True)
    cl2 = jnp.where(idx8 == i1, -jnp.inf, cl)
    v2 = jnp.max(cl2, axis=0, keepdims=True)
    i2 = jnp.min(jnp.where(cl2 == v2, idx8, 8), axis=0, keepdims=True)
    e21 = jnp.exp(v2 - v1)
    w1 = p_grp / (1.0 + e21)
    w2 = p_grp * e21 / (1.0 + e21)
    return gsel * EXPERTS_PER_GROUP + i1, gsel * EXPERTS_PER_GROUP + i2, w1, w2


def _pack_halves(v):
    half = v.shape[1] // 2
    lo = lax.bitcast_convert_type(v[:, :half].astype(BF16).astype(F32), U32)
    hi = lax.bitcast_convert_type(v[:, half:].astype(BF16).astype(F32), U32)
    return lax.bitcast_convert_type(lax.shift_right_logical(lo, U32(16)) | hi, I32)


def _unpack_halves(w):
    u = lax.bitcast_convert_type(w, U32)
    lo = lax.bitcast_convert_type(lax.shift_left(u, U32(16)), F32)
    hi = lax.bitcast_convert_type(u & U32(0xFFFF0000), F32)
    return lo, hi


def _mixer_kernel(x_ref, cos_ref, sin_ref, cos_t_ref, sin_t_ref, km_ref, vm_ref, win_hbm, wgrp_ref,
                  pscale_ref, bmain_ref, bhalo_ref, dmask_ref, qdec_ref, kdec_ref, gnw_ref, wp_ref, wr_ref,
                  wa_ref, wo_ref, lnw_ref, lnb_ref, rw_ref, rb_ref, xp_ref, eid_ref, wtok_ref,
                  state_ref, halo_ref, ypool_ref, yret_ref, yxa_ref, rq_ref, rqd_ref, rkt_ref, rkdt_ref, wkt_ref,
                  wpool_ref, win_ref, stage_ref, stage_sem, *, tile, cdec):
    j = pl.program_id(1)

    @pl.when(jnp.logical_and(pl.program_id(0) == 0, j == 0))
    def _():
        rows_per = stage_ref.shape[1]
        n_stage = win_ref.shape[0] // rows_per

        def fetch(i):
            return pltpu.make_async_copy(win_hbm.at[pl.ds(i * rows_per, rows_per)], stage_ref.at[i % 2],
                                         stage_sem.at[i % 2])

        fetch(0).start()
        for i in range(n_stage):
            fetch(i).wait()
            if i + 1 < n_stage:
                fetch(i + 1).start()
            win_ref[i * rows_per:(i + 1) * rows_per, :] = stage_ref[i % 2].astype(BF16)
        wk = win_ref[:, COL_K:COL_K + RET_HEADS * RET_QK_DIM].astype(F32)
        wkt_ref[...] = jnp.transpose(wk).astype(BF16)
        for g in range(len(POOL_WINDOWS)):
            rows = slice(g * POOL_GROUP_DIM, (g + 1) * POOL_GROUP_DIM)
            wg = (wgrp_ref[g].astype(F32) * pscale_ref[:, rows]).astype(BF16)
            wpool_ref[rows, :] = _dot(wg, wp_ref[rows, :]).astype(BF16)

    @pl.when(j == 0)
    def _():
        state_ref[...] = jnp.zeros_like(state_ref)
        halo_ref[...] = jnp.zeros_like(halo_ref)

    x = x_ref[...]
    xb = x.astype(BF16)

    def proj(col, width):
        return _dot(xb, win_ref[:, col:col + width])

    part = {}
    strips = [slice(c, c + STRIP) for c in range(0, D_MODEL, STRIP)]

    def gate(branch, cols):
        return _sigmoid(proj(COL_GATES + branch * D_MODEL + cols.start, STRIP))

    def pool_part():
        def pool_gates():
            part["pool_gate"] = [gate(0, c) for c in strips]

        _pool_branch(proj(COL_POOL, POOL_WIDTH).astype(BF16), j, halo_ref, bmain_ref, bhalo_ref, ypool_ref,
                     pool_gates)
        part["pool"] = [part["pool_gate"][i] * _dot(ypool_ref[...], wpool_ref[:, c]) for i, c in enumerate(strips)]

    def xa_part():
        _cross_attention_branch(proj(COL_XAQ, XA_WIDTH).astype(BF16), km_ref, vm_ref, yxa_ref)
        part["xa"] = [gate(2, c) * _dot(yxa_ref[...], wa_ref[:, c]) for c in strips]

    def ret_gate_part():
        part["ret_gate"] = [gate(1, c) for c in strips]

    silu_g = []
    for h in range(RET_HEADS):
        gh = proj(COL_G + h * RET_V_DIM, RET_V_DIM)
        silu_g.append(gh * _sigmoid(gh))
    fillers = [pool_part, xa_part, ret_gate_part] + [lambda: None] * (tile // RET_CHUNK - 3)
    _retention_branch(proj(COL_Q, RET_HEADS * RET_QK_DIM), _dot_nt(wkt_ref[...], xb),
                      proj(COL_V, RET_HEADS * RET_V_DIM).astype(BF16), silu_g, cos_ref, sin_ref, cos_t_ref,
                      sin_t_ref, dmask_ref, qdec_ref, kdec_ref, gnw_ref, state_ref, rq_ref, rqd_ref, rkt_ref,
                      rkdt_ref, yret_ref, cdec, fillers)
    merged = jnp.concatenate(
        [(part["pool"][i] + part["ret_gate"][i] * _dot(yret_ref[...], wr_ref[:, c]) + part["xa"][i]).astype(BF16)
         for i, c in enumerate(strips)], axis=1)
    h = jnp.concatenate([ALPHA * x[:, c] + _dot(merged, wo_ref[:, c]) for c in strips], axis=1)
    x1 = jnp.concatenate([_layer_norm(h[r:r + LN_ROWS], lnw_ref[...], lnb_ref[...])
                          for r in range(0, tile, LN_ROWS)], axis=0)
    xp_ref[...] = _pack_halves(x1)
    x1_hi = x1.astype(BF16)
    x1_lo = (x1 - x1_hi.astype(F32)).astype(BF16)
    p4 = _dot(jnp.concatenate([x1_hi, x1_lo], axis=0), rw_ref[...])
    logits = (p4[:tile, :ROUTER_ROWS] + p4[:tile, ROUTER_ROWS:]) + (p4[tile:, :ROUTER_ROWS] + p4[tile:, ROUTER_ROWS:])
    e0, e1, w0, w1 = _route(jnp.transpose(logits) + rb_ref[...])
    eid_ref[...] = jnp.concatenate([e0, e1, jnp.zeros((SUBLANES - TOP_K, tile), I32)], axis=0)
    w_t = jnp.concatenate([w0, w1, jnp.zeros((WTOK_LANES - TOP_K, tile), F32)], axis=0)
    wtok_ref[...] = jnp.transpose(w_t)


def _mixer(xf, rope, kv, win, wgrp, pscale, gnw, wp, wr, wa, wo, lnw, lnb, rw, rb, batch, seq, mem_len,
           tile=512):
    T = xf.shape[0]
    nj = seq // tile
    cos, sin, cos_t, sin_t = rope
    bmain, bhalo = _pool_bands()
    dmask, qdec, kdec, cdec = _ret_consts()
    resident = lambda a: pl.BlockSpec(a.shape, lambda b, j: (0,) * a.ndim, pipeline_mode=pl.Buffered(1))
    rowblk = lambda w: pl.BlockSpec((tile, w), lambda b, j: (b * nj + j, 0))
    colblk = lambda r: pl.BlockSpec((r, tile), lambda b, j: (0, b * nj + j))
    consts = (wgrp, pscale, bmain, bhalo, dmask, qdec, kdec, gnw, wp, wr, wa, wo, lnw, lnb, rw, rb)
    return pl.pallas_call(
        functools.partial(_mixer_kernel, tile=tile, cdec=cdec),
        grid=(batch, nj),
        in_specs=[rowblk(D_MODEL), rowblk(RET_QK_DIM), rowblk(RET_QK_DIM),
                  colblk(RET_QK_DIM // 2), colblk(RET_QK_DIM // 2),
                  pl.BlockSpec((mem_len, XA_WIDTH), lambda b, j: (b, 0)),
                  pl.BlockSpec((mem_len, XA_WIDTH), lambda b, j: (b, 1)),
                  pl.BlockSpec(memory_space=pl.ANY)] + [resident(a) for a in consts],
        out_specs=[rowblk(PACK_W), colblk(SUBLANES), rowblk(WTOK_LANES)],
        out_shape=[jax.ShapeDtypeStruct((T, PACK_W), I32), jax.ShapeDtypeStruct((SUBLANES, T), I32),
                   jax.ShapeDtypeStruct((T, WTOK_LANES), F32)],
        scratch_shapes=[pltpu.VMEM((RET_HEADS, RET_QK_DIM, RET_V_DIM), F32),
                        pltpu.VMEM((2, POOL_HALO, POOL_WIDTH), BF16),
                        pltpu.VMEM((tile, POOL_WIDTH), BF16),
                        pltpu.VMEM((tile, RET_HEADS * RET_V_DIM), BF16),
                        pltpu.VMEM((tile, XA_WIDTH), BF16),
                        pltpu.VMEM((tile, RET_HEADS * RET_QK_DIM), BF16),
                        pltpu.VMEM((tile, RET_HEADS * RET_QK_DIM), BF16),
                        pltpu.VMEM((RET_HEADS * RET_QK_DIM, tile), BF16),
                        pltpu.VMEM((RET_HEADS * RET_QK_DIM, tile), BF16),
                        pltpu.VMEM((RET_HEADS * RET_QK_DIM, D_MODEL), BF16),
                        pltpu.VMEM((POOL_WIDTH, D_MODEL), BF16),
                        pltpu.VMEM(win.shape, BF16),
                        pltpu.VMEM((2, W_STAGE_ROWS, win.shape[1]), F32),
                        pltpu.SemaphoreType.DMA((2,))],
        compiler_params=pltpu.CompilerParams(dimension_semantics=("arbitrary", "arbitrary"),
                                             vmem_limit_bytes=VMEM_LIMIT),
        name="mixer",
    )(xf, cos, sin, cos_t, sin_t, kv, kv, win, *consts)


META_LANES = LANES


def _positions_kernel(eid_ref, tri_ref, low_ref, pos_ref, first_tile_ref, n_tiles_ref, *, n_tok):
    n_chunks = n_tok // RANK_CHUNK
    erow = lax.broadcasted_iota(I32, (N_EXPERTS, RANK_CHUNK), 0)

    def onehot(c):
        sl = slice(c * RANK_CHUNK, (c + 1) * RANK_CHUNK)
        m0 = eid_ref[0:1, sl] == erow
        m1 = eid_ref[1:2, sl] == erow
        return m0, m1, jnp.where(m0, 1.0, 0.0) + jnp.where(m1, 1.0, 0.0)

    counts = jnp.zeros((N_EXPERTS, 1), F32)
    for c in range(n_chunks):
        counts = counts + jnp.sum(onehot(c)[2], axis=1, keepdims=True)
    ptiles = jnp.floor((counts + (MOE_TM - 1)) * (1.0 / MOE_TM))
    ptiles_b = jnp.broadcast_to(ptiles, (N_EXPERTS, LANES)).astype(BF16)
    start = _dot(low_ref[...], ptiles_b)[:, 0:1] * MOE_TM

    pos_ref[...] = jnp.zeros_like(pos_ref)
    carry = start - 1.0
    for c in range(n_chunks):
        sl = slice(c * RANK_CHUNK, (c + 1) * RANK_CHUNK)
        m0, m1, oh = onehot(c)
        rank = _dot(oh.astype(BF16), tri_ref[...]) + carry
        pos_ref[0:1, sl] = jnp.sum(jnp.where(m0, rank, 0.0), axis=0, keepdims=True).astype(I32)
        pos_ref[1:2, sl] = jnp.sum(jnp.where(m1, rank, 0.0), axis=0, keepdims=True).astype(I32)
        carry = carry + jnp.sum(oh, axis=1, keepdims=True)

    first_tile_ref[...] = jnp.broadcast_to(start * (1.0 / MOE_TM), first_tile_ref.shape).astype(I32)
    n_tiles_ref[...] = jnp.broadcast_to(ptiles, n_tiles_ref.shape).astype(I32)


def _positions(eid):
    T = eid.shape[1]
    r = np.arange(RANK_CHUNK)
    tri = jnp.asarray(r[:, None] <= r[None, :], BF16)
    e = np.arange(N_EXPERTS)
    low = jnp.asarray(e[None, :] < e[:, None], BF16)
    full = lambda a: pl.BlockSpec(a.shape, lambda i: (0,) * a.ndim)
    return pl.pallas_call(
        functools.partial(_positions_kernel, n_tok=T),
        grid=(1,),
        in_specs=[full(eid), full(tri), full(low)],
        out_specs=[pl.BlockSpec((SUBLANES, T), lambda i: (0, 0)),
                   pl.BlockSpec((N_EXPERTS, META_LANES), lambda i: (0, 0)),
                   pl.BlockSpec((N_EXPERTS, META_LANES), lambda i: (0, 0))],
        out_shape=[jax.ShapeDtypeStruct((SUBLANES, T), I32), jax.ShapeDtypeStruct((N_EXPERTS, META_LANES), I32),
                   jax.ShapeDtypeStruct((N_EXPERTS, META_LANES), I32)],
        name="route_positions",
    )(eid, tri, low)


def _sc_workers():
    info = plsc.get_sparse_core_info()
    return info.num_cores, info.num_cores * info.num_subcores


def _sc_scatter_rows(xp, pos2d, n_out):
    T, W = xp.shape
    n_cores, n_workers = _sc_workers()
    cpw = T // SC_CHUNK // n_workers
    mesh = plsc.VectorSubcoreMesh(core_axis_name="c", subcore_axis_name="s")

    @functools.partial(
        pl.kernel, mesh=mesh, out_type=jax.ShapeDtypeStruct((n_out, W), I32),
        scratch_types=[pltpu.VMEM((TOP_K * cpw, SC_CHUNK), I32), pltpu.VMEM((2, SC_CHUNK, W), I32),
                       pltpu.SemaphoreType.DMA((2,)), pltpu.SemaphoreType.DMA((2,))],
        name="sc_scatter_rows")
    def k(x_hbm, pos_hbm, out_hbm, idx_v, rows_v, rd_sem, wr_sem):
        wid = lax.axis_index("s") * n_cores + lax.axis_index("c")
        for s in range(TOP_K):
            pltpu.sync_copy(pos_hbm.at[pl.ds(s * (T // SC_CHUNK) + wid * cpw, cpw)],
                            idx_v.at[pl.ds(s * cpw, cpw)])

        def read(j):
            return pltpu.make_async_copy(x_hbm.at[pl.ds((wid * cpw + j) * SC_CHUNK, SC_CHUNK)],
                                         rows_v.at[j % 2], rd_sem.at[j % 2])

        def write(j, s):
            return pltpu.make_async_copy(rows_v.at[j % 2], out_hbm.at[idx_v.at[s * cpw + j]], wr_sem.at[j % 2])

        _sc_two_buffer_stream(cpw, read, lambda j: [write(j, s) for s in range(TOP_K)])

    return k(xp, pos2d)


def _sc_two_buffer_stream(n, read, writes):
    read(0).start()
    for j in range(n):
        read(j).wait()
        if j + 1 < n:
            if j >= 1:
                for w in writes(j - 1):
                    w.wait()
            read(j + 1).start()
        for w in writes(j):
            w.start()
    for j in range(max(n - 2, 0), n):
        for w in writes(j):
            w.wait()


def _sc_gather_rows(y, idx2d):
    W = y.shape[1]
    n = idx2d.shape[0] * SC_CHUNK
    n_cores, n_workers = _sc_workers()
    cpw = n // SC_CHUNK // n_workers
    mesh = plsc.VectorSubcoreMesh(core_axis_name="c", subcore_axis_name="s")

    @functools.partial(
        pl.kernel, mesh=mesh, out_type=jax.ShapeDtypeStruct((n, W), I32),
        scratch_types=[pltpu.VMEM((cpw, SC_CHUNK), I32), pltpu.VMEM((2, SC_CHUNK, W), I32),
                       pltpu.SemaphoreType.DMA((2,)), pltpu.SemaphoreType.DMA((2,))],
        name="sc_gather_rows")
    def k(y_hbm, idx_hbm, out_hbm, idx_v, rows_v, rd_sem, wr_sem):
        wid = lax.axis_index("s") * n_cores + lax.axis_index("c")
        pltpu.sync_copy(idx_hbm.at[pl.ds(wid * cpw, cpw)], idx_v)

        def read(j):
            return pltpu.make_async_copy(y_hbm.at[idx_v.at[j]], rows_v.at[j % 2], rd_sem.at[j % 2])

        def write(j):
            return pltpu.make_async_copy(rows_v.at[j % 2], out_hbm.at[pl.ds((wid * cpw + j) * SC_CHUNK, SC_CHUNK)],
                                         wr_sem.at[j % 2])

        _sc_two_buffer_stream(cpw, read, lambda j: [write(j)])

    return k(y, idx2d)


def _routed_kernel(first_ref, count_ref, xs_hbm, wg_ref, wu_ref, wd_ref, ys_hbm, xbuf, ybuf, in_sem, out_sem):
    e = pl.program_id(0)
    last = pl.num_programs(0) - 1
    total = first_ref[last] + count_ref[last]

    def in_copy(g):
        slot = lax.rem(g, RING)
        return pltpu.make_async_copy(xs_hbm.at[pl.ds(g * MOE_TM, MOE_TM)], xbuf.at[slot], in_sem.at[slot])

    def out_copy(g):
        slot = lax.rem(g, RING)
        return pltpu.make_async_copy(ybuf.at[slot], ys_hbm.at[pl.ds(g * MOE_TM, MOE_TM)], out_sem.at[slot])

    @pl.when(e == 0)
    def _():
        for g0 in range(RING - 1):
            @pl.when(g0 < total)
            def _():
                in_copy(g0).start()

    wg = wg_ref[0].astype(BF16)
    wu = wu_ref[0].astype(BF16)
    wd = wd_ref[0].astype(BF16)

    def tile_step(i, carry):
        g = first_ref[e] + i
        slot = lax.rem(g, RING)
        in_copy(g).wait()

        @pl.when(g + RING - 1 < total)
        def _():
            in_copy(g + RING - 1).start()

        @pl.when(g >= RING)
        def _():
            out_copy(g - RING).wait()

        lo, hi = _unpack_halves(xbuf[slot])
        lo = lo.astype(BF16)
        hi = hi.astype(BF16)
        a = _dot(lo, wg[:PACK_W]) + _dot(hi, wg[PACK_W:])
        b = _dot(lo, wu[:PACK_W]) + _dot(hi, wu[PACK_W:])
        act = (a * _sigmoid(a) * b).astype(BF16)
        ybuf[slot] = _pack_halves(_dot(act, wd))
        out_copy(g).start()
        return carry

    lax.fori_loop(0, count_ref[e], tile_step, 0)

    @pl.when(e == last)
    def _():
        for back in range(RING, 0, -1):
            @pl.when(total >= back)
            def _():
                out_copy(total - back).wait()


def _routed_mlp(first_tile, n_tiles, xs, wg, wu, wd):
    R = xs.shape[0]
    any_space = pl.BlockSpec(memory_space=pl.ANY)
    return pl.pallas_call(
        _routed_kernel,
        grid_spec=pltpu.PrefetchScalarGridSpec(
            num_scalar_prefetch=2,
            grid=(N_EXPERTS,),
            in_specs=[any_space,
                      pl.BlockSpec((1, D_MODEL, D_EXPERT), lambda e, ft, nt: (e, 0, 0)),
                      pl.BlockSpec((1, D_MODEL, D_EXPERT), lambda e, ft, nt: (e, 0, 0)),
                      pl.BlockSpec((1, D_EXPERT, D_MODEL), lambda e, ft, nt: (e, 0, 0))],
            out_specs=any_space,
            scratch_shapes=[pltpu.VMEM((RING, MOE_TM, PACK_W), I32), pltpu.VMEM((RING, MOE_TM, PACK_W), I32),
                            pltpu.SemaphoreType.DMA((RING,)), pltpu.SemaphoreType.DMA((RING,))]),
        out_shape=jax.ShapeDtypeStruct((R, PACK_W), I32),
        compiler_params=pltpu.CompilerParams(dimension_semantics=("arbitrary",)),
        name="routed_mlp",
    )(first_tile, n_tiles, xs, wg, wu, wd)


def _combine_kernel(xp_ref, y0_ref, y1_ref, wtok_ref, lnw_ref, lnb_ref, *out_refs):
    o_ref = out_refs[-1]
    w0 = wtok_ref[:, 0:1]
    w1 = wtok_ref[:, 1:2]
    xlo, xhi = _unpack_halves(xp_ref[...])
    y0lo, y0hi = _unpack_halves(y0_ref[...])
    y1lo, y1hi = _unpack_halves(y1_ref[...])
    h = jnp.concatenate([ALPHA * xlo + (w0 * y0lo + w1 * y1lo), ALPHA * xhi + (w0 * y0hi + w1 * y1hi)], axis=1)
    o_ref[...] = _layer_norm(h, lnw_ref[...], lnb_ref[...])


def _combine_ln2(xp, yg_parts, wtok, lnw, lnb, tile=1024):
    T = xp.shape[0]
    n_parts = len(yg_parts)
    nt = T // tile // n_parts
    full = lambda a: pl.BlockSpec(a.shape, lambda i: (0,) * a.ndim)
    out = None
    for p, yg in enumerate(yg_parts):
        rows = lambda w, p=p: pl.BlockSpec((tile, w), lambda i: (i + p * nt, 0))
        in_specs = [rows(PACK_W),
                    pl.BlockSpec((tile, PACK_W), lambda i: (i, 0)),
                    pl.BlockSpec((tile, PACK_W), lambda i: (i + nt, 0)),
                    rows(WTOK_LANES), full(lnw), full(lnb)]
        args = [xp, yg, yg, wtok, lnw, lnb]
        aliases = {}
        if out is not None:
            in_specs.append(pl.BlockSpec(memory_space=pl.ANY))
            args.append(out)
            aliases = {len(args) - 1: 0}
        out = pl.pallas_call(
            _combine_kernel,
            grid=(nt,),
            in_specs=in_specs,
            out_specs=rows(D_MODEL),
            out_shape=jax.ShapeDtypeStruct((T, D_MODEL), F32),
            input_output_aliases=aliases,
            compiler_params=pltpu.CompilerParams(dimension_semantics=("arbitrary",), vmem_limit_bytes=VMEM_LIMIT),
            name="combine_ln2",
        )(*args)
    return out


def _router_params(w_grp, b_grp, w_exp, b_exp):
    rw = jnp.zeros((D_MODEL, ROUTER_ROWS), F32)
    rw = rw.at[:, 0:N_GROUPS].set(w_grp).at[:, EXP_ROW0:EXP_ROW0 + N_EXPERTS].set(w_exp)
    rb = jnp.zeros((ROUTER_ROWS,), F32).at[N_GROUPS:8].set(NEG_BIG)
    rb = rb.at[0:N_GROUPS].set(b_grp).at[EXP_ROW0:EXP_ROW0 + N_EXPERTS].set(b_exp)
    rw_hi = rw.astype(BF16)
    rw_lo = (rw - rw_hi.astype(F32)).astype(BF16)
    return jnp.concatenate([rw_hi, rw_lo], axis=1), rb[:, None]


def kernel(x, mem, positions, w_in, w_pool_grp, pool_scale, ret_gn_w, w_mem_kv, w_br_pool, w_br_ret, w_br_xa,
           w_out, ln1_w, ln1_b, w_grp_router, b_grp_router, w_exp_router, b_exp_router, w_exp_gate, w_exp_up,
           w_exp_down, ln2_w, ln2_b):
    B, S, D = x.shape
    assert D == D_MODEL and w_in.shape[0] == DEPTH and S % 512 == 0
    T = B * S
    M = mem.shape[1]
    l = 0
    xf = x.reshape(T, D)

    rope = _rope_table(positions.reshape(1, T))
    kv = _mem_kv(mem.reshape(B * M, D), w_mem_kv[l].astype(BF16))
    rw, rb = _router_params(w_grp_router[l], b_grp_router[l], w_exp_router[l], b_exp_router[l])
    xp, eid, wtok = _mixer(xf, rope, kv, w_in[l],
                           w_pool_grp[l], pool_scale[l][None, :], ret_gn_w[l].reshape(1, -1),
                           w_br_pool[l].astype(BF16), w_br_ret[l].astype(BF16), w_br_xa[l].astype(BF16),
                           w_out[l].astype(BF16), ln1_w[l][None, :], ln1_b[l][None, :], rw, rb, B, S, M)

    pos, first_tile, n_tiles = _positions(eid)
    pos2d = pos[0:TOP_K].reshape(TOP_K * T // SC_CHUNK, SC_CHUNK)
    max_tiles = (TOP_K * T + N_EXPERTS * (MOE_TM - 1)) // MOE_TM
    xs = _sc_scatter_rows(xp, pos2d, max_tiles * MOE_TM)
    ys = _routed_mlp(first_tile[:, 0], n_tiles[:, 0], xs,
                     w_exp_gate[l].reshape(N_EXPERTS, D_MODEL, D_EXPERT),
                     w_exp_up[l].reshape(N_EXPERTS, D_MODEL, D_EXPERT),
                     w_exp_down[l].reshape(N_EXPERTS, D_EXPERT, D_MODEL))
    rng = T // COMBINE_PARTS
    yg_parts = [_sc_gather_rows(ys, pos[0:TOP_K, p * rng:(p + 1) * rng].reshape(TOP_K * rng // SC_CHUNK, SC_CHUNK))
                for p in range(COMBINE_PARTS)]
    out = _combine_ln2(xp, yg_parts, wtok, ln2_w[l][None, :], ln2_b[l][None, :])
    return out.reshape(B, S, D)
```

```python
import functools

import numpy as np
import jax
import jax.numpy as jnp
from jax import lax
from jax.experimental import pallas as pl
from jax.experimental.pallas import tpu as pltpu
from jax.experimental.pallas import tpu_sc as plsc

F32 = jnp.float32
BF16 = jnp.bfloat16
I32 = jnp.int32
U32 = jnp.uint32

D_MODEL = 1024
POOL_WINDOWS = (2, 4, 8, 16)
POOL_GROUP_DIM = 128
POOL_WIDTH = 512
POOL_HALO = 16
RET_HEADS = 4
RET_QK_DIM = 128
RET_V_DIM = 256
RET_CHUNK = 128
ROPE_BASE = 10000.0
XA_HEADS = 4
XA_HEAD_DIM = 128
XA_WIDTH = 512
N_GROUPS = 4
EXPERTS_PER_GROUP = 8
N_EXPERTS = N_GROUPS * EXPERTS_PER_GROUP
D_EXPERT = 256
LN_EPS = 1e-5
DEPTH = 1
ALPHA = (2.0 * DEPTH) ** 0.25
NEG_BIG = -1e30

COL_POOL, COL_Q, COL_K, COL_V, COL_G, COL_XAQ, COL_GATES = 0, 512, 1024, 1536, 2560, 3584, 4096

V7X_VMEM_BYTES = 64 * 1024 * 1024
VMEM_LIMIT = V7X_VMEM_BYTES * 7 // 8
SUBLANES = 8
LANES = 128

TOP_K = 2
PACK_W = D_MODEL // 2
MOE_TM = 512
SC_CHUNK = 64
RANK_CHUNK = 512
COMBINE_PARTS = 2
RING = 4
STRIP = 256
LN_ROWS = 32


def _dot(a, b):
    return jnp.dot(a, b, preferred_element_type=F32)


def _dot_nt(a, b):
    return lax.dot_general(a, b, (((1,), (1,)), ((), ())), preferred_element_type=F32)


def _sigmoid(z):
    return 1.0 / (1.0 + jnp.exp2(z * (-1.0 / np.log(2.0))))


def _layer_norm(h, w, b):
    mu = jnp.mean(h, axis=-1, keepdims=True)
    hc = h - mu
    var = jnp.mean(hc * hc, axis=-1, keepdims=True)
    return hc * lax.rsqrt(var + LN_EPS) * w + b


ROPE_LO = 64
ROPE_PARTS = 3


def _rope_kernel(pos_ref, freq_ref, tab_ref, cos_t_ref, sin_t_ref):
    pos = pos_ref[...]

    def emit(cos_t, sin_t):
        cos_t_ref[...] = cos_t
        sin_t_ref[...] = sin_t

    in_table = jnp.logical_and(jnp.min(pos) >= 0, jnp.max(pos) < ROPE_LO * LANES)

    @pl.when(in_table)
    def _():
        idx = lax.broadcasted_iota(I32, (LANES, pos.shape[1]), 0)
        pick_hi = jnp.where(idx == jnp.right_shift(pos, ROPE_LO.bit_length() - 1), 1.0, 0.0).astype(BF16)
        pick_lo = jnp.where(idx == (pos & (ROPE_LO - 1)), 1.0, 0.0).astype(BF16)

        def look(k, pick):
            return sum(_dot(tab_ref[k * ROPE_PARTS + p], pick) for p in range(ROPE_PARTS))

        cos_a, sin_a, cos_b, sin_b = look(0, pick_hi), look(1, pick_hi), look(2, pick_lo), look(3, pick_lo)
        emit(cos_a * cos_b - sin_a * sin_b, sin_a * cos_b + cos_a * sin_b)

    @pl.when(jnp.logical_not(in_table))
    def _():
        ang = freq_ref[...] * pos.astype(F32)
        emit(jnp.cos(ang), jnp.sin(ang))


def _rope_tables(inv_freq):
    f = inv_freq.astype(np.float64)[:, None]
    idx = np.arange(LANES, dtype=np.float64)[None, :]
    tabs = [np.cos(ROPE_LO * idx * f), np.sin(ROPE_LO * idx * f), np.cos(idx * f), np.sin(idx * f)]
    pieces = []
    for t in tabs:
        rest = t.astype(np.float32)
        for _ in range(ROPE_PARTS):
            piece = rest.astype(BF16)
            pieces.append(piece)
            rest = rest - piece.astype(np.float32)
    return jnp.asarray(np.stack(pieces))


def _rope_table(pos_row, tile=2048):
    T = pos_row.shape[1]
    half = RET_QK_DIM // 2
    inv_freq = (ROPE_BASE ** (-np.arange(half, dtype=np.float64) / half)).astype(np.float32)
    freq = jnp.asarray(inv_freq[:, None])
    tabs = _rope_tables(inv_freq)
    out_t = pl.BlockSpec((half, tile), lambda i: (0, i))
    return pl.pallas_call(
        _rope_kernel,
        grid=(T // tile,),
        in_specs=[pl.BlockSpec((1, tile), lambda i: (0, i)), pl.BlockSpec((half, 1), lambda i: (0, 0)),
                  pl.BlockSpec(tabs.shape, lambda i: (0, 0, 0))],
        out_specs=[out_t, out_t],
        out_shape=[jax.ShapeDtypeStruct((half, T), F32)] * 2,
        name="rope_table",
    )(pos_row, freq, tabs)


POOL_SUB = 256


def _pool_bands():
    r = np.arange(POOL_SUB)[:, None]
    c = np.arange(POOL_SUB)[None, :]
    ch = np.arange(POOL_HALO)[None, :] - POOL_HALO
    main = np.stack([((r - c >= 0) & (r - c < w)) for w in POOL_WINDOWS]).astype(np.float32)
    halo = np.stack([((r - ch >= 0) & (r - ch < w)) for w in POOL_WINDOWS]).astype(np.float32)
    return jnp.asarray(main, BF16), jnp.asarray(halo, BF16)


def _pool_branch(ub, j, halo_ref, bmain_ref, bhalo_ref, o_ref, filler):
    tile = ub.shape[0]
    s0 = j * tile
    slot = lax.rem(j, 2)
    blocks = [(sb * POOL_SUB, g) for sb in range(tile // POOL_SUB) for g in range(len(POOL_WINDOWS))]
    wsum = {}
    for r0, g in blocks:
        cols = slice(g * POOL_GROUP_DIM, (g + 1) * POOL_GROUP_DIM)
        prev = halo_ref[slot] if r0 == 0 else ub[r0 - POOL_HALO:r0]
        wsum[r0, g] = _dot(bmain_ref[g], ub[r0:r0 + POOL_SUB, cols]) + _dot(bhalo_ref[g], prev[:, cols])
    filler()
    for r0, g in blocks:
        cols = slice(g * POOL_GROUP_DIM, (g + 1) * POOL_GROUP_DIM)
        pos = s0 + r0 + lax.broadcasted_iota(I32, (POOL_SUB, POOL_GROUP_DIM), 0)
        cnt = jnp.minimum(pos + 1, POOL_WINDOWS[g]).astype(F32)
        pooled = wsum[r0, g] / cnt - ub[r0:r0 + POOL_SUB, cols].astype(F32)
        o_ref[r0:r0 + POOL_SUB, cols] = pooled.astype(BF16)
    halo_ref[1 - slot] = ub[tile - POOL_HALO:tile]


def _ret_consts():
    h = np.arange(RET_HEADS, dtype=np.float64)
    log_gamma = np.log1p(-np.exp2(-5.0 - h))
    pos = np.arange(RET_CHUNK, dtype=np.float64)
    diff = pos[:, None] - pos[None, :]
    kscale = RET_QK_DIM ** -0.5
    dmask = kscale * np.where(diff >= 0, np.exp(log_gamma[:, None, None] * np.maximum(diff, 0.0)), 0.0)
    qdec = np.exp(log_gamma[:, None] * (pos + 1.0)[None, :])
    kdec = kscale * np.exp(log_gamma[:, None] * (RET_CHUNK - 1.0 - pos)[None, :])
    cdec = np.exp(log_gamma * RET_CHUNK)
    lanes = lambda a: np.broadcast_to(a[:, :, None], (RET_HEADS, RET_CHUNK, RET_QK_DIM))
    kdec_t = np.broadcast_to(kdec[:, None, :], (RET_HEADS, RET_QK_DIM, RET_CHUNK))
    return (jnp.asarray(dmask, F32), jnp.asarray(lanes(qdec), F32), jnp.asarray(kdec_t, F32),
            tuple(float(v) for v in cdec))


def _retention_branch(q, k_t, v, silu_g, cos_t_ref, sin_t_ref, dmask_ref, qdec_ref, kdec_ref,
                      gnw_ref, state_ref, rq_ref, rqd_ref, rkt_ref, rkdt_ref, o_ref, cdec, fillers):
    tile = q.shape[0]
    n_chunks = tile // RET_CHUNK
    half = RET_QK_DIM // 2
    cos_t = cos_t_ref[...]
    sin_t = sin_t_ref[...]
    cos = jnp.transpose(jnp.concatenate([cos_t, cos_t], axis=0))
    sin = jnp.transpose(jnp.concatenate([-sin_t, sin_t], axis=0))
    for h in range(RET_HEADS):
        qk = slice(h * RET_QK_DIM, (h + 1) * RET_QK_DIM)
        qh = q[:, qk]
        qr = qh * cos + pltpu.roll(qh, half, 1) * sin
        rq_ref[:, qk] = qr.astype(BF16)
        rqd_ref[:, qk] = (qr * jnp.concatenate([qdec_ref[h]] * n_chunks, axis=0)).astype(BF16)
        k1 = k_t[h * RET_QK_DIM:h * RET_QK_DIM + half]
        k2 = k_t[h * RET_QK_DIM + half:(h + 1) * RET_QK_DIM]
        kr_t = jnp.concatenate([k1 * cos_t - k2 * sin_t, k2 * cos_t + k1 * sin_t], axis=0)
        rkt_ref[qk, :] = kr_t.astype(BF16)
        rkdt_ref[qk, :] = (kr_t * jnp.concatenate([kdec_ref[h]] * n_chunks, axis=1)).astype(BF16)

    chunks = [slice(c * RET_CHUNK, (c + 1) * RET_CHUNK) for c in range(n_chunks)]
    heads = [slice(h * RET_QK_DIM, (h + 1) * RET_QK_DIM) for h in range(RET_HEADS)]
    v_heads = [slice(h * RET_V_DIM, (h + 1) * RET_V_DIM) for h in range(RET_HEADS)]
    raw = {(c, h): _dot(rq_ref[chunks[c], heads[h]], rkt_ref[heads[h], chunks[c]])
           for c in range(n_chunks) for h in range(RET_HEADS)}
    incr = {(c, h): _dot(rkdt_ref[heads[h], chunks[c]], v[chunks[c], v_heads[h]])
            for c in range(n_chunks) for h in range(RET_HEADS)}
    state_in = {}
    for h in range(RET_HEADS):
        st = state_ref[h]
        for c in range(n_chunks):
            state_in[c, h] = st.astype(BF16)
            st = cdec[h] * st + incr[c, h]
        state_ref[h] = st

    for c in range(n_chunks):
        rows = chunks[c]
        fillers[c]()
        for h in range(RET_HEADS):
            qk = heads[h]
            v_cols = v_heads[h]
            scores = raw[c, h] * dmask_ref[h]
            lhs = jnp.concatenate([scores.astype(BF16), rqd_ref[rows, qk]], axis=1)
            y = _dot(lhs, jnp.concatenate([v[rows, v_cols], state_in[c, h]], axis=0))
            mu = jnp.mean(y, axis=-1, keepdims=True)
            yc = y - mu
            var = jnp.mean(yc * yc, axis=-1, keepdims=True)
            yn = yc * lax.rsqrt(var + LN_EPS) * gnw_ref[:, v_cols]
            o_ref[rows, v_cols] = (silu_g[h][rows] * yn).astype(BF16)


def _memkv_kernel(m_ref, w_ref, o_ref):
    o_ref[...] = _dot(m_ref[...].astype(BF16), w_ref[...]).astype(BF16)


def _mem_kv(memf, w_b):
    M, D = memf.shape
    N = w_b.shape[1]
    return pl.pallas_call(
        _memkv_kernel,
        grid=(1,),
        in_specs=[pl.BlockSpec((M, D), lambda i: (0, 0)), pl.BlockSpec((D, N), lambda i: (0, 0))],
        out_specs=pl.BlockSpec((M, N), lambda i: (0, 0)),
        out_shape=jax.ShapeDtypeStruct((M, N), BF16),
        name="mem_kv",
    )(memf, w_b)


def _cross_attention_branch(xq, k_ref, v_ref, o_ref):
    scale = XA_HEAD_DIM ** -0.5
    for h in range(XA_HEADS):
        cols = slice(h * XA_HEAD_DIM, (h + 1) * XA_HEAD_DIM)
        s = _dot_nt(xq[:, cols], k_ref[:, cols]) * scale
        m = jnp.max(s, axis=-1, keepdims=True)
        p = jnp.exp(s - m)
        l = jnp.sum(p, axis=-1, keepdims=True)
        o = _dot(p.astype(BF16), v_ref[:, cols]) / l
        o_ref[:, cols] = o.astype(BF16)


ROUTER_ROWS = 128
WTOK_LANES = 128
EXP_ROW0 = 8


def _route(logits_t):
    gl = logits_t[0:8]
    gmax = jnp.max(gl, axis=0, keepdims=True)
    p_grp = 1.0 / jnp.sum(jnp.exp(gl - gmax), axis=0, keepdims=True)
    idx8 = lax.broadcasted_iota(jnp.int32, gl.shape, 0)
    gsel = jnp.min(jnp.where(gl == gmax, idx8, 8), axis=0, keepdims=True)
    cl = jnp.zeros_like(gl)
    for g in range(N_GROUPS):
        r0 = EXP_ROW0 + g * EXPERTS_PER_GROUP
        cl = cl + jnp.where(gsel == g, logits_t[r0:r0 + EXPERTS_PER_GROUP], 0.0)
    v1 = jnp.max(cl, axis=0, keepdims=True)
    i1 = jnp.min(jnp.where(cl == v1, idx8, 8), axis=0, keepdims=True)
    cl2 = jnp.where(idx8 == i1, -jnp.inf, cl)
    v2 = jnp.max(cl2, axis=0, keepdims=True)
    i2 = jnp.min(jnp.where(cl2 == v2, idx8, 8), axis=0, keepdims=True)
    e21 = jnp.exp(v2 - v1)
    w1 = p_grp / (1.0 + e21)
    w2 = p_grp * e21 / (1.0 + e21)
    return gsel * EXPERTS_PER_GROUP + i1, gsel * EXPERTS_PER_GROUP + i2, w1, w2


def _pack_halves(v):
    half = v.shape[1] // 2
    lo = lax.bitcast_convert_type(v[:, :half].astype(BF16).astype(F32), U32)
    hi = lax.bitcast_convert_type(v[:, half:].astype(BF16).astype(F32), U32)
    return lax.bitcast_convert_type(lax.shift_right_logical(lo, U32(16)) | hi, I32)


def _unpack_halves(w):
    u = lax.bitcast_convert_type(w, U32)
    lo = lax.bitcast_convert_type(lax.shift_left(u, U32(16)), F32)
    hi = lax.bitcast_convert_type(u & U32(0xFFFF0000), F32)
    return lo, hi


def _mixer_kernel(x_ref, cos_t_ref, sin_t_ref, km_ref, vm_ref, win_ref, wgrp_ref,
                  pscale_ref, bmain_ref, bhalo_ref, dmask_ref, qdec_ref, kdec_ref, gnw_ref, wp_ref, wr_ref,
                  wa_ref, wo_ref, lnw_ref, lnb_ref, rw_ref, rb_ref, xp_ref, eid_ref, wtok_ref,
                  state_ref, halo_ref, ypool_ref, yret_ref, yxa_ref, rq_ref, rqd_ref, rkt_ref, rkdt_ref, wkt_ref,
                  wpool_ref, *, tile, cdec):
    j = pl.program_id(1)

    @pl.when(jnp.logical_and(pl.program_id(0) == 0, j == 0))
    def _():
        wk = win_ref[:, COL_K:COL_K + RET_HEADS * RET_QK_DIM].astype(F32)
        wkt_ref[...] = jnp.transpose(wk).astype(BF16)
        for g in range(len(POOL_WINDOWS)):
            rows = slice(g * POOL_GROUP_DIM, (g + 1) * POOL_GROUP_DIM)
            wg = (wgrp_ref[g].astype(F32) * pscale_ref[:, rows]).astype(BF16)
            wpool_ref[rows, :] = _dot(wg, wp_ref[rows, :]).astype(BF16)

    @pl.when(j == 0)
    def _():
        state_ref[...] = jnp.zeros_like(state_ref)
        halo_ref[...] = jnp.zeros_like(halo_ref)

    x = x_ref[...]
    xb = x.astype(BF16)

    def proj(col, width):
        return _dot(xb, win_ref[:, col:col + width])

    part = {}
    strips = [slice(c, c + STRIP) for c in range(0, D_MODEL, STRIP)]

    def gate(branch, cols):
        return _sigmoid(proj(COL_GATES + branch * D_MODEL + cols.start, STRIP))

    def pool_part():
        def pool_gates():
            part["pool_gate"] = [gate(0, c) for c in strips]

        _pool_branch(proj(COL_POOL, POOL_WIDTH).astype(BF16), j, halo_ref, bmain_ref, bhalo_ref, ypool_ref,
                     pool_gates)
        part["pool"] = [part["pool_gate"][i] * _dot(ypool_ref[...], wpool_ref[:, c]) for i, c in enumerate(strips)]

    def xa_part():
        _cross_attention_branch(proj(COL_XAQ, XA_WIDTH).astype(BF16), km_ref, vm_ref, yxa_ref)
        part["xa"] = [gate(2, c) * _dot(yxa_ref[...], wa_ref[:, c]) for c in strips]

    def ret_gate_part():
        part["ret_gate"] = [gate(1, c) for c in strips]

    silu_g = []
    for h in range(RET_HEADS):
        gh = proj(COL_G + h * RET_V_DIM, RET_V_DIM)
        silu_g.append(gh * _sigmoid(gh))
    fillers = [pool_part, xa_part, ret_gate_part] + [lambda: None] * (tile // RET_CHUNK - 3)
    _retention_branch(proj(COL_Q, RET_HEADS * RET_QK_DIM), _dot_nt(wkt_ref[...], xb),
                      proj(COL_V, RET_HEADS * RET_V_DIM).astype(BF16), silu_g, cos_t_ref, sin_t_ref, dmask_ref,
                      qdec_ref, kdec_ref, gnw_ref, state_ref, rq_ref, rqd_ref, rkt_ref, rkdt_ref, yret_ref, cdec,
                      fillers)
    merged = jnp.concatenate(
        [(part["pool"][i] + part["ret_gate"][i] * _dot(yret_ref[...], wr_ref[:, c]) + part["xa"][i]).astype(BF16)
         for i, c in enumerate(strips)], axis=1)
    h = jnp.concatenate([ALPHA * x[:, c] + _dot(merged, wo_ref[:, c]) for c in strips], axis=1)
    x1 = jnp.concatenate([_layer_norm(h[r:r + LN_ROWS], lnw_ref[...], lnb_ref[...])
                          for r in range(0, tile, LN_ROWS)], axis=0)
    xp_ref[...] = _pack_halves(x1)
    x1_hi = x1.astype(BF16)
    x1_lo = (x1 - x1_hi.astype(F32)).astype(BF16)
    p4 = _dot(jnp.concatenate([x1_hi, x1_lo], axis=0), rw_ref[...])
    logits = (p4[:tile, :ROUTER_ROWS] + p4[:tile, ROUTER_ROWS:]) + (p4[tile:, :ROUTER_ROWS] + p4[tile:, ROUTER_ROWS:])
    e0, e1, w0, w1 = _route(jnp.transpose(logits) + rb_ref[...])
    eid_ref[...] = jnp.concatenate([e0, e1, jnp.zeros((SUBLANES - TOP_K, tile), I32)], axis=0)
    w_t = jnp.concatenate([w0, w1, jnp.zeros((WTOK_LANES - TOP_K, tile), F32)], axis=0)
    wtok_ref[...] = jnp.transpose(w_t)


def _mixer(xf, rope, kv, win, wgrp, pscale, gnw, wp, wr, wa, wo, lnw, lnb, rw, rb, batch, seq, mem_len,
           tile=512):
    T = xf.shape[0]
    nj = seq // tile
    cos_t, sin_t = rope
    bmain, bhalo = _pool_bands()
    dmask, qdec, kdec, cdec = _ret_consts()
    resident = lambda a: pl.BlockSpec(a.shape, lambda b, j: (0,) * a.ndim, pipeline_mode=pl.Buffered(1))
    rowblk = lambda w: pl.BlockSpec((tile, w), lambda b, j: (b * nj + j, 0))
    colblk = lambda r: pl.BlockSpec((r, tile), lambda b, j: (0, b * nj + j))
    consts = (win, wgrp, pscale, bmain, bhalo, dmask, qdec, kdec, gnw, wp, wr, wa, wo, lnw, lnb, rw, rb)
    return pl.pallas_call(
        functools.partial(_mixer_kernel, tile=tile, cdec=cdec),
        grid=(batch, nj),
        in_specs=[rowblk(D_MODEL), colblk(RET_QK_DIM // 2), colblk(RET_QK_DIM // 2),
                  pl.BlockSpec((mem_len, XA_WIDTH), lambda b, j: (b, 0)),
                  pl.BlockSpec((mem_len, XA_WIDTH), lambda b, j: (b, 1))] + [resident(a) for a in consts],
        out_specs=[rowblk(PACK_W), colblk(SUBLANES), rowblk(WTOK_LANES)],
        out_shape=[jax.ShapeDtypeStruct((T, PACK_W), I32), jax.ShapeDtypeStruct((SUBLANES, T), I32),
                   jax.ShapeDtypeStruct((T, WTOK_LANES), F32)],
        scratch_shapes=[pltpu.VMEM((RET_HEADS, RET_QK_DIM, RET_V_DIM), F32),
                        pltpu.VMEM((2, POOL_HALO, POOL_WIDTH), BF16),
                        pltpu.VMEM((tile, POOL_WIDTH), BF16),
                        pltpu.VMEM((tile, RET_HEADS * RET_V_DIM), BF16),
                        pltpu.VMEM((tile, XA_WIDTH), BF16),
                        pltpu.VMEM((tile, RET_HEADS * RET_QK_DIM), BF16),
                        pltpu.VMEM((tile, RET_HEADS * RET_QK_DIM), BF16),
                        pltpu.VMEM((RET_HEADS * RET_QK_DIM, tile), BF16),
                        pltpu.VMEM((RET_HEADS * RET_QK_DIM, tile), BF16),
                        pltpu.VMEM((RET_HEADS * RET_QK_DIM, D_MODEL), BF16),
                        pltpu.VMEM((POOL_WIDTH, D_MODEL), BF16)],
        compiler_params=pltpu.CompilerParams(dimension_semantics=("arbitrary", "arbitrary"),
                                             vmem_limit_bytes=VMEM_LIMIT),
        name="mixer",
    )(xf, cos_t, sin_t, kv, kv, *consts)


META_LANES = LANES


def _positions_kernel(eid_ref, tri_ref, low_ref, pos_ref, first_tile_ref, n_tiles_ref, *, n_tok):
    n_chunks = n_tok // RANK_CHUNK
    erow = lax.broadcasted_iota(I32, (N_EXPERTS, RANK_CHUNK), 0)

    def onehot(c):
        sl = slice(c * RANK_CHUNK, (c + 1) * RANK_CHUNK)
        m0 = eid_ref[0:1, sl] == erow
        m1 = eid_ref[1:2, sl] == erow
        return m0, m1, jnp.where(m0, 1.0, 0.0) + jnp.where(m1, 1.0, 0.0)

    counts = jnp.zeros((N_EXPERTS, 1), F32)
    for c in range(n_chunks):
        counts = counts + jnp.sum(onehot(c)[2], axis=1, keepdims=True)
    ptiles = jnp.floor((counts + (MOE_TM - 1)) * (1.0 / MOE_TM))
    ptiles_b = jnp.broadcast_to(ptiles, (N_EXPERTS, LANES)).astype(BF16)
    start = _dot(low_ref[...], ptiles_b)[:, 0:1] * MOE_TM

    pos_ref[...] = jnp.zeros_like(pos_ref)
    carry = start - 1.0
    for c in range(n_chunks):
        sl = slice(c * RANK_CHUNK, (c + 1) * RANK_CHUNK)
        m0, m1, oh = onehot(c)
        rank = _dot(oh.astype(BF16), tri_ref[...]) + carry
        pos_ref[0:1, sl] = jnp.sum(jnp.where(m0, rank, 0.0), axis=0, keepdims=True).astype(I32)
        pos_ref[1:2, sl] = jnp.sum(jnp.where(m1, rank, 0.0), axis=0, keepdims=True).astype(I32)
        carry = carry + jnp.sum(oh, axis=1, keepdims=True)

    first_tile_ref[...] = jnp.broadcast_to(start * (1.0 / MOE_TM), first_tile_ref.shape).astype(I32)
    n_tiles_ref[...] = jnp.broadcast_to(ptiles, n_tiles_ref.shape).astype(I32)


def _positions(eid):
    T = eid.shape[1]
    r = np.arange(RANK_CHUNK)
    tri = jnp.asarray(r[:, None] <= r[None, :], BF16)
    e = np.arange(N_EXPERTS)
    low = jnp.asarray(e[None, :] < e[:, None], BF16)
    full = lambda a: pl.BlockSpec(a.shape, lambda i: (0,) * a.ndim)
    return pl.pallas_call(
        functools.partial(_positions_kernel, n_tok=T),
        grid=(1,),
        in_specs=[full(eid), full(tri), full(low)],
        out_specs=[pl.BlockSpec((SUBLANES, T), lambda i: (0, 0)),
                   pl.BlockSpec((N_EXPERTS, META_LANES), lambda i: (0, 0)),
                   pl.BlockSpec((N_EXPERTS, META_LANES), lambda i: (0, 0))],
        out_shape=[jax.ShapeDtypeStruct((SUBLANES, T), I32), jax.ShapeDtypeStruct((N_EXPERTS, META_LANES), I32),
                   jax.ShapeDtypeStruct((N_EXPERTS, META_LANES), I32)],
        name="route_positions",
    )(eid, tri, low)


def _sc_workers():
    info = plsc.get_sparse_core_info()
    return info.num_cores, info.num_cores * info.num_subcores


def _sc_scatter_rows(xp, pos2d, n_out):
    T, W = xp.shape
    n_cores, n_workers = _sc_workers()
    cpw = T // SC_CHUNK // n_workers
    mesh = plsc.VectorSubcoreMesh(core_axis_name="c", subcore_axis_name="s")

    @functools.partial(
        pl.kernel, mesh=mesh, out_type=jax.ShapeDtypeStruct((n_out, W), I32),
        scratch_types=[pltpu.VMEM((TOP_K * cpw, SC_CHUNK), I32), pltpu.VMEM((2, SC_CHUNK, W), I32),
                       pltpu.SemaphoreType.DMA((2,)), pltpu.SemaphoreType.DMA((2,))],
        name="sc_scatter_rows")
    def k(x_hbm, pos_hbm, out_hbm, idx_v, rows_v, rd_sem, wr_sem):
        wid = lax.axis_index("s") * n_cores + lax.axis_index("c")
        for s in range(TOP_K):
            pltpu.sync_copy(pos_hbm.at[pl.ds(s * (T // SC_CHUNK) + wid * cpw, cpw)],
                            idx_v.at[pl.ds(s * cpw, cpw)])

        def read(j):
            return pltpu.make_async_copy(x_hbm.at[pl.ds((wid * cpw + j) * SC_CHUNK, SC_CHUNK)],
                                         rows_v.at[j % 2], rd_sem.at[j % 2])

        def write(j, s):
            return pltpu.make_async_copy(rows_v.at[j % 2], out_hbm.at[idx_v.at[s * cpw + j]], wr_sem.at[j % 2])

        _sc_two_buffer_stream(cpw, read, lambda j: [write(j, s) for s in range(TOP_K)])

    return k(xp, pos2d)


def _sc_two_buffer_stream(n, read, writes):
    read(0).start()
    for j in range(n):
        read(j).wait()
        if j + 1 < n:
            if j >= 1:
                for w in writes(j - 1):
                    w.wait()
            read(j + 1).start()
        for w in writes(j):
            w.start()
    for j in range(max(n - 2, 0), n):
        for w in writes(j):
            w.wait()


def _sc_gather_rows(y, idx2d):
    W = y.shape[1]
    n = idx2d.shape[0] * SC_CHUNK
    n_cores, n_workers = _sc_workers()
    cpw = n // SC_CHUNK // n_workers
    mesh = plsc.VectorSubcoreMesh(core_axis_name="c", subcore_axis_name="s")

    @functools.partial(
        pl.kernel, mesh=mesh, out_type=jax.ShapeDtypeStruct((n, W), I32),
        scratch_types=[pltpu.VMEM((cpw, SC_CHUNK), I32), pltpu.VMEM((2, SC_CHUNK, W), I32),
                       pltpu.SemaphoreType.DMA((2,)), pltpu.SemaphoreType.DMA((2,))],
        name="sc_gather_rows")
    def k(y_hbm, idx_hbm, out_hbm, idx_v, rows_v, rd_sem, wr_sem):
        wid = lax.axis_index("s") * n_cores + lax.axis_index("c")
        pltpu.sync_copy(idx_hbm.at[pl.ds(wid * cpw, cpw)], idx_v)

        def read(j):
            return pltpu.make_async_copy(y_hbm.at[idx_v.at[j]], rows_v.at[j % 2], rd_sem.at[j % 2])

        def write(j):
            return pltpu.make_async_copy(rows_v.at[j % 2], out_hbm.at[pl.ds((wid * cpw + j) * SC_CHUNK, SC_CHUNK)],
                                         wr_sem.at[j % 2])

        _sc_two_buffer_stream(cpw, read, lambda j: [write(j)])

    return k(y, idx2d)


def _routed_kernel(first_ref, count_ref, xs_hbm, wg_ref, wu_ref, wd_ref, ys_hbm, xbuf, ybuf, in_sem, out_sem):
    e = pl.program_id(0)
    last = pl.num_programs(0) - 1
    total = first_ref[last] + count_ref[last]

    def in_copy(g):
        slot = lax.rem(g, RING)
        return pltpu.make_async_copy(xs_hbm.at[pl.ds(g * MOE_TM, MOE_TM)], xbuf.at[slot], in_sem.at[slot])

    def out_copy(g):
        slot = lax.rem(g, RING)
        return pltpu.make_async_copy(ybuf.at[slot], ys_hbm.at[pl.ds(g * MOE_TM, MOE_TM)], out_sem.at[slot])

    @pl.when(e == 0)
    def _():
        for g0 in range(RING - 1):
            @pl.when(g0 < total)
            def _():
                in_copy(g0).start()

    wg = wg_ref[0].astype(BF16)
    wu = wu_ref[0].astype(BF16)
    wd = wd_ref[0].astype(BF16)

    def tile_step(i, carry):
        g = first_ref[e] + i
        slot = lax.rem(g, RING)
        in_copy(g).wait()

        @pl.when(g + RING - 1 < total)
        def _():
            in_copy(g + RING - 1).start()

        @pl.when(g >= RING)
        def _():
            out_copy(g - RING).wait()

        lo, hi = _unpack_halves(xbuf[slot])
        lo = lo.astype(BF16)
        hi = hi.astype(BF16)
        a = _dot(lo, wg[:PACK_W]) + _dot(hi, wg[PACK_W:])
        b = _dot(lo, wu[:PACK_W]) + _dot(hi, wu[PACK_W:])
        act = (a * _sigmoid(a) * b).astype(BF16)
        ybuf[slot] = _pack_halves(_dot(act, wd))
        out_copy(g).start()
        return carry

    lax.fori_loop(0, count_ref[e], tile_step, 0)

    @pl.when(e == last)
    def _():
        for back in range(RING, 0, -1):
            @pl.when(total >= back)
            def _():
                out_copy(total - back).wait()


def _routed_mlp(first_tile, n_tiles, xs, wg, wu, wd):
    R = xs.shape[0]
    any_space = pl.BlockSpec(memory_space=pl.ANY)
    return pl.pallas_call(
        _routed_kernel,
        grid_spec=pltpu.PrefetchScalarGridSpec(
            num_scalar_prefetch=2,
            grid=(N_EXPERTS,),
            in_specs=[any_space,
                      pl.BlockSpec((1, D_MODEL, D_EXPERT), lambda e, ft, nt: (e, 0, 0)),
                      pl.BlockSpec((1, D_MODEL, D_EXPERT), lambda e, ft, nt: (e, 0, 0)),
                      pl.BlockSpec((1, D_EXPERT, D_MODEL), lambda e, ft, nt: (e, 0, 0))],
            out_specs=any_space,
            scratch_shapes=[pltpu.VMEM((RING, MOE_TM, PACK_W), I32), pltpu.VMEM((RING, MOE_TM, PACK_W), I32),
                            pltpu.SemaphoreType.DMA((RING,)), pltpu.SemaphoreType.DMA((RING,))]),
        out_shape=jax.ShapeDtypeStruct((R, PACK_W), I32),
        compiler_params=pltpu.CompilerParams(dimension_semantics=("arbitrary",)),
        name="routed_mlp",
    )(first_tile, n_tiles, xs, wg, wu, wd)


def _combine_kernel(xp_ref, y0_ref, y1_ref, wtok_ref, lnw_ref, lnb_ref, *out_refs):
    o_ref = out_refs[-1]
    w0 = wtok_ref[:, 0:1]
    w1 = wtok_ref[:, 1:2]
    xlo, xhi = _unpack_halves(xp_ref[...])
    y0lo, y0hi = _unpack_halves(y0_ref[...])
    y1lo, y1hi = _unpack_halves(y1_ref[...])
    h = jnp.concatenate([ALPHA * xlo + (w0 * y0lo + w1 * y1lo), ALPHA * xhi + (w0 * y0hi + w1 * y1hi)], axis=1)
    o_ref[...] = _layer_norm(h, lnw_ref[...], lnb_ref[...])


def _combine_ln2(xp, yg_parts, wtok, lnw, lnb, tile=1024):
    T = xp.shape[0]
    n_parts = len(yg_parts)
    nt = T // tile // n_parts
    full = lambda a: pl.BlockSpec(a.shape, lambda i: (0,) * a.ndim)
    out = None
    for p, yg in enumerate(yg_parts):
        rows = lambda w, p=p: pl.BlockSpec((tile, w), lambda i: (i + p * nt, 0))
        in_specs = [rows(PACK_W),
                    pl.BlockSpec((tile, PACK_W), lambda i: (i, 0)),
                    pl.BlockSpec((tile, PACK_W), lambda i: (i + nt, 0)),
                    rows(WTOK_LANES), full(lnw), full(lnb)]
        args = [xp, yg, yg, wtok, lnw, lnb]
        aliases = {}
        if out is not None:
            in_specs.append(pl.BlockSpec(memory_space=pl.ANY))
            args.append(out)
            aliases = {len(args) - 1: 0}
        out = pl.pallas_call(
            _combine_kernel,
            grid=(nt,),
            in_specs=in_specs,
            out_specs=rows(D_MODEL),
            out_shape=jax.ShapeDtypeStruct((T, D_MODEL), F32),
            input_output_aliases=aliases,
            compiler_params=pltpu.CompilerParams(dimension_semantics=("arbitrary",), vmem_limit_bytes=VMEM_LIMIT),
            name="combine_ln2",
        )(*args)
    return out


def _router_params(w_grp, b_grp, w_exp, b_exp):
    rw = jnp.zeros((D_MODEL, ROUTER_ROWS), F32)
    rw = rw.at[:, 0:N_GROUPS].set(w_grp).at[:, EXP_ROW0:EXP_ROW0 + N_EXPERTS].set(w_exp)
    rb = jnp.zeros((ROUTER_ROWS,), F32).at[N_GROUPS:8].set(NEG_BIG)
    rb = rb.at[0:N_GROUPS].set(b_grp).at[EXP_ROW0:EXP_ROW0 + N_EXPERTS].set(b_exp)
    rw_hi = rw.astype(BF16)
    rw_lo = (rw - rw_hi.astype(F32)).astype(BF16)
    return jnp.concatenate([rw_hi, rw_lo], axis=1), rb[:, None]


def kernel(x, mem, positions, w_in, w_pool_grp, pool_scale, ret_gn_w, w_mem_kv, w_br_pool, w_br_ret, w_br_xa,
           w_out, ln1_w, ln1_b, w_grp_router, b_grp_router, w_exp_router, b_exp_router, w_exp_gate, w_exp_up,
           w_exp_down, ln2_w, ln2_b):
    B, S, D = x.shape
    assert D == D_MODEL and w_in.shape[0] == DEPTH and S % 512 == 0
    T = B * S
    M = mem.shape[1]
    l = 0
    xf = x.reshape(T, D)

    rope = _rope_table(positions.reshape(1, T))
    kv = _mem_kv(mem.reshape(B * M, D), w_mem_kv[l].astype(BF16))
    rw, rb = _router_params(w_grp_router[l], b_grp_router[l], w_exp_router[l], b_exp_router[l])
    xp, eid, wtok = _mixer(xf, rope, kv, w_in[l].astype(BF16),
                           w_pool_grp[l], pool_scale[l][None, :], ret_gn_w[l].reshape(1, -1),
                           w_br_pool[l].astype(BF16), w_br_ret[l].astype(BF16), w_br_xa[l].astype(BF16),
                           w_out[l].astype(BF16), ln1_w[l][None, :], ln1_b[l][None, :], rw, rb, B, S, M)

    pos, first_tile, n_tiles = _positions(eid)
    pos2d = pos[0:TOP_K].reshape(TOP_K * T // SC_CHUNK, SC_CHUNK)
    max_tiles = (TOP_K * T + N_EXPERTS * (MOE_TM - 1)) // MOE_TM
    xs = _sc_scatter_rows(xp, pos2d, max_tiles * MOE_TM)
    ys = _routed_mlp(first_tile[:, 0], n_tiles[:, 0], xs,
                     w_exp_gate[l].reshape(N_EXPERTS, D_MODEL, D_EXPERT),
                     w_exp_up[l].reshape(N_EXPERTS, D_MODEL, D_EXPERT),
                     w_exp_down[l].reshape(N_EXPERTS, D_EXPERT, D_MODEL))
    rng = T // COMBINE_PARTS
    yg_parts = [_sc_gather_rows(ys, pos[0:TOP_K, p * rng:(p + 1) * rng].reshape(TOP_K * rng // SC_CHUNK, SC_CHUNK))
                for p in range(COMBINE_PARTS)]
    out = _combine_ln2(xp, yg_parts, wtok, ln2_w[l][None, :], ln2_b[l][None, :])
    return out.reshape(B, S, D)
```

```python
import functools

import numpy as np
import jax
import jax.numpy as jnp
from jax import lax
from jax.experimental import pallas as pl
from jax.experimental.pallas import tpu as pltpu
from jax.experimental.pallas import tpu_sc as plsc

F32 = jnp.float32
BF16 = jnp.bfloat16
I32 = jnp.int32
U32 = jnp.uint32

D_MODEL = 1024
POOL_WINDOWS = (2, 4, 8, 16)
POOL_GROUP_DIM = 128
POOL_WIDTH = 512
POOL_HALO = 16
RET_HEADS = 4
RET_QK_DIM = 128
RET_V_DIM = 256
RET_CHUNK = 128
ROPE_BASE = 10000.0
XA_HEADS = 4
XA_HEAD_DIM = 128
XA_WIDTH = 512
N_GROUPS = 4
EXPERTS_PER_GROUP = 8
N_EXPERTS = N_GROUPS * EXPERTS_PER_GROUP
D_EXPERT = 256
LN_EPS = 1e-5
DEPTH = 1
ALPHA = (2.0 * DEPTH) ** 0.25
NEG_BIG = -1e30

COL_POOL, COL_Q, COL_K, COL_V, COL_G, COL_XAQ, COL_GATES = 0, 512, 1024, 1536, 2560, 3584, 4096

V7X_VMEM_BYTES = 64 * 1024 * 1024
VMEM_LIMIT = V7X_VMEM_BYTES * 7 // 8
SUBLANES = 8
LANES = 128

TOP_K = 2
PACK_W = D_MODEL // 2
MOE_TM = 512
SC_CHUNK = 64
RANK_CHUNK = 512
COMBINE_PARTS = 2
RING = 4
STRIP = 256
LN_ROWS = 32


def _dot(a, b):
    return jnp.dot(a, b, preferred_element_type=F32)


def _dot_nt(a, b):
    return lax.dot_general(a, b, (((1,), (1,)), ((), ())), preferred_element_type=F32)


def _sigmoid(z):
    return 1.0 / (1.0 + jnp.exp2(z * (-1.0 / np.log(2.0))))


def _layer_norm(h, w, b):
    mu = jnp.mean(h, axis=-1, keepdims=True)
    hc = h - mu
    var = jnp.mean(hc * hc, axis=-1, keepdims=True)
    return hc * lax.rsqrt(var + LN_EPS) * w + b


ROPE_LO = 64
ROPE_PARTS = 3


def _rope_kernel(pos_ref, freq_ref, tab_ref, cos_t_ref, sin_t_ref):
    pos = pos_ref[...]

    def emit(cos_t, sin_t):
        cos_t_ref[...] = cos_t
        sin_t_ref[...] = sin_t

    in_table = jnp.logical_and(jnp.min(pos) >= 0, jnp.max(pos) < ROPE_LO * LANES)

    @pl.when(in_table)
    def _():
        idx = lax.broadcasted_iota(I32, (LANES, pos.shape[1]), 0)
        pick_hi = jnp.where(idx == jnp.right_shift(pos, ROPE_LO.bit_length() - 1), 1.0, 0.0).astype(BF16)
        pick_lo = jnp.where(idx == (pos & (ROPE_LO - 1)), 1.0, 0.0).astype(BF16)

        def look(k, pick):
            return sum(_dot(tab_ref[k * ROPE_PARTS + p], pick) for p in range(ROPE_PARTS))

        cos_a, sin_a, cos_b, sin_b = look(0, pick_hi), look(1, pick_hi), look(2, pick_lo), look(3, pick_lo)
        emit(cos_a * cos_b - sin_a * sin_b, sin_a * cos_b + cos_a * sin_b)

    @pl.when(jnp.logical_not(in_table))
    def _():
        ang = freq_ref[...] * pos.astype(F32)
        emit(jnp.cos(ang), jnp.sin(ang))


def _rope_tables(inv_freq):
    f = inv_freq.astype(np.float64)[:, None]
    idx = np.arange(LANES, dtype=np.float64)[None, :]
    tabs = [np.cos(ROPE_LO * idx * f), np.sin(ROPE_LO * idx * f), np.cos(idx * f), np.sin(idx * f)]
    pieces = []
    for t in tabs:
        rest = t.astype(np.float32)
        for _ in range(ROPE_PARTS):
            piece = rest.astype(BF16)
            pieces.append(piece)
            rest = rest - piece.astype(np.float32)
    return jnp.asarray(np.stack(pieces))


def _rope_table(pos_row, tile=2048):
    T = pos_row.shape[1]
    half = RET_QK_DIM // 2
    inv_freq = (ROPE_BASE ** (-np.arange(half, dtype=np.float64) / half)).astype(np.float32)
    freq = jnp.asarray(inv_freq[:, None])
    tabs = _rope_tables(inv_freq)
    out_t = pl.BlockSpec((half, tile), lambda i: (0, i))
    return pl.pallas_call(
        _rope_kernel,
        grid=(T // tile,),
        in_specs=[pl.BlockSpec((1, tile), lambda i: (0, i)), pl.BlockSpec((half, 1), lambda i: (0, 0)),
                  pl.BlockSpec(tabs.shape, lambda i: (0, 0, 0))],
        out_specs=[out_t, out_t],
        out_shape=[jax.ShapeDtypeStruct((half, T), F32)] * 2,
        name="rope_table",
    )(pos_row, freq, tabs)


POOL_SUB = 256


def _pool_bands():
    r = np.arange(POOL_SUB)[:, None]
    c = np.arange(POOL_SUB)[None, :]
    ch = np.arange(POOL_HALO)[None, :] - POOL_HALO
    main = np.stack([((r - c >= 0) & (r - c < w)) for w in POOL_WINDOWS]).astype(np.float32)
    halo = np.stack([((r - ch >= 0) & (r - ch < w)) for w in POOL_WINDOWS]).astype(np.float32)
    return jnp.asarray(main, BF16), jnp.asarray(halo, BF16)


def _pool_branch(ub, j, halo_ref, bmain_ref, bhalo_ref, o_ref, filler):
    tile = ub.shape[0]
    s0 = j * tile
    slot = lax.rem(j, 2)
    blocks = [(sb * POOL_SUB, g) for sb in range(tile // POOL_SUB) for g in range(len(POOL_WINDOWS))]
    wsum = {}
    for r0, g in blocks:
        cols = slice(g * POOL_GROUP_DIM, (g + 1) * POOL_GROUP_DIM)
        prev = halo_ref[slot] if r0 == 0 else ub[r0 - POOL_HALO:r0]
        wsum[r0, g] = _dot(bmain_ref[g], ub[r0:r0 + POOL_SUB, cols]) + _dot(bhalo_ref[g], prev[:, cols])
    filler()
    for r0, g in blocks:
        cols = slice(g * POOL_GROUP_DIM, (g + 1) * POOL_GROUP_DIM)
        pos = s0 + r0 + lax.broadcasted_iota(I32, (POOL_SUB, POOL_GROUP_DIM), 0)
        cnt = jnp.minimum(pos + 1, POOL_WINDOWS[g]).astype(F32)
        pooled = wsum[r0, g] / cnt - ub[r0:r0 + POOL_SUB, cols].astype(F32)
        o_ref[r0:r0 + POOL_SUB, cols] = pooled.astype(BF16)
    halo_ref[1 - slot] = ub[tile - POOL_HALO:tile]


def _ret_consts():
    h = np.arange(RET_HEADS, dtype=np.float64)
    log_gamma = np.log1p(-np.exp2(-5.0 - h))
    pos = np.arange(RET_CHUNK, dtype=np.float64)
    diff = pos[:, None] - pos[None, :]
    kscale = RET_QK_DIM ** -0.5
    dmask = kscale * np.where(diff >= 0, np.exp(log_gamma[:, None, None] * np.maximum(diff, 0.0)), 0.0)
    qdec = np.exp(log_gamma[:, None] * (pos + 1.0)[None, :])
    kdec = kscale * np.exp(log_gamma[:, None] * (RET_CHUNK - 1.0 - pos)[None, :])
    cdec = np.exp(log_gamma * RET_CHUNK)
    lanes = lambda a: np.broadcast_to(a[:, :, None], (RET_HEADS, RET_CHUNK, RET_QK_DIM))
    kdec_t = np.broadcast_to(kdec[:, None, :], (RET_HEADS, RET_QK_DIM, RET_CHUNK))
    return (jnp.asarray(dmask, F32), jnp.asarray(lanes(qdec), F32), jnp.asarray(kdec_t, F32),
            tuple(float(v) for v in cdec))


def _retention_branch(q, k_t, v, silu_g, cos_t_ref, sin_t_ref, dmask_ref, qdec_ref, kdec_ref,
                      gnw_ref, state_ref, rq_ref, rqd_ref, rkt_ref, rkdt_ref, o_ref, cdec, fillers):
    tile = q.shape[0]
    n_chunks = tile // RET_CHUNK
    half = RET_QK_DIM // 2
    cos_t = cos_t_ref[...]
    sin_t = sin_t_ref[...]
    cos = jnp.transpose(jnp.concatenate([cos_t, cos_t], axis=0))
    sin = jnp.transpose(jnp.concatenate([-sin_t, sin_t], axis=0))
    for h in range(RET_HEADS):
        qk = slice(h * RET_QK_DIM, (h + 1) * RET_QK_DIM)
        qh = q[:, qk]
        qr = qh * cos + pltpu.roll(qh, half, 1) * sin
        rq_ref[:, qk] = qr.astype(BF16)
        rqd_ref[:, qk] = (qr * jnp.concatenate([qdec_ref[h]] * n_chunks, axis=0)).astype(BF16)
        k1 = k_t[h * RET_QK_DIM:h * RET_QK_DIM + half]
        k2 = k_t[h * RET_QK_DIM + half:(h + 1) * RET_QK_DIM]
        kr_t = jnp.concatenate([k1 * cos_t - k2 * sin_t, k2 * cos_t + k1 * sin_t], axis=0)
        rkt_ref[qk, :] = kr_t.astype(BF16)
        rkdt_ref[qk, :] = (kr_t * jnp.concatenate([kdec_ref[h]] * n_chunks, axis=1)).astype(BF16)

    chunks = [slice(c * RET_CHUNK, (c + 1) * RET_CHUNK) for c in range(n_chunks)]
    heads = [slice(h * RET_QK_DIM, (h + 1) * RET_QK_DIM) for h in range(RET_HEADS)]
    v_heads = [slice(h * RET_V_DIM, (h + 1) * RET_V_DIM) for h in range(RET_HEADS)]
    raw = {(c, h): _dot(rq_ref[chunks[c], heads[h]], rkt_ref[heads[h], chunks[c]])
           for c in range(n_chunks) for h in range(RET_HEADS)}
    incr = {(c, h): _dot(rkdt_ref[heads[h], chunks[c]], v[chunks[c], v_heads[h]])
            for c in range(n_chunks) for h in range(RET_HEADS)}
    state_in = {}
    for h in range(RET_HEADS):
        st = state_ref[h]
        for c in range(n_chunks):
            state_in[c, h] = st.astype(BF16)
            st = cdec[h] * st + incr[c, h]
        state_ref[h] = st

    for c in range(n_chunks):
        rows = chunks[c]
        fillers[c]()
        for h in range(RET_HEADS):
            qk = heads[h]
            v_cols = v_heads[h]
            scores = raw[c, h] * dmask_ref[h]
            lhs = jnp.concatenate([scores.astype(BF16), rqd_ref[rows, qk]], axis=1)
            y = _dot(lhs, jnp.concatenate([v[rows, v_cols], state_in[c, h]], axis=0))
            mu = jnp.mean(y, axis=-1, keepdims=True)
            yc = y - mu
            var = jnp.mean(yc * yc, axis=-1, keepdims=True)
            yn = yc * lax.rsqrt(var + LN_EPS) * gnw_ref[:, v_cols]
            o_ref[rows, v_cols] = (silu_g[h][rows] * yn).astype(BF16)


def _memkv_kernel(m_ref, w_ref, o_ref):
    o_ref[...] = _dot(m_ref[...].astype(BF16), w_ref[...]).astype(BF16)


def _mem_kv(memf, w_b):
    M, D = memf.shape
    N = w_b.shape[1]
    return pl.pallas_call(
        _memkv_kernel,
        grid=(1,),
        in_specs=[pl.BlockSpec((M, D), lambda i: (0, 0)), pl.BlockSpec((D, N), lambda i: (0, 0))],
        out_specs=pl.BlockSpec((M, N), lambda i: (0, 0)),
        out_shape=jax.ShapeDtypeStruct((M, N), BF16),
        name="mem_kv",
    )(memf, w_b)


def _cross_attention_branch(xq, k_ref, v_ref, o_ref):
    scale = XA_HEAD_DIM ** -0.5
    for h in range(XA_HEADS):
        cols = slice(h * XA_HEAD_DIM, (h + 1) * XA_HEAD_DIM)
        s = _dot_nt(xq[:, cols], k_ref[:, cols]) * scale
        m = jnp.max(s, axis=-1, keepdims=True)
        p = jnp.exp(s - m)
        l = jnp.sum(p, axis=-1, keepdims=True)
        o = _dot(p.astype(BF16), v_ref[:, cols]) / l
        o_ref[:, cols] = o.astype(BF16)


ROUTER_ROWS = 128
EXP_ROW0 = 8


def _route(logits_t):
    gl = logits_t[0:8]
    gmax = jnp.max(gl, axis=0, keepdims=True)
    p_grp = 1.0 / jnp.sum(jnp.exp(gl - gmax), axis=0, keepdims=True)
    idx8 = lax.broadcasted_iota(jnp.int32, gl.shape, 0)
    gsel = jnp.min(jnp.where(gl == gmax, idx8, 8), axis=0, keepdims=True)
    cl = jnp.zeros_like(gl)
    for g in range(N_GROUPS):
        r0 = EXP_ROW0 + g * EXPERTS_PER_GROUP
        cl = cl + jnp.where(gsel == g, logits_t[r0:r0 + EXPERTS_PER_GROUP], 0.0)
    v1 = jnp.max(cl, axis=0, keepdims=True)
    i1 = jnp.min(jnp.where(cl == v1, idx8, 8), axis=0, keepdims=True)
    cl2 = jnp.where(idx8 == i1, -jnp.inf, cl)
    v2 = jnp.max(cl2, axis=0, keepdims=True)
    i2 = jnp.min(jnp.where(cl2 == v2, idx8, 8), axis=0, keepdims=True)
    e21 = jnp.exp(v2 - v1)
    w1 = p_grp / (1.0 + e21)
    w2 = p_grp * e21 / (1.0 + e21)
    return gsel * EXPERTS_PER_GROUP + i1, gsel * EXPERTS_PER_GROUP + i2, w1, w2


def _pack_halves(v):
    half = v.shape[1] // 2
    lo = lax.bitcast_convert_type(v[:, :half].astype(BF16).astype(F32), U32)
    hi = lax.bitcast_convert_type(v[:, half:].astype(BF16).astype(F32), U32)
    return lax.bitcast_convert_type(lax.shift_right_logical(lo, U32(16)) | hi, I32)


def _unpack_halves(w):
    u = lax.bitcast_convert_type(w, U32)
    lo = lax.bitcast_convert_type(lax.shift_left(u, U32(16)), F32)
    hi = lax.bitcast_convert_type(u & U32(0xFFFF0000), F32)
    return lo, hi


def _mixer_kernel(x_ref, cos_t_ref, sin_t_ref, km_ref, vm_ref, win_ref, wgrp_ref,
                  pscale_ref, bmain_ref, bhalo_ref, dmask_ref, qdec_ref, kdec_ref, gnw_ref, wp_ref, wr_ref,
                  wa_ref, wo_ref, lnw_ref, lnb_ref, rw_ref, rb_ref, xp_ref, route_ref,
                  state_ref, halo_ref, ypool_ref, yret_ref, yxa_ref, rq_ref, rqd_ref, rkt_ref, rkdt_ref, wkt_ref,
                  wpool_ref, *, tile, cdec):
    j = pl.program_id(1)

    @pl.when(jnp.logical_and(pl.program_id(0) == 0, j == 0))
    def _():
        wk = win_ref[:, COL_K:COL_K + RET_HEADS * RET_QK_DIM].astype(F32)
        wkt_ref[...] = jnp.transpose(wk).astype(BF16)
        for g in range(len(POOL_WINDOWS)):
            rows = slice(g * POOL_GROUP_DIM, (g + 1) * POOL_GROUP_DIM)
            wg = (wgrp_ref[g].astype(F32) * pscale_ref[:, rows]).astype(BF16)
            wpool_ref[rows, :] = _dot(wg, wp_ref[rows, :]).astype(BF16)

    @pl.when(j == 0)
    def _():
        state_ref[...] = jnp.zeros_like(state_ref)
        halo_ref[...] = jnp.zeros_like(halo_ref)

    x = x_ref[...]
    xb = x.astype(BF16)

    def proj(col, width):
        return _dot(xb, win_ref[:, col:col + width])

    part = {}
    strips = [slice(c, c + STRIP) for c in range(0, D_MODEL, STRIP)]

    def gate(branch, cols):
        return _sigmoid(proj(COL_GATES + branch * D_MODEL + cols.start, STRIP))

    def pool_part():
        def pool_gates():
            part["pool_gate"] = [gate(0, c) for c in strips]

        _pool_branch(proj(COL_POOL, POOL_WIDTH).astype(BF16), j, halo_ref, bmain_ref, bhalo_ref, ypool_ref,
                     pool_gates)
        part["pool"] = [part["pool_gate"][i] * _dot(ypool_ref[...], wpool_ref[:, c]) for i, c in enumerate(strips)]

    def xa_part():
        _cross_attention_branch(proj(COL_XAQ, XA_WIDTH).astype(BF16), km_ref, vm_ref, yxa_ref)
        part["xa"] = [gate(2, c) * _dot(yxa_ref[...], wa_ref[:, c]) for c in strips]

    def ret_gate_part():
        part["ret_gate"] = [gate(1, c) for c in strips]

    silu_g = []
    for h in range(RET_HEADS):
        gh = proj(COL_G + h * RET_V_DIM, RET_V_DIM)
        silu_g.append(gh * _sigmoid(gh))
    fillers = [pool_part, xa_part, ret_gate_part] + [lambda: None] * (tile // RET_CHUNK - 3)
    _retention_branch(proj(COL_Q, RET_HEADS * RET_QK_DIM), _dot_nt(wkt_ref[...], xb),
                      proj(COL_V, RET_HEADS * RET_V_DIM).astype(BF16), silu_g, cos_t_ref, sin_t_ref, dmask_ref,
                      qdec_ref, kdec_ref, gnw_ref, state_ref, rq_ref, rqd_ref, rkt_ref, rkdt_ref, yret_ref, cdec,
                      fillers)
    merged = jnp.concatenate(
        [(part["pool"][i] + part["ret_gate"][i] * _dot(yret_ref[...], wr_ref[:, c]) + part["xa"][i]).astype(BF16)
         for i, c in enumerate(strips)], axis=1)
    h = jnp.concatenate([ALPHA * x[:, c] + _dot(merged, wo_ref[:, c]) for c in strips], axis=1)
    x1 = jnp.concatenate([_layer_norm(h[r:r + LN_ROWS], lnw_ref[...], lnb_ref[...])
                          for r in range(0, tile, LN_ROWS)], axis=0)
    xp_ref[...] = _pack_halves(x1)
    x1_hi = x1.astype(BF16)
    x1_lo = (x1 - x1_hi.astype(F32)).astype(BF16)
    p4 = _dot(jnp.concatenate([x1_hi, x1_lo], axis=0), rw_ref[...])
    logits = (p4[:tile, :ROUTER_ROWS] + p4[:tile, ROUTER_ROWS:]) + (p4[tile:, :ROUTER_ROWS] + p4[tile:, ROUTER_ROWS:])
    e0, e1, w0, w1 = _route(jnp.transpose(logits) + rb_ref[...])
    w_bits = [lax.bitcast_convert_type(w, I32) for w in (w0, w1)]
    route_ref[...] = jnp.concatenate([e0, e1, *w_bits, jnp.zeros((SUBLANES - 2 * TOP_K, tile), I32)], axis=0)


def _mixer(xf, rope, kv, win, wgrp, pscale, gnw, wp, wr, wa, wo, lnw, lnb, rw, rb, batch, seq, mem_len,
           tile=512):
    T = xf.shape[0]
    nj = seq // tile
    cos_t, sin_t = rope
    bmain, bhalo = _pool_bands()
    dmask, qdec, kdec, cdec = _ret_consts()
    resident = lambda a: pl.BlockSpec(a.shape, lambda b, j: (0,) * a.ndim, pipeline_mode=pl.Buffered(1))
    rowblk = lambda w: pl.BlockSpec((tile, w), lambda b, j: (b * nj + j, 0))
    colblk = lambda r: pl.BlockSpec((r, tile), lambda b, j: (0, b * nj + j))
    consts = (win, wgrp, pscale, bmain, bhalo, dmask, qdec, kdec, gnw, wp, wr, wa, wo, lnw, lnb, rw, rb)
    return pl.pallas_call(
        functools.partial(_mixer_kernel, tile=tile, cdec=cdec),
        grid=(batch, nj),
        in_specs=[rowblk(D_MODEL), colblk(RET_QK_DIM // 2), colblk(RET_QK_DIM // 2),
                  pl.BlockSpec((mem_len, XA_WIDTH), lambda b, j: (b, 0)),
                  pl.BlockSpec((mem_len, XA_WIDTH), lambda b, j: (b, 1))] + [resident(a) for a in consts],
        out_specs=[rowblk(PACK_W), colblk(SUBLANES)],
        out_shape=[jax.ShapeDtypeStruct((T, PACK_W), I32), jax.ShapeDtypeStruct((SUBLANES, T), I32)],
        scratch_shapes=[pltpu.VMEM((RET_HEADS, RET_QK_DIM, RET_V_DIM), F32),
                        pltpu.VMEM((2, POOL_HALO, POOL_WIDTH), BF16),
                        pltpu.VMEM((tile, POOL_WIDTH), BF16),
                        pltpu.VMEM((tile, RET_HEADS * RET_V_DIM), BF16),
                        pltpu.VMEM((tile, XA_WIDTH), BF16),
                        pltpu.VMEM((tile, RET_HEADS * RET_QK_DIM), BF16),
                        pltpu.VMEM((tile, RET_HEADS * RET_QK_DIM), BF16),
                        pltpu.VMEM((RET_HEADS * RET_QK_DIM, tile), BF16),
                        pltpu.VMEM((RET_HEADS * RET_QK_DIM, tile), BF16),
                        pltpu.VMEM((RET_HEADS * RET_QK_DIM, D_MODEL), BF16),
                        pltpu.VMEM((POOL_WIDTH, D_MODEL), BF16)],
        compiler_params=pltpu.CompilerParams(dimension_semantics=("arbitrary", "arbitrary"),
                                             vmem_limit_bytes=VMEM_LIMIT),
        name="mixer",
    )(xf, cos_t, sin_t, kv, kv, *consts)


META_LANES = LANES


def _positions_kernel(eid_ref, tri_ref, low_ref, pos_ref, first_tile_ref, n_tiles_ref, *, n_tok):
    n_chunks = n_tok // RANK_CHUNK
    erow = lax.broadcasted_iota(I32, (N_EXPERTS, RANK_CHUNK), 0)

    def onehot(c):
        sl = slice(c * RANK_CHUNK, (c + 1) * RANK_CHUNK)
        m0 = eid_ref[0:1, sl] == erow
        m1 = eid_ref[1:2, sl] == erow
        return m0, m1, jnp.where(m0, 1.0, 0.0) + jnp.where(m1, 1.0, 0.0)

    counts = jnp.zeros((N_EXPERTS, 1), F32)
    for c in range(n_chunks):
        counts = counts + jnp.sum(onehot(c)[2], axis=1, keepdims=True)
    ptiles = jnp.floor((counts + (MOE_TM - 1)) * (1.0 / MOE_TM))
    ptiles_b = jnp.broadcast_to(ptiles, (N_EXPERTS, LANES)).astype(BF16)
    start = _dot(low_ref[...], ptiles_b)[:, 0:1] * MOE_TM

    pos_ref[...] = jnp.zeros_like(pos_ref)
    carry = start - 1.0
    for c in range(n_chunks):
        sl = slice(c * RANK_CHUNK, (c + 1) * RANK_CHUNK)
        m0, m1, oh = onehot(c)
        rank = _dot(oh.astype(BF16), tri_ref[...]) + carry
        pos_ref[0:1, sl] = jnp.sum(jnp.where(m0, rank, 0.0), axis=0, keepdims=True).astype(I32)
        pos_ref[1:2, sl] = jnp.sum(jnp.where(m1, rank, 0.0), axis=0, keepdims=True).astype(I32)
        carry = carry + jnp.sum(oh, axis=1, keepdims=True)

    first_tile_ref[...] = jnp.broadcast_to(start * (1.0 / MOE_TM), first_tile_ref.shape).astype(I32)
    n_tiles_ref[...] = jnp.broadcast_to(ptiles, n_tiles_ref.shape).astype(I32)


def _positions(eid):
    T = eid.shape[1]
    r = np.arange(RANK_CHUNK)
    tri = jnp.asarray(r[:, None] <= r[None, :], BF16)
    e = np.arange(N_EXPERTS)
    low = jnp.asarray(e[None, :] < e[:, None], BF16)
    full = lambda a: pl.BlockSpec(a.shape, lambda i: (0,) * a.ndim)
    return pl.pallas_call(
        functools.partial(_positions_kernel, n_tok=T),
        grid=(1,),
        in_specs=[full(eid), full(tri), full(low)],
        out_specs=[pl.BlockSpec((SUBLANES, T), lambda i: (0, 0)),
                   pl.BlockSpec((N_EXPERTS, META_LANES), lambda i: (0, 0)),
                   pl.BlockSpec((N_EXPERTS, META_LANES), lambda i: (0, 0))],
        out_shape=[jax.ShapeDtypeStruct((SUBLANES, T), I32), jax.ShapeDtypeStruct((N_EXPERTS, META_LANES), I32),
                   jax.ShapeDtypeStruct((N_EXPERTS, META_LANES), I32)],
        name="route_positions",
    )(eid, tri, low)


def _sc_workers():
    info = plsc.get_sparse_core_info()
    return info.num_cores, info.num_cores * info.num_subcores


def _sc_scatter_rows(xp, pos2d, n_out):
    T, W = xp.shape
    n_cores, n_workers = _sc_workers()
    cpw = T // SC_CHUNK // n_workers
    mesh = plsc.VectorSubcoreMesh(core_axis_name="c", subcore_axis_name="s")

    @functools.partial(
        pl.kernel, mesh=mesh, out_type=jax.ShapeDtypeStruct((n_out, W), I32),
        scratch_types=[pltpu.VMEM((TOP_K * cpw, SC_CHUNK), I32), pltpu.VMEM((2, SC_CHUNK, W), I32),
                       pltpu.SemaphoreType.DMA((2,)), pltpu.SemaphoreType.DMA((2,))],
        name="sc_scatter_rows")
    def k(x_hbm, pos_hbm, out_hbm, idx_v, rows_v, rd_sem, wr_sem):
        wid = lax.axis_index("s") * n_cores + lax.axis_index("c")
        for s in range(TOP_K):
            pltpu.sync_copy(pos_hbm.at[pl.ds(s * (T // SC_CHUNK) + wid * cpw, cpw)],
                            idx_v.at[pl.ds(s * cpw, cpw)])

        def read(j):
            return pltpu.make_async_copy(x_hbm.at[pl.ds((wid * cpw + j) * SC_CHUNK, SC_CHUNK)],
                                         rows_v.at[j % 2], rd_sem.at[j % 2])

        def write(j, s):
            return pltpu.make_async_copy(rows_v.at[j % 2], out_hbm.at[idx_v.at[s * cpw + j]], wr_sem.at[j % 2])

        _sc_two_buffer_stream(cpw, read, lambda j: [write(j, s) for s in range(TOP_K)])

    return k(xp, pos2d)


def _sc_two_buffer_stream(n, read, writes):
    read(0).start()
    for j in range(n):
        read(j).wait()
        if j + 1 < n:
            if j >= 1:
                for w in writes(j - 1):
                    w.wait()
            read(j + 1).start()
        for w in writes(j):
            w.start()
    for j in range(max(n - 2, 0), n):
        for w in writes(j):
            w.wait()


def _sc_gather_rows(y, idx2d):
    W = y.shape[1]
    n = idx2d.shape[0] * SC_CHUNK
    n_cores, n_workers = _sc_workers()
    cpw = n // SC_CHUNK // n_workers
    mesh = plsc.VectorSubcoreMesh(core_axis_name="c", subcore_axis_name="s")

    @functools.partial(
        pl.kernel, mesh=mesh, out_type=jax.ShapeDtypeStruct((n, W), I32),
        scratch_types=[pltpu.VMEM((cpw, SC_CHUNK), I32), pltpu.VMEM((2, SC_CHUNK, W), I32),
                       pltpu.SemaphoreType.DMA((2,)), pltpu.SemaphoreType.DMA((2,))],
        name="sc_gather_rows")
    def k(y_hbm, idx_hbm, out_hbm, idx_v, rows_v, rd_sem, wr_sem):
        wid = lax.axis_index("s") * n_cores + lax.axis_index("c")
        pltpu.sync_copy(idx_hbm.at[pl.ds(wid * cpw, cpw)], idx_v)

        def read(j):
            return pltpu.make_async_copy(y_hbm.at[idx_v.at[j]], rows_v.at[j % 2], rd_sem.at[j % 2])

        def write(j):
            return pltpu.make_async_copy(rows_v.at[j % 2], out_hbm.at[pl.ds((wid * cpw + j) * SC_CHUNK, SC_CHUNK)],
                                         wr_sem.at[j % 2])

        _sc_two_buffer_stream(cpw, read, lambda j: [write(j)])

    return k(y, idx2d)


def _routed_kernel(first_ref, count_ref, xs_hbm, wg_ref, wu_ref, wd_ref, ys_hbm, xbuf, ybuf, in_sem, out_sem):
    e = pl.program_id(0)
    last = pl.num_programs(0) - 1
    total = first_ref[last] + count_ref[last]

    def in_copy(g):
        slot = lax.rem(g, RING)
        return pltpu.make_async_copy(xs_hbm.at[pl.ds(g * MOE_TM, MOE_TM)], xbuf.at[slot], in_sem.at[slot])

    def out_copy(g):
        slot = lax.rem(g, RING)
        return pltpu.make_async_copy(ybuf.at[slot], ys_hbm.at[pl.ds(g * MOE_TM, MOE_TM)], out_sem.at[slot])

    @pl.when(e == 0)
    def _():
        for g0 in range(RING - 1):
            @pl.when(g0 < total)
            def _():
                in_copy(g0).start()

    wg = wg_ref[0].astype(BF16)
    wu = wu_ref[0].astype(BF16)
    wd = wd_ref[0].astype(BF16)

    def tile_step(i, carry):
        g = first_ref[e] + i
        slot = lax.rem(g, RING)
        in_copy(g).wait()

        @pl.when(g + RING - 1 < total)
        def _():
            in_copy(g + RING - 1).start()

        @pl.when(g >= RING)
        def _():
            out_copy(g - RING).wait()

        lo, hi = _unpack_halves(xbuf[slot])
        lo = lo.astype(BF16)
        hi = hi.astype(BF16)
        a = _dot(lo, wg[:PACK_W]) + _dot(hi, wg[PACK_W:])
        b = _dot(lo, wu[:PACK_W]) + _dot(hi, wu[PACK_W:])
        act = (a * _sigmoid(a) * b).astype(BF16)
        ybuf[slot] = _pack_halves(_dot(act, wd))
        out_copy(g).start()
        return carry

    lax.fori_loop(0, count_ref[e], tile_step, 0)

    @pl.when(e == last)
    def _():
        for back in range(RING, 0, -1):
            @pl.when(total >= back)
            def _():
                out_copy(total - back).wait()


def _routed_mlp(first_tile, n_tiles, xs, wg, wu, wd):
    R = xs.shape[0]
    any_space = pl.BlockSpec(memory_space=pl.ANY)
    return pl.pallas_call(
        _routed_kernel,
        grid_spec=pltpu.PrefetchScalarGridSpec(
            num_scalar_prefetch=2,
            grid=(N_EXPERTS,),
            in_specs=[any_space,
                      pl.BlockSpec((1, D_MODEL, D_EXPERT), lambda e, ft, nt: (e, 0, 0)),
                      pl.BlockSpec((1, D_MODEL, D_EXPERT), lambda e, ft, nt: (e, 0, 0)),
                      pl.BlockSpec((1, D_EXPERT, D_MODEL), lambda e, ft, nt: (e, 0, 0))],
            out_specs=any_space,
            scratch_shapes=[pltpu.VMEM((RING, MOE_TM, PACK_W), I32), pltpu.VMEM((RING, MOE_TM, PACK_W), I32),
                            pltpu.SemaphoreType.DMA((RING,)), pltpu.SemaphoreType.DMA((RING,))]),
        out_shape=jax.ShapeDtypeStruct((R, PACK_W), I32),
        compiler_params=pltpu.CompilerParams(dimension_semantics=("arbitrary",)),
        name="routed_mlp",
    )(first_tile, n_tiles, xs, wg, wu, wd)


def _combine_kernel(xp_ref, y0_ref, y1_ref, route_ref, lnw_ref, lnb_ref, *out_refs):
    o_ref = out_refs[-1]
    w_rows = lax.bitcast_convert_type(route_ref[...], F32)
    w_cols = jnp.transpose(jnp.concatenate([w_rows] * (LANES // SUBLANES), axis=0))
    w0 = w_cols[:, TOP_K:TOP_K + 1]
    w1 = w_cols[:, TOP_K + 1:TOP_K + 2]
    xlo, xhi = _unpack_halves(xp_ref[...])
    y0lo, y0hi = _unpack_halves(y0_ref[...])
    y1lo, y1hi = _unpack_halves(y1_ref[...])
    h = jnp.concatenate([ALPHA * xlo + (w0 * y0lo + w1 * y1lo), ALPHA * xhi + (w0 * y0hi + w1 * y1hi)], axis=1)
    o_ref[...] = _layer_norm(h, lnw_ref[...], lnb_ref[...])


def _combine_ln2(xp, yg_parts, route, lnw, lnb, tile=1024):
    T = xp.shape[0]
    n_parts = len(yg_parts)
    nt = T // tile // n_parts
    full = lambda a: pl.BlockSpec(a.shape, lambda i: (0,) * a.ndim)
    out = None
    for p, yg in enumerate(yg_parts):
        rows = lambda w, p=p: pl.BlockSpec((tile, w), lambda i: (i + p * nt, 0))
        in_specs = [rows(PACK_W),
                    pl.BlockSpec((tile, PACK_W), lambda i: (i, 0)),
                    pl.BlockSpec((tile, PACK_W), lambda i: (i + nt, 0)),
                    pl.BlockSpec((SUBLANES, tile), lambda i, p=p: (0, i + p * nt)), full(lnw), full(lnb)]
        args = [xp, yg, yg, route, lnw, lnb]
        aliases = {}
        if out is not None:
            in_specs.append(pl.BlockSpec(memory_space=pl.ANY))
            args.append(out)
            aliases = {len(args) - 1: 0}
        out = pl.pallas_call(
            _combine_kernel,
            grid=(nt,),
            in_specs=in_specs,
            out_specs=rows(D_MODEL),
            out_shape=jax.ShapeDtypeStruct((T, D_MODEL), F32),
            input_output_aliases=aliases,
            compiler_params=pltpu.CompilerParams(dimension_semantics=("arbitrary",), vmem_limit_bytes=VMEM_LIMIT),
            name="combine_ln2",
        )(*args)
    return out


def _router_params(w_grp, b_grp, w_exp, b_exp):
    rw = jnp.zeros((D_MODEL, ROUTER_ROWS), F32)
    rw = rw.at[:, 0:N_GROUPS].set(w_grp).at[:, EXP_ROW0:EXP_ROW0 + N_EXPERTS].set(w_exp)
    rb = jnp.zeros((ROUTER_ROWS,), F32).at[N_GROUPS:8].set(NEG_BIG)
    rb = rb.at[0:N_GROUPS].set(b_grp).at[EXP_ROW0:EXP_ROW0 + N_EXPERTS].set(b_exp)
    rw_hi = rw.astype(BF16)
    rw_lo = (rw - rw_hi.astype(F32)).astype(BF16)
    return jnp.concatenate([rw_hi, rw_lo], axis=1), rb[:, None]


def kernel(x, mem, positions, w_in, w_pool_grp, pool_scale, ret_gn_w, w_mem_kv, w_br_pool, w_br_ret, w_br_xa,
           w_out, ln1_w, ln1_b, w_grp_router, b_grp_router, w_exp_router, b_exp_router, w_exp_gate, w_exp_up,
           w_exp_down, ln2_w, ln2_b):
    B, S, D = x.shape
    assert D == D_MODEL and w_in.shape[0] == DEPTH and S % 512 == 0
    T = B * S
    M = mem.shape[1]
    l = 0
    xf = x.reshape(T, D)

    rope = _rope_table(positions.reshape(1, T))
    kv = _mem_kv(mem.reshape(B * M, D), w_mem_kv[l].astype(BF16))
    rw, rb = _router_params(w_grp_router[l], b_grp_router[l], w_exp_router[l], b_exp_router[l])
    xp, route = _mixer(xf, rope, kv, w_in[l].astype(BF16),
                           w_pool_grp[l], pool_scale[l][None, :], ret_gn_w[l].reshape(1, -1),
                           w_br_pool[l].astype(BF16), w_br_ret[l].astype(BF16), w_br_xa[l].astype(BF16),
                           w_out[l].astype(BF16), ln1_w[l][None, :], ln1_b[l][None, :], rw, rb, B, S, M)

    pos, first_tile, n_tiles = _positions(route)
    pos2d = pos[0:TOP_K].reshape(TOP_K * T // SC_CHUNK, SC_CHUNK)
    max_tiles = (TOP_K * T + N_EXPERTS * (MOE_TM - 1)) // MOE_TM
    xs = _sc_scatter_rows(xp, pos2d, max_tiles * MOE_TM)
    ys = _routed_mlp(first_tile[:, 0], n_tiles[:, 0], xs,
                     w_exp_gate[l].reshape(N_EXPERTS, D_MODEL, D_EXPERT),
                     w_exp_up[l].reshape(N_EXPERTS, D_MODEL, D_EXPERT),
                     w_exp_down[l].reshape(N_EXPERTS, D_EXPERT, D_MODEL))
    rng = T // COMBINE_PARTS
    yg_parts = [_sc_gather_rows(ys, pos[0:TOP_K, p * rng:(p + 1) * rng].reshape(TOP_K * rng // SC_CHUNK, SC_CHUNK))
                for p in range(COMBINE_PARTS)]
    out = _combine_ln2(xp, yg_parts, route, ln2_w[l][None, :], ln2_b[l][None, :])
    return out.reshape(B, S, D)
```

```python
import functools

import numpy as np
import jax
import jax.numpy as jnp
from jax import lax
from jax.experimental import pallas as pl
from jax.experimental.pallas import tpu as pltpu
from jax.experimental.pallas import tpu_sc as plsc

F32 = jnp.float32
BF16 = jnp.bfloat16
I32 = jnp.int32
U32 = jnp.uint32

D_MODEL = 1024
POOL_WINDOWS = (2, 4, 8, 16)
POOL_GROUP_DIM = 128
POOL_WIDTH = 512
POOL_HALO = 16
RET_HEADS = 4
RET_QK_DIM = 128
RET_V_DIM = 256
RET_CHUNK = 128
ROPE_BASE = 10000.0
XA_HEADS = 4
XA_HEAD_DIM = 128
XA_WIDTH = 512
N_GROUPS = 4
EXPERTS_PER_GROUP = 8
N_EXPERTS = N_GROUPS * EXPERTS_PER_GROUP
D_EXPERT = 256
LN_EPS = 1e-5
DEPTH = 1
ALPHA = (2.0 * DEPTH) ** 0.25
NEG_BIG = -1e30

COL_POOL, COL_Q, COL_K, COL_V, COL_G, COL_XAQ, COL_GATES = 0, 512, 1024, 1536, 2560, 3584, 4096

V7X_VMEM_BYTES = 64 * 1024 * 1024
VMEM_LIMIT = V7X_VMEM_BYTES * 7 // 8
SUBLANES = 8
LANES = 128

TOP_K = 2
PACK_W = D_MODEL // 2
MOE_TM = 512
SC_CHUNK = 64
RANK_CHUNK = 512
COMBINE_PARTS = 2
RING = 4
STRIP = 256
LN_ROWS = 32


def _dot(a, b):
    return jnp.dot(a, b, preferred_element_type=F32)


def _dot_nt(a, b):
    return lax.dot_general(a, b, (((1,), (1,)), ((), ())), preferred_element_type=F32)


def _sigmoid(z):
    return 1.0 / (1.0 + jnp.exp2(z * (-1.0 / np.log(2.0))))


def _layer_norm(h, w, b):
    mu = jnp.mean(h, axis=-1, keepdims=True)
    hc = h - mu
    var = jnp.mean(hc * hc, axis=-1, keepdims=True)
    return hc * lax.rsqrt(var + LN_EPS) * w + b


ROPE_LO = 64
ROPE_PARTS = 3


def _rope_kernel(pos_ref, freq_ref, tab_ref, cos_t_ref, sin_t_ref):
    pos = pos_ref[...]

    def emit(cos_t, sin_t):
        cos_t_ref[...] = cos_t
        sin_t_ref[...] = sin_t

    in_table = jnp.logical_and(jnp.min(pos) >= 0, jnp.max(pos) < ROPE_LO * LANES)

    @pl.when(in_table)
    def _():
        idx = lax.broadcasted_iota(I32, (LANES, pos.shape[1]), 0)
        pick_hi = jnp.where(idx == jnp.right_shift(pos, ROPE_LO.bit_length() - 1), 1.0, 0.0).astype(BF16)
        pick_lo = jnp.where(idx == (pos & (ROPE_LO - 1)), 1.0, 0.0).astype(BF16)

        def look(k, pick):
            return sum(_dot(tab_ref[k * ROPE_PARTS + p], pick) for p in range(ROPE_PARTS))

        cos_a, sin_a, cos_b, sin_b = look(0, pick_hi), look(1, pick_hi), look(2, pick_lo), look(3, pick_lo)
        emit(cos_a * cos_b - sin_a * sin_b, sin_a * cos_b + cos_a * sin_b)

    @pl.when(jnp.logical_not(in_table))
    def _():
        ang = freq_ref[...] * pos.astype(F32)
        emit(jnp.cos(ang), jnp.sin(ang))


def _rope_tables(inv_freq):
    f = inv_freq.astype(np.float64)[:, None]
    idx = np.arange(LANES, dtype=np.float64)[None, :]
    tabs = [np.cos(ROPE_LO * idx * f), np.sin(ROPE_LO * idx * f), np.cos(idx * f), np.sin(idx * f)]
    pieces = []
    for t in tabs:
        rest = t.astype(np.float32)
        for _ in range(ROPE_PARTS):
            piece = rest.astype(BF16)
            pieces.append(piece)
            rest = rest - piece.astype(np.float32)
    return jnp.asarray(np.stack(pieces))


def _rope_table(pos_row, tile=2048):
    T = pos_row.shape[1]
    half = RET_QK_DIM // 2
    inv_freq = (ROPE_BASE ** (-np.arange(half, dtype=np.float64) / half)).astype(np.float32)
    freq = jnp.asarray(inv_freq[:, None])
    tabs = _rope_tables(inv_freq)
    out_t = pl.BlockSpec((half, tile), lambda i: (0, i))
    return pl.pallas_call(
        _rope_kernel,
        grid=(T // tile,),
        in_specs=[pl.BlockSpec((1, tile), lambda i: (0, i)), pl.BlockSpec((half, 1), lambda i: (0, 0)),
                  pl.BlockSpec(tabs.shape, lambda i: (0, 0, 0))],
        out_specs=[out_t, out_t],
        out_shape=[jax.ShapeDtypeStruct((half, T), F32)] * 2,
        name="rope_table",
    )(pos_row, freq, tabs)


POOL_SUB = 256


def _pool_bands():
    r = np.arange(POOL_SUB)[:, None]
    c = np.arange(POOL_SUB)[None, :]
    ch = np.arange(POOL_HALO)[None, :] - POOL_HALO
    main = np.stack([((r - c >= 0) & (r - c < w)) for w in POOL_WINDOWS]).astype(np.float32)
    halo = np.stack([((r - ch >= 0) & (r - ch < w)) for w in POOL_WINDOWS]).astype(np.float32)
    return jnp.asarray(main, BF16), jnp.asarray(halo, BF16)


def _pool_branch(ub, j, halo_ref, bmain_ref, bhalo_ref, o_ref, filler):
    tile = ub.shape[0]
    s0 = j * tile
    slot = lax.rem(j, 2)
    blocks = [(sb * POOL_SUB, g) for sb in range(tile // POOL_SUB) for g in range(len(POOL_WINDOWS))]
    wsum = {}
    for r0, g in blocks:
        cols = slice(g * POOL_GROUP_DIM, (g + 1) * POOL_GROUP_DIM)
        prev = halo_ref[slot] if r0 == 0 else ub[r0 - POOL_HALO:r0]
        wsum[r0, g] = _dot(bmain_ref[g], ub[r0:r0 + POOL_SUB, cols]) + _dot(bhalo_ref[g], prev[:, cols])
    filler()
    for r0, g in blocks:
        cols = slice(g * POOL_GROUP_DIM, (g + 1) * POOL_GROUP_DIM)
        pos = s0 + r0 + lax.broadcasted_iota(I32, (POOL_SUB, POOL_GROUP_DIM), 0)
        cnt = jnp.minimum(pos + 1, POOL_WINDOWS[g]).astype(F32)
        pooled = wsum[r0, g] / cnt - ub[r0:r0 + POOL_SUB, cols].astype(F32)
        o_ref[r0:r0 + POOL_SUB, cols] = pooled.astype(BF16)
    halo_ref[1 - slot] = ub[tile - POOL_HALO:tile]


def _ret_consts():
    h = np.arange(RET_HEADS, dtype=np.float64)
    log_gamma = np.log1p(-np.exp2(-5.0 - h))
    pos = np.arange(RET_CHUNK, dtype=np.float64)
    diff = pos[:, None] - pos[None, :]
    kscale = RET_QK_DIM ** -0.5
    dmask = kscale * np.where(diff >= 0, np.exp(log_gamma[:, None, None] * np.maximum(diff, 0.0)), 0.0)
    qdec = np.exp(log_gamma[:, None] * (pos + 1.0)[None, :])
    kdec = kscale * np.exp(log_gamma[:, None] * (RET_CHUNK - 1.0 - pos)[None, :])
    cdec = np.exp(log_gamma * RET_CHUNK)
    lanes = lambda a: np.broadcast_to(a[:, :, None], (RET_HEADS, RET_CHUNK, RET_QK_DIM))
    kdec_t = np.broadcast_to(kdec[:, None, :], (RET_HEADS, RET_QK_DIM, RET_CHUNK))
    return (jnp.asarray(dmask, F32), jnp.asarray(lanes(qdec), F32), jnp.asarray(kdec_t, F32),
            tuple(float(v) for v in cdec))


def _retention_branch(q, k_t, v, silu_g, cos_t_ref, sin_t_ref, dmask_ref, qdec_ref, kdec_ref,
                      gnw_ref, state_ref, rq_ref, rqd_ref, rkt_ref, rkdt_ref, o_ref, cdec, fillers):
    tile = q.shape[0]
    n_chunks = tile // RET_CHUNK
    half = RET_QK_DIM // 2
    cos_t = cos_t_ref[...]
    sin_t = sin_t_ref[...]
    cos = jnp.transpose(jnp.concatenate([cos_t, cos_t], axis=0))
    sin = jnp.transpose(jnp.concatenate([-sin_t, sin_t], axis=0))
    for h in range(RET_HEADS):
        qk = slice(h * RET_QK_DIM, (h + 1) * RET_QK_DIM)
        qh = q[:, qk]
        qr = qh * cos + pltpu.roll(qh, half, 1) * sin
        rq_ref[:, qk] = qr.astype(BF16)
        rqd_ref[:, qk] = (qr * jnp.concatenate([qdec_ref[h]] * n_chunks, axis=0)).astype(BF16)
        k1 = k_t[h * RET_QK_DIM:h * RET_QK_DIM + half]
        k2 = k_t[h * RET_QK_DIM + half:(h + 1) * RET_QK_DIM]
        kr_t = jnp.concatenate([k1 * cos_t - k2 * sin_t, k2 * cos_t + k1 * sin_t], axis=0)
        rkt_ref[qk, :] = kr_t.astype(BF16)
        rkdt_ref[qk, :] = (kr_t * jnp.concatenate([kdec_ref[h]] * n_chunks, axis=1)).astype(BF16)

    chunks = [slice(c * RET_CHUNK, (c + 1) * RET_CHUNK) for c in range(n_chunks)]
    heads = [slice(h * RET_QK_DIM, (h + 1) * RET_QK_DIM) for h in range(RET_HEADS)]
    v_heads = [slice(h * RET_V_DIM, (h + 1) * RET_V_DIM) for h in range(RET_HEADS)]
    raw = {(c, h): _dot(rq_ref[chunks[c], heads[h]], rkt_ref[heads[h], chunks[c]])
           for c in range(n_chunks) for h in range(RET_HEADS)}
    incr = {(c, h): _dot(rkdt_ref[heads[h], chunks[c]], v[chunks[c], v_heads[h]])
            for c in range(n_chunks) for h in range(RET_HEADS)}
    state_in = {}
    for h in range(RET_HEADS):
        st = state_ref[h]
        for c in range(n_chunks):
            state_in[c, h] = st.astype(BF16)
            st = cdec[h] * st + incr[c, h]
        state_ref[h] = st

    for c in range(n_chunks):
        rows = chunks[c]
        fillers[c]()
        for h in range(RET_HEADS):
            qk = heads[h]
            v_cols = v_heads[h]
            scores = raw[c, h] * dmask_ref[h]
            lhs = jnp.concatenate([scores.astype(BF16), rqd_ref[rows, qk]], axis=1)
            y = _dot(lhs, jnp.concatenate([v[rows, v_cols], state_in[c, h]], axis=0))
            mu = jnp.mean(y, axis=-1, keepdims=True)
            yc = y - mu
            var = jnp.mean(yc * yc, axis=-1, keepdims=True)
            yn = yc * lax.rsqrt(var + LN_EPS) * gnw_ref[:, v_cols]
            o_ref[rows, v_cols] = (silu_g[h][rows] * yn).astype(BF16)


def _cross_attention_branch(xq, kv_ref, o_ref):
    scale = XA_HEAD_DIM ** -0.5
    for h in range(XA_HEADS):
        cols = slice(h * XA_HEAD_DIM, (h + 1) * XA_HEAD_DIM)
        v_cols = slice(XA_WIDTH + h * XA_HEAD_DIM, XA_WIDTH + (h + 1) * XA_HEAD_DIM)
        s = _dot_nt(xq[:, cols], kv_ref[:, cols]) * scale
        m = jnp.max(s, axis=-1, keepdims=True)
        p = jnp.exp(s - m)
        l = jnp.sum(p, axis=-1, keepdims=True)
        o = _dot(p.astype(BF16), kv_ref[:, v_cols]) / l
        o_ref[:, cols] = o.astype(BF16)


ROUTER_ROWS = 128
EXP_ROW0 = 8


def _route(logits_t):
    gl = logits_t[0:8]
    gmax = jnp.max(gl, axis=0, keepdims=True)
    p_grp = 1.0 / jnp.sum(jnp.exp(gl - gmax), axis=0, keepdims=True)
    idx8 = lax.broadcasted_iota(jnp.int32, gl.shape, 0)
    gsel = jnp.min(jnp.where(gl == gmax, idx8, 8), axis=0, keepdims=True)
    cl = jnp.zeros_like(gl)
    for g in range(N_GROUPS):
        r0 = EXP_ROW0 + g * EXPERTS_PER_GROUP
        cl = cl + jnp.where(gsel == g, logits_t[r0:r0 + EXPERTS_PER_GROUP], 0.0)
    v1 = jnp.max(cl, axis=0, keepdims=True)
    i1 = jnp.min(jnp.where(cl == v1, idx8, 8), axis=0, keepdims=True)
    cl2 = jnp.where(idx8 == i1, -jnp.inf, cl)
    v2 = jnp.max(cl2, axis=0, keepdims=True)
    i2 = jnp.min(jnp.where(cl2 == v2, idx8, 8), axis=0, keepdims=True)
    e21 = jnp.exp(v2 - v1)
    w1 = p_grp / (1.0 + e21)
    w2 = p_grp * e21 / (1.0 + e21)
    return gsel * EXPERTS_PER_GROUP + i1, gsel * EXPERTS_PER_GROUP + i2, w1, w2


def _pack_halves(v):
    half = v.shape[1] // 2
    lo = lax.bitcast_convert_type(v[:, :half].astype(BF16).astype(F32), U32)
    hi = lax.bitcast_convert_type(v[:, half:].astype(BF16).astype(F32), U32)
    return lax.bitcast_convert_type(lax.shift_right_logical(lo, U32(16)) | hi, I32)


def _unpack_halves(w):
    u = lax.bitcast_convert_type(w, U32)
    lo = lax.bitcast_convert_type(lax.shift_left(u, U32(16)), F32)
    hi = lax.bitcast_convert_type(u & U32(0xFFFF0000), F32)
    return lo, hi


def _mixer_kernel(x_ref, cos_t_ref, sin_t_ref, mem_ref, wmem_ref, win_ref, wgrp_ref,
                  pscale_ref, bmain_ref, bhalo_ref, dmask_ref, qdec_ref, kdec_ref, gnw_ref, wp_ref, wr_ref,
                  wa_ref, wo_ref, lnw_ref, lnb_ref, rw_ref, rb_ref, xp_ref, route_ref,
                  state_ref, halo_ref, ypool_ref, yret_ref, yxa_ref, rq_ref, rqd_ref, rkt_ref, rkdt_ref, wkt_ref,
                  wpool_ref, kv_ref, *, tile, cdec):
    j = pl.program_id(1)

    @pl.when(jnp.logical_and(pl.program_id(0) == 0, j == 0))
    def _():
        wk = win_ref[:, COL_K:COL_K + RET_HEADS * RET_QK_DIM].astype(F32)
        wkt_ref[...] = jnp.transpose(wk).astype(BF16)
        for g in range(len(POOL_WINDOWS)):
            rows = slice(g * POOL_GROUP_DIM, (g + 1) * POOL_GROUP_DIM)
            wg = (wgrp_ref[g].astype(F32) * pscale_ref[:, rows]).astype(BF16)
            wpool_ref[rows, :] = _dot(wg, wp_ref[rows, :]).astype(BF16)

    @pl.when(j == 0)
    def _():
        state_ref[...] = jnp.zeros_like(state_ref)
        halo_ref[...] = jnp.zeros_like(halo_ref)
        kv_ref[...] = _dot(mem_ref[...].astype(BF16), wmem_ref[...]).astype(BF16)

    x = x_ref[...]
    xb = x.astype(BF16)

    def proj(col, width):
        return _dot(xb, win_ref[:, col:col + width])

    part = {}
    strips = [slice(c, c + STRIP) for c in range(0, D_MODEL, STRIP)]

    def gate(branch, cols):
        return _sigmoid(proj(COL_GATES + branch * D_MODEL + cols.start, STRIP))

    def pool_part():
        def pool_gates():
            part["pool_gate"] = [gate(0, c) for c in strips]

        _pool_branch(proj(COL_POOL, POOL_WIDTH).astype(BF16), j, halo_ref, bmain_ref, bhalo_ref, ypool_ref,
                     pool_gates)
        part["pool"] = [part["pool_gate"][i] * _dot(ypool_ref[...], wpool_ref[:, c]) for i, c in enumerate(strips)]

    def xa_part():
        _cross_attention_branch(proj(COL_XAQ, XA_WIDTH).astype(BF16), kv_ref, yxa_ref)
        part["xa"] = [gate(2, c) * _dot(yxa_ref[...], wa_ref[:, c]) for c in strips]

    def ret_gate_part():
        part["ret_gate"] = [gate(1, c) for c in strips]

    silu_g = []
    for h in range(RET_HEADS):
        gh = proj(COL_G + h * RET_V_DIM, RET_V_DIM)
        silu_g.append(gh * _sigmoid(gh))
    fillers = [pool_part, xa_part, ret_gate_part] + [lambda: None] * (tile // RET_CHUNK - 3)
    _retention_branch(proj(COL_Q, RET_HEADS * RET_QK_DIM), _dot_nt(wkt_ref[...], xb),
                      proj(COL_V, RET_HEADS * RET_V_DIM).astype(BF16), silu_g, cos_t_ref, sin_t_ref, dmask_ref,
                      qdec_ref, kdec_ref, gnw_ref, state_ref, rq_ref, rqd_ref, rkt_ref, rkdt_ref, yret_ref, cdec,
                      fillers)
    merged = jnp.concatenate(
        [(part["pool"][i] + part["ret_gate"][i] * _dot(yret_ref[...], wr_ref[:, c]) + part["xa"][i]).astype(BF16)
         for i, c in enumerate(strips)], axis=1)
    h = jnp.concatenate([ALPHA * x[:, c] + _dot(merged, wo_ref[:, c]) for c in strips], axis=1)
    x1 = jnp.concatenate([_layer_norm(h[r:r + LN_ROWS], lnw_ref[...], lnb_ref[...])
                          for r in range(0, tile, LN_ROWS)], axis=0)
    xp_ref[...] = _pack_halves(x1)
    x1_hi = x1.astype(BF16)
    x1_lo = (x1 - x1_hi.astype(F32)).astype(BF16)
    p4 = _dot(jnp.concatenate([x1_hi, x1_lo], axis=0), rw_ref[...])
    logits = (p4[:tile, :ROUTER_ROWS] + p4[:tile, ROUTER_ROWS:]) + (p4[tile:, :ROUTER_ROWS] + p4[tile:, ROUTER_ROWS:])
    e0, e1, w0, w1 = _route(jnp.transpose(logits) + rb_ref[...])
    w_bits = [lax.bitcast_convert_type(w, I32) for w in (w0, w1)]
    route_ref[...] = jnp.concatenate([e0, e1, *w_bits, jnp.zeros((SUBLANES - 2 * TOP_K, tile), I32)], axis=0)


def _mixer(xf, rope, memf, wmem, win, wgrp, pscale, gnw, wp, wr, wa, wo, lnw, lnb, rw, rb, batch, seq, mem_len,
           tile=512):
    T = xf.shape[0]
    nj = seq // tile
    cos_t, sin_t = rope
    bmain, bhalo = _pool_bands()
    dmask, qdec, kdec, cdec = _ret_consts()
    resident = lambda a: pl.BlockSpec(a.shape, lambda b, j: (0,) * a.ndim, pipeline_mode=pl.Buffered(1))
    rowblk = lambda w: pl.BlockSpec((tile, w), lambda b, j: (b * nj + j, 0))
    colblk = lambda r: pl.BlockSpec((r, tile), lambda b, j: (0, b * nj + j))
    consts = (wmem, win, wgrp, pscale, bmain, bhalo, dmask, qdec, kdec, gnw, wp, wr, wa, wo, lnw, lnb, rw, rb)
    return pl.pallas_call(
        functools.partial(_mixer_kernel, tile=tile, cdec=cdec),
        grid=(batch, nj),
        in_specs=[rowblk(D_MODEL), colblk(RET_QK_DIM // 2), colblk(RET_QK_DIM // 2),
                  pl.BlockSpec((mem_len, D_MODEL), lambda b, j: (b, 0))] + [resident(a) for a in consts],
        out_specs=[rowblk(PACK_W), colblk(SUBLANES)],
        out_shape=[jax.ShapeDtypeStruct((T, PACK_W), I32), jax.ShapeDtypeStruct((SUBLANES, T), I32)],
        scratch_shapes=[pltpu.VMEM((RET_HEADS, RET_QK_DIM, RET_V_DIM), F32),
                        pltpu.VMEM((2, POOL_HALO, POOL_WIDTH), BF16),
                        pltpu.VMEM((tile, POOL_WIDTH), BF16),
                        pltpu.VMEM((tile, RET_HEADS * RET_V_DIM), BF16),
                        pltpu.VMEM((tile, XA_WIDTH), BF16),
                        pltpu.VMEM((tile, RET_HEADS * RET_QK_DIM), BF16),
                        pltpu.VMEM((tile, RET_HEADS * RET_QK_DIM), BF16),
                        pltpu.VMEM((RET_HEADS * RET_QK_DIM, tile), BF16),
                        pltpu.VMEM((RET_HEADS * RET_QK_DIM, tile), BF16),
                        pltpu.VMEM((RET_HEADS * RET_QK_DIM, D_MODEL), BF16),
                        pltpu.VMEM((POOL_WIDTH, D_MODEL), BF16),
                        pltpu.VMEM((mem_len, 2 * XA_WIDTH), BF16)],
        compiler_params=pltpu.CompilerParams(dimension_semantics=("arbitrary", "arbitrary"),
                                             vmem_limit_bytes=VMEM_LIMIT),
        name="mixer",
    )(xf, cos_t, sin_t, memf, *consts)


META_LANES = LANES


def _positions_kernel(eid_ref, tri_ref, low_ref, pos_ref, first_tile_ref, n_tiles_ref, *, n_tok):
    n_chunks = n_tok // RANK_CHUNK
    erow = lax.broadcasted_iota(I32, (N_EXPERTS, RANK_CHUNK), 0)

    def onehot(c):
        sl = slice(c * RANK_CHUNK, (c + 1) * RANK_CHUNK)
        m0 = eid_ref[0:1, sl] == erow
        m1 = eid_ref[1:2, sl] == erow
        return m0, m1, jnp.where(m0, 1.0, 0.0) + jnp.where(m1, 1.0, 0.0)

    counts = jnp.zeros((N_EXPERTS, 1), F32)
    for c in range(n_chunks):
        counts = counts + jnp.sum(onehot(c)[2], axis=1, keepdims=True)
    ptiles = jnp.floor((counts + (MOE_TM - 1)) * (1.0 / MOE_TM))
    ptiles_b = jnp.broadcast_to(ptiles, (N_EXPERTS, LANES)).astype(BF16)
    start = _dot(low_ref[...], ptiles_b)[:, 0:1] * MOE_TM

    pos_ref[...] = jnp.zeros_like(pos_ref)
    carry = start - 1.0
    for c in range(n_chunks):
        sl = slice(c * RANK_CHUNK, (c + 1) * RANK_CHUNK)
        m0, m1, oh = onehot(c)
        rank = _dot(oh.astype(BF16), tri_ref[...]) + carry
        pos_ref[0:1, sl] = jnp.sum(jnp.where(m0, rank, 0.0), axis=0, keepdims=True).astype(I32)
        pos_ref[1:2, sl] = jnp.sum(jnp.where(m1, rank, 0.0), axis=0, keepdims=True).astype(I32)
        carry = carry + jnp.sum(oh, axis=1, keepdims=True)

    first_tile_ref[...] = jnp.broadcast_to(start * (1.0 / MOE_TM), first_tile_ref.shape).astype(I32)
    n_tiles_ref[...] = jnp.broadcast_to(ptiles, n_tiles_ref.shape).astype(I32)


def _positions(eid):
    T = eid.shape[1]
    r = np.arange(RANK_CHUNK)
    tri = jnp.asarray(r[:, None] <= r[None, :], BF16)
    e = np.arange(N_EXPERTS)
    low = jnp.asarray(e[None, :] < e[:, None], BF16)
    full = lambda a: pl.BlockSpec(a.shape, lambda i: (0,) * a.ndim)
    return pl.pallas_call(
        functools.partial(_positions_kernel, n_tok=T),
        grid=(1,),
        in_specs=[full(eid), full(tri), full(low)],
        out_specs=[pl.BlockSpec((SUBLANES, T), lambda i: (0, 0)),
                   pl.BlockSpec((N_EXPERTS, META_LANES), lambda i: (0, 0)),
                   pl.BlockSpec((N_EXPERTS, META_LANES), lambda i: (0, 0))],
        out_shape=[jax.ShapeDtypeStruct((SUBLANES, T), I32), jax.ShapeDtypeStruct((N_EXPERTS, META_LANES), I32),
                   jax.ShapeDtypeStruct((N_EXPERTS, META_LANES), I32)],
        name="route_positions",
    )(eid, tri, low)


def _sc_workers():
    info = plsc.get_sparse_core_info()
    return info.num_cores, info.num_cores * info.num_subcores


def _sc_scatter_rows(xp, pos2d, n_out):
    T, W = xp.shape
    n_cores, n_workers = _sc_workers()
    cpw = T // SC_CHUNK // n_workers
    mesh = plsc.VectorSubcoreMesh(core_axis_name="c", subcore_axis_name="s")

    @functools.partial(
        pl.kernel, mesh=mesh, out_type=jax.ShapeDtypeStruct((n_out, W), I32),
        scratch_types=[pltpu.VMEM((TOP_K * cpw, SC_CHUNK), I32), pltpu.VMEM((2, SC_CHUNK, W), I32),
                       pltpu.SemaphoreType.DMA((2,)), pltpu.SemaphoreType.DMA((2,))],
        name="sc_scatter_rows")
    def k(x_hbm, pos_hbm, out_hbm, idx_v, rows_v, rd_sem, wr_sem):
        wid = lax.axis_index("s") * n_cores + lax.axis_index("c")
        for s in range(TOP_K):
            pltpu.sync_copy(pos_hbm.at[pl.ds(s * (T // SC_CHUNK) + wid * cpw, cpw)],
                            idx_v.at[pl.ds(s * cpw, cpw)])

        def read(j):
            return pltpu.make_async_copy(x_hbm.at[pl.ds((wid * cpw + j) * SC_CHUNK, SC_CHUNK)],
                                         rows_v.at[j % 2], rd_sem.at[j % 2])

        def write(j, s):
            return pltpu.make_async_copy(rows_v.at[j % 2], out_hbm.at[idx_v.at[s * cpw + j]], wr_sem.at[j % 2])

        _sc_two_buffer_stream(cpw, read, lambda j: [write(j, s) for s in range(TOP_K)])

    return k(xp, pos2d)


def _sc_two_buffer_stream(n, read, writes):
    read(0).start()
    for j in range(n):
        read(j).wait()
        if j + 1 < n:
            if j >= 1:
                for w in writes(j - 1):
                    w.wait()
            read(j + 1).start()
        for w in writes(j):
            w.start()
    for j in range(max(n - 2, 0), n):
        for w in writes(j):
            w.wait()


def _sc_gather_rows(y, idx2d):
    W = y.shape[1]
    n = idx2d.shape[0] * SC_CHUNK
    n_cores, n_workers = _sc_workers()
    cpw = n // SC_CHUNK // n_workers
    mesh = plsc.VectorSubcoreMesh(core_axis_name="c", subcore_axis_name="s")

    @functools.partial(
        pl.kernel, mesh=mesh, out_type=jax.ShapeDtypeStruct((n, W), I32),
        scratch_types=[pltpu.VMEM((cpw, SC_CHUNK), I32), pltpu.VMEM((2, SC_CHUNK, W), I32),
                       pltpu.SemaphoreType.DMA((2,)), pltpu.SemaphoreType.DMA((2,))],
        name="sc_gather_rows")
    def k(y_hbm, idx_hbm, out_hbm, idx_v, rows_v, rd_sem, wr_sem):
        wid = lax.axis_index("s") * n_cores + lax.axis_index("c")
        pltpu.sync_copy(idx_hbm.at[pl.ds(wid * cpw, cpw)], idx_v)

        def read(j):
            return pltpu.make_async_copy(y_hbm.at[idx_v.at[j]], rows_v.at[j % 2], rd_sem.at[j % 2])

        def write(j):
            return pltpu.make_async_copy(rows_v.at[j % 2], out_hbm.at[pl.ds((wid * cpw + j) * SC_CHUNK, SC_CHUNK)],
                                         wr_sem.at[j % 2])

        _sc_two_buffer_stream(cpw, read, lambda j: [write(j)])

    return k(y, idx2d)


def _routed_kernel(first_ref, count_ref, xs_hbm, wg_ref, wu_ref, wd_ref, ys_hbm, xbuf, ybuf, in_sem, out_sem):
    e = pl.program_id(0)
    last = pl.num_programs(0) - 1
    total = first_ref[last] + count_ref[last]

    def in_copy(g):
        slot = lax.rem(g, RING)
        return pltpu.make_async_copy(xs_hbm.at[pl.ds(g * MOE_TM, MOE_TM)], xbuf.at[slot], in_sem.at[slot])

    def out_copy(g):
        slot = lax.rem(g, RING)
        return pltpu.make_async_copy(ybuf.at[slot], ys_hbm.at[pl.ds(g * MOE_TM, MOE_TM)], out_sem.at[slot])

    @pl.when(e == 0)
    def _():
        for g0 in range(RING - 1):
            @pl.when(g0 < total)
            def _():
                in_copy(g0).start()

    wg = wg_ref[0].astype(BF16)
    wu = wu_ref[0].astype(BF16)
    wd = wd_ref[0].astype(BF16)

    def tile_step(i, carry):
        g = first_ref[e] + i
        slot = lax.rem(g, RING)
        in_copy(g).wait()

        @pl.when(g + RING - 1 < total)
        def _():
            in_copy(g + RING - 1).start()

        @pl.when(g >= RING)
        def _():
            out_copy(g - RING).wait()

        lo, hi = _unpack_halves(xbuf[slot])
        lo = lo.astype(BF16)
        hi = hi.astype(BF16)
        a = _dot(lo, wg[:PACK_W]) + _dot(hi, wg[PACK_W:])
        b = _dot(lo, wu[:PACK_W]) + _dot(hi, wu[PACK_W:])
        act = (a * _sigmoid(a) * b).astype(BF16)
        ybuf[slot] = _pack_halves(_dot(act, wd))
        out_copy(g).start()
        return carry

    lax.fori_loop(0, count_ref[e], tile_step, 0)

    @pl.when(e == last)
    def _():
        for back in range(RING, 0, -1):
            @pl.when(total >= back)
            def _():
                out_copy(total - back).wait()


def _routed_mlp(first_tile, n_tiles, xs, wg, wu, wd):
    R = xs.shape[0]
    any_space = pl.BlockSpec(memory_space=pl.ANY)
    return pl.pallas_call(
        _routed_kernel,
        grid_spec=pltpu.PrefetchScalarGridSpec(
            num_scalar_prefetch=2,
            grid=(N_EXPERTS,),
            in_specs=[any_space,
                      pl.BlockSpec((1, D_MODEL, D_EXPERT), lambda e, ft, nt: (e, 0, 0)),
                      pl.BlockSpec((1, D_MODEL, D_EXPERT), lambda e, ft, nt: (e, 0, 0)),
                      pl.BlockSpec((1, D_EXPERT, D_MODEL), lambda e, ft, nt: (e, 0, 0))],
            out_specs=any_space,
            scratch_shapes=[pltpu.VMEM((RING, MOE_TM, PACK_W), I32), pltpu.VMEM((RING, MOE_TM, PACK_W), I32),
                            pltpu.SemaphoreType.DMA((RING,)), pltpu.SemaphoreType.DMA((RING,))]),
        out_shape=jax.ShapeDtypeStruct((R, PACK_W), I32),
        compiler_params=pltpu.CompilerParams(dimension_semantics=("arbitrary",)),
        name="routed_mlp",
    )(first_tile, n_tiles, xs, wg, wu, wd)


def _combine_kernel(xp_ref, y0_ref, y1_ref, route_ref, lnw_ref, lnb_ref, *out_refs):
    o_ref = out_refs[-1]
    w_rows = lax.bitcast_convert_type(route_ref[...], F32)
    w_cols = jnp.transpose(jnp.concatenate([w_rows] * (LANES // SUBLANES), axis=0))
    w0 = w_cols[:, TOP_K:TOP_K + 1]
    w1 = w_cols[:, TOP_K + 1:TOP_K + 2]
    xlo, xhi = _unpack_halves(xp_ref[...])
    y0lo, y0hi = _unpack_halves(y0_ref[...])
    y1lo, y1hi = _unpack_halves(y1_ref[...])
    h = jnp.concatenate([ALPHA * xlo + (w0 * y0lo + w1 * y1lo), ALPHA * xhi + (w0 * y0hi + w1 * y1hi)], axis=1)
    o_ref[...] = _layer_norm(h, lnw_ref[...], lnb_ref[...])


def _combine_ln2(xp, yg_parts, route, lnw, lnb, tile=1024):
    T = xp.shape[0]
    n_parts = len(yg_parts)
    nt = T // tile // n_parts
    full = lambda a: pl.BlockSpec(a.shape, lambda i: (0,) * a.ndim)
    out = None
    for p, yg in enumerate(yg_parts):
        rows = lambda w, p=p: pl.BlockSpec((tile, w), lambda i: (i + p * nt, 0))
        in_specs = [rows(PACK_W),
                    pl.BlockSpec((tile, PACK_W), lambda i: (i, 0)),
                    pl.BlockSpec((tile, PACK_W), lambda i: (i + nt, 0)),
                    pl.BlockSpec((SUBLANES, tile), lambda i, p=p: (0, i + p * nt)), full(lnw), full(lnb)]
        args = [xp, yg, yg, route, lnw, lnb]
        aliases = {}
        if out is not None:
            in_specs.append(pl.BlockSpec(memory_space=pl.ANY))
            args.append(out)
            aliases = {len(args) - 1: 0}
        out = pl.pallas_call(
            _combine_kernel,
            grid=(nt,),
            in_specs=in_specs,
            out_specs=rows(D_MODEL),
            out_shape=jax.ShapeDtypeStruct((T, D_MODEL), F32),
            input_output_aliases=aliases,
            compiler_params=pltpu.CompilerParams(dimension_semantics=("arbitrary",), vmem_limit_bytes=VMEM_LIMIT),
            name="combine_ln2",
        )(*args)
    return out


def _router_params(w_grp, b_grp, w_exp, b_exp):
    rw = jnp.zeros((D_MODEL, ROUTER_ROWS), F32)
    rw = rw.at[:, 0:N_GROUPS].set(w_grp).at[:, EXP_ROW0:EXP_ROW0 + N_EXPERTS].set(w_exp)
    rb = jnp.zeros((ROUTER_ROWS,), F32).at[N_GROUPS:8].set(NEG_BIG)
    rb = rb.at[0:N_GROUPS].set(b_grp).at[EXP_ROW0:EXP_ROW0 + N_EXPERTS].set(b_exp)
    rw_hi = rw.astype(BF16)
    rw_lo = (rw - rw_hi.astype(F32)).astype(BF16)
    return jnp.concatenate([rw_hi, rw_lo], axis=1), rb[:, None]


def kernel(x, mem, positions, w_in, w_pool_grp, pool_scale, ret_gn_w, w_mem_kv, w_br_pool, w_br_ret, w_br_xa,
           w_out, ln1_w, ln1_b, w_grp_router, b_grp_router, w_exp_router, b_exp_router, w_exp_gate, w_exp_up,
           w_exp_down, ln2_w, ln2_b):
    B, S, D = x.shape
    assert D == D_MODEL and w_in.shape[0] == DEPTH and S % 512 == 0
    T = B * S
    M = mem.shape[1]
    l = 0
    xf = x.reshape(T, D)

    rope = _rope_table(positions.reshape(1, T))
    rw, rb = _router_params(w_grp_router[l], b_grp_router[l], w_exp_router[l], b_exp_router[l])
    xp, route = _mixer(xf, rope, mem.reshape(B * M, D), w_mem_kv[l].astype(BF16), w_in[l].astype(BF16),
                       w_pool_grp[l], pool_scale[l][None, :], ret_gn_w[l].reshape(1, -1),
                       w_br_pool[l].astype(BF16), w_br_ret[l].astype(BF16), w_br_xa[l].astype(BF16),
                       w_out[l].astype(BF16), ln1_w[l][None, :], ln1_b[l][None, :], rw, rb, B, S, M)

    pos, first_tile, n_tiles = _positions(route)
    pos2d = pos[0:TOP_K].reshape(TOP_K * T // SC_CHUNK, SC_CHUNK)
    max_tiles = (TOP_K * T + N_EXPERTS * (MOE_TM - 1)) // MOE_TM
    xs = _sc_scatter_rows(xp, pos2d, max_tiles * MOE_TM)
    ys = _routed_mlp(first_tile[:, 0], n_tiles[:, 0], xs,
                     w_exp_gate[l].reshape(N_EXPERTS, D_MODEL, D_EXPERT),
                     w_exp_up[l].reshape(N_EXPERTS, D_MODEL, D_EXPERT),
                     w_exp_down[l].reshape(N_EXPERTS, D_EXPERT, D_MODEL))
    rng = T // COMBINE_PARTS
    yg_parts = [_sc_gather_rows(ys, pos[0:TOP_K, p * rng:(p + 1) * rng].reshape(TOP_K * rng // SC_CHUNK, SC_CHUNK))
                for p in range(COMBINE_PARTS)]
    out = _combine_ln2(xp, yg_parts, route, ln2_w[l][None, :], ln2_b[l][None, :])
    return out.reshape(B, S, D)
```

```python
import functools

import numpy as np
import jax
import jax.numpy as jnp
from jax import lax
from jax.experimental import pallas as pl
from jax.experimental.pallas import tpu as pltpu
from jax.experimental.pallas import tpu_sc as plsc

F32 = jnp.float32
BF16 = jnp.bfloat16
I32 = jnp.int32
U32 = jnp.uint32

D_MODEL = 1024
POOL_WINDOWS = (2, 4, 8, 16)
POOL_GROUP_DIM = 128
POOL_WIDTH = 512
POOL_HALO = 16
RET_HEADS = 4
RET_QK_DIM = 128
RET_V_DIM = 256
RET_CHUNK = 128
ROPE_BASE = 10000.0
XA_HEADS = 4
XA_HEAD_DIM = 128
XA_WIDTH = 512
N_GROUPS = 4
EXPERTS_PER_GROUP = 8
N_EXPERTS = N_GROUPS * EXPERTS_PER_GROUP
D_EXPERT = 256
LN_EPS = 1e-5
DEPTH = 1
ALPHA = (2.0 * DEPTH) ** 0.25
NEG_BIG = -1e30

COL_POOL, COL_Q, COL_K, COL_V, COL_G, COL_XAQ, COL_GATES = 0, 512, 1024, 1536, 2560, 3584, 4096

V7X_VMEM_BYTES = 64 * 1024 * 1024
VMEM_LIMIT = V7X_VMEM_BYTES * 7 // 8
SUBLANES = 8
LANES = 128

TOP_K = 2
PACK_W = D_MODEL // 2
MOE_TM = 512
SC_CHUNK = 64
RANK_CHUNK = 512
COMBINE_PARTS = 2
RING = 4
STRIP = 256
LN_ROWS = 32


def _dot(a, b):
    return jnp.dot(a, b, preferred_element_type=F32)


def _dot_nt(a, b):
    return lax.dot_general(a, b, (((1,), (1,)), ((), ())), preferred_element_type=F32)


def _sigmoid(z):
    return 1.0 / (1.0 + jnp.exp2(z * (-1.0 / np.log(2.0))))


def _layer_norm(h, w, b):
    mu = jnp.mean(h, axis=-1, keepdims=True)
    hc = h - mu
    var = jnp.mean(hc * hc, axis=-1, keepdims=True)
    return hc * lax.rsqrt(var + LN_EPS) * w + b


ROPE_LO = 64
ROPE_PARTS = 3


def _rope_kernel(pos_ref, freq_ref, tab_ref, cos_t_ref, sin_t_ref):
    pos = pos_ref[...]

    def emit(cos_t, sin_t):
        cos_t_ref[...] = cos_t
        sin_t_ref[...] = sin_t

    in_table = jnp.logical_and(jnp.min(pos) >= 0, jnp.max(pos) < ROPE_LO * LANES)

    @pl.when(in_table)
    def _():
        idx = lax.broadcasted_iota(I32, (LANES, pos.shape[1]), 0)
        pick_hi = jnp.where(idx == jnp.right_shift(pos, ROPE_LO.bit_length() - 1), 1.0, 0.0).astype(BF16)
        pick_lo = jnp.where(idx == (pos & (ROPE_LO - 1)), 1.0, 0.0).astype(BF16)

        def look(k, pick):
            return sum(_dot(tab_ref[k * ROPE_PARTS + p], pick) for p in range(ROPE_PARTS))

        cos_a, sin_a, cos_b, sin_b = look(0, pick_hi), look(1, pick_hi), look(2, pick_lo), look(3, pick_lo)
        emit(cos_a * cos_b - sin_a * sin_b, sin_a * cos_b + cos_a * sin_b)

    @pl.when(jnp.logical_not(in_table))
    def _():
        ang = freq_ref[...] * pos.astype(F32)
        emit(jnp.cos(ang), jnp.sin(ang))


def _rope_tables(inv_freq):
    f = inv_freq.astype(np.float64)[:, None]
    idx = np.arange(LANES, dtype=np.float64)[None, :]
    tabs = [np.cos(ROPE_LO * idx * f), np.sin(ROPE_LO * idx * f), np.cos(idx * f), np.sin(idx * f)]
    pieces = []
    for t in tabs:
        rest = t.astype(np.float32)
        for _ in range(ROPE_PARTS):
            piece = rest.astype(BF16)
            pieces.append(piece)
            rest = rest - piece.astype(np.float32)
    return jnp.asarray(np.stack(pieces))


def _rope_table(pos_row, tile=2048):
    T = pos_row.shape[1]
    half = RET_QK_DIM // 2
    inv_freq = (ROPE_BASE ** (-np.arange(half, dtype=np.float64) / half)).astype(np.float32)
    freq = jnp.asarray(inv_freq[:, None])
    tabs = _rope_tables(inv_freq)
    out_t = pl.BlockSpec((half, tile), lambda i: (0, i))
    return pl.pallas_call(
        _rope_kernel,
        grid=(T // tile,),
        in_specs=[pl.BlockSpec((1, tile), lambda i: (0, i)), pl.BlockSpec((half, 1), lambda i: (0, 0)),
                  pl.BlockSpec(tabs.shape, lambda i: (0, 0, 0))],
        out_specs=[out_t, out_t],
        out_shape=[jax.ShapeDtypeStruct((half, T), F32)] * 2,
        name="rope_table",
    )(pos_row, freq, tabs)


POOL_SUB = 256


def _pool_bands():
    r = np.arange(POOL_SUB)[:, None]
    c = np.arange(POOL_SUB)[None, :]
    ch = np.arange(POOL_HALO)[None, :] - POOL_HALO
    main = np.stack([((r - c >= 0) & (r - c < w)) for w in POOL_WINDOWS]).astype(np.float32)
    halo = np.stack([((r - ch >= 0) & (r - ch < w)) for w in POOL_WINDOWS]).astype(np.float32)
    return jnp.asarray(main, BF16), jnp.asarray(halo, BF16)


def _pool_branch(ub, j, halo_ref, bmain_ref, bhalo_ref, o_ref, filler):
    tile = ub.shape[0]
    s0 = j * tile
    slot = lax.rem(j, 2)
    blocks = [(sb * POOL_SUB, g) for sb in range(tile // POOL_SUB) for g in range(len(POOL_WINDOWS))]
    wsum = {}
    for r0, g in blocks:
        cols = slice(g * POOL_GROUP_DIM, (g + 1) * POOL_GROUP_DIM)
        prev = halo_ref[slot] if r0 == 0 else ub[r0 - POOL_HALO:r0]
        wsum[r0, g] = _dot(bmain_ref[g], ub[r0:r0 + POOL_SUB, cols]) + _dot(bhalo_ref[g], prev[:, cols])
    filler()
    for r0, g in blocks:
        cols = slice(g * POOL_GROUP_DIM, (g + 1) * POOL_GROUP_DIM)
        pos = s0 + r0 + lax.broadcasted_iota(I32, (POOL_SUB, POOL_GROUP_DIM), 0)
        cnt = jnp.minimum(pos + 1, POOL_WINDOWS[g]).astype(F32)
        pooled = wsum[r0, g] / cnt - ub[r0:r0 + POOL_SUB, cols].astype(F32)
        o_ref[r0:r0 + POOL_SUB, cols] = pooled.astype(BF16)
    halo_ref[1 - slot] = ub[tile - POOL_HALO:tile]


def _ret_consts():
    h = np.arange(RET_HEADS, dtype=np.float64)
    log_gamma = np.log1p(-np.exp2(-5.0 - h))
    pos = np.arange(RET_CHUNK, dtype=np.float64)
    diff = pos[:, None] - pos[None, :]
    kscale = RET_QK_DIM ** -0.5
    dmask = kscale * np.where(diff >= 0, np.exp(log_gamma[:, None, None] * np.maximum(diff, 0.0)), 0.0)
    qdec = np.exp(log_gamma[:, None] * (pos + 1.0)[None, :])
    kdec = kscale * np.exp(log_gamma[:, None] * (RET_CHUNK - 1.0 - pos)[None, :])
    cdec = np.exp(log_gamma * RET_CHUNK)
    lanes = lambda a: np.broadcast_to(a[:, :, None], (RET_HEADS, RET_CHUNK, RET_QK_DIM))
    kdec_t = np.broadcast_to(kdec[:, None, :], (RET_HEADS, RET_QK_DIM, RET_CHUNK))
    return (jnp.asarray(dmask, F32), jnp.asarray(lanes(qdec), F32), jnp.asarray(kdec_t, F32),
            tuple(float(v) for v in cdec))


def _retention_branch(q, k_t, v, silu_g, cos_t_ref, sin_t_ref, dmask_ref, qdec_ref, kdec_ref,
                      gnw_ref, state_ref, rq_ref, rqd_ref, rkt_ref, rkdt_ref, o_ref, cdec, fillers):
    tile = q.shape[0]
    n_chunks = tile // RET_CHUNK
    half = RET_QK_DIM // 2
    cos_t = cos_t_ref[...]
    sin_t = sin_t_ref[...]
    cos = jnp.transpose(jnp.concatenate([cos_t, cos_t], axis=0))
    sin = jnp.transpose(jnp.concatenate([-sin_t, sin_t], axis=0))
    for h in range(RET_HEADS):
        qk = slice(h * RET_QK_DIM, (h + 1) * RET_QK_DIM)
        qh = q[:, qk]
        qr = qh * cos + pltpu.roll(qh, half, 1) * sin
        rq_ref[:, qk] = qr.astype(BF16)
        rqd_ref[:, qk] = (qr * jnp.concatenate([qdec_ref[h]] * n_chunks, axis=0)).astype(BF16)
        k1 = k_t[h * RET_QK_DIM:h * RET_QK_DIM + half]
        k2 = k_t[h * RET_QK_DIM + half:(h + 1) * RET_QK_DIM]
        kr_t = jnp.concatenate([k1 * cos_t - k2 * sin_t, k2 * cos_t + k1 * sin_t], axis=0)
        rkt_ref[qk, :] = kr_t.astype(BF16)
        rkdt_ref[qk, :] = (kr_t * jnp.concatenate([kdec_ref[h]] * n_chunks, axis=1)).astype(BF16)

    chunks = [slice(c * RET_CHUNK, (c + 1) * RET_CHUNK) for c in range(n_chunks)]
    heads = [slice(h * RET_QK_DIM, (h + 1) * RET_QK_DIM) for h in range(RET_HEADS)]
    v_heads = [slice(h * RET_V_DIM, (h + 1) * RET_V_DIM) for h in range(RET_HEADS)]
    raw = {(c, h): _dot(rq_ref[chunks[c], heads[h]], rkt_ref[heads[h], chunks[c]])
           for c in range(n_chunks) for h in range(RET_HEADS)}
    incr = {(c, h): _dot(rkdt_ref[heads[h], chunks[c]], v[chunks[c], v_heads[h]])
            for c in range(n_chunks) for h in range(RET_HEADS)}
    state_in = {}
    for h in range(RET_HEADS):
        st = state_ref[h]
        for c in range(n_chunks):
            state_in[c, h] = st.astype(BF16)
            st = cdec[h] * st + incr[c, h]
        state_ref[h] = st

    for c in range(n_chunks):
        rows = chunks[c]
        fillers[c]()
        for h in range(RET_HEADS):
            qk = heads[h]
            v_cols = v_heads[h]
            scores = raw[c, h] * dmask_ref[h]
            lhs = jnp.concatenate([scores.astype(BF16), rqd_ref[rows, qk]], axis=1)
            y = _dot(lhs, jnp.concatenate([v[rows, v_cols], state_in[c, h]], axis=0))
            mu = jnp.mean(y, axis=-1, keepdims=True)
            yc = y - mu
            var = jnp.mean(yc * yc, axis=-1, keepdims=True)
            yn = yc * lax.rsqrt(var + LN_EPS) * gnw_ref[:, v_cols]
            o_ref[rows, v_cols] = (silu_g[h][rows] * yn).astype(BF16)


def _cross_attention_branch(xq, kv_ref, o_ref):
    scale = XA_HEAD_DIM ** -0.5
    for h in range(XA_HEADS):
        cols = slice(h * XA_HEAD_DIM, (h + 1) * XA_HEAD_DIM)
        v_cols = slice(XA_WIDTH + h * XA_HEAD_DIM, XA_WIDTH + (h + 1) * XA_HEAD_DIM)
        s = _dot_nt(xq[:, cols], kv_ref[:, cols]) * scale
        m = jnp.max(s, axis=-1, keepdims=True)
        p = jnp.exp(s - m)
        l = jnp.sum(p, axis=-1, keepdims=True)
        o = _dot(p.astype(BF16), kv_ref[:, v_cols]) / l
        o_ref[:, cols] = o.astype(BF16)


ROUTER_ROWS = 128
EXP_ROW0 = 8


def _route(logits_t):
    gl = logits_t[0:8]
    gmax = jnp.max(gl, axis=0, keepdims=True)
    p_grp = 1.0 / jnp.sum(jnp.exp(gl - gmax), axis=0, keepdims=True)
    idx8 = lax.broadcasted_iota(jnp.int32, gl.shape, 0)
    gsel = jnp.min(jnp.where(gl == gmax, idx8, 8), axis=0, keepdims=True)
    cl = jnp.zeros_like(gl)
    for g in range(N_GROUPS):
        r0 = EXP_ROW0 + g * EXPERTS_PER_GROUP
        cl = cl + jnp.where(gsel == g, logits_t[r0:r0 + EXPERTS_PER_GROUP], 0.0)
    v1 = jnp.max(cl, axis=0, keepdims=True)
    i1 = jnp.min(jnp.where(cl == v1, idx8, 8), axis=0, keepdims=True)
    cl2 = jnp.where(idx8 == i1, -jnp.inf, cl)
    v2 = jnp.max(cl2, axis=0, keepdims=True)
    i2 = jnp.min(jnp.where(cl2 == v2, idx8, 8), axis=0, keepdims=True)
    e21 = jnp.exp(v2 - v1)
    w1 = p_grp / (1.0 + e21)
    w2 = p_grp * e21 / (1.0 + e21)
    return gsel * EXPERTS_PER_GROUP + i1, gsel * EXPERTS_PER_GROUP + i2, w1, w2


def _pack_halves(v):
    half = v.shape[1] // 2
    lo = lax.bitcast_convert_type(v[:, :half].astype(BF16).astype(F32), U32)
    hi = lax.bitcast_convert_type(v[:, half:].astype(BF16).astype(F32), U32)
    return lax.bitcast_convert_type(lax.shift_right_logical(lo, U32(16)) | hi, I32)


def _unpack_halves(w):
    u = lax.bitcast_convert_type(w, U32)
    lo = lax.bitcast_convert_type(lax.shift_left(u, U32(16)), F32)
    hi = lax.bitcast_convert_type(u & U32(0xFFFF0000), F32)
    return lo, hi


def _mixer_kernel(x_ref, cos_t_ref, sin_t_ref, mem_ref, wmem_ref, win_ref, wgrp_ref,
                  pscale_ref, bmain_ref, bhalo_ref, dmask_ref, qdec_ref, kdec_ref, gnw_ref, wp_ref, wr_ref,
                  wa_ref, wo_ref, lnw_ref, lnb_ref, rw_ref, rb_ref, xp_ref, route_ref,
                  state_ref, halo_ref, ypool_ref, yret_ref, yxa_ref, rq_ref, rqd_ref, rkt_ref, rkdt_ref, wkt_ref,
                  wpool_ref, kv_ref, *, tile, cdec):
    j = pl.program_id(1)

    @pl.when(jnp.logical_and(pl.program_id(0) == 0, j == 0))
    def _():
        wk = win_ref[:, COL_K:COL_K + RET_HEADS * RET_QK_DIM].astype(F32)
        wkt_ref[...] = jnp.transpose(wk).astype(BF16)
        for g in range(len(POOL_WINDOWS)):
            rows = slice(g * POOL_GROUP_DIM, (g + 1) * POOL_GROUP_DIM)
            wg = (wgrp_ref[g].astype(F32) * pscale_ref[:, rows]).astype(BF16)
            wpool_ref[rows, :] = _dot(wg, wp_ref[rows, :]).astype(BF16)

    @pl.when(j == 0)
    def _():
        state_ref[...] = jnp.zeros_like(state_ref)
        halo_ref[...] = jnp.zeros_like(halo_ref)
        kv_ref[...] = _dot(mem_ref[...].astype(BF16), wmem_ref[...]).astype(BF16)

    x = x_ref[...]
    xb = x.astype(BF16)

    def proj(col, width):
        return _dot(xb, win_ref[:, col:col + width])

    part = {}
    strips = [slice(c, c + STRIP) for c in range(0, D_MODEL, STRIP)]

    def gate(branch, cols):
        return _sigmoid(proj(COL_GATES + branch * D_MODEL + cols.start, STRIP))

    def pool_part():
        def pool_gates():
            part["pool_gate"] = [gate(0, c) for c in strips]

        _pool_branch(proj(COL_POOL, POOL_WIDTH).astype(BF16), j, halo_ref, bmain_ref, bhalo_ref, ypool_ref,
                     pool_gates)
        part["pool"] = [part["pool_gate"][i] * _dot(ypool_ref[...], wpool_ref[:, c]) for i, c in enumerate(strips)]

    def xa_part():
        _cross_attention_branch(proj(COL_XAQ, XA_WIDTH).astype(BF16), kv_ref, yxa_ref)
        part["xa"] = [gate(2, c) * _dot(yxa_ref[...], wa_ref[:, c]) for c in strips]

    def ret_gate_part():
        part["ret_gate"] = [gate(1, c) for c in strips]

    silu_g = []
    for h in range(RET_HEADS):
        gh = proj(COL_G + h * RET_V_DIM, RET_V_DIM)
        silu_g.append(gh * _sigmoid(gh))
    fillers = [pool_part, xa_part, ret_gate_part] + [lambda: None] * (tile // RET_CHUNK - 3)
    _retention_branch(proj(COL_Q, RET_HEADS * RET_QK_DIM), _dot_nt(wkt_ref[...], xb),
                      proj(COL_V, RET_HEADS * RET_V_DIM).astype(BF16), silu_g, cos_t_ref, sin_t_ref, dmask_ref,
                      qdec_ref, kdec_ref, gnw_ref, state_ref, rq_ref, rqd_ref, rkt_ref, rkdt_ref, yret_ref, cdec,
                      fillers)
    merged = jnp.concatenate(
        [(part["pool"][i] + part["ret_gate"][i] * _dot(yret_ref[...], wr_ref[:, c]) + part["xa"][i]).astype(BF16)
         for i, c in enumerate(strips)], axis=1)
    h = jnp.concatenate([ALPHA * x[:, c] + _dot(merged, wo_ref[:, c]) for c in strips], axis=1)
    x1 = jnp.concatenate([_layer_norm(h[r:r + LN_ROWS], lnw_ref[...], lnb_ref[...])
                          for r in range(0, tile, LN_ROWS)], axis=0)
    xp_ref[...] = _pack_halves(x1)
    x1_hi = x1.astype(BF16)
    x1_lo = (x1 - x1_hi.astype(F32)).astype(BF16)
    p4 = _dot(jnp.concatenate([x1_hi, x1_lo], axis=0), rw_ref[...])
    logits = (p4[:tile, :ROUTER_ROWS] + p4[:tile, ROUTER_ROWS:]) + (p4[tile:, :ROUTER_ROWS] + p4[tile:, ROUTER_ROWS:])
    e0, e1, w0, w1 = _route(jnp.transpose(logits) + rb_ref[...])
    w_bits = [lax.bitcast_convert_type(w, I32) for w in (w0, w1)]
    route_ref[...] = jnp.concatenate([e0, e1, *w_bits, jnp.zeros((SUBLANES - 2 * TOP_K, tile), I32)], axis=0)


def _mixer(xf, rope, memf, wmem, win, wgrp, pscale, gnw, wp, wr, wa, wo, lnw, lnb, rw, rb, batch, seq, mem_len,
           tile=512):
    T = xf.shape[0]
    nj = seq // tile
    cos_t, sin_t = rope
    bmain, bhalo = _pool_bands()
    dmask, qdec, kdec, cdec = _ret_consts()
    resident = lambda a: pl.BlockSpec(a.shape, lambda b, j: (0,) * a.ndim, pipeline_mode=pl.Buffered(1))
    rowblk = lambda w: pl.BlockSpec((tile, w), lambda b, j: (b * nj + j, 0))
    colblk = lambda r: pl.BlockSpec((r, tile), lambda b, j: (0, b * nj + j))
    consts = (wmem, win, wgrp, pscale, bmain, bhalo, dmask, qdec, kdec, gnw, wp, wr, wa, wo, lnw, lnb, rw, rb)
    return pl.pallas_call(
        functools.partial(_mixer_kernel, tile=tile, cdec=cdec),
        grid=(batch, nj),
        in_specs=[rowblk(D_MODEL), colblk(RET_QK_DIM // 2), colblk(RET_QK_DIM // 2),
                  pl.BlockSpec((mem_len, D_MODEL), lambda b, j: (b, 0))] + [resident(a) for a in consts],
        out_specs=[rowblk(PACK_W), colblk(SUBLANES)],
        out_shape=[jax.ShapeDtypeStruct((T, PACK_W), I32), jax.ShapeDtypeStruct((SUBLANES, T), I32)],
        scratch_shapes=[pltpu.VMEM((RET_HEADS, RET_QK_DIM, RET_V_DIM), F32),
                        pltpu.VMEM((2, POOL_HALO, POOL_WIDTH), BF16),
                        pltpu.VMEM((tile, POOL_WIDTH), BF16),
                        pltpu.VMEM((tile, RET_HEADS * RET_V_DIM), BF16),
                        pltpu.VMEM((tile, XA_WIDTH), BF16),
                        pltpu.VMEM((tile, RET_HEADS * RET_QK_DIM), BF16),
                        pltpu.VMEM((tile, RET_HEADS * RET_QK_DIM), BF16),
                        pltpu.VMEM((RET_HEADS * RET_QK_DIM, tile), BF16),
                        pltpu.VMEM((RET_HEADS * RET_QK_DIM, tile), BF16),
                        pltpu.VMEM((RET_HEADS * RET_QK_DIM, D_MODEL), BF16),
                        pltpu.VMEM((POOL_WIDTH, D_MODEL), BF16),
                        pltpu.VMEM((mem_len, 2 * XA_WIDTH), BF16)],
        compiler_params=pltpu.CompilerParams(dimension_semantics=("arbitrary", "arbitrary"),
                                             vmem_limit_bytes=VMEM_LIMIT),
        name="mixer",
    )(xf, cos_t, sin_t, memf, *consts)


META_LANES = LANES


def _positions_kernel(eid_ref, tri_ref, low_ref, pos_ref, first_tile_ref, n_tiles_ref, *, n_tok):
    n_chunks = n_tok // RANK_CHUNK
    erow = lax.broadcasted_iota(I32, (N_EXPERTS, RANK_CHUNK), 0)

    def onehot(c):
        sl = slice(c * RANK_CHUNK, (c + 1) * RANK_CHUNK)
        m0 = eid_ref[0:1, sl] == erow
        m1 = eid_ref[1:2, sl] == erow
        return m0, m1, jnp.where(m0, 1.0, 0.0) + jnp.where(m1, 1.0, 0.0)

    counts = jnp.zeros((N_EXPERTS, 1), F32)
    for c in range(n_chunks):
        counts = counts + jnp.sum(onehot(c)[2], axis=1, keepdims=True)
    ptiles = jnp.floor((counts + (MOE_TM - 1)) * (1.0 / MOE_TM))
    ptiles_b = jnp.broadcast_to(ptiles, (N_EXPERTS, LANES)).astype(BF16)
    start = _dot(low_ref[...], ptiles_b)[:, 0:1] * MOE_TM

    pos_ref[...] = jnp.zeros_like(pos_ref)
    carry = start - 1.0
    for c in range(n_chunks):
        sl = slice(c * RANK_CHUNK, (c + 1) * RANK_CHUNK)
        m0, m1, oh = onehot(c)
        rank = _dot(oh.astype(BF16), tri_ref[...]) + carry
        pos_ref[0:1, sl] = jnp.sum(jnp.where(m0, rank, 0.0), axis=0, keepdims=True).astype(I32)
        pos_ref[1:2, sl] = jnp.sum(jnp.where(m1, rank, 0.0), axis=0, keepdims=True).astype(I32)
        carry = carry + jnp.sum(oh, axis=1, keepdims=True)

    first_tile_ref[...] = jnp.broadcast_to(start * (1.0 / MOE_TM), first_tile_ref.shape).astype(I32)
    n_tiles_ref[...] = jnp.broadcast_to(ptiles, n_tiles_ref.shape).astype(I32)


def _positions(eid):
    T = eid.shape[1]
    r = np.arange(RANK_CHUNK)
    tri = jnp.asarray(r[:, None] <= r[None, :], BF16)
    e = np.arange(N_EXPERTS)
    low = jnp.asarray(e[None, :] < e[:, None], BF16)
    full = lambda a: pl.BlockSpec(a.shape, lambda i: (0,) * a.ndim)
    return pl.pallas_call(
        functools.partial(_positions_kernel, n_tok=T),
        grid=(1,),
        in_specs=[full(eid), full(tri), full(low)],
        out_specs=[pl.BlockSpec((SUBLANES, T), lambda i: (0, 0)),
                   pl.BlockSpec((N_EXPERTS, META_LANES), lambda i: (0, 0)),
                   pl.BlockSpec((N_EXPERTS, META_LANES), lambda i: (0, 0))],
        out_shape=[jax.ShapeDtypeStruct((SUBLANES, T), I32), jax.ShapeDtypeStruct((N_EXPERTS, META_LANES), I32),
                   jax.ShapeDtypeStruct((N_EXPERTS, META_LANES), I32)],
        name="route_positions",
    )(eid, tri, low)


def _sc_workers():
    info = plsc.get_sparse_core_info()
    return info.num_cores, info.num_cores * info.num_subcores


def _sc_scatter_rows(xp, pos2d, n_out):
    T, W = xp.shape
    n_cores, n_workers = _sc_workers()
    cpw = T // SC_CHUNK // n_workers
    mesh = plsc.VectorSubcoreMesh(core_axis_name="c", subcore_axis_name="s")

    @functools.partial(
        pl.kernel, mesh=mesh, out_type=jax.ShapeDtypeStruct((n_out, W), I32),
        scratch_types=[pltpu.VMEM((TOP_K * cpw, SC_CHUNK), I32), pltpu.VMEM((2, SC_CHUNK, W), I32),
                       pltpu.SemaphoreType.DMA((2,)), pltpu.SemaphoreType.DMA((2,))],
        name="sc_scatter_rows")
    def k(x_hbm, pos_hbm, out_hbm, idx_v, rows_v, rd_sem, wr_sem):
        wid = lax.axis_index("s") * n_cores + lax.axis_index("c")
        for s in range(TOP_K):
            pltpu.sync_copy(pos_hbm.at[pl.ds(s * (T // SC_CHUNK) + wid * cpw, cpw)],
                            idx_v.at[pl.ds(s * cpw, cpw)])

        def read(j):
            return pltpu.make_async_copy(x_hbm.at[pl.ds((wid * cpw + j) * SC_CHUNK, SC_CHUNK)],
                                         rows_v.at[j % 2], rd_sem.at[j % 2])

        def write(j, s):
            return pltpu.make_async_copy(rows_v.at[j % 2], out_hbm.at[idx_v.at[s * cpw + j]], wr_sem.at[j % 2])

        _sc_two_buffer_stream(cpw, read, lambda j: [write(j, s) for s in range(TOP_K)])

    return k(xp, pos2d)


def _sc_two_buffer_stream(n, read, writes):
    read(0).start()
    for j in range(n):
        read(j).wait()
        if j + 1 < n:
            if j >= 1:
                for w in writes(j - 1):
                    w.wait()
            read(j + 1).start()
        for w in writes(j):
            w.start()
    for j in range(max(n - 2, 0), n):
        for w in writes(j):
            w.wait()


def _sc_gather_rows(y, idx2d):
    W = y.shape[1]
    n = idx2d.shape[0] * SC_CHUNK
    n_cores, n_workers = _sc_workers()
    cpw = n // SC_CHUNK // n_workers
    mesh = plsc.VectorSubcoreMesh(core_axis_name="c", subcore_axis_name="s")

    @functools.partial(
        pl.kernel, mesh=mesh, out_type=jax.ShapeDtypeStruct((n, W), I32),
        scratch_types=[pltpu.VMEM((cpw, SC_CHUNK), I32), pltpu.VMEM((2, SC_CHUNK, W), I32),
                       pltpu.SemaphoreType.DMA((2,)), pltpu.SemaphoreType.DMA((2,))],
        name="sc_gather_rows")
    def k(y_hbm, idx_hbm, out_hbm, idx_v, rows_v, rd_sem, wr_sem):
        wid = lax.axis_index("s") * n_cores + lax.axis_index("c")
        pltpu.sync_copy(idx_hbm.at[pl.ds(wid * cpw, cpw)], idx_v)

        def read(j):
            return pltpu.make_async_copy(y_hbm.at[idx_v.at[j]], rows_v.at[j % 2], rd_sem.at[j % 2])

        def write(j):
            return pltpu.make_async_copy(rows_v.at[j % 2], out_hbm.at[pl.ds((wid * cpw + j) * SC_CHUNK, SC_CHUNK)],
                                         wr_sem.at[j % 2])

        _sc_two_buffer_stream(cpw, read, lambda j: [write(j)])

    return k(y, idx2d)


def _routed_kernel(first_ref, count_ref, xs_hbm, wg_ref, wu_ref, wd_ref, ys_hbm, xbuf, ybuf, in_sem, out_sem):
    e = pl.program_id(0)
    last = pl.num_programs(0) - 1
    total = first_ref[last] + count_ref[last]

    def in_copy(g):
        slot = lax.rem(g, RING)
        return pltpu.make_async_copy(xs_hbm.at[pl.ds(g * MOE_TM, MOE_TM)], xbuf.at[slot], in_sem.at[slot])

    def out_copy(g):
        slot = lax.rem(g, RING)
        return pltpu.make_async_copy(ybuf.at[slot], ys_hbm.at[pl.ds(g * MOE_TM, MOE_TM)], out_sem.at[slot])

    @pl.when(e == 0)
    def _():
        for g0 in range(RING - 1):
            @pl.when(g0 < total)
            def _():
                in_copy(g0).start()

    wg = wg_ref[0].astype(BF16)
    wu = wu_ref[0].astype(BF16)
    wd = wd_ref[0].astype(BF16)

    def tile_step(i, carry):
        g = first_ref[e] + i
        slot = lax.rem(g, RING)
        in_copy(g).wait()

        @pl.when(g + RING - 1 < total)
        def _():
            in_copy(g + RING - 1).start()

        @pl.when(g >= RING)
        def _():
            out_copy(g - RING).wait()

        lo, hi = _unpack_halves(xbuf[slot])
        lo = lo.astype(BF16)
        hi = hi.astype(BF16)
        a = _dot(lo, wg[:PACK_W]) + _dot(hi, wg[PACK_W:])
        b = _dot(lo, wu[:PACK_W]) + _dot(hi, wu[PACK_W:])
        act = (a * _sigmoid(a) * b).astype(BF16)
        ybuf[slot] = _pack_halves(_dot(act, wd))
        out_copy(g).start()
        return carry

    lax.fori_loop(0, count_ref[e], tile_step, 0)

    @pl.when(e == last)
    def _():
        for back in range(RING, 0, -1):
            @pl.when(total >= back)
            def _():
                out_copy(total - back).wait()


def _routed_mlp(first_tile, n_tiles, xs, wg, wu, wd):
    R = xs.shape[0]
    any_space = pl.BlockSpec(memory_space=pl.ANY)
    return pl.pallas_call(
        _routed_kernel,
        grid_spec=pltpu.PrefetchScalarGridSpec(
            num_scalar_prefetch=2,
            grid=(N_EXPERTS,),
            in_specs=[any_space,
                      pl.BlockSpec((1, D_MODEL, D_EXPERT), lambda e, ft, nt: (e, 0, 0)),
                      pl.BlockSpec((1, D_MODEL, D_EXPERT), lambda e, ft, nt: (e, 0, 0)),
                      pl.BlockSpec((1, D_EXPERT, D_MODEL), lambda e, ft, nt: (e, 0, 0))],
            out_specs=any_space,
            scratch_shapes=[pltpu.VMEM((RING, MOE_TM, PACK_W), I32), pltpu.VMEM((RING, MOE_TM, PACK_W), I32),
                            pltpu.SemaphoreType.DMA((RING,)), pltpu.SemaphoreType.DMA((RING,))]),
        out_shape=jax.ShapeDtypeStruct((R, PACK_W), I32),
        compiler_params=pltpu.CompilerParams(dimension_semantics=("arbitrary",)),
        name="routed_mlp",
    )(first_tile, n_tiles, xs, wg, wu, wd)


def _combine_kernel(xp_ref, y0_ref, y1_ref, route_ref, lnw_ref, lnb_ref, *out_refs):
    o_ref = out_refs[-1]
    w_rows = lax.bitcast_convert_type(route_ref[...], F32)
    w_cols = jnp.transpose(jnp.concatenate([w_rows] * (LANES // SUBLANES), axis=0))
    w0 = w_cols[:, TOP_K:TOP_K + 1]
    w1 = w_cols[:, TOP_K + 1:TOP_K + 2]
    xlo, xhi = _unpack_halves(xp_ref[...])
    y0lo, y0hi = _unpack_halves(y0_ref[...])
    y1lo, y1hi = _unpack_halves(y1_ref[...])
    h = jnp.concatenate([ALPHA * xlo + (w0 * y0lo + w1 * y1lo), ALPHA * xhi + (w0 * y0hi + w1 * y1hi)], axis=1)
    o_ref[...] = _layer_norm(h, lnw_ref[...], lnb_ref[...])


def _combine_ln2(xp, yg_parts, route, lnw, lnb, tile=1024):
    T = xp.shape[0]
    n_parts = len(yg_parts)
    nt = T // tile // n_parts
    full = lambda a: pl.BlockSpec(a.shape, lambda i: (0,) * a.ndim)
    out = None
    for p, yg in enumerate(yg_parts):
        rows = lambda w, p=p: pl.BlockSpec((tile, w), lambda i: (i + p * nt, 0))
        in_specs = [rows(PACK_W),
                    pl.BlockSpec((tile, PACK_W), lambda i: (i, 0)),
                    pl.BlockSpec((tile, PACK_W), lambda i: (i + nt, 0)),
                    pl.BlockSpec((SUBLANES, tile), lambda i, p=p: (0, i + p * nt)), full(lnw), full(lnb)]
        args = [xp, yg, yg, route, lnw, lnb]
        aliases = {}
        if out is not None:
            in_specs.append(pl.BlockSpec(memory_space=pl.ANY))
            args.append(out)
            aliases = {len(args) - 1: 0}
        out = pl.pallas_call(
            _combine_kernel,
            grid=(nt,),
            in_specs=in_specs,
            out_specs=rows(D_MODEL),
            out_shape=jax.ShapeDtypeStruct((T, D_MODEL), F32),
            input_output_aliases=aliases,
            compiler_params=pltpu.CompilerParams(dimension_semantics=("arbitrary",), vmem_limit_bytes=VMEM_LIMIT),
            name="combine_ln2",
        )(*args)
    return out


def _cast_kernel(*refs):
    n = len(refs) // 2
    for src, dst in zip(refs[:n], refs[n:]):
        dst[...] = src[...].astype(BF16)


def _cast_bf16(*weights):
    whole = lambda a: pl.BlockSpec(a.shape, lambda i: (0,) * a.ndim, pipeline_mode=pl.Buffered(1))
    return pl.pallas_call(
        _cast_kernel,
        grid=(1,),
        in_specs=[whole(w) for w in weights],
        out_specs=[whole(w) for w in weights],
        out_shape=[jax.ShapeDtypeStruct(w.shape, BF16) for w in weights],
        compiler_params=pltpu.CompilerParams(vmem_limit_bytes=VMEM_LIMIT),
        name="cast_weights",
    )(*weights)


def _router_params(w_grp, b_grp, w_exp, b_exp):
    rw = jnp.zeros((D_MODEL, ROUTER_ROWS), F32)
    rw = rw.at[:, 0:N_GROUPS].set(w_grp).at[:, EXP_ROW0:EXP_ROW0 + N_EXPERTS].set(w_exp)
    rb = jnp.zeros((ROUTER_ROWS,), F32).at[N_GROUPS:8].set(NEG_BIG)
    rb = rb.at[0:N_GROUPS].set(b_grp).at[EXP_ROW0:EXP_ROW0 + N_EXPERTS].set(b_exp)
    rw_hi = rw.astype(BF16)
    rw_lo = (rw - rw_hi.astype(F32)).astype(BF16)
    return jnp.concatenate([rw_hi, rw_lo], axis=1), rb[:, None]


def kernel(x, mem, positions, w_in, w_pool_grp, pool_scale, ret_gn_w, w_mem_kv, w_br_pool, w_br_ret, w_br_xa,
           w_out, ln1_w, ln1_b, w_grp_router, b_grp_router, w_exp_router, b_exp_router, w_exp_gate, w_exp_up,
           w_exp_down, ln2_w, ln2_b):
    B, S, D = x.shape
    assert D == D_MODEL and w_in.shape[0] == DEPTH and S % 512 == 0
    T = B * S
    M = mem.shape[1]
    l = 0
    xf = x.reshape(T, D)

    rope = _rope_table(positions.reshape(1, T))
    rw, rb = _router_params(w_grp_router[l], b_grp_router[l], w_exp_router[l], b_exp_router[l])
    wmem, wp, wr, wa, wo = _cast_bf16(w_mem_kv[l], w_br_pool[l], w_br_ret[l], w_br_xa[l], w_out[l])
    xp, route = _mixer(xf, rope, mem.reshape(B * M, D), wmem, w_in[l].astype(BF16),
                       w_pool_grp[l], pool_scale[l][None, :], ret_gn_w[l].reshape(1, -1),
                       wp, wr, wa, wo, ln1_w[l][None, :], ln1_b[l][None, :], rw, rb, B, S, M)

    pos, first_tile, n_tiles = _positions(route)
    pos2d = pos[0:TOP_K].reshape(TOP_K * T // SC_CHUNK, SC_CHUNK)
    max_tiles = (TOP_K * T + N_EXPERTS * (MOE_TM - 1)) // MOE_TM
    xs = _sc_scatter_rows(xp, pos2d, max_tiles * MOE_TM)
    ys = _routed_mlp(first_tile[:, 0], n_tiles[:, 0], xs,
                     w_exp_gate[l].reshape(N_EXPERTS, D_MODEL, D_EXPERT),
                     w_exp_up[l].reshape(N_EXPERTS, D_MODEL, D_EXPERT),
                     w_exp_down[l].reshape(N_EXPERTS, D_EXPERT, D_MODEL))
    rng = T // COMBINE_PARTS
    yg_parts = [_sc_gather_rows(ys, pos[0:TOP_K, p * rng:(p + 1) * rng].reshape(TOP_K * rng // SC_CHUNK, SC_CHUNK))
                for p in range(COMBINE_PARTS)]
    out = _combine_ln2(xp, yg_parts, route, ln2_w[l][None, :], ln2_b[l][None, :])
    return out.reshape(B, S, D)
```

```python
import functools

import numpy as np
import jax
import jax.numpy as jnp
from jax import lax
from jax.experimental import pallas as pl
from jax.experimental.pallas import tpu as pltpu
from jax.experimental.pallas import tpu_sc as plsc

F32 = jnp.float32
BF16 = jnp.bfloat16
I32 = jnp.int32
U32 = jnp.uint32

D_MODEL = 1024
POOL_WINDOWS = (2, 4, 8, 16)
POOL_GROUP_DIM = 128
POOL_WIDTH = 512
POOL_HALO = 16
RET_HEADS = 4
RET_QK_DIM = 128
RET_V_DIM = 256
RET_CHUNK = 128
ROPE_BASE = 10000.0
XA_HEADS = 4
XA_HEAD_DIM = 128
XA_WIDTH = 512
N_GROUPS = 4
EXPERTS_PER_GROUP = 8
N_EXPERTS = N_GROUPS * EXPERTS_PER_GROUP
D_EXPERT = 256
LN_EPS = 1e-5
DEPTH = 1
ALPHA = (2.0 * DEPTH) ** 0.25
NEG_BIG = -1e30

COL_POOL, COL_Q, COL_K, COL_V, COL_G, COL_XAQ, COL_GATES = 0, 512, 1024, 1536, 2560, 3584, 4096

V7X_VMEM_BYTES = 64 * 1024 * 1024
VMEM_LIMIT = V7X_VMEM_BYTES * 7 // 8
SUBLANES = 8
LANES = 128

TOP_K = 2
PACK_W = D_MODEL // 2
MOE_TM = 512
SC_CHUNK = 64
RANK_CHUNK = 512
COMBINE_PARTS = 2
EXPERTS_PER_STEP = 2
RING = 4
STRIP = 256
LN_ROWS = 32


def _dot(a, b):
    return jnp.dot(a, b, preferred_element_type=F32)


def _dot_nt(a, b):
    return lax.dot_general(a, b, (((1,), (1,)), ((), ())), preferred_element_type=F32)


def _sigmoid(z):
    return 1.0 / (1.0 + jnp.exp2(z * (-1.0 / np.log(2.0))))


def _layer_norm(h, w, b):
    mu = jnp.mean(h, axis=-1, keepdims=True)
    hc = h - mu
    var = jnp.mean(hc * hc, axis=-1, keepdims=True)
    return hc * lax.rsqrt(var + LN_EPS) * w + b


ROPE_LO = 64
ROPE_PARTS = 3


def _rope_kernel(pos_ref, freq_ref, tab_ref, cos_t_ref, sin_t_ref):
    pos = pos_ref[...]

    def emit(cos_t, sin_t):
        cos_t_ref[...] = cos_t
        sin_t_ref[...] = sin_t

    in_table = jnp.logical_and(jnp.min(pos) >= 0, jnp.max(pos) < ROPE_LO * LANES)

    @pl.when(in_table)
    def _():
        idx = lax.broadcasted_iota(I32, (LANES, pos.shape[1]), 0)
        pick_hi = jnp.where(idx == jnp.right_shift(pos, ROPE_LO.bit_length() - 1), 1.0, 0.0).astype(BF16)
        pick_lo = jnp.where(idx == (pos & (ROPE_LO - 1)), 1.0, 0.0).astype(BF16)

        def look(k, pick):
            return sum(_dot(tab_ref[k * ROPE_PARTS + p], pick) for p in range(ROPE_PARTS))

        cos_a, sin_a, cos_b, sin_b = look(0, pick_hi), look(1, pick_hi), look(2, pick_lo), look(3, pick_lo)
        emit(cos_a * cos_b - sin_a * sin_b, sin_a * cos_b + cos_a * sin_b)

    @pl.when(jnp.logical_not(in_table))
    def _():
        ang = freq_ref[...] * pos.astype(F32)
        emit(jnp.cos(ang), jnp.sin(ang))


def _rope_tables(inv_freq):
    f = inv_freq.astype(np.float64)[:, None]
    idx = np.arange(LANES, dtype=np.float64)[None, :]
    tabs = [np.cos(ROPE_LO * idx * f), np.sin(ROPE_LO * idx * f), np.cos(idx * f), np.sin(idx * f)]
    pieces = []
    for t in tabs:
        rest = t.astype(np.float32)
        for _ in range(ROPE_PARTS):
            piece = rest.astype(BF16)
            pieces.append(piece)
            rest = rest - piece.astype(np.float32)
    return jnp.asarray(np.stack(pieces))


def _rope_table(pos_row, tile=2048):
    T = pos_row.shape[1]
    half = RET_QK_DIM // 2
    inv_freq = (ROPE_BASE ** (-np.arange(half, dtype=np.float64) / half)).astype(np.float32)
    freq = jnp.asarray(inv_freq[:, None])
    tabs = _rope_tables(inv_freq)
    out_t = pl.BlockSpec((half, tile), lambda i: (0, i))
    return pl.pallas_call(
        _rope_kernel,
        grid=(T // tile,),
        in_specs=[pl.BlockSpec((1, tile), lambda i: (0, i)), pl.BlockSpec((half, 1), lambda i: (0, 0)),
                  pl.BlockSpec(tabs.shape, lambda i: (0, 0, 0))],
        out_specs=[out_t, out_t],
        out_shape=[jax.ShapeDtypeStruct((half, T), F32)] * 2,
        name="rope_table",
    )(pos_row, freq, tabs)


POOL_SUB = 256


def _pool_bands():
    r = np.arange(POOL_SUB)[:, None]
    c = np.arange(POOL_SUB)[None, :]
    ch = np.arange(POOL_HALO)[None, :] - POOL_HALO
    main = np.stack([((r - c >= 0) & (r - c < w)) for w in POOL_WINDOWS]).astype(np.float32)
    halo = np.stack([((r - ch >= 0) & (r - ch < w)) for w in POOL_WINDOWS]).astype(np.float32)
    return jnp.asarray(main, BF16), jnp.asarray(halo, BF16)


def _pool_branch(ub, j, halo_ref, bmain_ref, bhalo_ref, o_ref, filler):
    tile = ub.shape[0]
    s0 = j * tile
    slot = lax.rem(j, 2)
    blocks = [(sb * POOL_SUB, g) for sb in range(tile // POOL_SUB) for g in range(len(POOL_WINDOWS))]
    wsum = {}
    for r0, g in blocks:
        cols = slice(g * POOL_GROUP_DIM, (g + 1) * POOL_GROUP_DIM)
        prev = halo_ref[slot] if r0 == 0 else ub[r0 - POOL_HALO:r0]
        wsum[r0, g] = _dot(bmain_ref[g], ub[r0:r0 + POOL_SUB, cols]) + _dot(bhalo_ref[g], prev[:, cols])
    filler()
    for r0, g in blocks:
        cols = slice(g * POOL_GROUP_DIM, (g + 1) * POOL_GROUP_DIM)
        pos = s0 + r0 + lax.broadcasted_iota(I32, (POOL_SUB, POOL_GROUP_DIM), 0)
        cnt = jnp.minimum(pos + 1, POOL_WINDOWS[g]).astype(F32)
        pooled = wsum[r0, g] / cnt - ub[r0:r0 + POOL_SUB, cols].astype(F32)
        o_ref[r0:r0 + POOL_SUB, cols] = pooled.astype(BF16)
    halo_ref[1 - slot] = ub[tile - POOL_HALO:tile]


def _ret_consts():
    h = np.arange(RET_HEADS, dtype=np.float64)
    log_gamma = np.log1p(-np.exp2(-5.0 - h))
    pos = np.arange(RET_CHUNK, dtype=np.float64)
    diff = pos[:, None] - pos[None, :]
    kscale = RET_QK_DIM ** -0.5
    dmask = kscale * np.where(diff >= 0, np.exp(log_gamma[:, None, None] * np.maximum(diff, 0.0)), 0.0)
    qdec = np.exp(log_gamma[:, None] * (pos + 1.0)[None, :])
    kdec = kscale * np.exp(log_gamma[:, None] * (RET_CHUNK - 1.0 - pos)[None, :])
    cdec = np.exp(log_gamma * RET_CHUNK)
    lanes = lambda a: np.broadcast_to(a[:, :, None], (RET_HEADS, RET_CHUNK, RET_QK_DIM))
    kdec_t = np.broadcast_to(kdec[:, None, :], (RET_HEADS, RET_QK_DIM, RET_CHUNK))
    return (jnp.asarray(dmask, F32), jnp.asarray(lanes(qdec), F32), jnp.asarray(kdec_t, F32),
            tuple(float(v) for v in cdec))


def _retention_branch(q, k_t, v, silu_g, cos_t_ref, sin_t_ref, dmask_ref, qdec_ref, kdec_ref,
                      gnw_ref, state_ref, rq_ref, rqd_ref, rkt_ref, rkdt_ref, o_ref, cdec, fillers):
    tile = q.shape[0]
    n_chunks = tile // RET_CHUNK
    half = RET_QK_DIM // 2
    cos_t = cos_t_ref[...]
    sin_t = sin_t_ref[...]
    cos = jnp.transpose(jnp.concatenate([cos_t, cos_t], axis=0))
    sin = jnp.transpose(jnp.concatenate([-sin_t, sin_t], axis=0))
    for h in range(RET_HEADS):
        qk = slice(h * RET_QK_DIM, (h + 1) * RET_QK_DIM)
        qh = q[:, qk]
        qr = qh * cos + pltpu.roll(qh, half, 1) * sin
        rq_ref[:, qk] = qr.astype(BF16)
        rqd_ref[:, qk] = (qr * jnp.concatenate([qdec_ref[h]] * n_chunks, axis=0)).astype(BF16)
        k1 = k_t[h * RET_QK_DIM:h * RET_QK_DIM + half]
        k2 = k_t[h * RET_QK_DIM + half:(h + 1) * RET_QK_DIM]
        kr_t = jnp.concatenate([k1 * cos_t - k2 * sin_t, k2 * cos_t + k1 * sin_t], axis=0)
        rkt_ref[qk, :] = kr_t.astype(BF16)
        rkdt_ref[qk, :] = (kr_t * jnp.concatenate([kdec_ref[h]] * n_chunks, axis=1)).astype(BF16)

    chunks = [slice(c * RET_CHUNK, (c + 1) * RET_CHUNK) for c in range(n_chunks)]
    heads = [slice(h * RET_QK_DIM, (h + 1) * RET_QK_DIM) for h in range(RET_HEADS)]
    v_heads = [slice(h * RET_V_DIM, (h + 1) * RET_V_DIM) for h in range(RET_HEADS)]
    raw = {(c, h): _dot(rq_ref[chunks[c], heads[h]], rkt_ref[heads[h], chunks[c]])
           for c in range(n_chunks) for h in range(RET_HEADS)}
    incr = {(c, h): _dot(rkdt_ref[heads[h], chunks[c]], v[chunks[c], v_heads[h]])
            for c in range(n_chunks) for h in range(RET_HEADS)}
    state_in = {}
    for h in range(RET_HEADS):
        st = state_ref[h]
        for c in range(n_chunks):
            state_in[c, h] = st.astype(BF16)
            st = cdec[h] * st + incr[c, h]
        state_ref[h] = st

    for c in range(n_chunks):
        rows = chunks[c]
        fillers[c]()
        for h in range(RET_HEADS):
            qk = heads[h]
            v_cols = v_heads[h]
            scores = raw[c, h] * dmask_ref[h]
            lhs = jnp.concatenate([scores.astype(BF16), rqd_ref[rows, qk]], axis=1)
            y = _dot(lhs, jnp.concatenate([v[rows, v_cols], state_in[c, h]], axis=0))
            mu = jnp.mean(y, axis=-1, keepdims=True)
            yc = y - mu
            var = jnp.mean(yc * yc, axis=-1, keepdims=True)
            yn = yc * lax.rsqrt(var + LN_EPS) * gnw_ref[:, v_cols]
            o_ref[rows, v_cols] = (silu_g[h][rows] * yn).astype(BF16)


def _cross_attention_branch(xq, kv_ref, o_ref):
    scale = XA_HEAD_DIM ** -0.5
    for h in range(XA_HEADS):
        cols = slice(h * XA_HEAD_DIM, (h + 1) * XA_HEAD_DIM)
        v_cols = slice(XA_WIDTH + h * XA_HEAD_DIM, XA_WIDTH + (h + 1) * XA_HEAD_DIM)
        s = _dot_nt(xq[:, cols], kv_ref[:, cols]) * scale
        m = jnp.max(s, axis=-1, keepdims=True)
        p = jnp.exp(s - m)
        l = jnp.sum(p, axis=-1, keepdims=True)
        o = _dot(p.astype(BF16), kv_ref[:, v_cols]) / l
        o_ref[:, cols] = o.astype(BF16)


ROUTER_ROWS = 128
EXP_ROW0 = 8


def _route(logits_t):
    gl = logits_t[0:8]
    gmax = jnp.max(gl, axis=0, keepdims=True)
    p_grp = 1.0 / jnp.sum(jnp.exp(gl - gmax), axis=0, keepdims=True)
    idx8 = lax.broadcasted_iota(jnp.int32, gl.shape, 0)
    gsel = jnp.min(jnp.where(gl == gmax, idx8, 8), axis=0, keepdims=True)
    cl = jnp.zeros_like(gl)
    for g in range(N_GROUPS):
        r0 = EXP_ROW0 + g * EXPERTS_PER_GROUP
        cl = cl + jnp.where(gsel == g, logits_t[r0:r0 + EXPERTS_PER_GROUP], 0.0)
    v1 = jnp.max(cl, axis=0, keepdims=True)
    i1 = jnp.min(jnp.where(cl == v1, idx8, 8), axis=0, keepdims=True)
    cl2 = jnp.where(idx8 == i1, -jnp.inf, cl)
    v2 = jnp.max(cl2, axis=0, keepdims=True)
    i2 = jnp.min(jnp.where(cl2 == v2, idx8, 8), axis=0, keepdims=True)
    e21 = jnp.exp(v2 - v1)
    w1 = p_grp / (1.0 + e21)
    w2 = p_grp * e21 / (1.0 + e21)
    return gsel * EXPERTS_PER_GROUP + i1, gsel * EXPERTS_PER_GROUP + i2, w1, w2


def _pack_halves(v):
    half = v.shape[1] // 2
    lo = lax.bitcast_convert_type(v[:, :half].astype(BF16).astype(F32), U32)
    hi = lax.bitcast_convert_type(v[:, half:].astype(BF16).astype(F32), U32)
    return lax.bitcast_convert_type(lax.shift_right_logical(lo, U32(16)) | hi, I32)


def _unpack_halves(w):
    u = lax.bitcast_convert_type(w, U32)
    lo = lax.bitcast_convert_type(lax.shift_left(u, U32(16)), F32)
    hi = lax.bitcast_convert_type(u & U32(0xFFFF0000), F32)
    return lo, hi


def _mixer_kernel(x_ref, cos_t_ref, sin_t_ref, mem_ref, wmem_ref, win_ref, wgrp_ref,
                  pscale_ref, bmain_ref, bhalo_ref, dmask_ref, qdec_ref, kdec_ref, gnw_ref, wp_ref, wr_ref,
                  wa_ref, wo_ref, lnw_ref, lnb_ref, rw_ref, rb_ref, xp_ref, route_ref,
                  state_ref, halo_ref, ypool_ref, yret_ref, yxa_ref, rq_ref, rqd_ref, rkt_ref, rkdt_ref, wkt_ref,
                  wpool_ref, kv_ref, *, tile, cdec):
    j = pl.program_id(1)

    @pl.when(jnp.logical_and(pl.program_id(0) == 0, j == 0))
    def _():
        wk = win_ref[:, COL_K:COL_K + RET_HEADS * RET_QK_DIM].astype(F32)
        wkt_ref[...] = jnp.transpose(wk).astype(BF16)
        for g in range(len(POOL_WINDOWS)):
            rows = slice(g * POOL_GROUP_DIM, (g + 1) * POOL_GROUP_DIM)
            wg = (wgrp_ref[g].astype(F32) * pscale_ref[:, rows]).astype(BF16)
            wpool_ref[rows, :] = _dot(wg, wp_ref[rows, :]).astype(BF16)

    @pl.when(j == 0)
    def _():
        state_ref[...] = jnp.zeros_like(state_ref)
        halo_ref[...] = jnp.zeros_like(halo_ref)
        kv_ref[...] = _dot(mem_ref[...].astype(BF16), wmem_ref[...]).astype(BF16)

    x = x_ref[...]
    xb = x.astype(BF16)

    def proj(col, width):
        return _dot(xb, win_ref[:, col:col + width])

    part = {}
    strips = [slice(c, c + STRIP) for c in range(0, D_MODEL, STRIP)]

    def gate(branch, cols):
        return _sigmoid(proj(COL_GATES + branch * D_MODEL + cols.start, STRIP))

    def pool_part():
        def pool_gates():
            part["pool_gate"] = [gate(0, c) for c in strips]

        _pool_branch(proj(COL_POOL, POOL_WIDTH).astype(BF16), j, halo_ref, bmain_ref, bhalo_ref, ypool_ref,
                     pool_gates)
        part["pool"] = [part["pool_gate"][i] * _dot(ypool_ref[...], wpool_ref[:, c]) for i, c in enumerate(strips)]

    def xa_part():
        _cross_attention_branch(proj(COL_XAQ, XA_WIDTH).astype(BF16), kv_ref, yxa_ref)
        part["xa"] = [gate(2, c) * _dot(yxa_ref[...], wa_ref[:, c]) for c in strips]

    def ret_gate_part():
        part["ret_gate"] = [gate(1, c) for c in strips]

    silu_g = []
    for h in range(RET_HEADS):
        gh = proj(COL_G + h * RET_V_DIM, RET_V_DIM)
        silu_g.append(gh * _sigmoid(gh))
    fillers = [pool_part, xa_part, ret_gate_part] + [lambda: None] * (tile // RET_CHUNK - 3)
    _retention_branch(proj(COL_Q, RET_HEADS * RET_QK_DIM), _dot_nt(wkt_ref[...], xb),
                      proj(COL_V, RET_HEADS * RET_V_DIM).astype(BF16), silu_g, cos_t_ref, sin_t_ref, dmask_ref,
                      qdec_ref, kdec_ref, gnw_ref, state_ref, rq_ref, rqd_ref, rkt_ref, rkdt_ref, yret_ref, cdec,
                      fillers)
    merged = jnp.concatenate(
        [(part["pool"][i] + part["ret_gate"][i] * _dot(yret_ref[...], wr_ref[:, c]) + part["xa"][i]).astype(BF16)
         for i, c in enumerate(strips)], axis=1)
    h = jnp.concatenate([ALPHA * x[:, c] + _dot(merged, wo_ref[:, c]) for c in strips], axis=1)
    x1 = jnp.concatenate([_layer_norm(h[r:r + LN_ROWS], lnw_ref[...], lnb_ref[...])
                          for r in range(0, tile, LN_ROWS)], axis=0)
    xp_ref[...] = _pack_halves(x1)
    x1_hi = x1.astype(BF16)
    x1_lo = (x1 - x1_hi.astype(F32)).astype(BF16)
    p4 = _dot(jnp.concatenate([x1_hi, x1_lo], axis=0), rw_ref[...])
    logits = (p4[:tile, :ROUTER_ROWS] + p4[:tile, ROUTER_ROWS:]) + (p4[tile:, :ROUTER_ROWS] + p4[tile:, ROUTER_ROWS:])
    e0, e1, w0, w1 = _route(jnp.transpose(logits) + rb_ref[...])
    w_bits = [lax.bitcast_convert_type(w, I32) for w in (w0, w1)]
    route_ref[...] = jnp.concatenate([e0, e1, *w_bits, jnp.zeros((SUBLANES - 2 * TOP_K, tile), I32)], axis=0)


def _mixer(xf, rope, memf, wmem, win, wgrp, pscale, gnw, wp, wr, wa, wo, lnw, lnb, rw, rb, batch, seq, mem_len,
           tile=512):
    T = xf.shape[0]
    nj = seq // tile
    cos_t, sin_t = rope
    bmain, bhalo = _pool_bands()
    dmask, qdec, kdec, cdec = _ret_consts()
    resident = lambda a: pl.BlockSpec(a.shape, lambda b, j: (0,) * a.ndim, pipeline_mode=pl.Buffered(1))
    rowblk = lambda w: pl.BlockSpec((tile, w), lambda b, j: (b * nj + j, 0))
    colblk = lambda r: pl.BlockSpec((r, tile), lambda b, j: (0, b * nj + j))
    consts = (wmem, win, wgrp, pscale, bmain, bhalo, dmask, qdec, kdec, gnw, wp, wr, wa, wo, lnw, lnb, rw, rb)
    return pl.pallas_call(
        functools.partial(_mixer_kernel, tile=tile, cdec=cdec),
        grid=(batch, nj),
        in_specs=[rowblk(D_MODEL), colblk(RET_QK_DIM // 2), colblk(RET_QK_DIM // 2),
                  pl.BlockSpec((mem_len, D_MODEL), lambda b, j: (b, 0))] + [resident(a) for a in consts],
        out_specs=[rowblk(PACK_W), colblk(SUBLANES)],
        out_shape=[jax.ShapeDtypeStruct((T, PACK_W), I32), jax.ShapeDtypeStruct((SUBLANES, T), I32)],
        scratch_shapes=[pltpu.VMEM((RET_HEADS, RET_QK_DIM, RET_V_DIM), F32),
                        pltpu.VMEM((2, POOL_HALO, POOL_WIDTH), BF16),
                        pltpu.VMEM((tile, POOL_WIDTH), BF16),
                        pltpu.VMEM((tile, RET_HEADS * RET_V_DIM), BF16),
                        pltpu.VMEM((tile, XA_WIDTH), BF16),
                        pltpu.VMEM((tile, RET_HEADS * RET_QK_DIM), BF16),
                        pltpu.VMEM((tile, RET_HEADS * RET_QK_DIM), BF16),
                        pltpu.VMEM((RET_HEADS * RET_QK_DIM, tile), BF16),
                        pltpu.VMEM((RET_HEADS * RET_QK_DIM, tile), BF16),
                        pltpu.VMEM((RET_HEADS * RET_QK_DIM, D_MODEL), BF16),
                        pltpu.VMEM((POOL_WIDTH, D_MODEL), BF16),
                        pltpu.VMEM((mem_len, 2 * XA_WIDTH), BF16)],
        compiler_params=pltpu.CompilerParams(dimension_semantics=("arbitrary", "arbitrary"),
                                             vmem_limit_bytes=VMEM_LIMIT),
        name="mixer",
    )(xf, cos_t, sin_t, memf, *consts)


META_LANES = LANES


def _positions_kernel(eid_ref, tri_ref, low_ref, pos_ref, first_tile_ref, n_tiles_ref, *, n_tok):
    n_chunks = n_tok // RANK_CHUNK
    erow = lax.broadcasted_iota(I32, (N_EXPERTS, RANK_CHUNK), 0)

    def onehot(c):
        sl = slice(c * RANK_CHUNK, (c + 1) * RANK_CHUNK)
        m0 = eid_ref[0:1, sl] == erow
        m1 = eid_ref[1:2, sl] == erow
        return m0, m1, jnp.where(m0, 1.0, 0.0) + jnp.where(m1, 1.0, 0.0)

    counts = jnp.zeros((N_EXPERTS, 1), F32)
    for c in range(n_chunks):
        counts = counts + jnp.sum(onehot(c)[2], axis=1, keepdims=True)
    ptiles = jnp.floor((counts + (MOE_TM - 1)) * (1.0 / MOE_TM))
    ptiles_b = jnp.broadcast_to(ptiles, (N_EXPERTS, LANES)).astype(BF16)
    start = _dot(low_ref[...], ptiles_b)[:, 0:1] * MOE_TM

    pos_ref[...] = jnp.zeros_like(pos_ref)
    carry = start - 1.0
    for c in range(n_chunks):
        sl = slice(c * RANK_CHUNK, (c + 1) * RANK_CHUNK)
        m0, m1, oh = onehot(c)
        rank = _dot(oh.astype(BF16), tri_ref[...]) + carry
        pos_ref[0:1, sl] = jnp.sum(jnp.where(m0, rank, 0.0), axis=0, keepdims=True).astype(I32)
        pos_ref[1:2, sl] = jnp.sum(jnp.where(m1, rank, 0.0), axis=0, keepdims=True).astype(I32)
        carry = carry + jnp.sum(oh, axis=1, keepdims=True)

    first_tile_ref[...] = jnp.broadcast_to(start * (1.0 / MOE_TM), first_tile_ref.shape).astype(I32)
    n_tiles_ref[...] = jnp.broadcast_to(ptiles, n_tiles_ref.shape).astype(I32)


def _positions(eid):
    T = eid.shape[1]
    r = np.arange(RANK_CHUNK)
    tri = jnp.asarray(r[:, None] <= r[None, :], BF16)
    e = np.arange(N_EXPERTS)
    low = jnp.asarray(e[None, :] < e[:, None], BF16)
    full = lambda a: pl.BlockSpec(a.shape, lambda i: (0,) * a.ndim)
    return pl.pallas_call(
        functools.partial(_positions_kernel, n_tok=T),
        grid=(1,),
        in_specs=[full(eid), full(tri), full(low)],
        out_specs=[pl.BlockSpec((SUBLANES, T), lambda i: (0, 0)),
                   pl.BlockSpec((N_EXPERTS, META_LANES), lambda i: (0, 0)),
                   pl.BlockSpec((N_EXPERTS, META_LANES), lambda i: (0, 0))],
        out_shape=[jax.ShapeDtypeStruct((SUBLANES, T), I32), jax.ShapeDtypeStruct((N_EXPERTS, META_LANES), I32),
                   jax.ShapeDtypeStruct((N_EXPERTS, META_LANES), I32)],
        name="route_positions",
    )(eid, tri, low)


def _sc_workers():
    info = plsc.get_sparse_core_info()
    return info.num_cores, info.num_cores * info.num_subcores


def _sc_scatter_rows(xp, pos2d, n_out):
    T, W = xp.shape
    n_cores, n_workers = _sc_workers()
    cpw = T // SC_CHUNK // n_workers
    mesh = plsc.VectorSubcoreMesh(core_axis_name="c", subcore_axis_name="s")

    @functools.partial(
        pl.kernel, mesh=mesh, out_type=jax.ShapeDtypeStruct((n_out, W), I32),
        scratch_types=[pltpu.VMEM((TOP_K * cpw, SC_CHUNK), I32), pltpu.VMEM((2, SC_CHUNK, W), I32),
                       pltpu.SemaphoreType.DMA((2,)), pltpu.SemaphoreType.DMA((2,))],
        name="sc_scatter_rows")
    def k(x_hbm, pos_hbm, out_hbm, idx_v, rows_v, rd_sem, wr_sem):
        wid = lax.axis_index("s") * n_cores + lax.axis_index("c")
        for s in range(TOP_K):
            pltpu.sync_copy(pos_hbm.at[pl.ds(s * (T // SC_CHUNK) + wid * cpw, cpw)],
                            idx_v.at[pl.ds(s * cpw, cpw)])

        def read(j):
            return pltpu.make_async_copy(x_hbm.at[pl.ds((wid * cpw + j) * SC_CHUNK, SC_CHUNK)],
                                         rows_v.at[j % 2], rd_sem.at[j % 2])

        def write(j, s):
            return pltpu.make_async_copy(rows_v.at[j % 2], out_hbm.at[idx_v.at[s * cpw + j]], wr_sem.at[j % 2])

        _sc_two_buffer_stream(cpw, read, lambda j: [write(j, s) for s in range(TOP_K)])

    return k(xp, pos2d)


def _sc_two_buffer_stream(n, read, writes):
    read(0).start()
    for j in range(n):
        read(j).wait()
        if j + 1 < n:
            if j >= 1:
                for w in writes(j - 1):
                    w.wait()
            read(j + 1).start()
        for w in writes(j):
            w.start()
    for j in range(max(n - 2, 0), n):
        for w in writes(j):
            w.wait()


def _sc_gather_rows(y, idx2d):
    W = y.shape[1]
    n = idx2d.shape[0] * SC_CHUNK
    n_cores, n_workers = _sc_workers()
    cpw = n // SC_CHUNK // n_workers
    mesh = plsc.VectorSubcoreMesh(core_axis_name="c", subcore_axis_name="s")

    @functools.partial(
        pl.kernel, mesh=mesh, out_type=jax.ShapeDtypeStruct((n, W), I32),
        scratch_types=[pltpu.VMEM((cpw, SC_CHUNK), I32), pltpu.VMEM((2, SC_CHUNK, W), I32),
                       pltpu.SemaphoreType.DMA((2,)), pltpu.SemaphoreType.DMA((2,))],
        name="sc_gather_rows")
    def k(y_hbm, idx_hbm, out_hbm, idx_v, rows_v, rd_sem, wr_sem):
        wid = lax.axis_index("s") * n_cores + lax.axis_index("c")
        pltpu.sync_copy(idx_hbm.at[pl.ds(wid * cpw, cpw)], idx_v)

        def read(j):
            return pltpu.make_async_copy(y_hbm.at[idx_v.at[j]], rows_v.at[j % 2], rd_sem.at[j % 2])

        def write(j):
            return pltpu.make_async_copy(rows_v.at[j % 2], out_hbm.at[pl.ds((wid * cpw + j) * SC_CHUNK, SC_CHUNK)],
                                         wr_sem.at[j % 2])

        _sc_two_buffer_stream(cpw, read, lambda j: [write(j)])

    return k(y, idx2d)


def _routed_kernel(first_ref, count_ref, xs_hbm, wg_ref, wu_ref, wd_ref, ys_hbm, xbuf, ybuf, in_sem, out_sem):
    step = pl.program_id(0)
    total = first_ref[N_EXPERTS - 1] + count_ref[N_EXPERTS - 1]

    def in_copy(g):
        slot = lax.rem(g, RING)
        return pltpu.make_async_copy(xs_hbm.at[pl.ds(g * MOE_TM, MOE_TM)], xbuf.at[slot], in_sem.at[slot])

    def out_copy(g):
        slot = lax.rem(g, RING)
        return pltpu.make_async_copy(ybuf.at[slot], ys_hbm.at[pl.ds(g * MOE_TM, MOE_TM)], out_sem.at[slot])

    @pl.when(step == 0)
    def _():
        for g0 in range(RING - 1):
            @pl.when(g0 < total)
            def _():
                in_copy(g0).start()

    for k in range(EXPERTS_PER_STEP):
        _routed_expert(step * EXPERTS_PER_STEP + k, wg_ref[k].astype(BF16), wu_ref[k].astype(BF16),
                       wd_ref[k].astype(BF16), first_ref, count_ref, total, in_copy, out_copy, xbuf, ybuf)

    @pl.when(step == pl.num_programs(0) - 1)
    def _():
        for back in range(RING, 0, -1):
            @pl.when(total >= back)
            def _():
                out_copy(total - back).wait()


def _routed_expert(e, wg, wu, wd, first_ref, count_ref, total, in_copy, out_copy, xbuf, ybuf):
    def tile_step(i, carry):
        g = first_ref[e] + i
        slot = lax.rem(g, RING)
        in_copy(g).wait()

        @pl.when(g + RING - 1 < total)
        def _():
            in_copy(g + RING - 1).start()

        @pl.when(g >= RING)
        def _():
            out_copy(g - RING).wait()

        lo, hi = _unpack_halves(xbuf[slot])
        lo = lo.astype(BF16)
        hi = hi.astype(BF16)
        a = _dot(lo, wg[:PACK_W]) + _dot(hi, wg[PACK_W:])
        b = _dot(lo, wu[:PACK_W]) + _dot(hi, wu[PACK_W:])
        act = (a * _sigmoid(a) * b).astype(BF16)
        ybuf[slot] = _pack_halves(_dot(act, wd))
        out_copy(g).start()
        return carry

    lax.fori_loop(0, count_ref[e], tile_step, 0)


def _routed_mlp(first_tile, n_tiles, xs, wg, wu, wd):
    R = xs.shape[0]
    any_space = pl.BlockSpec(memory_space=pl.ANY)
    return pl.pallas_call(
        _routed_kernel,
        grid_spec=pltpu.PrefetchScalarGridSpec(
            num_scalar_prefetch=2,
            grid=(N_EXPERTS // EXPERTS_PER_STEP,),
            in_specs=[any_space,
                      pl.BlockSpec((EXPERTS_PER_STEP, D_MODEL, D_EXPERT), lambda s, ft, nt: (s, 0, 0)),
                      pl.BlockSpec((EXPERTS_PER_STEP, D_MODEL, D_EXPERT), lambda s, ft, nt: (s, 0, 0)),
                      pl.BlockSpec((EXPERTS_PER_STEP, D_EXPERT, D_MODEL), lambda s, ft, nt: (s, 0, 0))],
            out_specs=any_space,
            scratch_shapes=[pltpu.VMEM((RING, MOE_TM, PACK_W), I32), pltpu.VMEM((RING, MOE_TM, PACK_W), I32),
                            pltpu.SemaphoreType.DMA((RING,)), pltpu.SemaphoreType.DMA((RING,))]),
        out_shape=jax.ShapeDtypeStruct((R, PACK_W), I32),
        compiler_params=pltpu.CompilerParams(dimension_semantics=("arbitrary",)),
        name="routed_mlp",
    )(first_tile, n_tiles, xs, wg, wu, wd)


def _combine_kernel(xp_ref, y0_ref, y1_ref, route_ref, lnw_ref, lnb_ref, *out_refs):
    o_ref = out_refs[-1]
    w_rows = lax.bitcast_convert_type(route_ref[...], F32)
    w_cols = jnp.transpose(jnp.concatenate([w_rows] * (LANES // SUBLANES), axis=0))
    w0 = w_cols[:, TOP_K:TOP_K + 1]
    w1 = w_cols[:, TOP_K + 1:TOP_K + 2]
    xlo, xhi = _unpack_halves(xp_ref[...])
    y0lo, y0hi = _unpack_halves(y0_ref[...])
    y1lo, y1hi = _unpack_halves(y1_ref[...])
    h = jnp.concatenate([ALPHA * xlo + (w0 * y0lo + w1 * y1lo), ALPHA * xhi + (w0 * y0hi + w1 * y1hi)], axis=1)
    o_ref[...] = _layer_norm(h, lnw_ref[...], lnb_ref[...])


def _combine_ln2(xp, yg_parts, route, lnw, lnb, tile=1024):
    T = xp.shape[0]
    n_parts = len(yg_parts)
    nt = T // tile // n_parts
    full = lambda a: pl.BlockSpec(a.shape, lambda i: (0,) * a.ndim)
    out = None
    for p, yg in enumerate(yg_parts):
        rows = lambda w, p=p: pl.BlockSpec((tile, w), lambda i: (i + p * nt, 0))
        in_specs = [rows(PACK_W),
                    pl.BlockSpec((tile, PACK_W), lambda i: (i, 0)),
                    pl.BlockSpec((tile, PACK_W), lambda i: (i + nt, 0)),
                    pl.BlockSpec((SUBLANES, tile), lambda i, p=p: (0, i + p * nt)), full(lnw), full(lnb)]
        args = [xp, yg, yg, route, lnw, lnb]
        aliases = {}
        if out is not None:
            in_specs.append(pl.BlockSpec(memory_space=pl.ANY))
            args.append(out)
            aliases = {len(args) - 1: 0}
        out = pl.pallas_call(
            _combine_kernel,
            grid=(nt,),
            in_specs=in_specs,
            out_specs=rows(D_MODEL),
            out_shape=jax.ShapeDtypeStruct((T, D_MODEL), F32),
            input_output_aliases=aliases,
            compiler_params=pltpu.CompilerParams(dimension_semantics=("arbitrary",), vmem_limit_bytes=VMEM_LIMIT),
            name="combine_ln2",
        )(*args)
    return out


def _router_params(w_grp, b_grp, w_exp, b_exp):
    rw = jnp.zeros((D_MODEL, ROUTER_ROWS), F32)
    rw = rw.at[:, 0:N_GROUPS].set(w_grp).at[:, EXP_ROW0:EXP_ROW0 + N_EXPERTS].set(w_exp)
    rb = jnp.zeros((ROUTER_ROWS,), F32).at[N_GROUPS:8].set(NEG_BIG)
    rb = rb.at[0:N_GROUPS].set(b_grp).at[EXP_ROW0:EXP_ROW0 + N_EXPERTS].set(b_exp)
    rw_hi = rw.astype(BF16)
    rw_lo = (rw - rw_hi.astype(F32)).astype(BF16)
    return jnp.concatenate([rw_hi, rw_lo], axis=1), rb[:, None]


def kernel(x, mem, positions, w_in, w_pool_grp, pool_scale, ret_gn_w, w_mem_kv, w_br_pool, w_br_ret, w_br_xa,
           w_out, ln1_w, ln1_b, w_grp_router, b_grp_router, w_exp_router, b_exp_router, w_exp_gate, w_exp_up,
           w_exp_down, ln2_w, ln2_b):
    B, S, D = x.shape
    assert D == D_MODEL and w_in.shape[0] == DEPTH and S % 512 == 0
    T = B * S
    M = mem.shape[1]
    l = 0
    xf = x.reshape(T, D)

    rope = _rope_table(positions.reshape(1, T))
    rw, rb = _router_params(w_grp_router[l], b_grp_router[l], w_exp_router[l], b_exp_router[l])
    xp, route = _mixer(xf, rope, mem.reshape(B * M, D), w_mem_kv[l].astype(BF16), w_in[l].astype(BF16),
                       w_pool_grp[l], pool_scale[l][None, :], ret_gn_w[l].reshape(1, -1),
                       w_br_pool[l].astype(BF16), w_br_ret[l].astype(BF16), w_br_xa[l].astype(BF16),
                       w_out[l].astype(BF16), ln1_w[l][None, :], ln1_b[l][None, :], rw, rb, B, S, M)

    pos, first_tile, n_tiles = _positions(route)
    pos2d = pos[0:TOP_K].reshape(TOP_K * T // SC_CHUNK, SC_CHUNK)
    max_tiles = (TOP_K * T + N_EXPERTS * (MOE_TM - 1)) // MOE_TM
    xs = _sc_scatter_rows(xp, pos2d, max_tiles * MOE_TM)
    ys = _routed_mlp(first_tile[:, 0], n_tiles[:, 0], xs,
                     w_exp_gate[l].reshape(N_EXPERTS, D_MODEL, D_EXPERT),
                     w_exp_up[l].reshape(N_EXPERTS, D_MODEL, D_EXPERT),
                     w_exp_down[l].reshape(N_EXPERTS, D_EXPERT, D_MODEL))
    rng = T // COMBINE_PARTS
    yg_parts = [_sc_gather_rows(ys, pos[0:TOP_K, p * rng:(p + 1) * rng].reshape(TOP_K * rng // SC_CHUNK, SC_CHUNK))
                for p in range(COMBINE_PARTS)]
    out = _combine_ln2(xp, yg_parts, route, ln2_w[l][None, :], ln2_b[l][None, :])
    return out.reshape(B, S, D)
```

```python
import functools

import numpy as np
import jax
import jax.numpy as jnp
from jax import lax
from jax.experimental import pallas as pl
from jax.experimental.pallas import tpu as pltpu
from jax.experimental.pallas import tpu_sc as plsc

F32 = jnp.float32
BF16 = jnp.bfloat16
I32 = jnp.int32
U32 = jnp.uint32

D_MODEL = 1024
POOL_WINDOWS = (2, 4, 8, 16)
POOL_GROUP_DIM = 128
POOL_WIDTH = 512
POOL_HALO = 16
RET_HEADS = 4
RET_QK_DIM = 128
RET_V_DIM = 256
RET_CHUNK = 128
ROPE_BASE = 10000.0
XA_HEADS = 4
XA_HEAD_DIM = 128
XA_WIDTH = 512
N_GROUPS = 4
EXPERTS_PER_GROUP = 8
N_EXPERTS = N_GROUPS * EXPERTS_PER_GROUP
D_EXPERT = 256
LN_EPS = 1e-5
DEPTH = 1
ALPHA = (2.0 * DEPTH) ** 0.25
NEG_BIG = -1e30

COL_POOL, COL_Q, COL_K, COL_V, COL_G, COL_XAQ, COL_GATES = 0, 512, 1024, 1536, 2560, 3584, 4096

V7X_VMEM_BYTES = 64 * 1024 * 1024
VMEM_LIMIT = V7X_VMEM_BYTES * 7 // 8
SUBLANES = 8
LANES = 128

TOP_K = 2
PACK_W = D_MODEL // 2
MOE_TM = 512
SC_LANES = 16
SC_CHUNK = 64
RANK_CHUNK = 512
COMBINE_PARTS = 2
RING = 4
STRIP = 256
LN_ROWS = 32


def _dot(a, b):
    return jnp.dot(a, b, preferred_element_type=F32)


def _dot_nt(a, b):
    return lax.dot_general(a, b, (((1,), (1,)), ((), ())), preferred_element_type=F32)


def _sigmoid(z):
    return 1.0 / (1.0 + jnp.exp2(z * (-1.0 / np.log(2.0))))


def _layer_norm(h, w, b):
    mu = jnp.mean(h, axis=-1, keepdims=True)
    hc = h - mu
    var = jnp.mean(hc * hc, axis=-1, keepdims=True)
    return hc * lax.rsqrt(var + LN_EPS) * w + b


ROPE_LO = 64
ROPE_PARTS = 3


def _rope_kernel(pos_ref, freq_ref, tab_ref, cos_t_ref, sin_t_ref):
    pos = pos_ref[...]

    def emit(cos_t, sin_t):
        cos_t_ref[...] = cos_t
        sin_t_ref[...] = sin_t

    in_table = jnp.logical_and(jnp.min(pos) >= 0, jnp.max(pos) < ROPE_LO * LANES)

    @pl.when(in_table)
    def _():
        idx = lax.broadcasted_iota(I32, (LANES, pos.shape[1]), 0)
        pick_hi = jnp.where(idx == jnp.right_shift(pos, ROPE_LO.bit_length() - 1), 1.0, 0.0).astype(BF16)
        pick_lo = jnp.where(idx == (pos & (ROPE_LO - 1)), 1.0, 0.0).astype(BF16)

        def look(k, pick):
            return sum(_dot(tab_ref[k * ROPE_PARTS + p], pick) for p in range(ROPE_PARTS))

        cos_a, sin_a, cos_b, sin_b = look(0, pick_hi), look(1, pick_hi), look(2, pick_lo), look(3, pick_lo)
        emit(cos_a * cos_b - sin_a * sin_b, sin_a * cos_b + cos_a * sin_b)

    @pl.when(jnp.logical_not(in_table))
    def _():
        ang = freq_ref[...] * pos.astype(F32)
        emit(jnp.cos(ang), jnp.sin(ang))


def _rope_tables(inv_freq):
    f = inv_freq.astype(np.float64)[:, None]
    idx = np.arange(LANES, dtype=np.float64)[None, :]
    tabs = [np.cos(ROPE_LO * idx * f), np.sin(ROPE_LO * idx * f), np.cos(idx * f), np.sin(idx * f)]
    pieces = []
    for t in tabs:
        rest = t.astype(np.float32)
        for _ in range(ROPE_PARTS):
            piece = rest.astype(BF16)
            pieces.append(piece)
            rest = rest - piece.astype(np.float32)
    return jnp.asarray(np.stack(pieces))


def _rope_table(pos_row, tile=2048):
    T = pos_row.shape[1]
    half = RET_QK_DIM // 2
    inv_freq = (ROPE_BASE ** (-np.arange(half, dtype=np.float64) / half)).astype(np.float32)
    freq = jnp.asarray(inv_freq[:, None])
    tabs = _rope_tables(inv_freq)
    out_t = pl.BlockSpec((half, tile), lambda i: (0, i))
    return pl.pallas_call(
        _rope_kernel,
        grid=(T // tile,),
        in_specs=[pl.BlockSpec((1, tile), lambda i: (0, i)), pl.BlockSpec((half, 1), lambda i: (0, 0)),
                  pl.BlockSpec(tabs.shape, lambda i: (0, 0, 0))],
        out_specs=[out_t, out_t],
        out_shape=[jax.ShapeDtypeStruct((half, T), F32)] * 2,
        name="rope_table",
    )(pos_row, freq, tabs)


POOL_SUB = 256


def _pool_bands():
    r = np.arange(POOL_SUB)[:, None]
    c = np.arange(POOL_SUB)[None, :]
    ch = np.arange(POOL_HALO)[None, :] - POOL_HALO
    main = np.stack([((r - c >= 0) & (r - c < w)) for w in POOL_WINDOWS]).astype(np.float32)
    halo = np.stack([((r - ch >= 0) & (r - ch < w)) for w in POOL_WINDOWS]).astype(np.float32)
    return jnp.asarray(main, BF16), jnp.asarray(halo, BF16)


def _pool_branch(ub, j, halo_ref, bmain_ref, bhalo_ref, o_ref, filler):
    tile = ub.shape[0]
    s0 = j * tile
    slot = lax.rem(j, 2)
    blocks = [(sb * POOL_SUB, g) for sb in range(tile // POOL_SUB) for g in range(len(POOL_WINDOWS))]
    wsum = {}
    for r0, g in blocks:
        cols = slice(g * POOL_GROUP_DIM, (g + 1) * POOL_GROUP_DIM)
        prev = halo_ref[slot] if r0 == 0 else ub[r0 - POOL_HALO:r0]
        wsum[r0, g] = _dot(bmain_ref[g], ub[r0:r0 + POOL_SUB, cols]) + _dot(bhalo_ref[g], prev[:, cols])
    filler()
    for r0, g in blocks:
        cols = slice(g * POOL_GROUP_DIM, (g + 1) * POOL_GROUP_DIM)
        pos = s0 + r0 + lax.broadcasted_iota(I32, (POOL_SUB, POOL_GROUP_DIM), 0)
        cnt = jnp.minimum(pos + 1, POOL_WINDOWS[g]).astype(F32)
        pooled = wsum[r0, g] / cnt - ub[r0:r0 + POOL_SUB, cols].astype(F32)
        o_ref[r0:r0 + POOL_SUB, cols] = pooled.astype(BF16)
    halo_ref[1 - slot] = ub[tile - POOL_HALO:tile]


def _ret_consts():
    h = np.arange(RET_HEADS, dtype=np.float64)
    log_gamma = np.log1p(-np.exp2(-5.0 - h))
    pos = np.arange(RET_CHUNK, dtype=np.float64)
    diff = pos[:, None] - pos[None, :]
    kscale = RET_QK_DIM ** -0.5
    dmask = kscale * np.where(diff >= 0, np.exp(log_gamma[:, None, None] * np.maximum(diff, 0.0)), 0.0)
    qdec = np.exp(log_gamma[:, None] * (pos + 1.0)[None, :])
    kdec = kscale * np.exp(log_gamma[:, None] * (RET_CHUNK - 1.0 - pos)[None, :])
    cdec = np.exp(log_gamma * RET_CHUNK)
    lanes = lambda a: np.broadcast_to(a[:, :, None], (RET_HEADS, RET_CHUNK, RET_QK_DIM))
    kdec_t = np.broadcast_to(kdec[:, None, :], (RET_HEADS, RET_QK_DIM, RET_CHUNK))
    return (jnp.asarray(dmask, F32), jnp.asarray(lanes(qdec), F32), jnp.asarray(kdec_t, F32),
            tuple(float(v) for v in cdec))


def _retention_branch(q, k_t, v, silu_g, cos_t_ref, sin_t_ref, dmask_ref, qdec_ref, kdec_ref,
                      gnw_ref, state_ref, rq_ref, rqd_ref, rkt_ref, rkdt_ref, o_ref, cdec, fillers):
    tile = q.shape[0]
    n_chunks = tile // RET_CHUNK
    half = RET_QK_DIM // 2
    cos_t = cos_t_ref[...]
    sin_t = sin_t_ref[...]
    cos = jnp.transpose(jnp.concatenate([cos_t, cos_t], axis=0))
    sin = jnp.transpose(jnp.concatenate([-sin_t, sin_t], axis=0))
    for h in range(RET_HEADS):
        qk = slice(h * RET_QK_DIM, (h + 1) * RET_QK_DIM)
        qh = q[:, qk]
        qr = qh * cos + pltpu.roll(qh, half, 1) * sin
        rq_ref[:, qk] = qr.astype(BF16)
        rqd_ref[:, qk] = (qr * jnp.concatenate([qdec_ref[h]] * n_chunks, axis=0)).astype(BF16)
        k1 = k_t[h * RET_QK_DIM:h * RET_QK_DIM + half]
        k2 = k_t[h * RET_QK_DIM + half:(h + 1) * RET_QK_DIM]
        kr_t = jnp.concatenate([k1 * cos_t - k2 * sin_t, k2 * cos_t + k1 * sin_t], axis=0)
        rkt_ref[qk, :] = kr_t.astype(BF16)
        rkdt_ref[qk, :] = (kr_t * jnp.concatenate([kdec_ref[h]] * n_chunks, axis=1)).astype(BF16)

    chunks = [slice(c * RET_CHUNK, (c + 1) * RET_CHUNK) for c in range(n_chunks)]
    heads = [slice(h * RET_QK_DIM, (h + 1) * RET_QK_DIM) for h in range(RET_HEADS)]
    v_heads = [slice(h * RET_V_DIM, (h + 1) * RET_V_DIM) for h in range(RET_HEADS)]
    raw = {(c, h): _dot(rq_ref[chunks[c], heads[h]], rkt_ref[heads[h], chunks[c]])
           for c in range(n_chunks) for h in range(RET_HEADS)}
    incr = {(c, h): _dot(rkdt_ref[heads[h], chunks[c]], v[chunks[c], v_heads[h]])
            for c in range(n_chunks) for h in range(RET_HEADS)}
    state_in = {}
    for h in range(RET_HEADS):
        st = state_ref[h]
        for c in range(n_chunks):
            state_in[c, h] = st.astype(BF16)
            st = cdec[h] * st + incr[c, h]
        state_ref[h] = st

    for c in range(n_chunks):
        rows = chunks[c]
        fillers[c]()
        for h in range(RET_HEADS):
            qk = heads[h]
            v_cols = v_heads[h]
            scores = raw[c, h] * dmask_ref[h]
            lhs = jnp.concatenate([scores.astype(BF16), rqd_ref[rows, qk]], axis=1)
            y = _dot(lhs, jnp.concatenate([v[rows, v_cols], state_in[c, h]], axis=0))
            mu = jnp.mean(y, axis=-1, keepdims=True)
            yc = y - mu
            var = jnp.mean(yc * yc, axis=-1, keepdims=True)
            yn = yc * lax.rsqrt(var + LN_EPS) * gnw_ref[:, v_cols]
            o_ref[rows, v_cols] = (silu_g[h][rows] * yn).astype(BF16)


def _cross_attention_branch(xq, kv_ref, o_ref):
    scale = XA_HEAD_DIM ** -0.5
    for h in range(XA_HEADS):
        cols = slice(h * XA_HEAD_DIM, (h + 1) * XA_HEAD_DIM)
        v_cols = slice(XA_WIDTH + h * XA_HEAD_DIM, XA_WIDTH + (h + 1) * XA_HEAD_DIM)
        s = _dot_nt(xq[:, cols], kv_ref[:, cols]) * scale
        m = jnp.max(s, axis=-1, keepdims=True)
        p = jnp.exp(s - m)
        l = jnp.sum(p, axis=-1, keepdims=True)
        o = _dot(p.astype(BF16), kv_ref[:, v_cols]) / l
        o_ref[:, cols] = o.astype(BF16)


ROUTER_ROWS = 128
EXP_ROW0 = 8


def _route(logits_t):
    gl = logits_t[0:8]
    gmax = jnp.max(gl, axis=0, keepdims=True)
    p_grp = 1.0 / jnp.sum(jnp.exp(gl - gmax), axis=0, keepdims=True)
    idx8 = lax.broadcasted_iota(jnp.int32, gl.shape, 0)
    gsel = jnp.min(jnp.where(gl == gmax, idx8, 8), axis=0, keepdims=True)
    cl = jnp.zeros_like(gl)
    for g in range(N_GROUPS):
        r0 = EXP_ROW0 + g * EXPERTS_PER_GROUP
        cl = cl + jnp.where(gsel == g, logits_t[r0:r0 + EXPERTS_PER_GROUP], 0.0)
    v1 = jnp.max(cl, axis=0, keepdims=True)
    i1 = jnp.min(jnp.where(cl == v1, idx8, 8), axis=0, keepdims=True)
    cl2 = jnp.where(idx8 == i1, -jnp.inf, cl)
    v2 = jnp.max(cl2, axis=0, keepdims=True)
    i2 = jnp.min(jnp.where(cl2 == v2, idx8, 8), axis=0, keepdims=True)
    e21 = jnp.exp(v2 - v1)
    w1 = p_grp / (1.0 + e21)
    w2 = p_grp * e21 / (1.0 + e21)
    return gsel * EXPERTS_PER_GROUP + i1, gsel * EXPERTS_PER_GROUP + i2, w1, w2


def _pack_halves(v):
    half = v.shape[1] // 2
    lo = lax.bitcast_convert_type(v[:, :half].astype(BF16).astype(F32), U32)
    hi = lax.bitcast_convert_type(v[:, half:].astype(BF16).astype(F32), U32)
    return lax.bitcast_convert_type(lax.shift_right_logical(lo, U32(16)) | hi, I32)


def _unpack_halves(w):
    u = lax.bitcast_convert_type(w, U32)
    lo = lax.bitcast_convert_type(lax.shift_left(u, U32(16)), F32)
    hi = lax.bitcast_convert_type(u & U32(0xFFFF0000), F32)
    return lo, hi


def _mixer_kernel(x_ref, cos_t_ref, sin_t_ref, mem_ref, wmem_ref, win_ref, wgrp_ref,
                  pscale_ref, bmain_ref, bhalo_ref, dmask_ref, qdec_ref, kdec_ref, gnw_ref, wp_ref, wr_ref,
                  wa_ref, wo_ref, lnw_ref, lnb_ref, rw_ref, rb_ref, xp_ref, route_ref,
                  state_ref, halo_ref, ypool_ref, yret_ref, yxa_ref, rq_ref, rqd_ref, rkt_ref, rkdt_ref, wkt_ref,
                  wpool_ref, kv_ref, *, tile, cdec):
    j = pl.program_id(1)

    @pl.when(jnp.logical_and(pl.program_id(0) == 0, j == 0))
    def _():
        wk = win_ref[:, COL_K:COL_K + RET_HEADS * RET_QK_DIM].astype(F32)
        wkt_ref[...] = jnp.transpose(wk).astype(BF16)
        for g in range(len(POOL_WINDOWS)):
            rows = slice(g * POOL_GROUP_DIM, (g + 1) * POOL_GROUP_DIM)
            wg = (wgrp_ref[g].astype(F32) * pscale_ref[:, rows]).astype(BF16)
            wpool_ref[rows, :] = _dot(wg, wp_ref[rows, :]).astype(BF16)

    @pl.when(j == 0)
    def _():
        state_ref[...] = jnp.zeros_like(state_ref)
        halo_ref[...] = jnp.zeros_like(halo_ref)
        kv_ref[...] = _dot(mem_ref[...].astype(BF16), wmem_ref[...]).astype(BF16)

    x = x_ref[...]
    xb = x.astype(BF16)

    def proj(col, width):
        return _dot(xb, win_ref[:, col:col + width])

    part = {}
    strips = [slice(c, c + STRIP) for c in range(0, D_MODEL, STRIP)]

    def gate(branch, cols):
        return _sigmoid(proj(COL_GATES + branch * D_MODEL + cols.start, STRIP))

    def pool_part():
        def pool_gates():
            part["pool_gate"] = [gate(0, c) for c in strips]

        _pool_branch(proj(COL_POOL, POOL_WIDTH).astype(BF16), j, halo_ref, bmain_ref, bhalo_ref, ypool_ref,
                     pool_gates)
        part["pool"] = [part["pool_gate"][i] * _dot(ypool_ref[...], wpool_ref[:, c]) for i, c in enumerate(strips)]

    def xa_part():
        _cross_attention_branch(proj(COL_XAQ, XA_WIDTH).astype(BF16), kv_ref, yxa_ref)
        part["xa"] = [gate(2, c) * _dot(yxa_ref[...], wa_ref[:, c]) for c in strips]

    def ret_gate_part():
        part["ret_gate"] = [gate(1, c) for c in strips]

    silu_g = []
    for h in range(RET_HEADS):
        gh = proj(COL_G + h * RET_V_DIM, RET_V_DIM)
        silu_g.append(gh * _sigmoid(gh))
    fillers = [pool_part, xa_part, ret_gate_part] + [lambda: None] * (tile // RET_CHUNK - 3)
    _retention_branch(proj(COL_Q, RET_HEADS * RET_QK_DIM), _dot_nt(wkt_ref[...], xb),
                      proj(COL_V, RET_HEADS * RET_V_DIM).astype(BF16), silu_g, cos_t_ref, sin_t_ref, dmask_ref,
                      qdec_ref, kdec_ref, gnw_ref, state_ref, rq_ref, rqd_ref, rkt_ref, rkdt_ref, yret_ref, cdec,
                      fillers)
    merged = jnp.concatenate(
        [(part["pool"][i] + part["ret_gate"][i] * _dot(yret_ref[...], wr_ref[:, c]) + part["xa"][i]).astype(BF16)
         for i, c in enumerate(strips)], axis=1)
    h = jnp.concatenate([ALPHA * x[:, c] + _dot(merged, wo_ref[:, c]) for c in strips], axis=1)
    x1 = jnp.concatenate([_layer_norm(h[r:r + LN_ROWS], lnw_ref[...], lnb_ref[...])
                          for r in range(0, tile, LN_ROWS)], axis=0)
    xp_ref[...] = _pack_halves(x1)
    x1_hi = x1.astype(BF16)
    x1_lo = (x1 - x1_hi.astype(F32)).astype(BF16)
    p4 = _dot(jnp.concatenate([x1_hi, x1_lo], axis=0), rw_ref[...])
    logits = (p4[:tile, :ROUTER_ROWS] + p4[:tile, ROUTER_ROWS:]) + (p4[tile:, :ROUTER_ROWS] + p4[tile:, ROUTER_ROWS:])
    e0, e1, w0, w1 = _route(jnp.transpose(logits) + rb_ref[...])
    w_bits = [lax.bitcast_convert_type(w, I32) for w in (w0, w1)]
    route_ref[...] = jnp.concatenate([e0, e1, *w_bits, jnp.zeros((SUBLANES - 2 * TOP_K, tile), I32)], axis=0)


def _mixer(xf, rope, memf, wmem, win, wgrp, pscale, gnw, wp, wr, wa, wo, lnw, lnb, rw, rb, batch, seq, mem_len,
           tile=512):
    T = xf.shape[0]
    nj = seq // tile
    cos_t, sin_t = rope
    bmain, bhalo = _pool_bands()
    dmask, qdec, kdec, cdec = _ret_consts()
    resident = lambda a: pl.BlockSpec(a.shape, lambda b, j: (0,) * a.ndim, pipeline_mode=pl.Buffered(1))
    rowblk = lambda w: pl.BlockSpec((tile, w), lambda b, j: (b * nj + j, 0))
    colblk = lambda r: pl.BlockSpec((r, tile), lambda b, j: (0, b * nj + j))
    consts = (wmem, win, wgrp, pscale, bmain, bhalo, dmask, qdec, kdec, gnw, wp, wr, wa, wo, lnw, lnb, rw, rb)
    return pl.pallas_call(
        functools.partial(_mixer_kernel, tile=tile, cdec=cdec),
        grid=(batch, nj),
        in_specs=[rowblk(D_MODEL), colblk(RET_QK_DIM // 2), colblk(RET_QK_DIM // 2),
                  pl.BlockSpec((mem_len, D_MODEL), lambda b, j: (b, 0))] + [resident(a) for a in consts],
        out_specs=[rowblk(PACK_W), colblk(SUBLANES)],
        out_shape=[jax.ShapeDtypeStruct((T, PACK_W), I32), jax.ShapeDtypeStruct((SUBLANES, T), I32)],
        scratch_shapes=[pltpu.VMEM((RET_HEADS, RET_QK_DIM, RET_V_DIM), F32),
                        pltpu.VMEM((2, POOL_HALO, POOL_WIDTH), BF16),
                        pltpu.VMEM((tile, POOL_WIDTH), BF16),
                        pltpu.VMEM((tile, RET_HEADS * RET_V_DIM), BF16),
                        pltpu.VMEM((tile, XA_WIDTH), BF16),
                        pltpu.VMEM((tile, RET_HEADS * RET_QK_DIM), BF16),
                        pltpu.VMEM((tile, RET_HEADS * RET_QK_DIM), BF16),
                        pltpu.VMEM((RET_HEADS * RET_QK_DIM, tile), BF16),
                        pltpu.VMEM((RET_HEADS * RET_QK_DIM, tile), BF16),
                        pltpu.VMEM((RET_HEADS * RET_QK_DIM, D_MODEL), BF16),
                        pltpu.VMEM((POOL_WIDTH, D_MODEL), BF16),
                        pltpu.VMEM((mem_len, 2 * XA_WIDTH), BF16)],
        compiler_params=pltpu.CompilerParams(dimension_semantics=("arbitrary", "arbitrary"),
                                             vmem_limit_bytes=VMEM_LIMIT),
        name="mixer",
    )(xf, cos_t, sin_t, memf, *consts)


META_LANES = LANES


def _positions_kernel(eid_ref, tri_ref, low_ref, pos_ref, first_tile_ref, n_tiles_ref, *, n_tok):
    n_chunks = n_tok // RANK_CHUNK
    erow = lax.broadcasted_iota(I32, (N_EXPERTS, RANK_CHUNK), 0)

    def onehot(c):
        sl = slice(c * RANK_CHUNK, (c + 1) * RANK_CHUNK)
        m0 = eid_ref[0:1, sl] == erow
        m1 = eid_ref[1:2, sl] == erow
        return m0, m1, jnp.where(m0, 1.0, 0.0) + jnp.where(m1, 1.0, 0.0)

    counts = jnp.zeros((N_EXPERTS, 1), F32)
    for c in range(n_chunks):
        counts = counts + jnp.sum(onehot(c)[2], axis=1, keepdims=True)
    ptiles = jnp.floor((counts + (MOE_TM - 1)) * (1.0 / MOE_TM))
    ptiles_b = jnp.broadcast_to(ptiles, (N_EXPERTS, LANES)).astype(BF16)
    start = _dot(low_ref[...], ptiles_b)[:, 0:1] * MOE_TM

    pos_ref[...] = jnp.zeros_like(pos_ref)
    carry = start - 1.0
    for c in range(n_chunks):
        sl = slice(c * RANK_CHUNK, (c + 1) * RANK_CHUNK)
        m0, m1, oh = onehot(c)
        rank = _dot(oh.astype(BF16), tri_ref[...]) + carry
        pos_ref[0:1, sl] = jnp.sum(jnp.where(m0, rank, 0.0), axis=0, keepdims=True).astype(I32)
        pos_ref[1:2, sl] = jnp.sum(jnp.where(m1, rank, 0.0), axis=0, keepdims=True).astype(I32)
        carry = carry + jnp.sum(oh, axis=1, keepdims=True)

    first_tile_ref[...] = jnp.broadcast_to(start * (1.0 / MOE_TM), first_tile_ref.shape).astype(I32)
    n_tiles_ref[...] = jnp.broadcast_to(ptiles, n_tiles_ref.shape).astype(I32)


def _positions(eid):
    T = eid.shape[1]
    r = np.arange(RANK_CHUNK)
    tri = jnp.asarray(r[:, None] <= r[None, :], BF16)
    e = np.arange(N_EXPERTS)
    low = jnp.asarray(e[None, :] < e[:, None], BF16)
    full = lambda a: pl.BlockSpec(a.shape, lambda i: (0,) * a.ndim)
    return pl.pallas_call(
        functools.partial(_positions_kernel, n_tok=T),
        grid=(1,),
        in_specs=[full(eid), full(tri), full(low)],
        out_specs=[pl.BlockSpec((SUBLANES, T), lambda i: (0, 0)),
                   pl.BlockSpec((N_EXPERTS, META_LANES), lambda i: (0, 0)),
                   pl.BlockSpec((N_EXPERTS, META_LANES), lambda i: (0, 0))],
        out_shape=[jax.ShapeDtypeStruct((SUBLANES, T), I32), jax.ShapeDtypeStruct((N_EXPERTS, META_LANES), I32),
                   jax.ShapeDtypeStruct((N_EXPERTS, META_LANES), I32)],
        name="route_positions",
    )(eid, tri, low)


def _sc_workers():
    info = plsc.get_sparse_core_info()
    return info.num_cores, info.num_cores * info.num_subcores


def _sc_scatter_rows(xp, pos2d, n_out):
    T, W = xp.shape
    n_cores, n_workers = _sc_workers()
    cpw = T // SC_CHUNK // n_workers
    mesh = plsc.VectorSubcoreMesh(core_axis_name="c", subcore_axis_name="s")

    @functools.partial(
        pl.kernel, mesh=mesh, out_type=jax.ShapeDtypeStruct((n_out, W), I32),
        scratch_types=[pltpu.VMEM((TOP_K * cpw, SC_CHUNK), I32), pltpu.VMEM((2, SC_CHUNK, W), I32),
                       pltpu.SemaphoreType.DMA((2,)), pltpu.SemaphoreType.DMA((2,))],
        name="sc_scatter_rows")
    def k(x_hbm, pos_hbm, out_hbm, idx_v, rows_v, rd_sem, wr_sem):
        wid = lax.axis_index("s") * n_cores + lax.axis_index("c")
        for s in range(TOP_K):
            pltpu.sync_copy(pos_hbm.at[pl.ds(s * (T // SC_CHUNK) + wid * cpw, cpw)],
                            idx_v.at[pl.ds(s * cpw, cpw)])

        def read(j):
            return pltpu.make_async_copy(x_hbm.at[pl.ds((wid * cpw + j) * SC_CHUNK, SC_CHUNK)],
                                         rows_v.at[j % 2], rd_sem.at[j % 2])

        def write(j, s):
            return pltpu.make_async_copy(rows_v.at[j % 2], out_hbm.at[idx_v.at[s * cpw + j]], wr_sem.at[j % 2])

        _sc_two_buffer_stream(cpw, read, lambda j: [write(j, s) for s in range(TOP_K)])

    return k(xp, pos2d)


def _sc_two_buffer_stream(n, read, writes):
    read(0).start()
    for j in range(n):
        read(j).wait()
        if j + 1 < n:
            if j >= 1:
                for w in writes(j - 1):
                    w.wait()
            read(j + 1).start()
        for w in writes(j):
            w.start()
    for j in range(max(n - 2, 0), n):
        for w in writes(j):
            w.wait()


def _sc_gather_rows(y, idx2d):
    W = y.shape[1]
    n = idx2d.shape[0] * SC_CHUNK
    n_cores, n_workers = _sc_workers()
    cpw = n // SC_CHUNK // n_workers
    mesh = plsc.VectorSubcoreMesh(core_axis_name="c", subcore_axis_name="s")

    @functools.partial(
        pl.kernel, mesh=mesh, out_type=jax.ShapeDtypeStruct((n, W), I32),
        scratch_types=[pltpu.VMEM((cpw, SC_CHUNK), I32), pltpu.VMEM((2, SC_CHUNK, W), I32),
                       pltpu.SemaphoreType.DMA((2,)), pltpu.SemaphoreType.DMA((2,))],
        name="sc_gather_rows")
    def k(y_hbm, idx_hbm, out_hbm, idx_v, rows_v, rd_sem, wr_sem):
        wid = lax.axis_index("s") * n_cores + lax.axis_index("c")
        pltpu.sync_copy(idx_hbm.at[pl.ds(wid * cpw, cpw)], idx_v)

        def read(j):
            return pltpu.make_async_copy(y_hbm.at[idx_v.at[j]], rows_v.at[j % 2], rd_sem.at[j % 2])

        def write(j):
            return pltpu.make_async_copy(rows_v.at[j % 2], out_hbm.at[pl.ds((wid * cpw + j) * SC_CHUNK, SC_CHUNK)],
                                         wr_sem.at[j % 2])

        _sc_two_buffer_stream(cpw, read, lambda j: [write(j)])

    return k(y, idx2d)


def _bf16_bits(x):
    lsb = lax.shift_right_logical(x, I32(16)) & I32(1)
    r = x + I32(0x7FFF) + lsb
    is_nan = (x & I32(0x7FFFFFFF)) > I32(0x7F800000)
    r = jnp.where(is_nan, x | I32(0x00400000), r)
    return jnp.where((x & I32(0x7F800000)) == I32(0), x & I32(-2147483648), r)


def _sc_pack_cols(w, chunk_rows):
    R, C = w.shape
    half = C // 2
    n_cores, n_workers = _sc_workers()
    cpw = R // chunk_rows // n_workers
    assert cpw * chunk_rows * n_workers == R and half % SC_LANES == 0
    mesh = plsc.VectorSubcoreMesh(core_axis_name="c", subcore_axis_name="s")

    @functools.partial(
        pl.kernel, mesh=mesh, out_type=jax.ShapeDtypeStruct((R, half), I32),
        scratch_types=[pltpu.VMEM((chunk_rows, C), I32), pltpu.VMEM((chunk_rows, half), I32)],
        name="sc_pack_cols")
    def k(w_hbm, out_hbm, in_v, out_v):
        wid = lax.axis_index("s") * n_cores + lax.axis_index("c")

        def chunk(ci, carry):
            row0 = (wid * cpw + ci) * chunk_rows
            pltpu.sync_copy(w_hbm.at[pl.ds(row0, chunk_rows)], in_v)

            @plsc.parallel_loop(0, chunk_rows, unroll=2)
            def _(r):
                for j in range(half // SC_LANES):
                    lo = _bf16_bits(in_v[r, pl.ds(j * SC_LANES, SC_LANES)])
                    hi = _bf16_bits(in_v[r, pl.ds(half + j * SC_LANES, SC_LANES)])
                    out_v[r, pl.ds(j * SC_LANES, SC_LANES)] = lax.shift_right_logical(lo, I32(16)) | (hi & I32(-65536))

            pltpu.sync_copy(out_v, out_hbm.at[pl.ds(row0, chunk_rows)])
            return carry

        lax.fori_loop(0, cpw, chunk, 0)

    return k(lax.bitcast_convert_type(w, I32))


def _unpack_weight(w):
    lo, hi = _unpack_halves(w)
    return jnp.concatenate([lo.astype(BF16), hi.astype(BF16)], axis=1)


def _routed_kernel(first_ref, count_ref, xs_hbm, wg_ref, wu_ref, wd_ref, ys_hbm, xbuf, ybuf, in_sem, out_sem):
    e = pl.program_id(0)
    last = pl.num_programs(0) - 1
    total = first_ref[last] + count_ref[last]

    def in_copy(g):
        slot = lax.rem(g, RING)
        return pltpu.make_async_copy(xs_hbm.at[pl.ds(g * MOE_TM, MOE_TM)], xbuf.at[slot], in_sem.at[slot])

    def out_copy(g):
        slot = lax.rem(g, RING)
        return pltpu.make_async_copy(ybuf.at[slot], ys_hbm.at[pl.ds(g * MOE_TM, MOE_TM)], out_sem.at[slot])

    @pl.when(e == 0)
    def _():
        for g0 in range(RING - 1):
            @pl.when(g0 < total)
            def _():
                in_copy(g0).start()

    wg = _unpack_weight(wg_ref[0])
    wu = _unpack_weight(wu_ref[0])
    wd = _unpack_weight(wd_ref[0])

    def tile_step(i, carry):
        g = first_ref[e] + i
        slot = lax.rem(g, RING)
        in_copy(g).wait()

        @pl.when(g + RING - 1 < total)
        def _():
            in_copy(g + RING - 1).start()

        @pl.when(g >= RING)
        def _():
            out_copy(g - RING).wait()

        lo, hi = _unpack_halves(xbuf[slot])
        lo = lo.astype(BF16)
        hi = hi.astype(BF16)
        a = _dot(lo, wg[:PACK_W]) + _dot(hi, wg[PACK_W:])
        b = _dot(lo, wu[:PACK_W]) + _dot(hi, wu[PACK_W:])
        act = (a * _sigmoid(a) * b).astype(BF16)
        ybuf[slot] = _pack_halves(_dot(act, wd))
        out_copy(g).start()
        return carry

    lax.fori_loop(0, count_ref[e], tile_step, 0)

    @pl.when(e == last)
    def _():
        for back in range(RING, 0, -1):
            @pl.when(total >= back)
            def _():
                out_copy(total - back).wait()


def _routed_mlp(first_tile, n_tiles, xs, wg, wu, wd):
    R = xs.shape[0]
    any_space = pl.BlockSpec(memory_space=pl.ANY)
    return pl.pallas_call(
        _routed_kernel,
        grid_spec=pltpu.PrefetchScalarGridSpec(
            num_scalar_prefetch=2,
            grid=(N_EXPERTS,),
            in_specs=[any_space,
                      pl.BlockSpec((1, D_MODEL, D_EXPERT // 2), lambda e, ft, nt: (e, 0, 0)),
                      pl.BlockSpec((1, D_MODEL, D_EXPERT // 2), lambda e, ft, nt: (e, 0, 0)),
                      pl.BlockSpec((1, D_EXPERT, D_MODEL // 2), lambda e, ft, nt: (e, 0, 0))],
            out_specs=any_space,
            scratch_shapes=[pltpu.VMEM((RING, MOE_TM, PACK_W), I32), pltpu.VMEM((RING, MOE_TM, PACK_W), I32),
                            pltpu.SemaphoreType.DMA((RING,)), pltpu.SemaphoreType.DMA((RING,))]),
        out_shape=jax.ShapeDtypeStruct((R, PACK_W), I32),
        compiler_params=pltpu.CompilerParams(dimension_semantics=("arbitrary",)),
        name="routed_mlp",
    )(first_tile, n_tiles, xs, wg, wu, wd)


def _combine_kernel(xp_ref, y0_ref, y1_ref, route_ref, lnw_ref, lnb_ref, *out_refs):
    o_ref = out_refs[-1]
    w_rows = lax.bitcast_convert_type(route_ref[...], F32)
    w_cols = jnp.transpose(jnp.concatenate([w_rows] * (LANES // SUBLANES), axis=0))
    w0 = w_cols[:, TOP_K:TOP_K + 1]
    w1 = w_cols[:, TOP_K + 1:TOP_K + 2]
    xlo, xhi = _unpack_halves(xp_ref[...])
    y0lo, y0hi = _unpack_halves(y0_ref[...])
    y1lo, y1hi = _unpack_halves(y1_ref[...])
    h = jnp.concatenate([ALPHA * xlo + (w0 * y0lo + w1 * y1lo), ALPHA * xhi + (w0 * y0hi + w1 * y1hi)], axis=1)
    o_ref[...] = _layer_norm(h, lnw_ref[...], lnb_ref[...])


def _combine_ln2(xp, yg_parts, route, lnw, lnb, tile=1024):
    T = xp.shape[0]
    n_parts = len(yg_parts)
    nt = T // tile // n_parts
    full = lambda a: pl.BlockSpec(a.shape, lambda i: (0,) * a.ndim)
    out = None
    for p, yg in enumerate(yg_parts):
        rows = lambda w, p=p: pl.BlockSpec((tile, w), lambda i: (i + p * nt, 0))
        in_specs = [rows(PACK_W),
                    pl.BlockSpec((tile, PACK_W), lambda i: (i, 0)),
                    pl.BlockSpec((tile, PACK_W), lambda i: (i + nt, 0)),
                    pl.BlockSpec((SUBLANES, tile), lambda i, p=p: (0, i + p * nt)), full(lnw), full(lnb)]
        args = [xp, yg, yg, route, lnw, lnb]
        aliases = {}
        if out is not None:
            in_specs.append(pl.BlockSpec(memory_space=pl.ANY))
            args.append(out)
            aliases = {len(args) - 1: 0}
        out = pl.pallas_call(
            _combine_kernel,
            grid=(nt,),
            in_specs=in_specs,
            out_specs=rows(D_MODEL),
            out_shape=jax.ShapeDtypeStruct((T, D_MODEL), F32),
            input_output_aliases=aliases,
            compiler_params=pltpu.CompilerParams(dimension_semantics=("arbitrary",), vmem_limit_bytes=VMEM_LIMIT),
            name="combine_ln2",
        )(*args)
    return out


def _router_params(w_grp, b_grp, w_exp, b_exp):
    rw = jnp.zeros((D_MODEL, ROUTER_ROWS), F32)
    rw = rw.at[:, 0:N_GROUPS].set(w_grp).at[:, EXP_ROW0:EXP_ROW0 + N_EXPERTS].set(w_exp)
    rb = jnp.zeros((ROUTER_ROWS,), F32).at[N_GROUPS:8].set(NEG_BIG)
    rb = rb.at[0:N_GROUPS].set(b_grp).at[EXP_ROW0:EXP_ROW0 + N_EXPERTS].set(b_exp)
    rw_hi = rw.astype(BF16)
    rw_lo = (rw - rw_hi.astype(F32)).astype(BF16)
    return jnp.concatenate([rw_hi, rw_lo], axis=1), rb[:, None]


def kernel(x, mem, positions, w_in, w_pool_grp, pool_scale, ret_gn_w, w_mem_kv, w_br_pool, w_br_ret, w_br_xa,
           w_out, ln1_w, ln1_b, w_grp_router, b_grp_router, w_exp_router, b_exp_router, w_exp_gate, w_exp_up,
           w_exp_down, ln2_w, ln2_b):
    B, S, D = x.shape
    assert D == D_MODEL and w_in.shape[0] == DEPTH and S % 512 == 0
    T = B * S
    M = mem.shape[1]
    l = 0
    xf = x.reshape(T, D)

    wg = _sc_pack_cols(w_exp_gate[l].reshape(N_EXPERTS * D_MODEL, D_EXPERT), SC_CHUNK)
    wu = _sc_pack_cols(w_exp_up[l].reshape(N_EXPERTS * D_MODEL, D_EXPERT), SC_CHUNK)
    wd = _sc_pack_cols(w_exp_down[l].reshape(N_EXPERTS * D_EXPERT, D_MODEL), SC_CHUNK // 4)
    wg = wg.reshape(N_EXPERTS, D_MODEL, D_EXPERT // 2)
    wu = wu.reshape(N_EXPERTS, D_MODEL, D_EXPERT // 2)
    wd = wd.reshape(N_EXPERTS, D_EXPERT, D_MODEL // 2)
    rope = _rope_table(positions.reshape(1, T))
    rw, rb = _router_params(w_grp_router[l], b_grp_router[l], w_exp_router[l], b_exp_router[l])
    xp, route = _mixer(xf, rope, mem.reshape(B * M, D), w_mem_kv[l].astype(BF16), w_in[l].astype(BF16),
                       w_pool_grp[l], pool_scale[l][None, :], ret_gn_w[l].reshape(1, -1),
                       w_br_pool[l].astype(BF16), w_br_ret[l].astype(BF16), w_br_xa[l].astype(BF16),
                       w_out[l].astype(BF16), ln1_w[l][None, :], ln1_b[l][None, :], rw, rb, B, S, M)

    pos, first_tile, n_tiles = _positions(route)
    pos2d = pos[0:TOP_K].reshape(TOP_K * T // SC_CHUNK, SC_CHUNK)
    max_tiles = (TOP_K * T + N_EXPERTS * (MOE_TM - 1)) // MOE_TM
    xs = _sc_scatter_rows(xp, pos2d, max_tiles * MOE_TM)
    ys = _routed_mlp(first_tile[:, 0], n_tiles[:, 0], xs, wg, wu, wd)
    rng = T // COMBINE_PARTS
    yg_parts = [_sc_gather_rows(ys, pos[0:TOP_K, p * rng:(p + 1) * rng].reshape(TOP_K * rng // SC_CHUNK, SC_CHUNK))
                for p in range(COMBINE_PARTS)]
    out = _combine_ln2(xp, yg_parts, route, ln2_w[l][None, :], ln2_b[l][None, :])
    return out.reshape(B, S, D)
```

```python
import functools

import numpy as np
import jax
import jax.numpy as jnp
from jax import lax
from jax.experimental import pallas as pl
from jax.experimental.pallas import tpu as pltpu
from jax.experimental.pallas import tpu_sc as plsc

F32 = jnp.float32
BF16 = jnp.bfloat16
I32 = jnp.int32
U32 = jnp.uint32

D_MODEL = 1024
POOL_WINDOWS = (2, 4, 8, 16)
POOL_GROUP_DIM = 128
POOL_WIDTH = 512
POOL_HALO = 16
RET_HEADS = 4
RET_QK_DIM = 128
RET_V_DIM = 256
RET_CHUNK = 128
ROPE_BASE = 10000.0
XA_HEADS = 4
XA_HEAD_DIM = 128
XA_WIDTH = 512
N_GROUPS = 4
EXPERTS_PER_GROUP = 8
N_EXPERTS = N_GROUPS * EXPERTS_PER_GROUP
D_EXPERT = 256
LN_EPS = 1e-5
DEPTH = 1
ALPHA = (2.0 * DEPTH) ** 0.25
NEG_BIG = -1e30

COL_POOL, COL_Q, COL_K, COL_V, COL_G, COL_XAQ, COL_GATES = 0, 512, 1024, 1536, 2560, 3584, 4096

V7X_VMEM_BYTES = 64 * 1024 * 1024
VMEM_LIMIT = V7X_VMEM_BYTES * 7 // 8
SUBLANES = 8
LANES = 128

TOP_K = 2
PACK_W = D_MODEL // 2
MOE_TM = 512
SC_CHUNK = 64
RANK_CHUNK = 512
COMBINE_PARTS = 2
RING = 4
STRIP = 256
LN_ROWS = 32


def _dot(a, b):
    return jnp.dot(a, b, preferred_element_type=F32)


def _dot_nt(a, b):
    return lax.dot_general(a, b, (((1,), (1,)), ((), ())), preferred_element_type=F32)


def _sigmoid(z):
    return 1.0 / (1.0 + jnp.exp2(z * (-1.0 / np.log(2.0))))


def _layer_norm(h, w, b):
    mu = jnp.mean(h, axis=-1, keepdims=True)
    hc = h - mu
    var = jnp.mean(hc * hc, axis=-1, keepdims=True)
    return hc * lax.rsqrt(var + LN_EPS) * w + b


ROPE_LO = 64
ROPE_PARTS = 3


def _rope_kernel(pos_ref, freq_ref, tab_ref, cos_t_ref, sin_t_ref):
    pos = pos_ref[...]

    def emit(cos_t, sin_t):
        cos_t_ref[...] = cos_t
        sin_t_ref[...] = sin_t

    in_table = jnp.logical_and(jnp.min(pos) >= 0, jnp.max(pos) < ROPE_LO * LANES)

    @pl.when(in_table)
    def _():
        idx = lax.broadcasted_iota(I32, (LANES, pos.shape[1]), 0)
        pick_hi = jnp.where(idx == jnp.right_shift(pos, ROPE_LO.bit_length() - 1), 1.0, 0.0).astype(BF16)
        pick_lo = jnp.where(idx == (pos & (ROPE_LO - 1)), 1.0, 0.0).astype(BF16)

        def look(k, pick):
            return sum(_dot(tab_ref[k * ROPE_PARTS + p], pick) for p in range(ROPE_PARTS))

        cos_a, sin_a, cos_b, sin_b = look(0, pick_hi), look(1, pick_hi), look(2, pick_lo), look(3, pick_lo)
        emit(cos_a * cos_b - sin_a * sin_b, sin_a * cos_b + cos_a * sin_b)

    @pl.when(jnp.logical_not(in_table))
    def _():
        ang = freq_ref[...] * pos.astype(F32)
        emit(jnp.cos(ang), jnp.sin(ang))


def _rope_tables(inv_freq):
    f = inv_freq.astype(np.float64)[:, None]
    idx = np.arange(LANES, dtype=np.float64)[None, :]
    tabs = [np.cos(ROPE_LO * idx * f), np.sin(ROPE_LO * idx * f), np.cos(idx * f), np.sin(idx * f)]
    pieces = []
    for t in tabs:
        rest = t.astype(np.float32)
        for _ in range(ROPE_PARTS):
            piece = rest.astype(BF16)
            pieces.append(piece)
            rest = rest - piece.astype(np.float32)
    return jnp.asarray(np.stack(pieces))


def _rope_table(pos_row, tile=2048):
    T = pos_row.shape[1]
    half = RET_QK_DIM // 2
    inv_freq = (ROPE_BASE ** (-np.arange(half, dtype=np.float64) / half)).astype(np.float32)
    freq = jnp.asarray(inv_freq[:, None])
    tabs = _rope_tables(inv_freq)
    out_t = pl.BlockSpec((half, tile), lambda i: (0, i))
    return pl.pallas_call(
        _rope_kernel,
        grid=(T // tile,),
        in_specs=[pl.BlockSpec((1, tile), lambda i: (0, i)), pl.BlockSpec((half, 1), lambda i: (0, 0)),
                  pl.BlockSpec(tabs.shape, lambda i: (0, 0, 0))],
        out_specs=[out_t, out_t],
        out_shape=[jax.ShapeDtypeStruct((half, T), F32)] * 2,
        name="rope_table",
    )(pos_row, freq, tabs)


POOL_SUB = 256


def _pool_bands():
    r = np.arange(POOL_SUB)[:, None]
    c = np.arange(POOL_SUB)[None, :]
    ch = np.arange(POOL_HALO)[None, :] - POOL_HALO
    main = np.stack([((r - c >= 0) & (r - c < w)) for w in POOL_WINDOWS]).astype(np.float32)
    halo = np.stack([((r - ch >= 0) & (r - ch < w)) for w in POOL_WINDOWS]).astype(np.float32)
    return jnp.asarray(main, BF16), jnp.asarray(halo, BF16)


def _pool_branch(ub, j, halo_ref, bmain_ref, bhalo_ref, o_ref, filler):
    tile = ub.shape[0]
    s0 = j * tile
    slot = lax.rem(j, 2)
    blocks = [(sb * POOL_SUB, g) for sb in range(tile // POOL_SUB) for g in range(len(POOL_WINDOWS))]
    wsum = {}
    for r0, g in blocks:
        cols = slice(g * POOL_GROUP_DIM, (g + 1) * POOL_GROUP_DIM)
        prev = halo_ref[slot] if r0 == 0 else ub[r0 - POOL_HALO:r0]
        wsum[r0, g] = _dot(bmain_ref[g], ub[r0:r0 + POOL_SUB, cols]) + _dot(bhalo_ref[g], prev[:, cols])
    filler()
    for r0, g in blocks:
        cols = slice(g * POOL_GROUP_DIM, (g + 1) * POOL_GROUP_DIM)
        pos = s0 + r0 + lax.broadcasted_iota(I32, (POOL_SUB, POOL_GROUP_DIM), 0)
        cnt = jnp.minimum(pos + 1, POOL_WINDOWS[g]).astype(F32)
        pooled = wsum[r0, g] / cnt - ub[r0:r0 + POOL_SUB, cols].astype(F32)
        o_ref[r0:r0 + POOL_SUB, cols] = pooled.astype(BF16)
    halo_ref[1 - slot] = ub[tile - POOL_HALO:tile]


def _ret_consts():
    h = np.arange(RET_HEADS, dtype=np.float64)
    log_gamma = np.log1p(-np.exp2(-5.0 - h))
    pos = np.arange(RET_CHUNK, dtype=np.float64)
    diff = pos[:, None] - pos[None, :]
    kscale = RET_QK_DIM ** -0.5
    dmask = kscale * np.where(diff >= 0, np.exp(log_gamma[:, None, None] * np.maximum(diff, 0.0)), 0.0)
    qdec = np.exp(log_gamma[:, None] * (pos + 1.0)[None, :])
    kdec = kscale * np.exp(log_gamma[:, None] * (RET_CHUNK - 1.0 - pos)[None, :])
    cdec = np.exp(log_gamma * RET_CHUNK)
    lanes = lambda a: np.broadcast_to(a[:, :, None], (RET_HEADS, RET_CHUNK, RET_QK_DIM))
    kdec_t = np.broadcast_to(kdec[:, None, :], (RET_HEADS, RET_QK_DIM, RET_CHUNK))
    return (jnp.asarray(dmask, F32), jnp.asarray(lanes(qdec), F32), jnp.asarray(kdec_t, F32),
            tuple(float(v) for v in cdec))


def _retention_branch(q, k_t, v, silu_g, cos_t_ref, sin_t_ref, dmask_ref, qdec_ref, kdec_ref,
                      gnw_ref, state_ref, rq_ref, rqd_ref, rkt_ref, rkdt_ref, o_ref, cdec, fillers):
    tile = q.shape[0]
    n_chunks = tile // RET_CHUNK
    half = RET_QK_DIM // 2
    cos_t = cos_t_ref[...]
    sin_t = sin_t_ref[...]
    cos = jnp.transpose(jnp.concatenate([cos_t, cos_t], axis=0))
    sin = jnp.transpose(jnp.concatenate([-sin_t, sin_t], axis=0))
    for h in range(RET_HEADS):
        qk = slice(h * RET_QK_DIM, (h + 1) * RET_QK_DIM)
        qh = q[:, qk]
        qr = qh * cos + pltpu.roll(qh, half, 1) * sin
        rq_ref[:, qk] = qr.astype(BF16)
        rqd_ref[:, qk] = (qr * jnp.concatenate([qdec_ref[h]] * n_chunks, axis=0)).astype(BF16)
        k1 = k_t[h * RET_QK_DIM:h * RET_QK_DIM + half]
        k2 = k_t[h * RET_QK_DIM + half:(h + 1) * RET_QK_DIM]
        kr_t = jnp.concatenate([k1 * cos_t - k2 * sin_t, k2 * cos_t + k1 * sin_t], axis=0)
        rkt_ref[qk, :] = kr_t.astype(BF16)
        rkdt_ref[qk, :] = (kr_t * jnp.concatenate([kdec_ref[h]] * n_chunks, axis=1)).astype(BF16)

    chunks = [slice(c * RET_CHUNK, (c + 1) * RET_CHUNK) for c in range(n_chunks)]
    heads = [slice(h * RET_QK_DIM, (h + 1) * RET_QK_DIM) for h in range(RET_HEADS)]
    v_heads = [slice(h * RET_V_DIM, (h + 1) * RET_V_DIM) for h in range(RET_HEADS)]
    raw = {(c, h): _dot(rq_ref[chunks[c], heads[h]], rkt_ref[heads[h], chunks[c]])
           for c in range(n_chunks) for h in range(RET_HEADS)}
    incr = {(c, h): _dot(rkdt_ref[heads[h], chunks[c]], v[chunks[c], v_heads[h]])
            for c in range(n_chunks) for h in range(RET_HEADS)}
    state_in = {}
    for h in range(RET_HEADS):
        st = state_ref[h]
        for c in range(n_chunks):
            state_in[c, h] = st.astype(BF16)
            st = cdec[h] * st + incr[c, h]
        state_ref[h] = st

    for c in range(n_chunks):
        rows = chunks[c]
        fillers[c]()
        for h in range(RET_HEADS):
            qk = heads[h]
            v_cols = v_heads[h]
            scores = raw[c, h] * dmask_ref[h]
            lhs = jnp.concatenate([scores.astype(BF16), rqd_ref[rows, qk]], axis=1)
            y = _dot(lhs, jnp.concatenate([v[rows, v_cols], state_in[c, h]], axis=0))
            mu = jnp.mean(y, axis=-1, keepdims=True)
            yc = y - mu
            var = jnp.mean(yc * yc, axis=-1, keepdims=True)
            yn = yc * lax.rsqrt(var + LN_EPS) * gnw_ref[:, v_cols]
            o_ref[rows, v_cols] = (silu_g[h][rows] * yn).astype(BF16)


def _cross_attention_branch(xq, kv_ref, o_ref):
    scale = XA_HEAD_DIM ** -0.5
    for h in range(XA_HEADS):
        cols = slice(h * XA_HEAD_DIM, (h + 1) * XA_HEAD_DIM)
        v_cols = slice(XA_WIDTH + h * XA_HEAD_DIM, XA_WIDTH + (h + 1) * XA_HEAD_DIM)
        s = _dot_nt(xq[:, cols], kv_ref[:, cols]) * scale
        m = jnp.max(s, axis=-1, keepdims=True)
        p = jnp.exp(s - m)
        l = jnp.sum(p, axis=-1, keepdims=True)
        o = _dot(p.astype(BF16), kv_ref[:, v_cols]) / l
        o_ref[:, cols] = o.astype(BF16)


ROUTER_ROWS = 128
EXP_ROW0 = 8


def _route(logits_t):
    gl = logits_t[0:8]
    gmax = jnp.max(gl, axis=0, keepdims=True)
    p_grp = 1.0 / jnp.sum(jnp.exp(gl - gmax), axis=0, keepdims=True)
    idx8 = lax.broadcasted_iota(jnp.int32, gl.shape, 0)
    gsel = jnp.min(jnp.where(gl == gmax, idx8, 8), axis=0, keepdims=True)
    cl = jnp.zeros_like(gl)
    for g in range(N_GROUPS):
        r0 = EXP_ROW0 + g * EXPERTS_PER_GROUP
        cl = cl + jnp.where(gsel == g, logits_t[r0:r0 + EXPERTS_PER_GROUP], 0.0)
    v1 = jnp.max(cl, axis=0, keepdims=True)
    i1 = jnp.min(jnp.where(cl == v1, idx8, 8), axis=0, keepdims=True)
    cl2 = jnp.where(idx8 == i1, -jnp.inf, cl)
    v2 = jnp.max(cl2, axis=0, keepdims=True)
    i2 = jnp.min(jnp.where(cl2 == v2, idx8, 8), axis=0, keepdims=True)
    e21 = jnp.exp(v2 - v1)
    w1 = p_grp / (1.0 + e21)
    w2 = p_grp * e21 / (1.0 + e21)
    return gsel * EXPERTS_PER_GROUP + i1, gsel * EXPERTS_PER_GROUP + i2, w1, w2


def _pack_halves(v):
    half = v.shape[1] // 2
    lo = lax.bitcast_convert_type(v[:, :half].astype(BF16).astype(F32), U32)
    hi = lax.bitcast_convert_type(v[:, half:].astype(BF16).astype(F32), U32)
    return lax.bitcast_convert_type(lax.shift_right_logical(lo, U32(16)) | hi, I32)


def _unpack_halves(w):
    u = lax.bitcast_convert_type(w, U32)
    lo = lax.bitcast_convert_type(lax.shift_left(u, U32(16)), F32)
    hi = lax.bitcast_convert_type(u & U32(0xFFFF0000), F32)
    return lo, hi


def _mixer_kernel(x_ref, cos_t_ref, sin_t_ref, mem_ref, wmem_ref, win_ref, wgrp_ref,
                  pscale_ref, bmain_ref, bhalo_ref, dmask_ref, qdec_ref, kdec_ref, gnw_ref, wp_ref, wr_ref,
                  wa_ref, wo_ref, lnw_ref, lnb_ref, rw_ref, rb_ref, xp_ref, route_ref,
                  state_ref, halo_ref, ypool_ref, yret_ref, yxa_ref, rq_ref, rqd_ref, rkt_ref, rkdt_ref, wkt_ref,
                  wpool_ref, kv_ref, *, tile, cdec):
    j = pl.program_id(1)

    @pl.when(jnp.logical_and(pl.program_id(0) == 0, j == 0))
    def _():
        wk = win_ref[:, COL_K:COL_K + RET_HEADS * RET_QK_DIM].astype(F32)
        wkt_ref[...] = jnp.transpose(wk).astype(BF16)
        for g in range(len(POOL_WINDOWS)):
            rows = slice(g * POOL_GROUP_DIM, (g + 1) * POOL_GROUP_DIM)
            wg = (wgrp_ref[g].astype(F32) * pscale_ref[:, rows]).astype(BF16)
            wpool_ref[rows, :] = _dot(wg, wp_ref[rows, :]).astype(BF16)

    @pl.when(j == 0)
    def _():
        state_ref[...] = jnp.zeros_like(state_ref)
        halo_ref[...] = jnp.zeros_like(halo_ref)
        kv_ref[...] = _dot(mem_ref[...].astype(BF16), wmem_ref[...]).astype(BF16)

    x = x_ref[...]
    xb = x.astype(BF16)

    def proj(col, width):
        return _dot(xb, win_ref[:, col:col + width])

    part = {}
    strips = [slice(c, c + STRIP) for c in range(0, D_MODEL, STRIP)]

    def gate(branch, cols):
        return _sigmoid(proj(COL_GATES + branch * D_MODEL + cols.start, STRIP))

    def pool_part():
        def pool_gates():
            part["pool_gate"] = [gate(0, c) for c in strips]

        _pool_branch(proj(COL_POOL, POOL_WIDTH).astype(BF16), j, halo_ref, bmain_ref, bhalo_ref, ypool_ref,
                     pool_gates)
        part["pool"] = [part["pool_gate"][i] * _dot(ypool_ref[...], wpool_ref[:, c]) for i, c in enumerate(strips)]

    def xa_part():
        _cross_attention_branch(proj(COL_XAQ, XA_WIDTH).astype(BF16), kv_ref, yxa_ref)
        part["xa"] = [gate(2, c) * _dot(yxa_ref[...], wa_ref[:, c]) for c in strips]

    def ret_gate_part():
        part["ret_gate"] = [gate(1, c) for c in strips]

    silu_g = []
    for h in range(RET_HEADS):
        gh = proj(COL_G + h * RET_V_DIM, RET_V_DIM)
        silu_g.append(gh * _sigmoid(gh))
    fillers = [pool_part, xa_part, ret_gate_part] + [lambda: None] * (tile // RET_CHUNK - 3)
    _retention_branch(proj(COL_Q, RET_HEADS * RET_QK_DIM), _dot_nt(wkt_ref[...], xb),
                      proj(COL_V, RET_HEADS * RET_V_DIM).astype(BF16), silu_g, cos_t_ref, sin_t_ref, dmask_ref,
                      qdec_ref, kdec_ref, gnw_ref, state_ref, rq_ref, rqd_ref, rkt_ref, rkdt_ref, yret_ref, cdec,
                      fillers)
    merged = jnp.concatenate(
        [(part["pool"][i] + part["ret_gate"][i] * _dot(yret_ref[...], wr_ref[:, c]) + part["xa"][i]).astype(BF16)
         for i, c in enumerate(strips)], axis=1)
    h = jnp.concatenate([ALPHA * x[:, c] + _dot(merged, wo_ref[:, c]) for c in strips], axis=1)
    x1 = jnp.concatenate([_layer_norm(h[r:r + LN_ROWS], lnw_ref[...], lnb_ref[...])
                          for r in range(0, tile, LN_ROWS)], axis=0)
    xp_ref[...] = _pack_halves(x1)
    x1_hi = x1.astype(BF16)
    x1_lo = (x1 - x1_hi.astype(F32)).astype(BF16)
    p4 = _dot(jnp.concatenate([x1_hi, x1_lo], axis=0), rw_ref[...])
    logits = (p4[:tile, :ROUTER_ROWS] + p4[:tile, ROUTER_ROWS:]) + (p4[tile:, :ROUTER_ROWS] + p4[tile:, ROUTER_ROWS:])
    e0, e1, w0, w1 = _route(jnp.transpose(logits) + rb_ref[...])
    w_bits = [lax.bitcast_convert_type(w, I32) for w in (w0, w1)]
    route_ref[...] = jnp.concatenate([e0, e1, *w_bits, jnp.zeros((SUBLANES - 2 * TOP_K, tile), I32)], axis=0)


def _mixer(xf, rope, memf, wmem, win, wgrp, pscale, gnw, wp, wr, wa, wo, lnw, lnb, rw, rb, batch, seq, mem_len,
           tile=512):
    T = xf.shape[0]
    nj = seq // tile
    cos_t, sin_t = rope
    bmain, bhalo = _pool_bands()
    dmask, qdec, kdec, cdec = _ret_consts()
    resident = lambda a: pl.BlockSpec(a.shape, lambda b, j: (0,) * a.ndim, pipeline_mode=pl.Buffered(1))
    rowblk = lambda w: pl.BlockSpec((tile, w), lambda b, j: (b * nj + j, 0))
    colblk = lambda r: pl.BlockSpec((r, tile), lambda b, j: (0, b * nj + j))
    consts = (wmem, win, wgrp, pscale, bmain, bhalo, dmask, qdec, kdec, gnw, wp, wr, wa, wo, lnw, lnb, rw, rb)
    return pl.pallas_call(
        functools.partial(_mixer_kernel, tile=tile, cdec=cdec),
        grid=(batch, nj),
        in_specs=[rowblk(D_MODEL), colblk(RET_QK_DIM // 2), colblk(RET_QK_DIM // 2),
                  pl.BlockSpec((mem_len, D_MODEL), lambda b, j: (b, 0))] + [resident(a) for a in consts],
        out_specs=[rowblk(PACK_W), colblk(SUBLANES)],
        out_shape=[jax.ShapeDtypeStruct((T, PACK_W), I32), jax.ShapeDtypeStruct((SUBLANES, T), I32)],
        scratch_shapes=[pltpu.VMEM((RET_HEADS, RET_QK_DIM, RET_V_DIM), F32),
                        pltpu.VMEM((2, POOL_HALO, POOL_WIDTH), BF16),
                        pltpu.VMEM((tile, POOL_WIDTH), BF16),
                        pltpu.VMEM((tile, RET_HEADS * RET_V_DIM), BF16),
                        pltpu.VMEM((tile, XA_WIDTH), BF16),
                        pltpu.VMEM((tile, RET_HEADS * RET_QK_DIM), BF16),
                        pltpu.VMEM((tile, RET_HEADS * RET_QK_DIM), BF16),
                        pltpu.VMEM((RET_HEADS * RET_QK_DIM, tile), BF16),
                        pltpu.VMEM((RET_HEADS * RET_QK_DIM, tile), BF16),
                        pltpu.VMEM((RET_HEADS * RET_QK_DIM, D_MODEL), BF16),
                        pltpu.VMEM((POOL_WIDTH, D_MODEL), BF16),
                        pltpu.VMEM((mem_len, 2 * XA_WIDTH), BF16)],
        compiler_params=pltpu.CompilerParams(dimension_semantics=("arbitrary", "arbitrary"),
                                             vmem_limit_bytes=VMEM_LIMIT),
        name="mixer",
    )(xf, cos_t, sin_t, memf, *consts)


META_LANES = LANES


def _positions_kernel(eid_ref, tri_ref, low_ref, pos_ref, first_tile_ref, n_tiles_ref, *, n_tok):
    n_chunks = n_tok // RANK_CHUNK
    erow = lax.broadcasted_iota(I32, (N_EXPERTS, RANK_CHUNK), 0)

    def onehot(c):
        sl = slice(c * RANK_CHUNK, (c + 1) * RANK_CHUNK)
        m0 = eid_ref[0:1, sl] == erow
        m1 = eid_ref[1:2, sl] == erow
        return m0, m1, jnp.where(m0, 1.0, 0.0) + jnp.where(m1, 1.0, 0.0)

    counts = jnp.zeros((N_EXPERTS, 1), F32)
    for c in range(n_chunks):
        counts = counts + jnp.sum(onehot(c)[2], axis=1, keepdims=True)
    ptiles = jnp.floor((counts + (MOE_TM - 1)) * (1.0 / MOE_TM))
    ptiles_b = jnp.broadcast_to(ptiles, (N_EXPERTS, LANES)).astype(BF16)
    start = _dot(low_ref[...], ptiles_b)[:, 0:1] * MOE_TM

    pos_ref[...] = jnp.zeros_like(pos_ref)
    carry = start - 1.0
    for c in range(n_chunks):
        sl = slice(c * RANK_CHUNK, (c + 1) * RANK_CHUNK)
        m0, m1, oh = onehot(c)
        rank = _dot(oh.astype(BF16), tri_ref[...]) + carry
        pos_ref[0:1, sl] = jnp.sum(jnp.where(m0, rank, 0.0), axis=0, keepdims=True).astype(I32)
        pos_ref[1:2, sl] = jnp.sum(jnp.where(m1, rank, 0.0), axis=0, keepdims=True).astype(I32)
        carry = carry + jnp.sum(oh, axis=1, keepdims=True)

    first_tile_ref[...] = jnp.broadcast_to(start * (1.0 / MOE_TM), first_tile_ref.shape).astype(I32)
    n_tiles_ref[...] = jnp.broadcast_to(ptiles, n_tiles_ref.shape).astype(I32)


def _positions(eid):
    T = eid.shape[1]
    r = np.arange(RANK_CHUNK)
    tri = jnp.asarray(r[:, None] <= r[None, :], BF16)
    e = np.arange(N_EXPERTS)
    low = jnp.asarray(e[None, :] < e[:, None], BF16)
    full = lambda a: pl.BlockSpec(a.shape, lambda i: (0,) * a.ndim)
    return pl.pallas_call(
        functools.partial(_positions_kernel, n_tok=T),
        grid=(1,),
        in_specs=[full(eid), full(tri), full(low)],
        out_specs=[pl.BlockSpec((SUBLANES, T), lambda i: (0, 0)),
                   pl.BlockSpec((N_EXPERTS, META_LANES), lambda i: (0, 0)),
                   pl.BlockSpec((N_EXPERTS, META_LANES), lambda i: (0, 0))],
        out_shape=[jax.ShapeDtypeStruct((SUBLANES, T), I32), jax.ShapeDtypeStruct((N_EXPERTS, META_LANES), I32),
                   jax.ShapeDtypeStruct((N_EXPERTS, META_LANES), I32)],
        name="route_positions",
    )(eid, tri, low)


def _sc_workers():
    info = plsc.get_sparse_core_info()
    return info.num_cores, info.num_cores * info.num_subcores


def _sc_scatter_rows(xp, pos2d, n_out):
    T, W = xp.shape
    n_cores, n_workers = _sc_workers()
    cpw = T // SC_CHUNK // n_workers
    mesh = plsc.VectorSubcoreMesh(core_axis_name="c", subcore_axis_name="s")

    @functools.partial(
        pl.kernel, mesh=mesh, out_type=jax.ShapeDtypeStruct((n_out, W), I32),
        scratch_types=[pltpu.VMEM((TOP_K * cpw, SC_CHUNK), I32), pltpu.VMEM((2, SC_CHUNK, W), I32),
                       pltpu.SemaphoreType.DMA((2,)), pltpu.SemaphoreType.DMA((2,))],
        name="sc_scatter_rows")
    def k(x_hbm, pos_hbm, out_hbm, idx_v, rows_v, rd_sem, wr_sem):
        wid = lax.axis_index("s") * n_cores + lax.axis_index("c")
        for s in range(TOP_K):
            pltpu.sync_copy(pos_hbm.at[pl.ds(s * (T // SC_CHUNK) + wid * cpw, cpw)],
                            idx_v.at[pl.ds(s * cpw, cpw)])

        def read(j):
            return pltpu.make_async_copy(x_hbm.at[pl.ds((wid * cpw + j) * SC_CHUNK, SC_CHUNK)],
                                         rows_v.at[j % 2], rd_sem.at[j % 2])

        def write(j, s):
            return pltpu.make_async_copy(rows_v.at[j % 2], out_hbm.at[idx_v.at[s * cpw + j]], wr_sem.at[j % 2])

        _sc_two_buffer_stream(cpw, read, lambda j: [write(j, s) for s in range(TOP_K)])

    return k(xp, pos2d)


def _sc_two_buffer_stream(n, read, writes):
    read(0).start()
    for j in range(n):
        read(j).wait()
        if j + 1 < n:
            if j >= 1:
                for w in writes(j - 1):
                    w.wait()
            read(j + 1).start()
        for w in writes(j):
            w.start()
    for j in range(max(n - 2, 0), n):
        for w in writes(j):
            w.wait()


def _sc_gather_rows(y, idx2d):
    W = y.shape[1]
    n = idx2d.shape[0] * SC_CHUNK
    n_cores, n_workers = _sc_workers()
    cpw = n // SC_CHUNK // n_workers
    mesh = plsc.VectorSubcoreMesh(core_axis_name="c", subcore_axis_name="s")

    @functools.partial(
        pl.kernel, mesh=mesh, out_type=jax.ShapeDtypeStruct((n, W), I32),
        scratch_types=[pltpu.VMEM((cpw, SC_CHUNK), I32), pltpu.VMEM((2, SC_CHUNK, W), I32),
                       pltpu.SemaphoreType.DMA((2,)), pltpu.SemaphoreType.DMA((2,))],
        name="sc_gather_rows")
    def k(y_hbm, idx_hbm, out_hbm, idx_v, rows_v, rd_sem, wr_sem):
        wid = lax.axis_index("s") * n_cores + lax.axis_index("c")
        pltpu.sync_copy(idx_hbm.at[pl.ds(wid * cpw, cpw)], idx_v)

        def read(j):
            return pltpu.make_async_copy(y_hbm.at[idx_v.at[j]], rows_v.at[j % 2], rd_sem.at[j % 2])

        def write(j):
            return pltpu.make_async_copy(rows_v.at[j % 2], out_hbm.at[pl.ds((wid * cpw + j) * SC_CHUNK, SC_CHUNK)],
                                         wr_sem.at[j % 2])

        _sc_two_buffer_stream(cpw, read, lambda j: [write(j)])

    return k(y, idx2d)


def _routed_kernel(first_ref, count_ref, xs_hbm, wg_ref, wu_ref, wd_ref, ys_hbm, xbuf, ybuf, in_sem, out_sem):
    e = pl.program_id(0)
    last = pl.num_programs(0) - 1
    total = first_ref[last] + count_ref[last]

    def in_copy(g):
        slot = lax.rem(g, RING)
        return pltpu.make_async_copy(xs_hbm.at[pl.ds(g * MOE_TM, MOE_TM)], xbuf.at[slot], in_sem.at[slot])

    def out_copy(g):
        slot = lax.rem(g, RING)
        return pltpu.make_async_copy(ybuf.at[slot], ys_hbm.at[pl.ds(g * MOE_TM, MOE_TM)], out_sem.at[slot])

    @pl.when(e == 0)
    def _():
        for g0 in range(RING - 1):
            @pl.when(g0 < total)
            def _():
                in_copy(g0).start()

    wg = wg_ref[0].astype(BF16)
    wu = wu_ref[0].astype(BF16)
    wd = wd_ref[0].astype(BF16)

    def tile_step(i, carry):
        g = first_ref[e] + i
        slot = lax.rem(g, RING)
        in_copy(g).wait()

        @pl.when(g + RING - 1 < total)
        def _():
            in_copy(g + RING - 1).start()

        @pl.when(g >= RING)
        def _():
            out_copy(g - RING).wait()

        lo, hi = _unpack_halves(xbuf[slot])
        lo = lo.astype(BF16)
        hi = hi.astype(BF16)
        a = _dot(lo, wg[:PACK_W]) + _dot(hi, wg[PACK_W:])
        b = _dot(lo, wu[:PACK_W]) + _dot(hi, wu[PACK_W:])
        act = (a * _sigmoid(a) * b).astype(BF16)
        ybuf[slot] = _pack_halves(_dot(act, wd))
        out_copy(g).start()
        return carry

    lax.fori_loop(0, count_ref[e], tile_step, 0)

    @pl.when(e == last)
    def _():
        for back in range(RING, 0, -1):
            @pl.when(total >= back)
            def _():
                out_copy(total - back).wait()


def _routed_mlp(first_tile, n_tiles, xs, wg, wu, wd):
    R = xs.shape[0]
    any_space = pl.BlockSpec(memory_space=pl.ANY)
    return pl.pallas_call(
        _routed_kernel,
        grid_spec=pltpu.PrefetchScalarGridSpec(
            num_scalar_prefetch=2,
            grid=(N_EXPERTS,),
            in_specs=[any_space,
                      pl.BlockSpec((1, D_MODEL, D_EXPERT), lambda e, ft, nt: (e, 0, 0)),
                      pl.BlockSpec((1, D_MODEL, D_EXPERT), lambda e, ft, nt: (e, 0, 0)),
                      pl.BlockSpec((1, D_EXPERT, D_MODEL), lambda e, ft, nt: (e, 0, 0))],
            out_specs=any_space,
            scratch_shapes=[pltpu.VMEM((RING, MOE_TM, PACK_W), I32), pltpu.VMEM((RING, MOE_TM, PACK_W), I32),
                            pltpu.SemaphoreType.DMA((RING,)), pltpu.SemaphoreType.DMA((RING,))]),
        out_shape=jax.ShapeDtypeStruct((R, PACK_W), I32),
        compiler_params=pltpu.CompilerParams(dimension_semantics=("arbitrary",)),
        name="routed_mlp",
    )(first_tile, n_tiles, xs, wg, wu, wd)


def _combine_kernel(xp_ref, y0_ref, y1_ref, route_ref, lnw_ref, lnb_ref, *out_refs):
    o_ref = out_refs[-1]
    w_rows = lax.bitcast_convert_type(route_ref[...], F32)
    w_cols = jnp.transpose(jnp.concatenate([w_rows] * (LANES // SUBLANES), axis=0))
    w0 = w_cols[:, TOP_K:TOP_K + 1]
    w1 = w_cols[:, TOP_K + 1:TOP_K + 2]
    xlo, xhi = _unpack_halves(xp_ref[...])
    y0lo, y0hi = _unpack_halves(y0_ref[...])
    y1lo, y1hi = _unpack_halves(y1_ref[...])
    h = jnp.concatenate([ALPHA * xlo + (w0 * y0lo + w1 * y1lo), ALPHA * xhi + (w0 * y0hi + w1 * y1hi)], axis=1)
    o_ref[...] = _layer_norm(h, lnw_ref[...], lnb_ref[...])


def _combine_ln2(xp, yg_parts, route, lnw, lnb, tile=1024):
    T = xp.shape[0]
    n_parts = len(yg_parts)
    nt = T // tile // n_parts
    full = lambda a: pl.BlockSpec(a.shape, lambda i: (0,) * a.ndim)
    out = None
    for p, yg in enumerate(yg_parts):
        rows = lambda w, p=p: pl.BlockSpec((tile, w), lambda i: (i + p * nt, 0))
        in_specs = [rows(PACK_W),
                    pl.BlockSpec((tile, PACK_W), lambda i: (i, 0)),
                    pl.BlockSpec((tile, PACK_W), lambda i: (i + nt, 0)),
                    pl.BlockSpec((SUBLANES, tile), lambda i, p=p: (0, i + p * nt)), full(lnw), full(lnb)]
        args = [xp, yg, yg, route, lnw, lnb]
        aliases = {}
        if out is not None:
            in_specs.append(pl.BlockSpec(memory_space=pl.ANY))
            args.append(out)
            aliases = {len(args) - 1: 0}
        out = pl.pallas_call(
            _combine_kernel,
            grid=(nt,),
            in_specs=in_specs,
            out_specs=rows(D_MODEL),
            out_shape=jax.ShapeDtypeStruct((T, D_MODEL), F32),
            input_output_aliases=aliases,
            compiler_params=pltpu.CompilerParams(dimension_semantics=("arbitrary",), vmem_limit_bytes=VMEM_LIMIT),
            name="combine_ln2",
        )(*args)
    return out


def _router_params(w_grp, b_grp, w_exp, b_exp):
    tail = ROUTER_ROWS - EXP_ROW0 - N_EXPERTS
    rw = jnp.concatenate([w_grp, jnp.zeros((D_MODEL, EXP_ROW0 - N_GROUPS), F32),
                          w_exp, jnp.zeros((D_MODEL, tail), F32)], axis=1)
    rb = jnp.concatenate([b_grp, jnp.full((EXP_ROW0 - N_GROUPS,), NEG_BIG, F32), b_exp, jnp.zeros((tail,), F32)])
    rw_hi = rw.astype(BF16)
    rw_lo = (rw - rw_hi.astype(F32)).astype(BF16)
    return jnp.concatenate([rw_hi, rw_lo], axis=1), rb[:, None]


def kernel(x, mem, positions, w_in, w_pool_grp, pool_scale, ret_gn_w, w_mem_kv, w_br_pool, w_br_ret, w_br_xa,
           w_out, ln1_w, ln1_b, w_grp_router, b_grp_router, w_exp_router, b_exp_router, w_exp_gate, w_exp_up,
           w_exp_down, ln2_w, ln2_b):
    B, S, D = x.shape
    assert D == D_MODEL and w_in.shape[0] == DEPTH and S % 512 == 0
    T = B * S
    M = mem.shape[1]
    l = 0
    xf = x.reshape(T, D)

    rope = _rope_table(positions.reshape(1, T))
    rw, rb = _router_params(w_grp_router[l], b_grp_router[l], w_exp_router[l], b_exp_router[l])
    xp, route = _mixer(xf, rope, mem.reshape(B * M, D), w_mem_kv[l].astype(BF16), w_in[l].astype(BF16),
                       w_pool_grp[l], pool_scale[l][None, :], ret_gn_w[l].reshape(1, -1),
                       w_br_pool[l].astype(BF16), w_br_ret[l].astype(BF16), w_br_xa[l].astype(BF16),
                       w_out[l].astype(BF16), ln1_w[l][None, :], ln1_b[l][None, :], rw, rb, B, S, M)

    pos, first_tile, n_tiles = _positions(route)
    pos2d = pos[0:TOP_K].reshape(TOP_K * T // SC_CHUNK, SC_CHUNK)
    max_tiles = (TOP_K * T + N_EXPERTS * (MOE_TM - 1)) // MOE_TM
    xs = _sc_scatter_rows(xp, pos2d, max_tiles * MOE_TM)
    ys = _routed_mlp(first_tile[:, 0], n_tiles[:, 0], xs,
                     w_exp_gate[l].reshape(N_EXPERTS, D_MODEL, D_EXPERT),
                     w_exp_up[l].reshape(N_EXPERTS, D_MODEL, D_EXPERT),
                     w_exp_down[l].reshape(N_EXPERTS, D_EXPERT, D_MODEL))
    rng = T // COMBINE_PARTS
    yg_parts = [_sc_gather_rows(ys, pos[0:TOP_K, p * rng:(p + 1) * rng].reshape(TOP_K * rng // SC_CHUNK, SC_CHUNK))
                for p in range(COMBINE_PARTS)]
    out = _combine_ln2(xp, yg_parts, route, ln2_w[l][None, :], ln2_b[l][None, :])
    return out.reshape(B, S, D)
```

```python
import functools

import numpy as np
import jax
import jax.numpy as jnp
from jax import lax
from jax.experimental import pallas as pl
from jax.experimental.pallas import tpu as pltpu
from jax.experimental.pallas import tpu_sc as plsc

F32 = jnp.float32
BF16 = jnp.bfloat16
I32 = jnp.int32
U32 = jnp.uint32

D_MODEL = 1024
POOL_WINDOWS = (2, 4, 8, 16)
POOL_GROUP_DIM = 128
POOL_WIDTH = 512
POOL_HALO = 16
RET_HEADS = 4
RET_QK_DIM = 128
RET_V_DIM = 256
RET_CHUNK = 128
ROPE_BASE = 10000.0
XA_HEADS = 4
XA_HEAD_DIM = 128
XA_WIDTH = 512
N_GROUPS = 4
EXPERTS_PER_GROUP = 8
N_EXPERTS = N_GROUPS * EXPERTS_PER_GROUP
D_EXPERT = 256
LN_EPS = 1e-5
DEPTH = 1
ALPHA = (2.0 * DEPTH) ** 0.25
NEG_BIG = -1e30

COL_POOL, COL_Q, COL_K, COL_V, COL_G, COL_XAQ, COL_GATES = 0, 512, 1024, 1536, 2560, 3584, 4096

V7X_VMEM_BYTES = 64 * 1024 * 1024
VMEM_LIMIT = V7X_VMEM_BYTES * 7 // 8
SUBLANES = 8
LANES = 128

TOP_K = 2
PACK_W = D_MODEL // 2
MOE_TM = 512
SC_CHUNK = 64
RANK_CHUNK = 512
COMBINE_RANGES = (3, 5)
RING = 4
STRIP = 256
LN_ROWS = 32


def _dot(a, b):
    return jnp.dot(a, b, preferred_element_type=F32)


def _dot_nt(a, b):
    return lax.dot_general(a, b, (((1,), (1,)), ((), ())), preferred_element_type=F32)


def _sigmoid(z):
    return 1.0 / (1.0 + jnp.exp2(z * (-1.0 / np.log(2.0))))


def _layer_norm(h, w, b):
    mu = jnp.mean(h, axis=-1, keepdims=True)
    hc = h - mu
    var = jnp.mean(hc * hc, axis=-1, keepdims=True)
    return hc * lax.rsqrt(var + LN_EPS) * w + b


ROPE_LO = 64
ROPE_PARTS = 3


def _rope_kernel(pos_ref, freq_ref, tab_ref, cos_t_ref, sin_t_ref):
    pos = pos_ref[...]

    def emit(cos_t, sin_t):
        cos_t_ref[...] = cos_t
        sin_t_ref[...] = sin_t

    in_table = jnp.logical_and(jnp.min(pos) >= 0, jnp.max(pos) < ROPE_LO * LANES)

    @pl.when(in_table)
    def _():
        idx = lax.broadcasted_iota(I32, (LANES, pos.shape[1]), 0)
        pick_hi = jnp.where(idx == jnp.right_shift(pos, ROPE_LO.bit_length() - 1), 1.0, 0.0).astype(BF16)
        pick_lo = jnp.where(idx == (pos & (ROPE_LO - 1)), 1.0, 0.0).astype(BF16)

        def look(k, pick):
            return sum(_dot(tab_ref[k * ROPE_PARTS + p], pick) for p in range(ROPE_PARTS))

        cos_a, sin_a, cos_b, sin_b = look(0, pick_hi), look(1, pick_hi), look(2, pick_lo), look(3, pick_lo)
        emit(cos_a * cos_b - sin_a * sin_b, sin_a * cos_b + cos_a * sin_b)

    @pl.when(jnp.logical_not(in_table))
    def _():
        ang = freq_ref[...] * pos.astype(F32)
        emit(jnp.cos(ang), jnp.sin(ang))


def _rope_tables(inv_freq):
    f = inv_freq.astype(np.float64)[:, None]
    idx = np.arange(LANES, dtype=np.float64)[None, :]
    tabs = [np.cos(ROPE_LO * idx * f), np.sin(ROPE_LO * idx * f), np.cos(idx * f), np.sin(idx * f)]
    pieces = []
    for t in tabs:
        rest = t.astype(np.float32)
        for _ in range(ROPE_PARTS):
            piece = rest.astype(BF16)
            pieces.append(piece)
            rest = rest - piece.astype(np.float32)
    return jnp.asarray(np.stack(pieces))


def _rope_table(pos_row, tile=2048):
    T = pos_row.shape[1]
    half = RET_QK_DIM // 2
    inv_freq = (ROPE_BASE ** (-np.arange(half, dtype=np.float64) / half)).astype(np.float32)
    freq = jnp.asarray(inv_freq[:, None])
    tabs = _rope_tables(inv_freq)
    out_t = pl.BlockSpec((half, tile), lambda i: (0, i))
    return pl.pallas_call(
        _rope_kernel,
        grid=(T // tile,),
        in_specs=[pl.BlockSpec((1, tile), lambda i: (0, i)), pl.BlockSpec((half, 1), lambda i: (0, 0)),
                  pl.BlockSpec(tabs.shape, lambda i: (0, 0, 0))],
        out_specs=[out_t, out_t],
        out_shape=[jax.ShapeDtypeStruct((half, T), F32)] * 2,
        name="rope_table",
    )(pos_row, freq, tabs)


POOL_SUB = 256


def _pool_bands():
    r = np.arange(POOL_SUB)[:, None]
    c = np.arange(POOL_SUB)[None, :]
    ch = np.arange(POOL_HALO)[None, :] - POOL_HALO
    main = np.stack([((r - c >= 0) & (r - c < w)) for w in POOL_WINDOWS]).astype(np.float32)
    halo = np.stack([((r - ch >= 0) & (r - ch < w)) for w in POOL_WINDOWS]).astype(np.float32)
    return jnp.asarray(main, BF16), jnp.asarray(halo, BF16)


def _pool_branch(ub, j, halo_ref, bmain_ref, bhalo_ref, o_ref, filler):
    tile = ub.shape[0]
    s0 = j * tile
    slot = lax.rem(j, 2)
    blocks = [(sb * POOL_SUB, g) for sb in range(tile // POOL_SUB) for g in range(len(POOL_WINDOWS))]
    wsum = {}
    for r0, g in blocks:
        cols = slice(g * POOL_GROUP_DIM, (g + 1) * POOL_GROUP_DIM)
        prev = halo_ref[slot] if r0 == 0 else ub[r0 - POOL_HALO:r0]
        wsum[r0, g] = _dot(bmain_ref[g], ub[r0:r0 + POOL_SUB, cols]) + _dot(bhalo_ref[g], prev[:, cols])
    filler()
    for r0, g in blocks:
        cols = slice(g * POOL_GROUP_DIM, (g + 1) * POOL_GROUP_DIM)
        pos = s0 + r0 + lax.broadcasted_iota(I32, (POOL_SUB, POOL_GROUP_DIM), 0)
        cnt = jnp.minimum(pos + 1, POOL_WINDOWS[g]).astype(F32)
        pooled = wsum[r0, g] / cnt - ub[r0:r0 + POOL_SUB, cols].astype(F32)
        o_ref[r0:r0 + POOL_SUB, cols] = pooled.astype(BF16)
    halo_ref[1 - slot] = ub[tile - POOL_HALO:tile]


def _ret_consts():
    h = np.arange(RET_HEADS, dtype=np.float64)
    log_gamma = np.log1p(-np.exp2(-5.0 - h))
    pos = np.arange(RET_CHUNK, dtype=np.float64)
    diff = pos[:, None] - pos[None, :]
    kscale = RET_QK_DIM ** -0.5
    dmask = kscale * np.where(diff >= 0, np.exp(log_gamma[:, None, None] * np.maximum(diff, 0.0)), 0.0)
    qdec = np.exp(log_gamma[:, None] * (pos + 1.0)[None, :])
    kdec = kscale * np.exp(log_gamma[:, None] * (RET_CHUNK - 1.0 - pos)[None, :])
    cdec = np.exp(log_gamma * RET_CHUNK)
    lanes = lambda a: np.broadcast_to(a[:, :, None], (RET_HEADS, RET_CHUNK, RET_QK_DIM))
    kdec_t = np.broadcast_to(kdec[:, None, :], (RET_HEADS, RET_QK_DIM, RET_CHUNK))
    return (jnp.asarray(dmask, F32), jnp.asarray(lanes(qdec), F32), jnp.asarray(kdec_t, F32),
            tuple(float(v) for v in cdec))


def _retention_branch(q, k_t, v, silu_g, cos_t_ref, sin_t_ref, dmask_ref, qdec_ref, kdec_ref,
                      gnw_ref, state_ref, rq_ref, rqd_ref, rkt_ref, rkdt_ref, o_ref, cdec, fillers):
    tile = q.shape[0]
    n_chunks = tile // RET_CHUNK
    half = RET_QK_DIM // 2
    cos_t = cos_t_ref[...]
    sin_t = sin_t_ref[...]
    cos = jnp.transpose(jnp.concatenate([cos_t, cos_t], axis=0))
    sin = jnp.transpose(jnp.concatenate([-sin_t, sin_t], axis=0))
    for h in range(RET_HEADS):
        qk = slice(h * RET_QK_DIM, (h + 1) * RET_QK_DIM)
        qh = q[:, qk]
        qr = qh * cos + pltpu.roll(qh, half, 1) * sin
        rq_ref[:, qk] = qr.astype(BF16)
        rqd_ref[:, qk] = (qr * jnp.concatenate([qdec_ref[h]] * n_chunks, axis=0)).astype(BF16)
        k1 = k_t[h * RET_QK_DIM:h * RET_QK_DIM + half]
        k2 = k_t[h * RET_QK_DIM + half:(h + 1) * RET_QK_DIM]
        kr_t = jnp.concatenate([k1 * cos_t - k2 * sin_t, k2 * cos_t + k1 * sin_t], axis=0)
        rkt_ref[qk, :] = kr_t.astype(BF16)
        rkdt_ref[qk, :] = (kr_t * jnp.concatenate([kdec_ref[h]] * n_chunks, axis=1)).astype(BF16)

    chunks = [slice(c * RET_CHUNK, (c + 1) * RET_CHUNK) for c in range(n_chunks)]
    heads = [slice(h * RET_QK_DIM, (h + 1) * RET_QK_DIM) for h in range(RET_HEADS)]
    v_heads = [slice(h * RET_V_DIM, (h + 1) * RET_V_DIM) for h in range(RET_HEADS)]
    raw = {(c, h): _dot(rq_ref[chunks[c], heads[h]], rkt_ref[heads[h], chunks[c]])
           for c in range(n_chunks) for h in range(RET_HEADS)}
    incr = {(c, h): _dot(rkdt_ref[heads[h], chunks[c]], v[chunks[c], v_heads[h]])
            for c in range(n_chunks) for h in range(RET_HEADS)}
    state_in = {}
    for h in range(RET_HEADS):
        st = state_ref[h]
        for c in range(n_chunks):
            state_in[c, h] = st.astype(BF16)
            st = cdec[h] * st + incr[c, h]
        state_ref[h] = st

    for c in range(n_chunks):
        rows = chunks[c]
        fillers[c]()
        for h in range(RET_HEADS):
            qk = heads[h]
            v_cols = v_heads[h]
            scores = raw[c, h] * dmask_ref[h]
            lhs = jnp.concatenate([scores.astype(BF16), rqd_ref[rows, qk]], axis=1)
            y = _dot(lhs, jnp.concatenate([v[rows, v_cols], state_in[c, h]], axis=0))
            mu = jnp.mean(y, axis=-1, keepdims=True)
            yc = y - mu
            var = jnp.mean(yc * yc, axis=-1, keepdims=True)
            yn = yc * lax.rsqrt(var + LN_EPS) * gnw_ref[:, v_cols]
            o_ref[rows, v_cols] = (silu_g[h][rows] * yn).astype(BF16)


def _cross_attention_branch(xq, kv_ref, o_ref):
    scale = XA_HEAD_DIM ** -0.5
    for h in range(XA_HEADS):
        cols = slice(h * XA_HEAD_DIM, (h + 1) * XA_HEAD_DIM)
        v_cols = slice(XA_WIDTH + h * XA_HEAD_DIM, XA_WIDTH + (h + 1) * XA_HEAD_DIM)
        s = _dot_nt(xq[:, cols], kv_ref[:, cols]) * scale
        m = jnp.max(s, axis=-1, keepdims=True)
        p = jnp.exp(s - m)
        l = jnp.sum(p, axis=-1, keepdims=True)
        o = _dot(p.astype(BF16), kv_ref[:, v_cols]) / l
        o_ref[:, cols] = o.astype(BF16)


ROUTER_ROWS = 128
EXP_ROW0 = 8


def _route(logits_t):
    gl = logits_t[0:8]
    gmax = jnp.max(gl, axis=0, keepdims=True)
    p_grp = 1.0 / jnp.sum(jnp.exp(gl - gmax), axis=0, keepdims=True)
    idx8 = lax.broadcasted_iota(jnp.int32, gl.shape, 0)
    gsel = jnp.min(jnp.where(gl == gmax, idx8, 8), axis=0, keepdims=True)
    cl = jnp.zeros_like(gl)
    for g in range(N_GROUPS):
        r0 = EXP_ROW0 + g * EXPERTS_PER_GROUP
        cl = cl + jnp.where(gsel == g, logits_t[r0:r0 + EXPERTS_PER_GROUP], 0.0)
    v1 = jnp.max(cl, axis=0, keepdims=True)
    i1 = jnp.min(jnp.where(cl == v1, idx8, 8), axis=0, keepdims=True)
    cl2 = jnp.where(idx8 == i1, -jnp.inf, cl)
    v2 = jnp.max(cl2, axis=0, keepdims=True)
    i2 = jnp.min(jnp.where(cl2 == v2, idx8, 8), axis=0, keepdims=True)
    e21 = jnp.exp(v2 - v1)
    w1 = p_grp / (1.0 + e21)
    w2 = p_grp * e21 / (1.0 + e21)
    return gsel * EXPERTS_PER_GROUP + i1, gsel * EXPERTS_PER_GROUP + i2, w1, w2


def _pack_halves(v):
    half = v.shape[1] // 2
    lo = lax.bitcast_convert_type(v[:, :half].astype(BF16).astype(F32), U32)
    hi = lax.bitcast_convert_type(v[:, half:].astype(BF16).astype(F32), U32)
    return lax.bitcast_convert_type(lax.shift_right_logical(lo, U32(16)) | hi, I32)


def _unpack_halves(w):
    u = lax.bitcast_convert_type(w, U32)
    lo = lax.bitcast_convert_type(lax.shift_left(u, U32(16)), F32)
    hi = lax.bitcast_convert_type(u & U32(0xFFFF0000), F32)
    return lo, hi


def _mixer_kernel(x_ref, cos_t_ref, sin_t_ref, mem_ref, wmem_ref, win_ref, wgrp_ref,
                  pscale_ref, bmain_ref, bhalo_ref, dmask_ref, qdec_ref, kdec_ref, gnw_ref, wp_ref, wr_ref,
                  wa_ref, wo_ref, lnw_ref, lnb_ref, rw_ref, rb_ref, xp_ref, route_ref,
                  state_ref, halo_ref, ypool_ref, yret_ref, yxa_ref, rq_ref, rqd_ref, rkt_ref, rkdt_ref, wkt_ref,
                  wpool_ref, kv_ref, *, tile, cdec):
    j = pl.program_id(1)

    @pl.when(jnp.logical_and(pl.program_id(0) == 0, j == 0))
    def _():
        wk = win_ref[:, COL_K:COL_K + RET_HEADS * RET_QK_DIM].astype(F32)
        wkt_ref[...] = jnp.transpose(wk).astype(BF16)
        for g in range(len(POOL_WINDOWS)):
            rows = slice(g * POOL_GROUP_DIM, (g + 1) * POOL_GROUP_DIM)
            wg = (wgrp_ref[g].astype(F32) * pscale_ref[:, rows]).astype(BF16)
            wpool_ref[rows, :] = _dot(wg, wp_ref[rows, :]).astype(BF16)

    @pl.when(j == 0)
    def _():
        state_ref[...] = jnp.zeros_like(state_ref)
        halo_ref[...] = jnp.zeros_like(halo_ref)
        kv_ref[...] = _dot(mem_ref[...].astype(BF16), wmem_ref[...]).astype(BF16)

    x = x_ref[...]
    xb = x.astype(BF16)

    def proj(col, width):
        return _dot(xb, win_ref[:, col:col + width])

    part = {}
    strips = [slice(c, c + STRIP) for c in range(0, D_MODEL, STRIP)]

    def gate(branch, cols):
        return _sigmoid(proj(COL_GATES + branch * D_MODEL + cols.start, STRIP))

    def pool_part():
        def pool_gates():
            part["pool_gate"] = [gate(0, c) for c in strips]

        _pool_branch(proj(COL_POOL, POOL_WIDTH).astype(BF16), j, halo_ref, bmain_ref, bhalo_ref, ypool_ref,
                     pool_gates)
        part["pool"] = [part["pool_gate"][i] * _dot(ypool_ref[...], wpool_ref[:, c]) for i, c in enumerate(strips)]

    def xa_part():
        _cross_attention_branch(proj(COL_XAQ, XA_WIDTH).astype(BF16), kv_ref, yxa_ref)
        part["xa"] = [gate(2, c) * _dot(yxa_ref[...], wa_ref[:, c]) for c in strips]

    def ret_gate_part():
        part["ret_gate"] = [gate(1, c) for c in strips]

    silu_g = []
    for h in range(RET_HEADS):
        gh = proj(COL_G + h * RET_V_DIM, RET_V_DIM)
        silu_g.append(gh * _sigmoid(gh))
    fillers = [pool_part, xa_part, ret_gate_part] + [lambda: None] * (tile // RET_CHUNK - 3)
    _retention_branch(proj(COL_Q, RET_HEADS * RET_QK_DIM), _dot_nt(wkt_ref[...], xb),
                      proj(COL_V, RET_HEADS * RET_V_DIM).astype(BF16), silu_g, cos_t_ref, sin_t_ref, dmask_ref,
                      qdec_ref, kdec_ref, gnw_ref, state_ref, rq_ref, rqd_ref, rkt_ref, rkdt_ref, yret_ref, cdec,
                      fillers)
    merged = jnp.concatenate(
        [(part["pool"][i] + part["ret_gate"][i] * _dot(yret_ref[...], wr_ref[:, c]) + part["xa"][i]).astype(BF16)
         for i, c in enumerate(strips)], axis=1)
    h = jnp.concatenate([ALPHA * x[:, c] + _dot(merged, wo_ref[:, c]) for c in strips], axis=1)
    x1 = jnp.concatenate([_layer_norm(h[r:r + LN_ROWS], lnw_ref[...], lnb_ref[...])
                          for r in range(0, tile, LN_ROWS)], axis=0)
    xp_ref[...] = _pack_halves(x1)
    x1_hi = x1.astype(BF16)
    x1_lo = (x1 - x1_hi.astype(F32)).astype(BF16)
    p4 = _dot(jnp.concatenate([x1_hi, x1_lo], axis=0), rw_ref[...])
    logits = (p4[:tile, :ROUTER_ROWS] + p4[:tile, ROUTER_ROWS:]) + (p4[tile:, :ROUTER_ROWS] + p4[tile:, ROUTER_ROWS:])
    e0, e1, w0, w1 = _route(jnp.transpose(logits) + rb_ref[...])
    w_bits = [lax.bitcast_convert_type(w, I32) for w in (w0, w1)]
    route_ref[...] = jnp.concatenate([e0, e1, *w_bits, jnp.zeros((SUBLANES - 2 * TOP_K, tile), I32)], axis=0)


def _mixer(xf, rope, memf, wmem, win, wgrp, pscale, gnw, wp, wr, wa, wo, lnw, lnb, rw, rb, batch, seq, mem_len,
           tile=512):
    T = xf.shape[0]
    nj = seq // tile
    cos_t, sin_t = rope
    bmain, bhalo = _pool_bands()
    dmask, qdec, kdec, cdec = _ret_consts()
    resident = lambda a: pl.BlockSpec(a.shape, lambda b, j: (0,) * a.ndim, pipeline_mode=pl.Buffered(1))
    rowblk = lambda w: pl.BlockSpec((tile, w), lambda b, j: (b * nj + j, 0))
    colblk = lambda r: pl.BlockSpec((r, tile), lambda b, j: (0, b * nj + j))
    consts = (wmem, win, wgrp, pscale, bmain, bhalo, dmask, qdec, kdec, gnw, wp, wr, wa, wo, lnw, lnb, rw, rb)
    return pl.pallas_call(
        functools.partial(_mixer_kernel, tile=tile, cdec=cdec),
        grid=(batch, nj),
        in_specs=[rowblk(D_MODEL), colblk(RET_QK_DIM // 2), colblk(RET_QK_DIM // 2),
                  pl.BlockSpec((mem_len, D_MODEL), lambda b, j: (b, 0))] + [resident(a) for a in consts],
        out_specs=[rowblk(PACK_W), colblk(SUBLANES)],
        out_shape=[jax.ShapeDtypeStruct((T, PACK_W), I32), jax.ShapeDtypeStruct((SUBLANES, T), I32)],
        scratch_shapes=[pltpu.VMEM((RET_HEADS, RET_QK_DIM, RET_V_DIM), F32),
                        pltpu.VMEM((2, POOL_HALO, POOL_WIDTH), BF16),
                        pltpu.VMEM((tile, POOL_WIDTH), BF16),
                        pltpu.VMEM((tile, RET_HEADS * RET_V_DIM), BF16),
                        pltpu.VMEM((tile, XA_WIDTH), BF16),
                        pltpu.VMEM((tile, RET_HEADS * RET_QK_DIM), BF16),
                        pltpu.VMEM((tile, RET_HEADS * RET_QK_DIM), BF16),
                        pltpu.VMEM((RET_HEADS * RET_QK_DIM, tile), BF16),
                        pltpu.VMEM((RET_HEADS * RET_QK_DIM, tile), BF16),
                        pltpu.VMEM((RET_HEADS * RET_QK_DIM, D_MODEL), BF16),
                        pltpu.VMEM((POOL_WIDTH, D_MODEL), BF16),
                        pltpu.VMEM((mem_len, 2 * XA_WIDTH), BF16)],
        compiler_params=pltpu.CompilerParams(dimension_semantics=("arbitrary", "arbitrary"),
                                             vmem_limit_bytes=VMEM_LIMIT),
        name="mixer",
    )(xf, cos_t, sin_t, memf, *consts)


META_LANES = LANES


def _positions_kernel(eid_ref, tri_ref, low_ref, pos_ref, first_tile_ref, n_tiles_ref, *, n_tok):
    n_chunks = n_tok // RANK_CHUNK
    erow = lax.broadcasted_iota(I32, (N_EXPERTS, RANK_CHUNK), 0)

    def onehot(c):
        sl = slice(c * RANK_CHUNK, (c + 1) * RANK_CHUNK)
        m0 = eid_ref[0:1, sl] == erow
        m1 = eid_ref[1:2, sl] == erow
        return m0, m1, jnp.where(m0, 1.0, 0.0) + jnp.where(m1, 1.0, 0.0)

    counts = jnp.zeros((N_EXPERTS, 1), F32)
    for c in range(n_chunks):
        counts = counts + jnp.sum(onehot(c)[2], axis=1, keepdims=True)
    ptiles = jnp.floor((counts + (MOE_TM - 1)) * (1.0 / MOE_TM))
    ptiles_b = jnp.broadcast_to(ptiles, (N_EXPERTS, LANES)).astype(BF16)
    start = _dot(low_ref[...], ptiles_b)[:, 0:1] * MOE_TM

    pos_ref[...] = jnp.zeros_like(pos_ref)
    carry = start - 1.0
    for c in range(n_chunks):
        sl = slice(c * RANK_CHUNK, (c + 1) * RANK_CHUNK)
        m0, m1, oh = onehot(c)
        rank = _dot(oh.astype(BF16), tri_ref[...]) + carry
        pos_ref[0:1, sl] = jnp.sum(jnp.where(m0, rank, 0.0), axis=0, keepdims=True).astype(I32)
        pos_ref[1:2, sl] = jnp.sum(jnp.where(m1, rank, 0.0), axis=0, keepdims=True).astype(I32)
        carry = carry + jnp.sum(oh, axis=1, keepdims=True)

    first_tile_ref[...] = jnp.broadcast_to(start * (1.0 / MOE_TM), first_tile_ref.shape).astype(I32)
    n_tiles_ref[...] = jnp.broadcast_to(ptiles, n_tiles_ref.shape).astype(I32)


def _positions(eid):
    T = eid.shape[1]
    r = np.arange(RANK_CHUNK)
    tri = jnp.asarray(r[:, None] <= r[None, :], BF16)
    e = np.arange(N_EXPERTS)
    low = jnp.asarray(e[None, :] < e[:, None], BF16)
    full = lambda a: pl.BlockSpec(a.shape, lambda i: (0,) * a.ndim)
    return pl.pallas_call(
        functools.partial(_positions_kernel, n_tok=T),
        grid=(1,),
        in_specs=[full(eid), full(tri), full(low)],
        out_specs=[pl.BlockSpec((SUBLANES, T), lambda i: (0, 0)),
                   pl.BlockSpec((N_EXPERTS, META_LANES), lambda i: (0, 0)),
                   pl.BlockSpec((N_EXPERTS, META_LANES), lambda i: (0, 0))],
        out_shape=[jax.ShapeDtypeStruct((SUBLANES, T), I32), jax.ShapeDtypeStruct((N_EXPERTS, META_LANES), I32),
                   jax.ShapeDtypeStruct((N_EXPERTS, META_LANES), I32)],
        name="route_positions",
    )(eid, tri, low)


def _sc_workers():
    info = plsc.get_sparse_core_info()
    return info.num_cores, info.num_cores * info.num_subcores


def _sc_scatter_rows(xp, pos2d, n_out):
    T, W = xp.shape
    n_cores, n_workers = _sc_workers()
    cpw = T // SC_CHUNK // n_workers
    mesh = plsc.VectorSubcoreMesh(core_axis_name="c", subcore_axis_name="s")

    @functools.partial(
        pl.kernel, mesh=mesh, out_type=jax.ShapeDtypeStruct((n_out, W), I32),
        scratch_types=[pltpu.VMEM((TOP_K * cpw, SC_CHUNK), I32), pltpu.VMEM((2, SC_CHUNK, W), I32),
                       pltpu.SemaphoreType.DMA((2,)), pltpu.SemaphoreType.DMA((2,))],
        name="sc_scatter_rows")
    def k(x_hbm, pos_hbm, out_hbm, idx_v, rows_v, rd_sem, wr_sem):
        wid = lax.axis_index("s") * n_cores + lax.axis_index("c")
        for s in range(TOP_K):
            pltpu.sync_copy(pos_hbm.at[pl.ds(s * (T // SC_CHUNK) + wid * cpw, cpw)],
                            idx_v.at[pl.ds(s * cpw, cpw)])

        def read(j):
            return pltpu.make_async_copy(x_hbm.at[pl.ds((wid * cpw + j) * SC_CHUNK, SC_CHUNK)],
                                         rows_v.at[j % 2], rd_sem.at[j % 2])

        def write(j, s):
            return pltpu.make_async_copy(rows_v.at[j % 2], out_hbm.at[idx_v.at[s * cpw + j]], wr_sem.at[j % 2])

        _sc_two_buffer_stream(cpw, read, lambda j: [write(j, s) for s in range(TOP_K)])

    return k(xp, pos2d)


def _sc_two_buffer_stream(n, read, writes):
    read(0).start()
    for j in range(n):
        read(j).wait()
        if j + 1 < n:
            if j >= 1:
                for w in writes(j - 1):
                    w.wait()
            read(j + 1).start()
        for w in writes(j):
            w.start()
    for j in range(max(n - 2, 0), n):
        for w in writes(j):
            w.wait()


def _sc_gather_rows(y, idx2d):
    W = y.shape[1]
    n = idx2d.shape[0] * SC_CHUNK
    n_cores, n_workers = _sc_workers()
    cpw = n // SC_CHUNK // n_workers
    cpw_pad = -(-cpw // SUBLANES) * SUBLANES
    mesh = plsc.VectorSubcoreMesh(core_axis_name="c", subcore_axis_name="s")

    @functools.partial(
        pl.kernel, mesh=mesh, out_type=jax.ShapeDtypeStruct((n, W), I32),
        scratch_types=[pltpu.VMEM((cpw_pad, SC_CHUNK), I32), pltpu.VMEM((2, SC_CHUNK, W), I32),
                       pltpu.SemaphoreType.DMA((2,)), pltpu.SemaphoreType.DMA((2,))],
        name="sc_gather_rows")
    def k(y_hbm, idx_hbm, out_hbm, idx_v, rows_v, rd_sem, wr_sem):
        wid = lax.axis_index("s") * n_cores + lax.axis_index("c")
        pltpu.sync_copy(idx_hbm.at[wid], idx_v)

        def read(j):
            return pltpu.make_async_copy(y_hbm.at[idx_v.at[j]], rows_v.at[j % 2], rd_sem.at[j % 2])

        def write(j):
            return pltpu.make_async_copy(rows_v.at[j % 2], out_hbm.at[pl.ds((wid * cpw + j) * SC_CHUNK, SC_CHUNK)],
                                         wr_sem.at[j % 2])

        _sc_two_buffer_stream(cpw, read, lambda j: [write(j)])

    idx3d = jnp.pad(idx2d.reshape(n_workers, cpw, SC_CHUNK), ((0, 0), (0, cpw_pad - cpw), (0, 0)))
    return k(y, idx3d)


def _routed_kernel(first_ref, count_ref, xs_hbm, wg_ref, wu_ref, wd_ref, ys_hbm, xbuf, ybuf, in_sem, out_sem):
    e = pl.program_id(0)
    last = pl.num_programs(0) - 1
    total = first_ref[last] + count_ref[last]

    def in_copy(g):
        slot = lax.rem(g, RING)
        return pltpu.make_async_copy(xs_hbm.at[pl.ds(g * MOE_TM, MOE_TM)], xbuf.at[slot], in_sem.at[slot])

    def out_copy(g):
        slot = lax.rem(g, RING)
        return pltpu.make_async_copy(ybuf.at[slot], ys_hbm.at[pl.ds(g * MOE_TM, MOE_TM)], out_sem.at[slot])

    @pl.when(e == 0)
    def _():
        for g0 in range(RING - 1):
            @pl.when(g0 < total)
            def _():
                in_copy(g0).start()

    wg = wg_ref[0].astype(BF16)
    wu = wu_ref[0].astype(BF16)
    wd = wd_ref[0].astype(BF16)

    def tile_step(i, carry):
        g = first_ref[e] + i
        slot = lax.rem(g, RING)
        in_copy(g).wait()

        @pl.when(g + RING - 1 < total)
        def _():
            in_copy(g + RING - 1).start()

        @pl.when(g >= RING)
        def _():
            out_copy(g - RING).wait()

        lo, hi = _unpack_halves(xbuf[slot])
        lo = lo.astype(BF16)
        hi = hi.astype(BF16)
        a = _dot(lo, wg[:PACK_W]) + _dot(hi, wg[PACK_W:])
        b = _dot(lo, wu[:PACK_W]) + _dot(hi, wu[PACK_W:])
        act = (a * _sigmoid(a) * b).astype(BF16)
        ybuf[slot] = _pack_halves(_dot(act, wd))
        out_copy(g).start()
        return carry

    lax.fori_loop(0, count_ref[e], tile_step, 0)

    @pl.when(e == last)
    def _():
        for back in range(RING, 0, -1):
            @pl.when(total >= back)
            def _():
                out_copy(total - back).wait()


def _routed_mlp(first_tile, n_tiles, xs, wg, wu, wd):
    R = xs.shape[0]
    any_space = pl.BlockSpec(memory_space=pl.ANY)
    return pl.pallas_call(
        _routed_kernel,
        grid_spec=pltpu.PrefetchScalarGridSpec(
            num_scalar_prefetch=2,
            grid=(N_EXPERTS,),
            in_specs=[any_space,
                      pl.BlockSpec((1, D_MODEL, D_EXPERT), lambda e, ft, nt: (e, 0, 0)),
                      pl.BlockSpec((1, D_MODEL, D_EXPERT), lambda e, ft, nt: (e, 0, 0)),
                      pl.BlockSpec((1, D_EXPERT, D_MODEL), lambda e, ft, nt: (e, 0, 0))],
            out_specs=any_space,
            scratch_shapes=[pltpu.VMEM((RING, MOE_TM, PACK_W), I32), pltpu.VMEM((RING, MOE_TM, PACK_W), I32),
                            pltpu.SemaphoreType.DMA((RING,)), pltpu.SemaphoreType.DMA((RING,))]),
        out_shape=jax.ShapeDtypeStruct((R, PACK_W), I32),
        compiler_params=pltpu.CompilerParams(dimension_semantics=("arbitrary",)),
        name="routed_mlp",
    )(first_tile, n_tiles, xs, wg, wu, wd)


def _combine_kernel(xp_ref, y0_ref, y1_ref, route_ref, lnw_ref, lnb_ref, *out_refs):
    o_ref = out_refs[-1]
    w_rows = lax.bitcast_convert_type(route_ref[...], F32)
    w_cols = jnp.transpose(jnp.concatenate([w_rows] * (LANES // SUBLANES), axis=0))
    w0 = w_cols[:, TOP_K:TOP_K + 1]
    w1 = w_cols[:, TOP_K + 1:TOP_K + 2]
    xlo, xhi = _unpack_halves(xp_ref[...])
    y0lo, y0hi = _unpack_halves(y0_ref[...])
    y1lo, y1hi = _unpack_halves(y1_ref[...])
    h = jnp.concatenate([ALPHA * xlo + (w0 * y0lo + w1 * y1lo), ALPHA * xhi + (w0 * y0hi + w1 * y1hi)], axis=1)
    o_ref[...] = _layer_norm(h, lnw_ref[...], lnb_ref[...])


def _combine_ln2(xp, yg_parts, route, lnw, lnb, tile=1024):
    T = xp.shape[0]
    full = lambda a: pl.BlockSpec(a.shape, lambda i: (0,) * a.ndim)
    out = None
    b0 = 0
    for yg in yg_parts:
        nt = yg.shape[0] // TOP_K // tile
        rows = lambda w, b0=b0: pl.BlockSpec((tile, w), lambda i: (i + b0, 0))
        in_specs = [rows(PACK_W),
                    pl.BlockSpec((tile, PACK_W), lambda i: (i, 0)),
                    pl.BlockSpec((tile, PACK_W), lambda i, nt=nt: (i + nt, 0)),
                    pl.BlockSpec((SUBLANES, tile), lambda i, b0=b0: (0, i + b0)), full(lnw), full(lnb)]
        args = [xp, yg, yg, route, lnw, lnb]
        aliases = {}
        if out is not None:
            in_specs.append(pl.BlockSpec(memory_space=pl.ANY))
            args.append(out)
            aliases = {len(args) - 1: 0}
        out = pl.pallas_call(
            _combine_kernel,
            grid=(nt,),
            in_specs=in_specs,
            out_specs=rows(D_MODEL),
            out_shape=jax.ShapeDtypeStruct((T, D_MODEL), F32),
            input_output_aliases=aliases,
            compiler_params=pltpu.CompilerParams(dimension_semantics=("arbitrary",), vmem_limit_bytes=VMEM_LIMIT),
            name="combine_ln2",
        )(*args)
        b0 += nt
    return out


def _router_params(w_grp, b_grp, w_exp, b_exp):
    rw = jnp.zeros((D_MODEL, ROUTER_ROWS), F32)
    rw = rw.at[:, 0:N_GROUPS].set(w_grp).at[:, EXP_ROW0:EXP_ROW0 + N_EXPERTS].set(w_exp)
    rb = jnp.zeros((ROUTER_ROWS,), F32).at[N_GROUPS:8].set(NEG_BIG)
    rb = rb.at[0:N_GROUPS].set(b_grp).at[EXP_ROW0:EXP_ROW0 + N_EXPERTS].set(b_exp)
    rw_hi = rw.astype(BF16)
    rw_lo = (rw - rw_hi.astype(F32)).astype(BF16)
    return jnp.concatenate([rw_hi, rw_lo], axis=1), rb[:, None]


def kernel(x, mem, positions, w_in, w_pool_grp, pool_scale, ret_gn_w, w_mem_kv, w_br_pool, w_br_ret, w_br_xa,
           w_out, ln1_w, ln1_b, w_grp_router, b_grp_router, w_exp_router, b_exp_router, w_exp_gate, w_exp_up,
           w_exp_down, ln2_w, ln2_b):
    B, S, D = x.shape
    assert D == D_MODEL and w_in.shape[0] == DEPTH and S % 512 == 0
    T = B * S
    M = mem.shape[1]
    l = 0
    xf = x.reshape(T, D)

    rope = _rope_table(positions.reshape(1, T))
    rw, rb = _router_params(w_grp_router[l], b_grp_router[l], w_exp_router[l], b_exp_router[l])
    xp, route = _mixer(xf, rope, mem.reshape(B * M, D), w_mem_kv[l].astype(BF16), w_in[l].astype(BF16),
                       w_pool_grp[l], pool_scale[l][None, :], ret_gn_w[l].reshape(1, -1),
                       w_br_pool[l].astype(BF16), w_br_ret[l].astype(BF16), w_br_xa[l].astype(BF16),
                       w_out[l].astype(BF16), ln1_w[l][None, :], ln1_b[l][None, :], rw, rb, B, S, M)

    pos, first_tile, n_tiles = _positions(route)
    pos2d = pos[0:TOP_K].reshape(TOP_K * T // SC_CHUNK, SC_CHUNK)
    max_tiles = (TOP_K * T + N_EXPERTS * (MOE_TM - 1)) // MOE_TM
    xs = _sc_scatter_rows(xp, pos2d, max_tiles * MOE_TM)
    ys = _routed_mlp(first_tile[:, 0], n_tiles[:, 0], xs,
                     w_exp_gate[l].reshape(N_EXPERTS, D_MODEL, D_EXPERT),
                     w_exp_up[l].reshape(N_EXPERTS, D_MODEL, D_EXPERT),
                     w_exp_down[l].reshape(N_EXPERTS, D_EXPERT, D_MODEL))
    unit = T // sum(COMBINE_RANGES)
    bounds = [unit * sum(COMBINE_RANGES[:p]) for p in range(len(COMBINE_RANGES) + 1)]
    yg_parts = [_sc_gather_rows(ys, pos[0:TOP_K, a:b].reshape(TOP_K * (b - a) // SC_CHUNK, SC_CHUNK))
                for a, b in zip(bounds[:-1], bounds[1:])]
    out = _combine_ln2(xp, yg_parts, route, ln2_w[l][None, :], ln2_b[l][None, :])
    return out.reshape(B, S, D)
```

```python
import functools

import numpy as np
import jax
import jax.numpy as jnp
from jax import lax
from jax.experimental import pallas as pl
from jax.experimental.pallas import tpu as pltpu
from jax.experimental.pallas import tpu_sc as plsc

F32 = jnp.float32
BF16 = jnp.bfloat16
I32 = jnp.int32
U32 = jnp.uint32

D_MODEL = 1024
POOL_WINDOWS = (2, 4, 8, 16)
POOL_GROUP_DIM = 128
POOL_WIDTH = 512
POOL_HALO = 16
RET_HEADS = 4
RET_QK_DIM = 128
RET_V_DIM = 256
RET_CHUNK = 128
ROPE_BASE = 10000.0
XA_HEADS = 4
XA_HEAD_DIM = 128
XA_WIDTH = 512
N_GROUPS = 4
EXPERTS_PER_GROUP = 8
N_EXPERTS = N_GROUPS * EXPERTS_PER_GROUP
D_EXPERT = 256
LN_EPS = 1e-5
DEPTH = 1
ALPHA = (2.0 * DEPTH) ** 0.25
NEG_BIG = -1e30

COL_POOL, COL_Q, COL_K, COL_V, COL_G, COL_XAQ, COL_GATES = 0, 512, 1024, 1536, 2560, 3584, 4096

V7X_VMEM_BYTES = 64 * 1024 * 1024
VMEM_LIMIT = V7X_VMEM_BYTES * 7 // 8
SUBLANES = 8
LANES = 128

TOP_K = 2
PACK_W = D_MODEL // 2
MOE_TM = 512
SC_CHUNK = 64
RANK_CHUNK = 512
COMBINE_PARTS = 2
RING = 4
STRIP = 256
LN_ROWS = 32


def _dot(a, b):
    return jnp.dot(a, b, preferred_element_type=F32)


def _dot_nt(a, b):
    return lax.dot_general(a, b, (((1,), (1,)), ((), ())), preferred_element_type=F32)


def _sigmoid(z):
    return 1.0 / (1.0 + jnp.exp2(z * (-1.0 / np.log(2.0))))


def _layer_norm(h, w, b):
    mu = jnp.mean(h, axis=-1, keepdims=True)
    hc = h - mu
    var = jnp.mean(hc * hc, axis=-1, keepdims=True)
    return hc * lax.rsqrt(var + LN_EPS) * w + b


ROPE_LO = 64
ROPE_PARTS = 3


def _rope_kernel(pos_ref, freq_ref, tab_ref, cos_t_ref, sin_t_ref):
    pos = pos_ref[...]

    def emit(cos_t, sin_t):
        cos_t_ref[...] = cos_t
        sin_t_ref[...] = sin_t

    in_table = jnp.logical_and(jnp.min(pos) >= 0, jnp.max(pos) < ROPE_LO * LANES)

    @pl.when(in_table)
    def _():
        idx = lax.broadcasted_iota(I32, (LANES, pos.shape[1]), 0)
        pick_hi = jnp.where(idx == jnp.right_shift(pos, ROPE_LO.bit_length() - 1), 1.0, 0.0).astype(BF16)
        pick_lo = jnp.where(idx == (pos & (ROPE_LO - 1)), 1.0, 0.0).astype(BF16)

        def look(k, pick):
            return sum(_dot(tab_ref[k * ROPE_PARTS + p], pick) for p in range(ROPE_PARTS))

        cos_a, sin_a, cos_b, sin_b = look(0, pick_hi), look(1, pick_hi), look(2, pick_lo), look(3, pick_lo)
        emit(cos_a * cos_b - sin_a * sin_b, sin_a * cos_b + cos_a * sin_b)

    @pl.when(jnp.logical_not(in_table))
    def _():
        ang = freq_ref[...] * pos.astype(F32)
        emit(jnp.cos(ang), jnp.sin(ang))


def _rope_tables(inv_freq):
    f = inv_freq.astype(np.float64)[:, None]
    idx = np.arange(LANES, dtype=np.float64)[None, :]
    tabs = [np.cos(ROPE_LO * idx * f), np.sin(ROPE_LO * idx * f), np.cos(idx * f), np.sin(idx * f)]
    pieces = []
    for t in tabs:
        rest = t.astype(np.float32)
        for _ in range(ROPE_PARTS):
            piece = rest.astype(BF16)
            pieces.append(piece)
            rest = rest - piece.astype(np.float32)
    return jnp.asarray(np.stack(pieces))


def _rope_table(pos_row, tile=4096):
    T = pos_row.shape[1]
    half = RET_QK_DIM // 2
    inv_freq = (ROPE_BASE ** (-np.arange(half, dtype=np.float64) / half)).astype(np.float32)
    freq = jnp.asarray(inv_freq[:, None])
    tabs = _rope_tables(inv_freq)
    out_t = pl.BlockSpec((half, tile), lambda i: (0, i))
    return pl.pallas_call(
        _rope_kernel,
        grid=(T // tile,),
        in_specs=[pl.BlockSpec((1, tile), lambda i: (0, i)), pl.BlockSpec((half, 1), lambda i: (0, 0)),
                  pl.BlockSpec(tabs.shape, lambda i: (0, 0, 0))],
        out_specs=[out_t, out_t],
        out_shape=[jax.ShapeDtypeStruct((half, T), F32)] * 2,
        name="rope_table",
    )(pos_row, freq, tabs)


POOL_SUB = 256


def _pool_bands():
    r = np.arange(POOL_SUB)[:, None]
    c = np.arange(POOL_SUB)[None, :]
    ch = np.arange(POOL_HALO)[None, :] - POOL_HALO
    main = np.stack([((r - c >= 0) & (r - c < w)) for w in POOL_WINDOWS]).astype(np.float32)
    halo = np.stack([((r - ch >= 0) & (r - ch < w)) for w in POOL_WINDOWS]).astype(np.float32)
    return jnp.asarray(main, BF16), jnp.asarray(halo, BF16)


def _pool_branch(ub, j, halo_ref, bmain_ref, bhalo_ref, o_ref, filler):
    tile = ub.shape[0]
    s0 = j * tile
    slot = lax.rem(j, 2)
    blocks = [(sb * POOL_SUB, g) for sb in range(tile // POOL_SUB) for g in range(len(POOL_WINDOWS))]
    wsum = {}
    for r0, g in blocks:
        cols = slice(g * POOL_GROUP_DIM, (g + 1) * POOL_GROUP_DIM)
        prev = halo_ref[slot] if r0 == 0 else ub[r0 - POOL_HALO:r0]
        wsum[r0, g] = _dot(bmain_ref[g], ub[r0:r0 + POOL_SUB, cols]) + _dot(bhalo_ref[g], prev[:, cols])
    filler()
    for r0, g in blocks:
        cols = slice(g * POOL_GROUP_DIM, (g + 1) * POOL_GROUP_DIM)
        pos = s0 + r0 + lax.broadcasted_iota(I32, (POOL_SUB, POOL_GROUP_DIM), 0)
        cnt = jnp.minimum(pos + 1, POOL_WINDOWS[g]).astype(F32)
        pooled = wsum[r0, g] / cnt - ub[r0:r0 + POOL_SUB, cols].astype(F32)
        o_ref[r0:r0 + POOL_SUB, cols] = pooled.astype(BF16)
    halo_ref[1 - slot] = ub[tile - POOL_HALO:tile]


def _ret_consts():
    h = np.arange(RET_HEADS, dtype=np.float64)
    log_gamma = np.log1p(-np.exp2(-5.0 - h))
    pos = np.arange(RET_CHUNK, dtype=np.float64)
    diff = pos[:, None] - pos[None, :]
    kscale = RET_QK_DIM ** -0.5
    dmask = kscale * np.where(diff >= 0, np.exp(log_gamma[:, None, None] * np.maximum(diff, 0.0)), 0.0)
    qdec = np.exp(log_gamma[:, None] * (pos + 1.0)[None, :])
    kdec = kscale * np.exp(log_gamma[:, None] * (RET_CHUNK - 1.0 - pos)[None, :])
    cdec = np.exp(log_gamma * RET_CHUNK)
    lanes = lambda a: np.broadcast_to(a[:, :, None], (RET_HEADS, RET_CHUNK, RET_QK_DIM))
    kdec_t = np.broadcast_to(kdec[:, None, :], (RET_HEADS, RET_QK_DIM, RET_CHUNK))
    return (jnp.asarray(dmask, F32), jnp.asarray(lanes(qdec), F32), jnp.asarray(kdec_t, F32),
            tuple(float(v) for v in cdec))


def _retention_branch(q, k_t, v, silu_g, cos_t_ref, sin_t_ref, dmask_ref, qdec_ref, kdec_ref,
                      gnw_ref, state_ref, rq_ref, rqd_ref, rkt_ref, rkdt_ref, o_ref, cdec, fillers):
    tile = q.shape[0]
    n_chunks = tile // RET_CHUNK
    half = RET_QK_DIM // 2
    cos_t = cos_t_ref[...]
    sin_t = sin_t_ref[...]
    cos = jnp.transpose(jnp.concatenate([cos_t, cos_t], axis=0))
    sin = jnp.transpose(jnp.concatenate([-sin_t, sin_t], axis=0))
    for h in range(RET_HEADS):
        qk = slice(h * RET_QK_DIM, (h + 1) * RET_QK_DIM)
        qh = q[:, qk]
        qr = qh * cos + pltpu.roll(qh, half, 1) * sin
        rq_ref[:, qk] = qr.astype(BF16)
        rqd_ref[:, qk] = (qr * jnp.concatenate([qdec_ref[h]] * n_chunks, axis=0)).astype(BF16)
        k1 = k_t[h * RET_QK_DIM:h * RET_QK_DIM + half]
        k2 = k_t[h * RET_QK_DIM + half:(h + 1) * RET_QK_DIM]
        kr_t = jnp.concatenate([k1 * cos_t - k2 * sin_t, k2 * cos_t + k1 * sin_t], axis=0)
        rkt_ref[qk, :] = kr_t.astype(BF16)
        rkdt_ref[qk, :] = (kr_t * jnp.concatenate([kdec_ref[h]] * n_chunks, axis=1)).astype(BF16)

    chunks = [slice(c * RET_CHUNK, (c + 1) * RET_CHUNK) for c in range(n_chunks)]
    heads = [slice(h * RET_QK_DIM, (h + 1) * RET_QK_DIM) for h in range(RET_HEADS)]
    v_heads = [slice(h * RET_V_DIM, (h + 1) * RET_V_DIM) for h in range(RET_HEADS)]
    raw = {(c, h): _dot(rq_ref[chunks[c], heads[h]], rkt_ref[heads[h], chunks[c]])
           for c in range(n_chunks) for h in range(RET_HEADS)}
    incr = {(c, h): _dot(rkdt_ref[heads[h], chunks[c]], v[chunks[c], v_heads[h]])
            for c in range(n_chunks) for h in range(RET_HEADS)}
    state_in = {}
    for h in range(RET_HEADS):
        st = state_ref[h]
        for c in range(n_chunks):
            state_in[c, h] = st.astype(BF16)
            st = cdec[h] * st + incr[c, h]
        state_ref[h] = st

    for c in range(n_chunks):
        rows = chunks[c]
        fillers[c]()
        for h in range(RET_HEADS):
            qk = heads[h]
            v_cols = v_heads[h]
            scores = raw[c, h] * dmask_ref[h]
            lhs = jnp.concatenate([scores.astype(BF16), rqd_ref[rows, qk]], axis=1)
            y = _dot(lhs, jnp.concatenate([v[rows, v_cols], state_in[c, h]], axis=0))
            mu = jnp.mean(y, axis=-1, keepdims=True)
            yc = y - mu
            var = jnp.mean(yc * yc, axis=-1, keepdims=True)
            yn = yc * lax.rsqrt(var + LN_EPS) * gnw_ref[:, v_cols]
            o_ref[rows, v_cols] = (silu_g[h][rows] * yn).astype(BF16)


def _cross_attention_branch(xq, kv_ref, o_ref):
    scale = XA_HEAD_DIM ** -0.5
    for h in range(XA_HEADS):
        cols = slice(h * XA_HEAD_DIM, (h + 1) * XA_HEAD_DIM)
        v_cols = slice(XA_WIDTH + h * XA_HEAD_DIM, XA_WIDTH + (h + 1) * XA_HEAD_DIM)
        s = _dot_nt(xq[:, cols], kv_ref[:, cols]) * scale
        m = jnp.max(s, axis=-1, keepdims=True)
        p = jnp.exp(s - m)
        l = jnp.sum(p, axis=-1, keepdims=True)
        o = _dot(p.astype(BF16), kv_ref[:, v_cols]) / l
        o_ref[:, cols] = o.astype(BF16)


ROUTER_ROWS = 128
EXP_ROW0 = 8


def _route(logits_t):
    gl = logits_t[0:8]
    gmax = jnp.max(gl, axis=0, keepdims=True)
    p_grp = 1.0 / jnp.sum(jnp.exp(gl - gmax), axis=0, keepdims=True)
    idx8 = lax.broadcasted_iota(jnp.int32, gl.shape, 0)
    gsel = jnp.min(jnp.where(gl == gmax, idx8, 8), axis=0, keepdims=True)
    cl = jnp.zeros_like(gl)
    for g in range(N_GROUPS):
        r0 = EXP_ROW0 + g * EXPERTS_PER_GROUP
        cl = cl + jnp.where(gsel == g, logits_t[r0:r0 + EXPERTS_PER_GROUP], 0.0)
    v1 = jnp.max(cl, axis=0, keepdims=True)
    i1 = jnp.min(jnp.where(cl == v1, idx8, 8), axis=0, keepdims=True)
    cl2 = jnp.where(idx8 == i1, -jnp.inf, cl)
    v2 = jnp.max(cl2, axis=0, keepdims=True)
    i2 = jnp.min(jnp.where(cl2 == v2, idx8, 8), axis=0, keepdims=True)
    e21 = jnp.exp(v2 - v1)
    w1 = p_grp / (1.0 + e21)
    w2 = p_grp * e21 / (1.0 + e21)
    return gsel * EXPERTS_PER_GROUP + i1, gsel * EXPERTS_PER_GROUP + i2, w1, w2


def _pack_halves(v):
    half = v.shape[1] // 2
    lo = lax.bitcast_convert_type(v[:, :half].astype(BF16).astype(F32), U32)
    hi = lax.bitcast_convert_type(v[:, half:].astype(BF16).astype(F32), U32)
    return lax.bitcast_convert_type(lax.shift_right_logical(lo, U32(16)) | hi, I32)


def _unpack_halves(w):
    u = lax.bitcast_convert_type(w, U32)
    lo = lax.bitcast_convert_type(lax.shift_left(u, U32(16)), F32)
    hi = lax.bitcast_convert_type(u & U32(0xFFFF0000), F32)
    return lo, hi


def _mixer_kernel(x_ref, cos_t_ref, sin_t_ref, mem_ref, wmem_ref, win_ref, wgrp_ref,
                  pscale_ref, bmain_ref, bhalo_ref, dmask_ref, qdec_ref, kdec_ref, gnw_ref, wp_ref, wr_ref,
                  wa_ref, wo_ref, lnw_ref, lnb_ref, rw_ref, rb_ref, xp_ref, route_ref,
                  state_ref, halo_ref, ypool_ref, yret_ref, yxa_ref, rq_ref, rqd_ref, rkt_ref, rkdt_ref, wkt_ref,
                  wpool_ref, kv_ref, *, tile, cdec):
    j = pl.program_id(1)

    @pl.when(jnp.logical_and(pl.program_id(0) == 0, j == 0))
    def _():
        wk = win_ref[:, COL_K:COL_K + RET_HEADS * RET_QK_DIM].astype(F32)
        wkt_ref[...] = jnp.transpose(wk).astype(BF16)
        for g in range(len(POOL_WINDOWS)):
            rows = slice(g * POOL_GROUP_DIM, (g + 1) * POOL_GROUP_DIM)
            wg = (wgrp_ref[g].astype(F32) * pscale_ref[:, rows]).astype(BF16)
            wpool_ref[rows, :] = _dot(wg, wp_ref[rows, :]).astype(BF16)

    @pl.when(j == 0)
    def _():
        state_ref[...] = jnp.zeros_like(state_ref)
        halo_ref[...] = jnp.zeros_like(halo_ref)
        kv_ref[...] = _dot(mem_ref[...].astype(BF16), wmem_ref[...]).astype(BF16)

    x = x_ref[...]
    xb = x.astype(BF16)

    def proj(col, width):
        return _dot(xb, win_ref[:, col:col + width])

    part = {}
    strips = [slice(c, c + STRIP) for c in range(0, D_MODEL, STRIP)]

    def gate(branch, cols):
        return _sigmoid(proj(COL_GATES + branch * D_MODEL + cols.start, STRIP))

    def pool_part():
        def pool_gates():
            part["pool_gate"] = [gate(0, c) for c in strips]

        _pool_branch(proj(COL_POOL, POOL_WIDTH).astype(BF16), j, halo_ref, bmain_ref, bhalo_ref, ypool_ref,
                     pool_gates)
        part["pool"] = [part["pool_gate"][i] * _dot(ypool_ref[...], wpool_ref[:, c]) for i, c in enumerate(strips)]

    def xa_part():
        _cross_attention_branch(proj(COL_XAQ, XA_WIDTH).astype(BF16), kv_ref, yxa_ref)
        part["xa"] = [gate(2, c) * _dot(yxa_ref[...], wa_ref[:, c]) for c in strips]

    def ret_gate_part():
        part["ret_gate"] = [gate(1, c) for c in strips]

    silu_g = []
    for h in range(RET_HEADS):
        gh = proj(COL_G + h * RET_V_DIM, RET_V_DIM)
        silu_g.append(gh * _sigmoid(gh))
    fillers = [pool_part, xa_part, ret_gate_part] + [lambda: None] * (tile // RET_CHUNK - 3)
    _retention_branch(proj(COL_Q, RET_HEADS * RET_QK_DIM), _dot_nt(wkt_ref[...], xb),
                      proj(COL_V, RET_HEADS * RET_V_DIM).astype(BF16), silu_g, cos_t_ref, sin_t_ref, dmask_ref,
                      qdec_ref, kdec_ref, gnw_ref, state_ref, rq_ref, rqd_ref, rkt_ref, rkdt_ref, yret_ref, cdec,
                      fillers)
    merged = jnp.concatenate(
        [(part["pool"][i] + part["ret_gate"][i] * _dot(yret_ref[...], wr_ref[:, c]) + part["xa"][i]).astype(BF16)
         for i, c in enumerate(strips)], axis=1)
    h = jnp.concatenate([ALPHA * x[:, c] + _dot(merged, wo_ref[:, c]) for c in strips], axis=1)
    x1 = jnp.concatenate([_layer_norm(h[r:r + LN_ROWS], lnw_ref[...], lnb_ref[...])
                          for r in range(0, tile, LN_ROWS)], axis=0)
    xp_ref[...] = _pack_halves(x1)
    x1_hi = x1.astype(BF16)
    x1_lo = (x1 - x1_hi.astype(F32)).astype(BF16)
    p4 = _dot(jnp.concatenate([x1_hi, x1_lo], axis=0), rw_ref[...])
    logits = (p4[:tile, :ROUTER_ROWS] + p4[:tile, ROUTER_ROWS:]) + (p4[tile:, :ROUTER_ROWS] + p4[tile:, ROUTER_ROWS:])
    e0, e1, w0, w1 = _route(jnp.transpose(logits) + rb_ref[...])
    w_bits = [lax.bitcast_convert_type(w, I32) for w in (w0, w1)]
    route_ref[...] = jnp.concatenate([e0, e1, *w_bits, jnp.zeros((SUBLANES - 2 * TOP_K, tile), I32)], axis=0)


def _mixer(xf, rope, memf, wmem, win, wgrp, pscale, gnw, wp, wr, wa, wo, lnw, lnb, rw, rb, batch, seq, mem_len,
           tile=512):
    T = xf.shape[0]
    nj = seq // tile
    cos_t, sin_t = rope
    bmain, bhalo = _pool_bands()
    dmask, qdec, kdec, cdec = _ret_consts()
    resident = lambda a: pl.BlockSpec(a.shape, lambda b, j: (0,) * a.ndim, pipeline_mode=pl.Buffered(1))
    rowblk = lambda w: pl.BlockSpec((tile, w), lambda b, j: (b * nj + j, 0))
    colblk = lambda r: pl.BlockSpec((r, tile), lambda b, j: (0, b * nj + j))
    consts = (wmem, win, wgrp, pscale, bmain, bhalo, dmask, qdec, kdec, gnw, wp, wr, wa, wo, lnw, lnb, rw, rb)
    return pl.pallas_call(
        functools.partial(_mixer_kernel, tile=tile, cdec=cdec),
        grid=(batch, nj),
        in_specs=[rowblk(D_MODEL), colblk(RET_QK_DIM // 2), colblk(RET_QK_DIM // 2),
                  pl.BlockSpec((mem_len, D_MODEL), lambda b, j: (b, 0))] + [resident(a) for a in consts],
        out_specs=[rowblk(PACK_W), colblk(SUBLANES)],
        out_shape=[jax.ShapeDtypeStruct((T, PACK_W), I32), jax.ShapeDtypeStruct((SUBLANES, T), I32)],
        scratch_shapes=[pltpu.VMEM((RET_HEADS, RET_QK_DIM, RET_V_DIM), F32),
                        pltpu.VMEM((2, POOL_HALO, POOL_WIDTH), BF16),
                        pltpu.VMEM((tile, POOL_WIDTH), BF16),
                        pltpu.VMEM((tile, RET_HEADS * RET_V_DIM), BF16),
                        pltpu.VMEM((tile, XA_WIDTH), BF16),
                        pltpu.VMEM((tile, RET_HEADS * RET_QK_DIM), BF16),
                        pltpu.VMEM((tile, RET_HEADS * RET_QK_DIM), BF16),
                        pltpu.VMEM((RET_HEADS * RET_QK_DIM, tile), BF16),
                        pltpu.VMEM((RET_HEADS * RET_QK_DIM, tile), BF16),
                        pltpu.VMEM((RET_HEADS * RET_QK_DIM, D_MODEL), BF16),
                        pltpu.VMEM((POOL_WIDTH, D_MODEL), BF16),
                        pltpu.VMEM((mem_len, 2 * XA_WIDTH), BF16)],
        compiler_params=pltpu.CompilerParams(dimension_semantics=("arbitrary", "arbitrary"),
                                             vmem_limit_bytes=VMEM_LIMIT),
        name="mixer",
    )(xf, cos_t, sin_t, memf, *consts)


META_LANES = LANES


def _positions_kernel(eid_ref, tri_ref, low_ref, pos_ref, first_tile_ref, n_tiles_ref, *, n_tok):
    n_chunks = n_tok // RANK_CHUNK
    erow = lax.broadcasted_iota(I32, (N_EXPERTS, RANK_CHUNK), 0)

    def onehot(c):
        sl = slice(c * RANK_CHUNK, (c + 1) * RANK_CHUNK)
        m0 = eid_ref[0:1, sl] == erow
        m1 = eid_ref[1:2, sl] == erow
        return m0, m1, jnp.where(m0, 1.0, 0.0) + jnp.where(m1, 1.0, 0.0)

    counts = jnp.zeros((N_EXPERTS, 1), F32)
    for c in range(n_chunks):
        counts = counts + jnp.sum(onehot(c)[2], axis=1, keepdims=True)
    ptiles = jnp.floor((counts + (MOE_TM - 1)) * (1.0 / MOE_TM))
    ptiles_b = jnp.broadcast_to(ptiles, (N_EXPERTS, LANES)).astype(BF16)
    start = _dot(low_ref[...], ptiles_b)[:, 0:1] * MOE_TM

    pos_ref[...] = jnp.zeros_like(pos_ref)
    carry = start - 1.0
    for c in range(n_chunks):
        sl = slice(c * RANK_CHUNK, (c + 1) * RANK_CHUNK)
        m0, m1, oh = onehot(c)
        rank = _dot(oh.astype(BF16), tri_ref[...]) + carry
        pos_ref[0:1, sl] = jnp.sum(jnp.where(m0, rank, 0.0), axis=0, keepdims=True).astype(I32)
        pos_ref[1:2, sl] = jnp.sum(jnp.where(m1, rank, 0.0), axis=0, keepdims=True).astype(I32)
        carry = carry + jnp.sum(oh, axis=1, keepdims=True)

    first_tile_ref[...] = jnp.broadcast_to(start * (1.0 / MOE_TM), first_tile_ref.shape).astype(I32)
    n_tiles_ref[...] = jnp.broadcast_to(ptiles, n_tiles_ref.shape).astype(I32)


def _positions(eid):
    T = eid.shape[1]
    r = np.arange(RANK_CHUNK)
    tri = jnp.asarray(r[:, None] <= r[None, :], BF16)
    e = np.arange(N_EXPERTS)
    low = jnp.asarray(e[None, :] < e[:, None], BF16)
    full = lambda a: pl.BlockSpec(a.shape, lambda i: (0,) * a.ndim)
    return pl.pallas_call(
        functools.partial(_positions_kernel, n_tok=T),
        grid=(1,),
        in_specs=[full(eid), full(tri), full(low)],
        out_specs=[pl.BlockSpec((SUBLANES, T), lambda i: (0, 0)),
                   pl.BlockSpec((N_EXPERTS, META_LANES), lambda i: (0, 0)),
                   pl.BlockSpec((N_EXPERTS, META_LANES), lambda i: (0, 0))],
        out_shape=[jax.ShapeDtypeStruct((SUBLANES, T), I32), jax.ShapeDtypeStruct((N_EXPERTS, META_LANES), I32),
                   jax.ShapeDtypeStruct((N_EXPERTS, META_LANES), I32)],
        name="route_positions",
    )(eid, tri, low)


def _sc_workers():
    info = plsc.get_sparse_core_info()
    return info.num_cores, info.num_cores * info.num_subcores


def _sc_scatter_rows(xp, pos2d, n_out):
    T, W = xp.shape
    n_cores, n_workers = _sc_workers()
    cpw = T // SC_CHUNK // n_workers
    mesh = plsc.VectorSubcoreMesh(core_axis_name="c", subcore_axis_name="s")

    @functools.partial(
        pl.kernel, mesh=mesh, out_type=jax.ShapeDtypeStruct((n_out, W), I32),
        scratch_types=[pltpu.VMEM((TOP_K * cpw, SC_CHUNK), I32), pltpu.VMEM((2, SC_CHUNK, W), I32),
                       pltpu.SemaphoreType.DMA((2,)), pltpu.SemaphoreType.DMA((2,))],
        name="sc_scatter_rows")
    def k(x_hbm, pos_hbm, out_hbm, idx_v, rows_v, rd_sem, wr_sem):
        wid = lax.axis_index("s") * n_cores + lax.axis_index("c")
        for s in range(TOP_K):
            pltpu.sync_copy(pos_hbm.at[pl.ds(s * (T // SC_CHUNK) + wid * cpw, cpw)],
                            idx_v.at[pl.ds(s * cpw, cpw)])

        def read(j):
            return pltpu.make_async_copy(x_hbm.at[pl.ds((wid * cpw + j) * SC_CHUNK, SC_CHUNK)],
                                         rows_v.at[j % 2], rd_sem.at[j % 2])

        def write(j, s):
            return pltpu.make_async_copy(rows_v.at[j % 2], out_hbm.at[idx_v.at[s * cpw + j]], wr_sem.at[j % 2])

        _sc_two_buffer_stream(cpw, read, lambda j: [write(j, s) for s in range(TOP_K)])

    return k(xp, pos2d)


def _sc_two_buffer_stream(n, read, writes):
    read(0).start()
    for j in range(n):
        read(j).wait()
        if j + 1 < n:
            if j >= 1:
                for w in writes(j - 1):
                    w.wait()
            read(j + 1).start()
        for w in writes(j):
            w.start()
    for j in range(max(n - 2, 0), n):
        for w in writes(j):
            w.wait()


def _sc_gather_rows(y, idx2d):
    W = y.shape[1]
    n = idx2d.shape[0] * SC_CHUNK
    n_cores, n_workers = _sc_workers()
    cpw = n // SC_CHUNK // n_workers
    mesh = plsc.VectorSubcoreMesh(core_axis_name="c", subcore_axis_name="s")

    @functools.partial(
        pl.kernel, mesh=mesh, out_type=jax.ShapeDtypeStruct((n, W), I32),
        scratch_types=[pltpu.VMEM((cpw, SC_CHUNK), I32), pltpu.VMEM((2, SC_CHUNK, W), I32),
                       pltpu.SemaphoreType.DMA((2,)), pltpu.SemaphoreType.DMA((2,))],
        name="sc_gather_rows")
    def k(y_hbm, idx_hbm, out_hbm, idx_v, rows_v, rd_sem, wr_sem):
        wid = lax.axis_index("s") * n_cores + lax.axis_index("c")
        pltpu.sync_copy(idx_hbm.at[pl.ds(wid * cpw, cpw)], idx_v)

        def read(j):
            return pltpu.make_async_copy(y_hbm.at[idx_v.at[j]], rows_v.at[j % 2], rd_sem.at[j % 2])

        def write(j):
            return pltpu.make_async_copy(rows_v.at[j % 2], out_hbm.at[pl.ds((wid * cpw + j) * SC_CHUNK, SC_CHUNK)],
                                         wr_sem.at[j % 2])

        _sc_two_buffer_stream(cpw, read, lambda j: [write(j)])

    return k(y, idx2d)


def _routed_kernel(first_ref, count_ref, xs_hbm, wg_ref, wu_ref, wd_ref, ys_hbm, xbuf, ybuf, in_sem, out_sem):
    e = pl.program_id(0)
    last = pl.num_programs(0) - 1
    total = first_ref[last] + count_ref[last]

    def in_copy(g):
        slot = lax.rem(g, RING)
        return pltpu.make_async_copy(xs_hbm.at[pl.ds(g * MOE_TM, MOE_TM)], xbuf.at[slot], in_sem.at[slot])

    def out_copy(g):
        slot = lax.rem(g, RING)
        return pltpu.make_async_copy(ybuf.at[slot], ys_hbm.at[pl.ds(g * MOE_TM, MOE_TM)], out_sem.at[slot])

    @pl.when(e == 0)
    def _():
        for g0 in range(RING - 1):
            @pl.when(g0 < total)
            def _():
                in_copy(g0).start()

    wg = wg_ref[0].astype(BF16)
    wu = wu_ref[0].astype(BF16)
    wd = wd_ref[0].astype(BF16)

    def tile_step(i, carry):
        g = first_ref[e] + i
        slot = lax.rem(g, RING)
        in_copy(g).wait()

        @pl.when(g + RING - 1 < total)
        def _():
            in_copy(g + RING - 1).start()

        @pl.when(g >= RING)
        def _():
            out_copy(g - RING).wait()

        lo, hi = _unpack_halves(xbuf[slot])
        lo = lo.astype(BF16)
        hi = hi.astype(BF16)
        a = _dot(lo, wg[:PACK_W]) + _dot(hi, wg[PACK_W:])
        b = _dot(lo, wu[:PACK_W]) + _dot(hi, wu[PACK_W:])
        act = (a * _sigmoid(a) * b).astype(BF16)
        ybuf[slot] = _pack_halves(_dot(act, wd))
        out_copy(g).start()
        return carry

    lax.fori_loop(0, count_ref[e], tile_step, 0)

    @pl.when(e == last)
    def _():
        for back in range(RING, 0, -1):
            @pl.when(total >= back)
            def _():
                out_copy(total - back).wait()


def _routed_mlp(first_tile, n_tiles, xs, wg, wu, wd):
    R = xs.shape[0]
    any_space = pl.BlockSpec(memory_space=pl.ANY)
    return pl.pallas_call(
        _routed_kernel,
        grid_spec=pltpu.PrefetchScalarGridSpec(
            num_scalar_prefetch=2,
            grid=(N_EXPERTS,),
            in_specs=[any_space,
                      pl.BlockSpec((1, D_MODEL, D_EXPERT), lambda e, ft, nt: (e, 0, 0)),
                      pl.BlockSpec((1, D_MODEL, D_EXPERT), lambda e, ft, nt: (e, 0, 0)),
                      pl.BlockSpec((1, D_EXPERT, D_MODEL), lambda e, ft, nt: (e, 0, 0))],
            out_specs=any_space,
            scratch_shapes=[pltpu.VMEM((RING, MOE_TM, PACK_W), I32), pltpu.VMEM((RING, MOE_TM, PACK_W), I32),
                            pltpu.SemaphoreType.DMA((RING,)), pltpu.SemaphoreType.DMA((RING,))]),
        out_shape=jax.ShapeDtypeStruct((R, PACK_W), I32),
        compiler_params=pltpu.CompilerParams(dimension_semantics=("arbitrary",)),
        name="routed_mlp",
    )(first_tile, n_tiles, xs, wg, wu, wd)


def _combine_kernel(xp_ref, y0_ref, y1_ref, route_ref, lnw_ref, lnb_ref, *out_refs):
    o_ref = out_refs[-1]
    w_rows = lax.bitcast_convert_type(route_ref[...], F32)
    w_cols = jnp.transpose(jnp.concatenate([w_rows] * (LANES // SUBLANES), axis=0))
    w0 = w_cols[:, TOP_K:TOP_K + 1]
    w1 = w_cols[:, TOP_K + 1:TOP_K + 2]
    xlo, xhi = _unpack_halves(xp_ref[...])
    y0lo, y0hi = _unpack_halves(y0_ref[...])
    y1lo, y1hi = _unpack_halves(y1_ref[...])
    h = jnp.concatenate([ALPHA * xlo + (w0 * y0lo + w1 * y1lo), ALPHA * xhi + (w0 * y0hi + w1 * y1hi)], axis=1)
    o_ref[...] = _layer_norm(h, lnw_ref[...], lnb_ref[...])


def _combine_ln2(xp, yg_parts, route, lnw, lnb, tile=1024):
    T = xp.shape[0]
    n_parts = len(yg_parts)
    nt = T // tile // n_parts
    full = lambda a: pl.BlockSpec(a.shape, lambda i: (0,) * a.ndim)
    out = None
    for p, yg in enumerate(yg_parts):
        rows = lambda w, p=p: pl.BlockSpec((tile, w), lambda i: (i + p * nt, 0))
        in_specs = [rows(PACK_W),
                    pl.BlockSpec((tile, PACK_W), lambda i: (i, 0)),
                    pl.BlockSpec((tile, PACK_W), lambda i: (i + nt, 0)),
                    pl.BlockSpec((SUBLANES, tile), lambda i, p=p: (0, i + p * nt)), full(lnw), full(lnb)]
        args = [xp, yg, yg, route, lnw, lnb]
        aliases = {}
        if out is not None:
            in_specs.append(pl.BlockSpec(memory_space=pl.ANY))
            args.append(out)
            aliases = {len(args) - 1: 0}
        out = pl.pallas_call(
            _combine_kernel,
            grid=(nt,),
            in_specs=in_specs,
            out_specs=rows(D_MODEL),
            out_shape=jax.ShapeDtypeStruct((T, D_MODEL), F32),
            input_output_aliases=aliases,
            compiler_params=pltpu.CompilerParams(dimension_semantics=("arbitrary",), vmem_limit_bytes=VMEM_LIMIT),
            name="combine_ln2",
        )(*args)
    return out


def _router_params(w_grp, b_grp, w_exp, b_exp):
    rw = jnp.zeros((D_MODEL, ROUTER_ROWS), F32)
    rw = rw.at[:, 0:N_GROUPS].set(w_grp).at[:, EXP_ROW0:EXP_ROW0 + N_EXPERTS].set(w_exp)
    rb = jnp.zeros((ROUTER_ROWS,), F32).at[N_GROUPS:8].set(NEG_BIG)
    rb = rb.at[0:N_GROUPS].set(b_grp).at[EXP_ROW0:EXP_ROW0 + N_EXPERTS].set(b_exp)
    rw_hi = rw.astype(BF16)
    rw_lo = (rw - rw_hi.astype(F32)).astype(BF16)
    return jnp.concatenate([rw_hi, rw_lo], axis=1), rb[:, None]


def kernel(x, mem, positions, w_in, w_pool_grp, pool_scale, ret_gn_w, w_mem_kv, w_br_pool, w_br_ret, w_br_xa,
           w_out, ln1_w, ln1_b, w_grp_router, b_grp_router, w_exp_router, b_exp_router, w_exp_gate, w_exp_up,
           w_exp_down, ln2_w, ln2_b):
    B, S, D = x.shape
    assert D == D_MODEL and w_in.shape[0] == DEPTH and S % 512 == 0
    T = B * S
    M = mem.shape[1]
    l = 0
    xf = x.reshape(T, D)

    rope = _rope_table(positions.reshape(1, T))
    rw, rb = _router_params(w_grp_router[l], b_grp_router[l], w_exp_router[l], b_exp_router[l])
    xp, route = _mixer(xf, rope, mem.reshape(B * M, D), w_mem_kv[l].astype(BF16), w_in[l].astype(BF16),
                       w_pool_grp[l], pool_scale[l][None, :], ret_gn_w[l].reshape(1, -1),
                       w_br_pool[l].astype(BF16), w_br_ret[l].astype(BF16), w_br_xa[l].astype(BF16),
                       w_out[l].astype(BF16), ln1_w[l][None, :], ln1_b[l][None, :], rw, rb, B, S, M)

    pos, first_tile, n_tiles = _positions(route)
    pos2d = pos[0:TOP_K].reshape(TOP_K * T // SC_CHUNK, SC_CHUNK)
    max_tiles = (TOP_K * T + N_EXPERTS * (MOE_TM - 1)) // MOE_TM
    xs = _sc_scatter_rows(xp, pos2d, max_tiles * MOE_TM)
    ys = _routed_mlp(first_tile[:, 0], n_tiles[:, 0], xs,
                     w_exp_gate[l].reshape(N_EXPERTS, D_MODEL, D_EXPERT),
                     w_exp_up[l].reshape(N_EXPERTS, D_MODEL, D_EXPERT),
                     w_exp_down[l].reshape(N_EXPERTS, D_EXPERT, D_MODEL))
    rng = T // COMBINE_PARTS
    yg_parts = [_sc_gather_rows(ys, pos[0:TOP_K, p * rng:(p + 1) * rng].reshape(TOP_K * rng // SC_CHUNK, SC_CHUNK))
                for p in range(COMBINE_PARTS)]
    out = _combine_ln2(xp, yg_parts, route, ln2_w[l][None, :], ln2_b[l][None, :])
    return out.reshape(B, S, D)
```
